```python
import math
import jax, jax.numpy as jnp
from jax import lax
import numpy as np

D_MODEL = 1024
BATCH = 8
SEQ = 4096
DEPTH = 1

D_MIX = D_MODEL
D_CONV = D_MIX // 2
D_RNN = D_MIX - D_CONV
N_RNN_HEADS = 8
RNN_HEAD_DIM = D_RNN // N_RNN_HEADS
CONV_WIDTH = 31
RNN_CONV_WIDTH = 4
RG_C = 8.0
D_FF = ((8 * D_MODEL // 3 + 127) // 128) * 128
N_SUBLAYERS = 3
MACARON_W = 0.5
EPS = 1e-6

kernel_name = "hymba_style_conformer_rglru_macaron_adaln_block"


def rmsnorm(x, g):
    xf = x.astype(jnp.float32)
    y = xf * lax.rsqrt(jnp.mean(xf * xf, axis=-1, keepdims=True) + EPS)
    return y.astype(x.dtype) * g


def layernorm(x, g, b):
    xf = x.astype(jnp.float32)
    mu = jnp.mean(xf, axis=-1, keepdims=True)
    var = jnp.mean(jnp.square(xf - mu), axis=-1, keepdims=True)
    y = (xf - mu) * lax.rsqrt(var + EPS)
    return y.astype(x.dtype) * g + b


def ada_rmsnorm(x, g, shift, scale):
    return rmsnorm(x, g) * (1.0 + scale[:, None, :]) + shift[:, None, :]


def causal_depthwise_conv(u, w, b):
    k = w.shape[0]
    y = lax.conv_general_dilated(
        u, w[:, None, :], window_strides=(1,), padding=[(k - 1, 0)],
        dimension_numbers=("NWC", "WIO", "NWC"), feature_group_count=u.shape[-1])
    return y + b


def swiglu_ffn(h, w_in, w_out):
    gate, up = jnp.split(h @ w_in, 2, axis=-1)
    return (jax.nn.silu(gate) * up) @ w_out


def conformer_conv_group(u_val, u_gate, conv_w, conv_b, ln_g, ln_b):
    u = u_val * jax.nn.sigmoid(u_gate)
    u = causal_depthwise_conv(u, conv_w, conv_b)
    u = layernorm(u, ln_g, ln_b)
    return jax.nn.silu(u)


def _lru_combine(left, right):
    a_l, b_l = left
    a_r, b_r = right
    return a_l * a_r, a_r * b_l + b_r


def rglru_group(u_x, u_y, conv_w, conv_b, w_a, b_a, w_i, b_i, lru_lambda):
    bsz, seq, _ = u_x.shape
    xr = causal_depthwise_conv(u_x, conv_w, conv_b)
    xh = xr.reshape(bsz, seq, N_RNN_HEADS, RNN_HEAD_DIM)
    r = jax.nn.sigmoid(jnp.einsum("bshd,hde->bshe", xh, w_a).reshape(bsz, seq, D_RNN) + b_a)
    i = jax.nn.sigmoid(jnp.einsum("bshd,hde->bshe", xh, w_i).reshape(bsz, seq, D_RNN) + b_i)
    log_a = RG_C * r.astype(jnp.float32) * jax.nn.log_sigmoid(lru_lambda.astype(jnp.float32))
    a = jnp.exp(log_a)
    mult = jnp.sqrt(-jnp.expm1(2.0 * log_a))
    bterm = mult * (i.astype(jnp.float32) * xr.astype(jnp.float32))
    _, h = lax.associative_scan(_lru_combine, (a, bterm), axis=1)
    return jax.nn.gelu(u_y) * h.astype(u_x.dtype)


def _fwd_setup_inputs(seed: int = 0) -> dict:
    key = jax.random.key(seed)
    ks = jax.random.split(key, 32)
    f32 = jnp.float32
    L = DEPTH
    nrm = lambda k, shape, s: jax.random.normal(k, shape, f32) * s
    gain = lambda k, shape: 1.0 + 0.05 * jax.random.normal(k, shape, f32)
    a0 = jax.random.uniform(ks[20], (L, D_RNN), f32, 0.9, 0.999)
    s = a0 ** (1.0 / RG_C)
    lru_lambda = jnp.log(s) - jnp.log1p(-s)
    return {
        "x": jax.random.normal(ks[0], (BATCH, SEQ, D_MODEL), f32),
        "c": jax.random.normal(ks[1], (BATCH, D_MODEL), f32),
        "w_mod": nrm(ks[2], (L, D_MODEL, 3 * N_SUBLAYERS * D_MODEL), 0.5 * D_MODEL ** -0.5),
        "b_mod": nrm(ks[3], (L, 3 * N_SUBLAYERS * D_MODEL), 0.02),
        "g_ffn1": gain(ks[4], (L, D_MODEL)),
        "w_ffn1_in": nrm(ks[5], (L, D_MODEL, 2 * D_FF), D_MODEL ** -0.5),
        "w_ffn1_out": nrm(ks[6], (L, D_FF, D_MODEL), D_FF ** -0.5),
        "g_mix": gain(ks[7], (L, D_MODEL)),
        "w_in": nrm(ks[8], (L, D_MODEL, 2 * D_CONV + 2 * D_RNN), D_MODEL ** -0.5),
        "conv_w": nrm(ks[9], (L, CONV_WIDTH, D_CONV), CONV_WIDTH ** -0.5),
        "conv_b": nrm(ks[10], (L, D_CONV), 0.02),
        "ln_g": gain(ks[11], (L, D_CONV)),
        "ln_b": nrm(ks[12], (L, D_CONV), 0.02),
        "rnn_conv_w": nrm(ks[13], (L, RNN_CONV_WIDTH, D_RNN), RNN_CONV_WIDTH ** -0.5),
        "rnn_conv_b": nrm(ks[14], (L, D_RNN), 0.02),
        "w_a": nrm(ks[15], (L, N_RNN_HEADS, RNN_HEAD_DIM, RNN_HEAD_DIM), RNN_HEAD_DIM ** -0.5),
        "b_a": nrm(ks[16], (L, D_RNN), 0.02),
        "w_i": nrm(ks[17], (L, N_RNN_HEADS, RNN_HEAD_DIM, RNN_HEAD_DIM), RNN_HEAD_DIM ** -0.5),
        "b_i": nrm(ks[18], (L, D_RNN), 0.02),
        "lru_lambda": lru_lambda,
        "w_out": nrm(ks[19], (L, D_MIX, D_MODEL), D_MIX ** -0.5),
        "g_ffn2": gain(ks[21], (L, D_MODEL)),
        "w_ffn2_in": nrm(ks[22], (L, D_MODEL, 2 * D_FF), D_MODEL ** -0.5),
        "w_ffn2_out": nrm(ks[23], (L, D_FF, D_MODEL), D_FF ** -0.5),
        "w_fmod": nrm(ks[24], (D_MODEL, 2 * D_MODEL), 0.5 * D_MODEL ** -0.5),
        "b_fmod": nrm(ks[25], (2 * D_MODEL,), 0.02),
        "g_final": gain(ks[26], (D_MODEL,)),
    }


def _fwd_reference(x, c, w_mod, b_mod, g_ffn1, w_ffn1_in, w_ffn1_out, g_mix, w_in,
              conv_w, conv_b, ln_g, ln_b, rnn_conv_w, rnn_conv_b, w_a, b_a, w_i, b_i,
              lru_lambda, w_out, g_ffn2, w_ffn2_in, w_ffn2_out, w_fmod, b_fmod, g_final):
    c_act = jax.nn.silu(c)
    for l in range(DEPTH):
        mod = c_act @ w_mod[l] + b_mod[l]
        sh1, sc1, gt1, sh2, sc2, gt2, sh3, sc3, gt3 = jnp.split(mod, 3 * N_SUBLAYERS, axis=-1)

        h = ada_rmsnorm(x, g_ffn1[l], sh1, sc1)
        x = x + MACARON_W * gt1[:, None, :] * swiglu_ffn(h, w_ffn1_in[l], w_ffn1_out[l])

        h = ada_rmsnorm(x, g_mix[l], sh2, sc2)
        proj = h @ w_in[l]
        u_val, u_gate, u_x, u_y = jnp.split(
            proj, [D_CONV, 2 * D_CONV, 2 * D_CONV + D_RNN], axis=-1)
        y_conv = conformer_conv_group(u_val, u_gate, conv_w[l], conv_b[l], ln_g[l], ln_b[l])
        y_rnn = rglru_group(u_x, u_y, rnn_conv_w[l], rnn_conv_b[l], w_a[l], b_a[l],
                            w_i[l], b_i[l], lru_lambda[l])
        y_mix = jnp.concatenate([y_conv, y_rnn], axis=-1) @ w_out[l]
        x = x + gt2[:, None, :] * y_mix

        h = ada_rmsnorm(x, g_ffn2[l], sh3, sc3)
        x = x + MACARON_W * gt3[:, None, :] * swiglu_ffn(h, w_ffn2_in[l], w_ffn2_out[l])

    fmod = c_act @ w_fmod + b_fmod
    f_shift, f_scale = jnp.split(fmod, 2, axis=-1)
    return ada_rmsnorm(x, g_final, f_shift, f_scale)


import jax as _jax
import jax.numpy as _jnp

TWIN_FORMAT = 'train_step'
FWD_PARAMS = ['x', 'c', 'w_mod', 'b_mod', 'g_ffn1', 'w_ffn1_in', 'w_ffn1_out', 'g_mix', 'w_in', 'conv_w', 'conv_b', 'ln_g', 'ln_b', 'rnn_conv_w', 'rnn_conv_b', 'w_a', 'b_a', 'w_i', 'b_i', 'lru_lambda', 'w_out', 'g_ffn2', 'w_ffn2_in', 'w_ffn2_out', 'w_fmod', 'b_fmod', 'g_final']
TWIN_WEIGHTS = ['w_mod', 'b_mod', 'g_ffn1', 'w_ffn1_in', 'w_ffn1_out', 'g_mix', 'w_in', 'conv_w', 'conv_b', 'ln_g', 'ln_b', 'rnn_conv_w', 'rnn_conv_b', 'w_a', 'b_a', 'w_i', 'b_i', 'lru_lambda', 'w_out', 'g_ffn2', 'w_ffn2_in', 'w_ffn2_out', 'w_fmod', 'b_fmod', 'g_final']
TWIN_DIFF_INPUT = 'x'
TWIN_INPUTS = ['x', 'c', 'w_mod', 'b_mod', 'g_ffn1', 'w_ffn1_in', 'w_ffn1_out', 'g_mix', 'w_in', 'conv_w', 'conv_b', 'ln_g', 'ln_b', 'rnn_conv_w', 'rnn_conv_b', 'w_a', 'b_a', 'w_i', 'b_i', 'lru_lambda', 'w_out', 'g_ffn2', 'w_ffn2_in', 'w_ffn2_out', 'w_fmod', 'b_fmod', 'g_final', 'loss_target', 'm_w_mod', 'm_b_mod', 'm_g_ffn1', 'm_w_ffn1_in', 'm_w_ffn1_out', 'm_g_mix', 'm_w_in', 'm_conv_w', 'm_conv_b', 'm_ln_g', 'm_ln_b', 'm_rnn_conv_w', 'm_rnn_conv_b', 'm_w_a', 'm_b_a', 'm_w_i', 'm_b_i', 'm_lru_lambda', 'm_w_out', 'm_g_ffn2', 'm_w_ffn2_in', 'm_w_ffn2_out', 'm_w_fmod', 'm_b_fmod', 'm_g_final', 'v_w_mod', 'v_b_mod', 'v_g_ffn1', 'v_w_ffn1_in', 'v_w_ffn1_out', 'v_g_mix', 'v_w_in', 'v_conv_w', 'v_conv_b', 'v_ln_g', 'v_ln_b', 'v_rnn_conv_w', 'v_rnn_conv_b', 'v_w_a', 'v_b_a', 'v_w_i', 'v_b_i', 'v_lru_lambda', 'v_w_out', 'v_g_ffn2', 'v_w_ffn2_in', 'v_w_ffn2_out', 'v_w_fmod', 'v_b_fmod', 'v_g_final']
TWIN_OUTPUTS = ['loss', 'grad_x', 'grad_w_mod', 'grad_b_mod', 'grad_g_ffn1', 'grad_w_ffn1_in', 'grad_w_ffn1_out', 'grad_g_mix', 'grad_w_in', 'grad_conv_w', 'grad_conv_b', 'grad_ln_g', 'grad_ln_b', 'grad_rnn_conv_w', 'grad_rnn_conv_b', 'grad_w_a', 'grad_b_a', 'grad_w_i', 'grad_b_i', 'grad_lru_lambda', 'grad_w_out', 'grad_g_ffn2', 'grad_w_ffn2_in', 'grad_w_ffn2_out', 'grad_w_fmod', 'grad_b_fmod', 'grad_g_final', 'delta_w_mod', 'delta_b_mod', 'delta_g_ffn1', 'delta_w_ffn1_in', 'delta_w_ffn1_out', 'delta_g_mix', 'delta_w_in', 'delta_conv_w', 'delta_conv_b', 'delta_ln_g', 'delta_ln_b', 'delta_rnn_conv_w', 'delta_rnn_conv_b', 'delta_w_a', 'delta_b_a', 'delta_w_i', 'delta_b_i', 'delta_lru_lambda', 'delta_w_out', 'delta_g_ffn2', 'delta_w_ffn2_in', 'delta_w_ffn2_out', 'delta_w_fmod', 'delta_b_fmod', 'delta_g_final', 'new_m_w_mod', 'new_m_b_mod', 'new_m_g_ffn1', 'new_m_w_ffn1_in', 'new_m_w_ffn1_out', 'new_m_g_mix', 'new_m_w_in', 'new_m_conv_w', 'new_m_conv_b', 'new_m_ln_g', 'new_m_ln_b', 'new_m_rnn_conv_w', 'new_m_rnn_conv_b', 'new_m_w_a', 'new_m_b_a', 'new_m_w_i', 'new_m_b_i', 'new_m_lru_lambda', 'new_m_w_out', 'new_m_g_ffn2', 'new_m_w_ffn2_in', 'new_m_w_ffn2_out', 'new_m_w_fmod', 'new_m_b_fmod', 'new_m_g_final', 'new_v_w_mod', 'new_v_b_mod', 'new_v_g_ffn1', 'new_v_w_ffn1_in', 'new_v_w_ffn1_out', 'new_v_g_mix', 'new_v_w_in', 'new_v_conv_w', 'new_v_conv_b', 'new_v_ln_g', 'new_v_ln_b', 'new_v_rnn_conv_w', 'new_v_rnn_conv_b', 'new_v_w_a', 'new_v_b_a', 'new_v_w_i', 'new_v_b_i', 'new_v_lru_lambda', 'new_v_w_out', 'new_v_g_ffn2', 'new_v_w_ffn2_in', 'new_v_w_ffn2_out', 'new_v_w_fmod', 'new_v_b_fmod', 'new_v_g_final']
TWIN_LEAF_KINDS = {'loss': 'loss', 'grad_x': 'grad_x', 'grad_w_mod': 'grad_w', 'grad_b_mod': 'grad_w', 'grad_g_ffn1': 'grad_w', 'grad_w_ffn1_in': 'grad_w', 'grad_w_ffn1_out': 'grad_w', 'grad_g_mix': 'grad_w', 'grad_w_in': 'grad_w', 'grad_conv_w': 'grad_w', 'grad_conv_b': 'grad_w', 'grad_ln_g': 'grad_w', 'grad_ln_b': 'grad_w', 'grad_rnn_conv_w': 'grad_w', 'grad_rnn_conv_b': 'grad_w', 'grad_w_a': 'grad_w', 'grad_b_a': 'grad_w', 'grad_w_i': 'grad_w', 'grad_b_i': 'grad_w', 'grad_lru_lambda': 'grad_w', 'grad_w_out': 'grad_w', 'grad_g_ffn2': 'grad_w', 'grad_w_ffn2_in': 'grad_w', 'grad_w_ffn2_out': 'grad_w', 'grad_w_fmod': 'grad_w', 'grad_b_fmod': 'grad_w', 'grad_g_final': 'grad_w', 'delta_w_mod': 'delta_w', 'delta_b_mod': 'delta_w', 'delta_g_ffn1': 'delta_w', 'delta_w_ffn1_in': 'delta_w', 'delta_w_ffn1_out': 'delta_w', 'delta_g_mix': 'delta_w', 'delta_w_in': 'delta_w', 'delta_conv_w': 'delta_w', 'delta_conv_b': 'delta_w', 'delta_ln_g': 'delta_w', 'delta_ln_b': 'delta_w', 'delta_rnn_conv_w': 'delta_w', 'delta_rnn_conv_b': 'delta_w', 'delta_w_a': 'delta_w', 'delta_b_a': 'delta_w', 'delta_w_i': 'delta_w', 'delta_b_i': 'delta_w', 'delta_lru_lambda': 'delta_w', 'delta_w_out': 'delta_w', 'delta_g_ffn2': 'delta_w', 'delta_w_ffn2_in': 'delta_w', 'delta_w_ffn2_out': 'delta_w', 'delta_w_fmod': 'delta_w', 'delta_b_fmod': 'delta_w', 'delta_g_final': 'delta_w', 'new_m_w_mod': 'new_m', 'new_m_b_mod': 'new_m', 'new_m_g_ffn1': 'new_m', 'new_m_w_ffn1_in': 'new_m', 'new_m_w_ffn1_out': 'new_m', 'new_m_g_mix': 'new_m', 'new_m_w_in': 'new_m', 'new_m_conv_w': 'new_m', 'new_m_conv_b': 'new_m', 'new_m_ln_g': 'new_m', 'new_m_ln_b': 'new_m', 'new_m_rnn_conv_w': 'new_m', 'new_m_rnn_conv_b': 'new_m', 'new_m_w_a': 'new_m', 'new_m_b_a': 'new_m', 'new_m_w_i': 'new_m', 'new_m_b_i': 'new_m', 'new_m_lru_lambda': 'new_m', 'new_m_w_out': 'new_m', 'new_m_g_ffn2': 'new_m', 'new_m_w_ffn2_in': 'new_m', 'new_m_w_ffn2_out': 'new_m', 'new_m_w_fmod': 'new_m', 'new_m_b_fmod': 'new_m', 'new_m_g_final': 'new_m', 'new_v_w_mod': 'new_v', 'new_v_b_mod': 'new_v', 'new_v_g_ffn1': 'new_v', 'new_v_w_ffn1_in': 'new_v', 'new_v_w_ffn1_out': 'new_v', 'new_v_g_mix': 'new_v', 'new_v_w_in': 'new_v', 'new_v_conv_w': 'new_v', 'new_v_conv_b': 'new_v', 'new_v_ln_g': 'new_v', 'new_v_ln_b': 'new_v', 'new_v_rnn_conv_w': 'new_v', 'new_v_rnn_conv_b': 'new_v', 'new_v_w_a': 'new_v', 'new_v_b_a': 'new_v', 'new_v_w_i': 'new_v', 'new_v_b_i': 'new_v', 'new_v_lru_lambda': 'new_v', 'new_v_w_out': 'new_v', 'new_v_g_ffn2': 'new_v', 'new_v_w_ffn2_in': 'new_v', 'new_v_w_ffn2_out': 'new_v', 'new_v_w_fmod': 'new_v', 'new_v_b_fmod': 'new_v', 'new_v_g_final': 'new_v'}


def _forward(args):
    return _fwd_reference(*[args[k] for k in FWD_PARAMS])


def _output_shape():
    out = _jax.eval_shape(lambda: _forward(_fwd_setup_inputs(0)))
    return out.shape, out.dtype

N_MICROBATCH = 1
ADAM_LR = 0.001
ADAM_B1 = 0.9
ADAM_B2 = 0.999
ADAM_EPS = 1e-08
ADAM_WD = 0.01
ADAM_STEP = 10
PER_EXAMPLE_BATCH_AXIS = {'x': 0, 'c': 0, 'loss_target': 0}
SHARED_INPUTS = []
_WEIGHT_DTYPES = {'w_mod': _jnp.float32, 'b_mod': _jnp.float32, 'g_ffn1': _jnp.float32, 'w_ffn1_in': _jnp.float32, 'w_ffn1_out': _jnp.float32, 'g_mix': _jnp.float32, 'w_in': _jnp.float32, 'conv_w': _jnp.float32, 'conv_b': _jnp.float32, 'ln_g': _jnp.float32, 'ln_b': _jnp.float32, 'rnn_conv_w': _jnp.float32, 'rnn_conv_b': _jnp.float32, 'w_a': _jnp.float32, 'b_a': _jnp.float32, 'w_i': _jnp.float32, 'b_i': _jnp.float32, 'lru_lambda': _jnp.float32, 'w_out': _jnp.float32, 'g_ffn2': _jnp.float32, 'w_ffn2_in': _jnp.float32, 'w_ffn2_out': _jnp.float32, 'w_fmod': _jnp.float32, 'b_fmod': _jnp.float32, 'g_final': _jnp.float32}
MOMENT_SCALE = {'w_mod': 7.804683e-01, 'b_mod': 1.291960e+00, 'g_ffn1': 4.005124e-02, 'w_ffn1_in': 2.791556e-02, 'w_ffn1_out': 5.593924e-02, 'g_mix': 1.386097e-01, 'w_in': 5.013213e-01, 'conv_w': 1.970086e-01, 'conv_b': 1.400054e+00, 'ln_g': 6.597567e-01, 'ln_b': 8.784259e-01, 'rnn_conv_w': 8.155713e-01, 'rnn_conv_b': 2.461764e+00, 'w_a': 1.074944e-01, 'b_a': 1.062408e-01, 'w_i': 2.354822e-01, 'b_i': 3.457376e-01, 'lru_lambda': 2.550076e-01, 'w_out': 7.101688e-01, 'g_ffn2': 4.846924e-02, 'w_ffn2_in': 2.854979e-02, 'w_ffn2_out': 5.854882e-02, 'w_fmod': 8.844224e+00, 'b_fmod': 2.338480e+01, 'g_final': 3.643892e+01}


def _to_microbatches(a, axis):
    t = _jnp.moveaxis(a, axis, 0)
    t = t.reshape((N_MICROBATCH, t.shape[0] // N_MICROBATCH) + t.shape[1:])
    return _jnp.moveaxis(t, 1, axis + 1)


def setup_inputs(seed: int = 0) -> dict:
    inp = _fwd_setup_inputs(seed)
    key = _jax.random.fold_in(_jax.random.key(seed), 7919)
    shape, _ = _output_shape()
    out = dict(inp)
    out["loss_target"] = _jax.random.normal(_jax.random.fold_in(key, 0), shape, _jnp.float32)
    for i, name in enumerate(TWIN_WEIGHTS):
        w = inp[name].astype(_jnp.float32)
        if MOMENT_SCALE is None:
            s = _jnp.sqrt(_jnp.mean(_jnp.square(w)) + 1e-30)
        else:
            s = MOMENT_SCALE[name]
        km, kv = _jax.random.split(_jax.random.fold_in(key, i + 1))
        out[name] = w
        out["m_" + name] = s * _jax.random.normal(km, w.shape, _jnp.float32)
        out["v_" + name] = (s * s) * _jax.random.uniform(kv, w.shape, _jnp.float32, 0.5, 1.5)
    if N_MICROBATCH > 1:
        for name, axis in PER_EXAMPLE_BATCH_AXIS.items():
            out[name] = _to_microbatches(out[name], axis)
    return {'x': out['x'], 'c': out['c'], 'w_mod': out['w_mod'], 'b_mod': out['b_mod'], 'g_ffn1': out['g_ffn1'], 'w_ffn1_in': out['w_ffn1_in'], 'w_ffn1_out': out['w_ffn1_out'], 'g_mix': out['g_mix'], 'w_in': out['w_in'], 'conv_w': out['conv_w'], 'conv_b': out['conv_b'], 'ln_g': out['ln_g'], 'ln_b': out['ln_b'], 'rnn_conv_w': out['rnn_conv_w'], 'rnn_conv_b': out['rnn_conv_b'], 'w_a': out['w_a'], 'b_a': out['b_a'], 'w_i': out['w_i'], 'b_i': out['b_i'], 'lru_lambda': out['lru_lambda'], 'w_out': out['w_out'], 'g_ffn2': out['g_ffn2'], 'w_ffn2_in': out['w_ffn2_in'], 'w_ffn2_out': out['w_ffn2_out'], 'w_fmod': out['w_fmod'], 'b_fmod': out['b_fmod'], 'g_final': out['g_final'], 'loss_target': out['loss_target'], 'm_w_mod': out['m_w_mod'], 'm_b_mod': out['m_b_mod'], 'm_g_ffn1': out['m_g_ffn1'], 'm_w_ffn1_in': out['m_w_ffn1_in'], 'm_w_ffn1_out': out['m_w_ffn1_out'], 'm_g_mix': out['m_g_mix'], 'm_w_in': out['m_w_in'], 'm_conv_w': out['m_conv_w'], 'm_conv_b': out['m_conv_b'], 'm_ln_g': out['m_ln_g'], 'm_ln_b': out['m_ln_b'], 'm_rnn_conv_w': out['m_rnn_conv_w'], 'm_rnn_conv_b': out['m_rnn_conv_b'], 'm_w_a': out['m_w_a'], 'm_b_a': out['m_b_a'], 'm_w_i': out['m_w_i'], 'm_b_i': out['m_b_i'], 'm_lru_lambda': out['m_lru_lambda'], 'm_w_out': out['m_w_out'], 'm_g_ffn2': out['m_g_ffn2'], 'm_w_ffn2_in': out['m_w_ffn2_in'], 'm_w_ffn2_out': out['m_w_ffn2_out'], 'm_w_fmod': out['m_w_fmod'], 'm_b_fmod': out['m_b_fmod'], 'm_g_final': out['m_g_final'], 'v_w_mod': out['v_w_mod'], 'v_b_mod': out['v_b_mod'], 'v_g_ffn1': out['v_g_ffn1'], 'v_w_ffn1_in': out['v_w_ffn1_in'], 'v_w_ffn1_out': out['v_w_ffn1_out'], 'v_g_mix': out['v_g_mix'], 'v_w_in': out['v_w_in'], 'v_conv_w': out['v_conv_w'], 'v_conv_b': out['v_conv_b'], 'v_ln_g': out['v_ln_g'], 'v_ln_b': out['v_ln_b'], 'v_rnn_conv_w': out['v_rnn_conv_w'], 'v_rnn_conv_b': out['v_rnn_conv_b'], 'v_w_a': out['v_w_a'], 'v_b_a': out['v_b_a'], 'v_w_i': out['v_w_i'], 'v_b_i': out['v_b_i'], 'v_lru_lambda': out['v_lru_lambda'], 'v_w_out': out['v_w_out'], 'v_g_ffn2': out['v_g_ffn2'], 'v_w_ffn2_in': out['v_w_ffn2_in'], 'v_w_ffn2_out': out['v_w_ffn2_out'], 'v_w_fmod': out['v_w_fmod'], 'v_b_fmod': out['v_b_fmod'], 'v_g_final': out['v_g_final']}


def _loss(weights, diff, rest, loss_target):
    with _jax.named_scope("forward"):
        args = {**rest, TWIN_DIFF_INPUT: diff, **{k: w.astype(_WEIGHT_DTYPES[k]) for k, w in weights.items()}}
        y = _forward(args)
    with _jax.named_scope("loss_head"):
        err = _jnp.square(y.astype(_jnp.float32) - loss_target)
        return 0.5 * _jnp.sum(_jnp.mean(err, axis=-1)) if err.ndim else 0.5 * err


def _adamw(w, g, m, v):
    m = ADAM_B1 * m + (1.0 - ADAM_B1) * g
    v = ADAM_B2 * v + (1.0 - ADAM_B2) * _jnp.square(g)
    m_hat = m / (1.0 - ADAM_B1 ** ADAM_STEP)
    v_hat = v / (1.0 - ADAM_B2 ** ADAM_STEP)
    delta = -ADAM_LR * (m_hat / (_jnp.sqrt(v_hat) + ADAM_EPS) + ADAM_WD * w)
    return delta, m, v


def reference(x, c, w_mod, b_mod, g_ffn1, w_ffn1_in, w_ffn1_out, g_mix, w_in, conv_w, conv_b, ln_g, ln_b, rnn_conv_w, rnn_conv_b, w_a, b_a, w_i, b_i, lru_lambda, w_out, g_ffn2, w_ffn2_in, w_ffn2_out, w_fmod, b_fmod, g_final, loss_target, m_w_mod, m_b_mod, m_g_ffn1, m_w_ffn1_in, m_w_ffn1_out, m_g_mix, m_w_in, m_conv_w, m_conv_b, m_ln_g, m_ln_b, m_rnn_conv_w, m_rnn_conv_b, m_w_a, m_b_a, m_w_i, m_b_i, m_lru_lambda, m_w_out, m_g_ffn2, m_w_ffn2_in, m_w_ffn2_out, m_w_fmod, m_b_fmod, m_g_final, v_w_mod, v_b_mod, v_g_ffn1, v_w_ffn1_in, v_w_ffn1_out, v_g_mix, v_w_in, v_conv_w, v_conv_b, v_ln_g, v_ln_b, v_rnn_conv_w, v_rnn_conv_b, v_w_a, v_b_a, v_w_i, v_b_i, v_lru_lambda, v_w_out, v_g_ffn2, v_w_ffn2_in, v_w_ffn2_out, v_w_fmod, v_b_fmod, v_g_final):
    given = dict(x=x, c=c, w_mod=w_mod, b_mod=b_mod, g_ffn1=g_ffn1, w_ffn1_in=w_ffn1_in, w_ffn1_out=w_ffn1_out, g_mix=g_mix, w_in=w_in, conv_w=conv_w, conv_b=conv_b, ln_g=ln_g, ln_b=ln_b, rnn_conv_w=rnn_conv_w, rnn_conv_b=rnn_conv_b, w_a=w_a, b_a=b_a, w_i=w_i, b_i=b_i, lru_lambda=lru_lambda, w_out=w_out, g_ffn2=g_ffn2, w_ffn2_in=w_ffn2_in, w_ffn2_out=w_ffn2_out, w_fmod=w_fmod, b_fmod=b_fmod, g_final=g_final, loss_target=loss_target, m_w_mod=m_w_mod, m_b_mod=m_b_mod, m_g_ffn1=m_g_ffn1, m_w_ffn1_in=m_w_ffn1_in, m_w_ffn1_out=m_w_ffn1_out, m_g_mix=m_g_mix, m_w_in=m_w_in, m_conv_w=m_conv_w, m_conv_b=m_conv_b, m_ln_g=m_ln_g, m_ln_b=m_ln_b, m_rnn_conv_w=m_rnn_conv_w, m_rnn_conv_b=m_rnn_conv_b, m_w_a=m_w_a, m_b_a=m_b_a, m_w_i=m_w_i, m_b_i=m_b_i, m_lru_lambda=m_lru_lambda, m_w_out=m_w_out, m_g_ffn2=m_g_ffn2, m_w_ffn2_in=m_w_ffn2_in, m_w_ffn2_out=m_w_ffn2_out, m_w_fmod=m_w_fmod, m_b_fmod=m_b_fmod, m_g_final=m_g_final, v_w_mod=v_w_mod, v_b_mod=v_b_mod, v_g_ffn1=v_g_ffn1, v_w_ffn1_in=v_w_ffn1_in, v_w_ffn1_out=v_w_ffn1_out, v_g_mix=v_g_mix, v_w_in=v_w_in, v_conv_w=v_conv_w, v_conv_b=v_conv_b, v_ln_g=v_ln_g, v_ln_b=v_ln_b, v_rnn_conv_w=v_rnn_conv_w, v_rnn_conv_b=v_rnn_conv_b, v_w_a=v_w_a, v_b_a=v_b_a, v_w_i=v_w_i, v_b_i=v_b_i, v_lru_lambda=v_lru_lambda, v_w_out=v_w_out, v_g_ffn2=v_g_ffn2, v_w_ffn2_in=v_w_ffn2_in, v_w_ffn2_out=v_w_ffn2_out, v_w_fmod=v_w_fmod, v_b_fmod=v_b_fmod, v_g_final=v_g_final)
    weights = {n: given[n] for n in TWIN_WEIGHTS}
    shared = {n: given[n] for n in SHARED_INPUTS}
    per_example = {n: given[n] for n in ['x', 'c']}
    grad_fn = _jax.value_and_grad(_loss, argnums=(0, 1))

    def one_microbatch(ex, loss_target):
        ex = dict(ex)
        diff = ex.pop(TWIN_DIFF_INPUT)
        return grad_fn(weights, diff, {**shared, **ex}, loss_target)

    if N_MICROBATCH == 1:
        loss, (grad_w, grad_x) = one_microbatch(per_example, given["loss_target"])
    else:
        def body(carry, xs):
            loss_sum, grad_sum = carry
            l_k, (gw_k, gx_k) = one_microbatch(xs[0], xs[1])
            with _jax.named_scope("update"):
                return (loss_sum + l_k, _jax.tree.map(_jnp.add, grad_sum, gw_k)), gx_k

        init = (_jnp.zeros((), _jnp.float32), _jax.tree.map(_jnp.zeros_like, weights))
        (loss, grad_w), grad_x = _jax.lax.scan(body, init, (per_example, given["loss_target"]))
    with _jax.named_scope("update"):
        delta_w, new_m, new_v = {}, {}, {}
        for n in TWIN_WEIGHTS:
            delta_w[n], new_m[n], new_v[n] = _adamw(weights[n], grad_w[n], given["m_" + n], given["v_" + n])
    return (loss, grad_x, *[grad_w[n] for n in TWIN_WEIGHTS], *[delta_w[n] for n in TWIN_WEIGHTS],
            *[new_m[n] for n in TWIN_WEIGHTS], *[new_v[n] for n in TWIN_WEIGHTS])
```

```python
import functools
import math

import jax
import jax.numpy as jnp
from jax import lax
from jax.experimental import pallas as pl
from jax.experimental.pallas import tpu as pltpu

F32 = jnp.float32
BF16 = jnp.bfloat16
MESH_IDS = pl.DeviceIdType.MESH
NDEV = 8
EPS = 1e-6
RG_C = 8.0
CONV_W = 31
RNN_CONV_W = 4
LANES = 128
ADAM_LR = 0.001
ADAM_B1 = 0.9
ADAM_B2 = 0.999
ADAM_EPS = 1e-08
ADAM_WD = 0.01
ADAM_STEP = 10
SMALL_ROWS = 104
TOKEN_TILE = 512
WGRAD_TILE = 2048
HI = lax.Precision.HIGHEST


def _cp(sem, vmem_mb):
    return pltpu.CompilerParams(dimension_semantics=sem, vmem_limit_bytes=vmem_mb * 1024 * 1024)


def _dot(a, b):
    return jnp.dot(a, b, preferred_element_type=F32)


def _dot_nt(a, b):
    return lax.dot_general(a, b, (((1,), (1,)), ((), ())), preferred_element_type=F32)


def _dot_tn(a, b):
    return lax.dot_general(a, b, (((0,), (0,)), ((), ())), preferred_element_type=F32)


def _sigmoid(x):
    return 1.0 / (1.0 + jnp.exp(-x))


def _adaln(x, vec_ref):
    rstd = lax.rsqrt(jnp.mean(x * x, axis=-1, keepdims=True) + EPS)
    return (x * rstd) * vec_ref[0:1, :] * (1.0 + vec_ref[2:3, :]) + vec_ref[1:2, :]


def _adaln_bwd(x, dh, vec_ref, dvec_ref):
    rstd = lax.rsqrt(jnp.mean(x * x, axis=-1, keepdims=True) + EPS)
    xhat = x * rstd
    dvec_ref[0:1, :] += jnp.sum(dh * xhat, axis=0, keepdims=True)
    dvec_ref[1:2, :] += jnp.sum(dh, axis=0, keepdims=True)
    dxhat = dh * (vec_ref[0:1, :] * (1.0 + vec_ref[2:3, :]))
    return rstd * (dxhat - xhat * jnp.mean(dxhat * xhat, axis=-1, keepdims=True))


def _adaln_finish(vec_ref, dvec_ref):
    s = dvec_ref[0:1, :]
    dvec_ref[3:4, :] = vec_ref[0:1, :] * s
    dvec_ref[0:1, :] = (1.0 + vec_ref[2:3, :]) * s


def _gelu_and_grad(x):
    k0 = math.sqrt(2.0 / math.pi)
    x2 = x * x
    t = jnp.tanh(k0 * (x + 0.044715 * x * x2))
    g = 0.5 * x * (1.0 + t)
    dg = 0.5 * (1.0 + t) + 0.5 * x * (1.0 - t * t) * (k0 * (1.0 + 3.0 * 0.044715 * x2))
    return g, dg


def _log_sigmoid(x):
    z = jnp.exp(-jnp.abs(x))
    u = 1.0 + z
    d = u - 1.0
    log1p = jnp.where(d == 0.0, z, jnp.log(u) * (z / jnp.where(d == 0.0, 1.0, d)))
    return jnp.minimum(x, 0.0) - log1p


def _neg_expm1(x):
    series = -x * (1.0 + x * (0.5 + x * (1.0 / 6.0 + x * (1.0 / 24.0 + x * (1.0 / 120.0)))))
    return jnp.where(x > -0.05, series, 1.0 - jnp.exp(x))


def _scan_fwd(a, b):
    n = a.shape[0]
    row = lax.broadcasted_iota(jnp.int32, a.shape, 0)
    s = 1
    while s < n:
        ok = row >= s
        b = a * jnp.where(ok, pltpu.roll(b, s, 0), 0.0) + b
        if 2 * s < n:
            a = a * jnp.where(ok, pltpu.roll(a, s, 0), 1.0)
        s *= 2
    return b


def _scan_rev(a, d):
    n = a.shape[0]
    row = lax.broadcasted_iota(jnp.int32, a.shape, 0)
    s = 1
    while s < n:
        ok = row < n - s
        d = a * jnp.where(ok, pltpu.roll(d, n - s, 0), 0.0) + d
        if 2 * s < n:
            a = a * jnp.where(ok, pltpu.roll(a, n - s, 0), 1.0)
        s *= 2
    return d


def _rglru_gates(xr, wa_ref, wi_ref, rvec_ref):
    xb = xr.astype(BF16)
    r = _sigmoid(_dot(xb, wa_ref[...]) + rvec_ref[1:2, :])
    ig = _sigmoid(_dot(xb, wi_ref[...]) + rvec_ref[2:3, :])
    ls = _log_sigmoid(rvec_ref[3:4, :])
    log_a = RG_C * r * ls
    a = jnp.exp(log_a)
    mult = jnp.sqrt(_neg_expm1(2.0 * log_a))
    return xb, r, ig, ls, a, mult


def _rnn_conv(ux, rw_ref, rvec_ref, ext_ref):
    t = ux.shape[0]
    ext_ref[0:8, :] = jnp.zeros((8, ux.shape[1]), F32)
    ext_ref[8:, :] = ux
    xr = rvec_ref[0:1, :] + rw_ref[RNN_CONV_W - 1:RNN_CONV_W, :] * ux
    for k in range(RNN_CONV_W - 1):
        d = RNN_CONV_W - 1 - k
        xr = xr + rw_ref[k:k + 1, :] * ext_ref[8 - d:8 - d + t, :]
    return xr


def _ffn_fwd(x, vec, wi, wo, tm, name):
    t, d = x.shape
    nj, fb = wi.shape[1], wi.shape[3]

    def body(x_ref, vec_ref, wi_ref, wo_ref, xo_ref, h_ref, gu_ref, f_ref, acc_ref):
        j = pl.program_id(1)

        @pl.when(j == 0)
        def _():
            h_ref[...] = _adaln(x_ref[...], vec_ref).astype(BF16)
            acc_ref[...] = jnp.zeros_like(acc_ref)

        h = h_ref[...]
        gate = _dot(h, wi_ref[0])
        up = _dot(h, wi_ref[1])
        gu_ref[0] = gate.astype(BF16)
        gu_ref[1] = up.astype(BF16)
        act = (gate * _sigmoid(gate) * up).astype(BF16)
        acc_ref[...] += _dot(act, wo_ref[...])

        @pl.when(j == nj - 1)
        def _():
            f = acc_ref[...]
            f_ref[...] = f.astype(BF16)
            xo_ref[...] = x_ref[...] + 0.5 * vec_ref[3:4, :] * f

    return pl.pallas_call(
        body, name=name, grid=(t // tm, nj),
        in_specs=[pl.BlockSpec((tm, d), lambda i, j: (i, 0)),
                  pl.BlockSpec((8, d), lambda i, j: (0, 0)),
                  pl.BlockSpec((2, None, d, fb), lambda i, j: (0, j, 0, 0)),
                  pl.BlockSpec((fb, d), lambda i, j: (j, 0))],
        out_specs=[pl.BlockSpec((tm, d), lambda i, j: (i, 0)),
                   pl.BlockSpec((tm, d), lambda i, j: (i, 0)),
                   pl.BlockSpec((2, None, tm, fb), lambda i, j: (0, j, i, 0)),
                   pl.BlockSpec((tm, d), lambda i, j: (i, 0))],
        out_shape=[jax.ShapeDtypeStruct((t, d), F32), jax.ShapeDtypeStruct((t, d), BF16),
                   jax.ShapeDtypeStruct((2, nj, t, fb), BF16), jax.ShapeDtypeStruct((t, d), BF16)],
        scratch_shapes=[pltpu.VMEM((tm, d), F32)],
        compiler_params=_cp(("arbitrary", "arbitrary"), 48),
    )(x, vec, wi, wo)


def _mix_in(x, vec, win, tm, name):
    t, d = x.shape
    nb, _, cb = win.shape

    def body(x_ref, vec_ref, w_ref, h_ref, p_ref):
        h = _adaln(x_ref[...], vec_ref).astype(BF16)
        h_ref[...] = h
        for k in range(nb):
            p_ref[:, k * cb:(k + 1) * cb] = _dot(h, w_ref[k])

    return pl.pallas_call(
        body, name=name, grid=(t // tm,),
        in_specs=[pl.BlockSpec((tm, d), lambda i: (i, 0)),
                  pl.BlockSpec((8, d), lambda i: (0, 0)),
                  pl.BlockSpec((nb, d, cb), lambda i: (0, 0, 0))],
        out_specs=[pl.BlockSpec((tm, d), lambda i: (i, 0)),
                   pl.BlockSpec((tm, nb * cb), lambda i: (i, 0))],
        out_shape=[jax.ShapeDtypeStruct((t, d), BF16), jax.ShapeDtypeStruct((t, nb * cb), F32)],
        compiler_params=_cp(("arbitrary",), 48),
    )(x, vec, win)


def _conv_fwd(proj, cw32, name):
    t = proj.shape[0]
    nblk = cw32.shape[1] // LANES
    ch = min(t, 128)

    def body(val_ref, gate_ref, cw_ref, cv_ref, ext_ref):
        ext_ref[0:32, :] = jnp.zeros((32, LANES), F32)
        ext_ref[32:, :] = val_ref[...] * _sigmoid(gate_ref[...])
        for r in range(t // ch):
            acc = jnp.broadcast_to(cw_ref[31:32, :], (ch, LANES))
            for k in range(CONV_W):
                off = 32 + r * ch - (CONV_W - 1 - k)
                acc = acc + cw_ref[k:k + 1, :] * ext_ref[off:off + ch, :]
            cv_ref[r * ch:(r + 1) * ch, :] = acc

    return pl.pallas_call(
        body, name=name, grid=(nblk,),
        in_specs=[pl.BlockSpec((t, LANES), lambda c: (0, c)),
                  pl.BlockSpec((t, LANES), lambda c: (0, nblk + c)),
                  pl.BlockSpec((32, LANES), lambda c: (0, c))],
        out_specs=pl.BlockSpec((t, LANES), lambda c: (0, c)),
        out_shape=jax.ShapeDtypeStruct((t, nblk * LANES), F32),
        scratch_shapes=[pltpu.VMEM((t + 32, LANES), F32)],
        compiler_params=_cp(("arbitrary",), 48),
    )(proj, proj, cw32)


def _rnn_fwd(proj, rw8, rvec, wab, wib, name):
    t = proj.shape[0]
    nblk = rvec.shape[1] // LANES

    def body(ux_ref, uy_ref, rw_ref, rvec_ref, wa_ref, wi_ref, h_ref, yr_ref, ext_ref):
        xr = _rnn_conv(ux_ref[...], rw_ref, rvec_ref, ext_ref)
        _, _, ig, _, a, mult = _rglru_gates(xr, wa_ref, wi_ref, rvec_ref)
        h = _scan_fwd(a, mult * (ig * xr))
        h_ref[...] = h
        ge, _ = _gelu_and_grad(uy_ref[...])
        yr_ref[...] = (ge * h).astype(BF16)

    blk = lambda off: pl.BlockSpec((t, LANES), lambda c: (0, off + c))
    return pl.pallas_call(
        body, name=name, grid=(nblk,),
        in_specs=[blk(2 * nblk), blk(3 * nblk),
                  pl.BlockSpec((8, LANES), lambda c: (0, c)),
                  pl.BlockSpec((8, LANES), lambda c: (0, c)),
                  pl.BlockSpec((None, LANES, LANES), lambda c: (c, 0, 0)),
                  pl.BlockSpec((None, LANES, LANES), lambda c: (c, 0, 0))],
        out_specs=[blk(0), blk(0)],
        out_shape=[jax.ShapeDtypeStruct((t, nblk * LANES), F32), jax.ShapeDtypeStruct((t, nblk * LANES), BF16)],
        scratch_shapes=[pltpu.VMEM((t + 8, LANES), F32)],
        compiler_params=_cp(("arbitrary",), 56),
    )(proj, proj, rw8, rvec, wab, wib)


def _ln_silu(cv, lnv_ref):
    mu = jnp.mean(cv, axis=-1, keepdims=True)
    xc = cv - mu
    rs = lax.rsqrt(jnp.mean(xc * xc, axis=-1, keepdims=True) + EPS)
    chat = xc * rs
    z = chat * lnv_ref[0:1, :] + lnv_ref[1:2, :]
    sg = _sigmoid(z)
    return rs, chat, z, sg


def _mix_out(x, cv, yr, vec, lnv, wout, tm, name):
    t, d = x.shape
    dc = cv.shape[1]

    def body(x_ref, cv_ref, yr_ref, vec_ref, lnv_ref, w_ref, xo_ref, ym_ref, yc_ref):
        _, _, z, sg = _ln_silu(cv_ref[...], lnv_ref)
        yc = (z * sg).astype(BF16)
        yr = yr_ref[...]
        yc_ref[:, 0:dc] = yc
        yc_ref[:, dc:] = yr
        ym = _dot(yc, w_ref[0:dc, :]) + _dot(yr, w_ref[dc:, :])
        ym_ref[...] = ym.astype(BF16)
        xo_ref[...] = x_ref[...] + vec_ref[3:4, :] * ym

    return pl.pallas_call(
        body, name=name, grid=(t // tm,),
        in_specs=[pl.BlockSpec((tm, d), lambda i: (i, 0)),
                  pl.BlockSpec((tm, dc), lambda i: (i, 0)),
                  pl.BlockSpec((tm, dc), lambda i: (i, 0)),
                  pl.BlockSpec((8, d), lambda i: (0, 0)),
                  pl.BlockSpec((8, dc), lambda i: (0, 0)),
                  pl.BlockSpec((d, d), lambda i: (0, 0))],
        out_specs=[pl.BlockSpec((tm, d), lambda i: (i, 0)),
                   pl.BlockSpec((tm, d), lambda i: (i, 0)),
                   pl.BlockSpec((tm, d), lambda i: (i, 0))],
        out_shape=[jax.ShapeDtypeStruct((t, d), F32), jax.ShapeDtypeStruct((t, d), BF16),
                   jax.ShapeDtypeStruct((t, d), BF16)],
        compiler_params=_cp(("arbitrary",), 48),
    )(x, cv, yr, vec, lnv, wout)


def _final(x, tgt, vec, tm, name):
    t, d = x.shape
    nt = t // tm

    def body(x_ref, t_ref, vec_ref, dx_ref, dvec_ref):
        i = pl.program_id(0)

        @pl.when(i == 0)
        def _():
            dvec_ref[...] = jnp.zeros_like(dvec_ref)

        xv = x_ref[...]
        e = _adaln(xv, vec_ref) - t_ref[...]
        dvec_ref[4:5, :] += (0.5 / d) * jnp.sum(e * e, axis=0, keepdims=True)
        dx_ref[...] = _adaln_bwd(xv, e * (1.0 / d), vec_ref, dvec_ref)

        @pl.when(i == nt - 1)
        def _():
            _adaln_finish(vec_ref, dvec_ref)

    return pl.pallas_call(
        body, name=name, grid=(nt,),
        in_specs=[pl.BlockSpec((tm, d), lambda i: (i, 0)),
                  pl.BlockSpec((tm, d), lambda i: (i, 0)),
                  pl.BlockSpec((8, d), lambda i: (0, 0))],
        out_specs=[pl.BlockSpec((tm, d), lambda i: (i, 0)),
                   pl.BlockSpec((8, d), lambda i: (0, 0))],
        out_shape=[jax.ShapeDtypeStruct((t, d), F32), jax.ShapeDtypeStruct((8, d), F32)],
        compiler_params=_cp(("arbitrary",), 48),
    )(x, tgt, vec)


def _ffn_bwd(dxo, x, vec, gu, f, wi, wo, tm, name):
    t, d = x.shape
    nj, fb = wi.shape[1], wi.shape[3]
    nt = t // tm

    def body(dxo_ref, x_ref, vec_ref, gu_ref, f_ref, wi_ref, wo_ref,
             dx_ref, dgu_ref, act_ref, df_ref, dvec_ref, acc_ref):
        i = pl.program_id(0)
        j = pl.program_id(1)

        @pl.when((i == 0) & (j == 0))
        def _():
            dvec_ref[...] = jnp.zeros_like(dvec_ref)

        @pl.when(j == 0)
        def _():
            dxo_v = dxo_ref[...]
            df_ref[...] = (0.5 * vec_ref[3:4, :] * dxo_v).astype(BF16)
            dvec_ref[2:3, :] += 0.5 * jnp.sum(dxo_v * f_ref[...].astype(F32), axis=0, keepdims=True)
            acc_ref[...] = jnp.zeros_like(acc_ref)

        dact = _dot_nt(df_ref[...], wo_ref[...])
        g = gu_ref[0].astype(F32)
        u = gu_ref[1].astype(F32)
        sg = _sigmoid(g)
        sl = g * sg
        dgate = (dact * u * (sg * (1.0 + g * (1.0 - sg)))).astype(BF16)
        dup = (dact * sl).astype(BF16)
        act_ref[...] = (sl * u).astype(BF16)
        dgu_ref[0] = dgate
        dgu_ref[1] = dup
        acc_ref[...] += _dot_nt(dgate, wi_ref[0]) + _dot_nt(dup, wi_ref[1])

        @pl.when(j == nj - 1)
        def _():
            dx_ref[...] = dxo_ref[...] + _adaln_bwd(x_ref[...], acc_ref[...], vec_ref, dvec_ref)

        @pl.when((i == nt - 1) & (j == nj - 1))
        def _():
            _adaln_finish(vec_ref, dvec_ref)

    tile = pl.BlockSpec((tm, d), lambda i, j: (i, 0))
    return pl.pallas_call(
        body, name=name, grid=(nt, nj),
        in_specs=[tile, tile,
                  pl.BlockSpec((8, d), lambda i, j: (0, 0)),
                  pl.BlockSpec((2, None, tm, fb), lambda i, j: (0, j, i, 0)),
                  tile,
                  pl.BlockSpec((2, None, d, fb), lambda i, j: (0, j, 0, 0)),
                  pl.BlockSpec((fb, d), lambda i, j: (j, 0))],
        out_specs=[tile,
                   pl.BlockSpec((2, None, tm, fb), lambda i, j: (0, j, i, 0)),
                   pl.BlockSpec((None, tm, fb), lambda i, j: (j, i, 0)),
                   tile,
                   pl.BlockSpec((8, d), lambda i, j: (0, 0))],
        out_shape=[jax.ShapeDtypeStruct((t, d), F32), jax.ShapeDtypeStruct((2, nj, t, fb), BF16),
                   jax.ShapeDtypeStruct((nj, t, fb), BF16), jax.ShapeDtypeStruct((t, d), BF16),
                   jax.ShapeDtypeStruct((8, d), F32)],
        scratch_shapes=[pltpu.VMEM((tm, d), F32)],
        compiler_params=_cp(("arbitrary", "arbitrary"), 56),
    )(dxo, x, vec, gu, f, wi, wo)


def _mm_tn(a, b, a_spec, b_spec, nblk, nk, m, n, name):
    def body(a_ref, b_ref, o_ref, acc_ref):
        s = pl.program_id(1)

        @pl.when(s == 0)
        def _():
            acc_ref[...] = jnp.zeros_like(acc_ref)

        acc_ref[...] += _dot_tn(a_ref[...], b_ref[...])

        @pl.when(s == nk - 1)
        def _():
            o_ref[...] = acc_ref[...].astype(BF16)

    return pl.pallas_call(
        body, name=name, grid=(nblk, nk),
        in_specs=[a_spec, b_spec],
        out_specs=pl.BlockSpec((None, m, n), lambda k, s: (k, 0, 0)),
        out_shape=jax.ShapeDtypeStruct((nblk, m, n), BF16),
        scratch_shapes=[pltpu.VMEM((m, n), F32)],
        compiler_params=_cp(("arbitrary", "arbitrary"), 56),
    )(a, b)


def _mixout_bwd(dxo, ym, cv, hr, proj, vec, lnv, wout, tm, name):
    t, d = dxo.shape
    dc = cv.shape[1]
    nt = t // tm

    def body(dxo_ref, ym_ref, cv_ref, hr_ref, uy_ref, vec_ref, lnv_ref, w_ref,
             dym_ref, dcv_ref, dhr_ref, duy_ref, dln_ref, dgt_ref):
        i = pl.program_id(0)

        @pl.when(i == 0)
        def _():
            dln_ref[...] = jnp.zeros_like(dln_ref)
            dgt_ref[...] = jnp.zeros_like(dgt_ref)

        dxo_v = dxo_ref[...]
        dym = (vec_ref[3:4, :] * dxo_v).astype(BF16)
        dym_ref[...] = dym
        dgt_ref[0:1, :] += jnp.sum(dxo_v * ym_ref[...].astype(F32), axis=0, keepdims=True)
        dyc = _dot_nt(dym, w_ref[0:dc, :])
        dyr = _dot_nt(dym, w_ref[dc:, :])
        rs, chat, z, sg = _ln_silu(cv_ref[...], lnv_ref)
        dz = dyc * (sg * (1.0 + z * (1.0 - sg)))
        dln_ref[0:1, :] += jnp.sum(dz * chat, axis=0, keepdims=True)
        dln_ref[1:2, :] += jnp.sum(dz, axis=0, keepdims=True)
        dchat = dz * lnv_ref[0:1, :]
        dcv_ref[...] = rs * (dchat - jnp.mean(dchat, axis=-1, keepdims=True)
                             - chat * jnp.mean(dchat * chat, axis=-1, keepdims=True))
        ge, dge = _gelu_and_grad(uy_ref[...])
        dhr_ref[...] = dyr * ge
        duy_ref[...] = (dyr * hr_ref[...] * dge).astype(BF16)

    tile_d = pl.BlockSpec((tm, d), lambda i: (i, 0))
    tile_c = pl.BlockSpec((tm, dc), lambda i: (i, 0))
    return pl.pallas_call(
        body, name=name, grid=(nt,),
        in_specs=[tile_d, tile_d, tile_c, tile_c,
                  pl.BlockSpec((tm, dc), lambda i: (i, 3)),
                  pl.BlockSpec((8, d), lambda i: (0, 0)),
                  pl.BlockSpec((8, dc), lambda i: (0, 0)),
                  pl.BlockSpec((d, d), lambda i: (0, 0))],
        out_specs=[tile_d, tile_c, tile_c, tile_c,
                   pl.BlockSpec((8, dc), lambda i: (0, 0)),
                   pl.BlockSpec((8, d), lambda i: (0, 0))],
        out_shape=[jax.ShapeDtypeStruct((t, d), BF16), jax.ShapeDtypeStruct((t, dc), F32),
                   jax.ShapeDtypeStruct((t, dc), F32), jax.ShapeDtypeStruct((t, dc), BF16),
                   jax.ShapeDtypeStruct((8, dc), F32), jax.ShapeDtypeStruct((8, d), F32)],
        compiler_params=_cp(("arbitrary",), 48),
    )(dxo, ym, cv, hr, proj, vec, lnv, wout)


def _conv_bwd(proj, dcv, cw32, name):
    t = proj.shape[0]
    nblk = cw32.shape[1] // LANES
    ch = min(t, 128)

    def body(val_ref, gate_ref, dcv_ref, cw_ref, dval_ref, dgate_ref, dcw_ref, extu_ref, extd_ref):
        val = val_ref[...]
        sg = _sigmoid(gate_ref[...])
        extu_ref[0:32, :] = jnp.zeros((32, LANES), F32)
        extu_ref[32:, :] = val * sg
        dcv_v = dcv_ref[...]
        extd_ref[0:t, :] = dcv_v
        extd_ref[t:, :] = jnp.zeros((32, LANES), F32)
        for r in range(t // ch):
            acc = jnp.zeros((ch, LANES), F32)
            for k in range(CONV_W):
                off = r * ch + (CONV_W - 1 - k)
                acc = acc + cw_ref[k:k + 1, :] * extd_ref[off:off + ch, :]
            rows = slice(r * ch, (r + 1) * ch)
            sg_r = _sigmoid(gate_ref[rows, :])
            dval_ref[rows, :] = (acc * sg_r).astype(BF16)
            dgate_ref[rows, :] = (acc * val_ref[rows, :] * sg_r * (1.0 - sg_r)).astype(BF16)
        for k in range(CONV_W):
            off = 32 - (CONV_W - 1 - k)
            dcw_ref[k:k + 1, :] = jnp.sum(dcv_v * extu_ref[off:off + t, :], axis=0, keepdims=True)
        dcw_ref[31:32, :] = jnp.sum(dcv_v, axis=0, keepdims=True)

    blk = lambda off: pl.BlockSpec((t, LANES), lambda c: (0, off + c))
    return pl.pallas_call(
        body, name=name, grid=(nblk,),
        in_specs=[blk(0), blk(nblk), blk(0), pl.BlockSpec((32, LANES), lambda c: (0, c))],
        out_specs=[blk(0), blk(0), pl.BlockSpec((32, LANES), lambda c: (0, c))],
        out_shape=[jax.ShapeDtypeStruct((t, nblk * LANES), BF16), jax.ShapeDtypeStruct((t, nblk * LANES), BF16),
                   jax.ShapeDtypeStruct((32, nblk * LANES), F32)],
        scratch_shapes=[pltpu.VMEM((t + 32, LANES), F32), pltpu.VMEM((t + 32, LANES), F32)],
        compiler_params=_cp(("arbitrary",), 56),
    )(proj, proj, dcv, cw32)


def _rnn_bwd(proj, hr, dhr, rw8, rvec, wab, wib, name):
    t = proj.shape[0]
    nblk = rvec.shape[1] // LANES

    def body(ux_ref, h_ref, dh_ref, rw_ref, rvec_ref, wa_ref, wi_ref,
             dux_ref, sm_ref, dwa_ref, dwi_ref, ext_ref, extd_ref):
        xr = _rnn_conv(ux_ref[...], rw_ref, rvec_ref, ext_ref)
        xb, r, ig, ls, a, mult = _rglru_gates(xr, wa_ref, wi_ref, rvec_ref)
        row = lax.broadcasted_iota(jnp.int32, (t, LANES), 0)
        a_next = jnp.where(row < t - 1, pltpu.roll(a, t - 1, 0), 0.0)
        g = _scan_rev(a_next, dh_ref[...])
        hprev = jnp.where(row >= 1, pltpu.roll(h_ref[...], 1, 0), 0.0)
        da = g * hprev
        dmult = g * (ig * xr)
        dig = g * mult * xr
        dxr = g * mult * ig
        dlog_a = a * (da - dmult * a / mult)
        dr = dlog_a * (RG_C * ls)
        dls = RG_C * jnp.sum(dlog_a * r, axis=0, keepdims=True)
        dpr = dr * r * (1.0 - r)
        dpi = dig * ig * (1.0 - ig)
        dprb = dpr.astype(BF16)
        dpib = dpi.astype(BF16)
        dxr = dxr + _dot_nt(dprb, wa_ref[...]) + _dot_nt(dpib, wi_ref[...])
        dwa_ref[...] = _dot_tn(xb, dprb)
        dwi_ref[...] = _dot_tn(xb, dpib)
        extd_ref[0:t, :] = dxr
        extd_ref[t:, :] = jnp.zeros((8, LANES), F32)
        dux = rw_ref[RNN_CONV_W - 1:RNN_CONV_W, :] * dxr
        for k in range(RNN_CONV_W - 1):
            d = RNN_CONV_W - 1 - k
            dux = dux + rw_ref[k:k + 1, :] * extd_ref[d:d + t, :]
        dux_ref[...] = dux.astype(BF16)
        for k in range(RNN_CONV_W):
            d = RNN_CONV_W - 1 - k
            sm_ref[k:k + 1, :] = jnp.sum(dxr * ext_ref[8 - d:8 - d + t, :], axis=0, keepdims=True)
        sm_ref[4:5, :] = jnp.sum(dxr, axis=0, keepdims=True)
        sm_ref[5:6, :] = jnp.sum(dpr, axis=0, keepdims=True)
        sm_ref[6:7, :] = jnp.sum(dpi, axis=0, keepdims=True)
        sm_ref[7:8, :] = dls * _sigmoid(-rvec_ref[3:4, :])

    blk = lambda off: pl.BlockSpec((t, LANES), lambda c: (0, off + c))
    sq = pl.BlockSpec((None, LANES, LANES), lambda c: (c, 0, 0))
    return pl.pallas_call(
        body, name=name, grid=(nblk,),
        in_specs=[blk(2 * nblk), blk(0), blk(0),
                  pl.BlockSpec((8, LANES), lambda c: (0, c)),
                  pl.BlockSpec((8, LANES), lambda c: (0, c)), sq, sq],
        out_specs=[blk(0), pl.BlockSpec((8, LANES), lambda c: (0, c)), sq, sq],
        out_shape=[jax.ShapeDtypeStruct((t, nblk * LANES), BF16), jax.ShapeDtypeStruct((8, nblk * LANES), F32),
                   jax.ShapeDtypeStruct((nblk, LANES, LANES), F32), jax.ShapeDtypeStruct((nblk, LANES, LANES), F32)],
        scratch_shapes=[pltpu.VMEM((t + 8, LANES), F32), pltpu.VMEM((t + 8, LANES), F32)],
        compiler_params=_cp(("arbitrary",), 60),
    )(proj, hr, dhr, rw8, rvec, wab, wib)


def _mixin_bwd(dxo, x, parts, vec, win, tm, name):
    t, d = x.shape
    nb, _, cb = win.shape
    dc = parts[0].shape[1]
    per = dc // cb
    nt = t // tm

    def body(dxo_ref, x_ref, p0, p1, p2, p3, vec_ref, w_ref, dx_ref, dvec_ref):
        i = pl.program_id(0)

        @pl.when(i == 0)
        def _():
            dvec_ref[...] = jnp.zeros_like(dvec_ref)

        prefs = (p0, p1, p2, p3)
        dh = jnp.zeros((tm, d), F32)
        for k in range(nb):
            dh = dh + _dot_nt(prefs[k // per][:, (k % per) * cb:(k % per + 1) * cb], w_ref[k])
        dx_ref[...] = dxo_ref[...] + _adaln_bwd(x_ref[...], dh, vec_ref, dvec_ref)

        @pl.when(i == nt - 1)
        def _():
            _adaln_finish(vec_ref, dvec_ref)

    tile_d = pl.BlockSpec((tm, d), lambda i: (i, 0))
    tile_c = pl.BlockSpec((tm, dc), lambda i: (i, 0))
    return pl.pallas_call(
        body, name=name, grid=(nt,),
        in_specs=[tile_d, tile_d, tile_c, tile_c, tile_c, tile_c,
                  pl.BlockSpec((8, d), lambda i: (0, 0)),
                  pl.BlockSpec((nb, d, cb), lambda i: (0, 0, 0))],
        out_specs=[tile_d, pl.BlockSpec((8, d), lambda i: (0, 0))],
        out_shape=[jax.ShapeDtypeStruct((t, d), F32), jax.ShapeDtypeStruct((8, d), F32)],
        compiler_params=_cp(("arbitrary",), 48),
    )(dxo, x, *parts, vec, win)


def _coords():
    return lax.axis_index("x"), lax.axis_index("y"), lax.axis_index("c")


def _flip(v, bit):
    return 1 - v if bit else v


def _all_gather_weights(shards, name):
    n = len(shards)

    def body(*refs):
        ins, outs = refs[:n], refs[n:2 * n]
        send_sems, recv_sems, loc_sems = refs[2 * n:]
        x, y, c = _coords()
        me = 4 * x + 2 * y + c
        sib = (x, y, 1 - c)
        chips = [(1 - x, y), (x, 1 - y), (1 - x, 1 - y)]

        def cp(a, k, block, to, src=None):
            dst = outs[a].at[block]
            return pltpu.make_async_remote_copy(
                src_ref=dst if src is None else src, dst_ref=dst,
                send_sem=send_sems.at[a, k], recv_sem=recv_sems.at[a, k],
                device_id=to, device_id_type=MESH_IDS)

        local = [pltpu.make_async_copy(ins[a], outs[a].at[me], loc_sems.at[a]) for a in range(n)]
        for cpy in local:
            cpy.start()
        first = []
        for a in range(n):
            first.append(cp(a, 0, me, sib, src=ins[a]))
            for j, (cx, cy) in enumerate(chips):
                first.append(cp(a, 1 + j, me, (cx, cy, c), src=ins[a]))
        for cpy in first:
            cpy.start()
        passed = []
        for a in range(n):
            for j, (cx, cy) in enumerate(chips):
                blk = 4 * cx + 2 * cy + c
                cp(a, 1 + j, blk, sib).wait_recv()
                fwd = cp(a, 4 + j, blk, sib)
                fwd.start()
                passed.append(fwd)
        for a in range(n):
            cp(a, 0, 4 * x + 2 * y + (1 - c), sib).wait_recv()
            for j, (cx, cy) in enumerate(chips):
                cp(a, 4 + j, 4 * cx + 2 * cy + (1 - c), sib).wait_recv()
        for cpy in first + passed:
            cpy.wait_send()
        for cpy in local:
            cpy.wait()

    anyspec = pl.BlockSpec(memory_space=pl.ANY)
    return pl.pallas_call(
        body, name=name,
        in_specs=[anyspec] * n, out_specs=[anyspec] * n,
        out_shape=[jax.ShapeDtypeStruct((NDEV,) + s.shape, s.dtype) for s in shards],
        scratch_shapes=[pltpu.SemaphoreType.DMA((n, 7)), pltpu.SemaphoreType.DMA((n, 7)),
                        pltpu.SemaphoreType.DMA((n,))],
    )(*shards)


def _rs_sibling(parts, name):
    n = len(parts)

    def body(*refs):
        ins, outs = refs[:n], refs[n:2 * n]
        send_sems, recv_sems = refs[2 * n:]
        x, y, c = _coords()
        copies = []
        for a in range(n):
            for q in range(4):
                copies.append(pltpu.make_async_remote_copy(
                    src_ref=ins[a].at[2 * q + (1 - c)], dst_ref=outs[a].at[q],
                    send_sem=send_sems.at[a, q], recv_sem=recv_sems.at[a, q],
                    device_id=(x, y, 1 - c), device_id_type=MESH_IDS))
        for cpy in copies:
            cpy.start()
        for cpy in copies:
            cpy.wait()

    anyspec = pl.BlockSpec(memory_space=pl.ANY)
    return pl.pallas_call(
        body, name=name,
        in_specs=[anyspec] * n, out_specs=[anyspec] * n,
        out_shape=[jax.ShapeDtypeStruct((4,) + p.shape[1:], p.dtype) for p in parts],
        scratch_shapes=[pltpu.SemaphoreType.DMA((n, 4)), pltpu.SemaphoreType.DMA((n, 4))],
    )(*parts)


def _chip_sum(part, recv, sel, tr, name):
    _, _, r, c = part.shape

    def body(sel_ref, p_ref, r_ref, cs_ref, own_ref):
        q = pl.program_id(1)
        s = p_ref[...].astype(F32) + r_ref[...].astype(F32)
        cs_ref[...] = s.astype(BF16)

        @pl.when(q == sel_ref[1])
        def _():
            own_ref[...] = s

    return pl.pallas_call(
        body, name=name,
        grid_spec=pltpu.PrefetchScalarGridSpec(
            num_scalar_prefetch=1, grid=(r // tr, 4),
            in_specs=[pl.BlockSpec((None, None, tr, c), lambda i, q, s: (q, s[0], i, 0)),
                      pl.BlockSpec((None, tr, c), lambda i, q, s: (q, i, 0))],
            out_specs=[pl.BlockSpec((None, tr, c), lambda i, q, s: (q, i, 0)),
                       pl.BlockSpec((tr, c), lambda i, q, s: (i, 0))]),
        out_shape=[jax.ShapeDtypeStruct((4, r, c), BF16), jax.ShapeDtypeStruct((r, c), F32)],
        compiler_params=_cp(("arbitrary", "arbitrary"), 48),
    )(sel, part, recv)


def _rs_chips(sums, name):
    n = len(sums)
    rel = [(1, 0), (0, 1), (1, 1)]

    def body(*refs):
        ins, outs = refs[:n], refs[n:2 * n]
        send_sems, recv_sems = refs[2 * n:]
        x, y, c = _coords()
        copies = []
        for a in range(n):
            for k, (kx, ky) in enumerate(rel):
                tx, ty = _flip(x, kx), _flip(y, ky)
                copies.append(pltpu.make_async_remote_copy(
                    src_ref=ins[a].at[2 * tx + ty], dst_ref=outs[a].at[k],
                    send_sem=send_sems.at[a, k], recv_sem=recv_sems.at[a, k],
                    device_id=(tx, ty, c), device_id_type=MESH_IDS))
        for cpy in copies:
            cpy.start()
        for cpy in copies:
            cpy.wait()

    anyspec = pl.BlockSpec(memory_space=pl.ANY)
    return pl.pallas_call(
        body, name=name,
        in_specs=[anyspec] * n, out_specs=[anyspec] * n,
        out_shape=[jax.ShapeDtypeStruct((3,) + s.shape[1:], s.dtype) for s in sums],
        scratch_shapes=[pltpu.SemaphoreType.DMA((n, 3)), pltpu.SemaphoreType.DMA((n, 3))],
    )(*sums)


def _gather_direct(src_ref, buf_ref, send_sems, recv_sems):
    x, y, c = _coords()
    me = 4 * x + 2 * y + c
    buf_ref[me] = src_ref[...]
    copies = []
    for k in range(1, NDEV):
        kx, ky, kc = (k >> 2) & 1, (k >> 1) & 1, k & 1
        copies.append(pltpu.make_async_remote_copy(
            src_ref=src_ref, dst_ref=buf_ref.at[me],
            send_sem=send_sems.at[k - 1], recv_sem=recv_sems.at[k - 1],
            device_id=(_flip(x, kx), _flip(y, ky), _flip(c, kc)), device_id_type=MESH_IDS))
    for cpy in copies:
        cpy.start()
    for k in range(1, NDEV):
        kx, ky, kc = (k >> 2) & 1, (k >> 1) & 1, k & 1
        peer = 4 * _flip(x, kx) + 2 * _flip(y, ky) + _flip(c, kc)
        pltpu.make_async_remote_copy(
            src_ref=src_ref, dst_ref=buf_ref.at[peer],
            send_sem=send_sems.at[k - 1], recv_sem=recv_sems.at[k - 1],
            device_id=(x, y, c), device_id_type=MESH_IDS).wait_recv()
    for cpy in copies:
        cpy.wait_send()
    return me


def _mod_exchange(c_row, wmod, bmod, wfmod, bfmod, name):
    d = c_row.shape[1]
    nm, nf = wmod.shape[1], wfmod.shape[1]
    nw = nm + nf

    def body(c_ref, wm_ref, bm_ref, wf_ref, bf_ref, cs_ref, mod_ref, fmod_ref,
             slab_ref, csbuf_ref, mslab_ref, mbuf_ref, s1, r1, s2, r2):
        cv = c_ref[...]
        slab_ref[...] = jnp.broadcast_to(cv * _sigmoid(cv), (8, d))
        _gather_direct(slab_ref, csbuf_ref, s1, r1)
        for b in range(NDEV):
            cs_ref[b:b + 1, :] = csbuf_ref[b, 0:1, :]
        cs = cs_ref[...]
        mslab_ref[:, 0:nm] = jnp.dot(cs, wm_ref[...], precision=HI, preferred_element_type=F32) + bm_ref[...]
        mslab_ref[:, nm:] = jnp.dot(cs, wf_ref[...], precision=HI, preferred_element_type=F32) + bf_ref[...]
        me = _gather_direct(mslab_ref, mbuf_ref, s2, r2)
        mine = lax.broadcasted_iota(jnp.int32, (8, nw), 0) == me
        for k in range(NDEV):
            rowk = jnp.sum(jnp.where(mine, mbuf_ref[k], 0.0), axis=0, keepdims=True)
            mod_ref[k:k + 1, :] = rowk[:, 0:nm]
            fmod_ref[k:k + 1, :] = rowk[:, nm:]

    vm = pl.BlockSpec(memory_space=pltpu.VMEM)
    return pl.pallas_call(
        body, name=name,
        in_specs=[vm] * 5, out_specs=[vm] * 3,
        out_shape=[jax.ShapeDtypeStruct((NDEV, d), F32), jax.ShapeDtypeStruct((NDEV, nm), F32),
                   jax.ShapeDtypeStruct((NDEV, nf), F32)],
        scratch_shapes=[pltpu.VMEM((8, d), F32), pltpu.VMEM((NDEV, 8, d), F32),
                        pltpu.VMEM((8, nw), F32), pltpu.VMEM((NDEV, 8, nw), F32),
                        pltpu.SemaphoreType.DMA((7,)), pltpu.SemaphoreType.DMA((7,)),
                        pltpu.SemaphoreType.DMA((7,)), pltpu.SemaphoreType.DMA((7,))],
        compiler_params=pltpu.CompilerParams(vmem_limit_bytes=40 * 1024 * 1024),
    )(c_row, wmod, bmod, wfmod, bfmod)


def _small_exchange(pack, name):
    def body(p_ref, all_ref, sum_ref, s, r):
        _gather_direct(p_ref, all_ref, s, r)
        tot = all_ref[0]
        for k in range(1, NDEV):
            tot = tot + all_ref[k]
        sum_ref[...] = tot

    vm = pl.BlockSpec(memory_space=pltpu.VMEM)
    return pl.pallas_call(
        body, name=name,
        in_specs=[vm], out_specs=[vm, vm],
        out_shape=[jax.ShapeDtypeStruct((NDEV,) + pack.shape, F32), jax.ShapeDtypeStruct(pack.shape, F32)],
        scratch_shapes=[pltpu.SemaphoreType.DMA((7,)), pltpu.SemaphoreType.DMA((7,))],
    )(pack)


def _adamw_math(w, g, m, v):
    m = ADAM_B1 * m + (1.0 - ADAM_B1) * g
    v = ADAM_B2 * v + (1.0 - ADAM_B2) * (g * g)
    m_hat = m / (1.0 - ADAM_B1 ** ADAM_STEP)
    v_hat = v / (1.0 - ADAM_B2 ** ADAM_STEP)
    delta = -ADAM_LR * (m_hat / (jnp.sqrt(v_hat) + ADAM_EPS) + ADAM_WD * w)
    return delta, m, v


def _adamw_small(w, g, m, v, name):
    def body(w_ref, g_ref, m_ref, v_ref, d_ref, mo_ref, vo_ref):
        d_ref[...], mo_ref[...], vo_ref[...] = _adamw_math(w_ref[...], g_ref[...], m_ref[...], v_ref[...])

    vm = pl.BlockSpec(memory_space=pltpu.VMEM)
    sds = jax.ShapeDtypeStruct(w.shape, F32)
    return pl.pallas_call(body, name=name, in_specs=[vm] * 4, out_specs=[vm] * 3,
                          out_shape=[sds, sds, sds])(w, g, m, v)


def _rs_final(own, recv, w, m, v, tr, name):
    r, c = own.shape

    def body(o_ref, r_ref, w_ref, m_ref, v_ref, g_ref, d_ref, mo_ref, vo_ref):
        g = o_ref[...] + r_ref[0].astype(F32) + r_ref[1].astype(F32) + r_ref[2].astype(F32)
        g_ref[...] = g
        d_ref[...], mo_ref[...], vo_ref[...] = _adamw_math(w_ref[...], g, m_ref[...], v_ref[...])

    tile = pl.BlockSpec((tr, c), lambda i: (i, 0))
    sds = jax.ShapeDtypeStruct((r, c), F32)
    return pl.pallas_call(
        body, name=name, grid=(r // tr,),
        in_specs=[tile, pl.BlockSpec((3, tr, c), lambda i: (0, i, 0)), tile, tile, tile],
        out_specs=[tile] * 4, out_shape=[sds] * 4,
        compiler_params=_cp(("arbitrary",), 48),
    )(own, recv, w, m, v)


def _mod_weight_update(cs, dm, w, m, v, tr, name):
    r, c = w.shape

    def body(cs_ref, dm_ref, w_ref, m_ref, v_ref, g_ref, d_ref, mo_ref, vo_ref):
        g = lax.dot_general(cs_ref[...], dm_ref[...], (((0,), (0,)), ((), ())),
                            precision=HI, preferred_element_type=F32)
        g_ref[...] = g
        d_ref[...], mo_ref[...], vo_ref[...] = _adamw_math(w_ref[...], g, m_ref[...], v_ref[...])

    tile = pl.BlockSpec((tr, c), lambda i: (i, 0))
    sds = jax.ShapeDtypeStruct((r, c), F32)
    return pl.pallas_call(
        body, name=name, grid=(r // tr,),
        in_specs=[pl.BlockSpec((NDEV, tr), lambda i: (0, i)), pl.BlockSpec((NDEV, c), lambda i: (0, 0)),
                  tile, tile, tile],
        out_specs=[tile] * 4, out_shape=[sds] * 4,
        compiler_params=_cp(("arbitrary",), 48),
    )(cs, dm, w, m, v)


def _rows(*vs):
    d = vs[0].shape[-1]
    rows = [v.reshape(1, d) for v in vs]
    return jnp.concatenate(rows + [jnp.zeros((8 - len(rows), d), F32)], axis=0)


def _block_diag_pairs(w):
    hd = w.shape[-1]
    z = jnp.zeros((w.shape[0] // 2, hd, hd), w.dtype)
    top = jnp.concatenate([w[0::2], z], axis=2)
    bot = jnp.concatenate([z, w[1::2]], axis=2)
    return jnp.concatenate([top, bot], axis=1).astype(BF16)


def _diag_pairs(g):
    hd = g.shape[-1] // 2
    both = jnp.stack([g[:, :hd, :hd], g[:, hd:, hd:]], axis=1)
    return both.reshape(2 * g.shape[0], hd, hd)


def kernel(x, c, w_mod, b_mod, g_ffn1, w_ffn1_in, w_ffn1_out, g_mix, w_in, conv_w, conv_b, ln_g, ln_b, rnn_conv_w, rnn_conv_b, w_a, b_a, w_i, b_i, lru_lambda, w_out, g_ffn2, w_ffn2_in, w_ffn2_out, w_fmod, b_fmod, g_final, loss_target, m_w_mod, m_b_mod, m_g_ffn1, m_w_ffn1_in, m_w_ffn1_out, m_g_mix, m_w_in, m_conv_w, m_conv_b, m_ln_g, m_ln_b, m_rnn_conv_w, m_rnn_conv_b, m_w_a, m_b_a, m_w_i, m_b_i, m_lru_lambda, m_w_out, m_g_ffn2, m_w_ffn2_in, m_w_ffn2_out, m_w_fmod, m_b_fmod, m_g_final, v_w_mod, v_b_mod, v_g_ffn1, v_w_ffn1_in, v_w_ffn1_out, v_g_mix, v_w_in, v_conv_w, v_conv_b, v_ln_g, v_ln_b, v_rnn_conv_w, v_rnn_conv_b, v_w_a, v_b_a, v_w_i, v_b_i, v_lru_lambda, v_w_out, v_g_ffn2, v_w_ffn2_in, v_w_ffn2_out, v_w_fmod, v_b_fmod, v_g_final):
    t, d = x.shape[1], x.shape[2]
    fb = w_ffn1_in.shape[2]
    nm = w_mod.shape[2]
    nf = w_fmod.shape[1]
    dc = conv_b.shape[1]
    cl = conv_w.shape[2]
    tm = min(TOKEN_TILE, t)
    tk = min(WGRAD_TILE, t)
    nk = t // tk
    me = 4 * lax.axis_index("x") + 2 * lax.axis_index("y") + lax.axis_index("c")

    bmod_l = lax.dynamic_slice(b_mod, (0, me * nm), (1, nm))
    bfmod_l = lax.dynamic_slice(b_fmod.reshape(1, -1), (0, me * nf), (1, nf))
    cs, mod_rows, fmod_rows = _mod_exchange(c, w_mod[0], bmod_l, w_fmod, bfmod_l, "mod_exchange")
    mod = mod_rows.reshape(9, d)
    fmod = fmod_rows.reshape(2, d)
    vec1 = _rows(g_ffn1, mod[0], mod[1], mod[2])
    vecm = _rows(g_mix, mod[3], mod[4], mod[5])
    vec3 = _rows(g_ffn2, mod[6], mod[7], mod[8])
    vecf = _rows(g_final, fmod[0], fmod[1])

    shards = [w_ffn1_in[0], w_ffn1_out[0], w_in[0], w_out[0], w_ffn2_in[0], w_ffn2_out[0]]
    cwl = jnp.concatenate([conv_w[0], jnp.zeros((1, cl), F32), rnn_conv_w[0], jnp.zeros((4, cl), F32)], axis=0)
    wi1, wo1, win, wout, wi2, wo2, cwg = _all_gather_weights(
        [s.astype(BF16) for s in shards] + [cwl], "gather_weights")
    wi1 = wi1.reshape(2, 4, d, fb)
    wi2 = wi2.reshape(2, 4, d, fb)
    wo1 = wo1.reshape(4 * fb, d)
    wo2 = wo2.reshape(4 * fb, d)
    wout = wout.reshape(d, d)

    xin = x[0]
    x1, h1, gu1, f1 = _ffn_fwd(xin, vec1, wi1, wo1, tm, "ffn1_fwd")
    h2, proj = _mix_in(x1, vecm, win, tm, "mix_in")
    lnv = _rows(ln_g, ln_b)
    rvec = _rows(rnn_conv_b, b_a, b_i, lru_lambda)
    wab = _block_diag_pairs(w_a[0])
    wib = _block_diag_pairs(w_i[0])
    cwf = jnp.transpose(cwg, (1, 0, 2)).reshape(40, NDEV * cl)
    cw32 = jnp.concatenate([cwf[0:CONV_W], conv_b], axis=0)
    rw8 = cwf[32:40]

    cv = _conv_fwd(proj, cw32, "conv_fwd")
    hr, yr = _rnn_fwd(proj, rw8, rvec, wab, wib, "rnn_fwd")
    x2, ym, ycat = _mix_out(x1, cv, yr, vecm, lnv, wout, tm, "mix_out")
    x3, h3, gu3, f3 = _ffn_fwd(x2, vec3, wi2, wo2, tm, "ffn2_fwd")

    dx3, dvf = _final(x3, loss_target[0], vecf, tm, "final_loss")
    dx2, dgu3, act3, df3, dv3 = _ffn_bwd(dx3, x2, vec3, gu3, f3, wi2, wo2, tm, "ffn2_bwd")
    dym, dcv, dhr, duy, dln, dgt2 = _mixout_bwd(dx2, ym, cv, hr, proj, vecm, lnv, wout, tm, "mixout_bwd")
    dval, dgate, dcw = _conv_bwd(proj, dcv, cw32, "conv_bwd")
    dux, rsm, dwab, dwib = _rnn_bwd(proj, hr, dhr, rw8, rvec, wab, wib, "rnn_bwd")
    parts = [dval, dgate, dux, duy]
    dx1, dvm = _mixin_bwd(dx2, x1, parts, vecm, win, tm, "mixin_bwd")
    dx0, dgu1, act1, df1, dv1 = _ffn_bwd(dx1, xin, vec1, gu1, f1, wi1, wo1, tm, "ffn1_bwd")

    a_tok = lambda width: pl.BlockSpec((tk, width), lambda k, s: (s, 0))
    blk3 = lambda width: pl.BlockSpec((None, tk, width), lambda k, s: (k, s, 0))
    p_wi2 = _mm_tn(h3, dgu3.reshape(8, t, fb), a_tok(d), blk3(fb), 8, nk, d, fb, "wgrad_ffn2_in")
    p_wo2 = _mm_tn(act3, df3, blk3(fb), a_tok(d), 4, nk, fb, d, "wgrad_ffn2_out")
    p_wout = _mm_tn(ycat, dym, pl.BlockSpec((tk, LANES), lambda k, s: (s, k)), a_tok(d),
                    NDEV, nk, LANES, d, "wgrad_out")
    cbw = win.shape[2]
    p_win = jnp.concatenate(
        [_mm_tn(h2, p, a_tok(d), pl.BlockSpec((tk, cbw), lambda k, s: (s, k)), dc // cbw, nk, d, cbw,
                "wgrad_in_%d" % n) for n, p in enumerate(parts)], axis=0)
    p_wi1 = _mm_tn(h1, dgu1.reshape(8, t, fb), a_tok(d), blk3(fb), 8, nk, d, fb, "wgrad_ffn1_in")
    p_wo1 = _mm_tn(act1, df1, blk3(fb), a_tok(d), 4, nk, fb, d, "wgrad_ffn1_out")
    p_wo1 = p_wo1.reshape(NDEV, fb // 2, d)
    p_wo2 = p_wo2.reshape(NDEV, fb // 2, d)

    partials = [p_wi1, p_wo1, p_win, p_wout, p_wi2, p_wo2]
    from_sib = _rs_sibling(partials, "rs_sibling")
    sel = jnp.stack([lax.axis_index("c"), 2 * lax.axis_index("x") + lax.axis_index("y")]).astype(jnp.int32)
    tiles = [256, fb // 4, 512, 128, 256, fb // 4]
    sums, owns = [], []
    for n, (p, r) in enumerate(zip(partials, from_sib)):
        s_bf, own = _chip_sum(p.reshape((4, 2) + p.shape[1:]), r, sel, tiles[n], "chip_sum_%d" % n)
        sums.append(s_bf)
        owns.append(own)
    from_chips = _rs_chips(sums, "rs_chips")
    big = [(w_ffn1_in, m_w_ffn1_in, v_w_ffn1_in), (w_ffn1_out, m_w_ffn1_out, v_w_ffn1_out),
           (w_in, m_w_in, v_w_in), (w_out, m_w_out, v_w_out),
           (w_ffn2_in, m_w_ffn2_in, v_w_ffn2_in), (w_ffn2_out, m_w_ffn2_out, v_w_ffn2_out)]
    big_names = ["w_ffn1_in", "w_ffn1_out", "w_in", "w_out", "w_ffn2_in", "w_ffn2_out"]
    res = {}
    for n, (own, rc, (w, m, v)) in enumerate(zip(owns, from_chips, big)):
        out4 = _rs_final(own, rc, w[0], m[0], v[0], tiles[n], "rs_final_%d" % n)
        res[big_names[n]] = [o[None] for o in out4]

    dmod_row = jnp.concatenate([dv1[1:2], dv1[3:4], dv1[2:3], dvm[1:2], dvm[3:4], dgt2[0:1],
                                dv3[1:2], dv3[3:4], dv3[2:3]], axis=0)
    pack = jnp.concatenate([
        dmod_row, dvf[1:2], dvf[3:4],
        dv1[0:1], dvm[0:1], dv3[0:1], dvf[0:1],
        dcw.reshape(16, d),
        jnp.concatenate([dln[0:1], dln[1:2]], axis=1),
        rsm.reshape(4, d),
        _diag_pairs(dwab).reshape(32, d), _diag_pairs(dwib).reshape(32, d),
        dvf[4:5],
        jnp.zeros((SMALL_ROWS - 101, d), F32)], axis=0)
    allp, tot = _small_exchange(pack, "small_exchange")
    loss = jnp.sum(tot[100])

    dm_all = allp[:, 0:9, :].reshape(NDEV, 9 * d)
    dfm_all = allp[:, 9:11, :].reshape(NDEV, 2 * d)
    dm_l = lax.dynamic_slice(dm_all, (0, me * nm), (NDEV, nm))
    dfm_l = lax.dynamic_slice(dfm_all, (0, me * nf), (NDEV, nf))
    out_wmod = [o[None] for o in _mod_weight_update(cs, dm_l, w_mod[0], m_w_mod[0], v_w_mod[0], 256, "w_mod_update")]
    out_wfmod = _mod_weight_update(cs, dfm_l, w_fmod, m_w_fmod, v_w_fmod, 256, "w_fmod_update")
    res["w_mod"] = out_wmod
    res["w_fmod"] = list(out_wfmod)

    dcw_f = tot[15:31].reshape(32, dc)
    rsm_f = tot[32:36].reshape(8, dc)
    small_grads = {
        "b_mod": tot[0:9].reshape(1, 9 * d),
        "b_fmod": tot[9:11].reshape(2 * d),
        "g_ffn1": tot[11:12], "g_mix": tot[12:13], "g_ffn2": tot[13:14], "g_final": tot[14],
        "conv_w": lax.dynamic_slice(dcw_f, (0, me * cl), (CONV_W, cl))[None],
        "conv_b": dcw_f[31:32],
        "ln_g": tot[31:32, 0:dc], "ln_b": tot[31:32, dc:],
        "rnn_conv_w": lax.dynamic_slice(rsm_f, (0, me * cl), (RNN_CONV_W, cl))[None],
        "rnn_conv_b": rsm_f[4:5], "b_a": rsm_f[5:6], "b_i": rsm_f[6:7], "lru_lambda": rsm_f[7:8],
        "w_a": tot[36:68].reshape(w_a.shape), "w_i": tot[68:100].reshape(w_i.shape),
    }
    small_params = {
        "b_mod": (b_mod, m_b_mod, v_b_mod), "b_fmod": (b_fmod, m_b_fmod, v_b_fmod),
        "g_ffn1": (g_ffn1, m_g_ffn1, v_g_ffn1), "g_mix": (g_mix, m_g_mix, v_g_mix),
        "g_ffn2": (g_ffn2, m_g_ffn2, v_g_ffn2), "g_final": (g_final, m_g_final, v_g_final),
        "conv_w": (conv_w, m_conv_w, v_conv_w), "conv_b": (conv_b, m_conv_b, v_conv_b),
        "ln_g": (ln_g, m_ln_g, v_ln_g), "ln_b": (ln_b, m_ln_b, v_ln_b),
        "rnn_conv_w": (rnn_conv_w, m_rnn_conv_w, v_rnn_conv_w),
        "rnn_conv_b": (rnn_conv_b, m_rnn_conv_b, v_rnn_conv_b),
        "w_a": (w_a, m_w_a, v_w_a), "b_a": (b_a, m_b_a, v_b_a),
        "w_i": (w_i, m_w_i, v_w_i), "b_i": (b_i, m_b_i, v_b_i),
        "lru_lambda": (lru_lambda, m_lru_lambda, v_lru_lambda),
    }
    for name, g in small_grads.items():
        w, m, v = small_params[name]
        shp = w.shape
        two_d = (-1, shp[-1]) if w.ndim > 1 else (1, shp[0])
        outs = _adamw_small(w.reshape(two_d), g.reshape(two_d), m.reshape(two_d), v.reshape(two_d),
                            "adamw_" + name)
        res[name] = [g.reshape(shp)] + [o.reshape(shp) for o in outs]

    order = ["w_mod", "b_mod", "g_ffn1", "w_ffn1_in", "w_ffn1_out", "g_mix", "w_in", "conv_w", "conv_b",
             "ln_g", "ln_b", "rnn_conv_w", "rnn_conv_b", "w_a", "b_a", "w_i", "b_i", "lru_lambda", "w_out",
             "g_ffn2", "w_ffn2_in", "w_ffn2_out", "w_fmod", "b_fmod", "g_final"]
    return (loss, dx0[None], *[res[n][0] for n in order], *[res[n][1] for n in order],
            *[res[n][2] for n in order], *[res[n][3] for n in order])
```

```python
import functools
import math

import jax
import jax.numpy as jnp
from jax import lax
from jax.experimental import pallas as pl
from jax.experimental.pallas import tpu as pltpu

F32 = jnp.float32
BF16 = jnp.bfloat16
MESH_IDS = pl.DeviceIdType.MESH
NDEV = 8
EPS = 1e-6
RG_C = 8.0
CONV_W = 31
RNN_CONV_W = 4
LANES = 128
ADAM_LR = 0.001
ADAM_B1 = 0.9
ADAM_B2 = 0.999
ADAM_EPS = 1e-08
ADAM_WD = 0.01
ADAM_STEP = 10
SMALL_ROWS = 104
TOKEN_TILE = 512
WGRAD_TILE = 2048
HI = lax.Precision.HIGHEST


def _cp(sem, vmem_mb):
    return pltpu.CompilerParams(dimension_semantics=sem, vmem_limit_bytes=vmem_mb * 1024 * 1024)


def _dot(a, b):
    return jnp.dot(a, b, preferred_element_type=F32)


def _dot_nt(a, b):
    return lax.dot_general(a, b, (((1,), (1,)), ((), ())), preferred_element_type=F32)


def _dot_tn(a, b):
    return lax.dot_general(a, b, (((0,), (0,)), ((), ())), preferred_element_type=F32)


def _sigmoid(x):
    return 1.0 / (1.0 + jnp.exp(-x))


def _adaln(x, vec_ref):
    rstd = lax.rsqrt(jnp.mean(x * x, axis=-1, keepdims=True) + EPS)
    return (x * rstd) * vec_ref[0:1, :] * (1.0 + vec_ref[2:3, :]) + vec_ref[1:2, :]


def _adaln_bwd(x, dh, vec_ref, dvec_ref):
    rstd = lax.rsqrt(jnp.mean(x * x, axis=-1, keepdims=True) + EPS)
    xhat = x * rstd
    dvec_ref[0:1, :] += jnp.sum(dh * xhat, axis=0, keepdims=True)
    dvec_ref[1:2, :] += jnp.sum(dh, axis=0, keepdims=True)
    dxhat = dh * (vec_ref[0:1, :] * (1.0 + vec_ref[2:3, :]))
    return rstd * (dxhat - xhat * jnp.mean(dxhat * xhat, axis=-1, keepdims=True))


def _adaln_finish(vec_ref, dvec_ref):
    s = dvec_ref[0:1, :]
    dvec_ref[3:4, :] = vec_ref[0:1, :] * s
    dvec_ref[0:1, :] = (1.0 + vec_ref[2:3, :]) * s


def _gelu_and_grad(x):
    k0 = math.sqrt(2.0 / math.pi)
    x2 = x * x
    t = jnp.tanh(k0 * (x + 0.044715 * x * x2))
    g = 0.5 * x * (1.0 + t)
    dg = 0.5 * (1.0 + t) + 0.5 * x * (1.0 - t * t) * (k0 * (1.0 + 3.0 * 0.044715 * x2))
    return g, dg


def _log_sigmoid(x):
    z = jnp.exp(-jnp.abs(x))
    u = 1.0 + z
    d = u - 1.0
    log1p = jnp.where(d == 0.0, z, jnp.log(u) * (z / jnp.where(d == 0.0, 1.0, d)))
    return jnp.minimum(x, 0.0) - log1p


def _neg_expm1(x):
    series = -x * (1.0 + x * (0.5 + x * (1.0 / 6.0 + x * (1.0 / 24.0 + x * (1.0 / 120.0)))))
    return jnp.where(x > -0.05, series, 1.0 - jnp.exp(x))


def _scan_fwd(a, b):
    n = a.shape[0]
    row = lax.broadcasted_iota(jnp.int32, a.shape, 0)
    s = 1
    while s < n:
        ok = row >= s
        b = a * jnp.where(ok, pltpu.roll(b, s, 0), 0.0) + b
        if 2 * s < n:
            a = a * jnp.where(ok, pltpu.roll(a, s, 0), 1.0)
        s *= 2
    return b


def _scan_rev(a, d):
    n = a.shape[0]
    row = lax.broadcasted_iota(jnp.int32, a.shape, 0)
    s = 1
    while s < n:
        ok = row < n - s
        d = a * jnp.where(ok, pltpu.roll(d, n - s, 0), 0.0) + d
        if 2 * s < n:
            a = a * jnp.where(ok, pltpu.roll(a, n - s, 0), 1.0)
        s *= 2
    return d


def _rglru_gates(xr, wa_ref, wi_ref, rvec_ref):
    xb = xr.astype(BF16)
    r = _sigmoid(_dot(xb, wa_ref[...]) + rvec_ref[1:2, :])
    ig = _sigmoid(_dot(xb, wi_ref[...]) + rvec_ref[2:3, :])
    ls = _log_sigmoid(rvec_ref[3:4, :])
    log_a = RG_C * r * ls
    a = jnp.exp(log_a)
    mult = jnp.sqrt(_neg_expm1(2.0 * log_a))
    return xb, r, ig, ls, a, mult


def _rnn_conv(ux, rw_ref, rvec_ref, ext_ref):
    t = ux.shape[0]
    ext_ref[0:8, :] = jnp.zeros((8, ux.shape[1]), F32)
    ext_ref[8:, :] = ux
    xr = rvec_ref[0:1, :] + rw_ref[RNN_CONV_W - 1:RNN_CONV_W, :] * ux
    for k in range(RNN_CONV_W - 1):
        d = RNN_CONV_W - 1 - k
        xr = xr + rw_ref[k:k + 1, :] * ext_ref[8 - d:8 - d + t, :]
    return xr


def _ffn_fwd(x, vec, wi, wo, tm, name, gather=()):
    t, d = x.shape
    nj, fb = wi.shape[1], wi.shape[2]
    nt = t // tm
    n = len(gather)

    def body(*refs):
        x_ref, vec_ref, wi_ref, wo_ref = refs[:4]
        g_in = refs[4:4 + n]
        xo_ref, h_ref, gu_ref, f_ref = refs[4 + n:8 + n]
        g_out = refs[8 + n:8 + 2 * n]
        acc_ref = refs[8 + 2 * n]
        sems = refs[9 + 2 * n:]
        i = pl.program_id(0)
        j = pl.program_id(1)

        if n:
            @pl.when((i == 0) & (j == 0))
            def _():
                _gather_start(g_in, g_out, *sems)

        @pl.when(j == 0)
        def _():
            h_ref[...] = _adaln(x_ref[...], vec_ref).astype(BF16)
            acc_ref[...] = jnp.zeros_like(acc_ref)

        h = h_ref[...]
        gate = _dot_nt(h, wi_ref[0])
        up = _dot_nt(h, wi_ref[1])
        gu_ref[0] = gate.astype(BF16)
        gu_ref[1] = up.astype(BF16)
        act = (gate * _sigmoid(gate) * up).astype(BF16)
        acc_ref[...] += _dot(act, wo_ref[...])

        @pl.when(j == nj - 1)
        def _():
            f = acc_ref[...]
            f_ref[...] = f.astype(BF16)
            xo_ref[...] = x_ref[...] + 0.5 * vec_ref[3:4, :] * f

        if n:
            @pl.when((i == nt - 1) & (j == nj - 1))
            def _():
                _gather_finish(g_in, g_out, *sems)

    anyspec = pl.BlockSpec(memory_space=pl.ANY)
    tile = pl.BlockSpec((tm, d), lambda i, j: (i, 0))
    return pl.pallas_call(
        body, name=name, grid=(nt, nj),
        in_specs=[tile,
                  pl.BlockSpec((8, d), lambda i, j: (0, 0)),
                  pl.BlockSpec((2, None, fb, d), lambda i, j: (0, j, 0, 0)),
                  pl.BlockSpec((fb, d), lambda i, j: (j, 0))] + [anyspec] * n,
        out_specs=[tile, tile,
                   pl.BlockSpec((2, None, tm, fb), lambda i, j: (0, j, i, 0)),
                   tile] + [anyspec] * n,
        out_shape=[jax.ShapeDtypeStruct((t, d), F32), jax.ShapeDtypeStruct((t, d), BF16),
                   jax.ShapeDtypeStruct((2, nj, t, fb), BF16), jax.ShapeDtypeStruct((t, d), BF16)]
        + _gather_shapes(gather),
        scratch_shapes=[pltpu.VMEM((tm, d), F32)] + (_gather_sems(n) if n else []),
        compiler_params=_cp(("arbitrary", "arbitrary"), 48),
    )(x, vec, wi, wo, *gather)


def _mix_in(x, vec, win, tm, name):
    t, d = x.shape
    nb, _, cb = win.shape

    def body(x_ref, vec_ref, w_ref, h_ref, p_ref):
        h = _adaln(x_ref[...], vec_ref).astype(BF16)
        h_ref[...] = h
        for k in range(nb):
            p_ref[:, k * cb:(k + 1) * cb] = _dot(h, w_ref[k])

    return pl.pallas_call(
        body, name=name, grid=(t // tm,),
        in_specs=[pl.BlockSpec((tm, d), lambda i: (i, 0)),
                  pl.BlockSpec((8, d), lambda i: (0, 0)),
                  pl.BlockSpec((nb, d, cb), lambda i: (0, 0, 0))],
        out_specs=[pl.BlockSpec((tm, d), lambda i: (i, 0)),
                   pl.BlockSpec((tm, nb * cb), lambda i: (i, 0))],
        out_shape=[jax.ShapeDtypeStruct((t, d), BF16), jax.ShapeDtypeStruct((t, nb * cb), F32)],
        compiler_params=_cp(("arbitrary",), 48),
    )(x, vec, win)


def _conv_fwd(proj, cw32, name):
    t = proj.shape[0]
    nblk = cw32.shape[1] // LANES
    ch = min(t, 128)

    def body(val_ref, gate_ref, cw_ref, cv_ref, ext_ref):
        ext_ref[0:32, :] = jnp.zeros((32, LANES), F32)
        ext_ref[32:, :] = val_ref[...] * _sigmoid(gate_ref[...])
        for r in range(t // ch):
            acc = jnp.broadcast_to(cw_ref[31:32, :], (ch, LANES))
            for k in range(CONV_W):
                off = 32 + r * ch - (CONV_W - 1 - k)
                acc = acc + cw_ref[k:k + 1, :] * ext_ref[off:off + ch, :]
            cv_ref[r * ch:(r + 1) * ch, :] = acc

    return pl.pallas_call(
        body, name=name, grid=(nblk,),
        in_specs=[pl.BlockSpec((t, LANES), lambda c: (0, c)),
                  pl.BlockSpec((t, LANES), lambda c: (0, nblk + c)),
                  pl.BlockSpec((32, LANES), lambda c: (0, c))],
        out_specs=pl.BlockSpec((t, LANES), lambda c: (0, c)),
        out_shape=jax.ShapeDtypeStruct((t, nblk * LANES), F32),
        scratch_shapes=[pltpu.VMEM((t + 32, LANES), F32)],
        compiler_params=_cp(("arbitrary",), 48),
    )(proj, proj, cw32)


def _rnn_fwd(proj, rw8, rvec, wab, wib, name):
    t = proj.shape[0]
    nblk = rvec.shape[1] // LANES

    def body(ux_ref, uy_ref, rw_ref, rvec_ref, wa_ref, wi_ref, h_ref, yr_ref, ext_ref):
        xr = _rnn_conv(ux_ref[...], rw_ref, rvec_ref, ext_ref)
        _, _, ig, _, a, mult = _rglru_gates(xr, wa_ref, wi_ref, rvec_ref)
        h = _scan_fwd(a, mult * (ig * xr))
        h_ref[...] = h
        ge, _ = _gelu_and_grad(uy_ref[...])
        yr_ref[...] = (ge * h).astype(BF16)

    blk = lambda off: pl.BlockSpec((t, LANES), lambda c: (0, off + c))
    return pl.pallas_call(
        body, name=name, grid=(nblk,),
        in_specs=[blk(2 * nblk), blk(3 * nblk),
                  pl.BlockSpec((8, LANES), lambda c: (0, c)),
                  pl.BlockSpec((8, LANES), lambda c: (0, c)),
                  pl.BlockSpec((None, LANES, LANES), lambda c: (c, 0, 0)),
                  pl.BlockSpec((None, LANES, LANES), lambda c: (c, 0, 0))],
        out_specs=[blk(0), blk(0)],
        out_shape=[jax.ShapeDtypeStruct((t, nblk * LANES), F32), jax.ShapeDtypeStruct((t, nblk * LANES), BF16)],
        scratch_shapes=[pltpu.VMEM((t + 8, LANES), F32)],
        compiler_params=_cp(("arbitrary",), 56),
    )(proj, proj, rw8, rvec, wab, wib)


def _ln_silu(cv, lnv_ref):
    mu = jnp.mean(cv, axis=-1, keepdims=True)
    xc = cv - mu
    rs = lax.rsqrt(jnp.mean(xc * xc, axis=-1, keepdims=True) + EPS)
    chat = xc * rs
    z = chat * lnv_ref[0:1, :] + lnv_ref[1:2, :]
    sg = _sigmoid(z)
    return rs, chat, z, sg


def _mix_out(x, cv, yr, vec, lnv, wout, tm, name):
    t, d = x.shape
    dc = cv.shape[1]

    def body(x_ref, cv_ref, yr_ref, vec_ref, lnv_ref, w_ref, xo_ref, ym_ref, yc_ref):
        _, _, z, sg = _ln_silu(cv_ref[...], lnv_ref)
        yc = (z * sg).astype(BF16)
        yr = yr_ref[...]
        yc_ref[:, 0:dc] = yc
        yc_ref[:, dc:] = yr
        ym = _dot(yc, w_ref[0:dc, :]) + _dot(yr, w_ref[dc:, :])
        ym_ref[...] = ym.astype(BF16)
        xo_ref[...] = x_ref[...] + vec_ref[3:4, :] * ym

    return pl.pallas_call(
        body, name=name, grid=(t // tm,),
        in_specs=[pl.BlockSpec((tm, d), lambda i: (i, 0)),
                  pl.BlockSpec((tm, dc), lambda i: (i, 0)),
                  pl.BlockSpec((tm, dc), lambda i: (i, 0)),
                  pl.BlockSpec((8, d), lambda i: (0, 0)),
                  pl.BlockSpec((8, dc), lambda i: (0, 0)),
                  pl.BlockSpec((d, d), lambda i: (0, 0))],
        out_specs=[pl.BlockSpec((tm, d), lambda i: (i, 0)),
                   pl.BlockSpec((tm, d), lambda i: (i, 0)),
                   pl.BlockSpec((tm, d), lambda i: (i, 0))],
        out_shape=[jax.ShapeDtypeStruct((t, d), F32), jax.ShapeDtypeStruct((t, d), BF16),
                   jax.ShapeDtypeStruct((t, d), BF16)],
        compiler_params=_cp(("arbitrary",), 48),
    )(x, cv, yr, vec, lnv, wout)


def _final(x, tgt, vec, tm, name):
    t, d = x.shape
    nt = t // tm

    def body(x_ref, t_ref, vec_ref, dx_ref, dvec_ref):
        i = pl.program_id(0)

        @pl.when(i == 0)
        def _():
            dvec_ref[...] = jnp.zeros_like(dvec_ref)

        xv = x_ref[...]
        e = _adaln(xv, vec_ref) - t_ref[...]
        dvec_ref[4:5, :] += (0.5 / d) * jnp.sum(e * e, axis=0, keepdims=True)
        dx_ref[...] = _adaln_bwd(xv, e * (1.0 / d), vec_ref, dvec_ref)

        @pl.when(i == nt - 1)
        def _():
            _adaln_finish(vec_ref, dvec_ref)

    return pl.pallas_call(
        body, name=name, grid=(nt,),
        in_specs=[pl.BlockSpec((tm, d), lambda i: (i, 0)),
                  pl.BlockSpec((tm, d), lambda i: (i, 0)),
                  pl.BlockSpec((8, d), lambda i: (0, 0))],
        out_specs=[pl.BlockSpec((tm, d), lambda i: (i, 0)),
                   pl.BlockSpec((8, d), lambda i: (0, 0))],
        out_shape=[jax.ShapeDtypeStruct((t, d), F32), jax.ShapeDtypeStruct((8, d), F32)],
        compiler_params=_cp(("arbitrary",), 48),
    )(x, tgt, vec)


def _ffn_bwd(dxo, x, vec, gu, f, wi, wo, tm, name, scatter=()):
    t, d = x.shape
    nj, fb = wi.shape[1], wi.shape[2]
    nt = t // tm
    n = len(scatter)

    def body(*refs):
        dxo_ref, x_ref, vec_ref, gu_ref, f_ref, wi_ref, wo_ref = refs[:7]
        s_in = refs[7:7 + n]
        dx_ref, dgu_ref, act_ref, df_ref, dvec_ref = refs[7 + n:12 + n]
        s_out = refs[12 + n:12 + 2 * n]
        acc_ref = refs[12 + 2 * n]
        sems = refs[13 + 2 * n:]
        i = pl.program_id(0)
        j = pl.program_id(1)

        @pl.when((i == 0) & (j == 0))
        def _():
            dvec_ref[...] = jnp.zeros_like(dvec_ref)
            if n:
                for cpy in _chips_copies(s_in, s_out, *sems):
                    cpy.start()

        @pl.when(j == 0)
        def _():
            dxo_v = dxo_ref[...]
            df_ref[...] = (0.5 * vec_ref[3:4, :] * dxo_v).astype(BF16)
            dvec_ref[2:3, :] += 0.5 * jnp.sum(dxo_v * f_ref[...].astype(F32), axis=0, keepdims=True)
            acc_ref[...] = jnp.zeros_like(acc_ref)

        dact = _dot_nt(df_ref[...], wo_ref[...])
        g = gu_ref[0].astype(F32)
        u = gu_ref[1].astype(F32)
        sg = _sigmoid(g)
        sl = g * sg
        dgate = (dact * u * (sg * (1.0 + g * (1.0 - sg)))).astype(BF16)
        dup = (dact * sl).astype(BF16)
        act_ref[...] = (sl * u).astype(BF16)
        dgu_ref[0] = dgate
        dgu_ref[1] = dup
        acc_ref[...] += _dot(dgate, wi_ref[0]) + _dot(dup, wi_ref[1])

        @pl.when(j == nj - 1)
        def _():
            dx_ref[...] = dxo_ref[...] + _adaln_bwd(x_ref[...], acc_ref[...], vec_ref, dvec_ref)

        @pl.when((i == nt - 1) & (j == nj - 1))
        def _():
            _adaln_finish(vec_ref, dvec_ref)
            if n:
                for cpy in _chips_copies(s_in, s_out, *sems):
                    cpy.wait()

    anyspec = pl.BlockSpec(memory_space=pl.ANY)
    tile = pl.BlockSpec((tm, d), lambda i, j: (i, 0))
    chip_sems = [pltpu.SemaphoreType.DMA((n, 3)), pltpu.SemaphoreType.DMA((n, 3))] if n else []
    return pl.pallas_call(
        body, name=name, grid=(nt, nj),
        in_specs=[tile, tile,
                  pl.BlockSpec((8, d), lambda i, j: (0, 0)),
                  pl.BlockSpec((2, None, tm, fb), lambda i, j: (0, j, i, 0)),
                  tile,
                  pl.BlockSpec((2, None, fb, d), lambda i, j: (0, j, 0, 0)),
                  pl.BlockSpec((fb, d), lambda i, j: (j, 0))] + [anyspec] * n,
        out_specs=[tile,
                   pl.BlockSpec((2, None, tm, fb), lambda i, j: (0, j, i, 0)),
                   pl.BlockSpec((None, tm, fb), lambda i, j: (j, i, 0)),
                   tile,
                   pl.BlockSpec((8, d), lambda i, j: (0, 0))] + [anyspec] * n,
        out_shape=[jax.ShapeDtypeStruct((t, d), F32), jax.ShapeDtypeStruct((2, nj, t, fb), BF16),
                   jax.ShapeDtypeStruct((nj, t, fb), BF16), jax.ShapeDtypeStruct((t, d), BF16),
                   jax.ShapeDtypeStruct((8, d), F32)] + _chips_shapes(scatter),
        scratch_shapes=[pltpu.VMEM((tm, d), F32)] + chip_sems,
        compiler_params=_cp(("arbitrary", "arbitrary"), 56),
    )(dxo, x, vec, gu, f, wi, wo, *scatter)


def _mm_tn(a, b, a_spec, b_spec, nblk, nk, m, n, name):
    def body(a_ref, b_ref, o_ref, acc_ref):
        s = pl.program_id(1)

        @pl.when(s == 0)
        def _():
            acc_ref[...] = jnp.zeros_like(acc_ref)

        acc_ref[...] += _dot_tn(a_ref[...], b_ref[...])

        @pl.when(s == nk - 1)
        def _():
            o_ref[...] = acc_ref[...].astype(BF16)

    return pl.pallas_call(
        body, name=name, grid=(nblk, nk),
        in_specs=[a_spec, b_spec],
        out_specs=pl.BlockSpec((None, m, n), lambda k, s: (k, 0, 0)),
        out_shape=jax.ShapeDtypeStruct((nblk, m, n), BF16),
        scratch_shapes=[pltpu.VMEM((m, n), F32)],
        compiler_params=_cp(("arbitrary", "arbitrary"), 56),
    )(a, b)


def _mixout_bwd(dxo, ym, cv, hr, proj, vec, lnv, wout, tm, name, sibling=()):
    t, d = dxo.shape
    dc = cv.shape[1]
    nt = t // tm
    n = len(sibling)

    def body(*refs):
        dxo_ref, ym_ref, cv_ref, hr_ref, uy_ref, vec_ref, lnv_ref, w_ref = refs[:8]
        s_in = refs[8:8 + n]
        dym_ref, dcv_ref, dhr_ref, duy_ref, dln_ref, dgt_ref = refs[8 + n:14 + n]
        s_out = refs[14 + n:14 + 2 * n]
        sems = refs[14 + 2 * n:]
        i = pl.program_id(0)

        @pl.when(i == 0)
        def _():
            dln_ref[...] = jnp.zeros_like(dln_ref)
            dgt_ref[...] = jnp.zeros_like(dgt_ref)
            if n:
                for cpy in _sibling_copies(s_in, s_out, *sems):
                    cpy.start()

        dxo_v = dxo_ref[...]
        dym = (vec_ref[3:4, :] * dxo_v).astype(BF16)
        dym_ref[...] = dym
        dgt_ref[0:1, :] += jnp.sum(dxo_v * ym_ref[...].astype(F32), axis=0, keepdims=True)
        dyc = _dot_nt(dym, w_ref[0:dc, :])
        dyr = _dot_nt(dym, w_ref[dc:, :])
        rs, chat, z, sg = _ln_silu(cv_ref[...], lnv_ref)
        dz = dyc * (sg * (1.0 + z * (1.0 - sg)))
        dln_ref[0:1, :] += jnp.sum(dz * chat, axis=0, keepdims=True)
        dln_ref[1:2, :] += jnp.sum(dz, axis=0, keepdims=True)
        dchat = dz * lnv_ref[0:1, :]
        dcv_ref[...] = rs * (dchat - jnp.mean(dchat, axis=-1, keepdims=True)
                             - chat * jnp.mean(dchat * chat, axis=-1, keepdims=True))
        ge, dge = _gelu_and_grad(uy_ref[...])
        dhr_ref[...] = dyr * ge
        duy_ref[...] = (dyr * hr_ref[...] * dge).astype(BF16)

        if n:
            @pl.when(i == nt - 1)
            def _():
                for cpy in _sibling_copies(s_in, s_out, *sems):
                    cpy.wait()

    anyspec = pl.BlockSpec(memory_space=pl.ANY)
    tile_d = pl.BlockSpec((tm, d), lambda i: (i, 0))
    tile_c = pl.BlockSpec((tm, dc), lambda i: (i, 0))
    sib_sems = [pltpu.SemaphoreType.DMA((n, 4)), pltpu.SemaphoreType.DMA((n, 4))] if n else []
    return pl.pallas_call(
        body, name=name, grid=(nt,),
        in_specs=[tile_d, tile_d, tile_c, tile_c,
                  pl.BlockSpec((tm, dc), lambda i: (i, 3)),
                  pl.BlockSpec((8, d), lambda i: (0, 0)),
                  pl.BlockSpec((8, dc), lambda i: (0, 0)),
                  pl.BlockSpec((d, d), lambda i: (0, 0))] + [anyspec] * n,
        out_specs=[tile_d, tile_c, tile_c, tile_c,
                   pl.BlockSpec((8, dc), lambda i: (0, 0)),
                   pl.BlockSpec((8, d), lambda i: (0, 0))] + [anyspec] * n,
        out_shape=[jax.ShapeDtypeStruct((t, d), BF16), jax.ShapeDtypeStruct((t, dc), F32),
                   jax.ShapeDtypeStruct((t, dc), F32), jax.ShapeDtypeStruct((t, dc), BF16),
                   jax.ShapeDtypeStruct((8, dc), F32), jax.ShapeDtypeStruct((8, d), F32)] + _sibling_shapes(sibling),
        scratch_shapes=sib_sems,
        compiler_params=_cp(("arbitrary",), 48),
    )(dxo, ym, cv, hr, proj, vec, lnv, wout, *sibling)


def _conv_bwd(proj, dcv, cw32, name):
    t = proj.shape[0]
    nblk = cw32.shape[1] // LANES
    ch = min(t, 128)

    def body(val_ref, gate_ref, dcv_ref, cw_ref, dval_ref, dgate_ref, dcw_ref, extu_ref, extd_ref):
        val = val_ref[...]
        sg = _sigmoid(gate_ref[...])
        extu_ref[0:32, :] = jnp.zeros((32, LANES), F32)
        extu_ref[32:, :] = val * sg
        dcv_v = dcv_ref[...]
        extd_ref[0:t, :] = dcv_v
        extd_ref[t:, :] = jnp.zeros((32, LANES), F32)
        for r in range(t // ch):
            acc = jnp.zeros((ch, LANES), F32)
            for k in range(CONV_W):
                off = r * ch + (CONV_W - 1 - k)
                acc = acc + cw_ref[k:k + 1, :] * extd_ref[off:off + ch, :]
            rows = slice(r * ch, (r + 1) * ch)
            sg_r = _sigmoid(gate_ref[rows, :])
            dval_ref[rows, :] = (acc * sg_r).astype(BF16)
            dgate_ref[rows, :] = (acc * val_ref[rows, :] * sg_r * (1.0 - sg_r)).astype(BF16)
        for k in range(CONV_W):
            off = 32 - (CONV_W - 1 - k)
            dcw_ref[k:k + 1, :] = jnp.sum(dcv_v * extu_ref[off:off + t, :], axis=0, keepdims=True)
        dcw_ref[31:32, :] = jnp.sum(dcv_v, axis=0, keepdims=True)

    blk = lambda off: pl.BlockSpec((t, LANES), lambda c: (0, off + c))
    return pl.pallas_call(
        body, name=name, grid=(nblk,),
        in_specs=[blk(0), blk(nblk), blk(0), pl.BlockSpec((32, LANES), lambda c: (0, c))],
        out_specs=[blk(0), blk(0), pl.BlockSpec((32, LANES), lambda c: (0, c))],
        out_shape=[jax.ShapeDtypeStruct((t, nblk * LANES), BF16), jax.ShapeDtypeStruct((t, nblk * LANES), BF16),
                   jax.ShapeDtypeStruct((32, nblk * LANES), F32)],
        scratch_shapes=[pltpu.VMEM((t + 32, LANES), F32), pltpu.VMEM((t + 32, LANES), F32)],
        compiler_params=_cp(("arbitrary",), 56),
    )(proj, proj, dcv, cw32)


def _rnn_bwd(proj, hr, dhr, rw8, rvec, wab, wib, name):
    t = proj.shape[0]
    nblk = rvec.shape[1] // LANES

    def body(ux_ref, h_ref, dh_ref, rw_ref, rvec_ref, wa_ref, wi_ref,
             dux_ref, sm_ref, dwa_ref, dwi_ref, ext_ref, extd_ref):
        xr = _rnn_conv(ux_ref[...], rw_ref, rvec_ref, ext_ref)
        xb, r, ig, ls, a, mult = _rglru_gates(xr, wa_ref, wi_ref, rvec_ref)
        row = lax.broadcasted_iota(jnp.int32, (t, LANES), 0)
        a_next = jnp.where(row < t - 1, pltpu.roll(a, t - 1, 0), 0.0)
        g = _scan_rev(a_next, dh_ref[...])
        hprev = jnp.where(row >= 1, pltpu.roll(h_ref[...], 1, 0), 0.0)
        da = g * hprev
        dmult = g * (ig * xr)
        dig = g * mult * xr
        dxr = g * mult * ig
        dlog_a = a * (da - dmult * a / mult)
        dr = dlog_a * (RG_C * ls)
        dls = RG_C * jnp.sum(dlog_a * r, axis=0, keepdims=True)
        dpr = dr * r * (1.0 - r)
        dpi = dig * ig * (1.0 - ig)
        dprb = dpr.astype(BF16)
        dpib = dpi.astype(BF16)
        dxr = dxr + _dot_nt(dprb, wa_ref[...]) + _dot_nt(dpib, wi_ref[...])
        dwa_ref[...] = _dot_tn(xb, dprb)
        dwi_ref[...] = _dot_tn(xb, dpib)
        extd_ref[0:t, :] = dxr
        extd_ref[t:, :] = jnp.zeros((8, LANES), F32)
        dux = rw_ref[RNN_CONV_W - 1:RNN_CONV_W, :] * dxr
        for k in range(RNN_CONV_W - 1):
            d = RNN_CONV_W - 1 - k
            dux = dux + rw_ref[k:k + 1, :] * extd_ref[d:d + t, :]
        dux_ref[...] = dux.astype(BF16)
        for k in range(RNN_CONV_W):
            d = RNN_CONV_W - 1 - k
            sm_ref[k:k + 1, :] = jnp.sum(dxr * ext_ref[8 - d:8 - d + t, :], axis=0, keepdims=True)
        sm_ref[4:5, :] = jnp.sum(dxr, axis=0, keepdims=True)
        sm_ref[5:6, :] = jnp.sum(dpr, axis=0, keepdims=True)
        sm_ref[6:7, :] = jnp.sum(dpi, axis=0, keepdims=True)
        sm_ref[7:8, :] = dls * _sigmoid(-rvec_ref[3:4, :])

    blk = lambda off: pl.BlockSpec((t, LANES), lambda c: (0, off + c))
    sq = pl.BlockSpec((None, LANES, LANES), lambda c: (c, 0, 0))
    return pl.pallas_call(
        body, name=name, grid=(nblk,),
        in_specs=[blk(2 * nblk), blk(0), blk(0),
                  pl.BlockSpec((8, LANES), lambda c: (0, c)),
                  pl.BlockSpec((8, LANES), lambda c: (0, c)), sq, sq],
        out_specs=[blk(0), pl.BlockSpec((8, LANES), lambda c: (0, c)), sq, sq],
        out_shape=[jax.ShapeDtypeStruct((t, nblk * LANES), BF16), jax.ShapeDtypeStruct((8, nblk * LANES), F32),
                   jax.ShapeDtypeStruct((nblk, LANES, LANES), F32), jax.ShapeDtypeStruct((nblk, LANES, LANES), F32)],
        scratch_shapes=[pltpu.VMEM((t + 8, LANES), F32), pltpu.VMEM((t + 8, LANES), F32)],
        compiler_params=_cp(("arbitrary",), 60),
    )(proj, hr, dhr, rw8, rvec, wab, wib)


def _mixin_bwd(dxo, x, parts, vec, win, tm, name):
    t, d = x.shape
    nb, _, cb = win.shape
    dc = parts[0].shape[1]
    per = dc // cb
    nt = t // tm

    def body(dxo_ref, x_ref, p0, p1, p2, p3, vec_ref, w_ref, dx_ref, dvec_ref):
        i = pl.program_id(0)

        @pl.when(i == 0)
        def _():
            dvec_ref[...] = jnp.zeros_like(dvec_ref)

        prefs = (p0, p1, p2, p3)
        dh = jnp.zeros((tm, d), F32)
        for k in range(nb):
            dh = dh + _dot_nt(prefs[k // per][:, (k % per) * cb:(k % per + 1) * cb], w_ref[k])
        dx_ref[...] = dxo_ref[...] + _adaln_bwd(x_ref[...], dh, vec_ref, dvec_ref)

        @pl.when(i == nt - 1)
        def _():
            _adaln_finish(vec_ref, dvec_ref)

    tile_d = pl.BlockSpec((tm, d), lambda i: (i, 0))
    tile_c = pl.BlockSpec((tm, dc), lambda i: (i, 0))
    return pl.pallas_call(
        body, name=name, grid=(nt,),
        in_specs=[tile_d, tile_d, tile_c, tile_c, tile_c, tile_c,
                  pl.BlockSpec((8, d), lambda i: (0, 0)),
                  pl.BlockSpec((nb, d, cb), lambda i: (0, 0, 0))],
        out_specs=[tile_d, pl.BlockSpec((8, d), lambda i: (0, 0))],
        out_shape=[jax.ShapeDtypeStruct((t, d), F32), jax.ShapeDtypeStruct((8, d), F32)],
        compiler_params=_cp(("arbitrary",), 48),
    )(dxo, x, *parts, vec, win)


def _coords():
    return lax.axis_index("x"), lax.axis_index("y"), lax.axis_index("c")


def _flip(v, bit):
    return 1 - v if bit else v


def _gather_copy(outs, send_sems, recv_sems, a, k, block, to, src=None):
    dst = outs[a].at[block]
    return pltpu.make_async_remote_copy(
        src_ref=dst if src is None else src, dst_ref=dst,
        send_sem=send_sems.at[a, k], recv_sem=recv_sems.at[a, k],
        device_id=to, device_id_type=MESH_IDS)


def _gather_start(ins, outs, send_sems, recv_sems, loc_sems):
    x, y, c = _coords()
    me = 4 * x + 2 * y + c
    for a in range(len(ins)):
        pltpu.make_async_copy(ins[a], outs[a].at[me], loc_sems.at[a]).start()
    for a in range(len(ins)):
        _gather_copy(outs, send_sems, recv_sems, a, 0, me, (x, y, 1 - c), src=ins[a]).start()
        for j, (cx, cy) in enumerate([(1 - x, y), (x, 1 - y), (1 - x, 1 - y)]):
            _gather_copy(outs, send_sems, recv_sems, a, 1 + j, me, (cx, cy, c), src=ins[a]).start()


def _gather_finish(ins, outs, send_sems, recv_sems, loc_sems):
    x, y, c = _coords()
    me = 4 * x + 2 * y + c
    sib = (x, y, 1 - c)
    chips = [(1 - x, y), (x, 1 - y), (1 - x, 1 - y)]
    n = len(ins)
    for a in range(n):
        for j, (cx, cy) in enumerate(chips):
            blk = 4 * cx + 2 * cy + c
            _gather_copy(outs, send_sems, recv_sems, a, 1 + j, blk, sib).wait_recv()
            _gather_copy(outs, send_sems, recv_sems, a, 4 + j, blk, sib).start()
    for a in range(n):
        _gather_copy(outs, send_sems, recv_sems, a, 0, 4 * x + 2 * y + (1 - c), sib).wait_recv()
        for j, (cx, cy) in enumerate(chips):
            _gather_copy(outs, send_sems, recv_sems, a, 4 + j, 4 * cx + 2 * cy + (1 - c), sib).wait_recv()
    for a in range(n):
        _gather_copy(outs, send_sems, recv_sems, a, 0, me, sib, src=ins[a]).wait_send()
        for j, (cx, cy) in enumerate(chips):
            _gather_copy(outs, send_sems, recv_sems, a, 1 + j, me, (cx, cy, c), src=ins[a]).wait_send()
            _gather_copy(outs, send_sems, recv_sems, a, 4 + j, 4 * cx + 2 * cy + c, sib).wait_send()
        pltpu.make_async_copy(ins[a], outs[a].at[me], loc_sems.at[a]).wait()


def _gather_shapes(shards):
    return [jax.ShapeDtypeStruct((NDEV,) + s.shape, s.dtype) for s in shards]


def _gather_sems(n):
    return [pltpu.SemaphoreType.DMA((n, 7)), pltpu.SemaphoreType.DMA((n, 7)), pltpu.SemaphoreType.DMA((n,))]


def _all_gather_weights(shards, name):
    n = len(shards)

    def body(*refs):
        ins, outs, sems = refs[:n], refs[n:2 * n], refs[2 * n:]
        _gather_start(ins, outs, *sems)
        _gather_finish(ins, outs, *sems)

    anyspec = pl.BlockSpec(memory_space=pl.ANY)
    return pl.pallas_call(
        body, name=name,
        in_specs=[anyspec] * n, out_specs=[anyspec] * n,
        out_shape=_gather_shapes(shards), scratch_shapes=_gather_sems(n),
    )(*shards)


def _sibling_copies(ins, outs, send_sems, recv_sems):
    x, y, c = _coords()
    return [pltpu.make_async_remote_copy(
        src_ref=ins[a].at[2 * q + (1 - c)], dst_ref=outs[a].at[q],
        send_sem=send_sems.at[a, q], recv_sem=recv_sems.at[a, q],
        device_id=(x, y, 1 - c), device_id_type=MESH_IDS) for a in range(len(ins)) for q in range(4)]


def _sibling_shapes(parts):
    return [jax.ShapeDtypeStruct((4,) + p.shape[1:], p.dtype) for p in parts]


def _rs_sibling(parts, name):
    n = len(parts)

    def body(*refs):
        copies = _sibling_copies(refs[:n], refs[n:2 * n], *refs[2 * n:])
        for cpy in copies:
            cpy.start()
        for cpy in copies:
            cpy.wait()

    anyspec = pl.BlockSpec(memory_space=pl.ANY)
    return pl.pallas_call(
        body, name=name,
        in_specs=[anyspec] * n, out_specs=[anyspec] * n,
        out_shape=_sibling_shapes(parts),
        scratch_shapes=[pltpu.SemaphoreType.DMA((n, 4)), pltpu.SemaphoreType.DMA((n, 4))],
    )(*parts)


def _chips_copies(ins, outs, send_sems, recv_sems):
    x, y, c = _coords()
    copies = []
    for a in range(len(ins)):
        for k, (kx, ky) in enumerate([(1, 0), (0, 1), (1, 1)]):
            tx, ty = _flip(x, kx), _flip(y, ky)
            copies.append(pltpu.make_async_remote_copy(
                src_ref=ins[a].at[2 * tx + ty], dst_ref=outs[a].at[k],
                send_sem=send_sems.at[a, k], recv_sem=recv_sems.at[a, k],
                device_id=(tx, ty, c), device_id_type=MESH_IDS))
    return copies


def _chips_shapes(sums):
    return [jax.ShapeDtypeStruct((3,) + s.shape[1:], s.dtype) for s in sums]


def _rs_chips(sums, name):
    n = len(sums)

    def body(*refs):
        copies = _chips_copies(refs[:n], refs[n:2 * n], *refs[2 * n:])
        for cpy in copies:
            cpy.start()
        for cpy in copies:
            cpy.wait()

    anyspec = pl.BlockSpec(memory_space=pl.ANY)
    return pl.pallas_call(
        body, name=name,
        in_specs=[anyspec] * n, out_specs=[anyspec] * n,
        out_shape=_chips_shapes(sums),
        scratch_shapes=[pltpu.SemaphoreType.DMA((n, 3)), pltpu.SemaphoreType.DMA((n, 3))],
    )(*sums)


def _chip_sum(part, recv, sel, tr, name):
    _, _, r, c = part.shape

    def body(sel_ref, p_ref, r_ref, cs_ref, own_ref):
        q = pl.program_id(1)
        s = p_ref[...].astype(F32) + r_ref[...].astype(F32)
        cs_ref[...] = s.astype(BF16)

        @pl.when(q == sel_ref[1])
        def _():
            own_ref[...] = s

    return pl.pallas_call(
        body, name=name,
        grid_spec=pltpu.PrefetchScalarGridSpec(
            num_scalar_prefetch=1, grid=(r // tr, 4),
            in_specs=[pl.BlockSpec((None, None, tr, c), lambda i, q, s: (q, s[0], i, 0)),
                      pl.BlockSpec((None, tr, c), lambda i, q, s: (q, i, 0))],
            out_specs=[pl.BlockSpec((None, tr, c), lambda i, q, s: (q, i, 0)),
                       pl.BlockSpec((tr, c), lambda i, q, s: (i, 0))]),
        out_shape=[jax.ShapeDtypeStruct((4, r, c), BF16), jax.ShapeDtypeStruct((r, c), F32)],
        compiler_params=_cp(("arbitrary", "arbitrary"), 48),
    )(sel, part, recv)


def _gather_direct(src_ref, buf_ref, send_sems, recv_sems):
    x, y, c = _coords()
    me = 4 * x + 2 * y + c
    buf_ref[me] = src_ref[...]
    copies = []
    for k in range(1, NDEV):
        kx, ky, kc = (k >> 2) & 1, (k >> 1) & 1, k & 1
        copies.append(pltpu.make_async_remote_copy(
            src_ref=src_ref, dst_ref=buf_ref.at[me],
            send_sem=send_sems.at[k - 1], recv_sem=recv_sems.at[k - 1],
            device_id=(_flip(x, kx), _flip(y, ky), _flip(c, kc)), device_id_type=MESH_IDS))
    for cpy in copies:
        cpy.start()
    for k in range(1, NDEV):
        kx, ky, kc = (k >> 2) & 1, (k >> 1) & 1, k & 1
        peer = 4 * _flip(x, kx) + 2 * _flip(y, ky) + _flip(c, kc)
        pltpu.make_async_remote_copy(
            src_ref=src_ref, dst_ref=buf_ref.at[peer],
            send_sem=send_sems.at[k - 1], recv_sem=recv_sems.at[k - 1],
            device_id=(x, y, c), device_id_type=MESH_IDS).wait_recv()
    for cpy in copies:
        cpy.wait_send()
    return me


def _mod_exchange(c_row, wmod, bmod, wfmod, bfmod, name):
    d = c_row.shape[1]
    nm, nf = wmod.shape[1], wfmod.shape[1]
    nw = nm + nf

    def body(c_ref, wm_ref, bm_ref, wf_ref, bf_ref, cs_ref, mod_ref, fmod_ref,
             slab_ref, csbuf_ref, mslab_ref, mbuf_ref, s1, r1, s2, r2):
        cv = c_ref[...]
        slab_ref[...] = jnp.broadcast_to(cv * _sigmoid(cv), (8, d))
        _gather_direct(slab_ref, csbuf_ref, s1, r1)
        for b in range(NDEV):
            cs_ref[b:b + 1, :] = csbuf_ref[b, 0:1, :]
        cs = cs_ref[...]
        mslab_ref[:, 0:nm] = jnp.dot(cs, wm_ref[...], precision=HI, preferred_element_type=F32) + bm_ref[...]
        mslab_ref[:, nm:] = jnp.dot(cs, wf_ref[...], precision=HI, preferred_element_type=F32) + bf_ref[...]
        me = _gather_direct(mslab_ref, mbuf_ref, s2, r2)
        mine = lax.broadcasted_iota(jnp.int32, (8, nw), 0) == me
        for k in range(NDEV):
            rowk = jnp.sum(jnp.where(mine, mbuf_ref[k], 0.0), axis=0, keepdims=True)
            mod_ref[k:k + 1, :] = rowk[:, 0:nm]
            fmod_ref[k:k + 1, :] = rowk[:, nm:]

    vm = pl.BlockSpec(memory_space=pltpu.VMEM)
    return pl.pallas_call(
        body, name=name,
        in_specs=[vm] * 5, out_specs=[vm] * 3,
        out_shape=[jax.ShapeDtypeStruct((NDEV, d), F32), jax.ShapeDtypeStruct((NDEV, nm), F32),
                   jax.ShapeDtypeStruct((NDEV, nf), F32)],
        scratch_shapes=[pltpu.VMEM((8, d), F32), pltpu.VMEM((NDEV, 8, d), F32),
                        pltpu.VMEM((8, nw), F32), pltpu.VMEM((NDEV, 8, nw), F32),
                        pltpu.SemaphoreType.DMA((7,)), pltpu.SemaphoreType.DMA((7,)),
                        pltpu.SemaphoreType.DMA((7,)), pltpu.SemaphoreType.DMA((7,))],
        compiler_params=pltpu.CompilerParams(vmem_limit_bytes=40 * 1024 * 1024),
    )(c_row, wmod, bmod, wfmod, bfmod)


def _small_exchange(pack, name):
    def body(p_ref, all_ref, sum_ref, s, r):
        _gather_direct(p_ref, all_ref, s, r)
        tot = all_ref[0]
        for k in range(1, NDEV):
            tot = tot + all_ref[k]
        sum_ref[...] = tot

    vm = pl.BlockSpec(memory_space=pltpu.VMEM)
    return pl.pallas_call(
        body, name=name,
        in_specs=[vm], out_specs=[vm, vm],
        out_shape=[jax.ShapeDtypeStruct((NDEV,) + pack.shape, F32), jax.ShapeDtypeStruct(pack.shape, F32)],
        scratch_shapes=[pltpu.SemaphoreType.DMA((7,)), pltpu.SemaphoreType.DMA((7,))],
    )(pack)


def _adamw_math(w, g, m, v):
    m = ADAM_B1 * m + (1.0 - ADAM_B1) * g
    v = ADAM_B2 * v + (1.0 - ADAM_B2) * (g * g)
    m_hat = m / (1.0 - ADAM_B1 ** ADAM_STEP)
    v_hat = v / (1.0 - ADAM_B2 ** ADAM_STEP)
    delta = -ADAM_LR * (m_hat / (jnp.sqrt(v_hat) + ADAM_EPS) + ADAM_WD * w)
    return delta, m, v


def _adamw_small(w, g, m, v, name):
    def body(w_ref, g_ref, m_ref, v_ref, d_ref, mo_ref, vo_ref):
        d_ref[...], mo_ref[...], vo_ref[...] = _adamw_math(w_ref[...], g_ref[...], m_ref[...], v_ref[...])

    vm = pl.BlockSpec(memory_space=pltpu.VMEM)
    sds = jax.ShapeDtypeStruct(w.shape, F32)
    return pl.pallas_call(body, name=name, in_specs=[vm] * 4, out_specs=[vm] * 3,
                          out_shape=[sds, sds, sds])(w, g, m, v)


def _rs_final(own, recv, w, m, v, tr, name):
    r, c = own.shape

    def body(o_ref, r_ref, w_ref, m_ref, v_ref, g_ref, d_ref, mo_ref, vo_ref):
        g = o_ref[...] + r_ref[0].astype(F32) + r_ref[1].astype(F32) + r_ref[2].astype(F32)
        g_ref[...] = g
        d_ref[...], mo_ref[...], vo_ref[...] = _adamw_math(w_ref[...], g, m_ref[...], v_ref[...])

    tile = pl.BlockSpec((tr, c), lambda i: (i, 0))
    sds = jax.ShapeDtypeStruct((r, c), F32)
    return pl.pallas_call(
        body, name=name, grid=(r // tr,),
        in_specs=[tile, pl.BlockSpec((3, tr, c), lambda i: (0, i, 0)), tile, tile, tile],
        out_specs=[tile] * 4, out_shape=[sds] * 4,
        compiler_params=_cp(("arbitrary",), 48),
    )(own, recv, w, m, v)


def _mod_weight_update(cs, dm, w, m, v, tr, name):
    r, c = w.shape

    def body(cs_ref, dm_ref, w_ref, m_ref, v_ref, g_ref, d_ref, mo_ref, vo_ref):
        g = lax.dot_general(cs_ref[...], dm_ref[...], (((0,), (0,)), ((), ())),
                            precision=HI, preferred_element_type=F32)
        g_ref[...] = g
        d_ref[...], mo_ref[...], vo_ref[...] = _adamw_math(w_ref[...], g, m_ref[...], v_ref[...])

    tile = pl.BlockSpec((tr, c), lambda i: (i, 0))
    sds = jax.ShapeDtypeStruct((r, c), F32)
    return pl.pallas_call(
        body, name=name, grid=(r // tr,),
        in_specs=[pl.BlockSpec((NDEV, tr), lambda i: (0, i)), pl.BlockSpec((NDEV, c), lambda i: (0, 0)),
                  tile, tile, tile],
        out_specs=[tile] * 4, out_shape=[sds] * 4,
        compiler_params=_cp(("arbitrary",), 48),
    )(cs, dm, w, m, v)


def _rows(*vs):
    d = vs[0].shape[-1]
    rows = [v.reshape(1, d) for v in vs]
    return jnp.concatenate(rows + [jnp.zeros((8 - len(rows), d), F32)], axis=0)


def _block_diag_pairs(w):
    hd = w.shape[-1]
    z = jnp.zeros((w.shape[0] // 2, hd, hd), w.dtype)
    top = jnp.concatenate([w[0::2], z], axis=2)
    bot = jnp.concatenate([z, w[1::2]], axis=2)
    return jnp.concatenate([top, bot], axis=1).astype(BF16)


def _diag_pairs(g):
    hd = g.shape[-1] // 2
    both = jnp.stack([g[:, :hd, :hd], g[:, hd:, hd:]], axis=1)
    return both.reshape(2 * g.shape[0], hd, hd)


def kernel(x, c, w_mod, b_mod, g_ffn1, w_ffn1_in, w_ffn1_out, g_mix, w_in, conv_w, conv_b, ln_g, ln_b, rnn_conv_w, rnn_conv_b, w_a, b_a, w_i, b_i, lru_lambda, w_out, g_ffn2, w_ffn2_in, w_ffn2_out, w_fmod, b_fmod, g_final, loss_target, m_w_mod, m_b_mod, m_g_ffn1, m_w_ffn1_in, m_w_ffn1_out, m_g_mix, m_w_in, m_conv_w, m_conv_b, m_ln_g, m_ln_b, m_rnn_conv_w, m_rnn_conv_b, m_w_a, m_b_a, m_w_i, m_b_i, m_lru_lambda, m_w_out, m_g_ffn2, m_w_ffn2_in, m_w_ffn2_out, m_w_fmod, m_b_fmod, m_g_final, v_w_mod, v_b_mod, v_g_ffn1, v_w_ffn1_in, v_w_ffn1_out, v_g_mix, v_w_in, v_conv_w, v_conv_b, v_ln_g, v_ln_b, v_rnn_conv_w, v_rnn_conv_b, v_w_a, v_b_a, v_w_i, v_b_i, v_lru_lambda, v_w_out, v_g_ffn2, v_w_ffn2_in, v_w_ffn2_out, v_w_fmod, v_b_fmod, v_g_final):
    t, d = x.shape[1], x.shape[2]
    fb = w_ffn1_in.shape[2]
    nm = w_mod.shape[2]
    nf = w_fmod.shape[1]
    dc = conv_b.shape[1]
    cl = conv_w.shape[2]
    tm = min(TOKEN_TILE, t)
    tk = min(WGRAD_TILE, t)
    nk = t // tk
    me = 4 * lax.axis_index("x") + 2 * lax.axis_index("y") + lax.axis_index("c")

    bmod_l = lax.dynamic_slice(b_mod, (0, me * nm), (1, nm))
    bfmod_l = lax.dynamic_slice(b_fmod.reshape(1, -1), (0, me * nf), (1, nf))
    cs, mod_rows, fmod_rows = _mod_exchange(c, w_mod[0], bmod_l, w_fmod, bfmod_l, "mod_exchange")
    mod = mod_rows.reshape(9, d)
    fmod = fmod_rows.reshape(2, d)
    vec1 = _rows(g_ffn1, mod[0], mod[1], mod[2])
    vecm = _rows(g_mix, mod[3], mod[4], mod[5])
    vec3 = _rows(g_ffn2, mod[6], mod[7], mod[8])
    vecf = _rows(g_final, fmod[0], fmod[1])

    tr = jnp.transpose
    cwl = jnp.concatenate([conv_w[0], jnp.zeros((1, cl), F32), rnn_conv_w[0], jnp.zeros((4, cl), F32)], axis=0)
    wi1, wo1, cwg = _all_gather_weights(
        [tr(w_ffn1_in[0]).astype(BF16), w_ffn1_out[0].astype(BF16), cwl], "gather_ffn1")
    wi1 = wi1.reshape(2, 4, fb, d)
    wo1 = wo1.reshape(4 * fb, d)

    xin = x[0]
    later = [w_in[0].astype(BF16), w_out[0].astype(BF16), tr(w_ffn2_in[0]).astype(BF16), w_ffn2_out[0].astype(BF16)]
    x1, h1, gu1, f1, win, wout, wi2, wo2 = _ffn_fwd(xin, vec1, wi1, wo1, tm, "ffn1_fwd", gather=later)
    wi2 = wi2.reshape(2, 4, fb, d)
    wo2 = wo2.reshape(4 * fb, d)
    wout = wout.reshape(d, d)
    h2, proj = _mix_in(x1, vecm, win, tm, "mix_in")
    lnv = _rows(ln_g, ln_b)
    rvec = _rows(rnn_conv_b, b_a, b_i, lru_lambda)
    wab = _block_diag_pairs(w_a[0])
    wib = _block_diag_pairs(w_i[0])
    cwf = jnp.transpose(cwg, (1, 0, 2)).reshape(40, NDEV * cl)
    cw32 = jnp.concatenate([cwf[0:CONV_W], conv_b], axis=0)
    rw8 = cwf[32:40]

    cv = _conv_fwd(proj, cw32, "conv_fwd")
    hr, yr = _rnn_fwd(proj, rw8, rvec, wab, wib, "rnn_fwd")
    x2, ym, ycat = _mix_out(x1, cv, yr, vecm, lnv, wout, tm, "mix_out")
    x3, h3, gu3, f3 = _ffn_fwd(x2, vec3, wi2, wo2, tm, "ffn2_fwd")

    dx3, dvf = _final(x3, loss_target[0], vecf, tm, "final_loss")
    a_tok = lambda width: pl.BlockSpec((tk, width), lambda k, s: (s, 0))
    blk3 = lambda width: pl.BlockSpec((None, tk, width), lambda k, s: (k, s, 0))
    sel = jnp.stack([lax.axis_index("c"), 2 * lax.axis_index("x") + lax.axis_index("y")]).astype(jnp.int32)
    row_tile = {"w_ffn1_in": fb // 4, "w_ffn1_out": fb // 4, "w_in": 512, "w_out": 128,
                "w_ffn2_in": fb // 4, "w_ffn2_out": fb // 4}

    def chip_sums(names, partials, from_sib):
        out = [_chip_sum(p.reshape((4, 2) + p.shape[1:]), r, sel, row_tile[nm_], "chip_sum_" + nm_)
               for nm_, p, r in zip(names, partials, from_sib)]
        return [o[0] for o in out], [o[1] for o in out]

    dx2, dgu3, act3, df3, dv3 = _ffn_bwd(dx3, x2, vec3, gu3, f3, wi2, wo2, tm, "ffn2_bwd")
    p_wi2 = _mm_tn(dgu3.reshape(8, t, fb), h3, blk3(fb), a_tok(d), 8, nk, fb, d, "wgrad_ffn2_in")
    p_wo2 = _mm_tn(act3, df3, blk3(fb), a_tok(d), 4, nk, fb, d, "wgrad_ffn2_out").reshape(NDEV, fb // 2, d)
    names2 = ["w_ffn2_in", "w_ffn2_out"]
    dym, dcv, dhr, duy, dln, dgt2, sib_wi2, sib_wo2 = _mixout_bwd(
        dx2, ym, cv, hr, proj, vecm, lnv, wout, tm, "mixout_bwd", sibling=[p_wi2, p_wo2])
    sums2, owns2 = chip_sums(names2, [p_wi2, p_wo2], [sib_wi2, sib_wo2])
    dval, dgate, dcw = _conv_bwd(proj, dcv, cw32, "conv_bwd")
    dux, rsm, dwab, dwib = _rnn_bwd(proj, hr, dhr, rw8, rvec, wab, wib, "rnn_bwd")
    parts = [dval, dgate, dux, duy]
    dx1, dvm = _mixin_bwd(dx2, x1, parts, vecm, win, tm, "mixin_bwd")
    p_wout = _mm_tn(ycat, dym, pl.BlockSpec((tk, LANES), lambda k, s: (s, k)), a_tok(d),
                    NDEV, nk, LANES, d, "wgrad_out")
    cbw = win.shape[2]
    p_win = jnp.concatenate(
        [_mm_tn(h2, p, a_tok(d), pl.BlockSpec((tk, cbw), lambda k, s: (s, k)), dc // cbw, nk, d, cbw,
                "wgrad_in_%d" % n) for n, p in enumerate(parts)], axis=0)
    namesm = ["w_in", "w_out"]
    sumsm, ownsm = chip_sums(namesm, [p_win, p_wout], _rs_sibling([p_win, p_wout], "rs_sibling_mix"))
    dx0, dgu1, act1, df1, dv1, *from_chips_a = _ffn_bwd(
        dx1, xin, vec1, gu1, f1, wi1, wo1, tm, "ffn1_bwd", scatter=sumsm + sums2)
    p_wi1 = _mm_tn(dgu1.reshape(8, t, fb), h1, blk3(fb), a_tok(d), 8, nk, fb, d, "wgrad_ffn1_in")
    p_wo1 = _mm_tn(act1, df1, blk3(fb), a_tok(d), 4, nk, fb, d, "wgrad_ffn1_out").reshape(NDEV, fb // 2, d)
    names1 = ["w_ffn1_in", "w_ffn1_out"]
    sums1, owns1 = chip_sums(names1, [p_wi1, p_wo1], _rs_sibling([p_wi1, p_wo1], "rs_sibling_ffn1"))
    from_chips_b = _rs_chips(sums1, "rs_chips_ffn1")

    big = {"w_ffn1_in": (tr(w_ffn1_in[0]), tr(m_w_ffn1_in[0]), tr(v_w_ffn1_in[0])),
           "w_ffn1_out": (w_ffn1_out[0], m_w_ffn1_out[0], v_w_ffn1_out[0]),
           "w_in": (w_in[0], m_w_in[0], v_w_in[0]), "w_out": (w_out[0], m_w_out[0], v_w_out[0]),
           "w_ffn2_in": (tr(w_ffn2_in[0]), tr(m_w_ffn2_in[0]), tr(v_w_ffn2_in[0])),
           "w_ffn2_out": (w_ffn2_out[0], m_w_ffn2_out[0], v_w_ffn2_out[0])}
    res = {}
    for nm_, own, rc in zip(namesm + names2 + names1, ownsm + owns2 + owns1, list(from_chips_a) + list(from_chips_b)):
        out4 = _rs_final(own, rc, *big[nm_], row_tile[nm_], "rs_final_" + nm_)
        if nm_ in ("w_ffn1_in", "w_ffn2_in"):
            out4 = [tr(o) for o in out4]
        res[nm_] = [o[None] for o in out4]

    dmod_row = jnp.concatenate([dv1[1:2], dv1[3:4], dv1[2:3], dvm[1:2], dvm[3:4], dgt2[0:1],
                                dv3[1:2], dv3[3:4], dv3[2:3]], axis=0)
    pack = jnp.concatenate([
        dmod_row, dvf[1:2], dvf[3:4],
        dv1[0:1], dvm[0:1], dv3[0:1], dvf[0:1],
        dcw.reshape(16, d),
        jnp.concatenate([dln[0:1], dln[1:2]], axis=1),
        rsm.reshape(4, d),
        _diag_pairs(dwab).reshape(32, d), _diag_pairs(dwib).reshape(32, d),
        dvf[4:5],
        jnp.zeros((SMALL_ROWS - 101, d), F32)], axis=0)
    allp, tot = _small_exchange(pack, "small_exchange")
    loss = jnp.sum(tot[100])

    dm_all = allp[:, 0:9, :].reshape(NDEV, 9 * d)
    dfm_all = allp[:, 9:11, :].reshape(NDEV, 2 * d)
    dm_l = lax.dynamic_slice(dm_all, (0, me * nm), (NDEV, nm))
    dfm_l = lax.dynamic_slice(dfm_all, (0, me * nf), (NDEV, nf))
    out_wmod = [o[None] for o in _mod_weight_update(cs, dm_l, w_mod[0], m_w_mod[0], v_w_mod[0], 256, "w_mod_update")]
    out_wfmod = _mod_weight_update(cs, dfm_l, w_fmod, m_w_fmod, v_w_fmod, 256, "w_fmod_update")
    res["w_mod"] = out_wmod
    res["w_fmod"] = list(out_wfmod)

    dcw_f = tot[15:31].reshape(32, dc)
    rsm_f = tot[32:36].reshape(8, dc)
    small_grads = {
        "b_mod": tot[0:9].reshape(1, 9 * d),
        "b_fmod": tot[9:11].reshape(2 * d),
        "g_ffn1": tot[11:12], "g_mix": tot[12:13], "g_ffn2": tot[13:14], "g_final": tot[14],
        "conv_w": lax.dynamic_slice(dcw_f, (0, me * cl), (CONV_W, cl))[None],
        "conv_b": dcw_f[31:32],
        "ln_g": tot[31:32, 0:dc], "ln_b": tot[31:32, dc:],
        "rnn_conv_w": lax.dynamic_slice(rsm_f, (0, me * cl), (RNN_CONV_W, cl))[None],
        "rnn_conv_b": rsm_f[4:5], "b_a": rsm_f[5:6], "b_i": rsm_f[6:7], "lru_lambda": rsm_f[7:8],
        "w_a": tot[36:68].reshape(w_a.shape), "w_i": tot[68:100].reshape(w_i.shape),
    }
    small_params = {
        "b_mod": (b_mod, m_b_mod, v_b_mod), "b_fmod": (b_fmod, m_b_fmod, v_b_fmod),
        "g_ffn1": (g_ffn1, m_g_ffn1, v_g_ffn1), "g_mix": (g_mix, m_g_mix, v_g_mix),
        "g_ffn2": (g_ffn2, m_g_ffn2, v_g_ffn2), "g_final": (g_final, m_g_final, v_g_final),
        "conv_w": (conv_w, m_conv_w, v_conv_w), "conv_b": (conv_b, m_conv_b, v_conv_b),
        "ln_g": (ln_g, m_ln_g, v_ln_g), "ln_b": (ln_b, m_ln_b, v_ln_b),
        "rnn_conv_w": (rnn_conv_w, m_rnn_conv_w, v_rnn_conv_w),
        "rnn_conv_b": (rnn_conv_b, m_rnn_conv_b, v_rnn_conv_b),
        "w_a": (w_a, m_w_a, v_w_a), "b_a": (b_a, m_b_a, v_b_a),
        "w_i": (w_i, m_w_i, v_w_i), "b_i": (b_i, m_b_i, v_b_i),
        "lru_lambda": (lru_lambda, m_lru_lambda, v_lru_lambda),
    }
    for name, g in small_grads.items():
        w, m, v = small_params[name]
        shp = w.shape
        two_d = (-1, shp[-1]) if w.ndim > 1 else (1, shp[0])
        outs = _adamw_small(w.reshape(two_d), g.reshape(two_d), m.reshape(two_d), v.reshape(two_d),
                            "adamw_" + name)
        res[name] = [g.reshape(shp)] + [o.reshape(shp) for o in outs]

    order = ["w_mod", "b_mod", "g_ffn1", "w_ffn1_in", "w_ffn1_out", "g_mix", "w_in", "conv_w", "conv_b",
             "ln_g", "ln_b", "rnn_conv_w", "rnn_conv_b", "w_a", "b_a", "w_i", "b_i", "lru_lambda", "w_out",
             "g_ffn2", "w_ffn2_in", "w_ffn2_out", "w_fmod", "b_fmod", "g_final"]
    return (loss, dx0[None], *[res[n][0] for n in order], *[res[n][1] for n in order],
            *[res[n][2] for n in order], *[res[n][3] for n in order])
```

```python
import functools
import math

import jax
import jax.numpy as jnp
from jax import lax
from jax.experimental import pallas as pl
from jax.experimental.pallas import tpu as pltpu

F32 = jnp.float32
BF16 = jnp.bfloat16
MESH_IDS = pl.DeviceIdType.MESH
NDEV = 8
EPS = 1e-6
RG_C = 8.0
CONV_W = 31
RNN_CONV_W = 4
LANES = 128
ADAM_LR = 0.001
ADAM_B1 = 0.9
ADAM_B2 = 0.999
ADAM_EPS = 1e-08
ADAM_WD = 0.01
ADAM_STEP = 10
SMALL_ROWS = 104
TOKEN_TILE = 512
WGRAD_TILE = 2048
HI = lax.Precision.HIGHEST


def _cp(sem, vmem_mb):
    return pltpu.CompilerParams(dimension_semantics=sem, vmem_limit_bytes=vmem_mb * 1024 * 1024)


def _dot(a, b):
    return jnp.dot(a, b, preferred_element_type=F32)


def _dot_nt(a, b):
    return lax.dot_general(a, b, (((1,), (1,)), ((), ())), preferred_element_type=F32)


def _dot_tn(a, b):
    return lax.dot_general(a, b, (((0,), (0,)), ((), ())), preferred_element_type=F32)


def _sigmoid(x):
    return 1.0 / (1.0 + jnp.exp(-x))


def _adaln(x, vec_ref):
    rstd = lax.rsqrt(jnp.mean(x * x, axis=-1, keepdims=True) + EPS)
    return (x * rstd) * vec_ref[0:1, :] * (1.0 + vec_ref[2:3, :]) + vec_ref[1:2, :]


def _adaln_bwd(x, dh, vec_ref, dvec_ref):
    rstd = lax.rsqrt(jnp.mean(x * x, axis=-1, keepdims=True) + EPS)
    xhat = x * rstd
    dvec_ref[0:1, :] += jnp.sum(dh * xhat, axis=0, keepdims=True)
    dvec_ref[1:2, :] += jnp.sum(dh, axis=0, keepdims=True)
    dxhat = dh * (vec_ref[0:1, :] * (1.0 + vec_ref[2:3, :]))
    return rstd * (dxhat - xhat * jnp.mean(dxhat * xhat, axis=-1, keepdims=True))


def _adaln_finish(vec_ref, dvec_ref):
    s = dvec_ref[0:1, :]
    dvec_ref[3:4, :] = vec_ref[0:1, :] * s
    dvec_ref[0:1, :] = (1.0 + vec_ref[2:3, :]) * s


def _gelu_and_grad(x):
    k0 = math.sqrt(2.0 / math.pi)
    x2 = x * x
    t = jnp.tanh(k0 * (x + 0.044715 * x * x2))
    g = 0.5 * x * (1.0 + t)
    dg = 0.5 * (1.0 + t) + 0.5 * x * (1.0 - t * t) * (k0 * (1.0 + 3.0 * 0.044715 * x2))
    return g, dg


def _log_sigmoid(x):
    z = jnp.exp(-jnp.abs(x))
    u = 1.0 + z
    d = u - 1.0
    log1p = jnp.where(d == 0.0, z, jnp.log(u) * (z / jnp.where(d == 0.0, 1.0, d)))
    return jnp.minimum(x, 0.0) - log1p


def _neg_expm1(x):
    series = -x * (1.0 + x * (0.5 + x * (1.0 / 6.0 + x * (1.0 / 24.0 + x * (1.0 / 120.0)))))
    return jnp.where(x > -0.05, series, 1.0 - jnp.exp(x))


def _scan_fwd(a, b):
    n = a.shape[0]
    row = lax.broadcasted_iota(jnp.int32, a.shape, 0)
    s = 1
    while s < n:
        ok = row >= s
        b = a * jnp.where(ok, pltpu.roll(b, s, 0), 0.0) + b
        if 2 * s < n:
            a = a * jnp.where(ok, pltpu.roll(a, s, 0), 1.0)
        s *= 2
    return b


def _scan_rev(a, d):
    n = a.shape[0]
    row = lax.broadcasted_iota(jnp.int32, a.shape, 0)
    s = 1
    while s < n:
        ok = row < n - s
        d = a * jnp.where(ok, pltpu.roll(d, n - s, 0), 0.0) + d
        if 2 * s < n:
            a = a * jnp.where(ok, pltpu.roll(a, n - s, 0), 1.0)
        s *= 2
    return d


def _rglru_gates(xr, wa_ref, wi_ref, rvec_ref):
    xb = xr.astype(BF16)
    r = _sigmoid(_dot(xb, wa_ref[...]) + rvec_ref[1:2, :])
    ig = _sigmoid(_dot(xb, wi_ref[...]) + rvec_ref[2:3, :])
    ls = _log_sigmoid(rvec_ref[3:4, :])
    log_a = RG_C * r * ls
    a = jnp.exp(log_a)
    mult = jnp.sqrt(_neg_expm1(2.0 * log_a))
    return xb, r, ig, ls, a, mult


def _rnn_conv(ux, rw_ref, rvec_ref, ext_ref):
    t = ux.shape[0]
    ext_ref[0:8, :] = jnp.zeros((8, ux.shape[1]), F32)
    ext_ref[8:, :] = ux
    xr = rvec_ref[0:1, :] + rw_ref[RNN_CONV_W - 1:RNN_CONV_W, :] * ux
    for k in range(RNN_CONV_W - 1):
        d = RNN_CONV_W - 1 - k
        xr = xr + rw_ref[k:k + 1, :] * ext_ref[8 - d:8 - d + t, :]
    return xr


def _ffn_fwd(x, vec, wi, wo, tm, name, host=None):
    t, d = x.shape
    nj, fb = wi.shape[1], wi.shape[2]
    nt = t // tm

    def body(ins, outs, scr):
        x_ref, vec_ref, wi_ref, wo_ref = ins
        xo_ref, h_ref, gu_ref, f_ref = outs
        acc_ref, = scr
        j = pl.program_id(1)

        @pl.when(j == 0)
        def _():
            h_ref[...] = _adaln(x_ref[...], vec_ref).astype(BF16)
            acc_ref[...] = jnp.zeros_like(acc_ref)

        h = h_ref[...]
        gate = _dot_nt(h, wi_ref[0])
        up = _dot_nt(h, wi_ref[1])
        gu_ref[0] = gate.astype(BF16)
        gu_ref[1] = up.astype(BF16)
        act = (gate * _sigmoid(gate) * up).astype(BF16)
        acc_ref[...] += _dot(act, wo_ref[...])

        @pl.when(j == nj - 1)
        def _():
            f = acc_ref[...]
            f_ref[...] = f.astype(BF16)
            xo_ref[...] = x_ref[...] + 0.5 * vec_ref[3:4, :] * f

    tile = pl.BlockSpec((tm, d), lambda i, j: (i, 0))
    return _hosted_call(
        body, host, name, (nt, nj),
        [tile,
         pl.BlockSpec((8, d), lambda i, j: (0, 0)),
         pl.BlockSpec((2, None, fb, d), lambda i, j: (0, j, 0, 0)),
         pl.BlockSpec((fb, d), lambda i, j: (j, 0))],
        [tile, tile, pl.BlockSpec((2, None, tm, fb), lambda i, j: (0, j, i, 0)), tile],
        [jax.ShapeDtypeStruct((t, d), F32), jax.ShapeDtypeStruct((t, d), BF16),
         jax.ShapeDtypeStruct((2, nj, t, fb), BF16), jax.ShapeDtypeStruct((t, d), BF16)],
        [pltpu.VMEM((tm, d), F32)], ("arbitrary", "arbitrary"), 48, (x, vec, wi, wo))


def _mix_in(x, vec, win, tm, name):
    t, d = x.shape
    nb, _, cb = win.shape

    def body(x_ref, vec_ref, w_ref, h_ref, p_ref):
        h = _adaln(x_ref[...], vec_ref).astype(BF16)
        h_ref[...] = h
        for k in range(nb):
            p_ref[:, k * cb:(k + 1) * cb] = _dot(h, w_ref[k])

    return pl.pallas_call(
        body, name=name, grid=(t // tm,),
        in_specs=[pl.BlockSpec((tm, d), lambda i: (i, 0)),
                  pl.BlockSpec((8, d), lambda i: (0, 0)),
                  pl.BlockSpec((nb, d, cb), lambda i: (0, 0, 0))],
        out_specs=[pl.BlockSpec((tm, d), lambda i: (i, 0)),
                   pl.BlockSpec((tm, nb * cb), lambda i: (i, 0))],
        out_shape=[jax.ShapeDtypeStruct((t, d), BF16), jax.ShapeDtypeStruct((t, nb * cb), F32)],
        compiler_params=_cp(("arbitrary",), 48),
    )(x, vec, win)


def _conv_fwd(proj, cw32, name):
    t = proj.shape[0]
    nblk = cw32.shape[1] // LANES
    ch = min(t, 128)

    def body(val_ref, gate_ref, cw_ref, cv_ref, ext_ref):
        ext_ref[0:32, :] = jnp.zeros((32, LANES), F32)
        ext_ref[32:, :] = val_ref[...] * _sigmoid(gate_ref[...])
        for r in range(t // ch):
            acc = jnp.broadcast_to(cw_ref[31:32, :], (ch, LANES))
            for k in range(CONV_W):
                off = 32 + r * ch - (CONV_W - 1 - k)
                acc = acc + cw_ref[k:k + 1, :] * ext_ref[off:off + ch, :]
            cv_ref[r * ch:(r + 1) * ch, :] = acc

    return pl.pallas_call(
        body, name=name, grid=(nblk,),
        in_specs=[pl.BlockSpec((t, LANES), lambda c: (0, c)),
                  pl.BlockSpec((t, LANES), lambda c: (0, nblk + c)),
                  pl.BlockSpec((32, LANES), lambda c: (0, c))],
        out_specs=pl.BlockSpec((t, LANES), lambda c: (0, c)),
        out_shape=jax.ShapeDtypeStruct((t, nblk * LANES), F32),
        scratch_shapes=[pltpu.VMEM((t + 32, LANES), F32)],
        compiler_params=_cp(("arbitrary",), 48),
    )(proj, proj, cw32)


def _rnn_fwd(proj, rw8, rvec, wab, wib, name):
    t = proj.shape[0]
    nblk = rvec.shape[1] // LANES

    def body(ux_ref, uy_ref, rw_ref, rvec_ref, wa_ref, wi_ref, h_ref, yr_ref, ext_ref):
        xr = _rnn_conv(ux_ref[...], rw_ref, rvec_ref, ext_ref)
        _, _, ig, _, a, mult = _rglru_gates(xr, wa_ref, wi_ref, rvec_ref)
        h = _scan_fwd(a, mult * (ig * xr))
        h_ref[...] = h
        ge, _ = _gelu_and_grad(uy_ref[...])
        yr_ref[...] = (ge * h).astype(BF16)

    blk = lambda off: pl.BlockSpec((t, LANES), lambda c: (0, off + c))
    return pl.pallas_call(
        body, name=name, grid=(nblk,),
        in_specs=[blk(2 * nblk), blk(3 * nblk),
                  pl.BlockSpec((8, LANES), lambda c: (0, c)),
                  pl.BlockSpec((8, LANES), lambda c: (0, c)),
                  pl.BlockSpec((None, LANES, LANES), lambda c: (c, 0, 0)),
                  pl.BlockSpec((None, LANES, LANES), lambda c: (c, 0, 0))],
        out_specs=[blk(0), blk(0)],
        out_shape=[jax.ShapeDtypeStruct((t, nblk * LANES), F32), jax.ShapeDtypeStruct((t, nblk * LANES), BF16)],
        scratch_shapes=[pltpu.VMEM((t + 8, LANES), F32)],
        compiler_params=_cp(("arbitrary",), 56),
    )(proj, proj, rw8, rvec, wab, wib)


def _ln_silu(cv, lnv_ref):
    mu = jnp.mean(cv, axis=-1, keepdims=True)
    xc = cv - mu
    rs = lax.rsqrt(jnp.mean(xc * xc, axis=-1, keepdims=True) + EPS)
    chat = xc * rs
    z = chat * lnv_ref[0:1, :] + lnv_ref[1:2, :]
    sg = _sigmoid(z)
    return rs, chat, z, sg


def _mix_out(x, cv, yr, vec, lnv, wout, tm, name):
    t, d = x.shape
    dc = cv.shape[1]

    def body(x_ref, cv_ref, yr_ref, vec_ref, lnv_ref, w_ref, xo_ref, ym_ref, yc_ref):
        _, _, z, sg = _ln_silu(cv_ref[...], lnv_ref)
        yc = (z * sg).astype(BF16)
        yr = yr_ref[...]
        yc_ref[:, 0:dc] = yc
        yc_ref[:, dc:] = yr
        ym = _dot(yc, w_ref[0:dc, :]) + _dot(yr, w_ref[dc:, :])
        ym_ref[...] = ym.astype(BF16)
        xo_ref[...] = x_ref[...] + vec_ref[3:4, :] * ym

    return pl.pallas_call(
        body, name=name, grid=(t // tm,),
        in_specs=[pl.BlockSpec((tm, d), lambda i: (i, 0)),
                  pl.BlockSpec((tm, dc), lambda i: (i, 0)),
                  pl.BlockSpec((tm, dc), lambda i: (i, 0)),
                  pl.BlockSpec((8, d), lambda i: (0, 0)),
                  pl.BlockSpec((8, dc), lambda i: (0, 0)),
                  pl.BlockSpec((d, d), lambda i: (0, 0))],
        out_specs=[pl.BlockSpec((tm, d), lambda i: (i, 0)),
                   pl.BlockSpec((tm, d), lambda i: (i, 0)),
                   pl.BlockSpec((tm, d), lambda i: (i, 0))],
        out_shape=[jax.ShapeDtypeStruct((t, d), F32), jax.ShapeDtypeStruct((t, d), BF16),
                   jax.ShapeDtypeStruct((t, d), BF16)],
        compiler_params=_cp(("arbitrary",), 48),
    )(x, cv, yr, vec, lnv, wout)


def _final(x, tgt, vec, tm, name):
    t, d = x.shape
    nt = t // tm

    def body(x_ref, t_ref, vec_ref, dx_ref, dvec_ref):
        i = pl.program_id(0)

        @pl.when(i == 0)
        def _():
            dvec_ref[...] = jnp.zeros_like(dvec_ref)

        xv = x_ref[...]
        e = _adaln(xv, vec_ref) - t_ref[...]
        dvec_ref[4:5, :] += (0.5 / d) * jnp.sum(e * e, axis=0, keepdims=True)
        dx_ref[...] = _adaln_bwd(xv, e * (1.0 / d), vec_ref, dvec_ref)

        @pl.when(i == nt - 1)
        def _():
            _adaln_finish(vec_ref, dvec_ref)

    return pl.pallas_call(
        body, name=name, grid=(nt,),
        in_specs=[pl.BlockSpec((tm, d), lambda i: (i, 0)),
                  pl.BlockSpec((tm, d), lambda i: (i, 0)),
                  pl.BlockSpec((8, d), lambda i: (0, 0))],
        out_specs=[pl.BlockSpec((tm, d), lambda i: (i, 0)),
                   pl.BlockSpec((8, d), lambda i: (0, 0))],
        out_shape=[jax.ShapeDtypeStruct((t, d), F32), jax.ShapeDtypeStruct((8, d), F32)],
        compiler_params=_cp(("arbitrary",), 48),
    )(x, tgt, vec)


def _ffn_bwd_act(dxo, vec, gu, f, wo, tm, name, host=None):
    t, d = dxo.shape
    nj, fb = gu.shape[1], gu.shape[3]
    nt = t // tm

    def body(ins, outs, scr):
        dxo_ref, vec_ref, gu_ref, f_ref, wo_ref = ins
        dgu_ref, act_ref, df_ref, dvec_ref = outs
        i = pl.program_id(0)
        j = pl.program_id(1)

        @pl.when((i == 0) & (j == 0))
        def _():
            dvec_ref[...] = jnp.zeros_like(dvec_ref)

        @pl.when(j == 0)
        def _():
            dxo_v = dxo_ref[...]
            df_ref[...] = (0.5 * vec_ref[3:4, :] * dxo_v).astype(BF16)
            dvec_ref[2:3, :] += 0.5 * jnp.sum(dxo_v * f_ref[...].astype(F32), axis=0, keepdims=True)

        dact = _dot_nt(df_ref[...], wo_ref[...])
        g = gu_ref[0].astype(F32)
        u = gu_ref[1].astype(F32)
        sg = _sigmoid(g)
        sl = g * sg
        dgu_ref[0] = (dact * u * (sg * (1.0 + g * (1.0 - sg)))).astype(BF16)
        dgu_ref[1] = (dact * sl).astype(BF16)
        act_ref[...] = (sl * u).astype(BF16)

    tile = pl.BlockSpec((tm, d), lambda i, j: (i, 0))
    return _hosted_call(
        body, host, name, (nt, nj),
        [tile,
         pl.BlockSpec((8, d), lambda i, j: (0, 0)),
         pl.BlockSpec((2, None, tm, fb), lambda i, j: (0, j, i, 0)),
         tile,
         pl.BlockSpec((fb, d), lambda i, j: (j, 0))],
        [pl.BlockSpec((2, None, tm, fb), lambda i, j: (0, j, i, 0)),
         pl.BlockSpec((None, tm, fb), lambda i, j: (j, i, 0)),
         tile,
         pl.BlockSpec((8, d), lambda i, j: (0, 0))],
        [jax.ShapeDtypeStruct((2, nj, t, fb), BF16), jax.ShapeDtypeStruct((nj, t, fb), BF16),
         jax.ShapeDtypeStruct((t, d), BF16), jax.ShapeDtypeStruct((8, d), F32)],
        [], ("arbitrary", "arbitrary"), 48, (dxo, vec, gu, f, wo))


def _ffn_bwd_in(dxo, x, vec, dgu, wi, tm, name, host=None):
    t, d = x.shape
    nj, fb = wi.shape[1], wi.shape[2]
    nt = t // tm

    def body(ins, outs, scr):
        dxo_ref, x_ref, vec_ref, dgu_ref, wi_ref = ins
        dx_ref, dvec_ref = outs
        i = pl.program_id(0)

        @pl.when(i == 0)
        def _():
            dvec_ref[...] = jnp.zeros_like(dvec_ref)

        dh = jnp.zeros((tm, d), F32)
        for a in range(2):
            for k in range(nj):
                dh = dh + _dot(dgu_ref[a, k], wi_ref[a, k])
        dx_ref[...] = dxo_ref[...] + _adaln_bwd(x_ref[...], dh, vec_ref, dvec_ref)

        @pl.when(i == nt - 1)
        def _():
            _adaln_finish(vec_ref, dvec_ref)

    tile = pl.BlockSpec((tm, d), lambda i: (i, 0))
    return _hosted_call(
        body, host, name, (nt,),
        [tile, tile,
         pl.BlockSpec((8, d), lambda i: (0, 0)),
         pl.BlockSpec((2, nj, tm, fb), lambda i: (0, 0, i, 0)),
         pl.BlockSpec((2, nj, fb, d), lambda i: (0, 0, 0, 0))],
        [tile, pl.BlockSpec((8, d), lambda i: (0, 0))],
        [jax.ShapeDtypeStruct((t, d), F32), jax.ShapeDtypeStruct((8, d), F32)],
        [], ("arbitrary",), 60, (dxo, x, vec, dgu, wi))


def _mm_tn(a, b, a_spec, b_spec, nblk, nk, m, n, name):
    def body(a_ref, b_ref, o_ref, acc_ref):
        s = pl.program_id(1)

        @pl.when(s == 0)
        def _():
            acc_ref[...] = jnp.zeros_like(acc_ref)

        acc_ref[...] += _dot_tn(a_ref[...], b_ref[...])

        @pl.when(s == nk - 1)
        def _():
            o_ref[...] = acc_ref[...].astype(BF16)

    return pl.pallas_call(
        body, name=name, grid=(nblk, nk),
        in_specs=[a_spec, b_spec],
        out_specs=pl.BlockSpec((None, m, n), lambda k, s: (k, 0, 0)),
        out_shape=jax.ShapeDtypeStruct((nblk, m, n), BF16),
        scratch_shapes=[pltpu.VMEM((m, n), F32)],
        compiler_params=_cp(("arbitrary", "arbitrary"), 56),
    )(a, b)


def _wgrad_in(h, parts, cb, tk, name):
    t, d = h.shape
    per = parts[0].shape[1] // cb
    nblk = len(parts) * per
    nk = t // tk

    def body(h_ref, p0, p1, p2, p3, o_ref, acc_ref):
        k = pl.program_id(0)
        s = pl.program_id(1)

        @pl.when(s == 0)
        def _():
            acc_ref[...] = jnp.zeros_like(acc_ref)

        for p, p_ref in enumerate((p0, p1, p2, p3)):
            @pl.when(k // per == p)
            def _(p_ref=p_ref):
                acc_ref[...] += _dot_tn(h_ref[...], p_ref[...])

        @pl.when(s == nk - 1)
        def _():
            o_ref[...] = acc_ref[...].astype(BF16)

    def part_spec(p):
        return pl.BlockSpec((tk, cb), lambda k, s: (jnp.where(k // per == p, s, 0), jnp.where(k // per == p, k % per, 0)))

    return pl.pallas_call(
        body, name=name, grid=(nblk, nk),
        in_specs=[pl.BlockSpec((tk, d), lambda k, s: (s, 0))] + [part_spec(p) for p in range(len(parts))],
        out_specs=pl.BlockSpec((None, d, cb), lambda k, s: (k, 0, 0)),
        out_shape=jax.ShapeDtypeStruct((nblk, d, cb), BF16),
        scratch_shapes=[pltpu.VMEM((d, cb), F32)],
        compiler_params=_cp(("arbitrary", "arbitrary"), 48),
    )(h, *parts)


def _mixout_bwd(dxo, ym, cv, hr, proj, vec, lnv, wout, tm, name, host=None):
    t, d = dxo.shape
    dc = cv.shape[1]
    nt = t // tm

    def body(ins, outs, scr):
        dxo_ref, ym_ref, cv_ref, hr_ref, uy_ref, vec_ref, lnv_ref, w_ref = ins
        dym_ref, dcv_ref, dhr_ref, duy_ref, dln_ref, dgt_ref = outs
        i = pl.program_id(0)

        @pl.when(i == 0)
        def _():
            dln_ref[...] = jnp.zeros_like(dln_ref)
            dgt_ref[...] = jnp.zeros_like(dgt_ref)

        dxo_v = dxo_ref[...]
        dym = (vec_ref[3:4, :] * dxo_v).astype(BF16)
        dym_ref[...] = dym
        dgt_ref[0:1, :] += jnp.sum(dxo_v * ym_ref[...].astype(F32), axis=0, keepdims=True)
        dyc = _dot_nt(dym, w_ref[0:dc, :])
        dyr = _dot_nt(dym, w_ref[dc:, :])
        rs, chat, z, sg = _ln_silu(cv_ref[...], lnv_ref)
        dz = dyc * (sg * (1.0 + z * (1.0 - sg)))
        dln_ref[0:1, :] += jnp.sum(dz * chat, axis=0, keepdims=True)
        dln_ref[1:2, :] += jnp.sum(dz, axis=0, keepdims=True)
        dchat = dz * lnv_ref[0:1, :]
        dcv_ref[...] = rs * (dchat - jnp.mean(dchat, axis=-1, keepdims=True)
                             - chat * jnp.mean(dchat * chat, axis=-1, keepdims=True))
        ge, dge = _gelu_and_grad(uy_ref[...])
        dhr_ref[...] = dyr * ge
        duy_ref[...] = (dyr * hr_ref[...] * dge).astype(BF16)

    tile_d = pl.BlockSpec((tm, d), lambda i: (i, 0))
    tile_c = pl.BlockSpec((tm, dc), lambda i: (i, 0))
    return _hosted_call(
        body, host, name, (nt,),
        [tile_d, tile_d, tile_c, tile_c,
         pl.BlockSpec((tm, dc), lambda i: (i, 3)),
         pl.BlockSpec((8, d), lambda i: (0, 0)),
         pl.BlockSpec((8, dc), lambda i: (0, 0)),
         pl.BlockSpec((d, d), lambda i: (0, 0))],
        [tile_d, tile_c, tile_c, tile_c,
         pl.BlockSpec((8, dc), lambda i: (0, 0)),
         pl.BlockSpec((8, d), lambda i: (0, 0))],
        [jax.ShapeDtypeStruct((t, d), BF16), jax.ShapeDtypeStruct((t, dc), F32),
         jax.ShapeDtypeStruct((t, dc), F32), jax.ShapeDtypeStruct((t, dc), BF16),
         jax.ShapeDtypeStruct((8, dc), F32), jax.ShapeDtypeStruct((8, d), F32)],
        [], ("arbitrary",), 48, (dxo, ym, cv, hr, proj, vec, lnv, wout))


def _conv_bwd(proj, dcv, cw32, name):
    t = proj.shape[0]
    nblk = cw32.shape[1] // LANES
    ch = min(t, 128)

    def body(val_ref, gate_ref, dcv_ref, cw_ref, dval_ref, dgate_ref, dcw_ref, extu_ref, extd_ref):
        val = val_ref[...]
        sg = _sigmoid(gate_ref[...])
        extu_ref[0:32, :] = jnp.zeros((32, LANES), F32)
        extu_ref[32:, :] = val * sg
        dcv_v = dcv_ref[...]
        extd_ref[0:t, :] = dcv_v
        extd_ref[t:, :] = jnp.zeros((32, LANES), F32)
        for r in range(t // ch):
            acc = jnp.zeros((ch, LANES), F32)
            for k in range(CONV_W):
                off = r * ch + (CONV_W - 1 - k)
                acc = acc + cw_ref[k:k + 1, :] * extd_ref[off:off + ch, :]
            rows = slice(r * ch, (r + 1) * ch)
            sg_r = _sigmoid(gate_ref[rows, :])
            dval_ref[rows, :] = (acc * sg_r).astype(BF16)
            dgate_ref[rows, :] = (acc * val_ref[rows, :] * sg_r * (1.0 - sg_r)).astype(BF16)
        for k in range(CONV_W):
            off = 32 - (CONV_W - 1 - k)
            dcw_ref[k:k + 1, :] = jnp.sum(dcv_v * extu_ref[off:off + t, :], axis=0, keepdims=True)
        dcw_ref[31:32, :] = jnp.sum(dcv_v, axis=0, keepdims=True)

    blk = lambda off: pl.BlockSpec((t, LANES), lambda c: (0, off + c))
    return pl.pallas_call(
        body, name=name, grid=(nblk,),
        in_specs=[blk(0), blk(nblk), blk(0), pl.BlockSpec((32, LANES), lambda c: (0, c))],
        out_specs=[blk(0), blk(0), pl.BlockSpec((32, LANES), lambda c: (0, c))],
        out_shape=[jax.ShapeDtypeStruct((t, nblk * LANES), BF16), jax.ShapeDtypeStruct((t, nblk * LANES), BF16),
                   jax.ShapeDtypeStruct((32, nblk * LANES), F32)],
        scratch_shapes=[pltpu.VMEM((t + 32, LANES), F32), pltpu.VMEM((t + 32, LANES), F32)],
        compiler_params=_cp(("arbitrary",), 56),
    )(proj, proj, dcv, cw32)


def _rnn_bwd(proj, hr, dhr, rw8, rvec, wab, wib, name, host=None):
    t = proj.shape[0]
    nblk = rvec.shape[1] // LANES

    def body(ins, outs, scr):
        ux_ref, h_ref, dh_ref, rw_ref, rvec_ref, wa_ref, wi_ref = ins
        dux_ref, sm_ref, dwa_ref, dwi_ref = outs
        ext_ref, extd_ref = scr
        xr = _rnn_conv(ux_ref[...], rw_ref, rvec_ref, ext_ref)
        xb, r, ig, ls, a, mult = _rglru_gates(xr, wa_ref, wi_ref, rvec_ref)
        row = lax.broadcasted_iota(jnp.int32, (t, LANES), 0)
        a_next = jnp.where(row < t - 1, pltpu.roll(a, t - 1, 0), 0.0)
        g = _scan_rev(a_next, dh_ref[...])
        hprev = jnp.where(row >= 1, pltpu.roll(h_ref[...], 1, 0), 0.0)
        da = g * hprev
        dmult = g * (ig * xr)
        dig = g * mult * xr
        dxr = g * mult * ig
        dlog_a = a * (da - dmult * a / mult)
        dr = dlog_a * (RG_C * ls)
        dls = RG_C * jnp.sum(dlog_a * r, axis=0, keepdims=True)
        dpr = dr * r * (1.0 - r)
        dpi = dig * ig * (1.0 - ig)
        dprb = dpr.astype(BF16)
        dpib = dpi.astype(BF16)
        dxr = dxr + _dot_nt(dprb, wa_ref[...]) + _dot_nt(dpib, wi_ref[...])
        dwa_ref[...] = _dot_tn(xb, dprb)
        dwi_ref[...] = _dot_tn(xb, dpib)
        extd_ref[0:t, :] = dxr
        extd_ref[t:, :] = jnp.zeros((8, LANES), F32)
        dux = rw_ref[RNN_CONV_W - 1:RNN_CONV_W, :] * dxr
        for k in range(RNN_CONV_W - 1):
            d = RNN_CONV_W - 1 - k
            dux = dux + rw_ref[k:k + 1, :] * extd_ref[d:d + t, :]
        dux_ref[...] = dux.astype(BF16)
        for k in range(RNN_CONV_W):
            d = RNN_CONV_W - 1 - k
            sm_ref[k:k + 1, :] = jnp.sum(dxr * ext_ref[8 - d:8 - d + t, :], axis=0, keepdims=True)
        sm_ref[4:5, :] = jnp.sum(dxr, axis=0, keepdims=True)
        sm_ref[5:6, :] = jnp.sum(dpr, axis=0, keepdims=True)
        sm_ref[6:7, :] = jnp.sum(dpi, axis=0, keepdims=True)
        sm_ref[7:8, :] = dls * _sigmoid(-rvec_ref[3:4, :])

    blk = lambda off: pl.BlockSpec((t, LANES), lambda c: (0, off + c))
    sq = pl.BlockSpec((None, LANES, LANES), lambda c: (c, 0, 0))
    return _hosted_call(
        body, host, name, (nblk,),
        [blk(2 * nblk), blk(0), blk(0),
         pl.BlockSpec((8, LANES), lambda c: (0, c)),
         pl.BlockSpec((8, LANES), lambda c: (0, c)), sq, sq],
        [blk(0), pl.BlockSpec((8, LANES), lambda c: (0, c)), sq, sq],
        [jax.ShapeDtypeStruct((t, nblk * LANES), BF16), jax.ShapeDtypeStruct((8, nblk * LANES), F32),
         jax.ShapeDtypeStruct((nblk, LANES, LANES), F32), jax.ShapeDtypeStruct((nblk, LANES, LANES), F32)],
        [pltpu.VMEM((t + 8, LANES), F32), pltpu.VMEM((t + 8, LANES), F32)],
        ("arbitrary",), 60, (proj, hr, dhr, rw8, rvec, wab, wib))


def _mixin_bwd(dxo, x, parts, vec, win, tm, name):
    t, d = x.shape
    nb, _, cb = win.shape
    dc = parts[0].shape[1]
    per = dc // cb
    nt = t // tm

    def body(dxo_ref, x_ref, p0, p1, p2, p3, vec_ref, w_ref, dx_ref, dvec_ref):
        i = pl.program_id(0)

        @pl.when(i == 0)
        def _():
            dvec_ref[...] = jnp.zeros_like(dvec_ref)

        prefs = (p0, p1, p2, p3)
        dh = jnp.zeros((tm, d), F32)
        for k in range(nb):
            dh = dh + _dot_nt(prefs[k // per][:, (k % per) * cb:(k % per + 1) * cb], w_ref[k])
        dx_ref[...] = dxo_ref[...] + _adaln_bwd(x_ref[...], dh, vec_ref, dvec_ref)

        @pl.when(i == nt - 1)
        def _():
            _adaln_finish(vec_ref, dvec_ref)

    tile_d = pl.BlockSpec((tm, d), lambda i: (i, 0))
    tile_c = pl.BlockSpec((tm, dc), lambda i: (i, 0))
    return pl.pallas_call(
        body, name=name, grid=(nt,),
        in_specs=[tile_d, tile_d, tile_c, tile_c, tile_c, tile_c,
                  pl.BlockSpec((8, d), lambda i: (0, 0)),
                  pl.BlockSpec((nb, d, cb), lambda i: (0, 0, 0))],
        out_specs=[tile_d, pl.BlockSpec((8, d), lambda i: (0, 0))],
        out_shape=[jax.ShapeDtypeStruct((t, d), F32), jax.ShapeDtypeStruct((8, d), F32)],
        compiler_params=_cp(("arbitrary",), 48),
    )(dxo, x, *parts, vec, win)


def _coords():
    return lax.axis_index("x"), lax.axis_index("y"), lax.axis_index("c")


def _flip(v, bit):
    return 1 - v if bit else v


def _gather_copy(outs, send_sems, recv_sems, a, k, block, to, src=None):
    dst = outs[a].at[block]
    return pltpu.make_async_remote_copy(
        src_ref=dst if src is None else src, dst_ref=dst,
        send_sem=send_sems.at[a, k], recv_sem=recv_sems.at[a, k],
        device_id=to, device_id_type=MESH_IDS)


def _gather_start(ins, outs, send_sems, recv_sems, loc_sems):
    x, y, c = _coords()
    me = 4 * x + 2 * y + c
    for a in range(len(ins)):
        pltpu.make_async_copy(ins[a], outs[a].at[me], loc_sems.at[a]).start()
    for a in range(len(ins)):
        _gather_copy(outs, send_sems, recv_sems, a, 0, me, (x, y, 1 - c), src=ins[a]).start()
        for j, (cx, cy) in enumerate([(1 - x, y), (x, 1 - y), (1 - x, 1 - y)]):
            _gather_copy(outs, send_sems, recv_sems, a, 1 + j, me, (cx, cy, c), src=ins[a]).start()


def _gather_finish(ins, outs, send_sems, recv_sems, loc_sems):
    x, y, c = _coords()
    me = 4 * x + 2 * y + c
    sib = (x, y, 1 - c)
    chips = [(1 - x, y), (x, 1 - y), (1 - x, 1 - y)]
    n = len(ins)
    for a in range(n):
        for j, (cx, cy) in enumerate(chips):
            blk = 4 * cx + 2 * cy + c
            _gather_copy(outs, send_sems, recv_sems, a, 1 + j, blk, sib).wait_recv()
            _gather_copy(outs, send_sems, recv_sems, a, 4 + j, blk, sib).start()
    for a in range(n):
        _gather_copy(outs, send_sems, recv_sems, a, 0, 4 * x + 2 * y + (1 - c), sib).wait_recv()
        for j, (cx, cy) in enumerate(chips):
            _gather_copy(outs, send_sems, recv_sems, a, 4 + j, 4 * cx + 2 * cy + (1 - c), sib).wait_recv()
    for a in range(n):
        _gather_copy(outs, send_sems, recv_sems, a, 0, me, sib, src=ins[a]).wait_send()
        for j, (cx, cy) in enumerate(chips):
            _gather_copy(outs, send_sems, recv_sems, a, 1 + j, me, (cx, cy, c), src=ins[a]).wait_send()
            _gather_copy(outs, send_sems, recv_sems, a, 4 + j, 4 * cx + 2 * cy + c, sib).wait_send()
        pltpu.make_async_copy(ins[a], outs[a].at[me], loc_sems.at[a]).wait()


def _gather_shapes(shards):
    return [jax.ShapeDtypeStruct((NDEV,) + s.shape, s.dtype) for s in shards]


def _gather_sems(n):
    return [pltpu.SemaphoreType.DMA((n, 7)), pltpu.SemaphoreType.DMA((n, 7)), pltpu.SemaphoreType.DMA((n,))]


def _sibling_copies(ins, outs, send_sems, recv_sems):
    x, y, c = _coords()
    return [pltpu.make_async_remote_copy(
        src_ref=ins[a].at[2 * q + (1 - c)], dst_ref=outs[a].at[q],
        send_sem=send_sems.at[a, q], recv_sem=recv_sems.at[a, q],
        device_id=(x, y, 1 - c), device_id_type=MESH_IDS) for a in range(len(ins)) for q in range(4)]


def _sibling_shapes(parts):
    return [jax.ShapeDtypeStruct((4,) + p.shape[1:], p.dtype) for p in parts]


def _chips_copies(ins, outs, send_sems, recv_sems):
    x, y, c = _coords()
    copies = []
    for a in range(len(ins)):
        for k, (kx, ky) in enumerate([(1, 0), (0, 1), (1, 1)]):
            tx, ty = _flip(x, kx), _flip(y, ky)
            copies.append(pltpu.make_async_remote_copy(
                src_ref=ins[a].at[2 * tx + ty], dst_ref=outs[a].at[k],
                send_sem=send_sems.at[a, k], recv_sem=recv_sems.at[a, k],
                device_id=(tx, ty, c), device_id_type=MESH_IDS))
    return copies


def _chips_shapes(sums):
    return [jax.ShapeDtypeStruct((3,) + s.shape[1:], s.dtype) for s in sums]


class _Exchange:
    def __init__(self, kind, arrays):
        self.kind, self.arrays, self.n = kind, list(arrays), len(arrays)

    def out_shapes(self):
        return {"gather": _gather_shapes, "sibling": _sibling_shapes, "chips": _chips_shapes}[self.kind](self.arrays)

    def sems(self):
        if self.kind == "gather":
            return _gather_sems(self.n)
        k = 4 if self.kind == "sibling" else 3
        return [pltpu.SemaphoreType.DMA((self.n, k)), pltpu.SemaphoreType.DMA((self.n, k))]

    def _copies(self, ins, outs, sems):
        return (_sibling_copies if self.kind == "sibling" else _chips_copies)(ins, outs, *sems)

    def start(self, ins, outs, sems):
        if self.kind == "gather":
            _gather_start(ins, outs, *sems)
        else:
            for cpy in self._copies(ins, outs, sems):
                cpy.start()

    def finish(self, ins, outs, sems):
        if self.kind == "gather":
            _gather_finish(ins, outs, *sems)
        else:
            for cpy in self._copies(ins, outs, sems):
                cpy.wait()


def _hosted_call(body, host, name, grid, in_specs, out_specs, out_shape, scratch, sem, vmem_mb, args):
    n = host.n if host else 0
    ni, no, ns = len(in_specs), len(out_specs), len(scratch)

    def full(*refs):
        ins, h_in = refs[:ni], refs[ni:ni + n]
        outs, h_out = refs[ni + n:ni + n + no], refs[ni + n + no:ni + 2 * n + no]
        scr, sems = refs[ni + 2 * n + no:ni + 2 * n + no + ns], refs[ni + 2 * n + no + ns:]
        if host:
            first = functools.reduce(lambda a, b: a & b, [pl.program_id(k) == 0 for k in range(len(grid))])
            last = functools.reduce(lambda a, b: a & b, [pl.program_id(k) == g - 1 for k, g in enumerate(grid)])

            @pl.when(first)
            def _():
                host.start(h_in, h_out, sems)

        body(ins, outs, scr)

        if host:
            @pl.when(last)
            def _():
                host.finish(h_in, h_out, sems)

    anyspec = pl.BlockSpec(memory_space=pl.ANY)
    return pl.pallas_call(
        full, name=name, grid=grid,
        in_specs=list(in_specs) + [anyspec] * n, out_specs=list(out_specs) + [anyspec] * n,
        out_shape=list(out_shape) + (host.out_shapes() if host else []),
        scratch_shapes=list(scratch) + (host.sems() if host else []),
        compiler_params=_cp(sem, vmem_mb),
    )(*args, *(host.arrays if host else []))


def _exchange(host, name):
    def body(*refs):
        n = host.n
        host.start(refs[:n], refs[n:2 * n], refs[2 * n:])
        host.finish(refs[:n], refs[n:2 * n], refs[2 * n:])

    anyspec = pl.BlockSpec(memory_space=pl.ANY)
    return pl.pallas_call(
        body, name=name, in_specs=[anyspec] * host.n, out_specs=[anyspec] * host.n,
        out_shape=host.out_shapes(), scratch_shapes=host.sems(),
    )(*host.arrays)


def _chip_sum(part, recv, sel, tr, name):
    _, _, r, c = part.shape

    def body(sel_ref, p_ref, r_ref, cs_ref, own_ref):
        q = pl.program_id(1)
        s = p_ref[...].astype(F32) + r_ref[...].astype(F32)
        cs_ref[...] = s.astype(BF16)

        @pl.when(q == sel_ref[1])
        def _():
            own_ref[...] = s

    return pl.pallas_call(
        body, name=name,
        grid_spec=pltpu.PrefetchScalarGridSpec(
            num_scalar_prefetch=1, grid=(r // tr, 4),
            in_specs=[pl.BlockSpec((None, None, tr, c), lambda i, q, s: (q, s[0], i, 0)),
                      pl.BlockSpec((None, tr, c), lambda i, q, s: (q, i, 0))],
            out_specs=[pl.BlockSpec((None, tr, c), lambda i, q, s: (q, i, 0)),
                       pl.BlockSpec((tr, c), lambda i, q, s: (i, 0))]),
        out_shape=[jax.ShapeDtypeStruct((4, r, c), BF16), jax.ShapeDtypeStruct((r, c), F32)],
        compiler_params=_cp(("arbitrary", "arbitrary"), 48),
    )(sel, part, recv)


def _gather_direct(src_ref, buf_ref, send_sems, recv_sems):
    x, y, c = _coords()
    me = 4 * x + 2 * y + c
    buf_ref[me] = src_ref[...]
    copies = []
    for k in range(1, NDEV):
        kx, ky, kc = (k >> 2) & 1, (k >> 1) & 1, k & 1
        copies.append(pltpu.make_async_remote_copy(
            src_ref=src_ref, dst_ref=buf_ref.at[me],
            send_sem=send_sems.at[k - 1], recv_sem=recv_sems.at[k - 1],
            device_id=(_flip(x, kx), _flip(y, ky), _flip(c, kc)), device_id_type=MESH_IDS))
    for cpy in copies:
        cpy.start()
    for k in range(1, NDEV):
        kx, ky, kc = (k >> 2) & 1, (k >> 1) & 1, k & 1
        peer = 4 * _flip(x, kx) + 2 * _flip(y, ky) + _flip(c, kc)
        pltpu.make_async_remote_copy(
            src_ref=src_ref, dst_ref=buf_ref.at[peer],
            send_sem=send_sems.at[k - 1], recv_sem=recv_sems.at[k - 1],
            device_id=(x, y, c), device_id_type=MESH_IDS).wait_recv()
    for cpy in copies:
        cpy.wait_send()
    return me


def _mod_exchange(c_row, wmod, bmod, wfmod, bfmod, name):
    d = c_row.shape[1]
    nm, nf = wmod.shape[1], wfmod.shape[1]
    nw = nm + nf

    def body(c_ref, wm_ref, bm_ref, wf_ref, bf_ref, cs_ref, mod_ref, fmod_ref,
             slab_ref, csbuf_ref, mslab_ref, mbuf_ref, s1, r1, s2, r2):
        cv = c_ref[...]
        slab_ref[...] = jnp.broadcast_to(cv * _sigmoid(cv), (8, d))
        _gather_direct(slab_ref, csbuf_ref, s1, r1)
        for b in range(NDEV):
            cs_ref[b:b + 1, :] = csbuf_ref[b, 0:1, :]
        cs = cs_ref[...]
        mslab_ref[:, 0:nm] = jnp.dot(cs, wm_ref[...], precision=HI, preferred_element_type=F32) + bm_ref[...]
        mslab_ref[:, nm:] = jnp.dot(cs, wf_ref[...], precision=HI, preferred_element_type=F32) + bf_ref[...]
        me = _gather_direct(mslab_ref, mbuf_ref, s2, r2)
        mine = lax.broadcasted_iota(jnp.int32, (8, nw), 0) == me
        for k in range(NDEV):
            rowk = jnp.sum(jnp.where(mine, mbuf_ref[k], 0.0), axis=0, keepdims=True)
            mod_ref[k:k + 1, :] = rowk[:, 0:nm]
            fmod_ref[k:k + 1, :] = rowk[:, nm:]

    vm = pl.BlockSpec(memory_space=pltpu.VMEM)
    return pl.pallas_call(
        body, name=name,
        in_specs=[vm] * 5, out_specs=[vm] * 3,
        out_shape=[jax.ShapeDtypeStruct((NDEV, d), F32), jax.ShapeDtypeStruct((NDEV, nm), F32),
                   jax.ShapeDtypeStruct((NDEV, nf), F32)],
        scratch_shapes=[pltpu.VMEM((8, d), F32), pltpu.VMEM((NDEV, 8, d), F32),
                        pltpu.VMEM((8, nw), F32), pltpu.VMEM((NDEV, 8, nw), F32),
                        pltpu.SemaphoreType.DMA((7,)), pltpu.SemaphoreType.DMA((7,)),
                        pltpu.SemaphoreType.DMA((7,)), pltpu.SemaphoreType.DMA((7,))],
        compiler_params=pltpu.CompilerParams(vmem_limit_bytes=40 * 1024 * 1024),
    )(c_row, wmod, bmod, wfmod, bfmod)


def _small_exchange(pack, name):
    def body(p_ref, all_ref, sum_ref, s, r):
        _gather_direct(p_ref, all_ref, s, r)
        tot = all_ref[0]
        for k in range(1, NDEV):
            tot = tot + all_ref[k]
        sum_ref[...] = tot

    vm = pl.BlockSpec(memory_space=pltpu.VMEM)
    return pl.pallas_call(
        body, name=name,
        in_specs=[vm], out_specs=[vm, vm],
        out_shape=[jax.ShapeDtypeStruct((NDEV,) + pack.shape, F32), jax.ShapeDtypeStruct(pack.shape, F32)],
        scratch_shapes=[pltpu.SemaphoreType.DMA((7,)), pltpu.SemaphoreType.DMA((7,))],
    )(pack)


def _adamw_math(w, g, m, v):
    m = ADAM_B1 * m + (1.0 - ADAM_B1) * g
    v = ADAM_B2 * v + (1.0 - ADAM_B2) * (g * g)
    m_hat = m / (1.0 - ADAM_B1 ** ADAM_STEP)
    v_hat = v / (1.0 - ADAM_B2 ** ADAM_STEP)
    delta = -ADAM_LR * (m_hat / (jnp.sqrt(v_hat) + ADAM_EPS) + ADAM_WD * w)
    return delta, m, v


def _adamw_small(w, g, m, v, name):
    def body(w_ref, g_ref, m_ref, v_ref, d_ref, mo_ref, vo_ref):
        d_ref[...], mo_ref[...], vo_ref[...] = _adamw_math(w_ref[...], g_ref[...], m_ref[...], v_ref[...])

    vm = pl.BlockSpec(memory_space=pltpu.VMEM)
    sds = jax.ShapeDtypeStruct(w.shape, F32)
    return pl.pallas_call(body, name=name, in_specs=[vm] * 4, out_specs=[vm] * 3,
                          out_shape=[sds, sds, sds])(w, g, m, v)


def _rs_final(own, recv, w, m, v, tr, name):
    r, c = own.shape

    def body(o_ref, r_ref, w_ref, m_ref, v_ref, g_ref, d_ref, mo_ref, vo_ref):
        g = o_ref[...] + r_ref[0].astype(F32) + r_ref[1].astype(F32) + r_ref[2].astype(F32)
        g_ref[...] = g
        d_ref[...], mo_ref[...], vo_ref[...] = _adamw_math(w_ref[...], g, m_ref[...], v_ref[...])

    tile = pl.BlockSpec((tr, c), lambda i: (i, 0))
    sds = jax.ShapeDtypeStruct((r, c), F32)
    return pl.pallas_call(
        body, name=name, grid=(r // tr,),
        in_specs=[tile, pl.BlockSpec((3, tr, c), lambda i: (0, i, 0)), tile, tile, tile],
        out_specs=[tile] * 4, out_shape=[sds] * 4,
        compiler_params=_cp(("arbitrary",), 48),
    )(own, recv, w, m, v)


def _mod_weight_update(cs, dm, w, m, v, tr, name):
    r, c = w.shape

    def body(cs_ref, dm_ref, w_ref, m_ref, v_ref, g_ref, d_ref, mo_ref, vo_ref):
        g = lax.dot_general(cs_ref[...], dm_ref[...], (((0,), (0,)), ((), ())),
                            precision=HI, preferred_element_type=F32)
        g_ref[...] = g
        d_ref[...], mo_ref[...], vo_ref[...] = _adamw_math(w_ref[...], g, m_ref[...], v_ref[...])

    tile = pl.BlockSpec((tr, c), lambda i: (i, 0))
    sds = jax.ShapeDtypeStruct((r, c), F32)
    return pl.pallas_call(
        body, name=name, grid=(r // tr,),
        in_specs=[pl.BlockSpec((NDEV, tr), lambda i: (0, i)), pl.BlockSpec((NDEV, c), lambda i: (0, 0)),
                  tile, tile, tile],
        out_specs=[tile] * 4, out_shape=[sds] * 4,
        compiler_params=_cp(("arbitrary",), 48),
    )(cs, dm, w, m, v)


def _rows(*vs):
    d = vs[0].shape[-1]
    rows = [v.reshape(1, d) for v in vs]
    return jnp.concatenate(rows + [jnp.zeros((8 - len(rows), d), F32)], axis=0)


def _block_diag_pairs(w):
    hd = w.shape[-1]
    z = jnp.zeros((w.shape[0] // 2, hd, hd), w.dtype)
    top = jnp.concatenate([w[0::2], z], axis=2)
    bot = jnp.concatenate([z, w[1::2]], axis=2)
    return jnp.concatenate([top, bot], axis=1).astype(BF16)


def _diag_pairs(g):
    hd = g.shape[-1] // 2
    both = jnp.stack([g[:, :hd, :hd], g[:, hd:, hd:]], axis=1)
    return both.reshape(2 * g.shape[0], hd, hd)


def kernel(x, c, w_mod, b_mod, g_ffn1, w_ffn1_in, w_ffn1_out, g_mix, w_in, conv_w, conv_b, ln_g, ln_b, rnn_conv_w, rnn_conv_b, w_a, b_a, w_i, b_i, lru_lambda, w_out, g_ffn2, w_ffn2_in, w_ffn2_out, w_fmod, b_fmod, g_final, loss_target, m_w_mod, m_b_mod, m_g_ffn1, m_w_ffn1_in, m_w_ffn1_out, m_g_mix, m_w_in, m_conv_w, m_conv_b, m_ln_g, m_ln_b, m_rnn_conv_w, m_rnn_conv_b, m_w_a, m_b_a, m_w_i, m_b_i, m_lru_lambda, m_w_out, m_g_ffn2, m_w_ffn2_in, m_w_ffn2_out, m_w_fmod, m_b_fmod, m_g_final, v_w_mod, v_b_mod, v_g_ffn1, v_w_ffn1_in, v_w_ffn1_out, v_g_mix, v_w_in, v_conv_w, v_conv_b, v_ln_g, v_ln_b, v_rnn_conv_w, v_rnn_conv_b, v_w_a, v_b_a, v_w_i, v_b_i, v_lru_lambda, v_w_out, v_g_ffn2, v_w_ffn2_in, v_w_ffn2_out, v_w_fmod, v_b_fmod, v_g_final):
    t, d = x.shape[1], x.shape[2]
    fb = w_ffn1_in.shape[2]
    nm = w_mod.shape[2]
    nf = w_fmod.shape[1]
    dc = conv_b.shape[1]
    cl = conv_w.shape[2]
    tm = min(TOKEN_TILE, t)
    tk = min(WGRAD_TILE, t)
    nk = t // tk
    me = 4 * lax.axis_index("x") + 2 * lax.axis_index("y") + lax.axis_index("c")

    bmod_l = lax.dynamic_slice(b_mod, (0, me * nm), (1, nm))
    bfmod_l = lax.dynamic_slice(b_fmod.reshape(1, -1), (0, me * nf), (1, nf))
    cs, mod_rows, fmod_rows = _mod_exchange(c, w_mod[0], bmod_l, w_fmod, bfmod_l, "mod_exchange")
    mod = mod_rows.reshape(9, d)
    fmod = fmod_rows.reshape(2, d)
    vec1 = _rows(g_ffn1, mod[0], mod[1], mod[2])
    vecm = _rows(g_mix, mod[3], mod[4], mod[5])
    vec3 = _rows(g_ffn2, mod[6], mod[7], mod[8])
    vecf = _rows(g_final, fmod[0], fmod[1])

    tr = jnp.transpose
    cwl = jnp.concatenate([conv_w[0], jnp.zeros((1, cl), F32), rnn_conv_w[0], jnp.zeros((4, cl), F32)], axis=0)
    wi1, wo1, cwg = _exchange(
        _Exchange("gather", [tr(w_ffn1_in[0]).astype(BF16), w_ffn1_out[0].astype(BF16), cwl]), "gather_ffn1")
    wi1 = wi1.reshape(2, 4, fb, d)
    wo1 = wo1.reshape(4 * fb, d)

    xin = x[0]
    later = [w_in[0].astype(BF16), w_out[0].astype(BF16), tr(w_ffn2_in[0]).astype(BF16), w_ffn2_out[0].astype(BF16)]
    x1, h1, gu1, f1, win, wout, wi2, wo2 = _ffn_fwd(xin, vec1, wi1, wo1, tm, "ffn1_fwd",
                                                    host=_Exchange("gather", later))
    wi2 = wi2.reshape(2, 4, fb, d)
    wo2 = wo2.reshape(4 * fb, d)
    wout = wout.reshape(d, d)
    h2, proj = _mix_in(x1, vecm, win, tm, "mix_in")
    lnv = _rows(ln_g, ln_b)
    rvec = _rows(rnn_conv_b, b_a, b_i, lru_lambda)
    wab = _block_diag_pairs(w_a[0])
    wib = _block_diag_pairs(w_i[0])
    cwf = jnp.transpose(cwg, (1, 0, 2)).reshape(40, NDEV * cl)
    cw32 = jnp.concatenate([cwf[0:CONV_W], conv_b], axis=0)
    rw8 = cwf[32:40]

    cv = _conv_fwd(proj, cw32, "conv_fwd")
    hr, yr = _rnn_fwd(proj, rw8, rvec, wab, wib, "rnn_fwd")
    x2, ym, ycat = _mix_out(x1, cv, yr, vecm, lnv, wout, tm, "mix_out")
    x3, h3, gu3, f3 = _ffn_fwd(x2, vec3, wi2, wo2, tm, "ffn2_fwd")

    dx3, dvf = _final(x3, loss_target[0], vecf, tm, "final_loss")
    a_tok = lambda width: pl.BlockSpec((tk, width), lambda k, s: (s, 0))
    blk3 = lambda width: pl.BlockSpec((None, tk, width), lambda k, s: (k, s, 0))
    sel = jnp.stack([lax.axis_index("c"), 2 * lax.axis_index("x") + lax.axis_index("y")]).astype(jnp.int32)
    row_tile = {"w_ffn1_in": fb // 4, "w_ffn1_out": fb // 4, "w_in": 512, "w_out": 128,
                "w_ffn2_in": fb // 4, "w_ffn2_out": fb // 4}

    def chip_sums(names, partials, from_sib):
        out = [_chip_sum(p.reshape((4, 2) + p.shape[1:]), r, sel, row_tile[nm_], "chip_sum_" + nm_)
               for nm_, p, r in zip(names, partials, from_sib)]
        return [o[0] for o in out], [o[1] for o in out]

    dgu3, act3, df3, dva3 = _ffn_bwd_act(dx3, vec3, gu3, f3, wo2, tm, "ffn2_bwd_act")
    p_wi2 = _mm_tn(dgu3.reshape(8, t, fb), h3, blk3(fb), a_tok(d), 8, nk, fb, d, "wgrad_ffn2_in")
    p_wo2 = _mm_tn(act3, df3, blk3(fb), a_tok(d), 4, nk, fb, d, "wgrad_ffn2_out").reshape(NDEV, fb // 2, d)
    names2 = ["w_ffn2_in", "w_ffn2_out"]
    dx2, dv3, sib_wi2, sib_wo2 = _ffn_bwd_in(dx3, x2, vec3, dgu3, wi2, tm, "ffn2_bwd_in",
                                             host=_Exchange("sibling", [p_wi2, p_wo2]))
    (s_wi2, s_wo2), owns2 = chip_sums(names2, [p_wi2, p_wo2], [sib_wi2, sib_wo2])
    dym, dcv, dhr, duy, dln, dgt2, r_wo2 = _mixout_bwd(
        dx2, ym, cv, hr, proj, vecm, lnv, wout, tm, "mixout_bwd", host=_Exchange("chips", [s_wo2]))
    dval, dgate, dcw = _conv_bwd(proj, dcv, cw32, "conv_bwd")
    dux, rsm, dwab, dwib, r_wi2 = _rnn_bwd(proj, hr, dhr, rw8, rvec, wab, wib, "rnn_bwd",
                                           host=_Exchange("chips", [s_wi2]))
    parts = [dval, dgate, dux, duy]
    dx1, dvm = _mixin_bwd(dx2, x1, parts, vecm, win, tm, "mixin_bwd")
    p_wout = _mm_tn(ycat, dym, pl.BlockSpec((tk, LANES), lambda k, s: (s, k)), a_tok(d),
                    NDEV, nk, LANES, d, "wgrad_out")
    p_win = _wgrad_in(h2, parts, win.shape[2], tk, "wgrad_in")
    namesm = ["w_in", "w_out"]
    sumsm, ownsm = chip_sums(namesm, [p_win, p_wout],
                             _exchange(_Exchange("sibling", [p_win, p_wout]), "rs_sibling_mix"))
    dgu1, act1, df1, dva1, r_win, r_wout = _ffn_bwd_act(dx1, vec1, gu1, f1, wo1, tm, "ffn1_bwd_act",
                                                        host=_Exchange("chips", sumsm))
    p_wi1 = _mm_tn(dgu1.reshape(8, t, fb), h1, blk3(fb), a_tok(d), 8, nk, fb, d, "wgrad_ffn1_in")
    p_wo1 = _mm_tn(act1, df1, blk3(fb), a_tok(d), 4, nk, fb, d, "wgrad_ffn1_out").reshape(NDEV, fb // 2, d)
    names1 = ["w_ffn1_in", "w_ffn1_out"]
    sums1, owns1 = chip_sums(names1, [p_wi1, p_wo1],
                             _exchange(_Exchange("sibling", [p_wi1, p_wo1]), "rs_sibling_ffn1"))
    dx0, dv1, r_wi1, r_wo1 = _ffn_bwd_in(dx1, xin, vec1, dgu1, wi1, tm, "ffn1_bwd_in",
                                         host=_Exchange("chips", sums1))
    from_chips = {"w_in": r_win, "w_out": r_wout, "w_ffn2_in": r_wi2, "w_ffn2_out": r_wo2,
                  "w_ffn1_in": r_wi1, "w_ffn1_out": r_wo1}
    owns = dict(zip(namesm + names2 + names1, ownsm + owns2 + owns1))

    big = {"w_ffn1_in": (tr(w_ffn1_in[0]), tr(m_w_ffn1_in[0]), tr(v_w_ffn1_in[0])),
           "w_ffn1_out": (w_ffn1_out[0], m_w_ffn1_out[0], v_w_ffn1_out[0]),
           "w_in": (w_in[0], m_w_in[0], v_w_in[0]), "w_out": (w_out[0], m_w_out[0], v_w_out[0]),
           "w_ffn2_in": (tr(w_ffn2_in[0]), tr(m_w_ffn2_in[0]), tr(v_w_ffn2_in[0])),
           "w_ffn2_out": (w_ffn2_out[0], m_w_ffn2_out[0], v_w_ffn2_out[0])}
    res = {}
    for nm_ in namesm + names2 + names1:
        out4 = _rs_final(owns[nm_], from_chips[nm_], *big[nm_], row_tile[nm_], "rs_final_" + nm_)
        if nm_ in ("w_ffn1_in", "w_ffn2_in"):
            out4 = [tr(o) for o in out4]
        res[nm_] = [o[None] for o in out4]

    dmod_row = jnp.concatenate([dv1[1:2], dv1[3:4], dva1[2:3], dvm[1:2], dvm[3:4], dgt2[0:1],
                                dv3[1:2], dv3[3:4], dva3[2:3]], axis=0)
    pack = jnp.concatenate([
        dmod_row, dvf[1:2], dvf[3:4],
        dv1[0:1], dvm[0:1], dv3[0:1], dvf[0:1],
        dcw.reshape(16, d),
        jnp.concatenate([dln[0:1], dln[1:2]], axis=1),
        rsm.reshape(4, d),
        _diag_pairs(dwab).reshape(32, d), _diag_pairs(dwib).reshape(32, d),
        dvf[4:5],
        jnp.zeros((SMALL_ROWS - 101, d), F32)], axis=0)
    allp, tot = _small_exchange(pack, "small_exchange")
    loss = jnp.sum(tot[100])

    dm_all = allp[:, 0:9, :].reshape(NDEV, 9 * d)
    dfm_all = allp[:, 9:11, :].reshape(NDEV, 2 * d)
    dm_l = lax.dynamic_slice(dm_all, (0, me * nm), (NDEV, nm))
    dfm_l = lax.dynamic_slice(dfm_all, (0, me * nf), (NDEV, nf))
    out_wmod = [o[None] for o in _mod_weight_update(cs, dm_l, w_mod[0], m_w_mod[0], v_w_mod[0], 256, "w_mod_update")]
    out_wfmod = _mod_weight_update(cs, dfm_l, w_fmod, m_w_fmod, v_w_fmod, 256, "w_fmod_update")
    res["w_mod"] = out_wmod
    res["w_fmod"] = list(out_wfmod)

    dcw_f = tot[15:31].reshape(32, dc)
    rsm_f = tot[32:36].reshape(8, dc)
    small_grads = {
        "b_mod": tot[0:9].reshape(1, 9 * d),
        "b_fmod": tot[9:11].reshape(2 * d),
        "g_ffn1": tot[11:12], "g_mix": tot[12:13], "g_ffn2": tot[13:14], "g_final": tot[14],
        "conv_w": lax.dynamic_slice(dcw_f, (0, me * cl), (CONV_W, cl))[None],
        "conv_b": dcw_f[31:32],
        "ln_g": tot[31:32, 0:dc], "ln_b": tot[31:32, dc:],
        "rnn_conv_w": lax.dynamic_slice(rsm_f, (0, me * cl), (RNN_CONV_W, cl))[None],
        "rnn_conv_b": rsm_f[4:5], "b_a": rsm_f[5:6], "b_i": rsm_f[6:7], "lru_lambda": rsm_f[7:8],
        "w_a": tot[36:68].reshape(w_a.shape), "w_i": tot[68:100].reshape(w_i.shape),
    }
    small_params = {
        "b_mod": (b_mod, m_b_mod, v_b_mod), "b_fmod": (b_fmod, m_b_fmod, v_b_fmod),
        "g_ffn1": (g_ffn1, m_g_ffn1, v_g_ffn1), "g_mix": (g_mix, m_g_mix, v_g_mix),
        "g_ffn2": (g_ffn2, m_g_ffn2, v_g_ffn2), "g_final": (g_final, m_g_final, v_g_final),
        "conv_w": (conv_w, m_conv_w, v_conv_w), "conv_b": (conv_b, m_conv_b, v_conv_b),
        "ln_g": (ln_g, m_ln_g, v_ln_g), "ln_b": (ln_b, m_ln_b, v_ln_b),
        "rnn_conv_w": (rnn_conv_w, m_rnn_conv_w, v_rnn_conv_w),
        "rnn_conv_b": (rnn_conv_b, m_rnn_conv_b, v_rnn_conv_b),
        "w_a": (w_a, m_w_a, v_w_a), "b_a": (b_a, m_b_a, v_b_a),
        "w_i": (w_i, m_w_i, v_w_i), "b_i": (b_i, m_b_i, v_b_i),
        "lru_lambda": (lru_lambda, m_lru_lambda, v_lru_lambda),
    }
    for name, g in small_grads.items():
        w, m, v = small_params[name]
        shp = w.shape
        two_d = (-1, shp[-1]) if w.ndim > 1 else (1, shp[0])
        outs = _adamw_small(w.reshape(two_d), g.reshape(two_d), m.reshape(two_d), v.reshape(two_d),
                            "adamw_" + name)
        res[name] = [g.reshape(shp)] + [o.reshape(shp) for o in outs]

    order = ["w_mod", "b_mod", "g_ffn1", "w_ffn1_in", "w_ffn1_out", "g_mix", "w_in", "conv_w", "conv_b",
             "ln_g", "ln_b", "rnn_conv_w", "rnn_conv_b", "w_a", "b_a", "w_i", "b_i", "lru_lambda", "w_out",
             "g_ffn2", "w_ffn2_in", "w_ffn2_out", "w_fmod", "b_fmod", "g_final"]
    return (loss, dx0[None], *[res[n][0] for n in order], *[res[n][1] for n in order],
            *[res[n][2] for n in order], *[res[n][3] for n in order])
```

```python
import functools
import math

import jax
import jax.numpy as jnp
from jax import lax
from jax.experimental import pallas as pl
from jax.experimental.pallas import tpu as pltpu

F32 = jnp.float32
BF16 = jnp.bfloat16
MESH_IDS = pl.DeviceIdType.MESH
NDEV = 8
EPS = 1e-6
RG_C = 8.0
CONV_W = 31
RNN_CONV_W = 4
LANES = 128
ADAM_LR = 0.001
ADAM_B1 = 0.9
ADAM_B2 = 0.999
ADAM_EPS = 1e-08
ADAM_WD = 0.01
ADAM_STEP = 10
SMALL_ROWS = 104
TOKEN_TILE = 512
WGRAD_TILE = 2048
HI = lax.Precision.HIGHEST


def _cp(sem, vmem_mb):
    return pltpu.CompilerParams(dimension_semantics=sem, vmem_limit_bytes=vmem_mb * 1024 * 1024)


def _dot(a, b):
    return jnp.dot(a, b, preferred_element_type=F32)


def _dot_nt(a, b):
    return lax.dot_general(a, b, (((1,), (1,)), ((), ())), preferred_element_type=F32)


def _dot_tn(a, b):
    return lax.dot_general(a, b, (((0,), (0,)), ((), ())), preferred_element_type=F32)


def _sigmoid(x):
    return 1.0 / (1.0 + jnp.exp(-x))


def _adaln(x, vec_ref):
    rstd = lax.rsqrt(jnp.mean(x * x, axis=-1, keepdims=True) + EPS)
    return (x * rstd) * vec_ref[0:1, :] * (1.0 + vec_ref[2:3, :]) + vec_ref[1:2, :]


def _adaln_bwd(x, dh, vec_ref, dvec_ref):
    rstd = lax.rsqrt(jnp.mean(x * x, axis=-1, keepdims=True) + EPS)
    xhat = x * rstd
    dvec_ref[0:1, :] += jnp.sum(dh * xhat, axis=0, keepdims=True)
    dvec_ref[1:2, :] += jnp.sum(dh, axis=0, keepdims=True)
    dxhat = dh * (vec_ref[0:1, :] * (1.0 + vec_ref[2:3, :]))
    return rstd * (dxhat - xhat * jnp.mean(dxhat * xhat, axis=-1, keepdims=True))


def _adaln_finish(vec_ref, dvec_ref):
    s = dvec_ref[0:1, :]
    dvec_ref[3:4, :] = vec_ref[0:1, :] * s
    dvec_ref[0:1, :] = (1.0 + vec_ref[2:3, :]) * s


def _gelu_and_grad(x):
    k0 = math.sqrt(2.0 / math.pi)
    x2 = x * x
    t = jnp.tanh(k0 * (x + 0.044715 * x * x2))
    g = 0.5 * x * (1.0 + t)
    dg = 0.5 * (1.0 + t) + 0.5 * x * (1.0 - t * t) * (k0 * (1.0 + 3.0 * 0.044715 * x2))
    return g, dg


def _log_sigmoid(x):
    z = jnp.exp(-jnp.abs(x))
    u = 1.0 + z
    d = u - 1.0
    log1p = jnp.where(d == 0.0, z, jnp.log(u) * (z / jnp.where(d == 0.0, 1.0, d)))
    return jnp.minimum(x, 0.0) - log1p


def _neg_expm1(x):
    series = -x * (1.0 + x * (0.5 + x * (1.0 / 6.0 + x * (1.0 / 24.0 + x * (1.0 / 120.0)))))
    return jnp.where(x > -0.05, series, 1.0 - jnp.exp(x))


def _scan_fwd(a, b):
    n = a.shape[0]
    row = lax.broadcasted_iota(jnp.int32, a.shape, 0)
    s = 1
    while s < n:
        ok = row >= s
        b = a * jnp.where(ok, pltpu.roll(b, s, 0), 0.0) + b
        if 2 * s < n:
            a = a * jnp.where(ok, pltpu.roll(a, s, 0), 1.0)
        s *= 2
    return b


def _scan_rev(a, d):
    n = a.shape[0]
    row = lax.broadcasted_iota(jnp.int32, a.shape, 0)
    s = 1
    while s < n:
        ok = row < n - s
        d = a * jnp.where(ok, pltpu.roll(d, n - s, 0), 0.0) + d
        if 2 * s < n:
            a = a * jnp.where(ok, pltpu.roll(a, n - s, 0), 1.0)
        s *= 2
    return d


def _rglru_gates(xr, wa_ref, wi_ref, rvec_ref):
    xb = xr.astype(BF16)
    r = _sigmoid(_dot(xb, wa_ref[...]) + rvec_ref[1:2, :])
    ig = _sigmoid(_dot(xb, wi_ref[...]) + rvec_ref[2:3, :])
    ls = _log_sigmoid(rvec_ref[3:4, :])
    log_a = RG_C * r * ls
    a = jnp.exp(log_a)
    mult = jnp.sqrt(_neg_expm1(2.0 * log_a))
    return xb, r, ig, ls, a, mult


def _rnn_conv(ux, rw_ref, rvec_ref, ext_ref):
    t = ux.shape[0]
    ext_ref[0:8, :] = jnp.zeros((8, ux.shape[1]), F32)
    ext_ref[8:, :] = ux
    xr = rvec_ref[0:1, :] + rw_ref[RNN_CONV_W - 1:RNN_CONV_W, :] * ux
    for k in range(RNN_CONV_W - 1):
        d = RNN_CONV_W - 1 - k
        xr = xr + rw_ref[k:k + 1, :] * ext_ref[8 - d:8 - d + t, :]
    return xr


def _ffn_fwd(x, vec, wi, wo, tm, name, host=None):
    t, d = x.shape
    nj, fb = wi.shape[1], wi.shape[2]
    nt = t // tm

    def body(ins, outs, scr):
        x_ref, vec_ref, wi_ref, wo_ref = ins
        xo_ref, h_ref, gu_ref, f_ref = outs
        acc_ref, = scr
        j = pl.program_id(1)

        @pl.when(j == 0)
        def _():
            h_ref[...] = _adaln(x_ref[...], vec_ref).astype(BF16)
            acc_ref[...] = jnp.zeros_like(acc_ref)

        h = h_ref[...]
        gate = _dot_nt(h, wi_ref[0])
        up = _dot_nt(h, wi_ref[1])
        gu_ref[0] = gate.astype(BF16)
        gu_ref[1] = up.astype(BF16)
        act = (gate * _sigmoid(gate) * up).astype(BF16)
        acc_ref[...] += _dot(act, wo_ref[...])

        @pl.when(j == nj - 1)
        def _():
            f = acc_ref[...]
            f_ref[...] = f.astype(BF16)
            xo_ref[...] = x_ref[...] + 0.5 * vec_ref[3:4, :] * f

    tile = pl.BlockSpec((tm, d), lambda i, j: (i, 0))
    return _hosted_call(
        body, host, name, (nt, nj),
        [tile,
         pl.BlockSpec((8, d), lambda i, j: (0, 0)),
         pl.BlockSpec((2, None, fb, d), lambda i, j: (0, j, 0, 0)),
         pl.BlockSpec((fb, d), lambda i, j: (j, 0))],
        [tile, tile, pl.BlockSpec((2, None, tm, fb), lambda i, j: (0, j, i, 0)), tile],
        [jax.ShapeDtypeStruct((t, d), F32), jax.ShapeDtypeStruct((t, d), BF16),
         jax.ShapeDtypeStruct((2, nj, t, fb), BF16), jax.ShapeDtypeStruct((t, d), BF16)],
        [pltpu.VMEM((tm, d), F32)], ("arbitrary", "arbitrary"), 48, (x, vec, wi, wo))


def _mix_in(x, vec, win, tm, name):
    t, d = x.shape
    nb, _, cb = win.shape

    def body(x_ref, vec_ref, w_ref, h_ref, p_ref):
        h = _adaln(x_ref[...], vec_ref).astype(BF16)
        h_ref[...] = h
        for k in range(nb):
            p_ref[:, k * cb:(k + 1) * cb] = _dot(h, w_ref[k])

    return pl.pallas_call(
        body, name=name, grid=(t // tm,),
        in_specs=[pl.BlockSpec((tm, d), lambda i: (i, 0)),
                  pl.BlockSpec((8, d), lambda i: (0, 0)),
                  pl.BlockSpec((nb, d, cb), lambda i: (0, 0, 0))],
        out_specs=[pl.BlockSpec((tm, d), lambda i: (i, 0)),
                   pl.BlockSpec((tm, nb * cb), lambda i: (i, 0))],
        out_shape=[jax.ShapeDtypeStruct((t, d), BF16), jax.ShapeDtypeStruct((t, nb * cb), F32)],
        compiler_params=_cp(("arbitrary",), 48),
    )(x, vec, win)


def _conv_fwd(proj, cw32, name):
    t = proj.shape[0]
    nblk = cw32.shape[1] // LANES
    ch = min(t, 128)

    def body(val_ref, gate_ref, cw_ref, cv_ref, ext_ref):
        ext_ref[0:32, :] = jnp.zeros((32, LANES), F32)
        ext_ref[32:, :] = val_ref[...] * _sigmoid(gate_ref[...])
        for r in range(t // ch):
            acc = jnp.broadcast_to(cw_ref[31:32, :], (ch, LANES))
            for k in range(CONV_W):
                off = 32 + r * ch - (CONV_W - 1 - k)
                acc = acc + cw_ref[k:k + 1, :] * ext_ref[off:off + ch, :]
            cv_ref[r * ch:(r + 1) * ch, :] = acc

    return pl.pallas_call(
        body, name=name, grid=(nblk,),
        in_specs=[pl.BlockSpec((t, LANES), lambda c: (0, c)),
                  pl.BlockSpec((t, LANES), lambda c: (0, nblk + c)),
                  pl.BlockSpec((32, LANES), lambda c: (0, c))],
        out_specs=pl.BlockSpec((t, LANES), lambda c: (0, c)),
        out_shape=jax.ShapeDtypeStruct((t, nblk * LANES), F32),
        scratch_shapes=[pltpu.VMEM((t + 32, LANES), F32)],
        compiler_params=_cp(("arbitrary",), 48),
    )(proj, proj, cw32)


def _rnn_fwd(proj, rw8, rvec, wab, wib, name):
    t = proj.shape[0]
    nblk = rvec.shape[1] // LANES

    def body(ux_ref, uy_ref, rw_ref, rvec_ref, wa_ref, wi_ref, h_ref, yr_ref, ext_ref):
        xr = _rnn_conv(ux_ref[...], rw_ref, rvec_ref, ext_ref)
        _, _, ig, _, a, mult = _rglru_gates(xr, wa_ref, wi_ref, rvec_ref)
        h = _scan_fwd(a, mult * (ig * xr))
        h_ref[...] = h
        ge, _ = _gelu_and_grad(uy_ref[...])
        yr_ref[...] = (ge * h).astype(BF16)

    blk = lambda off: pl.BlockSpec((t, LANES), lambda c: (0, off + c))
    return pl.pallas_call(
        body, name=name, grid=(nblk,),
        in_specs=[blk(2 * nblk), blk(3 * nblk),
                  pl.BlockSpec((8, LANES), lambda c: (0, c)),
                  pl.BlockSpec((8, LANES), lambda c: (0, c)),
                  pl.BlockSpec((None, LANES, LANES), lambda c: (c, 0, 0)),
                  pl.BlockSpec((None, LANES, LANES), lambda c: (c, 0, 0))],
        out_specs=[blk(0), blk(0)],
        out_shape=[jax.ShapeDtypeStruct((t, nblk * LANES), F32), jax.ShapeDtypeStruct((t, nblk * LANES), BF16)],
        scratch_shapes=[pltpu.VMEM((t + 8, LANES), F32)],
        compiler_params=_cp(("arbitrary",), 56),
    )(proj, proj, rw8, rvec, wab, wib)


def _ln_silu(cv, lnv_ref):
    mu = jnp.mean(cv, axis=-1, keepdims=True)
    xc = cv - mu
    rs = lax.rsqrt(jnp.mean(xc * xc, axis=-1, keepdims=True) + EPS)
    chat = xc * rs
    z = chat * lnv_ref[0:1, :] + lnv_ref[1:2, :]
    sg = _sigmoid(z)
    return rs, chat, z, sg


def _mix_out(x, cv, yr, vec, lnv, wout, tm, name):
    t, d = x.shape
    dc = cv.shape[1]

    def body(x_ref, cv_ref, yr_ref, vec_ref, lnv_ref, w_ref, xo_ref, ym_ref, yc_ref):
        _, _, z, sg = _ln_silu(cv_ref[...], lnv_ref)
        yc = (z * sg).astype(BF16)
        yr = yr_ref[...]
        yc_ref[:, 0:dc] = yc
        yc_ref[:, dc:] = yr
        ym = _dot(yc, w_ref[0:dc, :]) + _dot(yr, w_ref[dc:, :])
        ym_ref[...] = ym.astype(BF16)
        xo_ref[...] = x_ref[...] + vec_ref[3:4, :] * ym

    return pl.pallas_call(
        body, name=name, grid=(t // tm,),
        in_specs=[pl.BlockSpec((tm, d), lambda i: (i, 0)),
                  pl.BlockSpec((tm, dc), lambda i: (i, 0)),
                  pl.BlockSpec((tm, dc), lambda i: (i, 0)),
                  pl.BlockSpec((8, d), lambda i: (0, 0)),
                  pl.BlockSpec((8, dc), lambda i: (0, 0)),
                  pl.BlockSpec((d, d), lambda i: (0, 0))],
        out_specs=[pl.BlockSpec((tm, d), lambda i: (i, 0)),
                   pl.BlockSpec((tm, d), lambda i: (i, 0)),
                   pl.BlockSpec((tm, d), lambda i: (i, 0))],
        out_shape=[jax.ShapeDtypeStruct((t, d), F32), jax.ShapeDtypeStruct((t, d), BF16),
                   jax.ShapeDtypeStruct((t, d), BF16)],
        compiler_params=_cp(("arbitrary",), 48),
    )(x, cv, yr, vec, lnv, wout)


def _final(x, tgt, vec, tm, name):
    t, d = x.shape
    nt = t // tm

    def body(x_ref, t_ref, vec_ref, dx_ref, dvec_ref):
        i = pl.program_id(0)

        @pl.when(i == 0)
        def _():
            dvec_ref[...] = jnp.zeros_like(dvec_ref)

        xv = x_ref[...]
        e = _adaln(xv, vec_ref) - t_ref[...]
        dvec_ref[4:5, :] += (0.5 / d) * jnp.sum(e * e, axis=0, keepdims=True)
        dx_ref[...] = _adaln_bwd(xv, e * (1.0 / d), vec_ref, dvec_ref)

        @pl.when(i == nt - 1)
        def _():
            _adaln_finish(vec_ref, dvec_ref)

    return pl.pallas_call(
        body, name=name, grid=(nt,),
        in_specs=[pl.BlockSpec((tm, d), lambda i: (i, 0)),
                  pl.BlockSpec((tm, d), lambda i: (i, 0)),
                  pl.BlockSpec((8, d), lambda i: (0, 0))],
        out_specs=[pl.BlockSpec((tm, d), lambda i: (i, 0)),
                   pl.BlockSpec((8, d), lambda i: (0, 0))],
        out_shape=[jax.ShapeDtypeStruct((t, d), F32), jax.ShapeDtypeStruct((8, d), F32)],
        compiler_params=_cp(("arbitrary",), 48),
    )(x, tgt, vec)


def _ffn_bwd_w(dxo, vec, gu, f, h, wo, tm, name, host=None):
    t, d = dxo.shape
    nj, fb = gu.shape[1], gu.shape[3]
    nt = t // tm

    def body(ins, outs, scr):
        dxo_ref, vec_ref, gu_ref, f_ref, h_ref, wo_ref = ins
        dgu_ref, dwi_ref, dwo_ref, dvec_ref = outs
        accg_ref, accu_ref, acco_ref = scr
        j = pl.program_id(0)
        i = pl.program_id(1)

        @pl.when((i == 0) & (j == 0))
        def _():
            dvec_ref[...] = jnp.zeros_like(dvec_ref)

        @pl.when(i == 0)
        def _():
            accg_ref[...] = jnp.zeros_like(accg_ref)
            accu_ref[...] = jnp.zeros_like(accu_ref)
            acco_ref[...] = jnp.zeros_like(acco_ref)

        dxo_v = dxo_ref[...]
        df = (0.5 * vec_ref[3:4, :] * dxo_v).astype(BF16)

        @pl.when(j == 0)
        def _():
            dvec_ref[2:3, :] += 0.5 * jnp.sum(dxo_v * f_ref[...].astype(F32), axis=0, keepdims=True)

        dact = _dot_nt(df, wo_ref[...])
        g = gu_ref[0].astype(F32)
        u = gu_ref[1].astype(F32)
        sg = _sigmoid(g)
        sl = g * sg
        dgate = (dact * u * (sg * (1.0 + g * (1.0 - sg)))).astype(BF16)
        dup = (dact * sl).astype(BF16)
        act = (sl * u).astype(BF16)
        dgu_ref[0] = dgate
        dgu_ref[1] = dup
        hb = h_ref[...]
        acco_ref[...] += _dot_tn(act, df)
        accg_ref[...] += _dot_tn(dgate, hb)
        accu_ref[...] += _dot_tn(dup, hb)

        @pl.when(i == nt - 1)
        def _():
            dwi_ref[0] = accg_ref[...].astype(BF16)
            dwi_ref[1] = accu_ref[...].astype(BF16)
            dwo_ref[...] = acco_ref[...].astype(BF16)

    tile = pl.BlockSpec((tm, d), lambda j, i: (i, 0))
    return _hosted_call(
        body, host, name, (nj, nt),
        [tile,
         pl.BlockSpec((8, d), lambda j, i: (0, 0)),
         pl.BlockSpec((2, None, tm, fb), lambda j, i: (0, j, i, 0)),
         pl.BlockSpec((tm, d), lambda j, i: (jnp.where(j == 0, i, 0), 0)),
         tile,
         pl.BlockSpec((fb, d), lambda j, i: (j, 0))],
        [pl.BlockSpec((2, None, tm, fb), lambda j, i: (0, j, i, 0)),
         pl.BlockSpec((2, None, fb, d), lambda j, i: (0, j, 0, 0)),
         pl.BlockSpec((None, fb, d), lambda j, i: (j, 0, 0)),
         pl.BlockSpec((8, d), lambda j, i: (0, 0))],
        [jax.ShapeDtypeStruct((2, nj, t, fb), BF16), jax.ShapeDtypeStruct((2, nj, fb, d), BF16),
         jax.ShapeDtypeStruct((nj, fb, d), BF16), jax.ShapeDtypeStruct((8, d), F32)],
        [pltpu.VMEM((fb, d), F32), pltpu.VMEM((fb, d), F32), pltpu.VMEM((fb, d), F32)],
        ("arbitrary", "arbitrary"), 56, (dxo, vec, gu, f, h, wo))


def _ffn_bwd_in(dxo, x, vec, dgu, wi, tm, name, host=None):
    t, d = x.shape
    nj, fb = wi.shape[1], wi.shape[2]
    nt = t // tm

    def body(ins, outs, scr):
        dxo_ref, x_ref, vec_ref, dgu_ref, wi_ref = ins
        dx_ref, dvec_ref = outs
        i = pl.program_id(0)

        @pl.when(i == 0)
        def _():
            dvec_ref[...] = jnp.zeros_like(dvec_ref)

        dh = jnp.zeros((tm, d), F32)
        for a in range(2):
            for k in range(nj):
                dh = dh + _dot(dgu_ref[a, k], wi_ref[a, k])
        dx_ref[...] = dxo_ref[...] + _adaln_bwd(x_ref[...], dh, vec_ref, dvec_ref)

        @pl.when(i == nt - 1)
        def _():
            _adaln_finish(vec_ref, dvec_ref)

    tile = pl.BlockSpec((tm, d), lambda i: (i, 0))
    return _hosted_call(
        body, host, name, (nt,),
        [tile, tile,
         pl.BlockSpec((8, d), lambda i: (0, 0)),
         pl.BlockSpec((2, nj, tm, fb), lambda i: (0, 0, i, 0)),
         pl.BlockSpec((2, nj, fb, d), lambda i: (0, 0, 0, 0))],
        [tile, pl.BlockSpec((8, d), lambda i: (0, 0))],
        [jax.ShapeDtypeStruct((t, d), F32), jax.ShapeDtypeStruct((8, d), F32)],
        [], ("arbitrary",), 60, (dxo, x, vec, dgu, wi))


def _mm_tn(a, b, a_spec, b_spec, nblk, nk, m, n, name):
    def body(a_ref, b_ref, o_ref, acc_ref):
        s = pl.program_id(1)

        @pl.when(s == 0)
        def _():
            acc_ref[...] = jnp.zeros_like(acc_ref)

        acc_ref[...] += _dot_tn(a_ref[...], b_ref[...])

        @pl.when(s == nk - 1)
        def _():
            o_ref[...] = acc_ref[...].astype(BF16)

    return pl.pallas_call(
        body, name=name, grid=(nblk, nk),
        in_specs=[a_spec, b_spec],
        out_specs=pl.BlockSpec((None, m, n), lambda k, s: (k, 0, 0)),
        out_shape=jax.ShapeDtypeStruct((nblk, m, n), BF16),
        scratch_shapes=[pltpu.VMEM((m, n), F32)],
        compiler_params=_cp(("arbitrary", "arbitrary"), 56),
    )(a, b)


def _wgrad_in(h, parts, cb, tk, name):
    t, d = h.shape
    per = parts[0].shape[1] // cb
    nblk = len(parts) * per
    nk = t // tk

    def body(h_ref, p0, p1, p2, p3, o_ref, acc_ref):
        k = pl.program_id(0)
        s = pl.program_id(1)

        @pl.when(s == 0)
        def _():
            acc_ref[...] = jnp.zeros_like(acc_ref)

        for p, p_ref in enumerate((p0, p1, p2, p3)):
            @pl.when(k // per == p)
            def _(p_ref=p_ref):
                acc_ref[...] += _dot_tn(h_ref[...], p_ref[...])

        @pl.when(s == nk - 1)
        def _():
            o_ref[...] = acc_ref[...].astype(BF16)

    def part_spec(p):
        return pl.BlockSpec((tk, cb), lambda k, s: (jnp.where(k // per == p, s, 0), jnp.where(k // per == p, k % per, 0)))

    return pl.pallas_call(
        body, name=name, grid=(nblk, nk),
        in_specs=[pl.BlockSpec((tk, d), lambda k, s: (s, 0))] + [part_spec(p) for p in range(len(parts))],
        out_specs=pl.BlockSpec((None, d, cb), lambda k, s: (k, 0, 0)),
        out_shape=jax.ShapeDtypeStruct((nblk, d, cb), BF16),
        scratch_shapes=[pltpu.VMEM((d, cb), F32)],
        compiler_params=_cp(("arbitrary", "arbitrary"), 48),
    )(h, *parts)


def _mixout_bwd(dxo, ym, cv, hr, proj, vec, lnv, wout, tm, name, host=None):
    t, d = dxo.shape
    dc = cv.shape[1]
    nt = t // tm

    def body(ins, outs, scr):
        dxo_ref, ym_ref, cv_ref, hr_ref, uy_ref, vec_ref, lnv_ref, w_ref = ins
        dym_ref, dcv_ref, dhr_ref, duy_ref, dln_ref, dgt_ref = outs
        i = pl.program_id(0)

        @pl.when(i == 0)
        def _():
            dln_ref[...] = jnp.zeros_like(dln_ref)
            dgt_ref[...] = jnp.zeros_like(dgt_ref)

        dxo_v = dxo_ref[...]
        dym = (vec_ref[3:4, :] * dxo_v).astype(BF16)
        dym_ref[...] = dym
        dgt_ref[0:1, :] += jnp.sum(dxo_v * ym_ref[...].astype(F32), axis=0, keepdims=True)
        dyc = _dot_nt(dym, w_ref[0:dc, :])
        dyr = _dot_nt(dym, w_ref[dc:, :])
        rs, chat, z, sg = _ln_silu(cv_ref[...], lnv_ref)
        dz = dyc * (sg * (1.0 + z * (1.0 - sg)))
        dln_ref[0:1, :] += jnp.sum(dz * chat, axis=0, keepdims=True)
        dln_ref[1:2, :] += jnp.sum(dz, axis=0, keepdims=True)
        dchat = dz * lnv_ref[0:1, :]
        dcv_ref[...] = rs * (dchat - jnp.mean(dchat, axis=-1, keepdims=True)
                             - chat * jnp.mean(dchat * chat, axis=-1, keepdims=True))
        ge, dge = _gelu_and_grad(uy_ref[...])
        dhr_ref[...] = dyr * ge
        duy_ref[...] = (dyr * hr_ref[...] * dge).astype(BF16)

    tile_d = pl.BlockSpec((tm, d), lambda i: (i, 0))
    tile_c = pl.BlockSpec((tm, dc), lambda i: (i, 0))
    return _hosted_call(
        body, host, name, (nt,),
        [tile_d, tile_d, tile_c, tile_c,
         pl.BlockSpec((tm, dc), lambda i: (i, 3)),
         pl.BlockSpec((8, d), lambda i: (0, 0)),
         pl.BlockSpec((8, dc), lambda i: (0, 0)),
         pl.BlockSpec((d, d), lambda i: (0, 0))],
        [tile_d, tile_c, tile_c, tile_c,
         pl.BlockSpec((8, dc), lambda i: (0, 0)),
         pl.BlockSpec((8, d), lambda i: (0, 0))],
        [jax.ShapeDtypeStruct((t, d), BF16), jax.ShapeDtypeStruct((t, dc), F32),
         jax.ShapeDtypeStruct((t, dc), F32), jax.ShapeDtypeStruct((t, dc), BF16),
         jax.ShapeDtypeStruct((8, dc), F32), jax.ShapeDtypeStruct((8, d), F32)],
        [], ("arbitrary",), 48, (dxo, ym, cv, hr, proj, vec, lnv, wout))


def _conv_bwd(proj, dcv, cw32, name):
    t = proj.shape[0]
    nblk = cw32.shape[1] // LANES
    ch = min(t, 128)

    def body(val_ref, gate_ref, dcv_ref, cw_ref, dval_ref, dgate_ref, dcw_ref, extu_ref, extd_ref):
        val = val_ref[...]
        sg = _sigmoid(gate_ref[...])
        extu_ref[0:32, :] = jnp.zeros((32, LANES), F32)
        extu_ref[32:, :] = val * sg
        dcv_v = dcv_ref[...]
        extd_ref[0:t, :] = dcv_v
        extd_ref[t:, :] = jnp.zeros((32, LANES), F32)
        for r in range(t // ch):
            acc = jnp.zeros((ch, LANES), F32)
            for k in range(CONV_W):
                off = r * ch + (CONV_W - 1 - k)
                acc = acc + cw_ref[k:k + 1, :] * extd_ref[off:off + ch, :]
            rows = slice(r * ch, (r + 1) * ch)
            sg_r = _sigmoid(gate_ref[rows, :])
            dval_ref[rows, :] = (acc * sg_r).astype(BF16)
            dgate_ref[rows, :] = (acc * val_ref[rows, :] * sg_r * (1.0 - sg_r)).astype(BF16)
        for k in range(CONV_W):
            off = 32 - (CONV_W - 1 - k)
            dcw_ref[k:k + 1, :] = jnp.sum(dcv_v * extu_ref[off:off + t, :], axis=0, keepdims=True)
        dcw_ref[31:32, :] = jnp.sum(dcv_v, axis=0, keepdims=True)

    blk = lambda off: pl.BlockSpec((t, LANES), lambda c: (0, off + c))
    return pl.pallas_call(
        body, name=name, grid=(nblk,),
        in_specs=[blk(0), blk(nblk), blk(0), pl.BlockSpec((32, LANES), lambda c: (0, c))],
        out_specs=[blk(0), blk(0), pl.BlockSpec((32, LANES), lambda c: (0, c))],
        out_shape=[jax.ShapeDtypeStruct((t, nblk * LANES), BF16), jax.ShapeDtypeStruct((t, nblk * LANES), BF16),
                   jax.ShapeDtypeStruct((32, nblk * LANES), F32)],
        scratch_shapes=[pltpu.VMEM((t + 32, LANES), F32), pltpu.VMEM((t + 32, LANES), F32)],
        compiler_params=_cp(("arbitrary",), 56),
    )(proj, proj, dcv, cw32)


def _rnn_bwd(proj, hr, dhr, rw8, rvec, wab, wib, name, host=None):
    t = proj.shape[0]
    nblk = rvec.shape[1] // LANES

    def body(ins, outs, scr):
        ux_ref, h_ref, dh_ref, rw_ref, rvec_ref, wa_ref, wi_ref = ins
        dux_ref, sm_ref, dwa_ref, dwi_ref = outs
        ext_ref, extd_ref = scr
        xr = _rnn_conv(ux_ref[...], rw_ref, rvec_ref, ext_ref)
        xb, r, ig, ls, a, mult = _rglru_gates(xr, wa_ref, wi_ref, rvec_ref)
        row = lax.broadcasted_iota(jnp.int32, (t, LANES), 0)
        a_next = jnp.where(row < t - 1, pltpu.roll(a, t - 1, 0), 0.0)
        g = _scan_rev(a_next, dh_ref[...])
        hprev = jnp.where(row >= 1, pltpu.roll(h_ref[...], 1, 0), 0.0)
        da = g * hprev
        dmult = g * (ig * xr)
        dig = g * mult * xr
        dxr = g * mult * ig
        dlog_a = a * (da - dmult * a / mult)
        dr = dlog_a * (RG_C * ls)
        dls = RG_C * jnp.sum(dlog_a * r, axis=0, keepdims=True)
        dpr = dr * r * (1.0 - r)
        dpi = dig * ig * (1.0 - ig)
        dprb = dpr.astype(BF16)
        dpib = dpi.astype(BF16)
        dxr = dxr + _dot_nt(dprb, wa_ref[...]) + _dot_nt(dpib, wi_ref[...])
        dwa_ref[...] = _dot_tn(xb, dprb)
        dwi_ref[...] = _dot_tn(xb, dpib)
        extd_ref[0:t, :] = dxr
        extd_ref[t:, :] = jnp.zeros((8, LANES), F32)
        dux = rw_ref[RNN_CONV_W - 1:RNN_CONV_W, :] * dxr
        for k in range(RNN_CONV_W - 1):
            d = RNN_CONV_W - 1 - k
            dux = dux + rw_ref[k:k + 1, :] * extd_ref[d:d + t, :]
        dux_ref[...] = dux.astype(BF16)
        for k in range(RNN_CONV_W):
            d = RNN_CONV_W - 1 - k
            sm_ref[k:k + 1, :] = jnp.sum(dxr * ext_ref[8 - d:8 - d + t, :], axis=0, keepdims=True)
        sm_ref[4:5, :] = jnp.sum(dxr, axis=0, keepdims=True)
        sm_ref[5:6, :] = jnp.sum(dpr, axis=0, keepdims=True)
        sm_ref[6:7, :] = jnp.sum(dpi, axis=0, keepdims=True)
        sm_ref[7:8, :] = dls * _sigmoid(-rvec_ref[3:4, :])

    blk = lambda off: pl.BlockSpec((t, LANES), lambda c: (0, off + c))
    sq = pl.BlockSpec((None, LANES, LANES), lambda c: (c, 0, 0))
    return _hosted_call(
        body, host, name, (nblk,),
        [blk(2 * nblk), blk(0), blk(0),
         pl.BlockSpec((8, LANES), lambda c: (0, c)),
         pl.BlockSpec((8, LANES), lambda c: (0, c)), sq, sq],
        [blk(0), pl.BlockSpec((8, LANES), lambda c: (0, c)), sq, sq],
        [jax.ShapeDtypeStruct((t, nblk * LANES), BF16), jax.ShapeDtypeStruct((8, nblk * LANES), F32),
         jax.ShapeDtypeStruct((nblk, LANES, LANES), F32), jax.ShapeDtypeStruct((nblk, LANES, LANES), F32)],
        [pltpu.VMEM((t + 8, LANES), F32), pltpu.VMEM((t + 8, LANES), F32)],
        ("arbitrary",), 60, (proj, hr, dhr, rw8, rvec, wab, wib))


def _mixin_bwd(dxo, x, parts, vec, win, tm, name):
    t, d = x.shape
    nb, _, cb = win.shape
    dc = parts[0].shape[1]
    per = dc // cb
    nt = t // tm

    def body(dxo_ref, x_ref, p0, p1, p2, p3, vec_ref, w_ref, dx_ref, dvec_ref):
        i = pl.program_id(0)

        @pl.when(i == 0)
        def _():
            dvec_ref[...] = jnp.zeros_like(dvec_ref)

        prefs = (p0, p1, p2, p3)
        dh = jnp.zeros((tm, d), F32)
        for k in range(nb):
            dh = dh + _dot_nt(prefs[k // per][:, (k % per) * cb:(k % per + 1) * cb], w_ref[k])
        dx_ref[...] = dxo_ref[...] + _adaln_bwd(x_ref[...], dh, vec_ref, dvec_ref)

        @pl.when(i == nt - 1)
        def _():
            _adaln_finish(vec_ref, dvec_ref)

    tile_d = pl.BlockSpec((tm, d), lambda i: (i, 0))
    tile_c = pl.BlockSpec((tm, dc), lambda i: (i, 0))
    return pl.pallas_call(
        body, name=name, grid=(nt,),
        in_specs=[tile_d, tile_d, tile_c, tile_c, tile_c, tile_c,
                  pl.BlockSpec((8, d), lambda i: (0, 0)),
                  pl.BlockSpec((nb, d, cb), lambda i: (0, 0, 0))],
        out_specs=[tile_d, pl.BlockSpec((8, d), lambda i: (0, 0))],
        out_shape=[jax.ShapeDtypeStruct((t, d), F32), jax.ShapeDtypeStruct((8, d), F32)],
        compiler_params=_cp(("arbitrary",), 48),
    )(dxo, x, *parts, vec, win)


def _coords():
    return lax.axis_index("x"), lax.axis_index("y"), lax.axis_index("c")


def _flip(v, bit):
    return 1 - v if bit else v


def _gather_copy(outs, send_sems, recv_sems, a, k, block, to, src=None):
    dst = outs[a].at[block]
    return pltpu.make_async_remote_copy(
        src_ref=dst if src is None else src, dst_ref=dst,
        send_sem=send_sems.at[a, k], recv_sem=recv_sems.at[a, k],
        device_id=to, device_id_type=MESH_IDS)


def _gather_start(ins, outs, send_sems, recv_sems, loc_sems):
    x, y, c = _coords()
    me = 4 * x + 2 * y + c
    for a in range(len(ins)):
        pltpu.make_async_copy(ins[a], outs[a].at[me], loc_sems.at[a]).start()
    for a in range(len(ins)):
        _gather_copy(outs, send_sems, recv_sems, a, 0, me, (x, y, 1 - c), src=ins[a]).start()
        for j, (cx, cy) in enumerate([(1 - x, y), (x, 1 - y), (1 - x, 1 - y)]):
            _gather_copy(outs, send_sems, recv_sems, a, 1 + j, me, (cx, cy, c), src=ins[a]).start()


def _gather_finish(ins, outs, send_sems, recv_sems, loc_sems):
    x, y, c = _coords()
    me = 4 * x + 2 * y + c
    sib = (x, y, 1 - c)
    chips = [(1 - x, y), (x, 1 - y), (1 - x, 1 - y)]
    n = len(ins)
    for a in range(n):
        for j, (cx, cy) in enumerate(chips):
            blk = 4 * cx + 2 * cy + c
            _gather_copy(outs, send_sems, recv_sems, a, 1 + j, blk, sib).wait_recv()
            _gather_copy(outs, send_sems, recv_sems, a, 4 + j, blk, sib).start()
    for a in range(n):
        _gather_copy(outs, send_sems, recv_sems, a, 0, 4 * x + 2 * y + (1 - c), sib).wait_recv()
        for j, (cx, cy) in enumerate(chips):
            _gather_copy(outs, send_sems, recv_sems, a, 4 + j, 4 * cx + 2 * cy + (1 - c), sib).wait_recv()
    for a in range(n):
        _gather_copy(outs, send_sems, recv_sems, a, 0, me, sib, src=ins[a]).wait_send()
        for j, (cx, cy) in enumerate(chips):
            _gather_copy(outs, send_sems, recv_sems, a, 1 + j, me, (cx, cy, c), src=ins[a]).wait_send()
            _gather_copy(outs, send_sems, recv_sems, a, 4 + j, 4 * cx + 2 * cy + c, sib).wait_send()
        pltpu.make_async_copy(ins[a], outs[a].at[me], loc_sems.at[a]).wait()


def _gather_shapes(shards):
    return [jax.ShapeDtypeStruct((NDEV,) + s.shape, s.dtype) for s in shards]


def _gather_sems(n):
    return [pltpu.SemaphoreType.DMA((n, 7)), pltpu.SemaphoreType.DMA((n, 7)), pltpu.SemaphoreType.DMA((n,))]


def _sibling_copies(ins, outs, send_sems, recv_sems):
    x, y, c = _coords()
    return [pltpu.make_async_remote_copy(
        src_ref=ins[a].at[2 * q + (1 - c)], dst_ref=outs[a].at[q],
        send_sem=send_sems.at[a, q], recv_sem=recv_sems.at[a, q],
        device_id=(x, y, 1 - c), device_id_type=MESH_IDS) for a in range(len(ins)) for q in range(4)]


def _sibling_shapes(parts):
    return [jax.ShapeDtypeStruct((4,) + p.shape[1:], p.dtype) for p in parts]


def _chips_copies(ins, outs, send_sems, recv_sems):
    x, y, c = _coords()
    copies = []
    for a in range(len(ins)):
        for k, (kx, ky) in enumerate([(1, 0), (0, 1), (1, 1)]):
            tx, ty = _flip(x, kx), _flip(y, ky)
            copies.append(pltpu.make_async_remote_copy(
                src_ref=ins[a].at[2 * tx + ty], dst_ref=outs[a].at[k],
                send_sem=send_sems.at[a, k], recv_sem=recv_sems.at[a, k],
                device_id=(tx, ty, c), device_id_type=MESH_IDS))
    return copies


def _chips_shapes(sums):
    return [jax.ShapeDtypeStruct((3,) + s.shape[1:], s.dtype) for s in sums]


class _Exchange:
    def __init__(self, kind, arrays):
        self.kind, self.arrays, self.n = kind, list(arrays), len(arrays)

    def out_shapes(self):
        return {"gather": _gather_shapes, "sibling": _sibling_shapes, "chips": _chips_shapes}[self.kind](self.arrays)

    def sems(self):
        if self.kind == "gather":
            return _gather_sems(self.n)
        k = 4 if self.kind == "sibling" else 3
        return [pltpu.SemaphoreType.DMA((self.n, k)), pltpu.SemaphoreType.DMA((self.n, k))]

    def _copies(self, ins, outs, sems):
        return (_sibling_copies if self.kind == "sibling" else _chips_copies)(ins, outs, *sems)

    def start(self, ins, outs, sems):
        if self.kind == "gather":
            _gather_start(ins, outs, *sems)
        else:
            for cpy in self._copies(ins, outs, sems):
                cpy.start()

    def finish(self, ins, outs, sems):
        if self.kind == "gather":
            _gather_finish(ins, outs, *sems)
        else:
            for cpy in self._copies(ins, outs, sems):
                cpy.wait()


def _hosted_call(body, host, name, grid, in_specs, out_specs, out_shape, scratch, sem, vmem_mb, args):
    n = host.n if host else 0
    ni, no, ns = len(in_specs), len(out_specs), len(scratch)

    def full(*refs):
        ins, h_in = refs[:ni], refs[ni:ni + n]
        outs, h_out = refs[ni + n:ni + n + no], refs[ni + n + no:ni + 2 * n + no]
        scr, sems = refs[ni + 2 * n + no:ni + 2 * n + no + ns], refs[ni + 2 * n + no + ns:]
        if host:
            first = functools.reduce(lambda a, b: a & b, [pl.program_id(k) == 0 for k in range(len(grid))])
            last = functools.reduce(lambda a, b: a & b, [pl.program_id(k) == g - 1 for k, g in enumerate(grid)])

            @pl.when(first)
            def _():
                host.start(h_in, h_out, sems)

        body(ins, outs, scr)

        if host:
            @pl.when(last)
            def _():
                host.finish(h_in, h_out, sems)

    anyspec = pl.BlockSpec(memory_space=pl.ANY)
    return pl.pallas_call(
        full, name=name, grid=grid,
        in_specs=list(in_specs) + [anyspec] * n, out_specs=list(out_specs) + [anyspec] * n,
        out_shape=list(out_shape) + (host.out_shapes() if host else []),
        scratch_shapes=list(scratch) + (host.sems() if host else []),
        compiler_params=_cp(sem, vmem_mb),
    )(*args, *(host.arrays if host else []))


def _exchange(host, name):
    def body(*refs):
        n = host.n
        host.start(refs[:n], refs[n:2 * n], refs[2 * n:])
        host.finish(refs[:n], refs[n:2 * n], refs[2 * n:])

    anyspec = pl.BlockSpec(memory_space=pl.ANY)
    return pl.pallas_call(
        body, name=name, in_specs=[anyspec] * host.n, out_specs=[anyspec] * host.n,
        out_shape=host.out_shapes(), scratch_shapes=host.sems(),
    )(*host.arrays)


def _chip_sum(part, recv, sel, tr, name):
    _, _, r, c = part.shape

    def body(sel_ref, p_ref, r_ref, cs_ref, own_ref):
        q = pl.program_id(1)
        s = p_ref[...].astype(F32) + r_ref[...].astype(F32)
        cs_ref[...] = s.astype(BF16)

        @pl.when(q == sel_ref[1])
        def _():
            own_ref[...] = s

    return pl.pallas_call(
        body, name=name,
        grid_spec=pltpu.PrefetchScalarGridSpec(
            num_scalar_prefetch=1, grid=(r // tr, 4),
            in_specs=[pl.BlockSpec((None, None, tr, c), lambda i, q, s: (q, s[0], i, 0)),
                      pl.BlockSpec((None, tr, c), lambda i, q, s: (q, i, 0))],
            out_specs=[pl.BlockSpec((None, tr, c), lambda i, q, s: (q, i, 0)),
                       pl.BlockSpec((tr, c), lambda i, q, s: (i, 0))]),
        out_shape=[jax.ShapeDtypeStruct((4, r, c), BF16), jax.ShapeDtypeStruct((r, c), F32)],
        compiler_params=_cp(("arbitrary", "arbitrary"), 48),
    )(sel, part, recv)


def _gather_direct(src_ref, buf_ref, send_sems, recv_sems):
    x, y, c = _coords()
    me = 4 * x + 2 * y + c
    buf_ref[me] = src_ref[...]
    copies = []
    for k in range(1, NDEV):
        kx, ky, kc = (k >> 2) & 1, (k >> 1) & 1, k & 1
        copies.append(pltpu.make_async_remote_copy(
            src_ref=src_ref, dst_ref=buf_ref.at[me],
            send_sem=send_sems.at[k - 1], recv_sem=recv_sems.at[k - 1],
            device_id=(_flip(x, kx), _flip(y, ky), _flip(c, kc)), device_id_type=MESH_IDS))
    for cpy in copies:
        cpy.start()
    for k in range(1, NDEV):
        kx, ky, kc = (k >> 2) & 1, (k >> 1) & 1, k & 1
        peer = 4 * _flip(x, kx) + 2 * _flip(y, ky) + _flip(c, kc)
        pltpu.make_async_remote_copy(
            src_ref=src_ref, dst_ref=buf_ref.at[peer],
            send_sem=send_sems.at[k - 1], recv_sem=recv_sems.at[k - 1],
            device_id=(x, y, c), device_id_type=MESH_IDS).wait_recv()
    for cpy in copies:
        cpy.wait_send()
    return me


def _mod_exchange(c_row, wmod, bmod, wfmod, bfmod, name):
    d = c_row.shape[1]
    nm, nf = wmod.shape[1], wfmod.shape[1]
    nw = nm + nf

    def body(c_ref, wm_ref, bm_ref, wf_ref, bf_ref, cs_ref, mod_ref, fmod_ref,
             slab_ref, csbuf_ref, mslab_ref, mbuf_ref, s1, r1, s2, r2):
        cv = c_ref[...]
        slab_ref[...] = jnp.broadcast_to(cv * _sigmoid(cv), (8, d))
        _gather_direct(slab_ref, csbuf_ref, s1, r1)
        for b in range(NDEV):
            cs_ref[b:b + 1, :] = csbuf_ref[b, 0:1, :]
        cs = cs_ref[...]
        mslab_ref[:, 0:nm] = jnp.dot(cs, wm_ref[...], precision=HI, preferred_element_type=F32) + bm_ref[...]
        mslab_ref[:, nm:] = jnp.dot(cs, wf_ref[...], precision=HI, preferred_element_type=F32) + bf_ref[...]
        me = _gather_direct(mslab_ref, mbuf_ref, s2, r2)
        mine = lax.broadcasted_iota(jnp.int32, (8, nw), 0) == me
        for k in range(NDEV):
            rowk = jnp.sum(jnp.where(mine, mbuf_ref[k], 0.0), axis=0, keepdims=True)
            mod_ref[k:k + 1, :] = rowk[:, 0:nm]
            fmod_ref[k:k + 1, :] = rowk[:, nm:]

    vm = pl.BlockSpec(memory_space=pltpu.VMEM)
    return pl.pallas_call(
        body, name=name,
        in_specs=[vm] * 5, out_specs=[vm] * 3,
        out_shape=[jax.ShapeDtypeStruct((NDEV, d), F32), jax.ShapeDtypeStruct((NDEV, nm), F32),
                   jax.ShapeDtypeStruct((NDEV, nf), F32)],
        scratch_shapes=[pltpu.VMEM((8, d), F32), pltpu.VMEM((NDEV, 8, d), F32),
                        pltpu.VMEM((8, nw), F32), pltpu.VMEM((NDEV, 8, nw), F32),
                        pltpu.SemaphoreType.DMA((7,)), pltpu.SemaphoreType.DMA((7,)),
                        pltpu.SemaphoreType.DMA((7,)), pltpu.SemaphoreType.DMA((7,))],
        compiler_params=pltpu.CompilerParams(vmem_limit_bytes=40 * 1024 * 1024),
    )(c_row, wmod, bmod, wfmod, bfmod)


def _small_exchange(pack, name):
    def body(p_ref, all_ref, sum_ref, s, r):
        _gather_direct(p_ref, all_ref, s, r)
        tot = all_ref[0]
        for k in range(1, NDEV):
            tot = tot + all_ref[k]
        sum_ref[...] = tot

    vm = pl.BlockSpec(memory_space=pltpu.VMEM)
    return pl.pallas_call(
        body, name=name,
        in_specs=[vm], out_specs=[vm, vm],
        out_shape=[jax.ShapeDtypeStruct((NDEV,) + pack.shape, F32), jax.ShapeDtypeStruct(pack.shape, F32)],
        scratch_shapes=[pltpu.SemaphoreType.DMA((7,)), pltpu.SemaphoreType.DMA((7,))],
    )(pack)


def _adamw_math(w, g, m, v):
    m = ADAM_B1 * m + (1.0 - ADAM_B1) * g
    v = ADAM_B2 * v + (1.0 - ADAM_B2) * (g * g)
    m_hat = m / (1.0 - ADAM_B1 ** ADAM_STEP)
    v_hat = v / (1.0 - ADAM_B2 ** ADAM_STEP)
    delta = -ADAM_LR * (m_hat / (jnp.sqrt(v_hat) + ADAM_EPS) + ADAM_WD * w)
    return delta, m, v


def _adamw_small(w, g, m, v, name):
    def body(w_ref, g_ref, m_ref, v_ref, d_ref, mo_ref, vo_ref):
        d_ref[...], mo_ref[...], vo_ref[...] = _adamw_math(w_ref[...], g_ref[...], m_ref[...], v_ref[...])

    vm = pl.BlockSpec(memory_space=pltpu.VMEM)
    sds = jax.ShapeDtypeStruct(w.shape, F32)
    return pl.pallas_call(body, name=name, in_specs=[vm] * 4, out_specs=[vm] * 3,
                          out_shape=[sds, sds, sds])(w, g, m, v)


def _rs_final(own, recv, w, m, v, tr, name):
    r, c = own.shape

    def body(o_ref, r_ref, w_ref, m_ref, v_ref, g_ref, d_ref, mo_ref, vo_ref):
        g = o_ref[...] + r_ref[0].astype(F32) + r_ref[1].astype(F32) + r_ref[2].astype(F32)
        g_ref[...] = g
        d_ref[...], mo_ref[...], vo_ref[...] = _adamw_math(w_ref[...], g, m_ref[...], v_ref[...])

    tile = pl.BlockSpec((tr, c), lambda i: (i, 0))
    sds = jax.ShapeDtypeStruct((r, c), F32)
    return pl.pallas_call(
        body, name=name, grid=(r // tr,),
        in_specs=[tile, pl.BlockSpec((3, tr, c), lambda i: (0, i, 0)), tile, tile, tile],
        out_specs=[tile] * 4, out_shape=[sds] * 4,
        compiler_params=_cp(("arbitrary",), 48),
    )(own, recv, w, m, v)


def _mod_weight_update(cs, dm, w, m, v, tr, name):
    r, c = w.shape

    def body(cs_ref, dm_ref, w_ref, m_ref, v_ref, g_ref, d_ref, mo_ref, vo_ref):
        g = lax.dot_general(cs_ref[...], dm_ref[...], (((0,), (0,)), ((), ())),
                            precision=HI, preferred_element_type=F32)
        g_ref[...] = g
        d_ref[...], mo_ref[...], vo_ref[...] = _adamw_math(w_ref[...], g, m_ref[...], v_ref[...])

    tile = pl.BlockSpec((tr, c), lambda i: (i, 0))
    sds = jax.ShapeDtypeStruct((r, c), F32)
    return pl.pallas_call(
        body, name=name, grid=(r // tr,),
        in_specs=[pl.BlockSpec((NDEV, tr), lambda i: (0, i)), pl.BlockSpec((NDEV, c), lambda i: (0, 0)),
                  tile, tile, tile],
        out_specs=[tile] * 4, out_shape=[sds] * 4,
        compiler_params=_cp(("arbitrary",), 48),
    )(cs, dm, w, m, v)


def _rows(*vs):
    d = vs[0].shape[-1]
    rows = [v.reshape(1, d) for v in vs]
    return jnp.concatenate(rows + [jnp.zeros((8 - len(rows), d), F32)], axis=0)


def _block_diag_pairs(w):
    hd = w.shape[-1]
    z = jnp.zeros((w.shape[0] // 2, hd, hd), w.dtype)
    top = jnp.concatenate([w[0::2], z], axis=2)
    bot = jnp.concatenate([z, w[1::2]], axis=2)
    return jnp.concatenate([top, bot], axis=1).astype(BF16)


def _diag_pairs(g):
    hd = g.shape[-1] // 2
    both = jnp.stack([g[:, :hd, :hd], g[:, hd:, hd:]], axis=1)
    return both.reshape(2 * g.shape[0], hd, hd)


def kernel(x, c, w_mod, b_mod, g_ffn1, w_ffn1_in, w_ffn1_out, g_mix, w_in, conv_w, conv_b, ln_g, ln_b, rnn_conv_w, rnn_conv_b, w_a, b_a, w_i, b_i, lru_lambda, w_out, g_ffn2, w_ffn2_in, w_ffn2_out, w_fmod, b_fmod, g_final, loss_target, m_w_mod, m_b_mod, m_g_ffn1, m_w_ffn1_in, m_w_ffn1_out, m_g_mix, m_w_in, m_conv_w, m_conv_b, m_ln_g, m_ln_b, m_rnn_conv_w, m_rnn_conv_b, m_w_a, m_b_a, m_w_i, m_b_i, m_lru_lambda, m_w_out, m_g_ffn2, m_w_ffn2_in, m_w_ffn2_out, m_w_fmod, m_b_fmod, m_g_final, v_w_mod, v_b_mod, v_g_ffn1, v_w_ffn1_in, v_w_ffn1_out, v_g_mix, v_w_in, v_conv_w, v_conv_b, v_ln_g, v_ln_b, v_rnn_conv_w, v_rnn_conv_b, v_w_a, v_b_a, v_w_i, v_b_i, v_lru_lambda, v_w_out, v_g_ffn2, v_w_ffn2_in, v_w_ffn2_out, v_w_fmod, v_b_fmod, v_g_final):
    t, d = x.shape[1], x.shape[2]
    fb = w_ffn1_in.shape[2]
    nm = w_mod.shape[2]
    nf = w_fmod.shape[1]
    dc = conv_b.shape[1]
    cl = conv_w.shape[2]
    tm = min(TOKEN_TILE, t)
    tk = min(WGRAD_TILE, t)
    nk = t // tk
    me = 4 * lax.axis_index("x") + 2 * lax.axis_index("y") + lax.axis_index("c")

    bmod_l = lax.dynamic_slice(b_mod, (0, me * nm), (1, nm))
    bfmod_l = lax.dynamic_slice(b_fmod.reshape(1, -1), (0, me * nf), (1, nf))
    cs, mod_rows, fmod_rows = _mod_exchange(c, w_mod[0], bmod_l, w_fmod, bfmod_l, "mod_exchange")
    mod = mod_rows.reshape(9, d)
    fmod = fmod_rows.reshape(2, d)
    vec1 = _rows(g_ffn1, mod[0], mod[1], mod[2])
    vecm = _rows(g_mix, mod[3], mod[4], mod[5])
    vec3 = _rows(g_ffn2, mod[6], mod[7], mod[8])
    vecf = _rows(g_final, fmod[0], fmod[1])

    tr = jnp.transpose
    cwl = jnp.concatenate([conv_w[0], jnp.zeros((1, cl), F32), rnn_conv_w[0], jnp.zeros((4, cl), F32)], axis=0)
    wi1, wo1, cwg = _exchange(
        _Exchange("gather", [tr(w_ffn1_in[0]).astype(BF16), w_ffn1_out[0].astype(BF16), cwl]), "gather_ffn1")
    wi1 = wi1.reshape(2, 4, fb, d)
    wo1 = wo1.reshape(4 * fb, d)

    xin = x[0]
    later = [w_in[0].astype(BF16), w_out[0].astype(BF16), tr(w_ffn2_in[0]).astype(BF16), w_ffn2_out[0].astype(BF16)]
    x1, h1, gu1, f1, win, wout, wi2, wo2 = _ffn_fwd(xin, vec1, wi1, wo1, tm, "ffn1_fwd",
                                                    host=_Exchange("gather", later))
    wi2 = wi2.reshape(2, 4, fb, d)
    wo2 = wo2.reshape(4 * fb, d)
    wout = wout.reshape(d, d)
    h2, proj = _mix_in(x1, vecm, win, tm, "mix_in")
    lnv = _rows(ln_g, ln_b)
    rvec = _rows(rnn_conv_b, b_a, b_i, lru_lambda)
    wab = _block_diag_pairs(w_a[0])
    wib = _block_diag_pairs(w_i[0])
    cwf = jnp.transpose(cwg, (1, 0, 2)).reshape(40, NDEV * cl)
    cw32 = jnp.concatenate([cwf[0:CONV_W], conv_b], axis=0)
    rw8 = cwf[32:40]

    cv = _conv_fwd(proj, cw32, "conv_fwd")
    hr, yr = _rnn_fwd(proj, rw8, rvec, wab, wib, "rnn_fwd")
    x2, ym, ycat = _mix_out(x1, cv, yr, vecm, lnv, wout, tm, "mix_out")
    x3, h3, gu3, f3 = _ffn_fwd(x2, vec3, wi2, wo2, tm, "ffn2_fwd")

    dx3, dvf = _final(x3, loss_target[0], vecf, tm, "final_loss")
    a_tok = lambda width: pl.BlockSpec((tk, width), lambda k, s: (s, 0))
    blk3 = lambda width: pl.BlockSpec((None, tk, width), lambda k, s: (k, s, 0))
    sel = jnp.stack([lax.axis_index("c"), 2 * lax.axis_index("x") + lax.axis_index("y")]).astype(jnp.int32)
    row_tile = {"w_ffn1_in": fb // 4, "w_ffn1_out": fb // 4, "w_in": 512, "w_out": 128,
                "w_ffn2_in": fb // 4, "w_ffn2_out": fb // 4}

    def chip_sums(names, partials, from_sib):
        out = [_chip_sum(p.reshape((4, 2) + p.shape[1:]), r, sel, row_tile[nm_], "chip_sum_" + nm_)
               for nm_, p, r in zip(names, partials, from_sib)]
        return [o[0] for o in out], [o[1] for o in out]

    dgu3, p_wi2, p_wo2, dva3 = _ffn_bwd_w(dx3, vec3, gu3, f3, h3, wo2, tm, "ffn2_bwd_w")
    p_wi2 = p_wi2.reshape(NDEV, fb, d)
    p_wo2 = p_wo2.reshape(NDEV, fb // 2, d)
    names2 = ["w_ffn2_in", "w_ffn2_out"]
    dx2, dv3, sib_wi2, sib_wo2 = _ffn_bwd_in(dx3, x2, vec3, dgu3, wi2, tm, "ffn2_bwd_in",
                                             host=_Exchange("sibling", [p_wi2, p_wo2]))
    (s_wi2, s_wo2), owns2 = chip_sums(names2, [p_wi2, p_wo2], [sib_wi2, sib_wo2])
    dym, dcv, dhr, duy, dln, dgt2, r_wo2 = _mixout_bwd(
        dx2, ym, cv, hr, proj, vecm, lnv, wout, tm, "mixout_bwd", host=_Exchange("chips", [s_wo2]))
    dval, dgate, dcw = _conv_bwd(proj, dcv, cw32, "conv_bwd")
    dux, rsm, dwab, dwib, r_wi2 = _rnn_bwd(proj, hr, dhr, rw8, rvec, wab, wib, "rnn_bwd",
                                           host=_Exchange("chips", [s_wi2]))
    parts = [dval, dgate, dux, duy]
    dx1, dvm = _mixin_bwd(dx2, x1, parts, vecm, win, tm, "mixin_bwd")
    p_wout = _mm_tn(ycat, dym, pl.BlockSpec((tk, LANES), lambda k, s: (s, k)), a_tok(d),
                    NDEV, nk, LANES, d, "wgrad_out")
    p_win = _wgrad_in(h2, parts, win.shape[2], tk, "wgrad_in")
    namesm = ["w_in", "w_out"]
    sumsm, ownsm = chip_sums(namesm, [p_win, p_wout],
                             _exchange(_Exchange("sibling", [p_win, p_wout]), "rs_sibling_mix"))
    dgu1, p_wi1, p_wo1, dva1, r_win, r_wout = _ffn_bwd_w(dx1, vec1, gu1, f1, h1, wo1, tm, "ffn1_bwd_w",
                                                         host=_Exchange("chips", sumsm))
    p_wi1 = p_wi1.reshape(NDEV, fb, d)
    p_wo1 = p_wo1.reshape(NDEV, fb // 2, d)
    names1 = ["w_ffn1_in", "w_ffn1_out"]
    sums1, owns1 = chip_sums(names1, [p_wi1, p_wo1],
                             _exchange(_Exchange("sibling", [p_wi1, p_wo1]), "rs_sibling_ffn1"))
    dx0, dv1, r_wi1, r_wo1 = _ffn_bwd_in(dx1, xin, vec1, dgu1, wi1, tm, "ffn1_bwd_in",
                                         host=_Exchange("chips", sums1))
    from_chips = {"w_in": r_win, "w_out": r_wout, "w_ffn2_in": r_wi2, "w_ffn2_out": r_wo2,
                  "w_ffn1_in": r_wi1, "w_ffn1_out": r_wo1}
    owns = dict(zip(namesm + names2 + names1, ownsm + owns2 + owns1))

    big = {"w_ffn1_in": (tr(w_ffn1_in[0]), tr(m_w_ffn1_in[0]), tr(v_w_ffn1_in[0])),
           "w_ffn1_out": (w_ffn1_out[0], m_w_ffn1_out[0], v_w_ffn1_out[0]),
           "w_in": (w_in[0], m_w_in[0], v_w_in[0]), "w_out": (w_out[0], m_w_out[0], v_w_out[0]),
           "w_ffn2_in": (tr(w_ffn2_in[0]), tr(m_w_ffn2_in[0]), tr(v_w_ffn2_in[0])),
           "w_ffn2_out": (w_ffn2_out[0], m_w_ffn2_out[0], v_w_ffn2_out[0])}
    res = {}
    for nm_ in namesm + names2 + names1:
        out4 = _rs_final(owns[nm_], from_chips[nm_], *big[nm_], row_tile[nm_], "rs_final_" + nm_)
        if nm_ in ("w_ffn1_in", "w_ffn2_in"):
            out4 = [tr(o) for o in out4]
        res[nm_] = [o[None] for o in out4]

    dmod_row = jnp.concatenate([dv1[1:2], dv1[3:4], dva1[2:3], dvm[1:2], dvm[3:4], dgt2[0:1],
                                dv3[1:2], dv3[3:4], dva3[2:3]], axis=0)
    pack = jnp.concatenate([
        dmod_row, dvf[1:2], dvf[3:4],
        dv1[0:1], dvm[0:1], dv3[0:1], dvf[0:1],
        dcw.reshape(16, d),
        jnp.concatenate([dln[0:1], dln[1:2]], axis=1),
        rsm.reshape(4, d),
        _diag_pairs(dwab).reshape(32, d), _diag_pairs(dwib).reshape(32, d),
        dvf[4:5],
        jnp.zeros((SMALL_ROWS - 101, d), F32)], axis=0)
    allp, tot = _small_exchange(pack, "small_exchange")
    loss = jnp.sum(tot[100])

    dm_all = allp[:, 0:9, :].reshape(NDEV, 9 * d)
    dfm_all = allp[:, 9:11, :].reshape(NDEV, 2 * d)
    dm_l = lax.dynamic_slice(dm_all, (0, me * nm), (NDEV, nm))
    dfm_l = lax.dynamic_slice(dfm_all, (0, me * nf), (NDEV, nf))
    out_wmod = [o[None] for o in _mod_weight_update(cs, dm_l, w_mod[0], m_w_mod[0], v_w_mod[0], 256, "w_mod_update")]
    out_wfmod = _mod_weight_update(cs, dfm_l, w_fmod, m_w_fmod, v_w_fmod, 256, "w_fmod_update")
    res["w_mod"] = out_wmod
    res["w_fmod"] = list(out_wfmod)

    dcw_f = tot[15:31].reshape(32, dc)
    rsm_f = tot[32:36].reshape(8, dc)
    small_grads = {
        "b_mod": tot[0:9].reshape(1, 9 * d),
        "b_fmod": tot[9:11].reshape(2 * d),
        "g_ffn1": tot[11:12], "g_mix": tot[12:13], "g_ffn2": tot[13:14], "g_final": tot[14],
        "conv_w": lax.dynamic_slice(dcw_f, (0, me * cl), (CONV_W, cl))[None],
        "conv_b": dcw_f[31:32],
        "ln_g": tot[31:32, 0:dc], "ln_b": tot[31:32, dc:],
        "rnn_conv_w": lax.dynamic_slice(rsm_f, (0, me * cl), (RNN_CONV_W, cl))[None],
        "rnn_conv_b": rsm_f[4:5], "b_a": rsm_f[5:6], "b_i": rsm_f[6:7], "lru_lambda": rsm_f[7:8],
        "w_a": tot[36:68].reshape(w_a.shape), "w_i": tot[68:100].reshape(w_i.shape),
    }
    small_params = {
        "b_mod": (b_mod, m_b_mod, v_b_mod), "b_fmod": (b_fmod, m_b_fmod, v_b_fmod),
        "g_ffn1": (g_ffn1, m_g_ffn1, v_g_ffn1), "g_mix": (g_mix, m_g_mix, v_g_mix),
        "g_ffn2": (g_ffn2, m_g_ffn2, v_g_ffn2), "g_final": (g_final, m_g_final, v_g_final),
        "conv_w": (conv_w, m_conv_w, v_conv_w), "conv_b": (conv_b, m_conv_b, v_conv_b),
        "ln_g": (ln_g, m_ln_g, v_ln_g), "ln_b": (ln_b, m_ln_b, v_ln_b),
        "rnn_conv_w": (rnn_conv_w, m_rnn_conv_w, v_rnn_conv_w),
        "rnn_conv_b": (rnn_conv_b, m_rnn_conv_b, v_rnn_conv_b),
        "w_a": (w_a, m_w_a, v_w_a), "b_a": (b_a, m_b_a, v_b_a),
        "w_i": (w_i, m_w_i, v_w_i), "b_i": (b_i, m_b_i, v_b_i),
        "lru_lambda": (lru_lambda, m_lru_lambda, v_lru_lambda),
    }
    for name, g in small_grads.items():
        w, m, v = small_params[name]
        shp = w.shape
        two_d = (-1, shp[-1]) if w.ndim > 1 else (1, shp[0])
        outs = _adamw_small(w.reshape(two_d), g.reshape(two_d), m.reshape(two_d), v.reshape(two_d),
                            "adamw_" + name)
        res[name] = [g.reshape(shp)] + [o.reshape(shp) for o in outs]

    order = ["w_mod", "b_mod", "g_ffn1", "w_ffn1_in", "w_ffn1_out", "g_mix", "w_in", "conv_w", "conv_b",
             "ln_g", "ln_b", "rnn_conv_w", "rnn_conv_b", "w_a", "b_a", "w_i", "b_i", "lru_lambda", "w_out",
             "g_ffn2", "w_ffn2_in", "w_ffn2_out", "w_fmod", "b_fmod", "g_final"]
    return (loss, dx0[None], *[res[n][0] for n in order], *[res[n][1] for n in order],
            *[res[n][2] for n in order], *[res[n][3] for n in order])
```

```python
import functools
import math

import jax
import jax.numpy as jnp
from jax import lax
from jax.experimental import pallas as pl
from jax.experimental.pallas import tpu as pltpu

F32 = jnp.float32
BF16 = jnp.bfloat16
MESH_IDS = pl.DeviceIdType.MESH
NDEV = 8
EPS = 1e-6
RG_C = 8.0
CONV_W = 31
RNN_CONV_W = 4
LANES = 128
ADAM_LR = 0.001
ADAM_B1 = 0.9
ADAM_B2 = 0.999
ADAM_EPS = 1e-08
ADAM_WD = 0.01
ADAM_STEP = 10
SMALL_ROWS = 104
TOKEN_TILE = 512
WGRAD_TILE = 2048
ROW_GROUP = 16
HI = lax.Precision.HIGHEST


def _cp(sem, vmem_mb):
    return pltpu.CompilerParams(dimension_semantics=sem, vmem_limit_bytes=vmem_mb * 1024 * 1024)


def _dot(a, b):
    return jnp.dot(a, b, preferred_element_type=F32)


def _dot_nt(a, b):
    return lax.dot_general(a, b, (((1,), (1,)), ((), ())), preferred_element_type=F32)


def _dot_tn(a, b):
    return lax.dot_general(a, b, (((0,), (0,)), ((), ())), preferred_element_type=F32)


def _sigmoid(x):
    return 1.0 / (1.0 + jnp.exp(-x))


def _adaln(x, vec_ref):
    rstd = lax.rsqrt(jnp.mean(x * x, axis=-1, keepdims=True) + EPS)
    return (x * rstd) * vec_ref[0:1, :] * (1.0 + vec_ref[2:3, :]) + vec_ref[1:2, :]


def _adaln_bwd(x, dh, vec_ref, dvec_ref):
    rstd = lax.rsqrt(jnp.mean(x * x, axis=-1, keepdims=True) + EPS)
    xhat = x * rstd
    dvec_ref[0:1, :] += jnp.sum(dh * xhat, axis=0, keepdims=True)
    dvec_ref[1:2, :] += jnp.sum(dh, axis=0, keepdims=True)
    dxhat = dh * (vec_ref[0:1, :] * (1.0 + vec_ref[2:3, :]))
    return rstd * (dxhat - xhat * jnp.mean(dxhat * xhat, axis=-1, keepdims=True))


def _adaln_finish(vec_ref, dvec_ref):
    s = dvec_ref[0:1, :]
    dvec_ref[3:4, :] = vec_ref[0:1, :] * s
    dvec_ref[0:1, :] = (1.0 + vec_ref[2:3, :]) * s


def _gelu_and_grad(x):
    k0 = math.sqrt(2.0 / math.pi)
    x2 = x * x
    t = jnp.tanh(k0 * (x + 0.044715 * x * x2))
    g = 0.5 * x * (1.0 + t)
    dg = 0.5 * (1.0 + t) + 0.5 * x * (1.0 - t * t) * (k0 * (1.0 + 3.0 * 0.044715 * x2))
    return g, dg


def _log_sigmoid(x):
    z = jnp.exp(-jnp.abs(x))
    u = 1.0 + z
    d = u - 1.0
    log1p = jnp.where(d == 0.0, z, jnp.log(u) * (z / jnp.where(d == 0.0, 1.0, d)))
    return jnp.minimum(x, 0.0) - log1p


def _neg_expm1(x):
    series = -x * (1.0 + x * (0.5 + x * (1.0 / 6.0 + x * (1.0 / 24.0 + x * (1.0 / 120.0)))))
    return jnp.where(x > -0.05, series, 1.0 - jnp.exp(x))


def _scan_fwd(a, b):
    n = a.shape[0]
    row = lax.broadcasted_iota(jnp.int32, a.shape, 0)
    s = 1
    while s < n:
        ok = row >= s
        b = a * jnp.where(ok, pltpu.roll(b, s, 0), 0.0) + b
        if 2 * s < n:
            a = a * jnp.where(ok, pltpu.roll(a, s, 0), 1.0)
        s *= 2
    return b


def _scan_rev(a, d):
    n = a.shape[0]
    row = lax.broadcasted_iota(jnp.int32, a.shape, 0)
    s = 1
    while s < n:
        ok = row < n - s
        d = a * jnp.where(ok, pltpu.roll(d, n - s, 0), 0.0) + d
        if 2 * s < n:
            a = a * jnp.where(ok, pltpu.roll(a, n - s, 0), 1.0)
        s *= 2
    return d


def _rglru_gates(xr, wa_ref, wi_ref, rvec_ref):
    xb = xr.astype(BF16)
    r = _sigmoid(_dot(xb, wa_ref[...]) + rvec_ref[1:2, :])
    ig = _sigmoid(_dot(xb, wi_ref[...]) + rvec_ref[2:3, :])
    ls = _log_sigmoid(rvec_ref[3:4, :])
    log_a = RG_C * r * ls
    a = jnp.exp(log_a)
    mult = jnp.sqrt(_neg_expm1(2.0 * log_a))
    return xb, r, ig, ls, a, mult


def _rnn_conv(ux, rw_ref, rvec_ref, ext_ref):
    t = ux.shape[0]
    ext_ref[0:8, :] = jnp.zeros((8, ux.shape[1]), F32)
    ext_ref[8:, :] = ux
    xr = rvec_ref[0:1, :] + rw_ref[RNN_CONV_W - 1:RNN_CONV_W, :] * ux
    for k in range(RNN_CONV_W - 1):
        d = RNN_CONV_W - 1 - k
        xr = xr + rw_ref[k:k + 1, :] * ext_ref[8 - d:8 - d + t, :]
    return xr


def _ffn_fwd(x, vec, wi, wo, tm, name, host=None):
    t, d = x.shape
    nj, fb = wi.shape[1], wi.shape[2]
    nt = t // tm

    def body(ins, outs, scr):
        x_ref, vec_ref, wi_ref, wo_ref = ins
        xo_ref, h_ref, gu_ref, f_ref = outs
        acc_ref, = scr
        j = pl.program_id(1)

        @pl.when(j == 0)
        def _():
            h_ref[...] = _adaln(x_ref[...], vec_ref).astype(BF16)
            acc_ref[...] = jnp.zeros_like(acc_ref)

        h = h_ref[...]
        gate = _dot_nt(h, wi_ref[0])
        up = _dot_nt(h, wi_ref[1])
        gu_ref[0] = gate.astype(BF16)
        gu_ref[1] = up.astype(BF16)
        act = (gate * _sigmoid(gate) * up).astype(BF16)
        acc_ref[...] += _dot(act, wo_ref[...])

        @pl.when(j == nj - 1)
        def _():
            f = acc_ref[...]
            f_ref[...] = f.astype(BF16)
            xo_ref[...] = x_ref[...] + 0.5 * vec_ref[3:4, :] * f

    tile = pl.BlockSpec((tm, d), lambda i, j: (i, 0))
    return _hosted_call(
        body, host, name, (nt, nj),
        [tile,
         pl.BlockSpec((8, d), lambda i, j: (0, 0)),
         pl.BlockSpec((2, None, fb, d), lambda i, j: (0, j, 0, 0)),
         pl.BlockSpec((fb, d), lambda i, j: (j, 0))],
        [tile, tile, pl.BlockSpec((2, None, tm, fb), lambda i, j: (0, j, i, 0)), tile],
        [jax.ShapeDtypeStruct((t, d), F32), jax.ShapeDtypeStruct((t, d), BF16),
         jax.ShapeDtypeStruct((2, nj, t, fb), BF16), jax.ShapeDtypeStruct((t, d), BF16)],
        [pltpu.VMEM((tm, d), F32)], ("arbitrary", "arbitrary"), 48, (x, vec, wi, wo))


def _mix_in(x, vec, win, tm, name):
    t, d = x.shape
    nb, _, cb = win.shape

    def body(x_ref, vec_ref, w_ref, h_ref, p_ref):
        h = _adaln(x_ref[...], vec_ref).astype(BF16)
        h_ref[...] = h
        for k in range(nb):
            p_ref[:, k * cb:(k + 1) * cb] = _dot(h, w_ref[k])

    return pl.pallas_call(
        body, name=name, grid=(t // tm,),
        in_specs=[pl.BlockSpec((tm, d), lambda i: (i, 0)),
                  pl.BlockSpec((8, d), lambda i: (0, 0)),
                  pl.BlockSpec((nb, d, cb), lambda i: (0, 0, 0))],
        out_specs=[pl.BlockSpec((tm, d), lambda i: (i, 0)),
                   pl.BlockSpec((tm, nb * cb), lambda i: (i, 0))],
        out_shape=[jax.ShapeDtypeStruct((t, d), BF16), jax.ShapeDtypeStruct((t, nb * cb), F32)],
        compiler_params=_cp(("arbitrary",), 48),
    )(x, vec, win)


def _conv_fwd(proj, cw32, name):
    t = proj.shape[0]
    nblk = cw32.shape[1] // LANES
    ch = min(t, 128)

    def body(val_ref, gate_ref, cw_ref, cv_ref, ext_ref):
        ext_ref[0:32, :] = jnp.zeros((32, LANES), F32)
        ext_ref[32:, :] = val_ref[...] * _sigmoid(gate_ref[...])
        for r in range(t // ch):
            acc = jnp.broadcast_to(cw_ref[31:32, :], (ch, LANES))
            for k in range(CONV_W):
                off = 32 + r * ch - (CONV_W - 1 - k)
                acc = acc + cw_ref[k:k + 1, :] * ext_ref[off:off + ch, :]
            cv_ref[r * ch:(r + 1) * ch, :] = acc

    return pl.pallas_call(
        body, name=name, grid=(nblk,),
        in_specs=[pl.BlockSpec((t, LANES), lambda c: (0, c)),
                  pl.BlockSpec((t, LANES), lambda c: (0, nblk + c)),
                  pl.BlockSpec((32, LANES), lambda c: (0, c))],
        out_specs=pl.BlockSpec((t, LANES), lambda c: (0, c)),
        out_shape=jax.ShapeDtypeStruct((t, nblk * LANES), F32),
        scratch_shapes=[pltpu.VMEM((t + 32, LANES), F32)],
        compiler_params=_cp(("arbitrary",), 48),
    )(proj, proj, cw32)


def _rnn_fwd(proj, rw8, rvec, wab, wib, name):
    t = proj.shape[0]
    nblk = rvec.shape[1] // LANES

    def body(ux_ref, uy_ref, rw_ref, rvec_ref, wa_ref, wi_ref, h_ref, yr_ref, ext_ref):
        xr = _rnn_conv(ux_ref[...], rw_ref, rvec_ref, ext_ref)
        _, _, ig, _, a, mult = _rglru_gates(xr, wa_ref, wi_ref, rvec_ref)
        h = _scan_fwd(a, mult * (ig * xr))
        h_ref[...] = h
        ge, _ = _gelu_and_grad(uy_ref[...])
        yr_ref[...] = (ge * h).astype(BF16)

    blk = lambda off: pl.BlockSpec((t, LANES), lambda c: (0, off + c))
    return pl.pallas_call(
        body, name=name, grid=(nblk,),
        in_specs=[blk(2 * nblk), blk(3 * nblk),
                  pl.BlockSpec((8, LANES), lambda c: (0, c)),
                  pl.BlockSpec((8, LANES), lambda c: (0, c)),
                  pl.BlockSpec((None, LANES, LANES), lambda c: (c, 0, 0)),
                  pl.BlockSpec((None, LANES, LANES), lambda c: (c, 0, 0))],
        out_specs=[blk(0), blk(0)],
        out_shape=[jax.ShapeDtypeStruct((t, nblk * LANES), F32), jax.ShapeDtypeStruct((t, nblk * LANES), BF16)],
        scratch_shapes=[pltpu.VMEM((t + 8, LANES), F32)],
        compiler_params=_cp(("arbitrary",), 56),
    )(proj, proj, rw8, rvec, wab, wib)


def _ln_silu(cv, lnv_ref):
    mu = jnp.mean(cv, axis=-1, keepdims=True)
    xc = cv - mu
    rs = lax.rsqrt(jnp.mean(xc * xc, axis=-1, keepdims=True) + EPS)
    chat = xc * rs
    z = chat * lnv_ref[0:1, :] + lnv_ref[1:2, :]
    sg = _sigmoid(z)
    return rs, chat, z, sg


def _mix_out(x, cv, yr, vec, lnv, wout, tm, name):
    t, d = x.shape
    dc = cv.shape[1]

    def body(x_ref, cv_ref, yr_ref, vec_ref, lnv_ref, w_ref, xo_ref, ym_ref, yc_ref):
        _, _, z, sg = _ln_silu(cv_ref[...], lnv_ref)
        yc = (z * sg).astype(BF16)
        yr = yr_ref[...]
        yc_ref[:, 0:dc] = yc
        yc_ref[:, dc:] = yr
        ym = _dot(yc, w_ref[0:dc, :]) + _dot(yr, w_ref[dc:, :])
        ym_ref[...] = ym.astype(BF16)
        xo_ref[...] = x_ref[...] + vec_ref[3:4, :] * ym

    return pl.pallas_call(
        body, name=name, grid=(t // tm,),
        in_specs=[pl.BlockSpec((tm, d), lambda i: (i, 0)),
                  pl.BlockSpec((tm, dc), lambda i: (i, 0)),
                  pl.BlockSpec((tm, dc), lambda i: (i, 0)),
                  pl.BlockSpec((8, d), lambda i: (0, 0)),
                  pl.BlockSpec((8, dc), lambda i: (0, 0)),
                  pl.BlockSpec((d, d), lambda i: (0, 0))],
        out_specs=[pl.BlockSpec((tm, d), lambda i: (i, 0)),
                   pl.BlockSpec((tm, d), lambda i: (i, 0)),
                   pl.BlockSpec((tm, d), lambda i: (i, 0))],
        out_shape=[jax.ShapeDtypeStruct((t, d), F32), jax.ShapeDtypeStruct((t, d), BF16),
                   jax.ShapeDtypeStruct((t, d), BF16)],
        compiler_params=_cp(("arbitrary",), 48),
    )(x, cv, yr, vec, lnv, wout)


def _final(x, tgt, vec, f, nvec, tm, name):
    t, d = x.shape
    nt = t // tm

    def body(x_ref, t_ref, vec_ref, f_ref, nvec_ref, dx_ref, dvec_ref, df_ref, dgt_ref):
        i = pl.program_id(0)

        @pl.when(i == 0)
        def _():
            dvec_ref[...] = jnp.zeros_like(dvec_ref)
            dgt_ref[...] = jnp.zeros_like(dgt_ref)

        xv = x_ref[...]
        e = _adaln(xv, vec_ref) - t_ref[...]
        dvec_ref[4:5, :] += (0.5 / d) * jnp.sum(e * e, axis=0, keepdims=True)
        dx = _adaln_bwd(xv, e * (1.0 / d), vec_ref, dvec_ref)
        dx_ref[...] = dx
        _emit_df(dx, f_ref, nvec_ref, df_ref, dgt_ref)

        @pl.when(i == nt - 1)
        def _():
            _adaln_finish(vec_ref, dvec_ref)

    tile = pl.BlockSpec((tm, d), lambda i: (i, 0))
    tab = pl.BlockSpec((8, d), lambda i: (0, 0))
    return pl.pallas_call(
        body, name=name, grid=(nt,),
        in_specs=[tile, tile, tab, tile, tab],
        out_specs=[tile, tab, tile, tab],
        out_shape=[jax.ShapeDtypeStruct((t, d), F32), jax.ShapeDtypeStruct((8, d), F32),
                   jax.ShapeDtypeStruct((t, d), BF16), jax.ShapeDtypeStruct((8, d), F32)],
        compiler_params=_cp(("arbitrary",), 48),
    )(x, tgt, vec, f, nvec)


def _emit_df(dx, f_ref, nvec_ref, df_ref, dgt_ref):
    df_ref[...] = (0.5 * nvec_ref[3:4, :] * dx).astype(BF16)
    dgt_ref[2:3, :] += 0.5 * jnp.sum(dx * f_ref[...].astype(F32), axis=0, keepdims=True)


def _ffn_bwd_w(df, gu, h, wo, tm, name, host=None):
    t, d = df.shape
    nj, fb = gu.shape[1], gu.shape[3]
    nt = t // tm
    sub = min(tm, ROW_GROUP)

    def body(ins, outs, scr):
        df_ref, gu_ref, h_ref, wo_ref = ins
        dgu_ref, dwi_ref, dwo_ref = outs
        accg_ref, accu_ref, acco_ref, dact_ref, act_ref = scr
        i = pl.program_id(1)

        @pl.when(i == 0)
        def _():
            accg_ref[...] = jnp.zeros_like(accg_ref)
            accu_ref[...] = jnp.zeros_like(accu_ref)
            acco_ref[...] = jnp.zeros_like(acco_ref)

        dact_ref[...] = _dot_nt(df_ref[...], wo_ref[...])
        for r in range(tm // sub):
            rows = slice(r * sub, (r + 1) * sub)
            g = gu_ref[0, rows, :].astype(F32)
            u = gu_ref[1, rows, :].astype(F32)
            dact = dact_ref[rows, :]
            sg = _sigmoid(g)
            sl = g * sg
            dgu_ref[0, rows, :] = (dact * u * (sg * (1.0 + g * (1.0 - sg)))).astype(BF16)
            dgu_ref[1, rows, :] = (dact * sl).astype(BF16)
            act_ref[rows, :] = (sl * u).astype(BF16)
        hb = h_ref[...]
        acco_ref[...] += _dot_tn(act_ref[...], df_ref[...])
        accg_ref[...] += _dot_tn(dgu_ref[0], hb)
        accu_ref[...] += _dot_tn(dgu_ref[1], hb)

        @pl.when(i == nt - 1)
        def _():
            dwi_ref[0] = accg_ref[...].astype(BF16)
            dwi_ref[1] = accu_ref[...].astype(BF16)
            dwo_ref[...] = acco_ref[...].astype(BF16)

    tile = pl.BlockSpec((tm, d), lambda j, i: (i, 0))
    return _hosted_call(
        body, host, name, (nj, nt),
        [tile,
         pl.BlockSpec((2, None, tm, fb), lambda j, i: (0, j, i, 0)),
         tile,
         pl.BlockSpec((fb, d), lambda j, i: (j, 0))],
        [pl.BlockSpec((2, None, tm, fb), lambda j, i: (0, j, i, 0)),
         pl.BlockSpec((2, None, fb, d), lambda j, i: (0, j, 0, 0)),
         pl.BlockSpec((None, fb, d), lambda j, i: (j, 0, 0))],
        [jax.ShapeDtypeStruct((2, nj, t, fb), BF16), jax.ShapeDtypeStruct((2, nj, fb, d), BF16),
         jax.ShapeDtypeStruct((nj, fb, d), BF16)],
        [pltpu.VMEM((fb, d), F32), pltpu.VMEM((fb, d), F32), pltpu.VMEM((fb, d), F32),
         pltpu.VMEM((tm, fb), F32), pltpu.VMEM((tm, fb), BF16)],
        ("arbitrary", "arbitrary"), 56, (df, gu, h, wo))


def _ffn_bwd_in(dxo, x, vec, dgu, wi, tm, name, host=None):
    t, d = x.shape
    nj, fb = wi.shape[1], wi.shape[2]
    nt = t // tm

    def body(ins, outs, scr):
        dxo_ref, x_ref, vec_ref, dgu_ref, wi_ref = ins
        dx_ref, dvec_ref = outs
        i = pl.program_id(0)

        @pl.when(i == 0)
        def _():
            dvec_ref[...] = jnp.zeros_like(dvec_ref)

        dh = jnp.zeros((tm, d), F32)
        for a in range(2):
            for k in range(nj):
                dh = dh + _dot(dgu_ref[a, k], wi_ref[a, k])
        dx_ref[...] = dxo_ref[...] + _adaln_bwd(x_ref[...], dh, vec_ref, dvec_ref)

        @pl.when(i == nt - 1)
        def _():
            _adaln_finish(vec_ref, dvec_ref)

    tile = pl.BlockSpec((tm, d), lambda i: (i, 0))
    return _hosted_call(
        body, host, name, (nt,),
        [tile, tile,
         pl.BlockSpec((8, d), lambda i: (0, 0)),
         pl.BlockSpec((2, nj, tm, fb), lambda i: (0, 0, i, 0)),
         pl.BlockSpec((2, nj, fb, d), lambda i: (0, 0, 0, 0))],
        [tile, pl.BlockSpec((8, d), lambda i: (0, 0))],
        [jax.ShapeDtypeStruct((t, d), F32), jax.ShapeDtypeStruct((8, d), F32)],
        [], ("arbitrary",), 60, (dxo, x, vec, dgu, wi))


def _mm_tn(a, b, a_spec, b_spec, nblk, nk, m, n, name):
    def body(a_ref, b_ref, o_ref, acc_ref):
        s = pl.program_id(1)

        @pl.when(s == 0)
        def _():
            acc_ref[...] = jnp.zeros_like(acc_ref)

        acc_ref[...] += _dot_tn(a_ref[...], b_ref[...])

        @pl.when(s == nk - 1)
        def _():
            o_ref[...] = acc_ref[...].astype(BF16)

    return pl.pallas_call(
        body, name=name, grid=(nblk, nk),
        in_specs=[a_spec, b_spec],
        out_specs=pl.BlockSpec((None, m, n), lambda k, s: (k, 0, 0)),
        out_shape=jax.ShapeDtypeStruct((nblk, m, n), BF16),
        scratch_shapes=[pltpu.VMEM((m, n), F32)],
        compiler_params=_cp(("arbitrary", "arbitrary"), 56),
    )(a, b)


def _wgrad_in(h, parts, cb, tk, name):
    t, d = h.shape
    per = parts[0].shape[1] // cb
    nblk = len(parts) * per
    nk = t // tk

    def body(h_ref, p0, p1, p2, p3, o_ref, acc_ref):
        k = pl.program_id(0)
        s = pl.program_id(1)

        @pl.when(s == 0)
        def _():
            acc_ref[...] = jnp.zeros_like(acc_ref)

        for p, p_ref in enumerate((p0, p1, p2, p3)):
            @pl.when(k // per == p)
            def _(p_ref=p_ref):
                acc_ref[...] += _dot_tn(h_ref[...], p_ref[...])

        @pl.when(s == nk - 1)
        def _():
            o_ref[...] = acc_ref[...].astype(BF16)

    def part_spec(p):
        return pl.BlockSpec((tk, cb), lambda k, s: (jnp.where(k // per == p, s, 0), jnp.where(k // per == p, k % per, 0)))

    return pl.pallas_call(
        body, name=name, grid=(nblk, nk),
        in_specs=[pl.BlockSpec((tk, d), lambda k, s: (s, 0))] + [part_spec(p) for p in range(len(parts))],
        out_specs=pl.BlockSpec((None, d, cb), lambda k, s: (k, 0, 0)),
        out_shape=jax.ShapeDtypeStruct((nblk, d, cb), BF16),
        scratch_shapes=[pltpu.VMEM((d, cb), F32)],
        compiler_params=_cp(("arbitrary", "arbitrary"), 48),
    )(h, *parts)


def _mixout_bwd(dxo, ym, cv, hr, proj, vec, lnv, wout, tm, name, host=None):
    t, d = dxo.shape
    dc = cv.shape[1]
    nt = t // tm

    def body(ins, outs, scr):
        dxo_ref, ym_ref, cv_ref, hr_ref, uy_ref, vec_ref, lnv_ref, w_ref = ins
        dym_ref, dcv_ref, dhr_ref, duy_ref, dln_ref, dgt_ref = outs
        i = pl.program_id(0)

        @pl.when(i == 0)
        def _():
            dln_ref[...] = jnp.zeros_like(dln_ref)
            dgt_ref[...] = jnp.zeros_like(dgt_ref)

        dxo_v = dxo_ref[...]
        dym = (vec_ref[3:4, :] * dxo_v).astype(BF16)
        dym_ref[...] = dym
        dgt_ref[0:1, :] += jnp.sum(dxo_v * ym_ref[...].astype(F32), axis=0, keepdims=True)
        dyc = _dot_nt(dym, w_ref[0:dc, :])
        dyr = _dot_nt(dym, w_ref[dc:, :])
        rs, chat, z, sg = _ln_silu(cv_ref[...], lnv_ref)
        dz = dyc * (sg * (1.0 + z * (1.0 - sg)))
        dln_ref[0:1, :] += jnp.sum(dz * chat, axis=0, keepdims=True)
        dln_ref[1:2, :] += jnp.sum(dz, axis=0, keepdims=True)
        dchat = dz * lnv_ref[0:1, :]
        dcv_ref[...] = rs * (dchat - jnp.mean(dchat, axis=-1, keepdims=True)
                             - chat * jnp.mean(dchat * chat, axis=-1, keepdims=True))
        ge, dge = _gelu_and_grad(uy_ref[...])
        dhr_ref[...] = dyr * ge
        duy_ref[...] = (dyr * hr_ref[...] * dge).astype(BF16)

    tile_d = pl.BlockSpec((tm, d), lambda i: (i, 0))
    tile_c = pl.BlockSpec((tm, dc), lambda i: (i, 0))
    return _hosted_call(
        body, host, name, (nt,),
        [tile_d, tile_d, tile_c, tile_c,
         pl.BlockSpec((tm, dc), lambda i: (i, 3)),
         pl.BlockSpec((8, d), lambda i: (0, 0)),
         pl.BlockSpec((8, dc), lambda i: (0, 0)),
         pl.BlockSpec((d, d), lambda i: (0, 0))],
        [tile_d, tile_c, tile_c, tile_c,
         pl.BlockSpec((8, dc), lambda i: (0, 0)),
         pl.BlockSpec((8, d), lambda i: (0, 0))],
        [jax.ShapeDtypeStruct((t, d), BF16), jax.ShapeDtypeStruct((t, dc), F32),
         jax.ShapeDtypeStruct((t, dc), F32), jax.ShapeDtypeStruct((t, dc), BF16),
         jax.ShapeDtypeStruct((8, dc), F32), jax.ShapeDtypeStruct((8, d), F32)],
        [], ("arbitrary",), 48, (dxo, ym, cv, hr, proj, vec, lnv, wout))


def _conv_bwd(proj, dcv, cw32, name):
    t = proj.shape[0]
    nblk = cw32.shape[1] // LANES
    ch = min(t, 128)

    def body(val_ref, gate_ref, dcv_ref, cw_ref, dval_ref, dgate_ref, dcw_ref, extu_ref, extd_ref):
        val = val_ref[...]
        sg = _sigmoid(gate_ref[...])
        extu_ref[0:32, :] = jnp.zeros((32, LANES), F32)
        extu_ref[32:, :] = val * sg
        dcv_v = dcv_ref[...]
        extd_ref[0:t, :] = dcv_v
        extd_ref[t:, :] = jnp.zeros((32, LANES), F32)
        for r in range(t // ch):
            acc = jnp.zeros((ch, LANES), F32)
            for k in range(CONV_W):
                off = r * ch + (CONV_W - 1 - k)
                acc = acc + cw_ref[k:k + 1, :] * extd_ref[off:off + ch, :]
            rows = slice(r * ch, (r + 1) * ch)
            sg_r = _sigmoid(gate_ref[rows, :])
            dval_ref[rows, :] = (acc * sg_r).astype(BF16)
            dgate_ref[rows, :] = (acc * val_ref[rows, :] * sg_r * (1.0 - sg_r)).astype(BF16)
        for k in range(CONV_W):
            off = 32 - (CONV_W - 1 - k)
            dcw_ref[k:k + 1, :] = jnp.sum(dcv_v * extu_ref[off:off + t, :], axis=0, keepdims=True)
        dcw_ref[31:32, :] = jnp.sum(dcv_v, axis=0, keepdims=True)

    blk = lambda off: pl.BlockSpec((t, LANES), lambda c: (0, off + c))
    return pl.pallas_call(
        body, name=name, grid=(nblk,),
        in_specs=[blk(0), blk(nblk), blk(0), pl.BlockSpec((32, LANES), lambda c: (0, c))],
        out_specs=[blk(0), blk(0), pl.BlockSpec((32, LANES), lambda c: (0, c))],
        out_shape=[jax.ShapeDtypeStruct((t, nblk * LANES), BF16), jax.ShapeDtypeStruct((t, nblk * LANES), BF16),
                   jax.ShapeDtypeStruct((32, nblk * LANES), F32)],
        scratch_shapes=[pltpu.VMEM((t + 32, LANES), F32), pltpu.VMEM((t + 32, LANES), F32)],
        compiler_params=_cp(("arbitrary",), 56),
    )(proj, proj, dcv, cw32)


def _rnn_bwd(proj, hr, dhr, rw8, rvec, wab, wib, name, host=None):
    t = proj.shape[0]
    nblk = rvec.shape[1] // LANES

    def body(ins, outs, scr):
        ux_ref, h_ref, dh_ref, rw_ref, rvec_ref, wa_ref, wi_ref = ins
        dux_ref, sm_ref, dwa_ref, dwi_ref = outs
        ext_ref, extd_ref = scr
        xr = _rnn_conv(ux_ref[...], rw_ref, rvec_ref, ext_ref)
        xb, r, ig, ls, a, mult = _rglru_gates(xr, wa_ref, wi_ref, rvec_ref)
        row = lax.broadcasted_iota(jnp.int32, (t, LANES), 0)
        a_next = jnp.where(row < t - 1, pltpu.roll(a, t - 1, 0), 0.0)
        g = _scan_rev(a_next, dh_ref[...])
        hprev = jnp.where(row >= 1, pltpu.roll(h_ref[...], 1, 0), 0.0)
        da = g * hprev
        dmult = g * (ig * xr)
        dig = g * mult * xr
        dxr = g * mult * ig
        dlog_a = a * (da - dmult * a / mult)
        dr = dlog_a * (RG_C * ls)
        dls = RG_C * jnp.sum(dlog_a * r, axis=0, keepdims=True)
        dpr = dr * r * (1.0 - r)
        dpi = dig * ig * (1.0 - ig)
        dprb = dpr.astype(BF16)
        dpib = dpi.astype(BF16)
        dxr = dxr + _dot_nt(dprb, wa_ref[...]) + _dot_nt(dpib, wi_ref[...])
        dwa_ref[...] = _dot_tn(xb, dprb)
        dwi_ref[...] = _dot_tn(xb, dpib)
        extd_ref[0:t, :] = dxr
        extd_ref[t:, :] = jnp.zeros((8, LANES), F32)
        dux = rw_ref[RNN_CONV_W - 1:RNN_CONV_W, :] * dxr
        for k in range(RNN_CONV_W - 1):
            d = RNN_CONV_W - 1 - k
            dux = dux + rw_ref[k:k + 1, :] * extd_ref[d:d + t, :]
        dux_ref[...] = dux.astype(BF16)
        for k in range(RNN_CONV_W):
            d = RNN_CONV_W - 1 - k
            sm_ref[k:k + 1, :] = jnp.sum(dxr * ext_ref[8 - d:8 - d + t, :], axis=0, keepdims=True)
        sm_ref[4:5, :] = jnp.sum(dxr, axis=0, keepdims=True)
        sm_ref[5:6, :] = jnp.sum(dpr, axis=0, keepdims=True)
        sm_ref[6:7, :] = jnp.sum(dpi, axis=0, keepdims=True)
        sm_ref[7:8, :] = dls * _sigmoid(-rvec_ref[3:4, :])

    blk = lambda off: pl.BlockSpec((t, LANES), lambda c: (0, off + c))
    sq = pl.BlockSpec((None, LANES, LANES), lambda c: (c, 0, 0))
    return _hosted_call(
        body, host, name, (nblk,),
        [blk(2 * nblk), blk(0), blk(0),
         pl.BlockSpec((8, LANES), lambda c: (0, c)),
         pl.BlockSpec((8, LANES), lambda c: (0, c)), sq, sq],
        [blk(0), pl.BlockSpec((8, LANES), lambda c: (0, c)), sq, sq],
        [jax.ShapeDtypeStruct((t, nblk * LANES), BF16), jax.ShapeDtypeStruct((8, nblk * LANES), F32),
         jax.ShapeDtypeStruct((nblk, LANES, LANES), F32), jax.ShapeDtypeStruct((nblk, LANES, LANES), F32)],
        [pltpu.VMEM((t + 8, LANES), F32), pltpu.VMEM((t + 8, LANES), F32)],
        ("arbitrary",), 60, (proj, hr, dhr, rw8, rvec, wab, wib))


def _mixin_bwd(dxo, x, parts, vec, win, f, nvec, tm, name):
    t, d = x.shape
    nb, _, cb = win.shape
    dc = parts[0].shape[1]
    per = dc // cb
    nt = t // tm

    def body(dxo_ref, x_ref, p0, p1, p2, p3, vec_ref, w_ref, f_ref, nvec_ref, dx_ref, dvec_ref, df_ref, dgt_ref):
        i = pl.program_id(0)

        @pl.when(i == 0)
        def _():
            dvec_ref[...] = jnp.zeros_like(dvec_ref)
            dgt_ref[...] = jnp.zeros_like(dgt_ref)

        prefs = (p0, p1, p2, p3)
        dh = jnp.zeros((tm, d), F32)
        for k in range(nb):
            dh = dh + _dot_nt(prefs[k // per][:, (k % per) * cb:(k % per + 1) * cb], w_ref[k])
        dx = dxo_ref[...] + _adaln_bwd(x_ref[...], dh, vec_ref, dvec_ref)
        dx_ref[...] = dx
        _emit_df(dx, f_ref, nvec_ref, df_ref, dgt_ref)

        @pl.when(i == nt - 1)
        def _():
            _adaln_finish(vec_ref, dvec_ref)

    tile_d = pl.BlockSpec((tm, d), lambda i: (i, 0))
    tile_c = pl.BlockSpec((tm, dc), lambda i: (i, 0))
    tab = pl.BlockSpec((8, d), lambda i: (0, 0))
    return pl.pallas_call(
        body, name=name, grid=(nt,),
        in_specs=[tile_d, tile_d, tile_c, tile_c, tile_c, tile_c, tab,
                  pl.BlockSpec((nb, d, cb), lambda i: (0, 0, 0)), tile_d, tab],
        out_specs=[tile_d, tab, tile_d, tab],
        out_shape=[jax.ShapeDtypeStruct((t, d), F32), jax.ShapeDtypeStruct((8, d), F32),
                   jax.ShapeDtypeStruct((t, d), BF16), jax.ShapeDtypeStruct((8, d), F32)],
        compiler_params=_cp(("arbitrary",), 48),
    )(dxo, x, *parts, vec, win, f, nvec)


def _coords():
    return lax.axis_index("x"), lax.axis_index("y"), lax.axis_index("c")


def _flip(v, bit):
    return 1 - v if bit else v


def _gather_copy(outs, send_sems, recv_sems, a, k, block, to, src=None):
    dst = outs[a].at[block]
    return pltpu.make_async_remote_copy(
        src_ref=dst if src is None else src, dst_ref=dst,
        send_sem=send_sems.at[a, k], recv_sem=recv_sems.at[a, k],
        device_id=to, device_id_type=MESH_IDS)


def _gather_start(ins, outs, send_sems, recv_sems, loc_sems):
    x, y, c = _coords()
    me = 4 * x + 2 * y + c
    for a in range(len(ins)):
        pltpu.make_async_copy(ins[a], outs[a].at[me], loc_sems.at[a]).start()
    for a in range(len(ins)):
        _gather_copy(outs, send_sems, recv_sems, a, 0, me, (x, y, 1 - c), src=ins[a]).start()
        for j, (cx, cy) in enumerate([(1 - x, y), (x, 1 - y), (1 - x, 1 - y)]):
            _gather_copy(outs, send_sems, recv_sems, a, 1 + j, me, (cx, cy, c), src=ins[a]).start()


def _gather_finish(ins, outs, send_sems, recv_sems, loc_sems):
    x, y, c = _coords()
    me = 4 * x + 2 * y + c
    sib = (x, y, 1 - c)
    chips = [(1 - x, y), (x, 1 - y), (1 - x, 1 - y)]
    n = len(ins)
    for a in range(n):
        for j, (cx, cy) in enumerate(chips):
            blk = 4 * cx + 2 * cy + c
            _gather_copy(outs, send_sems, recv_sems, a, 1 + j, blk, sib).wait_recv()
            _gather_copy(outs, send_sems, recv_sems, a, 4 + j, blk, sib).start()
    for a in range(n):
        _gather_copy(outs, send_sems, recv_sems, a, 0, 4 * x + 2 * y + (1 - c), sib).wait_recv()
        for j, (cx, cy) in enumerate(chips):
            _gather_copy(outs, send_sems, recv_sems, a, 4 + j, 4 * cx + 2 * cy + (1 - c), sib).wait_recv()
    for a in range(n):
        _gather_copy(outs, send_sems, recv_sems, a, 0, me, sib, src=ins[a]).wait_send()
        for j, (cx, cy) in enumerate(chips):
            _gather_copy(outs, send_sems, recv_sems, a, 1 + j, me, (cx, cy, c), src=ins[a]).wait_send()
            _gather_copy(outs, send_sems, recv_sems, a, 4 + j, 4 * cx + 2 * cy + c, sib).wait_send()
        pltpu.make_async_copy(ins[a], outs[a].at[me], loc_sems.at[a]).wait()


def _gather_shapes(shards):
    return [jax.ShapeDtypeStruct((NDEV,) + s.shape, s.dtype) for s in shards]


def _gather_sems(n):
    return [pltpu.SemaphoreType.DMA((n, 7)), pltpu.SemaphoreType.DMA((n, 7)), pltpu.SemaphoreType.DMA((n,))]


def _sibling_copies(ins, outs, send_sems, recv_sems):
    x, y, c = _coords()
    return [pltpu.make_async_remote_copy(
        src_ref=ins[a].at[2 * q + (1 - c)], dst_ref=outs[a].at[q],
        send_sem=send_sems.at[a, q], recv_sem=recv_sems.at[a, q],
        device_id=(x, y, 1 - c), device_id_type=MESH_IDS) for a in range(len(ins)) for q in range(4)]


def _sibling_shapes(parts):
    return [jax.ShapeDtypeStruct((4,) + p.shape[1:], p.dtype) for p in parts]


def _chips_copies(ins, outs, send_sems, recv_sems):
    x, y, c = _coords()
    copies = []
    for a in range(len(ins)):
        for k, (kx, ky) in enumerate([(1, 0), (0, 1), (1, 1)]):
            tx, ty = _flip(x, kx), _flip(y, ky)
            copies.append(pltpu.make_async_remote_copy(
                src_ref=ins[a].at[2 * tx + ty], dst_ref=outs[a].at[k],
                send_sem=send_sems.at[a, k], recv_sem=recv_sems.at[a, k],
                device_id=(tx, ty, c), device_id_type=MESH_IDS))
    return copies


def _chips_shapes(sums):
    return [jax.ShapeDtypeStruct((3,) + s.shape[1:], s.dtype) for s in sums]


def _direct_copies(ins, outs, send_sems, recv_sems):
    x, y, c = _coords()
    me = 4 * x + 2 * y + c
    copies = []
    for a in range(len(ins)):
        for k in range(1, NDEV):
            kx, ky, kc = (k >> 2) & 1, (k >> 1) & 1, k & 1
            copies.append(pltpu.make_async_remote_copy(
                src_ref=ins[a], dst_ref=outs[a].at[me],
                send_sem=send_sems.at[a, k - 1], recv_sem=recv_sems.at[a, k - 1],
                device_id=(_flip(x, kx), _flip(y, ky), _flip(c, kc)), device_id_type=MESH_IDS))
    return copies


class _Exchange:
    def __init__(self, kind, arrays):
        self.kind, self.arrays, self.n = kind, list(arrays), len(arrays)
        self.nsem = 3 if kind in ("gather", "direct") else 2

    def out_shapes(self):
        return {"gather": _gather_shapes, "direct": _gather_shapes, "sibling": _sibling_shapes,
                "chips": _chips_shapes}[self.kind](self.arrays)

    def sems(self):
        if self.kind in ("gather", "direct"):
            return _gather_sems(self.n)
        k = {"sibling": 4, "chips": 3}[self.kind]
        return [pltpu.SemaphoreType.DMA((self.n, k)), pltpu.SemaphoreType.DMA((self.n, k))]

    def _copies(self, ins, outs, sems):
        if self.kind == "direct":
            x, y, c = _coords()
            own = [pltpu.make_async_copy(ins[a], outs[a].at[4 * x + 2 * y + c], sems[2].at[a]) for a in range(self.n)]
            return own + _direct_copies(ins, outs, sems[0], sems[1])
        return {"sibling": _sibling_copies, "chips": _chips_copies}[self.kind](ins, outs, *sems)

    def start(self, ins, outs, sems):
        if self.kind == "gather":
            _gather_start(ins, outs, *sems)
        else:
            for cpy in self._copies(ins, outs, sems):
                cpy.start()

    def finish(self, ins, outs, sems):
        if self.kind == "gather":
            _gather_finish(ins, outs, *sems)
        else:
            for cpy in self._copies(ins, outs, sems):
                cpy.wait()


class _Exchanges:
    def __init__(self, *parts):
        self.parts = parts
        self.arrays = [a for p in parts for a in p.arrays]
        self.n = len(self.arrays)

    def out_shapes(self):
        return [s for p in self.parts for s in p.out_shapes()]

    def sems(self):
        return [s for p in self.parts for s in p.sems()]

    def _each(self, ins, outs, sems):
        a = s = 0
        for p in self.parts:
            yield p, ins[a:a + p.n], outs[a:a + p.n], sems[s:s + p.nsem]
            a, s = a + p.n, s + p.nsem

    def start(self, ins, outs, sems):
        for p, i, o, s in self._each(ins, outs, sems):
            p.start(i, o, s)

    def finish(self, ins, outs, sems):
        for p, i, o, s in self._each(ins, outs, sems):
            p.finish(i, o, s)


def _hosted_call(body, host, name, grid, in_specs, out_specs, out_shape, scratch, sem, vmem_mb, args):
    n = host.n if host else 0
    ni, no, ns = len(in_specs), len(out_specs), len(scratch)

    def full(*refs):
        ins, h_in = refs[:ni], refs[ni:ni + n]
        outs, h_out = refs[ni + n:ni + n + no], refs[ni + n + no:ni + 2 * n + no]
        scr, sems = refs[ni + 2 * n + no:ni + 2 * n + no + ns], refs[ni + 2 * n + no + ns:]
        if host and grid:
            first = functools.reduce(lambda a, b: a & b, [pl.program_id(k) == 0 for k in range(len(grid))])
            last = functools.reduce(lambda a, b: a & b, [pl.program_id(k) == g - 1 for k, g in enumerate(grid)])

            @pl.when(first)
            def _():
                host.start(h_in, h_out, sems)
        elif host:
            host.start(h_in, h_out, sems)

        body(ins, outs, scr)

        if host and grid:
            @pl.when(last)
            def _():
                host.finish(h_in, h_out, sems)
        elif host:
            host.finish(h_in, h_out, sems)

    anyspec = pl.BlockSpec(memory_space=pl.ANY)
    return pl.pallas_call(
        full, name=name, grid=grid,
        in_specs=list(in_specs) + [anyspec] * n, out_specs=list(out_specs) + [anyspec] * n,
        out_shape=list(out_shape) + (host.out_shapes() if host else []),
        scratch_shapes=list(scratch) + (host.sems() if host else []),
        compiler_params=_cp(sem, vmem_mb),
    )(*args, *(host.arrays if host else []))


def _exchange(host, name):
    def body(*refs):
        n = host.n
        host.start(refs[:n], refs[n:2 * n], refs[2 * n:])
        host.finish(refs[:n], refs[n:2 * n], refs[2 * n:])

    anyspec = pl.BlockSpec(memory_space=pl.ANY)
    return pl.pallas_call(
        body, name=name, in_specs=[anyspec] * host.n, out_specs=[anyspec] * host.n,
        out_shape=host.out_shapes(), scratch_shapes=host.sems(),
    )(*host.arrays)


def _chip_sum(part, recv, sel, tr, name):
    _, _, r, c = part.shape

    def body(sel_ref, p_ref, r_ref, cs_ref, own_ref):
        q = pl.program_id(1)
        s = p_ref[...].astype(F32) + r_ref[...].astype(F32)
        cs_ref[...] = s.astype(BF16)

        @pl.when(q == sel_ref[1])
        def _():
            own_ref[...] = s

    return pl.pallas_call(
        body, name=name,
        grid_spec=pltpu.PrefetchScalarGridSpec(
            num_scalar_prefetch=1, grid=(r // tr, 4),
            in_specs=[pl.BlockSpec((None, None, tr, c), lambda i, q, s: (q, s[0], i, 0)),
                      pl.BlockSpec((None, tr, c), lambda i, q, s: (q, i, 0))],
            out_specs=[pl.BlockSpec((None, tr, c), lambda i, q, s: (q, i, 0)),
                       pl.BlockSpec((tr, c), lambda i, q, s: (i, 0))]),
        out_shape=[jax.ShapeDtypeStruct((4, r, c), BF16), jax.ShapeDtypeStruct((r, c), F32)],
        compiler_params=_cp(("arbitrary", "arbitrary"), 48),
    )(sel, part, recv)


def _gather_direct(src_ref, buf_ref, send_sems, recv_sems):
    x, y, c = _coords()
    me = 4 * x + 2 * y + c
    buf_ref[me] = src_ref[...]
    copies = []
    for k in range(1, NDEV):
        kx, ky, kc = (k >> 2) & 1, (k >> 1) & 1, k & 1
        copies.append(pltpu.make_async_remote_copy(
            src_ref=src_ref, dst_ref=buf_ref.at[me],
            send_sem=send_sems.at[k - 1], recv_sem=recv_sems.at[k - 1],
            device_id=(_flip(x, kx), _flip(y, ky), _flip(c, kc)), device_id_type=MESH_IDS))
    for cpy in copies:
        cpy.start()
    for k in range(1, NDEV):
        kx, ky, kc = (k >> 2) & 1, (k >> 1) & 1, k & 1
        peer = 4 * _flip(x, kx) + 2 * _flip(y, ky) + _flip(c, kc)
        pltpu.make_async_remote_copy(
            src_ref=src_ref, dst_ref=buf_ref.at[peer],
            send_sem=send_sems.at[k - 1], recv_sem=recv_sems.at[k - 1],
            device_id=(x, y, c), device_id_type=MESH_IDS).wait_recv()
    for cpy in copies:
        cpy.wait_send()
    return me


def _mod_exchange(c_row, wmod, bmod, wfmod, bfmod, name, host=None):
    d = c_row.shape[1]
    nm, nf = wmod.shape[1], wfmod.shape[1]
    nw = nm + nf

    def body(ins, outs, scr):
        c_ref, wm_ref, bm_ref, wf_ref, bf_ref = ins
        cs_ref, mod_ref, fmod_ref = outs
        slab_ref, csbuf_ref, mslab_ref, mbuf_ref, s1, r1, s2, r2 = scr
        cv = c_ref[...]
        slab_ref[...] = jnp.broadcast_to(cv * _sigmoid(cv), (8, d))
        _gather_direct(slab_ref, csbuf_ref, s1, r1)
        for b in range(NDEV):
            cs_ref[b:b + 1, :] = csbuf_ref[b, 0:1, :]
        cs = cs_ref[...]
        mslab_ref[:, 0:nm] = jnp.dot(cs, wm_ref[...], precision=HI, preferred_element_type=F32) + bm_ref[...]
        mslab_ref[:, nm:] = jnp.dot(cs, wf_ref[...], precision=HI, preferred_element_type=F32) + bf_ref[...]
        me = _gather_direct(mslab_ref, mbuf_ref, s2, r2)
        mine = lax.broadcasted_iota(jnp.int32, (8, nw), 0) == me
        for k in range(NDEV):
            rowk = jnp.sum(jnp.where(mine, mbuf_ref[k], 0.0), axis=0, keepdims=True)
            mod_ref[k:k + 1, :] = rowk[:, 0:nm]
            fmod_ref[k:k + 1, :] = rowk[:, nm:]

    vm = pl.BlockSpec(memory_space=pltpu.VMEM)
    return _hosted_call(
        body, host, name, (), [vm] * 5, [vm] * 3,
        [jax.ShapeDtypeStruct((NDEV, d), F32), jax.ShapeDtypeStruct((NDEV, nm), F32),
         jax.ShapeDtypeStruct((NDEV, nf), F32)],
        [pltpu.VMEM((8, d), F32), pltpu.VMEM((NDEV, 8, d), F32),
         pltpu.VMEM((8, nw), F32), pltpu.VMEM((NDEV, 8, nw), F32),
         pltpu.SemaphoreType.DMA((7,)), pltpu.SemaphoreType.DMA((7,)),
         pltpu.SemaphoreType.DMA((7,)), pltpu.SemaphoreType.DMA((7,))],
        None, 40, (c_row, wmod, bmod, wfmod, bfmod))


def _table_sum(tabs, name):
    n = len(tabs)

    def body(*refs):
        for a in range(n):
            tot = refs[a][0]
            for k in range(1, NDEV):
                tot = tot + refs[a][k]
            refs[n + a][...] = tot

    vm = pl.BlockSpec(memory_space=pltpu.VMEM)
    return pl.pallas_call(
        body, name=name, in_specs=[vm] * n, out_specs=[vm] * n,
        out_shape=[jax.ShapeDtypeStruct(tb.shape[1:], F32) for tb in tabs],
    )(*tabs)


def _adamw_math(w, g, m, v):
    m = ADAM_B1 * m + (1.0 - ADAM_B1) * g
    v = ADAM_B2 * v + (1.0 - ADAM_B2) * (g * g)
    m_hat = m / (1.0 - ADAM_B1 ** ADAM_STEP)
    v_hat = v / (1.0 - ADAM_B2 ** ADAM_STEP)
    delta = -ADAM_LR * (m_hat / (jnp.sqrt(v_hat) + ADAM_EPS) + ADAM_WD * w)
    return delta, m, v


def _adamw_small(w, g, m, v, name):
    def body(w_ref, g_ref, m_ref, v_ref, d_ref, mo_ref, vo_ref):
        d_ref[...], mo_ref[...], vo_ref[...] = _adamw_math(w_ref[...], g_ref[...], m_ref[...], v_ref[...])

    vm = pl.BlockSpec(memory_space=pltpu.VMEM)
    sds = jax.ShapeDtypeStruct(w.shape, F32)
    return pl.pallas_call(body, name=name, in_specs=[vm] * 4, out_specs=[vm] * 3,
                          out_shape=[sds, sds, sds])(w, g, m, v)


def _rs_final(own, recv, w, m, v, tr, name):
    r, c = own.shape

    def body(o_ref, r_ref, w_ref, m_ref, v_ref, g_ref, d_ref, mo_ref, vo_ref):
        g = o_ref[...] + r_ref[0].astype(F32) + r_ref[1].astype(F32) + r_ref[2].astype(F32)
        g_ref[...] = g
        d_ref[...], mo_ref[...], vo_ref[...] = _adamw_math(w_ref[...], g, m_ref[...], v_ref[...])

    tile = pl.BlockSpec((tr, c), lambda i: (i, 0))
    sds = jax.ShapeDtypeStruct((r, c), F32)
    return pl.pallas_call(
        body, name=name, grid=(r // tr,),
        in_specs=[tile, pl.BlockSpec((3, tr, c), lambda i: (0, i, 0)), tile, tile, tile],
        out_specs=[tile] * 4, out_shape=[sds] * 4,
        compiler_params=_cp(("arbitrary",), 48),
    )(own, recv, w, m, v)


def _mod_weight_update(cs, dm, w, m, v, tr, name):
    r, c = w.shape

    def body(cs_ref, dm_ref, w_ref, m_ref, v_ref, g_ref, d_ref, mo_ref, vo_ref):
        g = lax.dot_general(cs_ref[...], dm_ref[...], (((0,), (0,)), ((), ())),
                            precision=HI, preferred_element_type=F32)
        g_ref[...] = g
        d_ref[...], mo_ref[...], vo_ref[...] = _adamw_math(w_ref[...], g, m_ref[...], v_ref[...])

    tile = pl.BlockSpec((tr, c), lambda i: (i, 0))
    sds = jax.ShapeDtypeStruct((r, c), F32)
    return pl.pallas_call(
        body, name=name, grid=(r // tr,),
        in_specs=[pl.BlockSpec((NDEV, tr), lambda i: (0, i)), pl.BlockSpec((NDEV, c), lambda i: (0, 0)),
                  tile, tile, tile],
        out_specs=[tile] * 4, out_shape=[sds] * 4,
        compiler_params=_cp(("arbitrary",), 48),
    )(cs, dm, w, m, v)


def _rows(*vs):
    d = vs[0].shape[-1]
    rows = [v.reshape(1, d) for v in vs]
    return jnp.concatenate(rows + [jnp.zeros((8 - len(rows), d), F32)], axis=0)


def _block_diag_pairs(w):
    hd = w.shape[-1]
    z = jnp.zeros((w.shape[0] // 2, hd, hd), w.dtype)
    top = jnp.concatenate([w[0::2], z], axis=2)
    bot = jnp.concatenate([z, w[1::2]], axis=2)
    return jnp.concatenate([top, bot], axis=1).astype(BF16)


def _diag_pairs(g):
    hd = g.shape[-1] // 2
    both = jnp.stack([g[:, :hd, :hd], g[:, hd:, hd:]], axis=1)
    return both.reshape(2 * g.shape[0], hd, hd)


def kernel(x, c, w_mod, b_mod, g_ffn1, w_ffn1_in, w_ffn1_out, g_mix, w_in, conv_w, conv_b, ln_g, ln_b, rnn_conv_w, rnn_conv_b, w_a, b_a, w_i, b_i, lru_lambda, w_out, g_ffn2, w_ffn2_in, w_ffn2_out, w_fmod, b_fmod, g_final, loss_target, m_w_mod, m_b_mod, m_g_ffn1, m_w_ffn1_in, m_w_ffn1_out, m_g_mix, m_w_in, m_conv_w, m_conv_b, m_ln_g, m_ln_b, m_rnn_conv_w, m_rnn_conv_b, m_w_a, m_b_a, m_w_i, m_b_i, m_lru_lambda, m_w_out, m_g_ffn2, m_w_ffn2_in, m_w_ffn2_out, m_w_fmod, m_b_fmod, m_g_final, v_w_mod, v_b_mod, v_g_ffn1, v_w_ffn1_in, v_w_ffn1_out, v_g_mix, v_w_in, v_conv_w, v_conv_b, v_ln_g, v_ln_b, v_rnn_conv_w, v_rnn_conv_b, v_w_a, v_b_a, v_w_i, v_b_i, v_lru_lambda, v_w_out, v_g_ffn2, v_w_ffn2_in, v_w_ffn2_out, v_w_fmod, v_b_fmod, v_g_final):
    t, d = x.shape[1], x.shape[2]
    fb = w_ffn1_in.shape[2]
    nm = w_mod.shape[2]
    nf = w_fmod.shape[1]
    dc = conv_b.shape[1]
    cl = conv_w.shape[2]
    tm = min(TOKEN_TILE, t)
    tk = min(WGRAD_TILE, t)
    nk = t // tk
    me = 4 * lax.axis_index("x") + 2 * lax.axis_index("y") + lax.axis_index("c")

    tr = jnp.transpose
    bmod_l = lax.dynamic_slice(b_mod, (0, me * nm), (1, nm))
    bfmod_l = lax.dynamic_slice(b_fmod.reshape(1, -1), (0, me * nf), (1, nf))
    cwl = jnp.concatenate([conv_w[0], jnp.zeros((1, cl), F32), rnn_conv_w[0], jnp.zeros((4, cl), F32)], axis=0)
    cs, mod_rows, fmod_rows, wi1, wo1, cwg = _mod_exchange(
        c, w_mod[0], bmod_l, w_fmod, bfmod_l, "mod_and_gather_ffn1",
        host=_Exchange("gather", [tr(w_ffn1_in[0]).astype(BF16), w_ffn1_out[0].astype(BF16), cwl]))
    wi1 = wi1.reshape(2, 4, fb, d)
    wo1 = wo1.reshape(4 * fb, d)
    mod = mod_rows.reshape(9, d)
    fmod = fmod_rows.reshape(2, d)
    vec1 = _rows(g_ffn1, mod[0], mod[1], mod[2])
    vecm = _rows(g_mix, mod[3], mod[4], mod[5])
    vec3 = _rows(g_ffn2, mod[6], mod[7], mod[8])
    vecf = _rows(g_final, fmod[0], fmod[1])

    xin = x[0]
    later = [w_in[0].astype(BF16), w_out[0].astype(BF16), tr(w_ffn2_in[0]).astype(BF16), w_ffn2_out[0].astype(BF16)]
    x1, h1, gu1, f1, win, wout, wi2, wo2 = _ffn_fwd(xin, vec1, wi1, wo1, tm, "ffn1_fwd",
                                                    host=_Exchange("gather", later))
    wi2 = wi2.reshape(2, 4, fb, d)
    wo2 = wo2.reshape(4 * fb, d)
    wout = wout.reshape(d, d)
    h2, proj = _mix_in(x1, vecm, win, tm, "mix_in")
    lnv = _rows(ln_g, ln_b)
    rvec = _rows(rnn_conv_b, b_a, b_i, lru_lambda)
    wab = _block_diag_pairs(w_a[0])
    wib = _block_diag_pairs(w_i[0])
    cwf = jnp.transpose(cwg, (1, 0, 2)).reshape(40, NDEV * cl)
    cw32 = jnp.concatenate([cwf[0:CONV_W], conv_b], axis=0)
    rw8 = cwf[32:40]

    cv = _conv_fwd(proj, cw32, "conv_fwd")
    hr, yr = _rnn_fwd(proj, rw8, rvec, wab, wib, "rnn_fwd")
    x2, ym, ycat = _mix_out(x1, cv, yr, vecm, lnv, wout, tm, "mix_out")
    x3, h3, gu3, f3 = _ffn_fwd(x2, vec3, wi2, wo2, tm, "ffn2_fwd")

    dx3, dvf, df3, dva3 = _final(x3, loss_target[0], vecf, f3, vec3, tm, "final_loss")
    a_tok = lambda width: pl.BlockSpec((tk, width), lambda k, s: (s, 0))
    blk3 = lambda width: pl.BlockSpec((None, tk, width), lambda k, s: (k, s, 0))
    sel = jnp.stack([lax.axis_index("c"), 2 * lax.axis_index("x") + lax.axis_index("y")]).astype(jnp.int32)
    row_tile = {"w_ffn1_in": fb // 4, "w_ffn1_out": fb // 4, "w_in": 512, "w_out": 128,
                "w_ffn2_in": fb // 4, "w_ffn2_out": fb // 4}

    def chip_sums(names, partials, from_sib):
        out = [_chip_sum(p.reshape((4, 2) + p.shape[1:]), r, sel, row_tile[nm_], "chip_sum_" + nm_)
               for nm_, p, r in zip(names, partials, from_sib)]
        return [o[0] for o in out], [o[1] for o in out]

    dgu3, p_wi2, p_wo2 = _ffn_bwd_w(df3, gu3, h3, wo2, tm, "ffn2_bwd_w")
    p_wi2 = p_wi2.reshape(NDEV, fb, d)
    p_wo2 = p_wo2.reshape(NDEV, fb // 2, d)
    names2 = ["w_ffn2_in", "w_ffn2_out"]
    dx2, dv3, sib_wi2, sib_wo2 = _ffn_bwd_in(dx3, x2, vec3, dgu3, wi2, tm, "ffn2_bwd_in",
                                             host=_Exchange("sibling", [p_wi2, p_wo2]))
    (s_wi2, s_wo2), owns2 = chip_sums(names2, [p_wi2, p_wo2], [sib_wi2, sib_wo2])
    dym, dcv, dhr, duy, dln, dgt2, r_wo2 = _mixout_bwd(
        dx2, ym, cv, hr, proj, vecm, lnv, wout, tm, "mixout_bwd", host=_Exchange("chips", [s_wo2]))
    dval, dgate, dcw = _conv_bwd(proj, dcv, cw32, "conv_bwd")
    dux, rsm, dwab, dwib, r_wi2 = _rnn_bwd(proj, hr, dhr, rw8, rvec, wab, wib, "rnn_bwd",
                                           host=_Exchange("chips", [s_wi2]))
    parts = [dval, dgate, dux, duy]
    dx1, dvm, df1, dva1 = _mixin_bwd(dx2, x1, parts, vecm, win, f1, vec1, tm, "mixin_bwd")
    p_wout = _mm_tn(ycat, dym, pl.BlockSpec((tk, LANES), lambda k, s: (s, k)), a_tok(d),
                    NDEV, nk, LANES, d, "wgrad_out")
    p_win = _wgrad_in(h2, parts, win.shape[2], tk, "wgrad_in")
    namesm = ["w_in", "w_out"]
    sumsm, ownsm = chip_sums(namesm, [p_win, p_wout],
                             _exchange(_Exchange("sibling", [p_win, p_wout]), "rs_sibling_mix"))
    lane_pad = lambda v: jnp.concatenate([v, jnp.zeros_like(v)], axis=1)
    early = jnp.concatenate([dva3, dv3, dvf, dvm, dgt2, dva1, dcw.reshape(16, d), lane_pad(dln), lane_pad(rsm),
                             _diag_pairs(dwab).reshape(32, d), _diag_pairs(dwib).reshape(32, d)], axis=0)
    dgu1, p_wi1, p_wo1, r_win, r_wout, all_early = _ffn_bwd_w(
        df1, gu1, h1, wo1, tm, "ffn1_bwd_w",
        host=_Exchanges(_Exchange("chips", sumsm), _Exchange("direct", [early])))
    p_wi1 = p_wi1.reshape(NDEV, fb, d)
    p_wo1 = p_wo1.reshape(NDEV, fb // 2, d)
    names1 = ["w_ffn1_in", "w_ffn1_out"]
    sums1, owns1 = chip_sums(names1, [p_wi1, p_wo1],
                             _exchange(_Exchange("sibling", [p_wi1, p_wo1]), "rs_sibling_ffn1"))
    dx0, dv1, r_wi1, r_wo1 = _ffn_bwd_in(dx1, xin, vec1, dgu1, wi1, tm, "ffn1_bwd_in",
                                         host=_Exchange("chips", sums1))
    from_chips = {"w_in": r_win, "w_out": r_wout, "w_ffn2_in": r_wi2, "w_ffn2_out": r_wo2,
                  "w_ffn1_in": r_wi1, "w_ffn1_out": r_wo1}
    owns = dict(zip(namesm + names2 + names1, ownsm + owns2 + owns1))

    big = {"w_ffn1_in": (tr(w_ffn1_in[0]), tr(m_w_ffn1_in[0]), tr(v_w_ffn1_in[0])),
           "w_ffn1_out": (w_ffn1_out[0], m_w_ffn1_out[0], v_w_ffn1_out[0]),
           "w_in": (w_in[0], m_w_in[0], v_w_in[0]), "w_out": (w_out[0], m_w_out[0], v_w_out[0]),
           "w_ffn2_in": (tr(w_ffn2_in[0]), tr(m_w_ffn2_in[0]), tr(v_w_ffn2_in[0])),
           "w_ffn2_out": (w_ffn2_out[0], m_w_ffn2_out[0], v_w_ffn2_out[0])}
    res = {}
    for nm_ in namesm + names2 + names1:
        out4 = _rs_final(owns[nm_], from_chips[nm_], *big[nm_], row_tile[nm_], "rs_final_" + nm_)
        if nm_ in ("w_ffn1_in", "w_ffn2_in"):
            out4 = [tr(o) for o in out4]
        res[nm_] = [o[None] for o in out4]

    (all_late,) = _exchange(_Exchange("direct", [dv1]), "late_table")
    te, tl = _table_sum([all_early, all_late], "table_sum")
    loss = jnp.sum(te[20])

    mod_rows_of = lambda e, l: [l[1], l[3], e[42], e[25], e[27], e[32], e[9], e[11], e[2]]
    dm_all = jnp.concatenate(mod_rows_of(jnp.swapaxes(all_early, 0, 1), jnp.swapaxes(all_late, 0, 1)), axis=1)
    dfm_all = jnp.concatenate([all_early[:, 17], all_early[:, 19]], axis=1)
    dm_l = lax.dynamic_slice(dm_all, (0, me * nm), (NDEV, nm))
    dfm_l = lax.dynamic_slice(dfm_all, (0, me * nf), (NDEV, nf))
    out_wmod = [o[None] for o in _mod_weight_update(cs, dm_l, w_mod[0], m_w_mod[0], v_w_mod[0], 256, "w_mod_update")]
    out_wfmod = _mod_weight_update(cs, dfm_l, w_fmod, m_w_fmod, v_w_fmod, 256, "w_fmod_update")
    res["w_mod"] = out_wmod
    res["w_fmod"] = list(out_wfmod)

    dcw_f = te[48:64].reshape(32, dc)
    rsm_f = te[72:80, 0:dc]
    small_grads = {
        "b_mod": jnp.concatenate(mod_rows_of(te, tl)).reshape(1, 9 * d),
        "b_fmod": jnp.concatenate([te[17], te[19]]),
        "g_ffn1": tl[0:1], "g_mix": te[24:25], "g_ffn2": te[8:9], "g_final": te[16],
        "conv_w": lax.dynamic_slice(dcw_f, (0, me * cl), (CONV_W, cl))[None],
        "conv_b": dcw_f[31:32],
        "ln_g": te[64:65, 0:dc], "ln_b": te[65:66, 0:dc],
        "rnn_conv_w": lax.dynamic_slice(rsm_f, (0, me * cl), (RNN_CONV_W, cl))[None],
        "rnn_conv_b": rsm_f[4:5], "b_a": rsm_f[5:6], "b_i": rsm_f[6:7], "lru_lambda": rsm_f[7:8],
        "w_a": te[80:112].reshape(w_a.shape), "w_i": te[112:144].reshape(w_i.shape),
    }
    small_params = {
        "b_mod": (b_mod, m_b_mod, v_b_mod), "b_fmod": (b_fmod, m_b_fmod, v_b_fmod),
        "g_ffn1": (g_ffn1, m_g_ffn1, v_g_ffn1), "g_mix": (g_mix, m_g_mix, v_g_mix),
        "g_ffn2": (g_ffn2, m_g_ffn2, v_g_ffn2), "g_final": (g_final, m_g_final, v_g_final),
        "conv_w": (conv_w, m_conv_w, v_conv_w), "conv_b": (conv_b, m_conv_b, v_conv_b),
        "ln_g": (ln_g, m_ln_g, v_ln_g), "ln_b": (ln_b, m_ln_b, v_ln_b),
        "rnn_conv_w": (rnn_conv_w, m_rnn_conv_w, v_rnn_conv_w),
        "rnn_conv_b": (rnn_conv_b, m_rnn_conv_b, v_rnn_conv_b),
        "w_a": (w_a, m_w_a, v_w_a), "b_a": (b_a, m_b_a, v_b_a),
        "w_i": (w_i, m_w_i, v_w_i), "b_i": (b_i, m_b_i, v_b_i),
        "lru_lambda": (lru_lambda, m_lru_lambda, v_lru_lambda),
    }
    for name, g in small_grads.items():
        w, m, v = small_params[name]
        shp = w.shape
        two_d = (-1, shp[-1]) if w.ndim > 1 else (1, shp[0])
        outs = _adamw_small(w.reshape(two_d), g.reshape(two_d), m.reshape(two_d), v.reshape(two_d),
                            "adamw_" + name)
        res[name] = [g.reshape(shp)] + [o.reshape(shp) for o in outs]

    order = ["w_mod", "b_mod", "g_ffn1", "w_ffn1_in", "w_ffn1_out", "g_mix", "w_in", "conv_w", "conv_b",
             "ln_g", "ln_b", "rnn_conv_w", "rnn_conv_b", "w_a", "b_a", "w_i", "b_i", "lru_lambda", "w_out",
             "g_ffn2", "w_ffn2_in", "w_ffn2_out", "w_fmod", "b_fmod", "g_final"]
    return (loss, dx0[None], *[res[n][0] for n in order], *[res[n][1] for n in order],
            *[res[n][2] for n in order], *[res[n][3] for n in order])
```

```python
import functools
import math

import jax
import jax.numpy as jnp
from jax import lax
from jax.experimental import pallas as pl
from jax.experimental.pallas import tpu as pltpu

F32 = jnp.float32
BF16 = jnp.bfloat16
MESH_IDS = pl.DeviceIdType.MESH
NDEV = 8
EPS = 1e-6
RG_C = 8.0
CONV_W = 31
RNN_CONV_W = 4
LANES = 128
ADAM_LR = 0.001
ADAM_B1 = 0.9
ADAM_B2 = 0.999
ADAM_EPS = 1e-08
ADAM_WD = 0.01
ADAM_STEP = 10
SMALL_ROWS = 104
TOKEN_TILE = 512
WGRAD_TILE = 2048
ROW_GROUP = 16
HI = lax.Precision.HIGHEST


def _cp(sem, vmem_mb):
    return pltpu.CompilerParams(dimension_semantics=sem, vmem_limit_bytes=vmem_mb * 1024 * 1024)


def _dot(a, b):
    return jnp.dot(a, b, preferred_element_type=F32)


def _dot_nt(a, b):
    return lax.dot_general(a, b, (((1,), (1,)), ((), ())), preferred_element_type=F32)


def _dot_tn(a, b):
    return lax.dot_general(a, b, (((0,), (0,)), ((), ())), preferred_element_type=F32)


def _sigmoid(x):
    return 1.0 / (1.0 + jnp.exp(-x))


def _adaln(x, vec_ref):
    rstd = lax.rsqrt(jnp.mean(x * x, axis=-1, keepdims=True) + EPS)
    return (x * rstd) * vec_ref[0:1, :] * (1.0 + vec_ref[2:3, :]) + vec_ref[1:2, :]


def _adaln_bwd(x, dh, vec_ref, dvec_ref):
    rstd = lax.rsqrt(jnp.mean(x * x, axis=-1, keepdims=True) + EPS)
    xhat = x * rstd
    dvec_ref[0:1, :] += jnp.sum(dh * xhat, axis=0, keepdims=True)
    dvec_ref[1:2, :] += jnp.sum(dh, axis=0, keepdims=True)
    dxhat = dh * (vec_ref[0:1, :] * (1.0 + vec_ref[2:3, :]))
    return rstd * (dxhat - xhat * jnp.mean(dxhat * xhat, axis=-1, keepdims=True))


def _adaln_finish(vec_ref, dvec_ref):
    s = dvec_ref[0:1, :]
    dvec_ref[3:4, :] = vec_ref[0:1, :] * s
    dvec_ref[0:1, :] = (1.0 + vec_ref[2:3, :]) * s


def _gelu_and_grad(x):
    k0 = math.sqrt(2.0 / math.pi)
    x2 = x * x
    t = jnp.tanh(k0 * (x + 0.044715 * x * x2))
    g = 0.5 * x * (1.0 + t)
    dg = 0.5 * (1.0 + t) + 0.5 * x * (1.0 - t * t) * (k0 * (1.0 + 3.0 * 0.044715 * x2))
    return g, dg


def _log_sigmoid(x):
    z = jnp.exp(-jnp.abs(x))
    u = 1.0 + z
    d = u - 1.0
    log1p = jnp.where(d == 0.0, z, jnp.log(u) * (z / jnp.where(d == 0.0, 1.0, d)))
    return jnp.minimum(x, 0.0) - log1p


def _neg_expm1(x):
    series = -x * (1.0 + x * (0.5 + x * (1.0 / 6.0 + x * (1.0 / 24.0 + x * (1.0 / 120.0)))))
    return jnp.where(x > -0.05, series, 1.0 - jnp.exp(x))


def _scan_fwd(a, b):
    n = a.shape[0]
    row = lax.broadcasted_iota(jnp.int32, a.shape, 0)
    s = 1
    while s < n:
        ok = row >= s
        b = a * jnp.where(ok, pltpu.roll(b, s, 0), 0.0) + b
        if 2 * s < n:
            a = a * jnp.where(ok, pltpu.roll(a, s, 0), 1.0)
        s *= 2
    return b


def _scan_rev(a, d):
    n = a.shape[0]
    row = lax.broadcasted_iota(jnp.int32, a.shape, 0)
    s = 1
    while s < n:
        ok = row < n - s
        d = a * jnp.where(ok, pltpu.roll(d, n - s, 0), 0.0) + d
        if 2 * s < n:
            a = a * jnp.where(ok, pltpu.roll(a, n - s, 0), 1.0)
        s *= 2
    return d


def _rglru_gates(xr, wa_ref, wi_ref, rvec_ref):
    xb = xr.astype(BF16)
    r = _sigmoid(_dot(xb, wa_ref[...]) + rvec_ref[1:2, :])
    ig = _sigmoid(_dot(xb, wi_ref[...]) + rvec_ref[2:3, :])
    ls = _log_sigmoid(rvec_ref[3:4, :])
    log_a = RG_C * r * ls
    a = jnp.exp(log_a)
    mult = jnp.sqrt(_neg_expm1(2.0 * log_a))
    return xb, r, ig, ls, a, mult


def _rnn_conv(ux, rw_ref, rvec_ref, ext_ref):
    t = ux.shape[0]
    ext_ref[0:8, :] = jnp.zeros((8, ux.shape[1]), F32)
    ext_ref[8:, :] = ux
    xr = rvec_ref[0:1, :] + rw_ref[RNN_CONV_W - 1:RNN_CONV_W, :] * ux
    for k in range(RNN_CONV_W - 1):
        d = RNN_CONV_W - 1 - k
        xr = xr + rw_ref[k:k + 1, :] * ext_ref[8 - d:8 - d + t, :]
    return xr


def _ffn_fwd(x, vec, wi, wo, tm, name, host=None):
    t, d = x.shape
    nj, fb = wi.shape[1], wi.shape[2]
    nt = t // tm

    def body(ins, outs, scr):
        x_ref, vec_ref, wi_ref, wo_ref = ins
        xo_ref, h_ref, gu_ref, f_ref = outs
        acc_ref, = scr
        j = pl.program_id(1)

        @pl.when(j == 0)
        def _():
            h_ref[...] = _adaln(x_ref[...], vec_ref).astype(BF16)
            acc_ref[...] = jnp.zeros_like(acc_ref)

        h = h_ref[...]
        gate = _dot_nt(h, wi_ref[0])
        up = _dot_nt(h, wi_ref[1])
        gu_ref[0] = gate.astype(BF16)
        gu_ref[1] = up.astype(BF16)
        act = (gate * _sigmoid(gate) * up).astype(BF16)
        acc_ref[...] += _dot(act, wo_ref[...])

        @pl.when(j == nj - 1)
        def _():
            f = acc_ref[...]
            f_ref[...] = f.astype(BF16)
            xo_ref[...] = x_ref[...] + 0.5 * vec_ref[3:4, :] * f

    tile = pl.BlockSpec((tm, d), lambda i, j: (i, 0))
    return _hosted_call(
        body, host, name, (nt, nj),
        [tile,
         pl.BlockSpec((8, d), lambda i, j: (0, 0)),
         pl.BlockSpec((2, None, fb, d), lambda i, j: (0, j, 0, 0)),
         pl.BlockSpec((fb, d), lambda i, j: (j, 0))],
        [tile, tile, pl.BlockSpec((2, None, tm, fb), lambda i, j: (0, j, i, 0)), tile],
        [jax.ShapeDtypeStruct((t, d), F32), jax.ShapeDtypeStruct((t, d), BF16),
         jax.ShapeDtypeStruct((2, nj, t, fb), BF16), jax.ShapeDtypeStruct((t, d), BF16)],
        [pltpu.VMEM((tm, d), F32)], ("arbitrary", "arbitrary"), 48, (x, vec, wi, wo))


def _mix_in(x, vec, win, tm, name, host=None):
    t, d = x.shape
    nb, _, cb = win.shape

    def body(ins, outs, scr):
        x_ref, vec_ref, w_ref = ins
        h_ref, p_ref = outs
        h = _adaln(x_ref[...], vec_ref).astype(BF16)
        h_ref[...] = h
        for k in range(nb):
            p_ref[:, k * cb:(k + 1) * cb] = _dot(h, w_ref[k])

    return _hosted_call(
        body, host, name, (t // tm,),
        [pl.BlockSpec((tm, d), lambda i: (i, 0)),
         pl.BlockSpec((8, d), lambda i: (0, 0)),
         pl.BlockSpec((nb, d, cb), lambda i: (0, 0, 0))],
        [pl.BlockSpec((tm, d), lambda i: (i, 0)),
         pl.BlockSpec((tm, nb * cb), lambda i: (i, 0))],
        [jax.ShapeDtypeStruct((t, d), BF16), jax.ShapeDtypeStruct((t, nb * cb), F32)],
        [], ("arbitrary",), 48, (x, vec, win))


def _conv_fwd(proj, cw32, name, host=None):
    t = proj.shape[0]
    nblk = cw32.shape[1] // LANES
    ch = min(t, 128)

    def body(ins, outs, scr):
        val_ref, gate_ref, cw_ref = ins
        cv_ref, = outs
        ext_ref, = scr
        ext_ref[0:32, :] = jnp.zeros((32, LANES), F32)
        ext_ref[32:, :] = val_ref[...] * _sigmoid(gate_ref[...])
        for r in range(t // ch):
            acc = jnp.broadcast_to(cw_ref[31:32, :], (ch, LANES))
            for k in range(CONV_W):
                off = 32 + r * ch - (CONV_W - 1 - k)
                acc = acc + cw_ref[k:k + 1, :] * ext_ref[off:off + ch, :]
            cv_ref[r * ch:(r + 1) * ch, :] = acc

    return _hosted_call(
        body, host, name, (nblk,),
        [pl.BlockSpec((t, LANES), lambda c: (0, c)),
         pl.BlockSpec((t, LANES), lambda c: (0, nblk + c)),
         pl.BlockSpec((32, LANES), lambda c: (0, c))],
        [pl.BlockSpec((t, LANES), lambda c: (0, c))],
        [jax.ShapeDtypeStruct((t, nblk * LANES), F32)],
        [pltpu.VMEM((t + 32, LANES), F32)], ("arbitrary",), 48, (proj, proj, cw32))


def _rnn_fwd(proj, rw8, rvec, wab, wib, name, host=None):
    t = proj.shape[0]
    nblk = rvec.shape[1] // LANES

    def body(ins, outs, scr):
        ux_ref, uy_ref, rw_ref, rvec_ref, wa_ref, wi_ref = ins
        h_ref, yr_ref = outs
        ext_ref, = scr
        xr = _rnn_conv(ux_ref[...], rw_ref, rvec_ref, ext_ref)
        _, _, ig, _, a, mult = _rglru_gates(xr, wa_ref, wi_ref, rvec_ref)
        h = _scan_fwd(a, mult * (ig * xr))
        h_ref[...] = h
        ge, _ = _gelu_and_grad(uy_ref[...])
        yr_ref[...] = (ge * h).astype(BF16)

    blk = lambda off: pl.BlockSpec((t, LANES), lambda c: (0, off + c))
    return _hosted_call(
        body, host, name, (nblk,),
        [blk(2 * nblk), blk(3 * nblk),
         pl.BlockSpec((8, LANES), lambda c: (0, c)),
         pl.BlockSpec((8, LANES), lambda c: (0, c)),
         pl.BlockSpec((None, LANES, LANES), lambda c: (c, 0, 0)),
         pl.BlockSpec((None, LANES, LANES), lambda c: (c, 0, 0))],
        [blk(0), blk(0)],
        [jax.ShapeDtypeStruct((t, nblk * LANES), F32), jax.ShapeDtypeStruct((t, nblk * LANES), BF16)],
        [pltpu.VMEM((t + 8, LANES), F32)], ("arbitrary",), 56, (proj, proj, rw8, rvec, wab, wib))


def _ln_silu(cv, lnv_ref):
    mu = jnp.mean(cv, axis=-1, keepdims=True)
    xc = cv - mu
    rs = lax.rsqrt(jnp.mean(xc * xc, axis=-1, keepdims=True) + EPS)
    chat = xc * rs
    z = chat * lnv_ref[0:1, :] + lnv_ref[1:2, :]
    sg = _sigmoid(z)
    return rs, chat, z, sg


def _mix_out(x, cv, yr, vec, lnv, wout, tm, name, host=None):
    t, d = x.shape
    dc = cv.shape[1]

    def body(ins, outs, scr):
        x_ref, cv_ref, yr_ref, vec_ref, lnv_ref, w_ref = ins
        xo_ref, ym_ref, yc_ref = outs
        _, _, z, sg = _ln_silu(cv_ref[...], lnv_ref)
        yc = (z * sg).astype(BF16)
        yr = yr_ref[...]
        yc_ref[:, 0:dc] = yc
        yc_ref[:, dc:] = yr
        ym = _dot(yc, w_ref[0:dc, :]) + _dot(yr, w_ref[dc:, :])
        ym_ref[...] = ym.astype(BF16)
        xo_ref[...] = x_ref[...] + vec_ref[3:4, :] * ym

    tile = pl.BlockSpec((tm, d), lambda i: (i, 0))
    return _hosted_call(
        body, host, name, (t // tm,),
        [tile,
         pl.BlockSpec((tm, dc), lambda i: (i, 0)),
         pl.BlockSpec((tm, dc), lambda i: (i, 0)),
         pl.BlockSpec((8, d), lambda i: (0, 0)),
         pl.BlockSpec((8, dc), lambda i: (0, 0)),
         pl.BlockSpec((d, d), lambda i: (0, 0))],
        [tile, tile, tile],
        [jax.ShapeDtypeStruct((t, d), F32), jax.ShapeDtypeStruct((t, d), BF16), jax.ShapeDtypeStruct((t, d), BF16)],
        [], ("arbitrary",), 48, (x, cv, yr, vec, lnv, wout))


def _final(x, tgt, vec, f, nvec, tm, name):
    t, d = x.shape
    nt = t // tm

    def body(x_ref, t_ref, vec_ref, f_ref, nvec_ref, dx_ref, dvec_ref, df_ref, dgt_ref):
        i = pl.program_id(0)

        @pl.when(i == 0)
        def _():
            dvec_ref[...] = jnp.zeros_like(dvec_ref)
            dgt_ref[...] = jnp.zeros_like(dgt_ref)

        xv = x_ref[...]
        e = _adaln(xv, vec_ref) - t_ref[...]
        dvec_ref[4:5, :] += (0.5 / d) * jnp.sum(e * e, axis=0, keepdims=True)
        dx = _adaln_bwd(xv, e * (1.0 / d), vec_ref, dvec_ref)
        dx_ref[...] = dx
        _emit_df(dx, f_ref, nvec_ref, df_ref, dgt_ref)

        @pl.when(i == nt - 1)
        def _():
            _adaln_finish(vec_ref, dvec_ref)

    tile = pl.BlockSpec((tm, d), lambda i: (i, 0))
    tab = pl.BlockSpec((8, d), lambda i: (0, 0))
    return pl.pallas_call(
        body, name=name, grid=(nt,),
        in_specs=[tile, tile, tab, tile, tab],
        out_specs=[tile, tab, tile, tab],
        out_shape=[jax.ShapeDtypeStruct((t, d), F32), jax.ShapeDtypeStruct((8, d), F32),
                   jax.ShapeDtypeStruct((t, d), BF16), jax.ShapeDtypeStruct((8, d), F32)],
        compiler_params=_cp(("arbitrary",), 48),
    )(x, tgt, vec, f, nvec)


def _emit_df(dx, f_ref, nvec_ref, df_ref, dgt_ref):
    df_ref[...] = (0.5 * nvec_ref[3:4, :] * dx).astype(BF16)
    dgt_ref[2:3, :] += 0.5 * jnp.sum(dx * f_ref[...].astype(F32), axis=0, keepdims=True)


def _ffn_bwd_w(df, gu, h, wo, tm, name, host=None):
    t, d = df.shape
    nj, fb = gu.shape[1], gu.shape[3]
    nt = t // tm
    sub = min(tm, ROW_GROUP)

    def body(ins, outs, scr):
        df_ref, gu_ref, h_ref, wo_ref = ins
        dgu_ref, dwi_ref, dwo_ref = outs
        accg_ref, accu_ref, acco_ref, dact_ref, act_ref = scr
        i = pl.program_id(1)

        @pl.when(i == 0)
        def _():
            accg_ref[...] = jnp.zeros_like(accg_ref)
            accu_ref[...] = jnp.zeros_like(accu_ref)
            acco_ref[...] = jnp.zeros_like(acco_ref)

        dact_ref[...] = _dot_nt(df_ref[...], wo_ref[...])
        for r in range(tm // sub):
            rows = slice(r * sub, (r + 1) * sub)
            g = gu_ref[0, rows, :].astype(F32)
            u = gu_ref[1, rows, :].astype(F32)
            dact = dact_ref[rows, :]
            sg = _sigmoid(g)
            sl = g * sg
            dgu_ref[0, rows, :] = (dact * u * (sg * (1.0 + g * (1.0 - sg)))).astype(BF16)
            dgu_ref[1, rows, :] = (dact * sl).astype(BF16)
            act_ref[rows, :] = (sl * u).astype(BF16)
        hb = h_ref[...]
        acco_ref[...] += _dot_tn(act_ref[...], df_ref[...])
        accg_ref[...] += _dot_tn(dgu_ref[0], hb)
        accu_ref[...] += _dot_tn(dgu_ref[1], hb)

        @pl.when(i == nt - 1)
        def _():
            dwi_ref[0] = accg_ref[...].astype(BF16)
            dwi_ref[1] = accu_ref[...].astype(BF16)
            dwo_ref[...] = acco_ref[...].astype(BF16)

    tile = pl.BlockSpec((tm, d), lambda j, i: (i, 0))
    return _hosted_call(
        body, host, name, (nj, nt),
        [tile,
         pl.BlockSpec((2, None, tm, fb), lambda j, i: (0, j, i, 0)),
         tile,
         pl.BlockSpec((fb, d), lambda j, i: (j, 0))],
        [pl.BlockSpec((2, None, tm, fb), lambda j, i: (0, j, i, 0)),
         pl.BlockSpec((2, None, fb, d), lambda j, i: (0, j, 0, 0)),
         pl.BlockSpec((None, fb, d), lambda j, i: (j, 0, 0))],
        [jax.ShapeDtypeStruct((2, nj, t, fb), BF16), jax.ShapeDtypeStruct((2, nj, fb, d), BF16),
         jax.ShapeDtypeStruct((nj, fb, d), BF16)],
        [pltpu.VMEM((fb, d), F32), pltpu.VMEM((fb, d), F32), pltpu.VMEM((fb, d), F32),
         pltpu.VMEM((tm, fb), F32), pltpu.VMEM((tm, fb), BF16)],
        ("arbitrary", "arbitrary"), 56, (df, gu, h, wo))


def _ffn_bwd_in(dxo, x, vec, dgu, wi, tm, name, host=None):
    t, d = x.shape
    nj, fb = wi.shape[1], wi.shape[2]
    nt = t // tm

    def body(ins, outs, scr):
        dxo_ref, x_ref, vec_ref, dgu_ref, wi_ref = ins
        dx_ref, dvec_ref = outs
        i = pl.program_id(0)

        @pl.when(i == 0)
        def _():
            dvec_ref[...] = jnp.zeros_like(dvec_ref)

        dh = jnp.zeros((tm, d), F32)
        for a in range(2):
            for k in range(nj):
                dh = dh + _dot(dgu_ref[a, k], wi_ref[a, k])
        dx_ref[...] = dxo_ref[...] + _adaln_bwd(x_ref[...], dh, vec_ref, dvec_ref)

        @pl.when(i == nt - 1)
        def _():
            _adaln_finish(vec_ref, dvec_ref)

    tile = pl.BlockSpec((tm, d), lambda i: (i, 0))
    return _hosted_call(
        body, host, name, (nt,),
        [tile, tile,
         pl.BlockSpec((8, d), lambda i: (0, 0)),
         pl.BlockSpec((2, nj, tm, fb), lambda i: (0, 0, i, 0)),
         pl.BlockSpec((2, nj, fb, d), lambda i: (0, 0, 0, 0))],
        [tile, pl.BlockSpec((8, d), lambda i: (0, 0))],
        [jax.ShapeDtypeStruct((t, d), F32), jax.ShapeDtypeStruct((8, d), F32)],
        [], ("arbitrary",), 60, (dxo, x, vec, dgu, wi))


def _mm_tn(a, b, a_spec, b_spec, nblk, nk, m, n, name):
    def body(a_ref, b_ref, o_ref, acc_ref):
        s = pl.program_id(1)

        @pl.when(s == 0)
        def _():
            acc_ref[...] = jnp.zeros_like(acc_ref)

        acc_ref[...] += _dot_tn(a_ref[...], b_ref[...])

        @pl.when(s == nk - 1)
        def _():
            o_ref[...] = acc_ref[...].astype(BF16)

    return pl.pallas_call(
        body, name=name, grid=(nblk, nk),
        in_specs=[a_spec, b_spec],
        out_specs=pl.BlockSpec((None, m, n), lambda k, s: (k, 0, 0)),
        out_shape=jax.ShapeDtypeStruct((nblk, m, n), BF16),
        scratch_shapes=[pltpu.VMEM((m, n), F32)],
        compiler_params=_cp(("arbitrary", "arbitrary"), 56),
    )(a, b)


def _wgrad_in(h, parts, cb, tk, name):
    t, d = h.shape
    dc = parts[0].shape[1]
    per = dc // cb
    nblk = len(parts) * per
    nk = t // tk

    def body(h_ref, p0, p1, p2, p3, o_ref, acc_ref):
        s = pl.program_id(0)

        @pl.when(s == 0)
        def _():
            acc_ref[...] = jnp.zeros_like(acc_ref)

        hb = h_ref[...]
        for p, p_ref in enumerate((p0, p1, p2, p3)):
            acc_ref[p] += _dot_tn(hb, p_ref[...])

        @pl.when(s == nk - 1)
        def _():
            for k in range(nblk):
                o_ref[k] = acc_ref[k // per, :, (k % per) * cb:(k % per + 1) * cb].astype(BF16)

    return pl.pallas_call(
        body, name=name, grid=(nk,),
        in_specs=[pl.BlockSpec((tk, d), lambda s: (s, 0))] + [pl.BlockSpec((tk, dc), lambda s: (s, 0))] * len(parts),
        out_specs=pl.BlockSpec((nblk, d, cb), lambda s: (0, 0, 0)),
        out_shape=jax.ShapeDtypeStruct((nblk, d, cb), BF16),
        scratch_shapes=[pltpu.VMEM((len(parts), d, dc), F32)],
        compiler_params=_cp(("arbitrary",), 56),
    )(h, *parts)


def _mixout_bwd(dxo, ym, cv, hr, proj, vec, lnv, wout, tm, name, host=None):
    t, d = dxo.shape
    dc = cv.shape[1]
    nt = t // tm

    def body(ins, outs, scr):
        dxo_ref, ym_ref, cv_ref, hr_ref, uy_ref, vec_ref, lnv_ref, w_ref = ins
        dym_ref, dcv_ref, dhr_ref, duy_ref, dln_ref, dgt_ref = outs
        i = pl.program_id(0)

        @pl.when(i == 0)
        def _():
            dln_ref[...] = jnp.zeros_like(dln_ref)
            dgt_ref[...] = jnp.zeros_like(dgt_ref)

        dxo_v = dxo_ref[...]
        dym = (vec_ref[3:4, :] * dxo_v).astype(BF16)
        dym_ref[...] = dym
        dgt_ref[0:1, :] += jnp.sum(dxo_v * ym_ref[...].astype(F32), axis=0, keepdims=True)
        dyc = _dot_nt(dym, w_ref[0:dc, :])
        dyr = _dot_nt(dym, w_ref[dc:, :])
        rs, chat, z, sg = _ln_silu(cv_ref[...], lnv_ref)
        dz = dyc * (sg * (1.0 + z * (1.0 - sg)))
        dln_ref[0:1, :] += jnp.sum(dz * chat, axis=0, keepdims=True)
        dln_ref[1:2, :] += jnp.sum(dz, axis=0, keepdims=True)
        dchat = dz * lnv_ref[0:1, :]
        dcv_ref[...] = rs * (dchat - jnp.mean(dchat, axis=-1, keepdims=True)
                             - chat * jnp.mean(dchat * chat, axis=-1, keepdims=True))
        ge, dge = _gelu_and_grad(uy_ref[...])
        dhr_ref[...] = dyr * ge
        duy_ref[...] = (dyr * hr_ref[...] * dge).astype(BF16)

    tile_d = pl.BlockSpec((tm, d), lambda i: (i, 0))
    tile_c = pl.BlockSpec((tm, dc), lambda i: (i, 0))
    return _hosted_call(
        body, host, name, (nt,),
        [tile_d, tile_d, tile_c, tile_c,
         pl.BlockSpec((tm, dc), lambda i: (i, 3)),
         pl.BlockSpec((8, d), lambda i: (0, 0)),
         pl.BlockSpec((8, dc), lambda i: (0, 0)),
         pl.BlockSpec((d, d), lambda i: (0, 0))],
        [tile_d, tile_c, tile_c, tile_c,
         pl.BlockSpec((8, dc), lambda i: (0, 0)),
         pl.BlockSpec((8, d), lambda i: (0, 0))],
        [jax.ShapeDtypeStruct((t, d), BF16), jax.ShapeDtypeStruct((t, dc), F32),
         jax.ShapeDtypeStruct((t, dc), F32), jax.ShapeDtypeStruct((t, dc), BF16),
         jax.ShapeDtypeStruct((8, dc), F32), jax.ShapeDtypeStruct((8, d), F32)],
        [], ("arbitrary",), 48, (dxo, ym, cv, hr, proj, vec, lnv, wout))


def _conv_bwd(proj, dcv, cw32, name):
    t = proj.shape[0]
    nblk = cw32.shape[1] // LANES
    ch = min(t, 128)

    def body(val_ref, gate_ref, dcv_ref, cw_ref, dval_ref, dgate_ref, dcw_ref, extu_ref, extd_ref):
        val = val_ref[...]
        sg = _sigmoid(gate_ref[...])
        extu_ref[0:32, :] = jnp.zeros((32, LANES), F32)
        extu_ref[32:, :] = val * sg
        dcv_v = dcv_ref[...]
        extd_ref[0:t, :] = dcv_v
        extd_ref[t:, :] = jnp.zeros((32, LANES), F32)
        for r in range(t // ch):
            acc = jnp.zeros((ch, LANES), F32)
            for k in range(CONV_W):
                off = r * ch + (CONV_W - 1 - k)
                acc = acc + cw_ref[k:k + 1, :] * extd_ref[off:off + ch, :]
            rows = slice(r * ch, (r + 1) * ch)
            sg_r = _sigmoid(gate_ref[rows, :])
            dval_ref[rows, :] = (acc * sg_r).astype(BF16)
            dgate_ref[rows, :] = (acc * val_ref[rows, :] * sg_r * (1.0 - sg_r)).astype(BF16)
        for k in range(CONV_W):
            off = 32 - (CONV_W - 1 - k)
            dcw_ref[k:k + 1, :] = jnp.sum(dcv_v * extu_ref[off:off + t, :], axis=0, keepdims=True)
        dcw_ref[31:32, :] = jnp.sum(dcv_v, axis=0, keepdims=True)

    blk = lambda off: pl.BlockSpec((t, LANES), lambda c: (0, off + c))
    return pl.pallas_call(
        body, name=name, grid=(nblk,),
        in_specs=[blk(0), blk(nblk), blk(0), pl.BlockSpec((32, LANES), lambda c: (0, c))],
        out_specs=[blk(0), blk(0), pl.BlockSpec((32, LANES), lambda c: (0, c))],
        out_shape=[jax.ShapeDtypeStruct((t, nblk * LANES), BF16), jax.ShapeDtypeStruct((t, nblk * LANES), BF16),
                   jax.ShapeDtypeStruct((32, nblk * LANES), F32)],
        scratch_shapes=[pltpu.VMEM((t + 32, LANES), F32), pltpu.VMEM((t + 32, LANES), F32)],
        compiler_params=_cp(("arbitrary",), 56),
    )(proj, proj, dcv, cw32)


def _rnn_bwd(proj, hr, dhr, rw8, rvec, wab, wib, name, host=None):
    t = proj.shape[0]
    nblk = rvec.shape[1] // LANES

    def body(ins, outs, scr):
        ux_ref, h_ref, dh_ref, rw_ref, rvec_ref, wa_ref, wi_ref = ins
        dux_ref, sm_ref, dwa_ref, dwi_ref = outs
        ext_ref, extd_ref = scr
        xr = _rnn_conv(ux_ref[...], rw_ref, rvec_ref, ext_ref)
        xb, r, ig, ls, a, mult = _rglru_gates(xr, wa_ref, wi_ref, rvec_ref)
        row = lax.broadcasted_iota(jnp.int32, (t, LANES), 0)
        a_next = jnp.where(row < t - 1, pltpu.roll(a, t - 1, 0), 0.0)
        g = _scan_rev(a_next, dh_ref[...])
        hprev = jnp.where(row >= 1, pltpu.roll(h_ref[...], 1, 0), 0.0)
        da = g * hprev
        dmult = g * (ig * xr)
        dig = g * mult * xr
        dxr = g * mult * ig
        dlog_a = a * (da - dmult * a / mult)
        dr = dlog_a * (RG_C * ls)
        dls = RG_C * jnp.sum(dlog_a * r, axis=0, keepdims=True)
        dpr = dr * r * (1.0 - r)
        dpi = dig * ig * (1.0 - ig)
        dprb = dpr.astype(BF16)
        dpib = dpi.astype(BF16)
        dxr = dxr + _dot_nt(dprb, wa_ref[...]) + _dot_nt(dpib, wi_ref[...])
        dwa_ref[...] = _dot_tn(xb, dprb)
        dwi_ref[...] = _dot_tn(xb, dpib)
        extd_ref[0:t, :] = dxr
        extd_ref[t:, :] = jnp.zeros((8, LANES), F32)
        dux = rw_ref[RNN_CONV_W - 1:RNN_CONV_W, :] * dxr
        for k in range(RNN_CONV_W - 1):
            d = RNN_CONV_W - 1 - k
            dux = dux + rw_ref[k:k + 1, :] * extd_ref[d:d + t, :]
        dux_ref[...] = dux.astype(BF16)
        for k in range(RNN_CONV_W):
            d = RNN_CONV_W - 1 - k
            sm_ref[k:k + 1, :] = jnp.sum(dxr * ext_ref[8 - d:8 - d + t, :], axis=0, keepdims=True)
        sm_ref[4:5, :] = jnp.sum(dxr, axis=0, keepdims=True)
        sm_ref[5:6, :] = jnp.sum(dpr, axis=0, keepdims=True)
        sm_ref[6:7, :] = jnp.sum(dpi, axis=0, keepdims=True)
        sm_ref[7:8, :] = dls * _sigmoid(-rvec_ref[3:4, :])

    blk = lambda off: pl.BlockSpec((t, LANES), lambda c: (0, off + c))
    sq = pl.BlockSpec((None, LANES, LANES), lambda c: (c, 0, 0))
    return _hosted_call(
        body, host, name, (nblk,),
        [blk(2 * nblk), blk(0), blk(0),
         pl.BlockSpec((8, LANES), lambda c: (0, c)),
         pl.BlockSpec((8, LANES), lambda c: (0, c)), sq, sq],
        [blk(0), pl.BlockSpec((8, LANES), lambda c: (0, c)), sq, sq],
        [jax.ShapeDtypeStruct((t, nblk * LANES), BF16), jax.ShapeDtypeStruct((8, nblk * LANES), F32),
         jax.ShapeDtypeStruct((nblk, LANES, LANES), F32), jax.ShapeDtypeStruct((nblk, LANES, LANES), F32)],
        [pltpu.VMEM((t + 8, LANES), F32), pltpu.VMEM((t + 8, LANES), F32)],
        ("arbitrary",), 60, (proj, hr, dhr, rw8, rvec, wab, wib))


def _mixin_bwd(dxo, x, parts, vec, win, f, nvec, tm, name):
    t, d = x.shape
    nb, _, cb = win.shape
    dc = parts[0].shape[1]
    per = dc // cb
    nt = t // tm

    def body(dxo_ref, x_ref, p0, p1, p2, p3, vec_ref, w_ref, f_ref, nvec_ref, dx_ref, dvec_ref, df_ref, dgt_ref):
        i = pl.program_id(0)

        @pl.when(i == 0)
        def _():
            dvec_ref[...] = jnp.zeros_like(dvec_ref)
            dgt_ref[...] = jnp.zeros_like(dgt_ref)

        prefs = (p0, p1, p2, p3)
        dh = jnp.zeros((tm, d), F32)
        for k in range(nb):
            dh = dh + _dot_nt(prefs[k // per][:, (k % per) * cb:(k % per + 1) * cb], w_ref[k])
        dx = dxo_ref[...] + _adaln_bwd(x_ref[...], dh, vec_ref, dvec_ref)
        dx_ref[...] = dx
        _emit_df(dx, f_ref, nvec_ref, df_ref, dgt_ref)

        @pl.when(i == nt - 1)
        def _():
            _adaln_finish(vec_ref, dvec_ref)

    tile_d = pl.BlockSpec((tm, d), lambda i: (i, 0))
    tile_c = pl.BlockSpec((tm, dc), lambda i: (i, 0))
    tab = pl.BlockSpec((8, d), lambda i: (0, 0))
    return pl.pallas_call(
        body, name=name, grid=(nt,),
        in_specs=[tile_d, tile_d, tile_c, tile_c, tile_c, tile_c, tab,
                  pl.BlockSpec((nb, d, cb), lambda i: (0, 0, 0)), tile_d, tab],
        out_specs=[tile_d, tab, tile_d, tab],
        out_shape=[jax.ShapeDtypeStruct((t, d), F32), jax.ShapeDtypeStruct((8, d), F32),
                   jax.ShapeDtypeStruct((t, d), BF16), jax.ShapeDtypeStruct((8, d), F32)],
        compiler_params=_cp(("arbitrary",), 48),
    )(dxo, x, *parts, vec, win, f, nvec)


def _coords():
    return lax.axis_index("x"), lax.axis_index("y"), lax.axis_index("c")


def _flip(v, bit):
    return 1 - v if bit else v


def _gather_copy(outs, send_sems, recv_sems, a, k, block, to, src=None):
    dst = outs[a].at[block]
    return pltpu.make_async_remote_copy(
        src_ref=dst if src is None else src, dst_ref=dst,
        send_sem=send_sems.at[a, k], recv_sem=recv_sems.at[a, k],
        device_id=to, device_id_type=MESH_IDS)


def _gather_start(ins, outs, send_sems, recv_sems, loc_sems):
    x, y, c = _coords()
    me = 4 * x + 2 * y + c
    for a in range(len(ins)):
        pltpu.make_async_copy(ins[a], outs[a].at[me], loc_sems.at[a]).start()
    for a in range(len(ins)):
        _gather_copy(outs, send_sems, recv_sems, a, 0, me, (x, y, 1 - c), src=ins[a]).start()
        for j, (cx, cy) in enumerate([(1 - x, y), (x, 1 - y), (1 - x, 1 - y)]):
            _gather_copy(outs, send_sems, recv_sems, a, 1 + j, me, (cx, cy, c), src=ins[a]).start()


def _gather_finish(ins, outs, send_sems, recv_sems, loc_sems):
    x, y, c = _coords()
    me = 4 * x + 2 * y + c
    sib = (x, y, 1 - c)
    chips = [(1 - x, y), (x, 1 - y), (1 - x, 1 - y)]
    n = len(ins)
    for a in range(n):
        for j, (cx, cy) in enumerate(chips):
            blk = 4 * cx + 2 * cy + c
            _gather_copy(outs, send_sems, recv_sems, a, 1 + j, blk, sib).wait_recv()
            _gather_copy(outs, send_sems, recv_sems, a, 4 + j, blk, sib).start()
    for a in range(n):
        _gather_copy(outs, send_sems, recv_sems, a, 0, 4 * x + 2 * y + (1 - c), sib).wait_recv()
        for j, (cx, cy) in enumerate(chips):
            _gather_copy(outs, send_sems, recv_sems, a, 4 + j, 4 * cx + 2 * cy + (1 - c), sib).wait_recv()
    for a in range(n):
        _gather_copy(outs, send_sems, recv_sems, a, 0, me, sib, src=ins[a]).wait_send()
        for j, (cx, cy) in enumerate(chips):
            _gather_copy(outs, send_sems, recv_sems, a, 1 + j, me, (cx, cy, c), src=ins[a]).wait_send()
            _gather_copy(outs, send_sems, recv_sems, a, 4 + j, 4 * cx + 2 * cy + c, sib).wait_send()
        pltpu.make_async_copy(ins[a], outs[a].at[me], loc_sems.at[a]).wait()


def _gather_shapes(shards):
    return [jax.ShapeDtypeStruct((NDEV,) + s.shape, s.dtype) for s in shards]


def _gather_sems(n):
    return [pltpu.SemaphoreType.DMA((n, 7)), pltpu.SemaphoreType.DMA((n, 7)), pltpu.SemaphoreType.DMA((n,))]


def _sibling_copies(ins, outs, send_sems, recv_sems):
    x, y, c = _coords()
    return [pltpu.make_async_remote_copy(
        src_ref=ins[a].at[2 * q + (1 - c)], dst_ref=outs[a].at[q],
        send_sem=send_sems.at[a, q], recv_sem=recv_sems.at[a, q],
        device_id=(x, y, 1 - c), device_id_type=MESH_IDS) for a in range(len(ins)) for q in range(4)]


def _sibling_shapes(parts):
    return [jax.ShapeDtypeStruct((4,) + p.shape[1:], p.dtype) for p in parts]


def _chips_copies(ins, outs, send_sems, recv_sems):
    x, y, c = _coords()
    copies = []
    for a in range(len(ins)):
        for k, (kx, ky) in enumerate([(1, 0), (0, 1), (1, 1)]):
            tx, ty = _flip(x, kx), _flip(y, ky)
            copies.append(pltpu.make_async_remote_copy(
                src_ref=ins[a].at[2 * tx + ty], dst_ref=outs[a].at[k],
                send_sem=send_sems.at[a, k], recv_sem=recv_sems.at[a, k],
                device_id=(tx, ty, c), device_id_type=MESH_IDS))
    return copies


def _chips_shapes(sums):
    return [jax.ShapeDtypeStruct((3,) + s.shape[1:], s.dtype) for s in sums]


def _direct_copies(ins, outs, send_sems, recv_sems):
    x, y, c = _coords()
    me = 4 * x + 2 * y + c
    copies = []
    for a in range(len(ins)):
        for k in range(1, NDEV):
            kx, ky, kc = (k >> 2) & 1, (k >> 1) & 1, k & 1
            copies.append(pltpu.make_async_remote_copy(
                src_ref=ins[a], dst_ref=outs[a].at[me],
                send_sem=send_sems.at[a, k - 1], recv_sem=recv_sems.at[a, k - 1],
                device_id=(_flip(x, kx), _flip(y, ky), _flip(c, kc)), device_id_type=MESH_IDS))
    return copies


class _Exchange:
    def __init__(self, kind, arrays):
        self.kind, self.arrays, self.n = kind, list(arrays), len(arrays)
        self.nsem = 3 if kind in ("gather", "direct") else 2

    def out_shapes(self):
        return {"gather": _gather_shapes, "direct": _gather_shapes, "sibling": _sibling_shapes,
                "chips": _chips_shapes}[self.kind](self.arrays)

    def sems(self):
        if self.kind in ("gather", "direct"):
            return _gather_sems(self.n)
        k = {"sibling": 4, "chips": 3}[self.kind]
        return [pltpu.SemaphoreType.DMA((self.n, k)), pltpu.SemaphoreType.DMA((self.n, k))]

    def _copies(self, ins, outs, sems):
        if self.kind == "direct":
            x, y, c = _coords()
            own = [pltpu.make_async_copy(ins[a], outs[a].at[4 * x + 2 * y + c], sems[2].at[a]) for a in range(self.n)]
            return own + _direct_copies(ins, outs, sems[0], sems[1])
        return {"sibling": _sibling_copies, "chips": _chips_copies}[self.kind](ins, outs, *sems)

    def start(self, ins, outs, sems):
        if self.kind == "gather":
            _gather_start(ins, outs, *sems)
        else:
            for cpy in self._copies(ins, outs, sems):
                cpy.start()

    def finish(self, ins, outs, sems):
        if self.kind == "gather":
            _gather_finish(ins, outs, *sems)
        else:
            for cpy in self._copies(ins, outs, sems):
                cpy.wait()


class _Exchanges:
    def __init__(self, *parts):
        self.parts = parts
        self.arrays = [a for p in parts for a in p.arrays]
        self.n = len(self.arrays)

    def out_shapes(self):
        return [s for p in self.parts for s in p.out_shapes()]

    def sems(self):
        return [s for p in self.parts for s in p.sems()]

    def _each(self, ins, outs, sems):
        a = s = 0
        for p in self.parts:
            yield p, ins[a:a + p.n], outs[a:a + p.n], sems[s:s + p.nsem]
            a, s = a + p.n, s + p.nsem

    def start(self, ins, outs, sems):
        for p, i, o, s in self._each(ins, outs, sems):
            p.start(i, o, s)

    def finish(self, ins, outs, sems):
        for p, i, o, s in self._each(ins, outs, sems):
            p.finish(i, o, s)


def _hosted_call(body, host, name, grid, in_specs, out_specs, out_shape, scratch, sem, vmem_mb, args):
    n = host.n if host else 0
    ni, no, ns = len(in_specs), len(out_specs), len(scratch)

    def full(*refs):
        ins, h_in = refs[:ni], refs[ni:ni + n]
        outs, h_out = refs[ni + n:ni + n + no], refs[ni + n + no:ni + 2 * n + no]
        scr, sems = refs[ni + 2 * n + no:ni + 2 * n + no + ns], refs[ni + 2 * n + no + ns:]
        if host and grid:
            first = functools.reduce(lambda a, b: a & b, [pl.program_id(k) == 0 for k in range(len(grid))])
            last = functools.reduce(lambda a, b: a & b, [pl.program_id(k) == g - 1 for k, g in enumerate(grid)])

            @pl.when(first)
            def _():
                host.start(h_in, h_out, sems)
        elif host:
            host.start(h_in, h_out, sems)

        body(ins, outs, scr)

        if host and grid:
            @pl.when(last)
            def _():
                host.finish(h_in, h_out, sems)
        elif host:
            host.finish(h_in, h_out, sems)

    anyspec = pl.BlockSpec(memory_space=pl.ANY)
    return pl.pallas_call(
        full, name=name, grid=grid,
        in_specs=list(in_specs) + [anyspec] * n, out_specs=list(out_specs) + [anyspec] * n,
        out_shape=list(out_shape) + (host.out_shapes() if host else []),
        scratch_shapes=list(scratch) + (host.sems() if host else []),
        compiler_params=_cp(sem, vmem_mb),
    )(*args, *(host.arrays if host else []))


def _exchange(host, name):
    def body(*refs):
        n = host.n
        host.start(refs[:n], refs[n:2 * n], refs[2 * n:])
        host.finish(refs[:n], refs[n:2 * n], refs[2 * n:])

    anyspec = pl.BlockSpec(memory_space=pl.ANY)
    return pl.pallas_call(
        body, name=name, in_specs=[anyspec] * host.n, out_specs=[anyspec] * host.n,
        out_shape=host.out_shapes(), scratch_shapes=host.sems(),
    )(*host.arrays)


def _chip_sum(part, recv, sel, tr, name):
    _, _, r, c = part.shape

    def body(sel_ref, p_ref, r_ref, cs_ref, own_ref):
        q = pl.program_id(1)
        s = p_ref[...].astype(F32) + r_ref[...].astype(F32)
        cs_ref[...] = s.astype(BF16)

        @pl.when(q == sel_ref[1])
        def _():
            own_ref[...] = s

    return pl.pallas_call(
        body, name=name,
        grid_spec=pltpu.PrefetchScalarGridSpec(
            num_scalar_prefetch=1, grid=(r // tr, 4),
            in_specs=[pl.BlockSpec((None, None, tr, c), lambda i, q, s: (q, s[0], i, 0)),
                      pl.BlockSpec((None, tr, c), lambda i, q, s: (q, i, 0))],
            out_specs=[pl.BlockSpec((None, tr, c), lambda i, q, s: (q, i, 0)),
                       pl.BlockSpec((tr, c), lambda i, q, s: (i, 0))]),
        out_shape=[jax.ShapeDtypeStruct((4, r, c), BF16), jax.ShapeDtypeStruct((r, c), F32)],
        compiler_params=_cp(("arbitrary", "arbitrary"), 48),
    )(sel, part, recv)


def _gather_direct(src_ref, buf_ref, send_sems, recv_sems):
    x, y, c = _coords()
    me = 4 * x + 2 * y + c
    buf_ref[me] = src_ref[...]
    copies = []
    for k in range(1, NDEV):
        kx, ky, kc = (k >> 2) & 1, (k >> 1) & 1, k & 1
        copies.append(pltpu.make_async_remote_copy(
            src_ref=src_ref, dst_ref=buf_ref.at[me],
            send_sem=send_sems.at[k - 1], recv_sem=recv_sems.at[k - 1],
            device_id=(_flip(x, kx), _flip(y, ky), _flip(c, kc)), device_id_type=MESH_IDS))
    for cpy in copies:
        cpy.start()
    for k in range(1, NDEV):
        kx, ky, kc = (k >> 2) & 1, (k >> 1) & 1, k & 1
        peer = 4 * _flip(x, kx) + 2 * _flip(y, ky) + _flip(c, kc)
        pltpu.make_async_remote_copy(
            src_ref=src_ref, dst_ref=buf_ref.at[peer],
            send_sem=send_sems.at[k - 1], recv_sem=recv_sems.at[k - 1],
            device_id=(x, y, c), device_id_type=MESH_IDS).wait_recv()
    for cpy in copies:
        cpy.wait_send()
    return me


def _mod_exchange(c_row, wmod, bmod, wfmod, bfmod, name, host=None):
    d = c_row.shape[1]
    nm, nf = wmod.shape[1], wfmod.shape[1]
    nw = nm + nf

    def body(ins, outs, scr):
        c_ref, wm_ref, bm_ref, wf_ref, bf_ref = ins
        cs_ref, mod_ref, fmod_ref = outs
        slab_ref, csbuf_ref, mslab_ref, mbuf_ref, s1, r1, s2, r2 = scr
        cv = c_ref[...]
        slab_ref[...] = jnp.broadcast_to(cv * _sigmoid(cv), (8, d))
        _gather_direct(slab_ref, csbuf_ref, s1, r1)
        for b in range(NDEV):
            cs_ref[b:b + 1, :] = csbuf_ref[b, 0:1, :]
        cs = cs_ref[...]
        mslab_ref[:, 0:nm] = jnp.dot(cs, wm_ref[...], precision=HI, preferred_element_type=F32) + bm_ref[...]
        mslab_ref[:, nm:] = jnp.dot(cs, wf_ref[...], precision=HI, preferred_element_type=F32) + bf_ref[...]
        me = _gather_direct(mslab_ref, mbuf_ref, s2, r2)
        mine = lax.broadcasted_iota(jnp.int32, (8, nw), 0) == me
        for k in range(NDEV):
            rowk = jnp.sum(jnp.where(mine, mbuf_ref[k], 0.0), axis=0, keepdims=True)
            mod_ref[k:k + 1, :] = rowk[:, 0:nm]
            fmod_ref[k:k + 1, :] = rowk[:, nm:]

    vm = pl.BlockSpec(memory_space=pltpu.VMEM)
    return _hosted_call(
        body, host, name, (), [vm] * 5, [vm] * 3,
        [jax.ShapeDtypeStruct((NDEV, d), F32), jax.ShapeDtypeStruct((NDEV, nm), F32),
         jax.ShapeDtypeStruct((NDEV, nf), F32)],
        [pltpu.VMEM((8, d), F32), pltpu.VMEM((NDEV, 8, d), F32),
         pltpu.VMEM((8, nw), F32), pltpu.VMEM((NDEV, 8, nw), F32),
         pltpu.SemaphoreType.DMA((7,)), pltpu.SemaphoreType.DMA((7,)),
         pltpu.SemaphoreType.DMA((7,)), pltpu.SemaphoreType.DMA((7,))],
        None, 40, (c_row, wmod, bmod, wfmod, bfmod))


def _table_sum(tabs, name):
    n = len(tabs)

    def body(*refs):
        for a in range(n):
            tot = refs[a][0]
            for k in range(1, NDEV):
                tot = tot + refs[a][k]
            refs[n + a][...] = tot

    vm = pl.BlockSpec(memory_space=pltpu.VMEM)
    return pl.pallas_call(
        body, name=name, in_specs=[vm] * n, out_specs=[vm] * n,
        out_shape=[jax.ShapeDtypeStruct(tb.shape[1:], F32) for tb in tabs],
    )(*tabs)


def _adamw_math(w, g, m, v):
    m = ADAM_B1 * m + (1.0 - ADAM_B1) * g
    v = ADAM_B2 * v + (1.0 - ADAM_B2) * (g * g)
    m_hat = m / (1.0 - ADAM_B1 ** ADAM_STEP)
    v_hat = v / (1.0 - ADAM_B2 ** ADAM_STEP)
    delta = -ADAM_LR * (m_hat / (jnp.sqrt(v_hat) + ADAM_EPS) + ADAM_WD * w)
    return delta, m, v


def _adamw_small(params, name):
    n = len(params)

    def body(*refs):
        for p in range(n):
            w_ref, g_ref, m_ref, v_ref = refs[4 * p:4 * p + 4]
            d_ref, mo_ref, vo_ref = refs[4 * n + 3 * p:4 * n + 3 * p + 3]
            d_ref[...], mo_ref[...], vo_ref[...] = _adamw_math(w_ref[...], g_ref[...], m_ref[...], v_ref[...])

    vm = pl.BlockSpec(memory_space=pltpu.VMEM)
    flat = [a for p in params for a in p]
    outs = pl.pallas_call(
        body, name=name, in_specs=[vm] * (4 * n), out_specs=[vm] * (3 * n),
        out_shape=[jax.ShapeDtypeStruct(p[0].shape, F32) for p in params for _ in range(3)])(*flat)
    return [outs[3 * p:3 * p + 3] for p in range(n)]


def _rs_final(own, recv, w, m, v, tr, name):
    r, c = own.shape

    def body(o_ref, r_ref, w_ref, m_ref, v_ref, g_ref, d_ref, mo_ref, vo_ref):
        g = o_ref[...] + r_ref[0].astype(F32) + r_ref[1].astype(F32) + r_ref[2].astype(F32)
        g_ref[...] = g
        d_ref[...], mo_ref[...], vo_ref[...] = _adamw_math(w_ref[...], g, m_ref[...], v_ref[...])

    tile = pl.BlockSpec((tr, c), lambda i: (i, 0))
    sds = jax.ShapeDtypeStruct((r, c), F32)
    return pl.pallas_call(
        body, name=name, grid=(r // tr,),
        in_specs=[tile, pl.BlockSpec((3, tr, c), lambda i: (0, i, 0)), tile, tile, tile],
        out_specs=[tile] * 4, out_shape=[sds] * 4,
        compiler_params=_cp(("arbitrary",), 48),
    )(own, recv, w, m, v)


def _mod_weight_update(cs, dm, w, m, v, tr, name):
    r, c = w.shape

    def body(cs_ref, dm_ref, w_ref, m_ref, v_ref, g_ref, d_ref, mo_ref, vo_ref):
        g = lax.dot_general(cs_ref[...], dm_ref[...], (((0,), (0,)), ((), ())),
                            precision=HI, preferred_element_type=F32)
        g_ref[...] = g
        d_ref[...], mo_ref[...], vo_ref[...] = _adamw_math(w_ref[...], g, m_ref[...], v_ref[...])

    tile = pl.BlockSpec((tr, c), lambda i: (i, 0))
    sds = jax.ShapeDtypeStruct((r, c), F32)
    return pl.pallas_call(
        body, name=name, grid=(r // tr,),
        in_specs=[pl.BlockSpec((NDEV, tr), lambda i: (0, i)), pl.BlockSpec((NDEV, c), lambda i: (0, 0)),
                  tile, tile, tile],
        out_specs=[tile] * 4, out_shape=[sds] * 4,
        compiler_params=_cp(("arbitrary",), 48),
    )(cs, dm, w, m, v)


def _rows(*vs):
    d = vs[0].shape[-1]
    rows = [v.reshape(1, d) for v in vs]
    return jnp.concatenate(rows + [jnp.zeros((8 - len(rows), d), F32)], axis=0)


def _block_diag_pairs(w):
    hd = w.shape[-1]
    z = jnp.zeros((w.shape[0] // 2, hd, hd), w.dtype)
    top = jnp.concatenate([w[0::2], z], axis=2)
    bot = jnp.concatenate([z, w[1::2]], axis=2)
    return jnp.concatenate([top, bot], axis=1).astype(BF16)


def _diag_pairs(g):
    hd = g.shape[-1] // 2
    both = jnp.stack([g[:, :hd, :hd], g[:, hd:, hd:]], axis=1)
    return both.reshape(2 * g.shape[0], hd, hd)


def kernel(x, c, w_mod, b_mod, g_ffn1, w_ffn1_in, w_ffn1_out, g_mix, w_in, conv_w, conv_b, ln_g, ln_b, rnn_conv_w, rnn_conv_b, w_a, b_a, w_i, b_i, lru_lambda, w_out, g_ffn2, w_ffn2_in, w_ffn2_out, w_fmod, b_fmod, g_final, loss_target, m_w_mod, m_b_mod, m_g_ffn1, m_w_ffn1_in, m_w_ffn1_out, m_g_mix, m_w_in, m_conv_w, m_conv_b, m_ln_g, m_ln_b, m_rnn_conv_w, m_rnn_conv_b, m_w_a, m_b_a, m_w_i, m_b_i, m_lru_lambda, m_w_out, m_g_ffn2, m_w_ffn2_in, m_w_ffn2_out, m_w_fmod, m_b_fmod, m_g_final, v_w_mod, v_b_mod, v_g_ffn1, v_w_ffn1_in, v_w_ffn1_out, v_g_mix, v_w_in, v_conv_w, v_conv_b, v_ln_g, v_ln_b, v_rnn_conv_w, v_rnn_conv_b, v_w_a, v_b_a, v_w_i, v_b_i, v_lru_lambda, v_w_out, v_g_ffn2, v_w_ffn2_in, v_w_ffn2_out, v_w_fmod, v_b_fmod, v_g_final):
    t, d = x.shape[1], x.shape[2]
    fb = w_ffn1_in.shape[2]
    nm = w_mod.shape[2]
    nf = w_fmod.shape[1]
    dc = conv_b.shape[1]
    cl = conv_w.shape[2]
    tm = min(TOKEN_TILE, t)
    tk = min(WGRAD_TILE, t)
    nk = t // tk
    me = 4 * lax.axis_index("x") + 2 * lax.axis_index("y") + lax.axis_index("c")

    tr = jnp.transpose
    bmod_l = lax.dynamic_slice(b_mod, (0, me * nm), (1, nm))
    bfmod_l = lax.dynamic_slice(b_fmod.reshape(1, -1), (0, me * nf), (1, nf))
    cwl = jnp.concatenate([conv_w[0], jnp.zeros((1, cl), F32), rnn_conv_w[0], jnp.zeros((4, cl), F32)], axis=0)
    cs, mod_rows, fmod_rows, wi1, wo1, cwg = _mod_exchange(
        c, w_mod[0], bmod_l, w_fmod, bfmod_l, "mod_and_gather_ffn1",
        host=_Exchange("gather", [tr(w_ffn1_in[0]).astype(BF16), w_ffn1_out[0].astype(BF16), cwl]))
    wi1 = wi1.reshape(2, 4, fb, d)
    wo1 = wo1.reshape(4 * fb, d)
    mod = mod_rows.reshape(9, d)
    fmod = fmod_rows.reshape(2, d)
    vec1 = _rows(g_ffn1, mod[0], mod[1], mod[2])
    vecm = _rows(g_mix, mod[3], mod[4], mod[5])
    vec3 = _rows(g_ffn2, mod[6], mod[7], mod[8])
    vecf = _rows(g_final, fmod[0], fmod[1])

    xin = x[0]
    later = [w_in[0].astype(BF16), w_out[0].astype(BF16), tr(w_ffn2_in[0]).astype(BF16), w_ffn2_out[0].astype(BF16)]
    x1, h1, gu1, f1, win, wout, wi2, wo2 = _ffn_fwd(xin, vec1, wi1, wo1, tm, "ffn1_fwd",
                                                    host=_Exchange("gather", later))
    wi2 = wi2.reshape(2, 4, fb, d)
    wo2 = wo2.reshape(4 * fb, d)
    wout = wout.reshape(d, d)
    h2, proj = _mix_in(x1, vecm, win, tm, "mix_in")
    lnv = _rows(ln_g, ln_b)
    rvec = _rows(rnn_conv_b, b_a, b_i, lru_lambda)
    wab = _block_diag_pairs(w_a[0])
    wib = _block_diag_pairs(w_i[0])
    cwf = jnp.transpose(cwg, (1, 0, 2)).reshape(40, NDEV * cl)
    cw32 = jnp.concatenate([cwf[0:CONV_W], conv_b], axis=0)
    rw8 = cwf[32:40]

    (cv,) = _conv_fwd(proj, cw32, "conv_fwd")
    hr, yr = _rnn_fwd(proj, rw8, rvec, wab, wib, "rnn_fwd")
    x2, ym, ycat = _mix_out(x1, cv, yr, vecm, lnv, wout, tm, "mix_out")
    x3, h3, gu3, f3 = _ffn_fwd(x2, vec3, wi2, wo2, tm, "ffn2_fwd")

    dx3, dvf, df3, dva3 = _final(x3, loss_target[0], vecf, f3, vec3, tm, "final_loss")
    a_tok = lambda width: pl.BlockSpec((tk, width), lambda k, s: (s, 0))
    blk3 = lambda width: pl.BlockSpec((None, tk, width), lambda k, s: (k, s, 0))
    sel = jnp.stack([lax.axis_index("c"), 2 * lax.axis_index("x") + lax.axis_index("y")]).astype(jnp.int32)
    row_tile = {"w_ffn1_in": fb // 4, "w_ffn1_out": fb // 4, "w_in": 512, "w_out": 128,
                "w_ffn2_in": fb // 4, "w_ffn2_out": fb // 4}

    def chip_sums(names, partials, from_sib):
        out = [_chip_sum(p.reshape((4, 2) + p.shape[1:]), r, sel, p.shape[1], "chip_sum_" + nm_)
               for nm_, p, r in zip(names, partials, from_sib)]
        return [o[0] for o in out], [o[1] for o in out]

    dgu3, p_wi2, p_wo2 = _ffn_bwd_w(df3, gu3, h3, wo2, tm, "ffn2_bwd_w")
    p_wi2 = p_wi2.reshape(NDEV, fb, d)
    p_wo2 = p_wo2.reshape(NDEV, fb // 2, d)
    names2 = ["w_ffn2_in", "w_ffn2_out"]
    dx2, dv3, sib_wi2, sib_wo2 = _ffn_bwd_in(dx3, x2, vec3, dgu3, wi2, tm, "ffn2_bwd_in",
                                             host=_Exchange("sibling", [p_wi2, p_wo2]))
    (s_wi2, s_wo2), owns2 = chip_sums(names2, [p_wi2, p_wo2], [sib_wi2, sib_wo2])
    dym, dcv, dhr, duy, dln, dgt2, r_wo2 = _mixout_bwd(
        dx2, ym, cv, hr, proj, vecm, lnv, wout, tm, "mixout_bwd", host=_Exchange("chips", [s_wo2]))
    dval, dgate, dcw = _conv_bwd(proj, dcv, cw32, "conv_bwd")
    dux, rsm, dwab, dwib, r_wi2 = _rnn_bwd(proj, hr, dhr, rw8, rvec, wab, wib, "rnn_bwd",
                                           host=_Exchange("chips", [s_wi2]))
    parts = [dval, dgate, dux, duy]
    dx1, dvm, df1, dva1 = _mixin_bwd(dx2, x1, parts, vecm, win, f1, vec1, tm, "mixin_bwd")
    p_wout = _mm_tn(ycat, dym, a_tok(d), a_tok(d), 1, nk, d, d, "wgrad_out").reshape(NDEV, d // NDEV, d)
    p_win = _wgrad_in(h2, parts, win.shape[2], min(tk, 1024), "wgrad_in")
    namesm = ["w_in", "w_out"]
    sumsm, ownsm = chip_sums(namesm, [p_win, p_wout],
                             _exchange(_Exchange("sibling", [p_win, p_wout]), "rs_sibling_mix"))
    lane_pad = lambda v: jnp.concatenate([v, jnp.zeros_like(v)], axis=1)
    early = jnp.concatenate([dva3, dv3, dvf, dvm, dgt2, dva1, dcw.reshape(16, d), lane_pad(dln), lane_pad(rsm),
                             _diag_pairs(dwab).reshape(32, d), _diag_pairs(dwib).reshape(32, d)], axis=0)
    dgu1, p_wi1, p_wo1, r_win, r_wout, all_early = _ffn_bwd_w(
        df1, gu1, h1, wo1, tm, "ffn1_bwd_w",
        host=_Exchanges(_Exchange("chips", sumsm), _Exchange("direct", [early])))
    p_wi1 = p_wi1.reshape(NDEV, fb, d)
    p_wo1 = p_wo1.reshape(NDEV, fb // 2, d)
    names1 = ["w_ffn1_in", "w_ffn1_out"]
    sums1, owns1 = chip_sums(names1, [p_wi1, p_wo1],
                             _exchange(_Exchange("sibling", [p_wi1, p_wo1]), "rs_sibling_ffn1"))
    dx0, dv1, r_wi1, r_wo1 = _ffn_bwd_in(dx1, xin, vec1, dgu1, wi1, tm, "ffn1_bwd_in",
                                         host=_Exchange("chips", sums1))
    from_chips = {"w_in": r_win, "w_out": r_wout, "w_ffn2_in": r_wi2, "w_ffn2_out": r_wo2,
                  "w_ffn1_in": r_wi1, "w_ffn1_out": r_wo1}
    owns = dict(zip(namesm + names2 + names1, ownsm + owns2 + owns1))

    big = {"w_ffn1_in": (tr(w_ffn1_in[0]), tr(m_w_ffn1_in[0]), tr(v_w_ffn1_in[0])),
           "w_ffn1_out": (w_ffn1_out[0], m_w_ffn1_out[0], v_w_ffn1_out[0]),
           "w_in": (w_in[0], m_w_in[0], v_w_in[0]), "w_out": (w_out[0], m_w_out[0], v_w_out[0]),
           "w_ffn2_in": (tr(w_ffn2_in[0]), tr(m_w_ffn2_in[0]), tr(v_w_ffn2_in[0])),
           "w_ffn2_out": (w_ffn2_out[0], m_w_ffn2_out[0], v_w_ffn2_out[0])}
    res = {}
    for nm_ in namesm + names2 + names1:
        out4 = _rs_final(owns[nm_], from_chips[nm_], *big[nm_], row_tile[nm_], "rs_final_" + nm_)
        if nm_ in ("w_ffn1_in", "w_ffn2_in"):
            out4 = [tr(o) for o in out4]
        res[nm_] = [o[None] for o in out4]

    (all_late,) = _exchange(_Exchange("direct", [dv1]), "late_table")
    te, tl = _table_sum([all_early, all_late], "table_sum")
    loss = jnp.sum(te[20])

    mod_rows_of = lambda e, l: [l[1], l[3], e[42], e[25], e[27], e[32], e[9], e[11], e[2]]
    dm_all = jnp.concatenate(mod_rows_of(jnp.swapaxes(all_early, 0, 1), jnp.swapaxes(all_late, 0, 1)), axis=1)
    dfm_all = jnp.concatenate([all_early[:, 17], all_early[:, 19]], axis=1)
    dm_l = lax.dynamic_slice(dm_all, (0, me * nm), (NDEV, nm))
    dfm_l = lax.dynamic_slice(dfm_all, (0, me * nf), (NDEV, nf))
    out_wmod = [o[None] for o in _mod_weight_update(cs, dm_l, w_mod[0], m_w_mod[0], v_w_mod[0], 256, "w_mod_update")]
    out_wfmod = _mod_weight_update(cs, dfm_l, w_fmod, m_w_fmod, v_w_fmod, 256, "w_fmod_update")
    res["w_mod"] = out_wmod
    res["w_fmod"] = list(out_wfmod)

    dcw_f = te[48:64].reshape(32, dc)
    rsm_f = te[72:80, 0:dc]
    small_grads = {
        "b_mod": jnp.concatenate(mod_rows_of(te, tl)).reshape(1, 9 * d),
        "b_fmod": jnp.concatenate([te[17], te[19]]),
        "g_ffn1": tl[0:1], "g_mix": te[24:25], "g_ffn2": te[8:9], "g_final": te[16],
        "conv_w": lax.dynamic_slice(dcw_f, (0, me * cl), (CONV_W, cl))[None],
        "conv_b": dcw_f[31:32],
        "ln_g": te[64:65, 0:dc], "ln_b": te[65:66, 0:dc],
        "rnn_conv_w": lax.dynamic_slice(rsm_f, (0, me * cl), (RNN_CONV_W, cl))[None],
        "rnn_conv_b": rsm_f[4:5], "b_a": rsm_f[5:6], "b_i": rsm_f[6:7], "lru_lambda": rsm_f[7:8],
        "w_a": te[80:112].reshape(w_a.shape), "w_i": te[112:144].reshape(w_i.shape),
    }
    small_params = {
        "b_mod": (b_mod, m_b_mod, v_b_mod), "b_fmod": (b_fmod, m_b_fmod, v_b_fmod),
        "g_ffn1": (g_ffn1, m_g_ffn1, v_g_ffn1), "g_mix": (g_mix, m_g_mix, v_g_mix),
        "g_ffn2": (g_ffn2, m_g_ffn2, v_g_ffn2), "g_final": (g_final, m_g_final, v_g_final),
        "conv_w": (conv_w, m_conv_w, v_conv_w), "conv_b": (conv_b, m_conv_b, v_conv_b),
        "ln_g": (ln_g, m_ln_g, v_ln_g), "ln_b": (ln_b, m_ln_b, v_ln_b),
        "rnn_conv_w": (rnn_conv_w, m_rnn_conv_w, v_rnn_conv_w),
        "rnn_conv_b": (rnn_conv_b, m_rnn_conv_b, v_rnn_conv_b),
        "w_a": (w_a, m_w_a, v_w_a), "b_a": (b_a, m_b_a, v_b_a),
        "w_i": (w_i, m_w_i, v_w_i), "b_i": (b_i, m_b_i, v_b_i),
        "lru_lambda": (lru_lambda, m_lru_lambda, v_lru_lambda),
    }
    two_d = lambda w: (-1, w.shape[-1]) if w.ndim > 1 else (1, w.shape[0])
    small_names = list(small_grads)
    small_outs = _adamw_small(
        [(w.reshape(two_d(w)), small_grads[nm_].reshape(two_d(w)), m.reshape(two_d(w)), v.reshape(two_d(w)))
         for nm_ in small_names for (w, m, v) in [small_params[nm_]]], "adamw_small")
    for nm_, outs in zip(small_names, small_outs):
        shp = small_params[nm_][0].shape
        res[nm_] = [small_grads[nm_].reshape(shp)] + [o.reshape(shp) for o in outs]

    order = ["w_mod", "b_mod", "g_ffn1", "w_ffn1_in", "w_ffn1_out", "g_mix", "w_in", "conv_w", "conv_b",
             "ln_g", "ln_b", "rnn_conv_w", "rnn_conv_b", "w_a", "b_a", "w_i", "b_i", "lru_lambda", "w_out",
             "g_ffn2", "w_ffn2_in", "w_ffn2_out", "w_fmod", "b_fmod", "g_final"]
    return (loss, dx0[None], *[res[n][0] for n in order], *[res[n][1] for n in order],
            *[res[n][2] for n in order], *[res[n][3] for n in order])
```

```python
import functools
import math

import jax
import jax.numpy as jnp
from jax import lax
from jax.experimental import pallas as pl
from jax.experimental.pallas import tpu as pltpu

F32 = jnp.float32
BF16 = jnp.bfloat16
MESH_IDS = pl.DeviceIdType.MESH
NDEV = 8
EPS = 1e-6
RG_C = 8.0
CONV_W = 31
RNN_CONV_W = 4
LANES = 128
ADAM_LR = 0.001
ADAM_B1 = 0.9
ADAM_B2 = 0.999
ADAM_EPS = 1e-08
ADAM_WD = 0.01
ADAM_STEP = 10
SMALL_ROWS = 104
TOKEN_TILE = 512
WGRAD_TILE = 2048
ROW_GROUP = 16
HI = lax.Precision.HIGHEST


def _cp(sem, vmem_mb):
    return pltpu.CompilerParams(dimension_semantics=sem, vmem_limit_bytes=vmem_mb * 1024 * 1024)


def _dot(a, b):
    return jnp.dot(a, b, preferred_element_type=F32)


def _dot_nt(a, b):
    return lax.dot_general(a, b, (((1,), (1,)), ((), ())), preferred_element_type=F32)


def _dot_tn(a, b):
    return lax.dot_general(a, b, (((0,), (0,)), ((), ())), preferred_element_type=F32)


def _sigmoid(x):
    return 1.0 / (1.0 + jnp.exp(-x))


def _adaln(x, vec_ref):
    rstd = lax.rsqrt(jnp.mean(x * x, axis=-1, keepdims=True) + EPS)
    return (x * rstd) * vec_ref[0:1, :] * (1.0 + vec_ref[2:3, :]) + vec_ref[1:2, :]


def _adaln_bwd(x, dh, vec_ref, dvec_ref):
    rstd = lax.rsqrt(jnp.mean(x * x, axis=-1, keepdims=True) + EPS)
    xhat = x * rstd
    dvec_ref[0:1, :] += jnp.sum(dh * xhat, axis=0, keepdims=True)
    dvec_ref[1:2, :] += jnp.sum(dh, axis=0, keepdims=True)
    dxhat = dh * (vec_ref[0:1, :] * (1.0 + vec_ref[2:3, :]))
    return rstd * (dxhat - xhat * jnp.mean(dxhat * xhat, axis=-1, keepdims=True))


def _adaln_finish(vec_ref, dvec_ref):
    s = dvec_ref[0:1, :]
    dvec_ref[3:4, :] = vec_ref[0:1, :] * s
    dvec_ref[0:1, :] = (1.0 + vec_ref[2:3, :]) * s


def _gelu_and_grad(x):
    k0 = math.sqrt(2.0 / math.pi)
    x2 = x * x
    t = jnp.tanh(k0 * (x + 0.044715 * x * x2))
    g = 0.5 * x * (1.0 + t)
    dg = 0.5 * (1.0 + t) + 0.5 * x * (1.0 - t * t) * (k0 * (1.0 + 3.0 * 0.044715 * x2))
    return g, dg


def _log_sigmoid(x):
    z = jnp.exp(-jnp.abs(x))
    u = 1.0 + z
    d = u - 1.0
    log1p = jnp.where(d == 0.0, z, jnp.log(u) * (z / jnp.where(d == 0.0, 1.0, d)))
    return jnp.minimum(x, 0.0) - log1p


def _neg_expm1(x):
    series = -x * (1.0 + x * (0.5 + x * (1.0 / 6.0 + x * (1.0 / 24.0 + x * (1.0 / 120.0)))))
    return jnp.where(x > -0.05, series, 1.0 - jnp.exp(x))


def _scan_fwd(a, b):
    n = a.shape[0]
    row = lax.broadcasted_iota(jnp.int32, a.shape, 0)
    s = 1
    while s < n:
        ok = row >= s
        b = a * jnp.where(ok, pltpu.roll(b, s, 0), 0.0) + b
        if 2 * s < n:
            a = a * jnp.where(ok, pltpu.roll(a, s, 0), 1.0)
        s *= 2
    return b


def _scan_rev(a, d):
    n = a.shape[0]
    row = lax.broadcasted_iota(jnp.int32, a.shape, 0)
    s = 1
    while s < n:
        ok = row < n - s
        d = a * jnp.where(ok, pltpu.roll(d, n - s, 0), 0.0) + d
        if 2 * s < n:
            a = a * jnp.where(ok, pltpu.roll(a, n - s, 0), 1.0)
        s *= 2
    return d


def _rglru_gates(xr, wa_ref, wi_ref, rvec_ref):
    xb = xr.astype(BF16)
    r = _sigmoid(_dot(xb, wa_ref[...]) + rvec_ref[1:2, :])
    ig = _sigmoid(_dot(xb, wi_ref[...]) + rvec_ref[2:3, :])
    ls = _log_sigmoid(rvec_ref[3:4, :])
    log_a = RG_C * r * ls
    a = jnp.exp(log_a)
    mult = jnp.sqrt(_neg_expm1(2.0 * log_a))
    return xb, r, ig, ls, a, mult


def _rnn_conv(ux, rw_ref, rvec_ref, ext_ref):
    t = ux.shape[0]
    ext_ref[0:8, :] = jnp.zeros((8, ux.shape[1]), F32)
    ext_ref[8:, :] = ux
    xr = rvec_ref[0:1, :] + rw_ref[RNN_CONV_W - 1:RNN_CONV_W, :] * ux
    for k in range(RNN_CONV_W - 1):
        d = RNN_CONV_W - 1 - k
        xr = xr + rw_ref[k:k + 1, :] * ext_ref[8 - d:8 - d + t, :]
    return xr


def _ffn_fwd(x, vec, wi, wo, tm, name, host=None):
    t, d = x.shape
    nj, fb = wi.shape[1], wi.shape[2]
    nt = t // tm

    def body(ins, outs, scr):
        x_ref, vec_ref, wi_ref, wo_ref = ins
        xo_ref, h_ref, gu_ref, f_ref = outs
        acc_ref, = scr
        j = pl.program_id(1)

        @pl.when(j == 0)
        def _():
            h_ref[...] = _adaln(x_ref[...], vec_ref).astype(BF16)
            acc_ref[...] = jnp.zeros_like(acc_ref)

        h = h_ref[...]
        gate = _dot_nt(h, wi_ref[0])
        up = _dot_nt(h, wi_ref[1])
        gu_ref[0] = gate.astype(BF16)
        gu_ref[1] = up.astype(BF16)
        act = (gate * _sigmoid(gate) * up).astype(BF16)
        acc_ref[...] += _dot(act, wo_ref[...])

        @pl.when(j == nj - 1)
        def _():
            f = acc_ref[...]
            f_ref[...] = f.astype(BF16)
            xo_ref[...] = x_ref[...] + 0.5 * vec_ref[3:4, :] * f

    tile = pl.BlockSpec((tm, d), lambda i, j: (i, 0))
    return _hosted_call(
        body, host, name, (nt, nj),
        [tile,
         pl.BlockSpec((8, d), lambda i, j: (0, 0)),
         pl.BlockSpec((2, None, fb, d), lambda i, j: (0, j, 0, 0)),
         pl.BlockSpec((fb, d), lambda i, j: (j, 0))],
        [tile, tile, pl.BlockSpec((2, None, tm, fb), lambda i, j: (0, j, i, 0)), tile],
        [jax.ShapeDtypeStruct((t, d), F32), jax.ShapeDtypeStruct((t, d), BF16),
         jax.ShapeDtypeStruct((2, nj, t, fb), BF16), jax.ShapeDtypeStruct((t, d), BF16)],
        [pltpu.VMEM((tm, d), F32)], ("arbitrary", "arbitrary"), 48, (x, vec, wi, wo))


def _mix_in(x, vec, win, tm, name, host=None):
    t, d = x.shape
    nb, _, cb = win.shape

    def body(ins, outs, scr):
        x_ref, vec_ref, w_ref = ins
        h_ref, p_ref = outs
        h = _adaln(x_ref[...], vec_ref).astype(BF16)
        h_ref[...] = h
        for k in range(nb):
            p_ref[:, k * cb:(k + 1) * cb] = _dot(h, w_ref[k])

    return _hosted_call(
        body, host, name, (t // tm,),
        [pl.BlockSpec((tm, d), lambda i: (i, 0)),
         pl.BlockSpec((8, d), lambda i: (0, 0)),
         pl.BlockSpec((nb, d, cb), lambda i: (0, 0, 0))],
        [pl.BlockSpec((tm, d), lambda i: (i, 0)),
         pl.BlockSpec((tm, nb * cb), lambda i: (i, 0))],
        [jax.ShapeDtypeStruct((t, d), BF16), jax.ShapeDtypeStruct((t, nb * cb), F32)],
        [], ("arbitrary",), 48, (x, vec, win))


def _conv_fwd(proj, cw32, name, host=None):
    t = proj.shape[0]
    nblk = cw32.shape[1] // LANES
    ch = min(t, 128)

    def body(ins, outs, scr):
        val_ref, gate_ref, cw_ref = ins
        cv_ref, = outs
        ext_ref, = scr
        ext_ref[0:32, :] = jnp.zeros((32, LANES), F32)
        ext_ref[32:, :] = val_ref[...] * _sigmoid(gate_ref[...])
        for r in range(t // ch):
            acc = jnp.broadcast_to(cw_ref[31:32, :], (ch, LANES))
            for k in range(CONV_W):
                off = 32 + r * ch - (CONV_W - 1 - k)
                acc = acc + cw_ref[k:k + 1, :] * ext_ref[off:off + ch, :]
            cv_ref[r * ch:(r + 1) * ch, :] = acc

    return _hosted_call(
        body, host, name, (nblk,),
        [pl.BlockSpec((t, LANES), lambda c: (0, c)),
         pl.BlockSpec((t, LANES), lambda c: (0, nblk + c)),
         pl.BlockSpec((32, LANES), lambda c: (0, c))],
        [pl.BlockSpec((t, LANES), lambda c: (0, c))],
        [jax.ShapeDtypeStruct((t, nblk * LANES), F32)],
        [pltpu.VMEM((t + 32, LANES), F32)], ("arbitrary",), 48, (proj, proj, cw32))


def _rnn_fwd(proj, rw8, rvec, wab, wib, name, host=None):
    t = proj.shape[0]
    nblk = rvec.shape[1] // LANES

    def body(ins, outs, scr):
        ux_ref, uy_ref, rw_ref, rvec_ref, wa_ref, wi_ref = ins
        h_ref, yr_ref = outs
        ext_ref, = scr
        xr = _rnn_conv(ux_ref[...], rw_ref, rvec_ref, ext_ref)
        _, _, ig, _, a, mult = _rglru_gates(xr, wa_ref, wi_ref, rvec_ref)
        h = _scan_fwd(a, mult * (ig * xr))
        h_ref[...] = h
        ge, _ = _gelu_and_grad(uy_ref[...])
        yr_ref[...] = (ge * h).astype(BF16)

    blk = lambda off: pl.BlockSpec((t, LANES), lambda c: (0, off + c))
    return _hosted_call(
        body, host, name, (nblk,),
        [blk(2 * nblk), blk(3 * nblk),
         pl.BlockSpec((8, LANES), lambda c: (0, c)),
         pl.BlockSpec((8, LANES), lambda c: (0, c)),
         pl.BlockSpec((None, LANES, LANES), lambda c: (c, 0, 0)),
         pl.BlockSpec((None, LANES, LANES), lambda c: (c, 0, 0))],
        [blk(0), blk(0)],
        [jax.ShapeDtypeStruct((t, nblk * LANES), F32), jax.ShapeDtypeStruct((t, nblk * LANES), BF16)],
        [pltpu.VMEM((t + 8, LANES), F32)], ("arbitrary",), 56, (proj, proj, rw8, rvec, wab, wib))


def _ln_silu(cv, lnv_ref):
    mu = jnp.mean(cv, axis=-1, keepdims=True)
    xc = cv - mu
    rs = lax.rsqrt(jnp.mean(xc * xc, axis=-1, keepdims=True) + EPS)
    chat = xc * rs
    z = chat * lnv_ref[0:1, :] + lnv_ref[1:2, :]
    sg = _sigmoid(z)
    return rs, chat, z, sg


def _mix_out(x, cv, yr, vec, lnv, wout, tm, name, host=None):
    t, d = x.shape
    dc = cv.shape[1]

    def body(ins, outs, scr):
        x_ref, cv_ref, yr_ref, vec_ref, lnv_ref, w_ref = ins
        xo_ref, ym_ref, yc_ref = outs
        _, _, z, sg = _ln_silu(cv_ref[...], lnv_ref)
        yc = (z * sg).astype(BF16)
        yr = yr_ref[...]
        yc_ref[:, 0:dc] = yc
        yc_ref[:, dc:] = yr
        ym = _dot(yc, w_ref[0:dc, :]) + _dot(yr, w_ref[dc:, :])
        ym_ref[...] = ym.astype(BF16)
        xo_ref[...] = x_ref[...] + vec_ref[3:4, :] * ym

    tile = pl.BlockSpec((tm, d), lambda i: (i, 0))
    return _hosted_call(
        body, host, name, (t // tm,),
        [tile,
         pl.BlockSpec((tm, dc), lambda i: (i, 0)),
         pl.BlockSpec((tm, dc), lambda i: (i, 0)),
         pl.BlockSpec((8, d), lambda i: (0, 0)),
         pl.BlockSpec((8, dc), lambda i: (0, 0)),
         pl.BlockSpec((d, d), lambda i: (0, 0))],
        [tile, tile, tile],
        [jax.ShapeDtypeStruct((t, d), F32), jax.ShapeDtypeStruct((t, d), BF16), jax.ShapeDtypeStruct((t, d), BF16)],
        [], ("arbitrary",), 48, (x, cv, yr, vec, lnv, wout))


def _final(x, tgt, vec, f, nvec, tm, name):
    t, d = x.shape
    nt = t // tm

    def body(x_ref, t_ref, vec_ref, f_ref, nvec_ref, dx_ref, dvec_ref, df_ref, dgt_ref):
        i = pl.program_id(0)

        @pl.when(i == 0)
        def _():
            dvec_ref[...] = jnp.zeros_like(dvec_ref)
            dgt_ref[...] = jnp.zeros_like(dgt_ref)

        xv = x_ref[...]
        e = _adaln(xv, vec_ref) - t_ref[...]
        dvec_ref[4:5, :] += (0.5 / d) * jnp.sum(e * e, axis=0, keepdims=True)
        dx = _adaln_bwd(xv, e * (1.0 / d), vec_ref, dvec_ref)
        dx_ref[...] = dx
        _emit_df(dx, f_ref, nvec_ref, df_ref, dgt_ref)

        @pl.when(i == nt - 1)
        def _():
            _adaln_finish(vec_ref, dvec_ref)

    tile = pl.BlockSpec((tm, d), lambda i: (i, 0))
    tab = pl.BlockSpec((8, d), lambda i: (0, 0))
    return pl.pallas_call(
        body, name=name, grid=(nt,),
        in_specs=[tile, tile, tab, tile, tab],
        out_specs=[tile, tab, tile, tab],
        out_shape=[jax.ShapeDtypeStruct((t, d), F32), jax.ShapeDtypeStruct((8, d), F32),
                   jax.ShapeDtypeStruct((t, d), BF16), jax.ShapeDtypeStruct((8, d), F32)],
        compiler_params=_cp(("arbitrary",), 48),
    )(x, tgt, vec, f, nvec)


def _emit_df(dx, f_ref, nvec_ref, df_ref, dgt_ref):
    df_ref[...] = (0.5 * nvec_ref[3:4, :] * dx).astype(BF16)
    dgt_ref[2:3, :] += 0.5 * jnp.sum(dx * f_ref[...].astype(F32), axis=0, keepdims=True)


def _ffn_bwd_w(df, gu, h, wo, tm, name, host=None):
    t, d = df.shape
    nj, fb = gu.shape[1], gu.shape[3]
    nt = t // tm
    sub = min(tm, ROW_GROUP)

    def body(ins, outs, scr):
        df_ref, gu_ref, h_ref, wo_ref = ins
        dgu_ref, dwi_ref, dwo_ref = outs
        accg_ref, accu_ref, acco_ref, dact_ref, act_ref = scr
        i = pl.program_id(1)

        @pl.when(i == 0)
        def _():
            accg_ref[...] = jnp.zeros_like(accg_ref)
            accu_ref[...] = jnp.zeros_like(accu_ref)
            acco_ref[...] = jnp.zeros_like(acco_ref)

        dact_ref[...] = _dot_nt(df_ref[...], wo_ref[...])
        for r in range(tm // sub):
            rows = slice(r * sub, (r + 1) * sub)
            g = gu_ref[0, rows, :].astype(F32)
            u = gu_ref[1, rows, :].astype(F32)
            dact = dact_ref[rows, :]
            sg = _sigmoid(g)
            sl = g * sg
            dgu_ref[0, rows, :] = (dact * u * (sg * (1.0 + g * (1.0 - sg)))).astype(BF16)
            dgu_ref[1, rows, :] = (dact * sl).astype(BF16)
            act_ref[rows, :] = (sl * u).astype(BF16)
        hb = h_ref[...]
        acco_ref[...] += _dot_tn(act_ref[...], df_ref[...])
        accg_ref[...] += _dot_tn(dgu_ref[0], hb)
        accu_ref[...] += _dot_tn(dgu_ref[1], hb)

        @pl.when(i == nt - 1)
        def _():
            dwi_ref[0] = accg_ref[...].astype(BF16)
            dwi_ref[1] = accu_ref[...].astype(BF16)
            dwo_ref[...] = acco_ref[...].astype(BF16)

    tile = pl.BlockSpec((tm, d), lambda j, i: (i, 0))
    return _hosted_call(
        body, host, name, (nj, nt),
        [tile,
         pl.BlockSpec((2, None, tm, fb), lambda j, i: (0, j, i, 0)),
         tile,
         pl.BlockSpec((fb, d), lambda j, i: (j, 0))],
        [pl.BlockSpec((2, None, tm, fb), lambda j, i: (0, j, i, 0)),
         pl.BlockSpec((2, None, fb, d), lambda j, i: (0, j, 0, 0)),
         pl.BlockSpec((None, fb, d), lambda j, i: (j, 0, 0))],
        [jax.ShapeDtypeStruct((2, nj, t, fb), BF16), jax.ShapeDtypeStruct((2, nj, fb, d), BF16),
         jax.ShapeDtypeStruct((nj, fb, d), BF16)],
        [pltpu.VMEM((fb, d), F32), pltpu.VMEM((fb, d), F32), pltpu.VMEM((fb, d), F32),
         pltpu.VMEM((tm, fb), F32), pltpu.VMEM((tm, fb), BF16)],
        ("arbitrary", "arbitrary"), 56, (df, gu, h, wo))


def _ffn_bwd_in(dxo, x, vec, dgu, wi, tm, name, host=None):
    t, d = x.shape
    nj, fb = wi.shape[1], wi.shape[2]
    nt = t // tm

    def body(ins, outs, scr):
        dxo_ref, x_ref, vec_ref, dgu_ref, wi_ref = ins
        dx_ref, dvec_ref = outs
        i = pl.program_id(0)

        @pl.when(i == 0)
        def _():
            dvec_ref[...] = jnp.zeros_like(dvec_ref)

        dh = jnp.zeros((tm, d), F32)
        for a in range(2):
            for k in range(nj):
                dh = dh + _dot(dgu_ref[a, k], wi_ref[a, k])
        dx_ref[...] = dxo_ref[...] + _adaln_bwd(x_ref[...], dh, vec_ref, dvec_ref)

        @pl.when(i == nt - 1)
        def _():
            _adaln_finish(vec_ref, dvec_ref)

    tile = pl.BlockSpec((tm, d), lambda i: (i, 0))
    return _hosted_call(
        body, host, name, (nt,),
        [tile, tile,
         pl.BlockSpec((8, d), lambda i: (0, 0)),
         pl.BlockSpec((2, nj, tm, fb), lambda i: (0, 0, i, 0)),
         pl.BlockSpec((2, nj, fb, d), lambda i: (0, 0, 0, 0))],
        [tile, pl.BlockSpec((8, d), lambda i: (0, 0))],
        [jax.ShapeDtypeStruct((t, d), F32), jax.ShapeDtypeStruct((8, d), F32)],
        [], ("arbitrary",), 60, (dxo, x, vec, dgu, wi))


def _mm_tn(a, b, a_spec, b_spec, nblk, nk, m, n, name):
    def body(a_ref, b_ref, o_ref, acc_ref):
        s = pl.program_id(1)

        @pl.when(s == 0)
        def _():
            acc_ref[...] = jnp.zeros_like(acc_ref)

        acc_ref[...] += _dot_tn(a_ref[...], b_ref[...])

        @pl.when(s == nk - 1)
        def _():
            o_ref[...] = acc_ref[...].astype(BF16)

    return pl.pallas_call(
        body, name=name, grid=(nblk, nk),
        in_specs=[a_spec, b_spec],
        out_specs=pl.BlockSpec((None, m, n), lambda k, s: (k, 0, 0)),
        out_shape=jax.ShapeDtypeStruct((nblk, m, n), BF16),
        scratch_shapes=[pltpu.VMEM((m, n), F32)],
        compiler_params=_cp(("arbitrary", "arbitrary"), 56),
    )(a, b)


def _wgrad_in(h, parts, cb, tk, name):
    t, d = h.shape
    dc = parts[0].shape[1]
    per = dc // cb
    nblk = len(parts) * per
    nk = t // tk

    def body(h_ref, p0, p1, p2, p3, o_ref, acc_ref):
        s = pl.program_id(0)

        @pl.when(s == 0)
        def _():
            acc_ref[...] = jnp.zeros_like(acc_ref)

        hb = h_ref[...]
        for p, p_ref in enumerate((p0, p1, p2, p3)):
            acc_ref[p] += _dot_tn(hb, p_ref[...])

        @pl.when(s == nk - 1)
        def _():
            for k in range(nblk):
                o_ref[k] = acc_ref[k // per, :, (k % per) * cb:(k % per + 1) * cb].astype(BF16)

    return pl.pallas_call(
        body, name=name, grid=(nk,),
        in_specs=[pl.BlockSpec((tk, d), lambda s: (s, 0))] + [pl.BlockSpec((tk, dc), lambda s: (s, 0))] * len(parts),
        out_specs=pl.BlockSpec((nblk, d, cb), lambda s: (0, 0, 0)),
        out_shape=jax.ShapeDtypeStruct((nblk, d, cb), BF16),
        scratch_shapes=[pltpu.VMEM((len(parts), d, dc), F32)],
        compiler_params=_cp(("arbitrary",), 56),
    )(h, *parts)


def _mixout_bwd(dxo, ym, cv, hr, proj, vec, lnv, wout, tm, name, host=None):
    t, d = dxo.shape
    dc = cv.shape[1]
    nt = t // tm

    def body(ins, outs, scr):
        dxo_ref, ym_ref, cv_ref, hr_ref, uy_ref, vec_ref, lnv_ref, w_ref = ins
        dym_ref, dcv_ref, dhr_ref, duy_ref, dln_ref, dgt_ref = outs
        i = pl.program_id(0)

        @pl.when(i == 0)
        def _():
            dln_ref[...] = jnp.zeros_like(dln_ref)
            dgt_ref[...] = jnp.zeros_like(dgt_ref)

        dxo_v = dxo_ref[...]
        dym = (vec_ref[3:4, :] * dxo_v).astype(BF16)
        dym_ref[...] = dym
        dgt_ref[0:1, :] += jnp.sum(dxo_v * ym_ref[...].astype(F32), axis=0, keepdims=True)
        dyc = _dot_nt(dym, w_ref[0:dc, :])
        dyr = _dot_nt(dym, w_ref[dc:, :])
        rs, chat, z, sg = _ln_silu(cv_ref[...], lnv_ref)
        dz = dyc * (sg * (1.0 + z * (1.0 - sg)))
        dln_ref[0:1, :] += jnp.sum(dz * chat, axis=0, keepdims=True)
        dln_ref[1:2, :] += jnp.sum(dz, axis=0, keepdims=True)
        dchat = dz * lnv_ref[0:1, :]
        dcv_ref[...] = rs * (dchat - jnp.mean(dchat, axis=-1, keepdims=True)
                             - chat * jnp.mean(dchat * chat, axis=-1, keepdims=True))
        ge, dge = _gelu_and_grad(uy_ref[...])
        dhr_ref[...] = dyr * ge
        duy_ref[...] = (dyr * hr_ref[...] * dge).astype(BF16)

    tile_d = pl.BlockSpec((tm, d), lambda i: (i, 0))
    tile_c = pl.BlockSpec((tm, dc), lambda i: (i, 0))
    return _hosted_call(
        body, host, name, (nt,),
        [tile_d, tile_d, tile_c, tile_c,
         pl.BlockSpec((tm, dc), lambda i: (i, 3)),
         pl.BlockSpec((8, d), lambda i: (0, 0)),
         pl.BlockSpec((8, dc), lambda i: (0, 0)),
         pl.BlockSpec((d, d), lambda i: (0, 0))],
        [tile_d, tile_c, tile_c, tile_c,
         pl.BlockSpec((8, dc), lambda i: (0, 0)),
         pl.BlockSpec((8, d), lambda i: (0, 0))],
        [jax.ShapeDtypeStruct((t, d), BF16), jax.ShapeDtypeStruct((t, dc), F32),
         jax.ShapeDtypeStruct((t, dc), F32), jax.ShapeDtypeStruct((t, dc), BF16),
         jax.ShapeDtypeStruct((8, dc), F32), jax.ShapeDtypeStruct((8, d), F32)],
        [], ("arbitrary",), 48, (dxo, ym, cv, hr, proj, vec, lnv, wout))


def _conv_bwd(proj, dcv, cw32, name):
    t = proj.shape[0]
    nblk = cw32.shape[1] // LANES
    ch = min(t, 128)

    def body(val_ref, gate_ref, dcv_ref, cw_ref, dval_ref, dgate_ref, dcw_ref, extu_ref, extd_ref):
        val = val_ref[...]
        sg = _sigmoid(gate_ref[...])
        extu_ref[0:32, :] = jnp.zeros((32, LANES), F32)
        extu_ref[32:, :] = val * sg
        dcv_v = dcv_ref[...]
        extd_ref[0:t, :] = dcv_v
        extd_ref[t:, :] = jnp.zeros((32, LANES), F32)
        for r in range(t // ch):
            acc = jnp.zeros((ch, LANES), F32)
            for k in range(CONV_W):
                off = r * ch + (CONV_W - 1 - k)
                acc = acc + cw_ref[k:k + 1, :] * extd_ref[off:off + ch, :]
            rows = slice(r * ch, (r + 1) * ch)
            sg_r = _sigmoid(gate_ref[rows, :])
            dval_ref[rows, :] = (acc * sg_r).astype(BF16)
            dgate_ref[rows, :] = (acc * val_ref[rows, :] * sg_r * (1.0 - sg_r)).astype(BF16)
        for k in range(CONV_W):
            off = 32 - (CONV_W - 1 - k)
            dcw_ref[k:k + 1, :] = jnp.sum(dcv_v * extu_ref[off:off + t, :], axis=0, keepdims=True)
        dcw_ref[31:32, :] = jnp.sum(dcv_v, axis=0, keepdims=True)

    blk = lambda off: pl.BlockSpec((t, LANES), lambda c: (0, off + c))
    return pl.pallas_call(
        body, name=name, grid=(nblk,),
        in_specs=[blk(0), blk(nblk), blk(0), pl.BlockSpec((32, LANES), lambda c: (0, c))],
        out_specs=[blk(0), blk(0), pl.BlockSpec((32, LANES), lambda c: (0, c))],
        out_shape=[jax.ShapeDtypeStruct((t, nblk * LANES), BF16), jax.ShapeDtypeStruct((t, nblk * LANES), BF16),
                   jax.ShapeDtypeStruct((32, nblk * LANES), F32)],
        scratch_shapes=[pltpu.VMEM((t + 32, LANES), F32), pltpu.VMEM((t + 32, LANES), F32)],
        compiler_params=_cp(("arbitrary",), 56),
    )(proj, proj, dcv, cw32)


def _rnn_bwd(proj, hr, dhr, rw8, rvec, wab, wib, name, host=None):
    t = proj.shape[0]
    nblk = rvec.shape[1] // LANES

    def body(ins, outs, scr):
        ux_ref, h_ref, dh_ref, rw_ref, rvec_ref, wa_ref, wi_ref = ins
        dux_ref, sm_ref, dwa_ref, dwi_ref = outs
        ext_ref, extd_ref = scr
        xr = _rnn_conv(ux_ref[...], rw_ref, rvec_ref, ext_ref)
        xb, r, ig, ls, a, mult = _rglru_gates(xr, wa_ref, wi_ref, rvec_ref)
        row = lax.broadcasted_iota(jnp.int32, (t, LANES), 0)
        a_next = jnp.where(row < t - 1, pltpu.roll(a, t - 1, 0), 0.0)
        g = _scan_rev(a_next, dh_ref[...])
        hprev = jnp.where(row >= 1, pltpu.roll(h_ref[...], 1, 0), 0.0)
        da = g * hprev
        dmult = g * (ig * xr)
        dig = g * mult * xr
        dxr = g * mult * ig
        dlog_a = a * (da - dmult * a / mult)
        dr = dlog_a * (RG_C * ls)
        dls = RG_C * jnp.sum(dlog_a * r, axis=0, keepdims=True)
        dpr = dr * r * (1.0 - r)
        dpi = dig * ig * (1.0 - ig)
        dprb = dpr.astype(BF16)
        dpib = dpi.astype(BF16)
        dxr = dxr + _dot_nt(dprb, wa_ref[...]) + _dot_nt(dpib, wi_ref[...])
        dwa_ref[...] = _dot_tn(xb, dprb)
        dwi_ref[...] = _dot_tn(xb, dpib)
        extd_ref[0:t, :] = dxr
        extd_ref[t:, :] = jnp.zeros((8, LANES), F32)
        dux = rw_ref[RNN_CONV_W - 1:RNN_CONV_W, :] * dxr
        for k in range(RNN_CONV_W - 1):
            d = RNN_CONV_W - 1 - k
            dux = dux + rw_ref[k:k + 1, :] * extd_ref[d:d + t, :]
        dux_ref[...] = dux.astype(BF16)
        for k in range(RNN_CONV_W):
            d = RNN_CONV_W - 1 - k
            sm_ref[k:k + 1, :] = jnp.sum(dxr * ext_ref[8 - d:8 - d + t, :], axis=0, keepdims=True)
        sm_ref[4:5, :] = jnp.sum(dxr, axis=0, keepdims=True)
        sm_ref[5:6, :] = jnp.sum(dpr, axis=0, keepdims=True)
        sm_ref[6:7, :] = jnp.sum(dpi, axis=0, keepdims=True)
        sm_ref[7:8, :] = dls * _sigmoid(-rvec_ref[3:4, :])

    blk = lambda off: pl.BlockSpec((t, LANES), lambda c: (0, off + c))
    sq = pl.BlockSpec((None, LANES, LANES), lambda c: (c, 0, 0))
    return _hosted_call(
        body, host, name, (nblk,),
        [blk(2 * nblk), blk(0), blk(0),
         pl.BlockSpec((8, LANES), lambda c: (0, c)),
         pl.BlockSpec((8, LANES), lambda c: (0, c)), sq, sq],
        [blk(0), pl.BlockSpec((8, LANES), lambda c: (0, c)), sq, sq],
        [jax.ShapeDtypeStruct((t, nblk * LANES), BF16), jax.ShapeDtypeStruct((8, nblk * LANES), F32),
         jax.ShapeDtypeStruct((nblk, LANES, LANES), F32), jax.ShapeDtypeStruct((nblk, LANES, LANES), F32)],
        [pltpu.VMEM((t + 8, LANES), F32), pltpu.VMEM((t + 8, LANES), F32)],
        ("arbitrary",), 60, (proj, hr, dhr, rw8, rvec, wab, wib))


def _mixin_bwd(dxo, x, parts, vec, win, f, nvec, tm, name):
    t, d = x.shape
    nb, _, cb = win.shape
    dc = parts[0].shape[1]
    per = dc // cb
    nt = t // tm

    def body(dxo_ref, x_ref, p0, p1, p2, p3, vec_ref, w_ref, f_ref, nvec_ref, dx_ref, dvec_ref, df_ref, dgt_ref):
        i = pl.program_id(0)

        @pl.when(i == 0)
        def _():
            dvec_ref[...] = jnp.zeros_like(dvec_ref)
            dgt_ref[...] = jnp.zeros_like(dgt_ref)

        prefs = (p0, p1, p2, p3)
        dh = jnp.zeros((tm, d), F32)
        for k in range(nb):
            dh = dh + _dot_nt(prefs[k // per][:, (k % per) * cb:(k % per + 1) * cb], w_ref[k])
        dx = dxo_ref[...] + _adaln_bwd(x_ref[...], dh, vec_ref, dvec_ref)
        dx_ref[...] = dx
        _emit_df(dx, f_ref, nvec_ref, df_ref, dgt_ref)

        @pl.when(i == nt - 1)
        def _():
            _adaln_finish(vec_ref, dvec_ref)

    tile_d = pl.BlockSpec((tm, d), lambda i: (i, 0))
    tile_c = pl.BlockSpec((tm, dc), lambda i: (i, 0))
    tab = pl.BlockSpec((8, d), lambda i: (0, 0))
    return pl.pallas_call(
        body, name=name, grid=(nt,),
        in_specs=[tile_d, tile_d, tile_c, tile_c, tile_c, tile_c, tab,
                  pl.BlockSpec((nb, d, cb), lambda i: (0, 0, 0)), tile_d, tab],
        out_specs=[tile_d, tab, tile_d, tab],
        out_shape=[jax.ShapeDtypeStruct((t, d), F32), jax.ShapeDtypeStruct((8, d), F32),
                   jax.ShapeDtypeStruct((t, d), BF16), jax.ShapeDtypeStruct((8, d), F32)],
        compiler_params=_cp(("arbitrary",), 48),
    )(dxo, x, *parts, vec, win, f, nvec)


def _coords():
    return lax.axis_index("x"), lax.axis_index("y"), lax.axis_index("c")


def _flip(v, bit):
    return 1 - v if bit else v


def _gather_copy(outs, send_sems, recv_sems, a, k, block, to, src=None):
    dst = outs[a].at[block]
    return pltpu.make_async_remote_copy(
        src_ref=dst if src is None else src, dst_ref=dst,
        send_sem=send_sems.at[a, k], recv_sem=recv_sems.at[a, k],
        device_id=to, device_id_type=MESH_IDS)


def _gather_start(ins, outs, send_sems, recv_sems, loc_sems):
    x, y, c = _coords()
    me = 4 * x + 2 * y + c
    for a in range(len(ins)):
        pltpu.make_async_copy(ins[a], outs[a].at[me], loc_sems.at[a]).start()
    for a in range(len(ins)):
        _gather_copy(outs, send_sems, recv_sems, a, 0, me, (x, y, 1 - c), src=ins[a]).start()
        for j, (cx, cy) in enumerate([(1 - x, y), (x, 1 - y), (1 - x, 1 - y)]):
            _gather_copy(outs, send_sems, recv_sems, a, 1 + j, me, (cx, cy, c), src=ins[a]).start()


def _gather_finish(ins, outs, send_sems, recv_sems, loc_sems):
    x, y, c = _coords()
    me = 4 * x + 2 * y + c
    sib = (x, y, 1 - c)
    chips = [(1 - x, y), (x, 1 - y), (1 - x, 1 - y)]
    n = len(ins)
    for a in range(n):
        for j, (cx, cy) in enumerate(chips):
            blk = 4 * cx + 2 * cy + c
            _gather_copy(outs, send_sems, recv_sems, a, 1 + j, blk, sib).wait_recv()
            _gather_copy(outs, send_sems, recv_sems, a, 4 + j, blk, sib).start()
    for a in range(n):
        _gather_copy(outs, send_sems, recv_sems, a, 0, 4 * x + 2 * y + (1 - c), sib).wait_recv()
        for j, (cx, cy) in enumerate(chips):
            _gather_copy(outs, send_sems, recv_sems, a, 4 + j, 4 * cx + 2 * cy + (1 - c), sib).wait_recv()
    for a in range(n):
        _gather_copy(outs, send_sems, recv_sems, a, 0, me, sib, src=ins[a]).wait_send()
        for j, (cx, cy) in enumerate(chips):
            _gather_copy(outs, send_sems, recv_sems, a, 1 + j, me, (cx, cy, c), src=ins[a]).wait_send()
            _gather_copy(outs, send_sems, recv_sems, a, 4 + j, 4 * cx + 2 * cy + c, sib).wait_send()
        pltpu.make_async_copy(ins[a], outs[a].at[me], loc_sems.at[a]).wait()


def _gather_shapes(shards):
    return [jax.ShapeDtypeStruct((NDEV,) + s.shape, s.dtype) for s in shards]


def _gather_sems(n):
    return [pltpu.SemaphoreType.DMA((n, 7)), pltpu.SemaphoreType.DMA((n, 7)), pltpu.SemaphoreType.DMA((n,))]


def _sibling_copies(ins, outs, send_sems, recv_sems):
    x, y, c = _coords()
    return [pltpu.make_async_remote_copy(
        src_ref=ins[a].at[2 * q + (1 - c)], dst_ref=outs[a].at[q],
        send_sem=send_sems.at[a, q], recv_sem=recv_sems.at[a, q],
        device_id=(x, y, 1 - c), device_id_type=MESH_IDS) for a in range(len(ins)) for q in range(4)]


def _sibling_shapes(parts):
    return [jax.ShapeDtypeStruct((4,) + p.shape[1:], p.dtype) for p in parts]


def _chips_copies(ins, outs, send_sems, recv_sems):
    x, y, c = _coords()
    copies = []
    for a in range(len(ins)):
        for k, (kx, ky) in enumerate([(1, 0), (0, 1), (1, 1)]):
            tx, ty = _flip(x, kx), _flip(y, ky)
            copies.append(pltpu.make_async_remote_copy(
                src_ref=ins[a].at[2 * tx + ty], dst_ref=outs[a].at[k],
                send_sem=send_sems.at[a, k], recv_sem=recv_sems.at[a, k],
                device_id=(tx, ty, c), device_id_type=MESH_IDS))
    return copies


def _chips_shapes(sums):
    return [jax.ShapeDtypeStruct((3,) + s.shape[1:], s.dtype) for s in sums]


def _direct_copies(ins, outs, send_sems, recv_sems):
    x, y, c = _coords()
    me = 4 * x + 2 * y + c
    copies = []
    for a in range(len(ins)):
        for k in range(1, NDEV):
            kx, ky, kc = (k >> 2) & 1, (k >> 1) & 1, k & 1
            copies.append(pltpu.make_async_remote_copy(
                src_ref=ins[a], dst_ref=outs[a].at[me],
                send_sem=send_sems.at[a, k - 1], recv_sem=recv_sems.at[a, k - 1],
                device_id=(_flip(x, kx), _flip(y, ky), _flip(c, kc)), device_id_type=MESH_IDS))
    return copies


class _Exchange:
    def __init__(self, kind, arrays):
        self.kind, self.arrays, self.n = kind, list(arrays), len(arrays)
        self.nsem = 3 if kind in ("gather", "direct") else 2

    def out_shapes(self):
        return {"gather": _gather_shapes, "direct": _gather_shapes, "sibling": _sibling_shapes,
                "chips": _chips_shapes}[self.kind](self.arrays)

    def sems(self):
        if self.kind in ("gather", "direct"):
            return _gather_sems(self.n)
        k = {"sibling": 4, "chips": 3}[self.kind]
        return [pltpu.SemaphoreType.DMA((self.n, k)), pltpu.SemaphoreType.DMA((self.n, k))]

    def _copies(self, ins, outs, sems):
        if self.kind == "direct":
            x, y, c = _coords()
            own = [pltpu.make_async_copy(ins[a], outs[a].at[4 * x + 2 * y + c], sems[2].at[a]) for a in range(self.n)]
            return own + _direct_copies(ins, outs, sems[0], sems[1])
        return {"sibling": _sibling_copies, "chips": _chips_copies}[self.kind](ins, outs, *sems)

    def start(self, ins, outs, sems):
        if self.kind == "gather":
            _gather_start(ins, outs, *sems)
        else:
            for cpy in self._copies(ins, outs, sems):
                cpy.start()

    def finish(self, ins, outs, sems):
        if self.kind == "gather":
            _gather_finish(ins, outs, *sems)
        else:
            for cpy in self._copies(ins, outs, sems):
                cpy.wait()


class _Exchanges:
    def __init__(self, *parts):
        self.parts = parts
        self.arrays = [a for p in parts for a in p.arrays]
        self.n = len(self.arrays)

    def out_shapes(self):
        return [s for p in self.parts for s in p.out_shapes()]

    def sems(self):
        return [s for p in self.parts for s in p.sems()]

    def _each(self, ins, outs, sems):
        a = s = 0
        for p in self.parts:
            yield p, ins[a:a + p.n], outs[a:a + p.n], sems[s:s + p.nsem]
            a, s = a + p.n, s + p.nsem

    def start(self, ins, outs, sems):
        for p, i, o, s in self._each(ins, outs, sems):
            p.start(i, o, s)

    def finish(self, ins, outs, sems):
        for p, i, o, s in self._each(ins, outs, sems):
            p.finish(i, o, s)


def _hosted_call(body, host, name, grid, in_specs, out_specs, out_shape, scratch, sem, vmem_mb, args):
    n = host.n if host else 0
    ni, no, ns = len(in_specs), len(out_specs), len(scratch)

    def full(*refs):
        ins, h_in = refs[:ni], refs[ni:ni + n]
        outs, h_out = refs[ni + n:ni + n + no], refs[ni + n + no:ni + 2 * n + no]
        scr, sems = refs[ni + 2 * n + no:ni + 2 * n + no + ns], refs[ni + 2 * n + no + ns:]
        if host and grid:
            first = functools.reduce(lambda a, b: a & b, [pl.program_id(k) == 0 for k in range(len(grid))])
            last = functools.reduce(lambda a, b: a & b, [pl.program_id(k) == g - 1 for k, g in enumerate(grid)])

            @pl.when(first)
            def _():
                host.start(h_in, h_out, sems)
        elif host:
            host.start(h_in, h_out, sems)

        body(ins, outs, scr)

        if host and grid:
            @pl.when(last)
            def _():
                host.finish(h_in, h_out, sems)
        elif host:
            host.finish(h_in, h_out, sems)

    anyspec = pl.BlockSpec(memory_space=pl.ANY)
    return pl.pallas_call(
        full, name=name, grid=grid,
        in_specs=list(in_specs) + [anyspec] * n, out_specs=list(out_specs) + [anyspec] * n,
        out_shape=list(out_shape) + (host.out_shapes() if host else []),
        scratch_shapes=list(scratch) + (host.sems() if host else []),
        compiler_params=_cp(sem, vmem_mb),
    )(*args, *(host.arrays if host else []))


def _exchange(host, name, after=()):
    n, na = host.n, len(after)

    def body(*refs):
        ins, outs, sems = refs[:n], refs[n + na:2 * n + na], refs[2 * n + na:]
        host.start(ins, outs, sems)
        host.finish(ins, outs, sems)

    anyspec = pl.BlockSpec(memory_space=pl.ANY)
    return pl.pallas_call(
        body, name=name, in_specs=[anyspec] * (n + na), out_specs=[anyspec] * n,
        out_shape=host.out_shapes(), scratch_shapes=host.sems(),
    )(*host.arrays, *after)


def _chips_split_start(sums, name):
    n = len(sums)
    hbm = pl.BlockSpec(memory_space=pltpu.HBM)
    sem = pl.BlockSpec(memory_space=pltpu.SEMAPHORE)

    def body(*refs):
        ins, lands = refs[:n], refs[n:2 * n]
        sems = refs[2 * n:2 * n + 6 * n]
        token = refs[-1]
        for cpy in _chips_copies(ins, lands, _SemGrid(sems[:3 * n], 3), _SemGrid(sems[3 * n:], 3)):
            cpy.start()
        token[...] = jnp.zeros_like(token)

    land_shapes = _chips_shapes(sums)
    lands = [pltpu.with_memory_space_constraint(lax.empty(s.shape, s.dtype), pltpu.HBM) for s in land_shapes]
    return pl.pallas_call(
        body, name=name,
        out_shape=(*[pltpu.SemaphoreType.DMA(())] * (6 * n),
                   *[pltpu.HBM(s.shape, s.dtype) for s in sums],
                   *[pltpu.HBM(s.shape, s.dtype) for s in land_shapes],
                   jax.ShapeDtypeStruct((8, LANES), F32)),
        in_specs=[hbm] * (2 * n),
        out_specs=(*[sem] * (6 * n), *[hbm] * (2 * n), pl.BlockSpec(memory_space=pltpu.VMEM)),
        input_output_aliases={i: 6 * n + i for i in range(2 * n)},
        compiler_params=pltpu.CompilerParams(has_side_effects=pltpu.SideEffectType.DATAFLOW_SIDE_EFFECTING),
    )(*[pltpu.with_memory_space_constraint(s, pltpu.HBM) for s in sums], *lands)


class _SemGrid:
    def __init__(self, sems, k):
        self.sems, self.k = sems, k

    @property
    def at(self):
        return self

    def __getitem__(self, idx):
        return self.sems[idx[0] * self.k + idx[1]]


def _chips_split_wait(started, n, after, name):
    sems = started[:6 * n]
    thru = started[6 * n:8 * n]
    hbm = pl.BlockSpec(memory_space=pltpu.HBM)
    sem = pl.BlockSpec(memory_space=pltpu.SEMAPHORE)

    def body(*refs):
        ins, lands = refs[:n], refs[n:2 * n]
        s = refs[2 * n:2 * n + 6 * n]
        for cpy in _chips_copies(ins, lands, _SemGrid(s[:3 * n], 3), _SemGrid(s[3 * n:], 3)):
            cpy.wait_send()
            cpy.wait_recv()

    outs = pl.pallas_call(
        body, name=name,
        out_shape=tuple(pltpu.HBM(a.shape, a.dtype) for a in thru),
        in_specs=[hbm] * (2 * n) + [sem] * (6 * n) + [pl.BlockSpec(memory_space=pl.ANY)],
        out_specs=tuple([hbm] * (2 * n)),
        input_output_aliases={i: i for i in range(2 * n)},
        compiler_params=pltpu.CompilerParams(has_side_effects=pltpu.SideEffectType.DATAFLOW_SIDE_EFFECTING),
    )(*thru, *sems, after)
    return list(outs[n:])


def _chip_sum(part, recv, sel, tr, name):
    _, _, r, c = part.shape

    def body(sel_ref, p_ref, r_ref, cs_ref, own_ref):
        q = pl.program_id(1)
        s = p_ref[...].astype(F32) + r_ref[...].astype(F32)
        cs_ref[...] = s.astype(BF16)

        @pl.when(q == sel_ref[1])
        def _():
            own_ref[...] = s

    return pl.pallas_call(
        body, name=name,
        grid_spec=pltpu.PrefetchScalarGridSpec(
            num_scalar_prefetch=1, grid=(r // tr, 4),
            in_specs=[pl.BlockSpec((None, None, tr, c), lambda i, q, s: (q, s[0], i, 0)),
                      pl.BlockSpec((None, tr, c), lambda i, q, s: (q, i, 0))],
            out_specs=[pl.BlockSpec((None, tr, c), lambda i, q, s: (q, i, 0)),
                       pl.BlockSpec((tr, c), lambda i, q, s: (i, 0))]),
        out_shape=[jax.ShapeDtypeStruct((4, r, c), BF16), jax.ShapeDtypeStruct((r, c), F32)],
        compiler_params=_cp(("arbitrary", "arbitrary"), 48),
    )(sel, part, recv)


def _gather_direct(src_ref, buf_ref, send_sems, recv_sems):
    x, y, c = _coords()
    me = 4 * x + 2 * y + c
    buf_ref[me] = src_ref[...]
    copies = []
    for k in range(1, NDEV):
        kx, ky, kc = (k >> 2) & 1, (k >> 1) & 1, k & 1
        copies.append(pltpu.make_async_remote_copy(
            src_ref=src_ref, dst_ref=buf_ref.at[me],
            send_sem=send_sems.at[k - 1], recv_sem=recv_sems.at[k - 1],
            device_id=(_flip(x, kx), _flip(y, ky), _flip(c, kc)), device_id_type=MESH_IDS))
    for cpy in copies:
        cpy.start()
    for k in range(1, NDEV):
        kx, ky, kc = (k >> 2) & 1, (k >> 1) & 1, k & 1
        peer = 4 * _flip(x, kx) + 2 * _flip(y, ky) + _flip(c, kc)
        pltpu.make_async_remote_copy(
            src_ref=src_ref, dst_ref=buf_ref.at[peer],
            send_sem=send_sems.at[k - 1], recv_sem=recv_sems.at[k - 1],
            device_id=(x, y, c), device_id_type=MESH_IDS).wait_recv()
    for cpy in copies:
        cpy.wait_send()
    return me


def _mod_exchange(c_row, wmod, bmod, wfmod, bfmod, name, host=None):
    d = c_row.shape[1]
    nm, nf = wmod.shape[1], wfmod.shape[1]
    nw = nm + nf

    def body(ins, outs, scr):
        c_ref, wm_ref, bm_ref, wf_ref, bf_ref = ins
        cs_ref, mod_ref, fmod_ref = outs
        slab_ref, csbuf_ref, mslab_ref, mbuf_ref, s1, r1, s2, r2 = scr
        cv = c_ref[...]
        slab_ref[...] = jnp.broadcast_to(cv * _sigmoid(cv), (8, d))
        _gather_direct(slab_ref, csbuf_ref, s1, r1)
        for b in range(NDEV):
            cs_ref[b:b + 1, :] = csbuf_ref[b, 0:1, :]
        cs = cs_ref[...]
        mslab_ref[:, 0:nm] = jnp.dot(cs, wm_ref[...], precision=HI, preferred_element_type=F32) + bm_ref[...]
        mslab_ref[:, nm:] = jnp.dot(cs, wf_ref[...], precision=HI, preferred_element_type=F32) + bf_ref[...]
        me = _gather_direct(mslab_ref, mbuf_ref, s2, r2)
        mine = lax.broadcasted_iota(jnp.int32, (8, nw), 0) == me
        for k in range(NDEV):
            rowk = jnp.sum(jnp.where(mine, mbuf_ref[k], 0.0), axis=0, keepdims=True)
            mod_ref[k:k + 1, :] = rowk[:, 0:nm]
            fmod_ref[k:k + 1, :] = rowk[:, nm:]

    vm = pl.BlockSpec(memory_space=pltpu.VMEM)
    return _hosted_call(
        body, host, name, (), [vm] * 5, [vm] * 3,
        [jax.ShapeDtypeStruct((NDEV, d), F32), jax.ShapeDtypeStruct((NDEV, nm), F32),
         jax.ShapeDtypeStruct((NDEV, nf), F32)],
        [pltpu.VMEM((8, d), F32), pltpu.VMEM((NDEV, 8, d), F32),
         pltpu.VMEM((8, nw), F32), pltpu.VMEM((NDEV, 8, nw), F32),
         pltpu.SemaphoreType.DMA((7,)), pltpu.SemaphoreType.DMA((7,)),
         pltpu.SemaphoreType.DMA((7,)), pltpu.SemaphoreType.DMA((7,))],
        None, 40, (c_row, wmod, bmod, wfmod, bfmod))


def _table_sum(tabs, name):
    n = len(tabs)

    def body(*refs):
        for a in range(n):
            tot = refs[a][0]
            for k in range(1, NDEV):
                tot = tot + refs[a][k]
            refs[n + a][...] = tot

    vm = pl.BlockSpec(memory_space=pltpu.VMEM)
    return pl.pallas_call(
        body, name=name, in_specs=[vm] * n, out_specs=[vm] * n,
        out_shape=[jax.ShapeDtypeStruct(tb.shape[1:], F32) for tb in tabs],
    )(*tabs)


def _adamw_math(w, g, m, v):
    m = ADAM_B1 * m + (1.0 - ADAM_B1) * g
    v = ADAM_B2 * v + (1.0 - ADAM_B2) * (g * g)
    m_hat = m / (1.0 - ADAM_B1 ** ADAM_STEP)
    v_hat = v / (1.0 - ADAM_B2 ** ADAM_STEP)
    delta = -ADAM_LR * (m_hat / (jnp.sqrt(v_hat) + ADAM_EPS) + ADAM_WD * w)
    return delta, m, v


def _adamw_small(params, name):
    n = len(params)

    def body(*refs):
        for p in range(n):
            w_ref, g_ref, m_ref, v_ref = refs[4 * p:4 * p + 4]
            d_ref, mo_ref, vo_ref = refs[4 * n + 3 * p:4 * n + 3 * p + 3]
            d_ref[...], mo_ref[...], vo_ref[...] = _adamw_math(w_ref[...], g_ref[...], m_ref[...], v_ref[...])

    vm = pl.BlockSpec(memory_space=pltpu.VMEM)
    flat = [a for p in params for a in p]
    outs = pl.pallas_call(
        body, name=name, in_specs=[vm] * (4 * n), out_specs=[vm] * (3 * n),
        out_shape=[jax.ShapeDtypeStruct(p[0].shape, F32) for p in params for _ in range(3)])(*flat)
    return [outs[3 * p:3 * p + 3] for p in range(n)]


def _rs_final(own, recv, w, m, v, tr, name):
    r, c = own.shape

    def body(o_ref, r_ref, w_ref, m_ref, v_ref, g_ref, d_ref, mo_ref, vo_ref):
        g = o_ref[...] + r_ref[0].astype(F32) + r_ref[1].astype(F32) + r_ref[2].astype(F32)
        g_ref[...] = g
        d_ref[...], mo_ref[...], vo_ref[...] = _adamw_math(w_ref[...], g, m_ref[...], v_ref[...])

    tile = pl.BlockSpec((tr, c), lambda i: (i, 0))
    sds = jax.ShapeDtypeStruct((r, c), F32)
    return pl.pallas_call(
        body, name=name, grid=(r // tr,),
        in_specs=[tile, pl.BlockSpec((3, tr, c), lambda i: (0, i, 0)), tile, tile, tile],
        out_specs=[tile] * 4, out_shape=[sds] * 4,
        compiler_params=_cp(("arbitrary",), 48),
    )(own, recv, w, m, v)


def _mod_weight_update(cs, dm, w, m, v, tr, name):
    r, c = w.shape

    def body(cs_ref, dm_ref, w_ref, m_ref, v_ref, g_ref, d_ref, mo_ref, vo_ref):
        g = lax.dot_general(cs_ref[...], dm_ref[...], (((0,), (0,)), ((), ())),
                            precision=HI, preferred_element_type=F32)
        g_ref[...] = g
        d_ref[...], mo_ref[...], vo_ref[...] = _adamw_math(w_ref[...], g, m_ref[...], v_ref[...])

    tile = pl.BlockSpec((tr, c), lambda i: (i, 0))
    sds = jax.ShapeDtypeStruct((r, c), F32)
    return pl.pallas_call(
        body, name=name, grid=(r // tr,),
        in_specs=[pl.BlockSpec((NDEV, tr), lambda i: (0, i)), pl.BlockSpec((NDEV, c), lambda i: (0, 0)),
                  tile, tile, tile],
        out_specs=[tile] * 4, out_shape=[sds] * 4,
        compiler_params=_cp(("arbitrary",), 48),
    )(cs, dm, w, m, v)


def _rows(*vs):
    d = vs[0].shape[-1]
    rows = [v.reshape(1, d) for v in vs]
    return jnp.concatenate(rows + [jnp.zeros((8 - len(rows), d), F32)], axis=0)


def _block_diag_pairs(w):
    hd = w.shape[-1]
    z = jnp.zeros((w.shape[0] // 2, hd, hd), w.dtype)
    top = jnp.concatenate([w[0::2], z], axis=2)
    bot = jnp.concatenate([z, w[1::2]], axis=2)
    return jnp.concatenate([top, bot], axis=1).astype(BF16)


def _diag_pairs(g):
    hd = g.shape[-1] // 2
    both = jnp.stack([g[:, :hd, :hd], g[:, hd:, hd:]], axis=1)
    return both.reshape(2 * g.shape[0], hd, hd)


def kernel(x, c, w_mod, b_mod, g_ffn1, w_ffn1_in, w_ffn1_out, g_mix, w_in, conv_w, conv_b, ln_g, ln_b, rnn_conv_w, rnn_conv_b, w_a, b_a, w_i, b_i, lru_lambda, w_out, g_ffn2, w_ffn2_in, w_ffn2_out, w_fmod, b_fmod, g_final, loss_target, m_w_mod, m_b_mod, m_g_ffn1, m_w_ffn1_in, m_w_ffn1_out, m_g_mix, m_w_in, m_conv_w, m_conv_b, m_ln_g, m_ln_b, m_rnn_conv_w, m_rnn_conv_b, m_w_a, m_b_a, m_w_i, m_b_i, m_lru_lambda, m_w_out, m_g_ffn2, m_w_ffn2_in, m_w_ffn2_out, m_w_fmod, m_b_fmod, m_g_final, v_w_mod, v_b_mod, v_g_ffn1, v_w_ffn1_in, v_w_ffn1_out, v_g_mix, v_w_in, v_conv_w, v_conv_b, v_ln_g, v_ln_b, v_rnn_conv_w, v_rnn_conv_b, v_w_a, v_b_a, v_w_i, v_b_i, v_lru_lambda, v_w_out, v_g_ffn2, v_w_ffn2_in, v_w_ffn2_out, v_w_fmod, v_b_fmod, v_g_final):
    t, d = x.shape[1], x.shape[2]
    fb = w_ffn1_in.shape[2]
    nm = w_mod.shape[2]
    nf = w_fmod.shape[1]
    dc = conv_b.shape[1]
    cl = conv_w.shape[2]
    tm = min(TOKEN_TILE, t)
    tk = min(WGRAD_TILE, t)
    nk = t // tk
    me = 4 * lax.axis_index("x") + 2 * lax.axis_index("y") + lax.axis_index("c")

    tr = jnp.transpose
    bmod_l = lax.dynamic_slice(b_mod, (0, me * nm), (1, nm))
    bfmod_l = lax.dynamic_slice(b_fmod.reshape(1, -1), (0, me * nf), (1, nf))
    cwl = jnp.concatenate([conv_w[0], jnp.zeros((1, cl), F32), rnn_conv_w[0], jnp.zeros((4, cl), F32)], axis=0)
    cs, mod_rows, fmod_rows, wi1, wo1, cwg = _mod_exchange(
        c, w_mod[0], bmod_l, w_fmod, bfmod_l, "mod_and_gather_ffn1",
        host=_Exchange("gather", [tr(w_ffn1_in[0]).astype(BF16), w_ffn1_out[0].astype(BF16), cwl]))
    wi1 = wi1.reshape(2, 4, fb, d)
    wo1 = wo1.reshape(4 * fb, d)
    mod = mod_rows.reshape(9, d)
    fmod = fmod_rows.reshape(2, d)
    vec1 = _rows(g_ffn1, mod[0], mod[1], mod[2])
    vecm = _rows(g_mix, mod[3], mod[4], mod[5])
    vec3 = _rows(g_ffn2, mod[6], mod[7], mod[8])
    vecf = _rows(g_final, fmod[0], fmod[1])

    xin = x[0]
    later = [w_in[0].astype(BF16), w_out[0].astype(BF16), tr(w_ffn2_in[0]).astype(BF16), w_ffn2_out[0].astype(BF16)]
    x1, h1, gu1, f1, win, wout, wi2, wo2 = _ffn_fwd(xin, vec1, wi1, wo1, tm, "ffn1_fwd",
                                                    host=_Exchange("gather", later))
    wi2 = wi2.reshape(2, 4, fb, d)
    wo2 = wo2.reshape(4 * fb, d)
    wout = wout.reshape(d, d)
    h2, proj = _mix_in(x1, vecm, win, tm, "mix_in")
    lnv = _rows(ln_g, ln_b)
    rvec = _rows(rnn_conv_b, b_a, b_i, lru_lambda)
    wab = _block_diag_pairs(w_a[0])
    wib = _block_diag_pairs(w_i[0])
    cwf = jnp.transpose(cwg, (1, 0, 2)).reshape(40, NDEV * cl)
    cw32 = jnp.concatenate([cwf[0:CONV_W], conv_b], axis=0)
    rw8 = cwf[32:40]

    (cv,) = _conv_fwd(proj, cw32, "conv_fwd")
    hr, yr = _rnn_fwd(proj, rw8, rvec, wab, wib, "rnn_fwd")
    x2, ym, ycat = _mix_out(x1, cv, yr, vecm, lnv, wout, tm, "mix_out")
    x3, h3, gu3, f3 = _ffn_fwd(x2, vec3, wi2, wo2, tm, "ffn2_fwd")

    dx3, dvf, df3, dva3 = _final(x3, loss_target[0], vecf, f3, vec3, tm, "final_loss")
    a_tok = lambda width: pl.BlockSpec((tk, width), lambda k, s: (s, 0))
    blk3 = lambda width: pl.BlockSpec((None, tk, width), lambda k, s: (k, s, 0))
    sel = jnp.stack([lax.axis_index("c"), 2 * lax.axis_index("x") + lax.axis_index("y")]).astype(jnp.int32)
    row_tile = {"w_ffn1_in": fb // 4, "w_ffn1_out": fb // 4, "w_in": 512, "w_out": 128,
                "w_ffn2_in": fb // 4, "w_ffn2_out": fb // 4}

    def chip_sums(names, partials, from_sib):
        out = [_chip_sum(p.reshape((4, 2) + p.shape[1:]), r, sel, p.shape[1], "chip_sum_" + nm_)
               for nm_, p, r in zip(names, partials, from_sib)]
        return [o[0] for o in out], [o[1] for o in out]

    dgu3, p_wi2, p_wo2 = _ffn_bwd_w(df3, gu3, h3, wo2, tm, "ffn2_bwd_w")
    p_wi2 = p_wi2.reshape(NDEV, fb, d)
    p_wo2 = p_wo2.reshape(NDEV, fb // 2, d)
    names2 = ["w_ffn2_in", "w_ffn2_out"]
    dx2, dv3, sib_wi2, sib_wo2 = _ffn_bwd_in(dx3, x2, vec3, dgu3, wi2, tm, "ffn2_bwd_in",
                                             host=_Exchange("sibling", [p_wi2, p_wo2]))
    (s_wi2, s_wo2), owns2 = chip_sums(names2, [p_wi2, p_wo2], [sib_wi2, sib_wo2])
    dym, dcv, dhr, duy, dln, dgt2, r_wo2 = _mixout_bwd(
        dx2, ym, cv, hr, proj, vecm, lnv, wout, tm, "mixout_bwd", host=_Exchange("chips", [s_wo2]))
    dval, dgate, dcw = _conv_bwd(proj, dcv, cw32, "conv_bwd")
    dux, rsm, dwab, dwib, r_wi2 = _rnn_bwd(proj, hr, dhr, rw8, rvec, wab, wib, "rnn_bwd",
                                           host=_Exchange("chips", [s_wi2]))
    parts = [dval, dgate, dux, duy]
    dx1, dvm, df1, dva1 = _mixin_bwd(dx2, x1, parts, vecm, win, f1, vec1, tm, "mixin_bwd")
    p_wout = _mm_tn(ycat, dym, a_tok(d), a_tok(d), 1, nk, d, d, "wgrad_out").reshape(NDEV, d // NDEV, d)
    p_win = _wgrad_in(h2, parts, win.shape[2], min(tk, 1024), "wgrad_in")
    namesm = ["w_in", "w_out"]
    sumsm, ownsm = chip_sums(namesm, [p_win, p_wout],
                             _exchange(_Exchange("sibling", [p_win, p_wout]), "rs_sibling_mix"))
    lane_pad = lambda v: jnp.concatenate([v, jnp.zeros_like(v)], axis=1)
    early = jnp.concatenate([dva3, dv3, dvf, dvm, dgt2, dva1, dcw.reshape(16, d), lane_pad(dln), lane_pad(rsm),
                             _diag_pairs(dwab).reshape(32, d), _diag_pairs(dwib).reshape(32, d)], axis=0)
    dgu1, p_wi1, p_wo1, r_win, r_wout, all_early = _ffn_bwd_w(
        df1, gu1, h1, wo1, tm, "ffn1_bwd_w",
        host=_Exchanges(_Exchange("chips", sumsm), _Exchange("direct", [early])))
    p_wi1 = p_wi1.reshape(NDEV, fb, d)
    p_wo1 = p_wo1.reshape(NDEV, fb // 2, d)
    names1 = ["w_ffn1_in", "w_ffn1_out"]
    sums1, owns1 = chip_sums(names1, [p_wi1, p_wo1],
                             _exchange(_Exchange("sibling", [p_wi1, p_wo1]), "rs_sibling_ffn1"))
    started = _chips_split_start(sums1, "rs_chips_ffn1_start")
    dx0, dv1 = _ffn_bwd_in(dx1, xin, vec1 + started[-1][0:1, 0:1], dgu1, wi1, tm, "ffn1_bwd_in")
    from_chips = {"w_in": r_win, "w_out": r_wout, "w_ffn2_in": r_wi2, "w_ffn2_out": r_wo2}
    owns = dict(zip(namesm + names2 + names1, ownsm + owns2 + owns1))

    big = {"w_ffn1_in": (tr(w_ffn1_in[0]), tr(m_w_ffn1_in[0]), tr(v_w_ffn1_in[0])),
           "w_ffn1_out": (w_ffn1_out[0], m_w_ffn1_out[0], v_w_ffn1_out[0]),
           "w_in": (w_in[0], m_w_in[0], v_w_in[0]), "w_out": (w_out[0], m_w_out[0], v_w_out[0]),
           "w_ffn2_in": (tr(w_ffn2_in[0]), tr(m_w_ffn2_in[0]), tr(v_w_ffn2_in[0])),
           "w_ffn2_out": (w_ffn2_out[0], m_w_ffn2_out[0], v_w_ffn2_out[0])}
    res = {}

    def final_sum(nm_):
        out4 = _rs_final(owns[nm_], from_chips[nm_], *big[nm_], row_tile[nm_], "rs_final_" + nm_)
        res[nm_] = [(tr(o) if nm_ in ("w_ffn1_in", "w_ffn2_in") else o)[None] for o in out4]
        return out4[0]

    done = [final_sum(nm_) for nm_ in namesm + names2]
    dfm_all = jnp.concatenate([all_early[:, 17], all_early[:, 19]], axis=1)
    dfm_l = lax.dynamic_slice(dfm_all, (0, me * nf), (NDEV, nf))
    res["w_fmod"] = list(_mod_weight_update(cs, dfm_l, w_fmod, m_w_fmod, v_w_fmod, 256, "w_fmod_update"))
    (all_late,) = _exchange(_Exchange("direct", [dv1]), "late_table", after=done + [res["w_fmod"][0]])
    from_chips["w_ffn1_in"], from_chips["w_ffn1_out"] = _chips_split_wait(
        started, len(sums1), all_late, "rs_chips_ffn1_wait")
    for nm_ in names1:
        final_sum(nm_)
    te, tl = _table_sum([all_early, all_late], "table_sum")
    loss = jnp.sum(te[20])

    mod_rows_of = lambda e, l: [l[1], l[3], e[42], e[25], e[27], e[32], e[9], e[11], e[2]]
    dm_all = jnp.concatenate(mod_rows_of(jnp.swapaxes(all_early, 0, 1), jnp.swapaxes(all_late, 0, 1)), axis=1)
    dm_l = lax.dynamic_slice(dm_all, (0, me * nm), (NDEV, nm))
    res["w_mod"] = [o[None] for o in
                    _mod_weight_update(cs, dm_l, w_mod[0], m_w_mod[0], v_w_mod[0], 256, "w_mod_update")]

    dcw_f = te[48:64].reshape(32, dc)
    rsm_f = te[72:80, 0:dc]
    small_grads = {
        "b_mod": jnp.concatenate(mod_rows_of(te, tl)).reshape(1, 9 * d),
        "b_fmod": jnp.concatenate([te[17], te[19]]),
        "g_ffn1": tl[0:1], "g_mix": te[24:25], "g_ffn2": te[8:9], "g_final": te[16],
        "conv_w": lax.dynamic_slice(dcw_f, (0, me * cl), (CONV_W, cl))[None],
        "conv_b": dcw_f[31:32],
        "ln_g": te[64:65, 0:dc], "ln_b": te[65:66, 0:dc],
        "rnn_conv_w": lax.dynamic_slice(rsm_f, (0, me * cl), (RNN_CONV_W, cl))[None],
        "rnn_conv_b": rsm_f[4:5], "b_a": rsm_f[5:6], "b_i": rsm_f[6:7], "lru_lambda": rsm_f[7:8],
        "w_a": te[80:112].reshape(w_a.shape), "w_i": te[112:144].reshape(w_i.shape),
    }
    small_params = {
        "b_mod": (b_mod, m_b_mod, v_b_mod), "b_fmod": (b_fmod, m_b_fmod, v_b_fmod),
        "g_ffn1": (g_ffn1, m_g_ffn1, v_g_ffn1), "g_mix": (g_mix, m_g_mix, v_g_mix),
        "g_ffn2": (g_ffn2, m_g_ffn2, v_g_ffn2), "g_final": (g_final, m_g_final, v_g_final),
        "conv_w": (conv_w, m_conv_w, v_conv_w), "conv_b": (conv_b, m_conv_b, v_conv_b),
        "ln_g": (ln_g, m_ln_g, v_ln_g), "ln_b": (ln_b, m_ln_b, v_ln_b),
        "rnn_conv_w": (rnn_conv_w, m_rnn_conv_w, v_rnn_conv_w),
        "rnn_conv_b": (rnn_conv_b, m_rnn_conv_b, v_rnn_conv_b),
        "w_a": (w_a, m_w_a, v_w_a), "b_a": (b_a, m_b_a, v_b_a),
        "w_i": (w_i, m_w_i, v_w_i), "b_i": (b_i, m_b_i, v_b_i),
        "lru_lambda": (lru_lambda, m_lru_lambda, v_lru_lambda),
    }
    two_d = lambda w: (-1, w.shape[-1]) if w.ndim > 1 else (1, w.shape[0])
    small_names = list(small_grads)
    small_outs = _adamw_small(
        [(w.reshape(two_d(w)), small_grads[nm_].reshape(two_d(w)), m.reshape(two_d(w)), v.reshape(two_d(w)))
         for nm_ in small_names for (w, m, v) in [small_params[nm_]]], "adamw_small")
    for nm_, outs in zip(small_names, small_outs):
        shp = small_params[nm_][0].shape
        res[nm_] = [small_grads[nm_].reshape(shp)] + [o.reshape(shp) for o in outs]

    order = ["w_mod", "b_mod", "g_ffn1", "w_ffn1_in", "w_ffn1_out", "g_mix", "w_in", "conv_w", "conv_b",
             "ln_g", "ln_b", "rnn_conv_w", "rnn_conv_b", "w_a", "b_a", "w_i", "b_i", "lru_lambda", "w_out",
             "g_ffn2", "w_ffn2_in", "w_ffn2_out", "w_fmod", "b_fmod", "g_final"]
    return (loss, dx0[None], *[res[n][0] for n in order], *[res[n][1] for n in order],
            *[res[n][2] for n in order], *[res[n][3] for n in order])
```

```python
import functools
import math

import jax
import jax.numpy as jnp
from jax import lax
from jax.experimental import pallas as pl
from jax.experimental.pallas import tpu as pltpu

F32 = jnp.float32
BF16 = jnp.bfloat16
MESH_IDS = pl.DeviceIdType.MESH
NDEV = 8
EPS = 1e-6
RG_C = 8.0
CONV_W = 31
RNN_CONV_W = 4
LANES = 128
ADAM_LR = 0.001
ADAM_B1 = 0.9
ADAM_B2 = 0.999
ADAM_EPS = 1e-08
ADAM_WD = 0.01
ADAM_STEP = 10
SMALL_ROWS = 104
TOKEN_TILE = 512
WGRAD_TILE = 2048
ROW_GROUP = 16
HI = lax.Precision.HIGHEST


def _cp(sem, vmem_mb):
    return pltpu.CompilerParams(dimension_semantics=sem, vmem_limit_bytes=vmem_mb * 1024 * 1024)


def _dot(a, b):
    return jnp.dot(a, b, preferred_element_type=F32)


def _dot_nt(a, b):
    return lax.dot_general(a, b, (((1,), (1,)), ((), ())), preferred_element_type=F32)


def _dot_tn(a, b):
    return lax.dot_general(a, b, (((0,), (0,)), ((), ())), preferred_element_type=F32)


def _sigmoid(x):
    return 1.0 / (1.0 + jnp.exp(-x))


def _adaln(x, vec_ref):
    rstd = lax.rsqrt(jnp.mean(x * x, axis=-1, keepdims=True) + EPS)
    return (x * rstd) * vec_ref[0:1, :] * (1.0 + vec_ref[2:3, :]) + vec_ref[1:2, :]


def _adaln_bwd(x, dh, vec_ref, dvec_ref):
    rstd = lax.rsqrt(jnp.mean(x * x, axis=-1, keepdims=True) + EPS)
    xhat = x * rstd
    dvec_ref[0:1, :] += jnp.sum(dh * xhat, axis=0, keepdims=True)
    dvec_ref[1:2, :] += jnp.sum(dh, axis=0, keepdims=True)
    dxhat = dh * (vec_ref[0:1, :] * (1.0 + vec_ref[2:3, :]))
    return rstd * (dxhat - xhat * jnp.mean(dxhat * xhat, axis=-1, keepdims=True))


def _adaln_finish(vec_ref, dvec_ref):
    s = dvec_ref[0:1, :]
    dvec_ref[3:4, :] = vec_ref[0:1, :] * s
    dvec_ref[0:1, :] = (1.0 + vec_ref[2:3, :]) * s


def _gelu_and_grad(x):
    k0 = math.sqrt(2.0 / math.pi)
    x2 = x * x
    t = jnp.tanh(k0 * (x + 0.044715 * x * x2))
    g = 0.5 * x * (1.0 + t)
    dg = 0.5 * (1.0 + t) + 0.5 * x * (1.0 - t * t) * (k0 * (1.0 + 3.0 * 0.044715 * x2))
    return g, dg


def _log_sigmoid(x):
    z = jnp.exp(-jnp.abs(x))
    u = 1.0 + z
    d = u - 1.0
    log1p = jnp.where(d == 0.0, z, jnp.log(u) * (z / jnp.where(d == 0.0, 1.0, d)))
    return jnp.minimum(x, 0.0) - log1p


def _neg_expm1(x):
    series = -x * (1.0 + x * (0.5 + x * (1.0 / 6.0 + x * (1.0 / 24.0 + x * (1.0 / 120.0)))))
    return jnp.where(x > -0.05, series, 1.0 - jnp.exp(x))


SUBLANES = 8


def _doubling_scan(a, b, reverse):
    n = a.shape[0]
    row = lax.broadcasted_iota(jnp.int32, a.shape, 0)
    s = 1
    while s < n:
        ok = (row < n - s) if reverse else (row >= s)
        shift = n - s if reverse else s
        b = a * jnp.where(ok, pltpu.roll(b, shift, 0), 0.0) + b
        if 2 * s < n:
            a = a * jnp.where(ok, pltpu.roll(a, shift, 0), 1.0)
        s *= 2
    return b


def _tiled_scan(a, b, reverse, sa_ref, sb_ref, carry_ref, out_ref):
    n = a.shape[0]
    nt8 = n // SUBLANES
    sub = lax.broadcasted_iota(jnp.int32, a.shape, 0) % SUBLANES
    for s in (1, 2, 4):
        ok = (sub < SUBLANES - s) if reverse else (sub >= s)
        shift = n - s if reverse else s
        b = a * jnp.where(ok, pltpu.roll(b, shift, 0), 0.0) + b
        a = a * jnp.where(ok, pltpu.roll(a, shift, 0), 1.0)
    sa_ref[...] = a
    sb_ref[...] = b
    edge = 0 if reverse else SUBLANES - 1
    at = sa_ref[pl.ds(edge, nt8, stride=SUBLANES), :]
    bt = sb_ref[pl.ds(edge, nt8, stride=SUBLANES), :]
    xt = _doubling_scan(at, bt, reverse)
    rowt = lax.broadcasted_iota(jnp.int32, xt.shape, 0)
    if reverse:
        carry_ref[...] = jnp.where(rowt < nt8 - 1, pltpu.roll(xt, nt8 - 1, 0), 0.0)
    else:
        carry_ref[...] = jnp.where(rowt >= 1, pltpu.roll(xt, 1, 0), 0.0)
    for r in range(nt8):
        rows = slice(r * SUBLANES, (r + 1) * SUBLANES)
        out_ref[rows, :] = sa_ref[rows, :] * carry_ref[r:r + 1, :] + sb_ref[rows, :]


def _rglru_gates(xr, wa_ref, wi_ref, rvec_ref):
    xb = xr.astype(BF16)
    r = _sigmoid(_dot(xb, wa_ref[...]) + rvec_ref[1:2, :])
    ig = _sigmoid(_dot(xb, wi_ref[...]) + rvec_ref[2:3, :])
    ls = _log_sigmoid(rvec_ref[3:4, :])
    log_a = RG_C * r * ls
    a = jnp.exp(log_a)
    mult = jnp.sqrt(_neg_expm1(2.0 * log_a))
    return xb, r, ig, ls, a, mult


def _rnn_conv(ux, rw_ref, rvec_ref, ext_ref):
    t = ux.shape[0]
    ext_ref[0:8, :] = jnp.zeros((8, ux.shape[1]), F32)
    ext_ref[8:, :] = ux
    xr = rvec_ref[0:1, :] + rw_ref[RNN_CONV_W - 1:RNN_CONV_W, :] * ux
    for k in range(RNN_CONV_W - 1):
        d = RNN_CONV_W - 1 - k
        xr = xr + rw_ref[k:k + 1, :] * ext_ref[8 - d:8 - d + t, :]
    return xr


def _ffn_fwd(x, vec, wi, wo, tm, name, host=None):
    t, d = x.shape
    nj, fb = wi.shape[1], wi.shape[2]
    nt = t // tm

    def body(ins, outs, scr):
        x_ref, vec_ref, wi_ref, wo_ref = ins
        xo_ref, h_ref, gu_ref, f_ref = outs
        acc_ref, = scr
        j = pl.program_id(1)

        @pl.when(j == 0)
        def _():
            h_ref[...] = _adaln(x_ref[...], vec_ref).astype(BF16)
            acc_ref[...] = jnp.zeros_like(acc_ref)

        h = h_ref[...]
        gate = _dot_nt(h, wi_ref[0])
        up = _dot_nt(h, wi_ref[1])
        gu_ref[0] = gate.astype(BF16)
        gu_ref[1] = up.astype(BF16)
        act = (gate * _sigmoid(gate) * up).astype(BF16)
        acc_ref[...] += _dot(act, wo_ref[...])

        @pl.when(j == nj - 1)
        def _():
            f = acc_ref[...]
            f_ref[...] = f.astype(BF16)
            xo_ref[...] = x_ref[...] + 0.5 * vec_ref[3:4, :] * f

    tile = pl.BlockSpec((tm, d), lambda i, j: (i, 0))
    return _hosted_call(
        body, host, name, (nt, nj),
        [tile,
         pl.BlockSpec((8, d), lambda i, j: (0, 0)),
         pl.BlockSpec((2, None, fb, d), lambda i, j: (0, j, 0, 0)),
         pl.BlockSpec((fb, d), lambda i, j: (j, 0))],
        [tile, tile, pl.BlockSpec((2, None, tm, fb), lambda i, j: (0, j, i, 0)), tile],
        [jax.ShapeDtypeStruct((t, d), F32), jax.ShapeDtypeStruct((t, d), BF16),
         jax.ShapeDtypeStruct((2, nj, t, fb), BF16), jax.ShapeDtypeStruct((t, d), BF16)],
        [pltpu.VMEM((tm, d), F32)], ("arbitrary", "arbitrary"), 48, (x, vec, wi, wo))


def _mix_in(x, vec, win, tm, name, host=None):
    t, d = x.shape
    nb, _, cb = win.shape

    def body(ins, outs, scr):
        x_ref, vec_ref, w_ref = ins
        h_ref, p_ref = outs
        h = _adaln(x_ref[...], vec_ref).astype(BF16)
        h_ref[...] = h
        for k in range(nb):
            p_ref[:, k * cb:(k + 1) * cb] = _dot(h, w_ref[k])

    return _hosted_call(
        body, host, name, (t // tm,),
        [pl.BlockSpec((tm, d), lambda i: (i, 0)),
         pl.BlockSpec((8, d), lambda i: (0, 0)),
         pl.BlockSpec((nb, d, cb), lambda i: (0, 0, 0))],
        [pl.BlockSpec((tm, d), lambda i: (i, 0)),
         pl.BlockSpec((tm, nb * cb), lambda i: (i, 0))],
        [jax.ShapeDtypeStruct((t, d), BF16), jax.ShapeDtypeStruct((t, nb * cb), F32)],
        [], ("arbitrary",), 48, (x, vec, win))


def _conv_fwd(proj, cw32, name, host=None):
    t = proj.shape[0]
    nblk = cw32.shape[1] // LANES
    ch = min(t, 128)

    def body(ins, outs, scr):
        val_ref, gate_ref, cw_ref = ins
        cv_ref, = outs
        ext_ref, = scr
        ext_ref[0:32, :] = jnp.zeros((32, LANES), F32)
        ext_ref[32:, :] = val_ref[...] * _sigmoid(gate_ref[...])
        for r in range(t // ch):
            acc = jnp.broadcast_to(cw_ref[31:32, :], (ch, LANES))
            for k in range(CONV_W):
                off = 32 + r * ch - (CONV_W - 1 - k)
                acc = acc + cw_ref[k:k + 1, :] * ext_ref[off:off + ch, :]
            cv_ref[r * ch:(r + 1) * ch, :] = acc

    return _hosted_call(
        body, host, name, (nblk,),
        [pl.BlockSpec((t, LANES), lambda c: (0, c)),
         pl.BlockSpec((t, LANES), lambda c: (0, nblk + c)),
         pl.BlockSpec((32, LANES), lambda c: (0, c))],
        [pl.BlockSpec((t, LANES), lambda c: (0, c))],
        [jax.ShapeDtypeStruct((t, nblk * LANES), F32)],
        [pltpu.VMEM((t + 32, LANES), F32)], ("arbitrary",), 48, (proj, proj, cw32))


def _rnn_fwd(proj, rw8, rvec, wab, wib, name, host=None):
    t = proj.shape[0]
    nblk = rvec.shape[1] // LANES

    def body(ins, outs, scr):
        ux_ref, uy_ref, rw_ref, rvec_ref, wa_ref, wi_ref = ins
        h_ref, yr_ref = outs
        ext_ref, sa_ref, sb_ref, carry_ref = scr
        xr = _rnn_conv(ux_ref[...], rw_ref, rvec_ref, ext_ref)
        _, _, ig, _, a, mult = _rglru_gates(xr, wa_ref, wi_ref, rvec_ref)
        _tiled_scan(a, mult * (ig * xr), False, sa_ref, sb_ref, carry_ref, h_ref)
        ge, _ = _gelu_and_grad(uy_ref[...])
        yr_ref[...] = (ge * h_ref[...]).astype(BF16)

    blk = lambda off: pl.BlockSpec((t, LANES), lambda c: (0, off + c))
    return _hosted_call(
        body, host, name, (nblk,),
        [blk(2 * nblk), blk(3 * nblk),
         pl.BlockSpec((8, LANES), lambda c: (0, c)),
         pl.BlockSpec((8, LANES), lambda c: (0, c)),
         pl.BlockSpec((None, LANES, LANES), lambda c: (c, 0, 0)),
         pl.BlockSpec((None, LANES, LANES), lambda c: (c, 0, 0))],
        [blk(0), blk(0)],
        [jax.ShapeDtypeStruct((t, nblk * LANES), F32), jax.ShapeDtypeStruct((t, nblk * LANES), BF16)],
        [pltpu.VMEM((t + 8, LANES), F32), pltpu.VMEM((t, LANES), F32), pltpu.VMEM((t, LANES), F32),
         pltpu.VMEM((t // SUBLANES, LANES), F32)], ("arbitrary",), 56, (proj, proj, rw8, rvec, wab, wib))


def _ln_silu(cv, lnv_ref):
    mu = jnp.mean(cv, axis=-1, keepdims=True)
    xc = cv - mu
    rs = lax.rsqrt(jnp.mean(xc * xc, axis=-1, keepdims=True) + EPS)
    chat = xc * rs
    z = chat * lnv_ref[0:1, :] + lnv_ref[1:2, :]
    sg = _sigmoid(z)
    return rs, chat, z, sg


def _mix_out(x, cv, yr, vec, lnv, wout, tm, name, host=None):
    t, d = x.shape
    dc = cv.shape[1]

    def body(ins, outs, scr):
        x_ref, cv_ref, yr_ref, vec_ref, lnv_ref, w_ref = ins
        xo_ref, ym_ref, yc_ref = outs
        _, _, z, sg = _ln_silu(cv_ref[...], lnv_ref)
        yc = (z * sg).astype(BF16)
        yr = yr_ref[...]
        yc_ref[:, 0:dc] = yc
        yc_ref[:, dc:] = yr
        ym = _dot(yc, w_ref[0:dc, :]) + _dot(yr, w_ref[dc:, :])
        ym_ref[...] = ym.astype(BF16)
        xo_ref[...] = x_ref[...] + vec_ref[3:4, :] * ym

    tile = pl.BlockSpec((tm, d), lambda i: (i, 0))
    return _hosted_call(
        body, host, name, (t // tm,),
        [tile,
         pl.BlockSpec((tm, dc), lambda i: (i, 0)),
         pl.BlockSpec((tm, dc), lambda i: (i, 0)),
         pl.BlockSpec((8, d), lambda i: (0, 0)),
         pl.BlockSpec((8, dc), lambda i: (0, 0)),
         pl.BlockSpec((d, d), lambda i: (0, 0))],
        [tile, tile, tile],
        [jax.ShapeDtypeStruct((t, d), F32), jax.ShapeDtypeStruct((t, d), BF16), jax.ShapeDtypeStruct((t, d), BF16)],
        [], ("arbitrary",), 48, (x, cv, yr, vec, lnv, wout))


def _final(x, tgt, vec, f, nvec, tm, name):
    t, d = x.shape
    nt = t // tm

    def body(x_ref, t_ref, vec_ref, f_ref, nvec_ref, dx_ref, dvec_ref, df_ref, dgt_ref):
        i = pl.program_id(0)

        @pl.when(i == 0)
        def _():
            dvec_ref[...] = jnp.zeros_like(dvec_ref)
            dgt_ref[...] = jnp.zeros_like(dgt_ref)

        xv = x_ref[...]
        e = _adaln(xv, vec_ref) - t_ref[...]
        dvec_ref[4:5, :] += (0.5 / d) * jnp.sum(e * e, axis=0, keepdims=True)
        dx = _adaln_bwd(xv, e * (1.0 / d), vec_ref, dvec_ref)
        dx_ref[...] = dx
        _emit_df(dx, f_ref, nvec_ref, df_ref, dgt_ref)

        @pl.when(i == nt - 1)
        def _():
            _adaln_finish(vec_ref, dvec_ref)

    tile = pl.BlockSpec((tm, d), lambda i: (i, 0))
    tab = pl.BlockSpec((8, d), lambda i: (0, 0))
    return pl.pallas_call(
        body, name=name, grid=(nt,),
        in_specs=[tile, tile, tab, tile, tab],
        out_specs=[tile, tab, tile, tab],
        out_shape=[jax.ShapeDtypeStruct((t, d), F32), jax.ShapeDtypeStruct((8, d), F32),
                   jax.ShapeDtypeStruct((t, d), BF16), jax.ShapeDtypeStruct((8, d), F32)],
        compiler_params=_cp(("arbitrary",), 48),
    )(x, tgt, vec, f, nvec)


def _emit_df(dx, f_ref, nvec_ref, df_ref, dgt_ref):
    df_ref[...] = (0.5 * nvec_ref[3:4, :] * dx).astype(BF16)
    dgt_ref[2:3, :] += 0.5 * jnp.sum(dx * f_ref[...].astype(F32), axis=0, keepdims=True)


def _ffn_bwd_w(df, gu, h, wo, tm, name, host=None):
    t, d = df.shape
    nj, fb = gu.shape[1], gu.shape[3]
    nt = t // tm
    sub = min(tm, ROW_GROUP)

    def body(ins, outs, scr):
        df_ref, gu_ref, h_ref, wo_ref = ins
        dgu_ref, dwi_ref, dwo_ref = outs
        accg_ref, accu_ref, acco_ref, dact_ref, act_ref = scr
        i = pl.program_id(1)

        @pl.when(i == 0)
        def _():
            accg_ref[...] = jnp.zeros_like(accg_ref)
            accu_ref[...] = jnp.zeros_like(accu_ref)
            acco_ref[...] = jnp.zeros_like(acco_ref)

        dact_ref[...] = _dot_nt(df_ref[...], wo_ref[...])
        for r in range(tm // sub):
            rows = slice(r * sub, (r + 1) * sub)
            g = gu_ref[0, rows, :].astype(F32)
            u = gu_ref[1, rows, :].astype(F32)
            dact = dact_ref[rows, :]
            sg = _sigmoid(g)
            sl = g * sg
            dgu_ref[0, rows, :] = (dact * u * (sg * (1.0 + g * (1.0 - sg)))).astype(BF16)
            dgu_ref[1, rows, :] = (dact * sl).astype(BF16)
            act_ref[rows, :] = (sl * u).astype(BF16)
        hb = h_ref[...]
        acco_ref[...] += _dot_tn(act_ref[...], df_ref[...])
        accg_ref[...] += _dot_tn(dgu_ref[0], hb)
        accu_ref[...] += _dot_tn(dgu_ref[1], hb)

        @pl.when(i == nt - 1)
        def _():
            dwi_ref[0] = accg_ref[...].astype(BF16)
            dwi_ref[1] = accu_ref[...].astype(BF16)
            dwo_ref[...] = acco_ref[...].astype(BF16)

    tile = pl.BlockSpec((tm, d), lambda j, i: (i, 0))
    return _hosted_call(
        body, host, name, (nj, nt),
        [tile,
         pl.BlockSpec((2, None, tm, fb), lambda j, i: (0, j, i, 0)),
         tile,
         pl.BlockSpec((fb, d), lambda j, i: (j, 0))],
        [pl.BlockSpec((2, None, tm, fb), lambda j, i: (0, j, i, 0)),
         pl.BlockSpec((2, None, fb, d), lambda j, i: (0, j, 0, 0)),
         pl.BlockSpec((None, fb, d), lambda j, i: (j, 0, 0))],
        [jax.ShapeDtypeStruct((2, nj, t, fb), BF16), jax.ShapeDtypeStruct((2, nj, fb, d), BF16),
         jax.ShapeDtypeStruct((nj, fb, d), BF16)],
        [pltpu.VMEM((fb, d), F32), pltpu.VMEM((fb, d), F32), pltpu.VMEM((fb, d), F32),
         pltpu.VMEM((tm, fb), F32), pltpu.VMEM((tm, fb), BF16)],
        ("arbitrary", "arbitrary"), 56, (df, gu, h, wo))


def _ffn_bwd_in(dxo, x, vec, dgu, wi, tm, name, host=None):
    t, d = x.shape
    nj, fb = wi.shape[1], wi.shape[2]
    nt = t // tm

    def body(ins, outs, scr):
        dxo_ref, x_ref, vec_ref, dgu_ref, wi_ref = ins
        dx_ref, dvec_ref = outs
        i = pl.program_id(0)

        @pl.when(i == 0)
        def _():
            dvec_ref[...] = jnp.zeros_like(dvec_ref)

        dh = jnp.zeros((tm, d), F32)
        for a in range(2):
            for k in range(nj):
                dh = dh + _dot(dgu_ref[a, k], wi_ref[a, k])
        dx_ref[...] = dxo_ref[...] + _adaln_bwd(x_ref[...], dh, vec_ref, dvec_ref)

        @pl.when(i == nt - 1)
        def _():
            _adaln_finish(vec_ref, dvec_ref)

    tile = pl.BlockSpec((tm, d), lambda i: (i, 0))
    return _hosted_call(
        body, host, name, (nt,),
        [tile, tile,
         pl.BlockSpec((8, d), lambda i: (0, 0)),
         pl.BlockSpec((2, nj, tm, fb), lambda i: (0, 0, i, 0)),
         pl.BlockSpec((2, nj, fb, d), lambda i: (0, 0, 0, 0))],
        [tile, pl.BlockSpec((8, d), lambda i: (0, 0))],
        [jax.ShapeDtypeStruct((t, d), F32), jax.ShapeDtypeStruct((8, d), F32)],
        [], ("arbitrary",), 60, (dxo, x, vec, dgu, wi))


def _mm_tn(a, b, a_spec, b_spec, nblk, nk, m, n, name):
    def body(a_ref, b_ref, o_ref, acc_ref):
        s = pl.program_id(1)

        @pl.when(s == 0)
        def _():
            acc_ref[...] = jnp.zeros_like(acc_ref)

        acc_ref[...] += _dot_tn(a_ref[...], b_ref[...])

        @pl.when(s == nk - 1)
        def _():
            o_ref[...] = acc_ref[...].astype(BF16)

    return pl.pallas_call(
        body, name=name, grid=(nblk, nk),
        in_specs=[a_spec, b_spec],
        out_specs=pl.BlockSpec((None, m, n), lambda k, s: (k, 0, 0)),
        out_shape=jax.ShapeDtypeStruct((nblk, m, n), BF16),
        scratch_shapes=[pltpu.VMEM((m, n), F32)],
        compiler_params=_cp(("arbitrary", "arbitrary"), 56),
    )(a, b)


def _wgrad_in(h, parts, cb, tk, name):
    t, d = h.shape
    dc = parts[0].shape[1]
    per = dc // cb
    nblk = len(parts) * per
    nk = t // tk

    def body(h_ref, p0, p1, p2, p3, o_ref, acc_ref):
        s = pl.program_id(0)

        @pl.when(s == 0)
        def _():
            acc_ref[...] = jnp.zeros_like(acc_ref)

        hb = h_ref[...]
        for p, p_ref in enumerate((p0, p1, p2, p3)):
            acc_ref[p] += _dot_tn(hb, p_ref[...])

        @pl.when(s == nk - 1)
        def _():
            for k in range(nblk):
                o_ref[k] = acc_ref[k // per, :, (k % per) * cb:(k % per + 1) * cb].astype(BF16)

    return pl.pallas_call(
        body, name=name, grid=(nk,),
        in_specs=[pl.BlockSpec((tk, d), lambda s: (s, 0))] + [pl.BlockSpec((tk, dc), lambda s: (s, 0))] * len(parts),
        out_specs=pl.BlockSpec((nblk, d, cb), lambda s: (0, 0, 0)),
        out_shape=jax.ShapeDtypeStruct((nblk, d, cb), BF16),
        scratch_shapes=[pltpu.VMEM((len(parts), d, dc), F32)],
        compiler_params=_cp(("arbitrary",), 56),
    )(h, *parts)


def _mixout_bwd(dxo, ym, cv, hr, proj, vec, lnv, wout, tm, name, host=None):
    t, d = dxo.shape
    dc = cv.shape[1]
    nt = t // tm

    def body(ins, outs, scr):
        dxo_ref, ym_ref, cv_ref, hr_ref, uy_ref, vec_ref, lnv_ref, w_ref = ins
        dym_ref, dcv_ref, dhr_ref, duy_ref, dln_ref, dgt_ref = outs
        i = pl.program_id(0)

        @pl.when(i == 0)
        def _():
            dln_ref[...] = jnp.zeros_like(dln_ref)
            dgt_ref[...] = jnp.zeros_like(dgt_ref)

        dxo_v = dxo_ref[...]
        dym = (vec_ref[3:4, :] * dxo_v).astype(BF16)
        dym_ref[...] = dym
        dgt_ref[0:1, :] += jnp.sum(dxo_v * ym_ref[...].astype(F32), axis=0, keepdims=True)
        dyc = _dot_nt(dym, w_ref[0:dc, :])
        dyr = _dot_nt(dym, w_ref[dc:, :])
        rs, chat, z, sg = _ln_silu(cv_ref[...], lnv_ref)
        dz = dyc * (sg * (1.0 + z * (1.0 - sg)))
        dln_ref[0:1, :] += jnp.sum(dz * chat, axis=0, keepdims=True)
        dln_ref[1:2, :] += jnp.sum(dz, axis=0, keepdims=True)
        dchat = dz * lnv_ref[0:1, :]
        dcv_ref[...] = rs * (dchat - jnp.mean(dchat, axis=-1, keepdims=True)
                             - chat * jnp.mean(dchat * chat, axis=-1, keepdims=True))
        ge, dge = _gelu_and_grad(uy_ref[...])
        dhr_ref[...] = dyr * ge
        duy_ref[...] = (dyr * hr_ref[...] * dge).astype(BF16)

    tile_d = pl.BlockSpec((tm, d), lambda i: (i, 0))
    tile_c = pl.BlockSpec((tm, dc), lambda i: (i, 0))
    return _hosted_call(
        body, host, name, (nt,),
        [tile_d, tile_d, tile_c, tile_c,
         pl.BlockSpec((tm, dc), lambda i: (i, 3)),
         pl.BlockSpec((8, d), lambda i: (0, 0)),
         pl.BlockSpec((8, dc), lambda i: (0, 0)),
         pl.BlockSpec((d, d), lambda i: (0, 0))],
        [tile_d, tile_c, tile_c, tile_c,
         pl.BlockSpec((8, dc), lambda i: (0, 0)),
         pl.BlockSpec((8, d), lambda i: (0, 0))],
        [jax.ShapeDtypeStruct((t, d), BF16), jax.ShapeDtypeStruct((t, dc), F32),
         jax.ShapeDtypeStruct((t, dc), F32), jax.ShapeDtypeStruct((t, dc), BF16),
         jax.ShapeDtypeStruct((8, dc), F32), jax.ShapeDtypeStruct((8, d), F32)],
        [], ("arbitrary",), 48, (dxo, ym, cv, hr, proj, vec, lnv, wout))


def _conv_bwd(proj, dcv, cw32, name):
    t = proj.shape[0]
    nblk = cw32.shape[1] // LANES
    ch = min(t, 128)

    def body(val_ref, gate_ref, dcv_ref, cw_ref, dval_ref, dgate_ref, dcw_ref, extu_ref, extd_ref):
        val = val_ref[...]
        sg = _sigmoid(gate_ref[...])
        extu_ref[0:32, :] = jnp.zeros((32, LANES), F32)
        extu_ref[32:, :] = val * sg
        dcv_v = dcv_ref[...]
        extd_ref[0:t, :] = dcv_v
        extd_ref[t:, :] = jnp.zeros((32, LANES), F32)
        for r in range(t // ch):
            acc = jnp.zeros((ch, LANES), F32)
            for k in range(CONV_W):
                off = r * ch + (CONV_W - 1 - k)
                acc = acc + cw_ref[k:k + 1, :] * extd_ref[off:off + ch, :]
            rows = slice(r * ch, (r + 1) * ch)
            sg_r = _sigmoid(gate_ref[rows, :])
            dval_ref[rows, :] = (acc * sg_r).astype(BF16)
            dgate_ref[rows, :] = (acc * val_ref[rows, :] * sg_r * (1.0 - sg_r)).astype(BF16)
        for k in range(CONV_W):
            off = 32 - (CONV_W - 1 - k)
            dcw_ref[k:k + 1, :] = jnp.sum(dcv_v * extu_ref[off:off + t, :], axis=0, keepdims=True)
        dcw_ref[31:32, :] = jnp.sum(dcv_v, axis=0, keepdims=True)

    blk = lambda off: pl.BlockSpec((t, LANES), lambda c: (0, off + c))
    return pl.pallas_call(
        body, name=name, grid=(nblk,),
        in_specs=[blk(0), blk(nblk), blk(0), pl.BlockSpec((32, LANES), lambda c: (0, c))],
        out_specs=[blk(0), blk(0), pl.BlockSpec((32, LANES), lambda c: (0, c))],
        out_shape=[jax.ShapeDtypeStruct((t, nblk * LANES), BF16), jax.ShapeDtypeStruct((t, nblk * LANES), BF16),
                   jax.ShapeDtypeStruct((32, nblk * LANES), F32)],
        scratch_shapes=[pltpu.VMEM((t + 32, LANES), F32), pltpu.VMEM((t + 32, LANES), F32)],
        compiler_params=_cp(("arbitrary",), 56),
    )(proj, proj, dcv, cw32)


def _rnn_bwd(proj, hr, dhr, rw8, rvec, wab, wib, name, host=None):
    t = proj.shape[0]
    nblk = rvec.shape[1] // LANES

    def body(ins, outs, scr):
        ux_ref, h_ref, dh_ref, rw_ref, rvec_ref, wa_ref, wi_ref = ins
        dux_ref, sm_ref, dwa_ref, dwi_ref = outs
        ext_ref, extd_ref, sa_ref, sb_ref, carry_ref = scr
        xr = _rnn_conv(ux_ref[...], rw_ref, rvec_ref, ext_ref)
        xb, r, ig, ls, a, mult = _rglru_gates(xr, wa_ref, wi_ref, rvec_ref)
        row = lax.broadcasted_iota(jnp.int32, (t, LANES), 0)
        a_next = jnp.where(row < t - 1, pltpu.roll(a, t - 1, 0), 0.0)
        _tiled_scan(a_next, dh_ref[...], True, sa_ref, sb_ref, carry_ref, extd_ref)
        g = extd_ref[0:t, :]
        hprev = jnp.where(row >= 1, pltpu.roll(h_ref[...], 1, 0), 0.0)
        da = g * hprev
        dmult = g * (ig * xr)
        dig = g * mult * xr
        dxr = g * mult * ig
        dlog_a = a * (da - dmult * a / mult)
        dr = dlog_a * (RG_C * ls)
        dls = RG_C * jnp.sum(dlog_a * r, axis=0, keepdims=True)
        dpr = dr * r * (1.0 - r)
        dpi = dig * ig * (1.0 - ig)
        dprb = dpr.astype(BF16)
        dpib = dpi.astype(BF16)
        dxr = dxr + _dot_nt(dprb, wa_ref[...]) + _dot_nt(dpib, wi_ref[...])
        dwa_ref[...] = _dot_tn(xb, dprb)
        dwi_ref[...] = _dot_tn(xb, dpib)
        extd_ref[0:t, :] = dxr
        extd_ref[t:, :] = jnp.zeros((8, LANES), F32)
        dux = rw_ref[RNN_CONV_W - 1:RNN_CONV_W, :] * dxr
        for k in range(RNN_CONV_W - 1):
            d = RNN_CONV_W - 1 - k
            dux = dux + rw_ref[k:k + 1, :] * extd_ref[d:d + t, :]
        dux_ref[...] = dux.astype(BF16)
        for k in range(RNN_CONV_W):
            d = RNN_CONV_W - 1 - k
            sm_ref[k:k + 1, :] = jnp.sum(dxr * ext_ref[8 - d:8 - d + t, :], axis=0, keepdims=True)
        sm_ref[4:5, :] = jnp.sum(dxr, axis=0, keepdims=True)
        sm_ref[5:6, :] = jnp.sum(dpr, axis=0, keepdims=True)
        sm_ref[6:7, :] = jnp.sum(dpi, axis=0, keepdims=True)
        sm_ref[7:8, :] = dls * _sigmoid(-rvec_ref[3:4, :])

    blk = lambda off: pl.BlockSpec((t, LANES), lambda c: (0, off + c))
    sq = pl.BlockSpec((None, LANES, LANES), lambda c: (c, 0, 0))
    return _hosted_call(
        body, host, name, (nblk,),
        [blk(2 * nblk), blk(0), blk(0),
         pl.BlockSpec((8, LANES), lambda c: (0, c)),
         pl.BlockSpec((8, LANES), lambda c: (0, c)), sq, sq],
        [blk(0), pl.BlockSpec((8, LANES), lambda c: (0, c)), sq, sq],
        [jax.ShapeDtypeStruct((t, nblk * LANES), BF16), jax.ShapeDtypeStruct((8, nblk * LANES), F32),
         jax.ShapeDtypeStruct((nblk, LANES, LANES), F32), jax.ShapeDtypeStruct((nblk, LANES, LANES), F32)],
        [pltpu.VMEM((t + 8, LANES), F32), pltpu.VMEM((t + 8, LANES), F32), pltpu.VMEM((t, LANES), F32),
         pltpu.VMEM((t, LANES), F32), pltpu.VMEM((t // SUBLANES, LANES), F32)],
        ("arbitrary",), 60, (proj, hr, dhr, rw8, rvec, wab, wib))


def _mixin_bwd(dxo, x, parts, vec, win, f, nvec, tm, name):
    t, d = x.shape
    nb, _, cb = win.shape
    dc = parts[0].shape[1]
    per = dc // cb
    nt = t // tm

    def body(dxo_ref, x_ref, p0, p1, p2, p3, vec_ref, w_ref, f_ref, nvec_ref, dx_ref, dvec_ref, df_ref, dgt_ref):
        i = pl.program_id(0)

        @pl.when(i == 0)
        def _():
            dvec_ref[...] = jnp.zeros_like(dvec_ref)
            dgt_ref[...] = jnp.zeros_like(dgt_ref)

        prefs = (p0, p1, p2, p3)
        dh = jnp.zeros((tm, d), F32)
        for k in range(nb):
            dh = dh + _dot_nt(prefs[k // per][:, (k % per) * cb:(k % per + 1) * cb], w_ref[k])
        dx = dxo_ref[...] + _adaln_bwd(x_ref[...], dh, vec_ref, dvec_ref)
        dx_ref[...] = dx
        _emit_df(dx, f_ref, nvec_ref, df_ref, dgt_ref)

        @pl.when(i == nt - 1)
        def _():
            _adaln_finish(vec_ref, dvec_ref)

    tile_d = pl.BlockSpec((tm, d), lambda i: (i, 0))
    tile_c = pl.BlockSpec((tm, dc), lambda i: (i, 0))
    tab = pl.BlockSpec((8, d), lambda i: (0, 0))
    return pl.pallas_call(
        body, name=name, grid=(nt,),
        in_specs=[tile_d, tile_d, tile_c, tile_c, tile_c, tile_c, tab,
                  pl.BlockSpec((nb, d, cb), lambda i: (0, 0, 0)), tile_d, tab],
        out_specs=[tile_d, tab, tile_d, tab],
        out_shape=[jax.ShapeDtypeStruct((t, d), F32), jax.ShapeDtypeStruct((8, d), F32),
                   jax.ShapeDtypeStruct((t, d), BF16), jax.ShapeDtypeStruct((8, d), F32)],
        compiler_params=_cp(("arbitrary",), 48),
    )(dxo, x, *parts, vec, win, f, nvec)


def _coords():
    return lax.axis_index("x"), lax.axis_index("y"), lax.axis_index("c")


def _flip(v, bit):
    return 1 - v if bit else v


def _gather_copy(outs, send_sems, recv_sems, a, k, block, to, src=None):
    dst = outs[a].at[block]
    return pltpu.make_async_remote_copy(
        src_ref=dst if src is None else src, dst_ref=dst,
        send_sem=send_sems.at[a, k], recv_sem=recv_sems.at[a, k],
        device_id=to, device_id_type=MESH_IDS)


def _gather_start(ins, outs, send_sems, recv_sems, loc_sems):
    x, y, c = _coords()
    me = 4 * x + 2 * y + c
    for a in range(len(ins)):
        pltpu.make_async_copy(ins[a], outs[a].at[me], loc_sems.at[a]).start()
    for a in range(len(ins)):
        _gather_copy(outs, send_sems, recv_sems, a, 0, me, (x, y, 1 - c), src=ins[a]).start()
        for j, (cx, cy) in enumerate([(1 - x, y), (x, 1 - y), (1 - x, 1 - y)]):
            _gather_copy(outs, send_sems, recv_sems, a, 1 + j, me, (cx, cy, c), src=ins[a]).start()


def _gather_finish(ins, outs, send_sems, recv_sems, loc_sems):
    x, y, c = _coords()
    me = 4 * x + 2 * y + c
    sib = (x, y, 1 - c)
    chips = [(1 - x, y), (x, 1 - y), (1 - x, 1 - y)]
    n = len(ins)
    for a in range(n):
        for j, (cx, cy) in enumerate(chips):
            blk = 4 * cx + 2 * cy + c
            _gather_copy(outs, send_sems, recv_sems, a, 1 + j, blk, sib).wait_recv()
            _gather_copy(outs, send_sems, recv_sems, a, 4 + j, blk, sib).start()
    for a in range(n):
        _gather_copy(outs, send_sems, recv_sems, a, 0, 4 * x + 2 * y + (1 - c), sib).wait_recv()
        for j, (cx, cy) in enumerate(chips):
            _gather_copy(outs, send_sems, recv_sems, a, 4 + j, 4 * cx + 2 * cy + (1 - c), sib).wait_recv()
    for a in range(n):
        _gather_copy(outs, send_sems, recv_sems, a, 0, me, sib, src=ins[a]).wait_send()
        for j, (cx, cy) in enumerate(chips):
            _gather_copy(outs, send_sems, recv_sems, a, 1 + j, me, (cx, cy, c), src=ins[a]).wait_send()
            _gather_copy(outs, send_sems, recv_sems, a, 4 + j, 4 * cx + 2 * cy + c, sib).wait_send()
        pltpu.make_async_copy(ins[a], outs[a].at[me], loc_sems.at[a]).wait()


def _gather_shapes(shards):
    return [jax.ShapeDtypeStruct((NDEV,) + s.shape, s.dtype) for s in shards]


def _gather_sems(n):
    return [pltpu.SemaphoreType.DMA((n, 7)), pltpu.SemaphoreType.DMA((n, 7)), pltpu.SemaphoreType.DMA((n,))]


def _sibling_copies(ins, outs, send_sems, recv_sems):
    x, y, c = _coords()
    return [pltpu.make_async_remote_copy(
        src_ref=ins[a].at[2 * q + (1 - c)], dst_ref=outs[a].at[q],
        send_sem=send_sems.at[a, q], recv_sem=recv_sems.at[a, q],
        device_id=(x, y, 1 - c), device_id_type=MESH_IDS) for a in range(len(ins)) for q in range(4)]


def _sibling_shapes(parts):
    return [jax.ShapeDtypeStruct((4,) + p.shape[1:], p.dtype) for p in parts]


def _chips_copies(ins, outs, send_sems, recv_sems):
    x, y, c = _coords()
    copies = []
    for a in range(len(ins)):
        for k, (kx, ky) in enumerate([(1, 0), (0, 1), (1, 1)]):
            tx, ty = _flip(x, kx), _flip(y, ky)
            copies.append(pltpu.make_async_remote_copy(
                src_ref=ins[a].at[2 * tx + ty], dst_ref=outs[a].at[k],
                send_sem=send_sems.at[a, k], recv_sem=recv_sems.at[a, k],
                device_id=(tx, ty, c), device_id_type=MESH_IDS))
    return copies


def _chips_shapes(sums):
    return [jax.ShapeDtypeStruct((3,) + s.shape[1:], s.dtype) for s in sums]


def _direct_copies(ins, outs, send_sems, recv_sems):
    x, y, c = _coords()
    me = 4 * x + 2 * y + c
    copies = []
    for a in range(len(ins)):
        for k in range(1, NDEV):
            kx, ky, kc = (k >> 2) & 1, (k >> 1) & 1, k & 1
            copies.append(pltpu.make_async_remote_copy(
                src_ref=ins[a], dst_ref=outs[a].at[me],
                send_sem=send_sems.at[a, k - 1], recv_sem=recv_sems.at[a, k - 1],
                device_id=(_flip(x, kx), _flip(y, ky), _flip(c, kc)), device_id_type=MESH_IDS))
    return copies


class _Exchange:
    def __init__(self, kind, arrays):
        self.kind, self.arrays, self.n = kind, list(arrays), len(arrays)
        self.nsem = 3 if kind in ("gather", "direct") else 2

    def out_shapes(self):
        return {"gather": _gather_shapes, "direct": _gather_shapes, "sibling": _sibling_shapes,
                "chips": _chips_shapes}[self.kind](self.arrays)

    def sems(self):
        if self.kind in ("gather", "direct"):
            return _gather_sems(self.n)
        k = {"sibling": 4, "chips": 3}[self.kind]
        return [pltpu.SemaphoreType.DMA((self.n, k)), pltpu.SemaphoreType.DMA((self.n, k))]

    def _copies(self, ins, outs, sems):
        if self.kind == "direct":
            x, y, c = _coords()
            own = [pltpu.make_async_copy(ins[a], outs[a].at[4 * x + 2 * y + c], sems[2].at[a]) for a in range(self.n)]
            return own + _direct_copies(ins, outs, sems[0], sems[1])
        return {"sibling": _sibling_copies, "chips": _chips_copies}[self.kind](ins, outs, *sems)

    def start(self, ins, outs, sems):
        if self.kind == "gather":
            _gather_start(ins, outs, *sems)
        else:
            for cpy in self._copies(ins, outs, sems):
                cpy.start()

    def finish(self, ins, outs, sems):
        if self.kind == "gather":
            _gather_finish(ins, outs, *sems)
        else:
            for cpy in self._copies(ins, outs, sems):
                cpy.wait()


class _Exchanges:
    def __init__(self, *parts):
        self.parts = parts
        self.arrays = [a for p in parts for a in p.arrays]
        self.n = len(self.arrays)

    def out_shapes(self):
        return [s for p in self.parts for s in p.out_shapes()]

    def sems(self):
        return [s for p in self.parts for s in p.sems()]

    def _each(self, ins, outs, sems):
        a = s = 0
        for p in self.parts:
            yield p, ins[a:a + p.n], outs[a:a + p.n], sems[s:s + p.nsem]
            a, s = a + p.n, s + p.nsem

    def start(self, ins, outs, sems):
        for p, i, o, s in self._each(ins, outs, sems):
            p.start(i, o, s)

    def finish(self, ins, outs, sems):
        for p, i, o, s in self._each(ins, outs, sems):
            p.finish(i, o, s)


def _hosted_call(body, host, name, grid, in_specs, out_specs, out_shape, scratch, sem, vmem_mb, args):
    n = host.n if host else 0
    ni, no, ns = len(in_specs), len(out_specs), len(scratch)

    def full(*refs):
        ins, h_in = refs[:ni], refs[ni:ni + n]
        outs, h_out = refs[ni + n:ni + n + no], refs[ni + n + no:ni + 2 * n + no]
        scr, sems = refs[ni + 2 * n + no:ni + 2 * n + no + ns], refs[ni + 2 * n + no + ns:]
        if host and grid:
            first = functools.reduce(lambda a, b: a & b, [pl.program_id(k) == 0 for k in range(len(grid))])
            last = functools.reduce(lambda a, b: a & b, [pl.program_id(k) == g - 1 for k, g in enumerate(grid)])

            @pl.when(first)
            def _():
                host.start(h_in, h_out, sems)
        elif host:
            host.start(h_in, h_out, sems)

        body(ins, outs, scr)

        if host and grid:
            @pl.when(last)
            def _():
                host.finish(h_in, h_out, sems)
        elif host:
            host.finish(h_in, h_out, sems)

    anyspec = pl.BlockSpec(memory_space=pl.ANY)
    return pl.pallas_call(
        full, name=name, grid=grid,
        in_specs=list(in_specs) + [anyspec] * n, out_specs=list(out_specs) + [anyspec] * n,
        out_shape=list(out_shape) + (host.out_shapes() if host else []),
        scratch_shapes=list(scratch) + (host.sems() if host else []),
        compiler_params=_cp(sem, vmem_mb),
    )(*args, *(host.arrays if host else []))


def _exchange(host, name, after=()):
    n, na = host.n, len(after)

    def body(*refs):
        ins, outs, sems = refs[:n], refs[n + na:2 * n + na], refs[2 * n + na:]
        host.start(ins, outs, sems)
        host.finish(ins, outs, sems)

    anyspec = pl.BlockSpec(memory_space=pl.ANY)
    return pl.pallas_call(
        body, name=name, in_specs=[anyspec] * (n + na), out_specs=[anyspec] * n,
        out_shape=host.out_shapes(), scratch_shapes=host.sems(),
    )(*host.arrays, *after)


def _chips_split_start(sums, name):
    n = len(sums)
    hbm = pl.BlockSpec(memory_space=pltpu.HBM)
    sem = pl.BlockSpec(memory_space=pltpu.SEMAPHORE)

    def body(*refs):
        ins, lands = refs[:n], refs[n:2 * n]
        sems = refs[2 * n:2 * n + 6 * n]
        token = refs[-1]
        for cpy in _chips_copies(ins, lands, _SemGrid(sems[:3 * n], 3), _SemGrid(sems[3 * n:], 3)):
            cpy.start()
        token[...] = jnp.zeros_like(token)

    land_shapes = _chips_shapes(sums)
    lands = [pltpu.with_memory_space_constraint(lax.empty(s.shape, s.dtype), pltpu.HBM) for s in land_shapes]
    return pl.pallas_call(
        body, name=name,
        out_shape=(*[pltpu.SemaphoreType.DMA(())] * (6 * n),
                   *[pltpu.HBM(s.shape, s.dtype) for s in sums],
                   *[pltpu.HBM(s.shape, s.dtype) for s in land_shapes],
                   jax.ShapeDtypeStruct((8, LANES), F32)),
        in_specs=[hbm] * (2 * n),
        out_specs=(*[sem] * (6 * n), *[hbm] * (2 * n), pl.BlockSpec(memory_space=pltpu.VMEM)),
        input_output_aliases={i: 6 * n + i for i in range(2 * n)},
        compiler_params=pltpu.CompilerParams(has_side_effects=pltpu.SideEffectType.DATAFLOW_SIDE_EFFECTING),
    )(*[pltpu.with_memory_space_constraint(s, pltpu.HBM) for s in sums], *lands)


class _SemGrid:
    def __init__(self, sems, k):
        self.sems, self.k = sems, k

    @property
    def at(self):
        return self

    def __getitem__(self, idx):
        return self.sems[idx[0] * self.k + idx[1]]


def _chips_split_wait(started, n, after, name):
    sems = started[:6 * n]
    thru = started[6 * n:8 * n]
    hbm = pl.BlockSpec(memory_space=pltpu.HBM)
    sem = pl.BlockSpec(memory_space=pltpu.SEMAPHORE)

    def body(*refs):
        ins, lands = refs[:n], refs[n:2 * n]
        s = refs[2 * n:2 * n + 6 * n]
        for cpy in _chips_copies(ins, lands, _SemGrid(s[:3 * n], 3), _SemGrid(s[3 * n:], 3)):
            cpy.wait_send()
            cpy.wait_recv()

    outs = pl.pallas_call(
        body, name=name,
        out_shape=tuple(pltpu.HBM(a.shape, a.dtype) for a in thru),
        in_specs=[hbm] * (2 * n) + [sem] * (6 * n) + [pl.BlockSpec(memory_space=pl.ANY)],
        out_specs=tuple([hbm] * (2 * n)),
        input_output_aliases={i: i for i in range(2 * n)},
        compiler_params=pltpu.CompilerParams(has_side_effects=pltpu.SideEffectType.DATAFLOW_SIDE_EFFECTING),
    )(*thru, *sems, after)
    return list(outs[n:])


def _chip_sum(part, recv, sel, tr, name):
    _, _, r, c = part.shape

    def body(sel_ref, p_ref, r_ref, cs_ref, own_ref):
        q = pl.program_id(1)
        s = p_ref[...].astype(F32) + r_ref[...].astype(F32)
        cs_ref[...] = s.astype(BF16)

        @pl.when(q == sel_ref[1])
        def _():
            own_ref[...] = s

    return pl.pallas_call(
        body, name=name,
        grid_spec=pltpu.PrefetchScalarGridSpec(
            num_scalar_prefetch=1, grid=(r // tr, 4),
            in_specs=[pl.BlockSpec((None, None, tr, c), lambda i, q, s: (q, s[0], i, 0)),
                      pl.BlockSpec((None, tr, c), lambda i, q, s: (q, i, 0))],
            out_specs=[pl.BlockSpec((None, tr, c), lambda i, q, s: (q, i, 0)),
                       pl.BlockSpec((tr, c), lambda i, q, s: (i, 0))]),
        out_shape=[jax.ShapeDtypeStruct((4, r, c), BF16), jax.ShapeDtypeStruct((r, c), F32)],
        compiler_params=_cp(("arbitrary", "arbitrary"), 48),
    )(sel, part, recv)


def _gather_direct(src_ref, buf_ref, send_sems, recv_sems):
    x, y, c = _coords()
    me = 4 * x + 2 * y + c
    buf_ref[me] = src_ref[...]
    copies = []
    for k in range(1, NDEV):
        kx, ky, kc = (k >> 2) & 1, (k >> 1) & 1, k & 1
        copies.append(pltpu.make_async_remote_copy(
            src_ref=src_ref, dst_ref=buf_ref.at[me],
            send_sem=send_sems.at[k - 1], recv_sem=recv_sems.at[k - 1],
            device_id=(_flip(x, kx), _flip(y, ky), _flip(c, kc)), device_id_type=MESH_IDS))
    for cpy in copies:
        cpy.start()
    for k in range(1, NDEV):
        kx, ky, kc = (k >> 2) & 1, (k >> 1) & 1, k & 1
        peer = 4 * _flip(x, kx) + 2 * _flip(y, ky) + _flip(c, kc)
        pltpu.make_async_remote_copy(
            src_ref=src_ref, dst_ref=buf_ref.at[peer],
            send_sem=send_sems.at[k - 1], recv_sem=recv_sems.at[k - 1],
            device_id=(x, y, c), device_id_type=MESH_IDS).wait_recv()
    for cpy in copies:
        cpy.wait_send()
    return me


def _mod_exchange(c_row, wmod, bmod, wfmod, bfmod, name, host=None):
    d = c_row.shape[1]
    nm, nf = wmod.shape[1], wfmod.shape[1]
    nw = nm + nf

    def body(ins, outs, scr):
        c_ref, wm_ref, bm_ref, wf_ref, bf_ref = ins
        cs_ref, mod_ref, fmod_ref = outs
        slab_ref, csbuf_ref, mslab_ref, mbuf_ref, s1, r1, s2, r2 = scr
        cv = c_ref[...]
        slab_ref[...] = jnp.broadcast_to(cv * _sigmoid(cv), (8, d))
        _gather_direct(slab_ref, csbuf_ref, s1, r1)
        for b in range(NDEV):
            cs_ref[b:b + 1, :] = csbuf_ref[b, 0:1, :]
        cs = cs_ref[...]
        mslab_ref[:, 0:nm] = jnp.dot(cs, wm_ref[...], precision=HI, preferred_element_type=F32) + bm_ref[...]
        mslab_ref[:, nm:] = jnp.dot(cs, wf_ref[...], precision=HI, preferred_element_type=F32) + bf_ref[...]
        me = _gather_direct(mslab_ref, mbuf_ref, s2, r2)
        mine = lax.broadcasted_iota(jnp.int32, (8, nw), 0) == me
        for k in range(NDEV):
            rowk = jnp.sum(jnp.where(mine, mbuf_ref[k], 0.0), axis=0, keepdims=True)
            mod_ref[k:k + 1, :] = rowk[:, 0:nm]
            fmod_ref[k:k + 1, :] = rowk[:, nm:]

    vm = pl.BlockSpec(memory_space=pltpu.VMEM)
    return _hosted_call(
        body, host, name, (), [vm] * 5, [vm] * 3,
        [jax.ShapeDtypeStruct((NDEV, d), F32), jax.ShapeDtypeStruct((NDEV, nm), F32),
         jax.ShapeDtypeStruct((NDEV, nf), F32)],
        [pltpu.VMEM((8, d), F32), pltpu.VMEM((NDEV, 8, d), F32),
         pltpu.VMEM((8, nw), F32), pltpu.VMEM((NDEV, 8, nw), F32),
         pltpu.SemaphoreType.DMA((7,)), pltpu.SemaphoreType.DMA((7,)),
         pltpu.SemaphoreType.DMA((7,)), pltpu.SemaphoreType.DMA((7,))],
        None, 40, (c_row, wmod, bmod, wfmod, bfmod))


def _table_sum(tabs, name):
    n = len(tabs)

    def body(*refs):
        for a in range(n):
            tot = refs[a][0]
            for k in range(1, NDEV):
                tot = tot + refs[a][k]
            refs[n + a][...] = tot

    vm = pl.BlockSpec(memory_space=pltpu.VMEM)
    return pl.pallas_call(
        body, name=name, in_specs=[vm] * n, out_specs=[vm] * n,
        out_shape=[jax.ShapeDtypeStruct(tb.shape[1:], F32) for tb in tabs],
    )(*tabs)


def _adamw_math(w, g, m, v):
    m = ADAM_B1 * m + (1.0 - ADAM_B1) * g
    v = ADAM_B2 * v + (1.0 - ADAM_B2) * (g * g)
    m_hat = m / (1.0 - ADAM_B1 ** ADAM_STEP)
    v_hat = v / (1.0 - ADAM_B2 ** ADAM_STEP)
    delta = -ADAM_LR * (m_hat / (jnp.sqrt(v_hat) + ADAM_EPS) + ADAM_WD * w)
    return delta, m, v


def _adamw_small(params, name):
    n = len(params)

    def body(*refs):
        for p in range(n):
            w_ref, g_ref, m_ref, v_ref = refs[4 * p:4 * p + 4]
            d_ref, mo_ref, vo_ref = refs[4 * n + 3 * p:4 * n + 3 * p + 3]
            d_ref[...], mo_ref[...], vo_ref[...] = _adamw_math(w_ref[...], g_ref[...], m_ref[...], v_ref[...])

    vm = pl.BlockSpec(memory_space=pltpu.VMEM)
    flat = [a for p in params for a in p]
    outs = pl.pallas_call(
        body, name=name, in_specs=[vm] * (4 * n), out_specs=[vm] * (3 * n),
        out_shape=[jax.ShapeDtypeStruct(p[0].shape, F32) for p in params for _ in range(3)])(*flat)
    return [outs[3 * p:3 * p + 3] for p in range(n)]


def _rs_final(own, recv, w, m, v, tr, name):
    r, c = own.shape

    def body(o_ref, r_ref, w_ref, m_ref, v_ref, g_ref, d_ref, mo_ref, vo_ref):
        g = o_ref[...] + r_ref[0].astype(F32) + r_ref[1].astype(F32) + r_ref[2].astype(F32)
        g_ref[...] = g
        d_ref[...], mo_ref[...], vo_ref[...] = _adamw_math(w_ref[...], g, m_ref[...], v_ref[...])

    tile = pl.BlockSpec((tr, c), lambda i: (i, 0))
    sds = jax.ShapeDtypeStruct((r, c), F32)
    return pl.pallas_call(
        body, name=name, grid=(r // tr,),
        in_specs=[tile, pl.BlockSpec((3, tr, c), lambda i: (0, i, 0)), tile, tile, tile],
        out_specs=[tile] * 4, out_shape=[sds] * 4,
        compiler_params=_cp(("arbitrary",), 48),
    )(own, recv, w, m, v)


def _mod_weight_update(cs, dm, w, m, v, tr, name):
    r, c = w.shape

    def body(cs_ref, dm_ref, w_ref, m_ref, v_ref, g_ref, d_ref, mo_ref, vo_ref):
        g = lax.dot_general(cs_ref[...], dm_ref[...], (((0,), (0,)), ((), ())),
                            precision=HI, preferred_element_type=F32)
        g_ref[...] = g
        d_ref[...], mo_ref[...], vo_ref[...] = _adamw_math(w_ref[...], g, m_ref[...], v_ref[...])

    tile = pl.BlockSpec((tr, c), lambda i: (i, 0))
    sds = jax.ShapeDtypeStruct((r, c), F32)
    return pl.pallas_call(
        body, name=name, grid=(r // tr,),
        in_specs=[pl.BlockSpec((NDEV, tr), lambda i: (0, i)), pl.BlockSpec((NDEV, c), lambda i: (0, 0)),
                  tile, tile, tile],
        out_specs=[tile] * 4, out_shape=[sds] * 4,
        compiler_params=_cp(("arbitrary",), 48),
    )(cs, dm, w, m, v)


def _rows(*vs):
    d = vs[0].shape[-1]
    rows = [v.reshape(1, d) for v in vs]
    return jnp.concatenate(rows + [jnp.zeros((8 - len(rows), d), F32)], axis=0)


def _block_diag_pairs(w):
    hd = w.shape[-1]
    z = jnp.zeros((w.shape[0] // 2, hd, hd), w.dtype)
    top = jnp.concatenate([w[0::2], z], axis=2)
    bot = jnp.concatenate([z, w[1::2]], axis=2)
    return jnp.concatenate([top, bot], axis=1).astype(BF16)


def _diag_pairs(g):
    hd = g.shape[-1] // 2
    both = jnp.stack([g[:, :hd, :hd], g[:, hd:, hd:]], axis=1)
    return both.reshape(2 * g.shape[0], hd, hd)


def kernel(x, c, w_mod, b_mod, g_ffn1, w_ffn1_in, w_ffn1_out, g_mix, w_in, conv_w, conv_b, ln_g, ln_b, rnn_conv_w, rnn_conv_b, w_a, b_a, w_i, b_i, lru_lambda, w_out, g_ffn2, w_ffn2_in, w_ffn2_out, w_fmod, b_fmod, g_final, loss_target, m_w_mod, m_b_mod, m_g_ffn1, m_w_ffn1_in, m_w_ffn1_out, m_g_mix, m_w_in, m_conv_w, m_conv_b, m_ln_g, m_ln_b, m_rnn_conv_w, m_rnn_conv_b, m_w_a, m_b_a, m_w_i, m_b_i, m_lru_lambda, m_w_out, m_g_ffn2, m_w_ffn2_in, m_w_ffn2_out, m_w_fmod, m_b_fmod, m_g_final, v_w_mod, v_b_mod, v_g_ffn1, v_w_ffn1_in, v_w_ffn1_out, v_g_mix, v_w_in, v_conv_w, v_conv_b, v_ln_g, v_ln_b, v_rnn_conv_w, v_rnn_conv_b, v_w_a, v_b_a, v_w_i, v_b_i, v_lru_lambda, v_w_out, v_g_ffn2, v_w_ffn2_in, v_w_ffn2_out, v_w_fmod, v_b_fmod, v_g_final):
    t, d = x.shape[1], x.shape[2]
    fb = w_ffn1_in.shape[2]
    nm = w_mod.shape[2]
    nf = w_fmod.shape[1]
    dc = conv_b.shape[1]
    cl = conv_w.shape[2]
    tm = min(TOKEN_TILE, t)
    tk = min(WGRAD_TILE, t)
    nk = t // tk
    me = 4 * lax.axis_index("x") + 2 * lax.axis_index("y") + lax.axis_index("c")

    tr = jnp.transpose
    bmod_l = lax.dynamic_slice(b_mod, (0, me * nm), (1, nm))
    bfmod_l = lax.dynamic_slice(b_fmod.reshape(1, -1), (0, me * nf), (1, nf))
    cwl = jnp.concatenate([conv_w[0], jnp.zeros((1, cl), F32), rnn_conv_w[0], jnp.zeros((4, cl), F32)], axis=0)
    cs, mod_rows, fmod_rows, wi1, wo1, cwg = _mod_exchange(
        c, w_mod[0], bmod_l, w_fmod, bfmod_l, "mod_and_gather_ffn1",
        host=_Exchange("gather", [tr(w_ffn1_in[0]).astype(BF16), w_ffn1_out[0].astype(BF16), cwl]))
    wi1 = wi1.reshape(2, 4, fb, d)
    wo1 = wo1.reshape(4 * fb, d)
    mod = mod_rows.reshape(9, d)
    fmod = fmod_rows.reshape(2, d)
    vec1 = _rows(g_ffn1, mod[0], mod[1], mod[2])
    vecm = _rows(g_mix, mod[3], mod[4], mod[5])
    vec3 = _rows(g_ffn2, mod[6], mod[7], mod[8])
    vecf = _rows(g_final, fmod[0], fmod[1])

    xin = x[0]
    later = [w_in[0].astype(BF16), w_out[0].astype(BF16), tr(w_ffn2_in[0]).astype(BF16), w_ffn2_out[0].astype(BF16)]
    x1, h1, gu1, f1, win, wout, wi2, wo2 = _ffn_fwd(xin, vec1, wi1, wo1, tm, "ffn1_fwd",
                                                    host=_Exchange("gather", later))
    wi2 = wi2.reshape(2, 4, fb, d)
    wo2 = wo2.reshape(4 * fb, d)
    wout = wout.reshape(d, d)
    h2, proj = _mix_in(x1, vecm, win, tm, "mix_in")
    lnv = _rows(ln_g, ln_b)
    rvec = _rows(rnn_conv_b, b_a, b_i, lru_lambda)
    wab = _block_diag_pairs(w_a[0])
    wib = _block_diag_pairs(w_i[0])
    cwf = jnp.transpose(cwg, (1, 0, 2)).reshape(40, NDEV * cl)
    cw32 = jnp.concatenate([cwf[0:CONV_W], conv_b], axis=0)
    rw8 = cwf[32:40]

    (cv,) = _conv_fwd(proj, cw32, "conv_fwd")
    hr, yr = _rnn_fwd(proj, rw8, rvec, wab, wib, "rnn_fwd")
    x2, ym, ycat = _mix_out(x1, cv, yr, vecm, lnv, wout, tm, "mix_out")
    x3, h3, gu3, f3 = _ffn_fwd(x2, vec3, wi2, wo2, tm, "ffn2_fwd")

    dx3, dvf, df3, dva3 = _final(x3, loss_target[0], vecf, f3, vec3, tm, "final_loss")
    a_tok = lambda width: pl.BlockSpec((tk, width), lambda k, s: (s, 0))
    blk3 = lambda width: pl.BlockSpec((None, tk, width), lambda k, s: (k, s, 0))
    sel = jnp.stack([lax.axis_index("c"), 2 * lax.axis_index("x") + lax.axis_index("y")]).astype(jnp.int32)
    row_tile = {"w_ffn1_in": fb // 4, "w_ffn1_out": fb // 4, "w_in": 512, "w_out": 128,
                "w_ffn2_in": fb // 4, "w_ffn2_out": fb // 4}

    def chip_sums(names, partials, from_sib):
        out = [_chip_sum(p.reshape((4, 2) + p.shape[1:]), r, sel, p.shape[1], "chip_sum_" + nm_)
               for nm_, p, r in zip(names, partials, from_sib)]
        return [o[0] for o in out], [o[1] for o in out]

    dgu3, p_wi2, p_wo2 = _ffn_bwd_w(df3, gu3, h3, wo2, tm, "ffn2_bwd_w")
    p_wi2 = p_wi2.reshape(NDEV, fb, d)
    p_wo2 = p_wo2.reshape(NDEV, fb // 2, d)
    names2 = ["w_ffn2_in", "w_ffn2_out"]
    dx2, dv3, sib_wi2, sib_wo2 = _ffn_bwd_in(dx3, x2, vec3, dgu3, wi2, tm, "ffn2_bwd_in",
                                             host=_Exchange("sibling", [p_wi2, p_wo2]))
    sums2, owns2 = chip_sums(names2, [p_wi2, p_wo2], [sib_wi2, sib_wo2])
    started2 = _chips_split_start(sums2, "rs_chips_ffn2_start")
    dym, dcv, dhr, duy, dln, dgt2 = _mixout_bwd(
        dx2, ym, cv, hr, proj, vecm + started2[-1][0:1, 0:1], lnv, wout, tm, "mixout_bwd")
    dval, dgate, dcw = _conv_bwd(proj, dcv, cw32, "conv_bwd")
    dux, rsm, dwab, dwib = _rnn_bwd(proj, hr, dhr, rw8, rvec, wab, wib, "rnn_bwd")
    parts = [dval, dgate, dux, duy]
    dx1, dvm, df1, dva1 = _mixin_bwd(dx2, x1, parts, vecm, win, f1, vec1, tm, "mixin_bwd")
    p_wout = _mm_tn(ycat, dym, a_tok(d), a_tok(d), 1, nk, d, d, "wgrad_out").reshape(NDEV, d // NDEV, d)
    p_win = _wgrad_in(h2, parts, win.shape[2], min(tk, 1024), "wgrad_in")
    namesm = ["w_in", "w_out"]
    sumsm, ownsm = chip_sums(namesm, [p_win, p_wout],
                             _exchange(_Exchange("sibling", [p_win, p_wout]), "rs_sibling_mix"))
    lane_pad = lambda v: jnp.concatenate([v, jnp.zeros_like(v)], axis=1)
    startedm = _chips_split_start(sumsm, "rs_chips_mix_start")
    early = jnp.concatenate([dva3, dv3, dvf, dvm, dgt2, dva1 + startedm[-1][0:1, 0:1], dcw.reshape(16, d),
                             lane_pad(dln), lane_pad(rsm),
                             _diag_pairs(dwab).reshape(32, d), _diag_pairs(dwib).reshape(32, d)], axis=0)
    dgu1, p_wi1, p_wo1, all_early = _ffn_bwd_w(
        df1, gu1, h1, wo1, tm, "ffn1_bwd_w", host=_Exchange("direct", [early]))
    p_wi1 = p_wi1.reshape(NDEV, fb, d)
    p_wo1 = p_wo1.reshape(NDEV, fb // 2, d)
    names1 = ["w_ffn1_in", "w_ffn1_out"]
    sums1, owns1 = chip_sums(names1, [p_wi1, p_wo1],
                             _exchange(_Exchange("sibling", [p_wi1, p_wo1]), "rs_sibling_ffn1"))
    started = _chips_split_start(sums1, "rs_chips_ffn1_start")
    dx0, dv1 = _ffn_bwd_in(dx1, xin, vec1 + started[-1][0:1, 0:1], dgu1, wi1, tm, "ffn1_bwd_in")
    from_chips = dict(zip(names2, _chips_split_wait(started2, len(sums2), dx0, "rs_chips_ffn2_wait")))
    from_chips.update(zip(namesm, _chips_split_wait(startedm, len(sumsm), dx0, "rs_chips_mix_wait")))
    owns = dict(zip(namesm + names2 + names1, ownsm + owns2 + owns1))

    big = {"w_ffn1_in": (tr(w_ffn1_in[0]), tr(m_w_ffn1_in[0]), tr(v_w_ffn1_in[0])),
           "w_ffn1_out": (w_ffn1_out[0], m_w_ffn1_out[0], v_w_ffn1_out[0]),
           "w_in": (w_in[0], m_w_in[0], v_w_in[0]), "w_out": (w_out[0], m_w_out[0], v_w_out[0]),
           "w_ffn2_in": (tr(w_ffn2_in[0]), tr(m_w_ffn2_in[0]), tr(v_w_ffn2_in[0])),
           "w_ffn2_out": (w_ffn2_out[0], m_w_ffn2_out[0], v_w_ffn2_out[0])}
    res = {}

    def final_sum(nm_):
        out4 = _rs_final(owns[nm_], from_chips[nm_], *big[nm_], row_tile[nm_], "rs_final_" + nm_)
        res[nm_] = [(tr(o) if nm_ in ("w_ffn1_in", "w_ffn2_in") else o)[None] for o in out4]
        return out4[0]

    done = [final_sum(nm_) for nm_ in namesm + names2]
    dfm_all = jnp.concatenate([all_early[:, 17], all_early[:, 19]], axis=1)
    dfm_l = lax.dynamic_slice(dfm_all, (0, me * nf), (NDEV, nf))
    res["w_fmod"] = list(_mod_weight_update(cs, dfm_l, w_fmod, m_w_fmod, v_w_fmod, 256, "w_fmod_update"))
    (all_late,) = _exchange(_Exchange("direct", [dv1]), "late_table", after=done + [res["w_fmod"][0]])
    from_chips["w_ffn1_in"], from_chips["w_ffn1_out"] = _chips_split_wait(
        started, len(sums1), all_late, "rs_chips_ffn1_wait")
    for nm_ in names1:
        final_sum(nm_)
    te, tl = _table_sum([all_early, all_late], "table_sum")
    loss = jnp.sum(te[20])

    mod_rows_of = lambda e, l: [l[1], l[3], e[42], e[25], e[27], e[32], e[9], e[11], e[2]]
    dm_all = jnp.concatenate(mod_rows_of(jnp.swapaxes(all_early, 0, 1), jnp.swapaxes(all_late, 0, 1)), axis=1)
    dm_l = lax.dynamic_slice(dm_all, (0, me * nm), (NDEV, nm))
    res["w_mod"] = [o[None] for o in
                    _mod_weight_update(cs, dm_l, w_mod[0], m_w_mod[0], v_w_mod[0], 256, "w_mod_update")]

    dcw_f = te[48:64].reshape(32, dc)
    rsm_f = te[72:80, 0:dc]
    small_grads = {
        "b_mod": jnp.concatenate(mod_rows_of(te, tl)).reshape(1, 9 * d),
        "b_fmod": jnp.concatenate([te[17], te[19]]),
        "g_ffn1": tl[0:1], "g_mix": te[24:25], "g_ffn2": te[8:9], "g_final": te[16],
        "conv_w": lax.dynamic_slice(dcw_f, (0, me * cl), (CONV_W, cl))[None],
        "conv_b": dcw_f[31:32],
        "ln_g": te[64:65, 0:dc], "ln_b": te[65:66, 0:dc],
        "rnn_conv_w": lax.dynamic_slice(rsm_f, (0, me * cl), (RNN_CONV_W, cl))[None],
        "rnn_conv_b": rsm_f[4:5], "b_a": rsm_f[5:6], "b_i": rsm_f[6:7], "lru_lambda": rsm_f[7:8],
        "w_a": te[80:112].reshape(w_a.shape), "w_i": te[112:144].reshape(w_i.shape),
    }
    small_params = {
        "b_mod": (b_mod, m_b_mod, v_b_mod), "b_fmod": (b_fmod, m_b_fmod, v_b_fmod),
        "g_ffn1": (g_ffn1, m_g_ffn1, v_g_ffn1), "g_mix": (g_mix, m_g_mix, v_g_mix),
        "g_ffn2": (g_ffn2, m_g_ffn2, v_g_ffn2), "g_final": (g_final, m_g_final, v_g_final),
        "conv_w": (conv_w, m_conv_w, v_conv_w), "conv_b": (conv_b, m_conv_b, v_conv_b),
        "ln_g": (ln_g, m_ln_g, v_ln_g), "ln_b": (ln_b, m_ln_b, v_ln_b),
        "rnn_conv_w": (rnn_conv_w, m_rnn_conv_w, v_rnn_conv_w),
        "rnn_conv_b": (rnn_conv_b, m_rnn_conv_b, v_rnn_conv_b),
        "w_a": (w_a, m_w_a, v_w_a), "b_a": (b_a, m_b_a, v_b_a),
        "w_i": (w_i, m_w_i, v_w_i), "b_i": (b_i, m_b_i, v_b_i),
        "lru_lambda": (lru_lambda, m_lru_lambda, v_lru_lambda),
    }
    two_d = lambda w: (-1, w.shape[-1]) if w.ndim > 1 else (1, w.shape[0])
    small_names = list(small_grads)
    small_outs = _adamw_small(
        [(w.reshape(two_d(w)), small_grads[nm_].reshape(two_d(w)), m.reshape(two_d(w)), v.reshape(two_d(w)))
         for nm_ in small_names for (w, m, v) in [small_params[nm_]]], "adamw_small")
    for nm_, outs in zip(small_names, small_outs):
        shp = small_params[nm_][0].shape
        res[nm_] = [small_grads[nm_].reshape(shp)] + [o.reshape(shp) for o in outs]

    order = ["w_mod", "b_mod", "g_ffn1", "w_ffn1_in", "w_ffn1_out", "g_mix", "w_in", "conv_w", "conv_b",
             "ln_g", "ln_b", "rnn_conv_w", "rnn_conv_b", "w_a", "b_a", "w_i", "b_i", "lru_lambda", "w_out",
             "g_ffn2", "w_ffn2_in", "w_ffn2_out", "w_fmod", "b_fmod", "g_final"]
    return (loss, dx0[None], *[res[n][0] for n in order], *[res[n][1] for n in order],
            *[res[n][2] for n in order], *[res[n][3] for n in order])
```

```python
import functools
import math

import jax
import jax.numpy as jnp
from jax import lax
from jax.experimental import pallas as pl
from jax.experimental.pallas import tpu as pltpu

F32 = jnp.float32
BF16 = jnp.bfloat16
MESH_IDS = pl.DeviceIdType.MESH
NDEV = 8
EPS = 1e-6
RG_C = 8.0
CONV_W = 31
RNN_CONV_W = 4
LANES = 128
ADAM_LR = 0.001
ADAM_B1 = 0.9
ADAM_B2 = 0.999
ADAM_EPS = 1e-08
ADAM_WD = 0.01
ADAM_STEP = 10
TOKEN_TILE = 512
WGRAD_TILE = 2048
ROW_GROUP = 16
HI = lax.Precision.HIGHEST


def _cp(sem, vmem_mb):
    return pltpu.CompilerParams(dimension_semantics=sem, vmem_limit_bytes=vmem_mb * 1024 * 1024)


def _dot(a, b):
    return jnp.dot(a, b, preferred_element_type=F32)


def _dot_nt(a, b):
    return lax.dot_general(a, b, (((1,), (1,)), ((), ())), preferred_element_type=F32)


def _dot_tn(a, b):
    return lax.dot_general(a, b, (((0,), (0,)), ((), ())), preferred_element_type=F32)


def _sigmoid(x):
    return 1.0 / (1.0 + jnp.exp(-x))


def _adaln(x, vec_ref):
    rstd = lax.rsqrt(jnp.mean(x * x, axis=-1, keepdims=True) + EPS)
    return (x * rstd) * vec_ref[0:1, :] * (1.0 + vec_ref[2:3, :]) + vec_ref[1:2, :]


def _adaln_bwd(x, dh, vec_ref, dvec_ref):
    rstd = lax.rsqrt(jnp.mean(x * x, axis=-1, keepdims=True) + EPS)
    xhat = x * rstd
    dvec_ref[0:1, :] += jnp.sum(dh * xhat, axis=0, keepdims=True)
    dvec_ref[1:2, :] += jnp.sum(dh, axis=0, keepdims=True)
    dxhat = dh * (vec_ref[0:1, :] * (1.0 + vec_ref[2:3, :]))
    return rstd * (dxhat - xhat * jnp.mean(dxhat * xhat, axis=-1, keepdims=True))


def _adaln_finish(vec_ref, dvec_ref):
    s = dvec_ref[0:1, :]
    dvec_ref[3:4, :] = vec_ref[0:1, :] * s
    dvec_ref[0:1, :] = (1.0 + vec_ref[2:3, :]) * s


def _gelu_and_grad(x):
    k0 = math.sqrt(2.0 / math.pi)
    x2 = x * x
    t = jnp.tanh(k0 * (x + 0.044715 * x * x2))
    g = 0.5 * x * (1.0 + t)
    dg = 0.5 * (1.0 + t) + 0.5 * x * (1.0 - t * t) * (k0 * (1.0 + 3.0 * 0.044715 * x2))
    return g, dg


def _log_sigmoid(x):
    z = jnp.exp(-jnp.abs(x))
    u = 1.0 + z
    d = u - 1.0
    log1p = jnp.where(d == 0.0, z, jnp.log(u) * (z / jnp.where(d == 0.0, 1.0, d)))
    return jnp.minimum(x, 0.0) - log1p


def _neg_expm1(x):
    series = -x * (1.0 + x * (0.5 + x * (1.0 / 6.0 + x * (1.0 / 24.0 + x * (1.0 / 120.0)))))
    return jnp.where(x > -0.05, series, 1.0 - jnp.exp(x))


SUBLANES = 8


def _doubling_scan(a, b, reverse):
    n = a.shape[0]
    row = lax.broadcasted_iota(jnp.int32, a.shape, 0)
    s = 1
    while s < n:
        ok = (row < n - s) if reverse else (row >= s)
        shift = n - s if reverse else s
        b = a * jnp.where(ok, pltpu.roll(b, shift, 0), 0.0) + b
        if 2 * s < n:
            a = a * jnp.where(ok, pltpu.roll(a, shift, 0), 1.0)
        s *= 2
    return b


def _tiled_scan(a, b, reverse, sa_ref, sb_ref, carry_ref, out_ref):
    n = a.shape[0]
    nt8 = n // SUBLANES
    sub = lax.broadcasted_iota(jnp.int32, a.shape, 0) % SUBLANES
    for s in (1, 2, 4):
        ok = (sub < SUBLANES - s) if reverse else (sub >= s)
        shift = n - s if reverse else s
        b = a * jnp.where(ok, pltpu.roll(b, shift, 0), 0.0) + b
        a = a * jnp.where(ok, pltpu.roll(a, shift, 0), 1.0)
    sa_ref[...] = a
    sb_ref[...] = b
    edge = 0 if reverse else SUBLANES - 1
    at = sa_ref[pl.ds(edge, nt8, stride=SUBLANES), :]
    bt = sb_ref[pl.ds(edge, nt8, stride=SUBLANES), :]
    xt = _doubling_scan(at, bt, reverse)
    rowt = lax.broadcasted_iota(jnp.int32, xt.shape, 0)
    if reverse:
        carry_ref[...] = jnp.where(rowt < nt8 - 1, pltpu.roll(xt, nt8 - 1, 0), 0.0)
    else:
        carry_ref[...] = jnp.where(rowt >= 1, pltpu.roll(xt, 1, 0), 0.0)
    for r in range(nt8):
        rows = slice(r * SUBLANES, (r + 1) * SUBLANES)
        out_ref[rows, :] = sa_ref[rows, :] * carry_ref[r:r + 1, :] + sb_ref[rows, :]


def _rglru_gates(xr, wa_ref, wi_ref, rvec_ref):
    xb = xr.astype(BF16)
    r = _sigmoid(_dot(xb, wa_ref[...]) + rvec_ref[1:2, :])
    ig = _sigmoid(_dot(xb, wi_ref[...]) + rvec_ref[2:3, :])
    ls = _log_sigmoid(rvec_ref[3:4, :])
    log_a = RG_C * r * ls
    a = jnp.exp(log_a)
    mult = jnp.sqrt(_neg_expm1(2.0 * log_a))
    return xb, r, ig, ls, a, mult


def _rnn_conv(ux, rw_ref, rvec_ref, ext_ref):
    t = ux.shape[0]
    ext_ref[0:8, :] = jnp.zeros((8, ux.shape[1]), F32)
    ext_ref[8:, :] = ux
    xr = rvec_ref[0:1, :] + rw_ref[RNN_CONV_W - 1:RNN_CONV_W, :] * ux
    for k in range(RNN_CONV_W - 1):
        d = RNN_CONV_W - 1 - k
        xr = xr + rw_ref[k:k + 1, :] * ext_ref[8 - d:8 - d + t, :]
    return xr


def _ffn_fwd(x, vec, wi, wo, tm, name, host=None):
    t, d = x.shape
    nj, fb = wi.shape[1], wi.shape[2]
    nt = t // tm

    def body(ins, outs, scr):
        x_ref, vec_ref, wi_ref, wo_ref = ins
        xo_ref, h_ref, gu_ref, f_ref = outs
        acc_ref, = scr
        j = pl.program_id(1)

        @pl.when(j == 0)
        def _():
            h_ref[...] = _adaln(x_ref[...], vec_ref).astype(BF16)
            acc_ref[...] = jnp.zeros_like(acc_ref)

        h = h_ref[...]
        gate = _dot_nt(h, wi_ref[0])
        up = _dot_nt(h, wi_ref[1])
        gu_ref[0] = gate.astype(BF16)
        gu_ref[1] = up.astype(BF16)
        act = (gate * _sigmoid(gate) * up).astype(BF16)
        acc_ref[...] += _dot(act, wo_ref[...])

        @pl.when(j == nj - 1)
        def _():
            f = acc_ref[...]
            f_ref[...] = f.astype(BF16)
            xo_ref[...] = x_ref[...] + 0.5 * vec_ref[3:4, :] * f

    tile = pl.BlockSpec((tm, d), lambda i, j: (i, 0))
    return _hosted_call(
        body, host, name, (nt, nj),
        [tile,
         pl.BlockSpec((8, d), lambda i, j: (0, 0)),
         pl.BlockSpec((2, None, fb, d), lambda i, j: (0, j, 0, 0)),
         pl.BlockSpec((fb, d), lambda i, j: (j, 0))],
        [tile, tile, pl.BlockSpec((2, None, tm, fb), lambda i, j: (0, j, i, 0)), tile],
        [jax.ShapeDtypeStruct((t, d), F32), jax.ShapeDtypeStruct((t, d), BF16),
         jax.ShapeDtypeStruct((2, nj, t, fb), BF16), jax.ShapeDtypeStruct((t, d), BF16)],
        [pltpu.VMEM((tm, d), F32)], ("arbitrary", "arbitrary"), 48, (x, vec, wi, wo))


def _ffn_fwd_loss(x, vec, wi, wo, tgt, fvec, tm, name):
    t, d = x.shape
    nj, fb = wi.shape[1], wi.shape[2]
    nt = t // tm

    def body(x_ref, vec_ref, wi_ref, wo_ref, t_ref, fvec_ref, h_ref, gu_ref, dx_ref, dvec_ref, df_ref, dgt_ref,
             acc_ref):
        i = pl.program_id(0)
        j = pl.program_id(1)

        @pl.when((i == 0) & (j == 0))
        def _():
            dvec_ref[...] = jnp.zeros_like(dvec_ref)
            dgt_ref[...] = jnp.zeros_like(dgt_ref)

        @pl.when(j == 0)
        def _():
            h_ref[...] = _adaln(x_ref[...], vec_ref).astype(BF16)
            acc_ref[...] = jnp.zeros_like(acc_ref)

        h = h_ref[...]
        gate = _dot_nt(h, wi_ref[0])
        up = _dot_nt(h, wi_ref[1])
        gu_ref[0] = gate.astype(BF16)
        gu_ref[1] = up.astype(BF16)
        act = (gate * _sigmoid(gate) * up).astype(BF16)
        acc_ref[...] += _dot(act, wo_ref[...])

        @pl.when(j == nj - 1)
        def _():
            f = acc_ref[...]
            xo = x_ref[...] + 0.5 * vec_ref[3:4, :] * f
            e = _adaln(xo, fvec_ref) - t_ref[...]
            dvec_ref[4:5, :] += (0.5 / d) * jnp.sum(e * e, axis=0, keepdims=True)
            dx = _adaln_bwd(xo, e * (1.0 / d), fvec_ref, dvec_ref)
            dx_ref[...] = dx
            df_ref[...] = (0.5 * vec_ref[3:4, :] * dx).astype(BF16)
            dgt_ref[2:3, :] += 0.5 * jnp.sum(dx * f, axis=0, keepdims=True)

        @pl.when((i == nt - 1) & (j == nj - 1))
        def _():
            _adaln_finish(fvec_ref, dvec_ref)

    tile = pl.BlockSpec((tm, d), lambda i, j: (i, 0))
    tab = pl.BlockSpec((8, d), lambda i, j: (0, 0))
    return pl.pallas_call(
        body, name=name, grid=(nt, nj),
        in_specs=[tile, tab,
                  pl.BlockSpec((2, None, fb, d), lambda i, j: (0, j, 0, 0)),
                  pl.BlockSpec((fb, d), lambda i, j: (j, 0)),
                  pl.BlockSpec((tm, d), lambda i, j: (jnp.where(j == nj - 1, i, jnp.maximum(i - 1, 0)), 0)), tab],
        out_specs=[tile, pl.BlockSpec((2, None, tm, fb), lambda i, j: (0, j, i, 0)), tile, tab, tile, tab],
        out_shape=[jax.ShapeDtypeStruct((t, d), BF16), jax.ShapeDtypeStruct((2, nj, t, fb), BF16),
                   jax.ShapeDtypeStruct((t, d), F32), jax.ShapeDtypeStruct((8, d), F32),
                   jax.ShapeDtypeStruct((t, d), BF16), jax.ShapeDtypeStruct((8, d), F32)],
        scratch_shapes=[pltpu.VMEM((tm, d), F32)],
        compiler_params=_cp(("arbitrary", "arbitrary"), 56),
    )(x, vec, wi, wo, tgt, fvec)


def _mix_in(x, vec, win, tm, name, host=None):
    t, d = x.shape
    nb, _, cb = win.shape

    def body(ins, outs, scr):
        x_ref, vec_ref, w_ref = ins
        h_ref, p_ref = outs
        h = _adaln(x_ref[...], vec_ref).astype(BF16)
        h_ref[...] = h
        for k in range(nb):
            p_ref[:, k * cb:(k + 1) * cb] = _dot(h, w_ref[k])

    return _hosted_call(
        body, host, name, (t // tm,),
        [pl.BlockSpec((tm, d), lambda i: (i, 0)),
         pl.BlockSpec((8, d), lambda i: (0, 0)),
         pl.BlockSpec((nb, d, cb), lambda i: (0, 0, 0))],
        [pl.BlockSpec((tm, d), lambda i: (i, 0)),
         pl.BlockSpec((tm, nb * cb), lambda i: (i, 0))],
        [jax.ShapeDtypeStruct((t, d), BF16), jax.ShapeDtypeStruct((t, nb * cb), F32)],
        [], ("arbitrary",), 48, (x, vec, win))


def _conv_fwd(proj, cw32, name, host=None):
    t = proj.shape[0]
    nblk = cw32.shape[1] // LANES
    ch = min(t, 128)

    def body(ins, outs, scr):
        val_ref, gate_ref, cw_ref = ins
        cv_ref, = outs
        ext_ref, = scr
        ext_ref[0:32, :] = jnp.zeros((32, LANES), F32)
        ext_ref[32:, :] = val_ref[...] * _sigmoid(gate_ref[...])
        for r in range(t // ch):
            acc = jnp.broadcast_to(cw_ref[31:32, :], (ch, LANES))
            for k in range(CONV_W):
                off = 32 + r * ch - (CONV_W - 1 - k)
                acc = acc + cw_ref[k:k + 1, :] * ext_ref[off:off + ch, :]
            cv_ref[r * ch:(r + 1) * ch, :] = acc

    return _hosted_call(
        body, host, name, (nblk,),
        [pl.BlockSpec((t, LANES), lambda c: (0, c)),
         pl.BlockSpec((t, LANES), lambda c: (0, nblk + c)),
         pl.BlockSpec((32, LANES), lambda c: (0, c))],
        [pl.BlockSpec((t, LANES), lambda c: (0, c))],
        [jax.ShapeDtypeStruct((t, nblk * LANES), F32)],
        [pltpu.VMEM((t + 32, LANES), F32)], ("arbitrary",), 48, (proj, proj, cw32))


def _rnn_fwd(proj, rw8, rvec, wab, wib, name, host=None):
    t = proj.shape[0]
    nblk = rvec.shape[1] // LANES

    def body(ins, outs, scr):
        ux_ref, uy_ref, rw_ref, rvec_ref, wa_ref, wi_ref = ins
        h_ref, yr_ref = outs
        ext_ref, sa_ref, sb_ref, carry_ref = scr
        xr = _rnn_conv(ux_ref[...], rw_ref, rvec_ref, ext_ref)
        _, _, ig, _, a, mult = _rglru_gates(xr, wa_ref, wi_ref, rvec_ref)
        _tiled_scan(a, mult * (ig * xr), False, sa_ref, sb_ref, carry_ref, h_ref)
        ge, _ = _gelu_and_grad(uy_ref[...])
        yr_ref[...] = (ge * h_ref[...]).astype(BF16)

    blk = lambda off: pl.BlockSpec((t, LANES), lambda c: (0, off + c))
    return _hosted_call(
        body, host, name, (nblk,),
        [blk(2 * nblk), blk(3 * nblk),
         pl.BlockSpec((8, LANES), lambda c: (0, c)),
         pl.BlockSpec((8, LANES), lambda c: (0, c)),
         pl.BlockSpec((None, LANES, LANES), lambda c: (c, 0, 0)),
         pl.BlockSpec((None, LANES, LANES), lambda c: (c, 0, 0))],
        [blk(0), blk(0)],
        [jax.ShapeDtypeStruct((t, nblk * LANES), F32), jax.ShapeDtypeStruct((t, nblk * LANES), BF16)],
        [pltpu.VMEM((t + 8, LANES), F32), pltpu.VMEM((t, LANES), F32), pltpu.VMEM((t, LANES), F32),
         pltpu.VMEM((t // SUBLANES, LANES), F32)], ("arbitrary",), 56, (proj, proj, rw8, rvec, wab, wib))


def _ln_silu(cv, lnv_ref):
    mu = jnp.mean(cv, axis=-1, keepdims=True)
    xc = cv - mu
    rs = lax.rsqrt(jnp.mean(xc * xc, axis=-1, keepdims=True) + EPS)
    chat = xc * rs
    z = chat * lnv_ref[0:1, :] + lnv_ref[1:2, :]
    sg = _sigmoid(z)
    return rs, chat, z, sg


def _mix_out(x, cv, yr, vec, lnv, wout, tm, name, host=None):
    t, d = x.shape
    dc = cv.shape[1]

    def body(ins, outs, scr):
        x_ref, cv_ref, yr_ref, vec_ref, lnv_ref, w_ref = ins
        xo_ref, ym_ref, yc_ref = outs
        _, _, z, sg = _ln_silu(cv_ref[...], lnv_ref)
        yc = (z * sg).astype(BF16)
        yr = yr_ref[...]
        yc_ref[:, 0:dc] = yc
        yc_ref[:, dc:] = yr
        ym = _dot(yc, w_ref[0:dc, :]) + _dot(yr, w_ref[dc:, :])
        ym_ref[...] = ym.astype(BF16)
        xo_ref[...] = x_ref[...] + vec_ref[3:4, :] * ym

    tile = pl.BlockSpec((tm, d), lambda i: (i, 0))
    return _hosted_call(
        body, host, name, (t // tm,),
        [tile,
         pl.BlockSpec((tm, dc), lambda i: (i, 0)),
         pl.BlockSpec((tm, dc), lambda i: (i, 0)),
         pl.BlockSpec((8, d), lambda i: (0, 0)),
         pl.BlockSpec((8, dc), lambda i: (0, 0)),
         pl.BlockSpec((d, d), lambda i: (0, 0))],
        [tile, tile, tile],
        [jax.ShapeDtypeStruct((t, d), F32), jax.ShapeDtypeStruct((t, d), BF16), jax.ShapeDtypeStruct((t, d), BF16)],
        [], ("arbitrary",), 48, (x, cv, yr, vec, lnv, wout))


def _emit_df(dx, f_ref, nvec_ref, df_ref, dgt_ref):
    df_ref[...] = (0.5 * nvec_ref[3:4, :] * dx).astype(BF16)
    dgt_ref[2:3, :] += 0.5 * jnp.sum(dx * f_ref[...].astype(F32), axis=0, keepdims=True)


def _ffn_bwd_w(df, gu, h, wo, tm, name, host=None):
    t, d = df.shape
    nj, fb = gu.shape[1], gu.shape[3]
    nt = t // tm
    sub = min(tm, ROW_GROUP)

    def body(ins, outs, scr):
        df_ref, gu_ref, h_ref, wo_ref = ins
        dgu_ref, dwi_ref, dwo_ref = outs
        accg_ref, accu_ref, acco_ref, dact_ref, act_ref = scr
        i = pl.program_id(1)

        @pl.when(i == 0)
        def _():
            accg_ref[...] = jnp.zeros_like(accg_ref)
            accu_ref[...] = jnp.zeros_like(accu_ref)
            acco_ref[...] = jnp.zeros_like(acco_ref)

        dact_ref[...] = _dot_nt(df_ref[...], wo_ref[...])
        for r in range(tm // sub):
            rows = slice(r * sub, (r + 1) * sub)
            g = gu_ref[0, rows, :].astype(F32)
            u = gu_ref[1, rows, :].astype(F32)
            dact = dact_ref[rows, :]
            sg = _sigmoid(g)
            sl = g * sg
            dgu_ref[0, rows, :] = (dact * u * (sg * (1.0 + g * (1.0 - sg)))).astype(BF16)
            dgu_ref[1, rows, :] = (dact * sl).astype(BF16)
            act_ref[rows, :] = (sl * u).astype(BF16)
        hb = h_ref[...]
        acco_ref[...] += _dot_tn(act_ref[...], df_ref[...])
        accg_ref[...] += _dot_tn(dgu_ref[0], hb)
        accu_ref[...] += _dot_tn(dgu_ref[1], hb)

        @pl.when(i == nt - 1)
        def _():
            dwi_ref[0] = accg_ref[...].astype(BF16)
            dwi_ref[1] = accu_ref[...].astype(BF16)
            dwo_ref[...] = acco_ref[...].astype(BF16)

    tile = pl.BlockSpec((tm, d), lambda j, i: (i, 0))
    return _hosted_call(
        body, host, name, (nj, nt),
        [tile,
         pl.BlockSpec((2, None, tm, fb), lambda j, i: (0, j, i, 0)),
         tile,
         pl.BlockSpec((fb, d), lambda j, i: (j, 0))],
        [pl.BlockSpec((2, None, tm, fb), lambda j, i: (0, j, i, 0)),
         pl.BlockSpec((2, None, fb, d), lambda j, i: (0, j, 0, 0)),
         pl.BlockSpec((None, fb, d), lambda j, i: (j, 0, 0))],
        [jax.ShapeDtypeStruct((2, nj, t, fb), BF16), jax.ShapeDtypeStruct((2, nj, fb, d), BF16),
         jax.ShapeDtypeStruct((nj, fb, d), BF16)],
        [pltpu.VMEM((fb, d), F32), pltpu.VMEM((fb, d), F32), pltpu.VMEM((fb, d), F32),
         pltpu.VMEM((tm, fb), F32), pltpu.VMEM((tm, fb), BF16)],
        ("arbitrary", "arbitrary"), 56, (df, gu, h, wo))


def _ffn_bwd_in(dxo, x, vec, dgu, wi, tm, name, host=None):
    t, d = x.shape
    nj, fb = wi.shape[1], wi.shape[2]
    nt = t // tm

    def body(ins, outs, scr):
        dxo_ref, x_ref, vec_ref, dgu_ref, wi_ref = ins
        dx_ref, dvec_ref = outs
        i = pl.program_id(0)

        @pl.when(i == 0)
        def _():
            dvec_ref[...] = jnp.zeros_like(dvec_ref)

        dh = jnp.zeros((tm, d), F32)
        for a in range(2):
            for k in range(nj):
                dh = dh + _dot(dgu_ref[a, k], wi_ref[a, k])
        dx_ref[...] = dxo_ref[...] + _adaln_bwd(x_ref[...], dh, vec_ref, dvec_ref)

        @pl.when(i == nt - 1)
        def _():
            _adaln_finish(vec_ref, dvec_ref)

    tile = pl.BlockSpec((tm, d), lambda i: (i, 0))
    return _hosted_call(
        body, host, name, (nt,),
        [tile, tile,
         pl.BlockSpec((8, d), lambda i: (0, 0)),
         pl.BlockSpec((2, nj, tm, fb), lambda i: (0, 0, i, 0)),
         pl.BlockSpec((2, nj, fb, d), lambda i: (0, 0, 0, 0))],
        [tile, pl.BlockSpec((8, d), lambda i: (0, 0))],
        [jax.ShapeDtypeStruct((t, d), F32), jax.ShapeDtypeStruct((8, d), F32)],
        [], ("arbitrary",), 60, (dxo, x, vec, dgu, wi))


def _mm_tn(a, b, a_spec, b_spec, nblk, nk, m, n, name):
    def body(a_ref, b_ref, o_ref, acc_ref):
        s = pl.program_id(1)

        @pl.when(s == 0)
        def _():
            acc_ref[...] = jnp.zeros_like(acc_ref)

        acc_ref[...] += _dot_tn(a_ref[...], b_ref[...])

        @pl.when(s == nk - 1)
        def _():
            o_ref[...] = acc_ref[...].astype(BF16)

    return pl.pallas_call(
        body, name=name, grid=(nblk, nk),
        in_specs=[a_spec, b_spec],
        out_specs=pl.BlockSpec((None, m, n), lambda k, s: (k, 0, 0)),
        out_shape=jax.ShapeDtypeStruct((nblk, m, n), BF16),
        scratch_shapes=[pltpu.VMEM((m, n), F32)],
        compiler_params=_cp(("arbitrary", "arbitrary"), 56),
    )(a, b)


def _wgrad_in(h, parts, cb, tk, name):
    t, d = h.shape
    dc = parts[0].shape[1]
    per = dc // cb
    nblk = len(parts) * per
    nk = t // tk

    def body(h_ref, p0, p1, p2, p3, o_ref, acc_ref):
        s = pl.program_id(0)

        @pl.when(s == 0)
        def _():
            acc_ref[...] = jnp.zeros_like(acc_ref)

        hb = h_ref[...]
        for p, p_ref in enumerate((p0, p1, p2, p3)):
            acc_ref[p] += _dot_tn(hb, p_ref[...])

        @pl.when(s == nk - 1)
        def _():
            for k in range(nblk):
                o_ref[k] = acc_ref[k // per, :, (k % per) * cb:(k % per + 1) * cb].astype(BF16)

    return pl.pallas_call(
        body, name=name, grid=(nk,),
        in_specs=[pl.BlockSpec((tk, d), lambda s: (s, 0))] + [pl.BlockSpec((tk, dc), lambda s: (s, 0))] * len(parts),
        out_specs=pl.BlockSpec((nblk, d, cb), lambda s: (0, 0, 0)),
        out_shape=jax.ShapeDtypeStruct((nblk, d, cb), BF16),
        scratch_shapes=[pltpu.VMEM((len(parts), d, dc), F32)],
        compiler_params=_cp(("arbitrary",), 56),
    )(h, *parts)


def _mixout_bwd(dxo, ym, cv, hr, proj, vec, lnv, wout, tm, name, host=None):
    t, d = dxo.shape
    dc = cv.shape[1]
    nt = t // tm

    def body(ins, outs, scr):
        dxo_ref, ym_ref, cv_ref, hr_ref, uy_ref, vec_ref, lnv_ref, w_ref = ins
        dym_ref, dcv_ref, dhr_ref, duy_ref, dln_ref, dgt_ref = outs
        i = pl.program_id(0)

        @pl.when(i == 0)
        def _():
            dln_ref[...] = jnp.zeros_like(dln_ref)
            dgt_ref[...] = jnp.zeros_like(dgt_ref)

        dxo_v = dxo_ref[...]
        dym = (vec_ref[3:4, :] * dxo_v).astype(BF16)
        dym_ref[...] = dym
        dgt_ref[0:1, :] += jnp.sum(dxo_v * ym_ref[...].astype(F32), axis=0, keepdims=True)
        dyc = _dot_nt(dym, w_ref[0:dc, :])
        dyr = _dot_nt(dym, w_ref[dc:, :])
        rs, chat, z, sg = _ln_silu(cv_ref[...], lnv_ref)
        dz = dyc * (sg * (1.0 + z * (1.0 - sg)))
        dln_ref[0:1, :] += jnp.sum(dz * chat, axis=0, keepdims=True)
        dln_ref[1:2, :] += jnp.sum(dz, axis=0, keepdims=True)
        dchat = dz * lnv_ref[0:1, :]
        dcv_ref[...] = (rs * (dchat - jnp.mean(dchat, axis=-1, keepdims=True)
                              - chat * jnp.mean(dchat * chat, axis=-1, keepdims=True))).astype(BF16)
        ge, dge = _gelu_and_grad(uy_ref[...])
        dhr_ref[...] = (dyr * ge).astype(BF16)
        duy_ref[...] = (dyr * hr_ref[...] * dge).astype(BF16)

    tile_d = pl.BlockSpec((tm, d), lambda i: (i, 0))
    tile_c = pl.BlockSpec((tm, dc), lambda i: (i, 0))
    return _hosted_call(
        body, host, name, (nt,),
        [tile_d, tile_d, tile_c, tile_c,
         pl.BlockSpec((tm, dc), lambda i: (i, 3)),
         pl.BlockSpec((8, d), lambda i: (0, 0)),
         pl.BlockSpec((8, dc), lambda i: (0, 0)),
         pl.BlockSpec((d, d), lambda i: (0, 0))],
        [tile_d, tile_c, tile_c, tile_c,
         pl.BlockSpec((8, dc), lambda i: (0, 0)),
         pl.BlockSpec((8, d), lambda i: (0, 0))],
        [jax.ShapeDtypeStruct((t, d), BF16), jax.ShapeDtypeStruct((t, dc), BF16),
         jax.ShapeDtypeStruct((t, dc), BF16), jax.ShapeDtypeStruct((t, dc), BF16),
         jax.ShapeDtypeStruct((8, dc), F32), jax.ShapeDtypeStruct((8, d), F32)],
        [], ("arbitrary",), 48, (dxo, ym, cv, hr, proj, vec, lnv, wout))


def _conv_bwd(proj, dcv, cw32, name):
    t = proj.shape[0]
    nblk = cw32.shape[1] // LANES
    ch = min(t, 128)

    def body(val_ref, gate_ref, dcv_ref, cw_ref, dval_ref, dgate_ref, dcw_ref, extu_ref, extd_ref):
        val = val_ref[...]
        sg = _sigmoid(gate_ref[...])
        extu_ref[0:32, :] = jnp.zeros((32, LANES), F32)
        extu_ref[32:, :] = val * sg
        dcv_v = dcv_ref[...].astype(F32)
        extd_ref[0:t, :] = dcv_v
        extd_ref[t:, :] = jnp.zeros((32, LANES), F32)
        for r in range(t // ch):
            acc = jnp.zeros((ch, LANES), F32)
            for k in range(CONV_W):
                off = r * ch + (CONV_W - 1 - k)
                acc = acc + cw_ref[k:k + 1, :] * extd_ref[off:off + ch, :]
            rows = slice(r * ch, (r + 1) * ch)
            sg_r = _sigmoid(gate_ref[rows, :])
            dval_ref[rows, :] = (acc * sg_r).astype(BF16)
            dgate_ref[rows, :] = (acc * val_ref[rows, :] * sg_r * (1.0 - sg_r)).astype(BF16)
        for k in range(CONV_W):
            off = 32 - (CONV_W - 1 - k)
            dcw_ref[k:k + 1, :] = jnp.sum(dcv_v * extu_ref[off:off + t, :], axis=0, keepdims=True)
        dcw_ref[31:32, :] = jnp.sum(dcv_v, axis=0, keepdims=True)

    blk = lambda off: pl.BlockSpec((t, LANES), lambda c: (0, off + c))
    return pl.pallas_call(
        body, name=name, grid=(nblk,),
        in_specs=[blk(0), blk(nblk), blk(0), pl.BlockSpec((32, LANES), lambda c: (0, c))],
        out_specs=[blk(0), blk(0), pl.BlockSpec((32, LANES), lambda c: (0, c))],
        out_shape=[jax.ShapeDtypeStruct((t, nblk * LANES), BF16), jax.ShapeDtypeStruct((t, nblk * LANES), BF16),
                   jax.ShapeDtypeStruct((32, nblk * LANES), F32)],
        scratch_shapes=[pltpu.VMEM((t + 32, LANES), F32), pltpu.VMEM((t + 32, LANES), F32)],
        compiler_params=_cp(("arbitrary",), 56),
    )(proj, proj, dcv, cw32)


def _rnn_bwd(proj, hr, dhr, rw8, rvec, wab, wib, name, host=None):
    t = proj.shape[0]
    nblk = rvec.shape[1] // LANES

    def body(ins, outs, scr):
        ux_ref, h_ref, dh_ref, rw_ref, rvec_ref, wa_ref, wi_ref = ins
        dux_ref, sm_ref, dwa_ref, dwi_ref = outs
        ext_ref, extd_ref, sa_ref, sb_ref, carry_ref = scr
        xr = _rnn_conv(ux_ref[...], rw_ref, rvec_ref, ext_ref)
        xb, r, ig, ls, a, mult = _rglru_gates(xr, wa_ref, wi_ref, rvec_ref)
        row = lax.broadcasted_iota(jnp.int32, (t, LANES), 0)
        a_next = jnp.where(row < t - 1, pltpu.roll(a, t - 1, 0), 0.0)
        _tiled_scan(a_next, dh_ref[...].astype(F32), True, sa_ref, sb_ref, carry_ref, extd_ref)
        g = extd_ref[0:t, :]
        hprev = jnp.where(row >= 1, pltpu.roll(h_ref[...], 1, 0), 0.0)
        da = g * hprev
        dmult = g * (ig * xr)
        dig = g * mult * xr
        dxr = g * mult * ig
        dlog_a = a * (da - dmult * a / mult)
        dr = dlog_a * (RG_C * ls)
        dls = RG_C * jnp.sum(dlog_a * r, axis=0, keepdims=True)
        dpr = dr * r * (1.0 - r)
        dpi = dig * ig * (1.0 - ig)
        dprb = dpr.astype(BF16)
        dpib = dpi.astype(BF16)
        dxr = dxr + _dot_nt(dprb, wa_ref[...]) + _dot_nt(dpib, wi_ref[...])
        dwa_ref[...] = _dot_tn(xb, dprb)
        dwi_ref[...] = _dot_tn(xb, dpib)
        extd_ref[0:t, :] = dxr
        extd_ref[t:, :] = jnp.zeros((8, LANES), F32)
        dux = rw_ref[RNN_CONV_W - 1:RNN_CONV_W, :] * dxr
        for k in range(RNN_CONV_W - 1):
            d = RNN_CONV_W - 1 - k
            dux = dux + rw_ref[k:k + 1, :] * extd_ref[d:d + t, :]
        dux_ref[...] = dux.astype(BF16)
        for k in range(RNN_CONV_W):
            d = RNN_CONV_W - 1 - k
            sm_ref[k:k + 1, :] = jnp.sum(dxr * ext_ref[8 - d:8 - d + t, :], axis=0, keepdims=True)
        sm_ref[4:5, :] = jnp.sum(dxr, axis=0, keepdims=True)
        sm_ref[5:6, :] = jnp.sum(dpr, axis=0, keepdims=True)
        sm_ref[6:7, :] = jnp.sum(dpi, axis=0, keepdims=True)
        sm_ref[7:8, :] = dls * _sigmoid(-rvec_ref[3:4, :])

    blk = lambda off: pl.BlockSpec((t, LANES), lambda c: (0, off + c))
    sq = pl.BlockSpec((None, LANES, LANES), lambda c: (c, 0, 0))
    return _hosted_call(
        body, host, name, (nblk,),
        [blk(2 * nblk), blk(0), blk(0),
         pl.BlockSpec((8, LANES), lambda c: (0, c)),
         pl.BlockSpec((8, LANES), lambda c: (0, c)), sq, sq],
        [blk(0), pl.BlockSpec((8, LANES), lambda c: (0, c)), sq, sq],
        [jax.ShapeDtypeStruct((t, nblk * LANES), BF16), jax.ShapeDtypeStruct((8, nblk * LANES), F32),
         jax.ShapeDtypeStruct((nblk, LANES, LANES), F32), jax.ShapeDtypeStruct((nblk, LANES, LANES), F32)],
        [pltpu.VMEM((t + 8, LANES), F32), pltpu.VMEM((t + 8, LANES), F32), pltpu.VMEM((t, LANES), F32),
         pltpu.VMEM((t, LANES), F32), pltpu.VMEM((t // SUBLANES, LANES), F32)],
        ("arbitrary",), 60, (proj, hr, dhr, rw8, rvec, wab, wib))


def _mixin_bwd(dxo, x, parts, vec, win, f, nvec, tm, name):
    t, d = x.shape
    nb, _, cb = win.shape
    dc = parts[0].shape[1]
    per = dc // cb
    nt = t // tm

    def body(dxo_ref, x_ref, p0, p1, p2, p3, vec_ref, w_ref, f_ref, nvec_ref, dx_ref, dvec_ref, df_ref, dgt_ref):
        i = pl.program_id(0)

        @pl.when(i == 0)
        def _():
            dvec_ref[...] = jnp.zeros_like(dvec_ref)
            dgt_ref[...] = jnp.zeros_like(dgt_ref)

        prefs = (p0, p1, p2, p3)
        dh = jnp.zeros((tm, d), F32)
        for k in range(nb):
            dh = dh + _dot_nt(prefs[k // per][:, (k % per) * cb:(k % per + 1) * cb], w_ref[k])
        dx = dxo_ref[...] + _adaln_bwd(x_ref[...], dh, vec_ref, dvec_ref)
        dx_ref[...] = dx
        _emit_df(dx, f_ref, nvec_ref, df_ref, dgt_ref)

        @pl.when(i == nt - 1)
        def _():
            _adaln_finish(vec_ref, dvec_ref)

    tile_d = pl.BlockSpec((tm, d), lambda i: (i, 0))
    tile_c = pl.BlockSpec((tm, dc), lambda i: (i, 0))
    tab = pl.BlockSpec((8, d), lambda i: (0, 0))
    return pl.pallas_call(
        body, name=name, grid=(nt,),
        in_specs=[tile_d, tile_d, tile_c, tile_c, tile_c, tile_c, tab,
                  pl.BlockSpec((nb, d, cb), lambda i: (0, 0, 0)), tile_d, tab],
        out_specs=[tile_d, tab, tile_d, tab],
        out_shape=[jax.ShapeDtypeStruct((t, d), F32), jax.ShapeDtypeStruct((8, d), F32),
                   jax.ShapeDtypeStruct((t, d), BF16), jax.ShapeDtypeStruct((8, d), F32)],
        compiler_params=_cp(("arbitrary",), 48),
    )(dxo, x, *parts, vec, win, f, nvec)


def _coords():
    return lax.axis_index("x"), lax.axis_index("y"), lax.axis_index("c")


def _flip(v, bit):
    return 1 - v if bit else v


def _gather_copy(outs, send_sems, recv_sems, a, k, block, to, src=None):
    dst = outs[a].at[block]
    return pltpu.make_async_remote_copy(
        src_ref=dst if src is None else src, dst_ref=dst,
        send_sem=send_sems.at[a, k], recv_sem=recv_sems.at[a, k],
        device_id=to, device_id_type=MESH_IDS)


def _gather_start(ins, outs, send_sems, recv_sems, loc_sems):
    x, y, c = _coords()
    me = 4 * x + 2 * y + c
    for a in range(len(ins)):
        pltpu.make_async_copy(ins[a], outs[a].at[me], loc_sems.at[a]).start()
    for a in range(len(ins)):
        _gather_copy(outs, send_sems, recv_sems, a, 0, me, (x, y, 1 - c), src=ins[a]).start()
        for j, (cx, cy) in enumerate([(1 - x, y), (x, 1 - y), (1 - x, 1 - y)]):
            _gather_copy(outs, send_sems, recv_sems, a, 1 + j, me, (cx, cy, c), src=ins[a]).start()


def _gather_finish(ins, outs, send_sems, recv_sems, loc_sems):
    x, y, c = _coords()
    me = 4 * x + 2 * y + c
    sib = (x, y, 1 - c)
    chips = [(1 - x, y), (x, 1 - y), (1 - x, 1 - y)]
    n = len(ins)
    for a in range(n):
        for j, (cx, cy) in enumerate(chips):
            blk = 4 * cx + 2 * cy + c
            _gather_copy(outs, send_sems, recv_sems, a, 1 + j, blk, sib).wait_recv()
            _gather_copy(outs, send_sems, recv_sems, a, 4 + j, blk, sib).start()
    for a in range(n):
        _gather_copy(outs, send_sems, recv_sems, a, 0, 4 * x + 2 * y + (1 - c), sib).wait_recv()
        for j, (cx, cy) in enumerate(chips):
            _gather_copy(outs, send_sems, recv_sems, a, 4 + j, 4 * cx + 2 * cy + (1 - c), sib).wait_recv()
    for a in range(n):
        _gather_copy(outs, send_sems, recv_sems, a, 0, me, sib, src=ins[a]).wait_send()
        for j, (cx, cy) in enumerate(chips):
            _gather_copy(outs, send_sems, recv_sems, a, 1 + j, me, (cx, cy, c), src=ins[a]).wait_send()
            _gather_copy(outs, send_sems, recv_sems, a, 4 + j, 4 * cx + 2 * cy + c, sib).wait_send()
        pltpu.make_async_copy(ins[a], outs[a].at[me], loc_sems.at[a]).wait()


def _gather_shapes(shards):
    return [jax.ShapeDtypeStruct((NDEV,) + s.shape, s.dtype) for s in shards]


def _gather_sems(n):
    return [pltpu.SemaphoreType.DMA((n, 7)), pltpu.SemaphoreType.DMA((n, 7)), pltpu.SemaphoreType.DMA((n,))]


def _sibling_copies(ins, outs, send_sems, recv_sems):
    x, y, c = _coords()
    return [pltpu.make_async_remote_copy(
        src_ref=ins[a].at[2 * q + (1 - c)], dst_ref=outs[a].at[q],
        send_sem=send_sems.at[a, q], recv_sem=recv_sems.at[a, q],
        device_id=(x, y, 1 - c), device_id_type=MESH_IDS) for a in range(len(ins)) for q in range(4)]


def _sibling_shapes(parts):
    return [jax.ShapeDtypeStruct((4,) + p.shape[1:], p.dtype) for p in parts]


def _chips_copies(ins, outs, send_sems, recv_sems):
    x, y, c = _coords()
    copies = []
    for a in range(len(ins)):
        for k, (kx, ky) in enumerate([(1, 0), (0, 1), (1, 1)]):
            tx, ty = _flip(x, kx), _flip(y, ky)
            copies.append(pltpu.make_async_remote_copy(
                src_ref=ins[a].at[2 * tx + ty], dst_ref=outs[a].at[k],
                send_sem=send_sems.at[a, k], recv_sem=recv_sems.at[a, k],
                device_id=(tx, ty, c), device_id_type=MESH_IDS))
    return copies


def _chips_shapes(sums):
    return [jax.ShapeDtypeStruct((3,) + s.shape[1:], s.dtype) for s in sums]


def _direct_copies(ins, outs, send_sems, recv_sems):
    x, y, c = _coords()
    me = 4 * x + 2 * y + c
    copies = []
    for a in range(len(ins)):
        for k in range(1, NDEV):
            kx, ky, kc = (k >> 2) & 1, (k >> 1) & 1, k & 1
            copies.append(pltpu.make_async_remote_copy(
                src_ref=ins[a], dst_ref=outs[a].at[me],
                send_sem=send_sems.at[a, k - 1], recv_sem=recv_sems.at[a, k - 1],
                device_id=(_flip(x, kx), _flip(y, ky), _flip(c, kc)), device_id_type=MESH_IDS))
    return copies


class _Exchange:
    def __init__(self, kind, arrays):
        self.kind, self.arrays, self.n = kind, list(arrays), len(arrays)

    def out_shapes(self):
        return {"gather": _gather_shapes, "direct": _gather_shapes, "sibling": _sibling_shapes,
                "chips": _chips_shapes}[self.kind](self.arrays)

    def sems(self):
        if self.kind in ("gather", "direct"):
            return _gather_sems(self.n)
        k = {"sibling": 4, "chips": 3}[self.kind]
        return [pltpu.SemaphoreType.DMA((self.n, k)), pltpu.SemaphoreType.DMA((self.n, k))]

    def _copies(self, ins, outs, sems):
        if self.kind == "direct":
            x, y, c = _coords()
            own = [pltpu.make_async_copy(ins[a], outs[a].at[4 * x + 2 * y + c], sems[2].at[a]) for a in range(self.n)]
            return own + _direct_copies(ins, outs, sems[0], sems[1])
        return {"sibling": _sibling_copies, "chips": _chips_copies}[self.kind](ins, outs, *sems)

    def start(self, ins, outs, sems):
        if self.kind == "gather":
            _gather_start(ins, outs, *sems)
        else:
            for cpy in self._copies(ins, outs, sems):
                cpy.start()

    def finish(self, ins, outs, sems):
        if self.kind == "gather":
            _gather_finish(ins, outs, *sems)
        else:
            for cpy in self._copies(ins, outs, sems):
                cpy.wait()


def _hosted_call(body, host, name, grid, in_specs, out_specs, out_shape, scratch, sem, vmem_mb, args):
    n = host.n if host else 0
    ni, no, ns = len(in_specs), len(out_specs), len(scratch)

    def full(*refs):
        ins, h_in = refs[:ni], refs[ni:ni + n]
        outs, h_out = refs[ni + n:ni + n + no], refs[ni + n + no:ni + 2 * n + no]
        scr, sems = refs[ni + 2 * n + no:ni + 2 * n + no + ns], refs[ni + 2 * n + no + ns:]
        if host and grid:
            first = functools.reduce(lambda a, b: a & b, [pl.program_id(k) == 0 for k in range(len(grid))])
            last = functools.reduce(lambda a, b: a & b, [pl.program_id(k) == g - 1 for k, g in enumerate(grid)])

            @pl.when(first)
            def _():
                host.start(h_in, h_out, sems)
        elif host:
            host.start(h_in, h_out, sems)

        body(ins, outs, scr)

        if host and grid:
            @pl.when(last)
            def _():
                host.finish(h_in, h_out, sems)
        elif host:
            host.finish(h_in, h_out, sems)

    anyspec = pl.BlockSpec(memory_space=pl.ANY)
    return pl.pallas_call(
        full, name=name, grid=grid,
        in_specs=list(in_specs) + [anyspec] * n, out_specs=list(out_specs) + [anyspec] * n,
        out_shape=list(out_shape) + (host.out_shapes() if host else []),
        scratch_shapes=list(scratch) + (host.sems() if host else []),
        compiler_params=_cp(sem, vmem_mb),
    )(*args, *(host.arrays if host else []))


def _exchange(host, name, after=()):
    n, na = host.n, len(after)

    def body(*refs):
        ins, outs, sems = refs[:n], refs[n + na:2 * n + na], refs[2 * n + na:]
        host.start(ins, outs, sems)
        host.finish(ins, outs, sems)

    anyspec = pl.BlockSpec(memory_space=pl.ANY)
    return pl.pallas_call(
        body, name=name, in_specs=[anyspec] * (n + na), out_specs=[anyspec] * n,
        out_shape=host.out_shapes(), scratch_shapes=host.sems(),
    )(*host.arrays, *after)


def _chips_split_start(sums, name):
    n = len(sums)
    hbm = pl.BlockSpec(memory_space=pltpu.HBM)
    sem = pl.BlockSpec(memory_space=pltpu.SEMAPHORE)

    def body(*refs):
        ins, lands = refs[:n], refs[n:2 * n]
        sems = refs[2 * n:2 * n + 6 * n]
        token = refs[-1]
        for cpy in _chips_copies(ins, lands, _SemGrid(sems[:3 * n], 3), _SemGrid(sems[3 * n:], 3)):
            cpy.start()
        token[...] = jnp.zeros_like(token)

    land_shapes = _chips_shapes(sums)
    lands = [pltpu.with_memory_space_constraint(lax.empty(s.shape, s.dtype), pltpu.HBM) for s in land_shapes]
    return pl.pallas_call(
        body, name=name,
        out_shape=(*[pltpu.SemaphoreType.DMA(())] * (6 * n),
                   *[pltpu.HBM(s.shape, s.dtype) for s in sums],
                   *[pltpu.HBM(s.shape, s.dtype) for s in land_shapes],
                   jax.ShapeDtypeStruct((8, LANES), F32)),
        in_specs=[hbm] * (2 * n),
        out_specs=(*[sem] * (6 * n), *[hbm] * (2 * n), pl.BlockSpec(memory_space=pltpu.VMEM)),
        input_output_aliases={i: 6 * n + i for i in range(2 * n)},
        compiler_params=pltpu.CompilerParams(has_side_effects=pltpu.SideEffectType.DATAFLOW_SIDE_EFFECTING),
    )(*[pltpu.with_memory_space_constraint(s, pltpu.HBM) for s in sums], *lands)


class _SemGrid:
    def __init__(self, sems, k):
        self.sems, self.k = sems, k

    @property
    def at(self):
        return self

    def __getitem__(self, idx):
        return self.sems[idx[0] * self.k + idx[1]]


def _chips_split_wait(started, n, after, name):
    sems = started[:6 * n]
    thru = started[6 * n:8 * n]
    hbm = pl.BlockSpec(memory_space=pltpu.HBM)
    sem = pl.BlockSpec(memory_space=pltpu.SEMAPHORE)

    def body(*refs):
        ins, lands = refs[:n], refs[n:2 * n]
        s = refs[2 * n:2 * n + 6 * n]
        for cpy in _chips_copies(ins, lands, _SemGrid(s[:3 * n], 3), _SemGrid(s[3 * n:], 3)):
            cpy.wait_send()
            cpy.wait_recv()

    outs = pl.pallas_call(
        body, name=name,
        out_shape=tuple(pltpu.HBM(a.shape, a.dtype) for a in thru),
        in_specs=[hbm] * (2 * n) + [sem] * (6 * n) + [pl.BlockSpec(memory_space=pl.ANY)],
        out_specs=tuple([hbm] * (2 * n)),
        input_output_aliases={i: i for i in range(2 * n)},
        compiler_params=pltpu.CompilerParams(has_side_effects=pltpu.SideEffectType.DATAFLOW_SIDE_EFFECTING),
    )(*thru, *sems, after)
    return list(outs[n:])


def _chip_sum(part, recv, sel, tr, name):
    _, _, r, c = part.shape

    def body(sel_ref, p_ref, r_ref, cs_ref, own_ref):
        q = pl.program_id(1)
        s = p_ref[...].astype(F32) + r_ref[...].astype(F32)
        cs_ref[...] = s.astype(BF16)

        @pl.when(q == sel_ref[1])
        def _():
            own_ref[...] = s

    return pl.pallas_call(
        body, name=name,
        grid_spec=pltpu.PrefetchScalarGridSpec(
            num_scalar_prefetch=1, grid=(r // tr, 4),
            in_specs=[pl.BlockSpec((None, None, tr, c), lambda i, q, s: (q, s[0], i, 0)),
                      pl.BlockSpec((None, tr, c), lambda i, q, s: (q, i, 0))],
            out_specs=[pl.BlockSpec((None, tr, c), lambda i, q, s: (q, i, 0)),
                       pl.BlockSpec((tr, c), lambda i, q, s: (i, 0))]),
        out_shape=[jax.ShapeDtypeStruct((4, r, c), BF16), jax.ShapeDtypeStruct((r, c), F32)],
        compiler_params=_cp(("arbitrary", "arbitrary"), 48),
    )(sel, part, recv)


def _gather_direct(src_ref, buf_ref, send_sems, recv_sems):
    x, y, c = _coords()
    me = 4 * x + 2 * y + c
    buf_ref[me] = src_ref[...]
    copies = []
    for k in range(1, NDEV):
        kx, ky, kc = (k >> 2) & 1, (k >> 1) & 1, k & 1
        copies.append(pltpu.make_async_remote_copy(
            src_ref=src_ref, dst_ref=buf_ref.at[me],
            send_sem=send_sems.at[k - 1], recv_sem=recv_sems.at[k - 1],
            device_id=(_flip(x, kx), _flip(y, ky), _flip(c, kc)), device_id_type=MESH_IDS))
    for cpy in copies:
        cpy.start()
    for k in range(1, NDEV):
        kx, ky, kc = (k >> 2) & 1, (k >> 1) & 1, k & 1
        peer = 4 * _flip(x, kx) + 2 * _flip(y, ky) + _flip(c, kc)
        pltpu.make_async_remote_copy(
            src_ref=src_ref, dst_ref=buf_ref.at[peer],
            send_sem=send_sems.at[k - 1], recv_sem=recv_sems.at[k - 1],
            device_id=(x, y, c), device_id_type=MESH_IDS).wait_recv()
    for cpy in copies:
        cpy.wait_send()
    return me


def _mod_exchange(c_row, wmod, bmod, wfmod, bfmod, name, host=None):
    d = c_row.shape[1]
    nm, nf = wmod.shape[1], wfmod.shape[1]
    nw = nm + nf

    def body(ins, outs, scr):
        c_ref, wm_ref, bm_ref, wf_ref, bf_ref = ins
        cs_ref, mod_ref, fmod_ref = outs
        slab_ref, csbuf_ref, mslab_ref, mbuf_ref, s1, r1, s2, r2 = scr
        cv = c_ref[...]
        slab_ref[...] = jnp.broadcast_to(cv * _sigmoid(cv), (8, d))
        _gather_direct(slab_ref, csbuf_ref, s1, r1)
        for b in range(NDEV):
            cs_ref[b:b + 1, :] = csbuf_ref[b, 0:1, :]
        cs = cs_ref[...]
        mslab_ref[:, 0:nm] = jnp.dot(cs, wm_ref[...], precision=HI, preferred_element_type=F32) + bm_ref[...]
        mslab_ref[:, nm:] = jnp.dot(cs, wf_ref[...], precision=HI, preferred_element_type=F32) + bf_ref[...]
        me = _gather_direct(mslab_ref, mbuf_ref, s2, r2)
        mine = lax.broadcasted_iota(jnp.int32, (8, nw), 0) == me
        for k in range(NDEV):
            rowk = jnp.sum(jnp.where(mine, mbuf_ref[k], 0.0), axis=0, keepdims=True)
            mod_ref[k:k + 1, :] = rowk[:, 0:nm]
            fmod_ref[k:k + 1, :] = rowk[:, nm:]

    vm = pl.BlockSpec(memory_space=pltpu.VMEM)
    return _hosted_call(
        body, host, name, (), [vm] * 5, [vm] * 3,
        [jax.ShapeDtypeStruct((NDEV, d), F32), jax.ShapeDtypeStruct((NDEV, nm), F32),
         jax.ShapeDtypeStruct((NDEV, nf), F32)],
        [pltpu.VMEM((8, d), F32), pltpu.VMEM((NDEV, 8, d), F32),
         pltpu.VMEM((8, nw), F32), pltpu.VMEM((NDEV, 8, nw), F32),
         pltpu.SemaphoreType.DMA((7,)), pltpu.SemaphoreType.DMA((7,)),
         pltpu.SemaphoreType.DMA((7,)), pltpu.SemaphoreType.DMA((7,))],
        None, 40, (c_row, wmod, bmod, wfmod, bfmod))


def _table_sum(tabs, name):
    n = len(tabs)

    def body(*refs):
        for a in range(n):
            tot = refs[a][0]
            for k in range(1, NDEV):
                tot = tot + refs[a][k]
            refs[n + a][...] = tot

    vm = pl.BlockSpec(memory_space=pltpu.VMEM)
    return pl.pallas_call(
        body, name=name, in_specs=[vm] * n, out_specs=[vm] * n,
        out_shape=[jax.ShapeDtypeStruct(tb.shape[1:], F32) for tb in tabs],
    )(*tabs)


def _adamw_math(w, g, m, v):
    m = ADAM_B1 * m + (1.0 - ADAM_B1) * g
    v = ADAM_B2 * v + (1.0 - ADAM_B2) * (g * g)
    m_hat = m / (1.0 - ADAM_B1 ** ADAM_STEP)
    v_hat = v / (1.0 - ADAM_B2 ** ADAM_STEP)
    delta = -ADAM_LR * (m_hat / (jnp.sqrt(v_hat) + ADAM_EPS) + ADAM_WD * w)
    return delta, m, v


def _adamw_small(params, name):
    n = len(params)

    def body(*refs):
        for p in range(n):
            w_ref, g_ref, m_ref, v_ref = refs[4 * p:4 * p + 4]
            d_ref, mo_ref, vo_ref = refs[4 * n + 3 * p:4 * n + 3 * p + 3]
            d_ref[...], mo_ref[...], vo_ref[...] = _adamw_math(w_ref[...], g_ref[...], m_ref[...], v_ref[...])

    vm = pl.BlockSpec(memory_space=pltpu.VMEM)
    flat = [a for p in params for a in p]
    outs = pl.pallas_call(
        body, name=name, in_specs=[vm] * (4 * n), out_specs=[vm] * (3 * n),
        out_shape=[jax.ShapeDtypeStruct(p[0].shape, F32) for p in params for _ in range(3)])(*flat)
    return [outs[3 * p:3 * p + 3] for p in range(n)]


def _rs_final(own, recv, w, m, v, tr, name):
    r, c = own.shape

    def body(o_ref, r_ref, w_ref, m_ref, v_ref, g_ref, d_ref, mo_ref, vo_ref):
        g = o_ref[...] + r_ref[0].astype(F32) + r_ref[1].astype(F32) + r_ref[2].astype(F32)
        g_ref[...] = g
        d_ref[...], mo_ref[...], vo_ref[...] = _adamw_math(w_ref[...], g, m_ref[...], v_ref[...])

    tile = pl.BlockSpec((tr, c), lambda i: (i, 0))
    sds = jax.ShapeDtypeStruct((r, c), F32)
    return pl.pallas_call(
        body, name=name, grid=(r // tr,),
        in_specs=[tile, pl.BlockSpec((3, tr, c), lambda i: (0, i, 0)), tile, tile, tile],
        out_specs=[tile] * 4, out_shape=[sds] * 4,
        compiler_params=_cp(("arbitrary",), 48),
    )(own, recv, w, m, v)


def _mod_weight_update(cs, dm, w, m, v, tr, name):
    r, c = w.shape

    def body(cs_ref, dm_ref, w_ref, m_ref, v_ref, g_ref, d_ref, mo_ref, vo_ref):
        g = lax.dot_general(cs_ref[...], dm_ref[...], (((0,), (0,)), ((), ())),
                            precision=HI, preferred_element_type=F32)
        g_ref[...] = g
        d_ref[...], mo_ref[...], vo_ref[...] = _adamw_math(w_ref[...], g, m_ref[...], v_ref[...])

    tile = pl.BlockSpec((tr, c), lambda i: (i, 0))
    sds = jax.ShapeDtypeStruct((r, c), F32)
    return pl.pallas_call(
        body, name=name, grid=(r // tr,),
        in_specs=[pl.BlockSpec((NDEV, tr), lambda i: (0, i)), pl.BlockSpec((NDEV, c), lambda i: (0, 0)),
                  tile, tile, tile],
        out_specs=[tile] * 4, out_shape=[sds] * 4,
        compiler_params=_cp(("arbitrary",), 48),
    )(cs, dm, w, m, v)


def _rows(*vs):
    d = vs[0].shape[-1]
    rows = [v.reshape(1, d) for v in vs]
    return jnp.concatenate(rows + [jnp.zeros((8 - len(rows), d), F32)], axis=0)


def _block_diag_pairs(w):
    hd = w.shape[-1]
    z = jnp.zeros((w.shape[0] // 2, hd, hd), w.dtype)
    top = jnp.concatenate([w[0::2], z], axis=2)
    bot = jnp.concatenate([z, w[1::2]], axis=2)
    return jnp.concatenate([top, bot], axis=1).astype(BF16)


def _diag_pairs(g):
    hd = g.shape[-1] // 2
    both = jnp.stack([g[:, :hd, :hd], g[:, hd:, hd:]], axis=1)
    return both.reshape(2 * g.shape[0], hd, hd)


def kernel(x, c, w_mod, b_mod, g_ffn1, w_ffn1_in, w_ffn1_out, g_mix, w_in, conv_w, conv_b, ln_g, ln_b, rnn_conv_w, rnn_conv_b, w_a, b_a, w_i, b_i, lru_lambda, w_out, g_ffn2, w_ffn2_in, w_ffn2_out, w_fmod, b_fmod, g_final, loss_target, m_w_mod, m_b_mod, m_g_ffn1, m_w_ffn1_in, m_w_ffn1_out, m_g_mix, m_w_in, m_conv_w, m_conv_b, m_ln_g, m_ln_b, m_rnn_conv_w, m_rnn_conv_b, m_w_a, m_b_a, m_w_i, m_b_i, m_lru_lambda, m_w_out, m_g_ffn2, m_w_ffn2_in, m_w_ffn2_out, m_w_fmod, m_b_fmod, m_g_final, v_w_mod, v_b_mod, v_g_ffn1, v_w_ffn1_in, v_w_ffn1_out, v_g_mix, v_w_in, v_conv_w, v_conv_b, v_ln_g, v_ln_b, v_rnn_conv_w, v_rnn_conv_b, v_w_a, v_b_a, v_w_i, v_b_i, v_lru_lambda, v_w_out, v_g_ffn2, v_w_ffn2_in, v_w_ffn2_out, v_w_fmod, v_b_fmod, v_g_final):
    t, d = x.shape[1], x.shape[2]
    fb = w_ffn1_in.shape[2]
    nm = w_mod.shape[2]
    nf = w_fmod.shape[1]
    dc = conv_b.shape[1]
    cl = conv_w.shape[2]
    tm = min(TOKEN_TILE, t)
    tk = min(WGRAD_TILE, t)
    nk = t // tk
    me = 4 * lax.axis_index("x") + 2 * lax.axis_index("y") + lax.axis_index("c")

    tr = jnp.transpose
    bmod_l = lax.dynamic_slice(b_mod, (0, me * nm), (1, nm))
    bfmod_l = lax.dynamic_slice(b_fmod.reshape(1, -1), (0, me * nf), (1, nf))
    cwl = jnp.concatenate([conv_w[0], jnp.zeros((1, cl), F32), rnn_conv_w[0], jnp.zeros((4, cl), F32)], axis=0)
    cs, mod_rows, fmod_rows, wi1, wo1, cwg = _mod_exchange(
        c, w_mod[0], bmod_l, w_fmod, bfmod_l, "mod_and_gather_ffn1",
        host=_Exchange("gather", [tr(w_ffn1_in[0]).astype(BF16), w_ffn1_out[0].astype(BF16), cwl]))
    wi1 = wi1.reshape(2, 4, fb, d)
    wo1 = wo1.reshape(4 * fb, d)
    mod = mod_rows.reshape(9, d)
    fmod = fmod_rows.reshape(2, d)
    vec1 = _rows(g_ffn1, mod[0], mod[1], mod[2])
    vecm = _rows(g_mix, mod[3], mod[4], mod[5])
    vec3 = _rows(g_ffn2, mod[6], mod[7], mod[8])
    vecf = _rows(g_final, fmod[0], fmod[1])

    xin = x[0]
    later = [w_in[0].astype(BF16), w_out[0].astype(BF16), tr(w_ffn2_in[0]).astype(BF16), w_ffn2_out[0].astype(BF16)]
    x1, h1, gu1, f1, win, wout, wi2, wo2 = _ffn_fwd(xin, vec1, wi1, wo1, tm, "ffn1_fwd",
                                                    host=_Exchange("gather", later))
    wi2 = wi2.reshape(2, 4, fb, d)
    wo2 = wo2.reshape(4 * fb, d)
    wout = wout.reshape(d, d)
    h2, proj = _mix_in(x1, vecm, win, tm, "mix_in")
    lnv = _rows(ln_g, ln_b)
    rvec = _rows(rnn_conv_b, b_a, b_i, lru_lambda)
    wab = _block_diag_pairs(w_a[0])
    wib = _block_diag_pairs(w_i[0])
    cwf = jnp.transpose(cwg, (1, 0, 2)).reshape(40, NDEV * cl)
    cw32 = jnp.concatenate([cwf[0:CONV_W], conv_b], axis=0)
    rw8 = cwf[32:40]

    (cv,) = _conv_fwd(proj, cw32, "conv_fwd")
    hr, yr = _rnn_fwd(proj, rw8, rvec, wab, wib, "rnn_fwd")
    x2, ym, ycat = _mix_out(x1, cv, yr, vecm, lnv, wout, tm, "mix_out")
    h3, gu3, dx3, dvf, df3, dva3 = _ffn_fwd_loss(x2, vec3, wi2, wo2, loss_target[0], vecf, tm, "ffn2_fwd_loss")
    a_tok = lambda width: pl.BlockSpec((tk, width), lambda k, s: (s, 0))
    sel = jnp.stack([lax.axis_index("c"), 2 * lax.axis_index("x") + lax.axis_index("y")]).astype(jnp.int32)
    row_tile = {"w_ffn1_in": fb // 4, "w_ffn1_out": fb // 4, "w_in": 512, "w_out": 128,
                "w_ffn2_in": fb // 4, "w_ffn2_out": fb // 4}

    def chip_sums(names, partials, from_sib):
        out = [_chip_sum(p.reshape((4, 2) + p.shape[1:]), r, sel, p.shape[1], "chip_sum_" + nm_)
               for nm_, p, r in zip(names, partials, from_sib)]
        return [o[0] for o in out], [o[1] for o in out]

    dgu3, p_wi2, p_wo2 = _ffn_bwd_w(df3, gu3, h3, wo2, tm, "ffn2_bwd_w")
    p_wi2 = p_wi2.reshape(NDEV, fb, d)
    p_wo2 = p_wo2.reshape(NDEV, fb // 2, d)
    names2 = ["w_ffn2_in", "w_ffn2_out"]
    dx2, dv3, sib_wi2, sib_wo2 = _ffn_bwd_in(dx3, x2, vec3, dgu3, wi2, tm, "ffn2_bwd_in",
                                             host=_Exchange("sibling", [p_wi2, p_wo2]))
    sums2, owns2 = chip_sums(names2, [p_wi2, p_wo2], [sib_wi2, sib_wo2])
    started2 = _chips_split_start(sums2, "rs_chips_ffn2_start")
    dym, dcv, dhr, duy, dln, dgt2 = _mixout_bwd(
        dx2, ym, cv, hr, proj, vecm + started2[-1][0:1, 0:1], lnv, wout, tm, "mixout_bwd")
    dval, dgate, dcw = _conv_bwd(proj, dcv, cw32, "conv_bwd")
    dux, rsm, dwab, dwib = _rnn_bwd(proj, hr, dhr, rw8, rvec, wab, wib, "rnn_bwd")
    parts = [dval, dgate, dux, duy]
    dx1, dvm, df1, dva1 = _mixin_bwd(dx2, x1, parts, vecm, win, f1, vec1, tm, "mixin_bwd")
    p_wout = _mm_tn(ycat, dym, a_tok(d), a_tok(d), 1, nk, d, d, "wgrad_out").reshape(NDEV, d // NDEV, d)
    p_win = _wgrad_in(h2, parts, win.shape[2], min(tk, 1024), "wgrad_in")
    namesm = ["w_in", "w_out"]
    sumsm, ownsm = chip_sums(namesm, [p_win, p_wout],
                             _exchange(_Exchange("sibling", [p_win, p_wout]), "rs_sibling_mix"))
    lane_pad = lambda v: jnp.concatenate([v, jnp.zeros_like(v)], axis=1)
    startedm = _chips_split_start(sumsm, "rs_chips_mix_start")
    early = jnp.concatenate([dva3, dv3, dvf, dvm, dgt2, dva1 + startedm[-1][0:1, 0:1], dcw.reshape(16, d),
                             lane_pad(dln), lane_pad(rsm),
                             _diag_pairs(dwab).reshape(32, d), _diag_pairs(dwib).reshape(32, d)], axis=0)
    dgu1, p_wi1, p_wo1, all_early = _ffn_bwd_w(
        df1, gu1, h1, wo1, tm, "ffn1_bwd_w", host=_Exchange("direct", [early]))
    p_wi1 = p_wi1.reshape(NDEV, fb, d)
    p_wo1 = p_wo1.reshape(NDEV, fb // 2, d)
    names1 = ["w_ffn1_in", "w_ffn1_out"]
    sums1, owns1 = chip_sums(names1, [p_wi1, p_wo1],
                             _exchange(_Exchange("sibling", [p_wi1, p_wo1]), "rs_sibling_ffn1"))
    started = _chips_split_start(sums1, "rs_chips_ffn1_start")
    dx0, dv1 = _ffn_bwd_in(dx1, xin, vec1 + started[-1][0:1, 0:1], dgu1, wi1, tm, "ffn1_bwd_in")
    from_chips = dict(zip(names2, _chips_split_wait(started2, len(sums2), dx0, "rs_chips_ffn2_wait")))
    from_chips.update(zip(namesm, _chips_split_wait(startedm, len(sumsm), dx0, "rs_chips_mix_wait")))
    owns = dict(zip(namesm + names2 + names1, ownsm + owns2 + owns1))

    big = {"w_ffn1_in": (tr(w_ffn1_in[0]), tr(m_w_ffn1_in[0]), tr(v_w_ffn1_in[0])),
           "w_ffn1_out": (w_ffn1_out[0], m_w_ffn1_out[0], v_w_ffn1_out[0]),
           "w_in": (w_in[0], m_w_in[0], v_w_in[0]), "w_out": (w_out[0], m_w_out[0], v_w_out[0]),
           "w_ffn2_in": (tr(w_ffn2_in[0]), tr(m_w_ffn2_in[0]), tr(v_w_ffn2_in[0])),
           "w_ffn2_out": (w_ffn2_out[0], m_w_ffn2_out[0], v_w_ffn2_out[0])}
    res = {}

    def final_sum(nm_):
        out4 = _rs_final(owns[nm_], from_chips[nm_], *big[nm_], row_tile[nm_], "rs_final_" + nm_)
        res[nm_] = [(tr(o) if nm_ in ("w_ffn1_in", "w_ffn2_in") else o)[None] for o in out4]
        return out4[0]

    done = [final_sum(nm_) for nm_ in namesm + names2]
    dfm_all = jnp.concatenate([all_early[:, 17], all_early[:, 19]], axis=1)
    dfm_l = lax.dynamic_slice(dfm_all, (0, me * nf), (NDEV, nf))
    res["w_fmod"] = list(_mod_weight_update(cs, dfm_l, w_fmod, m_w_fmod, v_w_fmod, 256, "w_fmod_update"))
    (all_late,) = _exchange(_Exchange("direct", [dv1]), "late_table", after=done + [res["w_fmod"][0]])
    from_chips["w_ffn1_in"], from_chips["w_ffn1_out"] = _chips_split_wait(
        started, len(sums1), all_late, "rs_chips_ffn1_wait")
    for nm_ in names1:
        final_sum(nm_)
    te, tl = _table_sum([all_early, all_late], "table_sum")
    loss = jnp.sum(te[20])

    mod_rows_of = lambda e, l: [l[1], l[3], e[42], e[25], e[27], e[32], e[9], e[11], e[2]]
    dm_all = jnp.concatenate(mod_rows_of(jnp.swapaxes(all_early, 0, 1), jnp.swapaxes(all_late, 0, 1)), axis=1)
    dm_l = lax.dynamic_slice(dm_all, (0, me * nm), (NDEV, nm))
    res["w_mod"] = [o[None] for o in
                    _mod_weight_update(cs, dm_l, w_mod[0], m_w_mod[0], v_w_mod[0], 256, "w_mod_update")]

    dcw_f = te[48:64].reshape(32, dc)
    rsm_f = te[72:80, 0:dc]
    small_grads = {
        "b_mod": jnp.concatenate(mod_rows_of(te, tl)).reshape(1, 9 * d),
        "b_fmod": jnp.concatenate([te[17], te[19]]),
        "g_ffn1": tl[0:1], "g_mix": te[24:25], "g_ffn2": te[8:9], "g_final": te[16],
        "conv_w": lax.dynamic_slice(dcw_f, (0, me * cl), (CONV_W, cl))[None],
        "conv_b": dcw_f[31:32],
        "ln_g": te[64:65, 0:dc], "ln_b": te[65:66, 0:dc],
        "rnn_conv_w": lax.dynamic_slice(rsm_f, (0, me * cl), (RNN_CONV_W, cl))[None],
        "rnn_conv_b": rsm_f[4:5], "b_a": rsm_f[5:6], "b_i": rsm_f[6:7], "lru_lambda": rsm_f[7:8],
        "w_a": te[80:112].reshape(w_a.shape), "w_i": te[112:144].reshape(w_i.shape),
    }
    small_params = {
        "b_mod": (b_mod, m_b_mod, v_b_mod), "b_fmod": (b_fmod, m_b_fmod, v_b_fmod),
        "g_ffn1": (g_ffn1, m_g_ffn1, v_g_ffn1), "g_mix": (g_mix, m_g_mix, v_g_mix),
        "g_ffn2": (g_ffn2, m_g_ffn2, v_g_ffn2), "g_final": (g_final, m_g_final, v_g_final),
        "conv_w": (conv_w, m_conv_w, v_conv_w), "conv_b": (conv_b, m_conv_b, v_conv_b),
        "ln_g": (ln_g, m_ln_g, v_ln_g), "ln_b": (ln_b, m_ln_b, v_ln_b),
        "rnn_conv_w": (rnn_conv_w, m_rnn_conv_w, v_rnn_conv_w),
        "rnn_conv_b": (rnn_conv_b, m_rnn_conv_b, v_rnn_conv_b),
        "w_a": (w_a, m_w_a, v_w_a), "b_a": (b_a, m_b_a, v_b_a),
        "w_i": (w_i, m_w_i, v_w_i), "b_i": (b_i, m_b_i, v_b_i),
        "lru_lambda": (lru_lambda, m_lru_lambda, v_lru_lambda),
    }
    two_d = lambda w: (-1, w.shape[-1]) if w.ndim > 1 else (1, w.shape[0])
    small_names = list(small_grads)
    small_outs = _adamw_small(
        [(w.reshape(two_d(w)), small_grads[nm_].reshape(two_d(w)), m.reshape(two_d(w)), v.reshape(two_d(w)))
         for nm_ in small_names for (w, m, v) in [small_params[nm_]]], "adamw_small")
    for nm_, outs in zip(small_names, small_outs):
        shp = small_params[nm_][0].shape
        res[nm_] = [small_grads[nm_].reshape(shp)] + [o.reshape(shp) for o in outs]

    order = ["w_mod", "b_mod", "g_ffn1", "w_ffn1_in", "w_ffn1_out", "g_mix", "w_in", "conv_w", "conv_b",
             "ln_g", "ln_b", "rnn_conv_w", "rnn_conv_b", "w_a", "b_a", "w_i", "b_i", "lru_lambda", "w_out",
             "g_ffn2", "w_ffn2_in", "w_ffn2_out", "w_fmod", "b_fmod", "g_final"]
    return (loss, dx0[None], *[res[n][0] for n in order], *[res[n][1] for n in order],
            *[res[n][2] for n in order], *[res[n][3] for n in order])
```

```python
import functools
import math

import jax
import jax.numpy as jnp
from jax import lax
from jax.experimental import pallas as pl
from jax.experimental.pallas import tpu as pltpu

F32 = jnp.float32
BF16 = jnp.bfloat16
MESH_IDS = pl.DeviceIdType.MESH
NDEV = 8
EPS = 1e-6
RG_C = 8.0
CONV_W = 31
RNN_CONV_W = 4
LANES = 128
ADAM_LR = 0.001
ADAM_B1 = 0.9
ADAM_B2 = 0.999
ADAM_EPS = 1e-08
ADAM_WD = 0.01
ADAM_STEP = 10
TOKEN_TILE = 512
WGRAD_TILE = 2048
ROW_GROUP = 16
HI = lax.Precision.HIGHEST


def _cp(sem, vmem_mb):
    return pltpu.CompilerParams(dimension_semantics=sem, vmem_limit_bytes=vmem_mb * 1024 * 1024)


def _dot(a, b):
    return jnp.dot(a, b, preferred_element_type=F32)


def _dot_nt(a, b):
    return lax.dot_general(a, b, (((1,), (1,)), ((), ())), preferred_element_type=F32)


def _dot_tn(a, b):
    return lax.dot_general(a, b, (((0,), (0,)), ((), ())), preferred_element_type=F32)


def _sigmoid(x):
    return 1.0 / (1.0 + jnp.exp(-x))


def _adaln(x, vec_ref):
    rstd = lax.rsqrt(jnp.mean(x * x, axis=-1, keepdims=True) + EPS)
    return (x * rstd) * vec_ref[0:1, :] * (1.0 + vec_ref[2:3, :]) + vec_ref[1:2, :]


def _adaln_bwd(x, dh, vec_ref, dvec_ref):
    rstd = lax.rsqrt(jnp.mean(x * x, axis=-1, keepdims=True) + EPS)
    xhat = x * rstd
    dvec_ref[0:1, :] += jnp.sum(dh * xhat, axis=0, keepdims=True)
    dvec_ref[1:2, :] += jnp.sum(dh, axis=0, keepdims=True)
    dxhat = dh * (vec_ref[0:1, :] * (1.0 + vec_ref[2:3, :]))
    return rstd * (dxhat - xhat * jnp.mean(dxhat * xhat, axis=-1, keepdims=True))


def _adaln_finish(vec_ref, dvec_ref):
    s = dvec_ref[0:1, :]
    dvec_ref[3:4, :] = vec_ref[0:1, :] * s
    dvec_ref[0:1, :] = (1.0 + vec_ref[2:3, :]) * s


def _gelu_and_grad(x):
    k0 = math.sqrt(2.0 / math.pi)
    x2 = x * x
    t = jnp.tanh(k0 * (x + 0.044715 * x * x2))
    g = 0.5 * x * (1.0 + t)
    dg = 0.5 * (1.0 + t) + 0.5 * x * (1.0 - t * t) * (k0 * (1.0 + 3.0 * 0.044715 * x2))
    return g, dg


def _log_sigmoid(x):
    z = jnp.exp(-jnp.abs(x))
    u = 1.0 + z
    d = u - 1.0
    log1p = jnp.where(d == 0.0, z, jnp.log(u) * (z / jnp.where(d == 0.0, 1.0, d)))
    return jnp.minimum(x, 0.0) - log1p


def _neg_expm1(x):
    series = -x * (1.0 + x * (0.5 + x * (1.0 / 6.0 + x * (1.0 / 24.0 + x * (1.0 / 120.0)))))
    return jnp.where(x > -0.05, series, 1.0 - jnp.exp(x))


SUBLANES = 8


def _doubling_scan(a, b, reverse):
    n = a.shape[0]
    row = lax.broadcasted_iota(jnp.int32, a.shape, 0)
    s = 1
    while s < n:
        ok = (row < n - s) if reverse else (row >= s)
        shift = n - s if reverse else s
        b = a * jnp.where(ok, pltpu.roll(b, shift, 0), 0.0) + b
        if 2 * s < n:
            a = a * jnp.where(ok, pltpu.roll(a, shift, 0), 1.0)
        s *= 2
    return b


def _tiled_scan(a, b, reverse, sa_ref, sb_ref, carry_ref, out_ref):
    n = a.shape[0]
    nt8 = n // SUBLANES
    sub = lax.broadcasted_iota(jnp.int32, a.shape, 0) % SUBLANES
    for s in (1, 2, 4):
        ok = (sub < SUBLANES - s) if reverse else (sub >= s)
        shift = n - s if reverse else s
        b = a * jnp.where(ok, pltpu.roll(b, shift, 0), 0.0) + b
        a = a * jnp.where(ok, pltpu.roll(a, shift, 0), 1.0)
    sa_ref[...] = a
    sb_ref[...] = b
    edge = 0 if reverse else SUBLANES - 1
    at = sa_ref[pl.ds(edge, nt8, stride=SUBLANES), :]
    bt = sb_ref[pl.ds(edge, nt8, stride=SUBLANES), :]
    xt = _doubling_scan(at, bt, reverse)
    rowt = lax.broadcasted_iota(jnp.int32, xt.shape, 0)
    if reverse:
        carry_ref[...] = jnp.where(rowt < nt8 - 1, pltpu.roll(xt, nt8 - 1, 0), 0.0)
    else:
        carry_ref[...] = jnp.where(rowt >= 1, pltpu.roll(xt, 1, 0), 0.0)
    for r in range(nt8):
        rows = slice(r * SUBLANES, (r + 1) * SUBLANES)
        out_ref[rows, :] = sa_ref[rows, :] * carry_ref[r:r + 1, :] + sb_ref[rows, :]


def _rglru_gates(xr, wa_ref, wi_ref, rvec_ref):
    xb = xr.astype(BF16)
    r = _sigmoid(_dot(xb, wa_ref[...]) + rvec_ref[1:2, :])
    ig = _sigmoid(_dot(xb, wi_ref[...]) + rvec_ref[2:3, :])
    ls = _log_sigmoid(rvec_ref[3:4, :])
    log_a = RG_C * r * ls
    a = jnp.exp(log_a)
    mult = jnp.sqrt(_neg_expm1(2.0 * log_a))
    return xb, r, ig, ls, a, mult


def _rnn_conv(ux, rw_ref, rvec_ref, ext_ref):
    t = ux.shape[0]
    ext_ref[0:8, :] = jnp.zeros((8, ux.shape[1]), F32)
    ext_ref[8:, :] = ux
    xr = rvec_ref[0:1, :] + rw_ref[RNN_CONV_W - 1:RNN_CONV_W, :] * ux
    for k in range(RNN_CONV_W - 1):
        d = RNN_CONV_W - 1 - k
        xr = xr + rw_ref[k:k + 1, :] * ext_ref[8 - d:8 - d + t, :]
    return xr


def _ffn_fwd(x, vec, wi, wo, tm, name, host=None):
    t, d = x.shape
    nj, fb = wi.shape[1], wi.shape[2]
    nt = t // tm

    def body(ins, outs, scr):
        x_ref, vec_ref, wi_ref, wo_ref = ins
        xo_ref, h_ref, gu_ref, f_ref = outs
        acc_ref, = scr
        j = pl.program_id(1)

        @pl.when(j == 0)
        def _():
            h_ref[...] = _adaln(x_ref[...], vec_ref).astype(BF16)
            acc_ref[...] = jnp.zeros_like(acc_ref)

        h = h_ref[...]
        gate = _dot_nt(h, wi_ref[0])
        up = _dot_nt(h, wi_ref[1])
        gu_ref[0] = gate.astype(BF16)
        gu_ref[1] = up.astype(BF16)
        act = (gate * _sigmoid(gate) * up).astype(BF16)
        acc_ref[...] += _dot(act, wo_ref[...])

        @pl.when(j == nj - 1)
        def _():
            f = acc_ref[...]
            f_ref[...] = f.astype(BF16)
            xo_ref[...] = x_ref[...] + 0.5 * vec_ref[3:4, :] * f

    tile = pl.BlockSpec((tm, d), lambda i, j: (i, 0))
    return _hosted_call(
        body, host, name, (nt, nj),
        [tile,
         pl.BlockSpec((8, d), lambda i, j: (0, 0)),
         pl.BlockSpec((2, None, fb, d), lambda i, j: (0, j, 0, 0)),
         pl.BlockSpec((fb, d), lambda i, j: (j, 0))],
        [tile, tile, pl.BlockSpec((2, None, tm, fb), lambda i, j: (0, j, i, 0)), tile],
        [jax.ShapeDtypeStruct((t, d), F32), jax.ShapeDtypeStruct((t, d), BF16),
         jax.ShapeDtypeStruct((2, nj, t, fb), BF16), jax.ShapeDtypeStruct((t, d), BF16)],
        [pltpu.VMEM((tm, d), F32)], ("arbitrary", "arbitrary"), 48, (x, vec, wi, wo))


def _ffn_fwd_loss(x, vec, wi, wo, tgt, fvec, tm, name):
    t, d = x.shape
    nj, fb = wi.shape[1], wi.shape[2]
    nt = t // tm

    def body(x_ref, vec_ref, wi_ref, wo_ref, t_ref, fvec_ref, h_ref, gu_ref, dx_ref, dvec_ref, df_ref, dgt_ref,
             acc_ref):
        i = pl.program_id(0)
        j = pl.program_id(1)

        @pl.when((i == 0) & (j == 0))
        def _():
            dvec_ref[...] = jnp.zeros_like(dvec_ref)
            dgt_ref[...] = jnp.zeros_like(dgt_ref)

        @pl.when(j == 0)
        def _():
            h_ref[...] = _adaln(x_ref[...], vec_ref).astype(BF16)
            acc_ref[...] = jnp.zeros_like(acc_ref)

        h = h_ref[...]
        gate = _dot_nt(h, wi_ref[0])
        up = _dot_nt(h, wi_ref[1])
        gu_ref[0] = gate.astype(BF16)
        gu_ref[1] = up.astype(BF16)
        act = (gate * _sigmoid(gate) * up).astype(BF16)
        acc_ref[...] += _dot(act, wo_ref[...])

        @pl.when(j == nj - 1)
        def _():
            f = acc_ref[...]
            xo = x_ref[...] + 0.5 * vec_ref[3:4, :] * f
            e = _adaln(xo, fvec_ref) - t_ref[...]
            dvec_ref[4:5, :] += (0.5 / d) * jnp.sum(e * e, axis=0, keepdims=True)
            dx = _adaln_bwd(xo, e * (1.0 / d), fvec_ref, dvec_ref)
            dx_ref[...] = dx
            df_ref[...] = (0.5 * vec_ref[3:4, :] * dx).astype(BF16)
            dgt_ref[2:3, :] += 0.5 * jnp.sum(dx * f, axis=0, keepdims=True)

        @pl.when((i == nt - 1) & (j == nj - 1))
        def _():
            _adaln_finish(fvec_ref, dvec_ref)

    tile = pl.BlockSpec((tm, d), lambda i, j: (i, 0))
    tab = pl.BlockSpec((8, d), lambda i, j: (0, 0))
    return pl.pallas_call(
        body, name=name, grid=(nt, nj),
        in_specs=[tile, tab,
                  pl.BlockSpec((2, None, fb, d), lambda i, j: (0, j, 0, 0)),
                  pl.BlockSpec((fb, d), lambda i, j: (j, 0)),
                  pl.BlockSpec((tm, d), lambda i, j: (jnp.where(j == nj - 1, i, jnp.maximum(i - 1, 0)), 0)), tab],
        out_specs=[tile, pl.BlockSpec((2, None, tm, fb), lambda i, j: (0, j, i, 0)), tile, tab, tile, tab],
        out_shape=[jax.ShapeDtypeStruct((t, d), BF16), jax.ShapeDtypeStruct((2, nj, t, fb), BF16),
                   jax.ShapeDtypeStruct((t, d), F32), jax.ShapeDtypeStruct((8, d), F32),
                   jax.ShapeDtypeStruct((t, d), BF16), jax.ShapeDtypeStruct((8, d), F32)],
        scratch_shapes=[pltpu.VMEM((tm, d), F32)],
        compiler_params=_cp(("arbitrary", "arbitrary"), 56),
    )(x, vec, wi, wo, tgt, fvec)


def _mix_in(x, vec, win, tm, name, host=None):
    t, d = x.shape
    nb, _, cb = win.shape

    def body(ins, outs, scr):
        x_ref, vec_ref, w_ref = ins
        h_ref, p_ref = outs
        h = _adaln(x_ref[...], vec_ref).astype(BF16)
        h_ref[...] = h
        for k in range(nb):
            p_ref[:, k * cb:(k + 1) * cb] = _dot(h, w_ref[k])

    return _hosted_call(
        body, host, name, (t // tm,),
        [pl.BlockSpec((tm, d), lambda i: (i, 0)),
         pl.BlockSpec((8, d), lambda i: (0, 0)),
         pl.BlockSpec((nb, d, cb), lambda i: (0, 0, 0))],
        [pl.BlockSpec((tm, d), lambda i: (i, 0)),
         pl.BlockSpec((tm, nb * cb), lambda i: (i, 0))],
        [jax.ShapeDtypeStruct((t, d), BF16), jax.ShapeDtypeStruct((t, nb * cb), F32)],
        [], ("arbitrary",), 48, (x, vec, win))


def _conv_fwd(proj, cw32, name, host=None):
    t = proj.shape[0]
    nblk = cw32.shape[1] // LANES
    ch = min(t, 128)

    def body(ins, outs, scr):
        val_ref, gate_ref, cw_ref = ins
        cv_ref, = outs
        ext_ref, = scr
        ext_ref[0:32, :] = jnp.zeros((32, LANES), F32)
        ext_ref[32:, :] = val_ref[...] * _sigmoid(gate_ref[...])
        for r in range(t // ch):
            acc = jnp.broadcast_to(cw_ref[31:32, :], (ch, LANES))
            for k in range(CONV_W):
                off = 32 + r * ch - (CONV_W - 1 - k)
                acc = acc + cw_ref[k:k + 1, :] * ext_ref[off:off + ch, :]
            cv_ref[r * ch:(r + 1) * ch, :] = acc

    return _hosted_call(
        body, host, name, (nblk,),
        [pl.BlockSpec((t, LANES), lambda c: (0, c)),
         pl.BlockSpec((t, LANES), lambda c: (0, nblk + c)),
         pl.BlockSpec((32, LANES), lambda c: (0, c))],
        [pl.BlockSpec((t, LANES), lambda c: (0, c))],
        [jax.ShapeDtypeStruct((t, nblk * LANES), F32)],
        [pltpu.VMEM((t + 32, LANES), F32)], ("arbitrary",), 48, (proj, proj, cw32))


def _rnn_fwd(proj, rw8, rvec, wab, wib, name, host=None):
    t = proj.shape[0]
    nblk = rvec.shape[1] // LANES

    def body(ins, outs, scr):
        ux_ref, uy_ref, rw_ref, rvec_ref, wa_ref, wi_ref = ins
        h_ref, yr_ref = outs
        ext_ref, sa_ref, sb_ref, carry_ref = scr
        xr = _rnn_conv(ux_ref[...], rw_ref, rvec_ref, ext_ref)
        _, _, ig, _, a, mult = _rglru_gates(xr, wa_ref, wi_ref, rvec_ref)
        _tiled_scan(a, mult * (ig * xr), False, sa_ref, sb_ref, carry_ref, h_ref)
        ge, _ = _gelu_and_grad(uy_ref[...])
        yr_ref[...] = (ge * h_ref[...]).astype(BF16)

    blk = lambda off: pl.BlockSpec((t, LANES), lambda c: (0, off + c))
    return _hosted_call(
        body, host, name, (nblk,),
        [blk(2 * nblk), blk(3 * nblk),
         pl.BlockSpec((8, LANES), lambda c: (0, c)),
         pl.BlockSpec((8, LANES), lambda c: (0, c)),
         pl.BlockSpec((None, LANES, LANES), lambda c: (c, 0, 0)),
         pl.BlockSpec((None, LANES, LANES), lambda c: (c, 0, 0))],
        [blk(0), blk(0)],
        [jax.ShapeDtypeStruct((t, nblk * LANES), F32), jax.ShapeDtypeStruct((t, nblk * LANES), BF16)],
        [pltpu.VMEM((t + 8, LANES), F32), pltpu.VMEM((t, LANES), F32), pltpu.VMEM((t, LANES), F32),
         pltpu.VMEM((t // SUBLANES, LANES), F32)], ("arbitrary",), 56, (proj, proj, rw8, rvec, wab, wib))


def _ln_silu(cv, lnv_ref):
    mu = jnp.mean(cv, axis=-1, keepdims=True)
    xc = cv - mu
    rs = lax.rsqrt(jnp.mean(xc * xc, axis=-1, keepdims=True) + EPS)
    chat = xc * rs
    z = chat * lnv_ref[0:1, :] + lnv_ref[1:2, :]
    sg = _sigmoid(z)
    return rs, chat, z, sg


def _mix_out(x, cv, yr, vec, lnv, wout, tm, name, host=None):
    t, d = x.shape
    dc = cv.shape[1]

    def body(ins, outs, scr):
        x_ref, cv_ref, yr_ref, vec_ref, lnv_ref, w_ref = ins
        xo_ref, ym_ref, yc_ref = outs
        _, _, z, sg = _ln_silu(cv_ref[...], lnv_ref)
        yc = (z * sg).astype(BF16)
        yr = yr_ref[...]
        yc_ref[:, 0:dc] = yc
        yc_ref[:, dc:] = yr
        ym = _dot(yc, w_ref[0:dc, :]) + _dot(yr, w_ref[dc:, :])
        ym_ref[...] = ym.astype(BF16)
        xo_ref[...] = x_ref[...] + vec_ref[3:4, :] * ym

    tile = pl.BlockSpec((tm, d), lambda i: (i, 0))
    return _hosted_call(
        body, host, name, (t // tm,),
        [tile,
         pl.BlockSpec((tm, dc), lambda i: (i, 0)),
         pl.BlockSpec((tm, dc), lambda i: (i, 0)),
         pl.BlockSpec((8, d), lambda i: (0, 0)),
         pl.BlockSpec((8, dc), lambda i: (0, 0)),
         pl.BlockSpec((d, d), lambda i: (0, 0))],
        [tile, tile, tile],
        [jax.ShapeDtypeStruct((t, d), F32), jax.ShapeDtypeStruct((t, d), BF16), jax.ShapeDtypeStruct((t, d), BF16)],
        [], ("arbitrary",), 48, (x, cv, yr, vec, lnv, wout))


def _emit_df(dx, f_ref, nvec_ref, df_ref, dgt_ref):
    df_ref[...] = (0.5 * nvec_ref[3:4, :] * dx).astype(BF16)
    dgt_ref[2:3, :] += 0.5 * jnp.sum(dx * f_ref[...].astype(F32), axis=0, keepdims=True)


def _ffn_bwd_w(df, gu, h, wo, tm, name, host=None):
    t, d = df.shape
    nj, fb = gu.shape[1], gu.shape[3]
    nt = t // tm
    sub = min(tm, ROW_GROUP)

    def body(ins, outs, scr):
        df_ref, gu_ref, h_ref, wo_ref = ins
        dgu_ref, dwi_ref, dwo_ref = outs
        accg_ref, accu_ref, acco_ref, dact_ref, act_ref = scr
        i = pl.program_id(1)

        @pl.when(i == 0)
        def _():
            accg_ref[...] = jnp.zeros_like(accg_ref)
            accu_ref[...] = jnp.zeros_like(accu_ref)
            acco_ref[...] = jnp.zeros_like(acco_ref)

        dact_ref[...] = _dot_nt(df_ref[...], wo_ref[...])
        for r in range(tm // sub):
            rows = slice(r * sub, (r + 1) * sub)
            g = gu_ref[0, rows, :].astype(F32)
            u = gu_ref[1, rows, :].astype(F32)
            dact = dact_ref[rows, :]
            sg = _sigmoid(g)
            sl = g * sg
            dgu_ref[0, rows, :] = (dact * u * (sg * (1.0 + g * (1.0 - sg)))).astype(BF16)
            dgu_ref[1, rows, :] = (dact * sl).astype(BF16)
            act_ref[rows, :] = (sl * u).astype(BF16)
        hb = h_ref[...]
        acco_ref[...] += _dot_tn(act_ref[...], df_ref[...])
        accg_ref[...] += _dot_tn(dgu_ref[0], hb)
        accu_ref[...] += _dot_tn(dgu_ref[1], hb)

        @pl.when(i == nt - 1)
        def _():
            dwi_ref[0] = accg_ref[...].astype(BF16)
            dwi_ref[1] = accu_ref[...].astype(BF16)
            dwo_ref[...] = acco_ref[...].astype(BF16)

    tile = pl.BlockSpec((tm, d), lambda j, i: (i, 0))
    return _hosted_call(
        body, host, name, (nj, nt),
        [tile,
         pl.BlockSpec((2, None, tm, fb), lambda j, i: (0, j, i, 0)),
         tile,
         pl.BlockSpec((fb, d), lambda j, i: (j, 0))],
        [pl.BlockSpec((2, None, tm, fb), lambda j, i: (0, j, i, 0)),
         pl.BlockSpec((2, None, fb, d), lambda j, i: (0, j, 0, 0)),
         pl.BlockSpec((None, fb, d), lambda j, i: (j, 0, 0))],
        [jax.ShapeDtypeStruct((2, nj, t, fb), BF16), jax.ShapeDtypeStruct((2, nj, fb, d), BF16),
         jax.ShapeDtypeStruct((nj, fb, d), BF16)],
        [pltpu.VMEM((fb, d), F32), pltpu.VMEM((fb, d), F32), pltpu.VMEM((fb, d), F32),
         pltpu.VMEM((tm, fb), F32), pltpu.VMEM((tm, fb), BF16)],
        ("arbitrary", "arbitrary"), 56, (df, gu, h, wo))


def _ffn_bwd_in(dxo, x, vec, dgu, wi, tm, name, host=None):
    t, d = x.shape
    nj, fb = wi.shape[1], wi.shape[2]
    nt = t // tm

    def body(ins, outs, scr):
        dxo_ref, x_ref, vec_ref, dgu_ref, wi_ref = ins
        dx_ref, dvec_ref = outs
        i = pl.program_id(0)

        @pl.when(i == 0)
        def _():
            dvec_ref[...] = jnp.zeros_like(dvec_ref)

        dh = jnp.zeros((tm, d), F32)
        for a in range(2):
            for k in range(nj):
                dh = dh + _dot(dgu_ref[a, k], wi_ref[a, k])
        dx_ref[...] = dxo_ref[...] + _adaln_bwd(x_ref[...], dh, vec_ref, dvec_ref)

        @pl.when(i == nt - 1)
        def _():
            _adaln_finish(vec_ref, dvec_ref)

    tile = pl.BlockSpec((tm, d), lambda i: (i, 0))
    return _hosted_call(
        body, host, name, (nt,),
        [tile, tile,
         pl.BlockSpec((8, d), lambda i: (0, 0)),
         pl.BlockSpec((2, nj, tm, fb), lambda i: (0, 0, i, 0)),
         pl.BlockSpec((2, nj, fb, d), lambda i: (0, 0, 0, 0))],
        [tile, pl.BlockSpec((8, d), lambda i: (0, 0))],
        [jax.ShapeDtypeStruct((t, d), F32), jax.ShapeDtypeStruct((8, d), F32)],
        [], ("arbitrary",), 60, (dxo, x, vec, dgu, wi))


def _mm_tn(a, b, a_spec, b_spec, nblk, nk, m, n, name):
    def body(a_ref, b_ref, o_ref, acc_ref):
        s = pl.program_id(1)

        @pl.when(s == 0)
        def _():
            acc_ref[...] = jnp.zeros_like(acc_ref)

        acc_ref[...] += _dot_tn(a_ref[...], b_ref[...])

        @pl.when(s == nk - 1)
        def _():
            o_ref[...] = acc_ref[...].astype(BF16)

    return pl.pallas_call(
        body, name=name, grid=(nblk, nk),
        in_specs=[a_spec, b_spec],
        out_specs=pl.BlockSpec((None, m, n), lambda k, s: (k, 0, 0)),
        out_shape=jax.ShapeDtypeStruct((nblk, m, n), BF16),
        scratch_shapes=[pltpu.VMEM((m, n), F32)],
        compiler_params=_cp(("arbitrary", "arbitrary"), 56),
    )(a, b)


def _wgrad_in(h, parts, cb, tk, name):
    t, d = h.shape
    dc = parts[0].shape[1]
    per = dc // cb
    nblk = len(parts) * per
    nk = t // tk

    def body(h_ref, p0, p1, p2, p3, o_ref, acc_ref):
        s = pl.program_id(0)

        @pl.when(s == 0)
        def _():
            acc_ref[...] = jnp.zeros_like(acc_ref)

        hb = h_ref[...]
        for p, p_ref in enumerate((p0, p1, p2, p3)):
            acc_ref[p] += _dot_tn(hb, p_ref[...])

        @pl.when(s == nk - 1)
        def _():
            for k in range(nblk):
                o_ref[k] = acc_ref[k // per, :, (k % per) * cb:(k % per + 1) * cb].astype(BF16)

    return pl.pallas_call(
        body, name=name, grid=(nk,),
        in_specs=[pl.BlockSpec((tk, d), lambda s: (s, 0))] + [pl.BlockSpec((tk, dc), lambda s: (s, 0))] * len(parts),
        out_specs=pl.BlockSpec((nblk, d, cb), lambda s: (0, 0, 0)),
        out_shape=jax.ShapeDtypeStruct((nblk, d, cb), BF16),
        scratch_shapes=[pltpu.VMEM((len(parts), d, dc), F32)],
        compiler_params=_cp(("arbitrary",), 56),
    )(h, *parts)


def _mixout_bwd(dxo, ym, cv, hr, proj, vec, lnv, wout, tm, name, host=None):
    t, d = dxo.shape
    dc = cv.shape[1]
    nt = t // tm

    def body(ins, outs, scr):
        dxo_ref, ym_ref, cv_ref, hr_ref, uy_ref, vec_ref, lnv_ref, w_ref = ins
        dym_ref, dcv_ref, dhr_ref, duy_ref, dln_ref, dgt_ref = outs
        i = pl.program_id(0)

        @pl.when(i == 0)
        def _():
            dln_ref[...] = jnp.zeros_like(dln_ref)
            dgt_ref[...] = jnp.zeros_like(dgt_ref)

        dxo_v = dxo_ref[...]
        dym = (vec_ref[3:4, :] * dxo_v).astype(BF16)
        dym_ref[...] = dym
        dgt_ref[0:1, :] += jnp.sum(dxo_v * ym_ref[...].astype(F32), axis=0, keepdims=True)
        dyc = _dot_nt(dym, w_ref[0:dc, :])
        dyr = _dot_nt(dym, w_ref[dc:, :])
        rs, chat, z, sg = _ln_silu(cv_ref[...], lnv_ref)
        dz = dyc * (sg * (1.0 + z * (1.0 - sg)))
        dln_ref[0:1, :] += jnp.sum(dz * chat, axis=0, keepdims=True)
        dln_ref[1:2, :] += jnp.sum(dz, axis=0, keepdims=True)
        dchat = dz * lnv_ref[0:1, :]
        dcv_ref[...] = (rs * (dchat - jnp.mean(dchat, axis=-1, keepdims=True)
                              - chat * jnp.mean(dchat * chat, axis=-1, keepdims=True))).astype(BF16)
        ge, dge = _gelu_and_grad(uy_ref[...])
        dhr_ref[...] = (dyr * ge).astype(BF16)
        duy_ref[...] = (dyr * hr_ref[...] * dge).astype(BF16)

    tile_d = pl.BlockSpec((tm, d), lambda i: (i, 0))
    tile_c = pl.BlockSpec((tm, dc), lambda i: (i, 0))
    return _hosted_call(
        body, host, name, (nt,),
        [tile_d, tile_d, tile_c, tile_c,
         pl.BlockSpec((tm, dc), lambda i: (i, 3)),
         pl.BlockSpec((8, d), lambda i: (0, 0)),
         pl.BlockSpec((8, dc), lambda i: (0, 0)),
         pl.BlockSpec((d, d), lambda i: (0, 0))],
        [tile_d, tile_c, tile_c, tile_c,
         pl.BlockSpec((8, dc), lambda i: (0, 0)),
         pl.BlockSpec((8, d), lambda i: (0, 0))],
        [jax.ShapeDtypeStruct((t, d), BF16), jax.ShapeDtypeStruct((t, dc), BF16),
         jax.ShapeDtypeStruct((t, dc), BF16), jax.ShapeDtypeStruct((t, dc), BF16),
         jax.ShapeDtypeStruct((8, dc), F32), jax.ShapeDtypeStruct((8, d), F32)],
        [], ("arbitrary",), 48, (dxo, ym, cv, hr, proj, vec, lnv, wout))


def _conv_bwd(proj, dcv, cw32, name):
    t = proj.shape[0]
    nblk = cw32.shape[1] // LANES
    ch = min(t, 128)

    def body(val_ref, gate_ref, dcv_ref, cw_ref, dval_ref, dgate_ref, dcw_ref, extu_ref, extd_ref):
        val = val_ref[...]
        sg = _sigmoid(gate_ref[...])
        extu_ref[0:32, :] = jnp.zeros((32, LANES), F32)
        extu_ref[32:, :] = val * sg
        dcv_v = dcv_ref[...].astype(F32)
        extd_ref[0:t, :] = dcv_v
        extd_ref[t:, :] = jnp.zeros((32, LANES), F32)
        for r in range(t // ch):
            acc = jnp.zeros((ch, LANES), F32)
            for k in range(CONV_W):
                off = r * ch + (CONV_W - 1 - k)
                acc = acc + cw_ref[k:k + 1, :] * extd_ref[off:off + ch, :]
            rows = slice(r * ch, (r + 1) * ch)
            sg_r = _sigmoid(gate_ref[rows, :])
            dval_ref[rows, :] = (acc * sg_r).astype(BF16)
            dgate_ref[rows, :] = (acc * val_ref[rows, :] * sg_r * (1.0 - sg_r)).astype(BF16)
        for k in range(CONV_W):
            off = 32 - (CONV_W - 1 - k)
            dcw_ref[k:k + 1, :] = jnp.sum(dcv_v * extu_ref[off:off + t, :], axis=0, keepdims=True)
        dcw_ref[31:32, :] = jnp.sum(dcv_v, axis=0, keepdims=True)

    blk = lambda off: pl.BlockSpec((t, LANES), lambda c: (0, off + c))
    return pl.pallas_call(
        body, name=name, grid=(nblk,),
        in_specs=[blk(0), blk(nblk), blk(0), pl.BlockSpec((32, LANES), lambda c: (0, c))],
        out_specs=[blk(0), blk(0), pl.BlockSpec((32, LANES), lambda c: (0, c))],
        out_shape=[jax.ShapeDtypeStruct((t, nblk * LANES), BF16), jax.ShapeDtypeStruct((t, nblk * LANES), BF16),
                   jax.ShapeDtypeStruct((32, nblk * LANES), F32)],
        scratch_shapes=[pltpu.VMEM((t + 32, LANES), F32), pltpu.VMEM((t + 32, LANES), F32)],
        compiler_params=_cp(("arbitrary",), 56),
    )(proj, proj, dcv, cw32)


def _rnn_bwd(proj, hr, dhr, rw8, rvec, wab, wib, name, host=None):
    t = proj.shape[0]
    nblk = rvec.shape[1] // LANES

    def body(ins, outs, scr):
        ux_ref, h_ref, dh_ref, rw_ref, rvec_ref, wa_ref, wi_ref = ins
        dux_ref, sm_ref, dwa_ref, dwi_ref = outs
        ext_ref, extd_ref, sa_ref, sb_ref, carry_ref = scr
        xr = _rnn_conv(ux_ref[...], rw_ref, rvec_ref, ext_ref)
        xb, r, ig, ls, a, mult = _rglru_gates(xr, wa_ref, wi_ref, rvec_ref)
        row = lax.broadcasted_iota(jnp.int32, (t, LANES), 0)
        a_next = jnp.where(row < t - 1, pltpu.roll(a, t - 1, 0), 0.0)
        _tiled_scan(a_next, dh_ref[...].astype(F32), True, sa_ref, sb_ref, carry_ref, extd_ref)
        g = extd_ref[0:t, :]
        hprev = jnp.where(row >= 1, pltpu.roll(h_ref[...], 1, 0), 0.0)
        da = g * hprev
        dmult = g * (ig * xr)
        dig = g * mult * xr
        dxr = g * mult * ig
        dlog_a = a * (da - dmult * a / mult)
        dr = dlog_a * (RG_C * ls)
        dls = RG_C * jnp.sum(dlog_a * r, axis=0, keepdims=True)
        dpr = dr * r * (1.0 - r)
        dpi = dig * ig * (1.0 - ig)
        dprb = dpr.astype(BF16)
        dpib = dpi.astype(BF16)
        dxr = dxr + _dot_nt(dprb, wa_ref[...]) + _dot_nt(dpib, wi_ref[...])
        dwa_ref[...] = _dot_tn(xb, dprb)
        dwi_ref[...] = _dot_tn(xb, dpib)
        extd_ref[0:t, :] = dxr
        extd_ref[t:, :] = jnp.zeros((8, LANES), F32)
        dux = rw_ref[RNN_CONV_W - 1:RNN_CONV_W, :] * dxr
        for k in range(RNN_CONV_W - 1):
            d = RNN_CONV_W - 1 - k
            dux = dux + rw_ref[k:k + 1, :] * extd_ref[d:d + t, :]
        dux_ref[...] = dux.astype(BF16)
        for k in range(RNN_CONV_W):
            d = RNN_CONV_W - 1 - k
            sm_ref[k:k + 1, :] = jnp.sum(dxr * ext_ref[8 - d:8 - d + t, :], axis=0, keepdims=True)
        sm_ref[4:5, :] = jnp.sum(dxr, axis=0, keepdims=True)
        sm_ref[5:6, :] = jnp.sum(dpr, axis=0, keepdims=True)
        sm_ref[6:7, :] = jnp.sum(dpi, axis=0, keepdims=True)
        sm_ref[7:8, :] = dls * _sigmoid(-rvec_ref[3:4, :])

    blk = lambda off: pl.BlockSpec((t, LANES), lambda c: (0, off + c))
    sq = pl.BlockSpec((None, LANES, LANES), lambda c: (c, 0, 0))
    return _hosted_call(
        body, host, name, (nblk,),
        [blk(2 * nblk), blk(0), blk(0),
         pl.BlockSpec((8, LANES), lambda c: (0, c)),
         pl.BlockSpec((8, LANES), lambda c: (0, c)), sq, sq],
        [blk(0), pl.BlockSpec((8, LANES), lambda c: (0, c)), sq, sq],
        [jax.ShapeDtypeStruct((t, nblk * LANES), BF16), jax.ShapeDtypeStruct((8, nblk * LANES), F32),
         jax.ShapeDtypeStruct((nblk, LANES, LANES), F32), jax.ShapeDtypeStruct((nblk, LANES, LANES), F32)],
        [pltpu.VMEM((t + 8, LANES), F32), pltpu.VMEM((t + 8, LANES), F32), pltpu.VMEM((t, LANES), F32),
         pltpu.VMEM((t, LANES), F32), pltpu.VMEM((t // SUBLANES, LANES), F32)],
        ("arbitrary",), 60, (proj, hr, dhr, rw8, rvec, wab, wib))


def _mixin_bwd(dxo, x, parts, vec, win, f, nvec, tm, name):
    t, d = x.shape
    nb, _, cb = win.shape
    dc = parts[0].shape[1]
    per = dc // cb
    nt = t // tm

    def body(dxo_ref, x_ref, p0, p1, p2, p3, vec_ref, w_ref, f_ref, nvec_ref, dx_ref, dvec_ref, df_ref, dgt_ref):
        i = pl.program_id(0)

        @pl.when(i == 0)
        def _():
            dvec_ref[...] = jnp.zeros_like(dvec_ref)
            dgt_ref[...] = jnp.zeros_like(dgt_ref)

        prefs = (p0, p1, p2, p3)
        dh = jnp.zeros((tm, d), F32)
        for k in range(nb):
            dh = dh + _dot_nt(prefs[k // per][:, (k % per) * cb:(k % per + 1) * cb], w_ref[k])
        dx = dxo_ref[...] + _adaln_bwd(x_ref[...], dh, vec_ref, dvec_ref)
        dx_ref[...] = dx
        _emit_df(dx, f_ref, nvec_ref, df_ref, dgt_ref)

        @pl.when(i == nt - 1)
        def _():
            _adaln_finish(vec_ref, dvec_ref)

    tile_d = pl.BlockSpec((tm, d), lambda i: (i, 0))
    tile_c = pl.BlockSpec((tm, dc), lambda i: (i, 0))
    tab = pl.BlockSpec((8, d), lambda i: (0, 0))
    return pl.pallas_call(
        body, name=name, grid=(nt,),
        in_specs=[tile_d, tile_d, tile_c, tile_c, tile_c, tile_c, tab,
                  pl.BlockSpec((nb, d, cb), lambda i: (0, 0, 0)), tile_d, tab],
        out_specs=[tile_d, tab, tile_d, tab],
        out_shape=[jax.ShapeDtypeStruct((t, d), F32), jax.ShapeDtypeStruct((8, d), F32),
                   jax.ShapeDtypeStruct((t, d), BF16), jax.ShapeDtypeStruct((8, d), F32)],
        compiler_params=_cp(("arbitrary",), 48),
    )(dxo, x, *parts, vec, win, f, nvec)


def _coords():
    return lax.axis_index("x"), lax.axis_index("y"), lax.axis_index("c")


def _flip(v, bit):
    return 1 - v if bit else v


def _gather_copy(outs, send_sems, recv_sems, a, k, block, to, src=None):
    dst = outs[a].at[block]
    return pltpu.make_async_remote_copy(
        src_ref=dst if src is None else src, dst_ref=dst,
        send_sem=send_sems.at[a, k], recv_sem=recv_sems.at[a, k],
        device_id=to, device_id_type=MESH_IDS)


def _gather_start(ins, outs, send_sems, recv_sems, loc_sems):
    x, y, c = _coords()
    me = 4 * x + 2 * y + c
    for a in range(len(ins)):
        pltpu.make_async_copy(ins[a], outs[a].at[me], loc_sems.at[a]).start()
    for a in range(len(ins)):
        _gather_copy(outs, send_sems, recv_sems, a, 0, me, (x, y, 1 - c), src=ins[a]).start()
        for j, (cx, cy) in enumerate([(1 - x, y), (x, 1 - y), (1 - x, 1 - y)]):
            _gather_copy(outs, send_sems, recv_sems, a, 1 + j, me, (cx, cy, c), src=ins[a]).start()


def _gather_finish(ins, outs, send_sems, recv_sems, loc_sems):
    x, y, c = _coords()
    me = 4 * x + 2 * y + c
    sib = (x, y, 1 - c)
    chips = [(1 - x, y), (x, 1 - y), (1 - x, 1 - y)]
    n = len(ins)
    for a in range(n):
        for j, (cx, cy) in enumerate(chips):
            blk = 4 * cx + 2 * cy + c
            _gather_copy(outs, send_sems, recv_sems, a, 1 + j, blk, sib).wait_recv()
            _gather_copy(outs, send_sems, recv_sems, a, 4 + j, blk, sib).start()
    for a in range(n):
        _gather_copy(outs, send_sems, recv_sems, a, 0, 4 * x + 2 * y + (1 - c), sib).wait_recv()
        for j, (cx, cy) in enumerate(chips):
            _gather_copy(outs, send_sems, recv_sems, a, 4 + j, 4 * cx + 2 * cy + (1 - c), sib).wait_recv()
    for a in range(n):
        _gather_copy(outs, send_sems, recv_sems, a, 0, me, sib, src=ins[a]).wait_send()
        for j, (cx, cy) in enumerate(chips):
            _gather_copy(outs, send_sems, recv_sems, a, 1 + j, me, (cx, cy, c), src=ins[a]).wait_send()
            _gather_copy(outs, send_sems, recv_sems, a, 4 + j, 4 * cx + 2 * cy + c, sib).wait_send()
        pltpu.make_async_copy(ins[a], outs[a].at[me], loc_sems.at[a]).wait()


def _gather_shapes(shards):
    return [jax.ShapeDtypeStruct((NDEV,) + s.shape, s.dtype) for s in shards]


def _gather_sems(n):
    return [pltpu.SemaphoreType.DMA((n, 7)), pltpu.SemaphoreType.DMA((n, 7)), pltpu.SemaphoreType.DMA((n,))]


def _sibling_copies(ins, outs, send_sems, recv_sems):
    x, y, c = _coords()
    return [pltpu.make_async_remote_copy(
        src_ref=ins[a].at[2 * q + (1 - c)], dst_ref=outs[a].at[q],
        send_sem=send_sems.at[a, q], recv_sem=recv_sems.at[a, q],
        device_id=(x, y, 1 - c), device_id_type=MESH_IDS) for a in range(len(ins)) for q in range(4)]


def _sibling_shapes(parts):
    return [jax.ShapeDtypeStruct((4,) + p.shape[1:], p.dtype) for p in parts]


def _chips_copies(ins, outs, send_sems, recv_sems):
    x, y, c = _coords()
    copies = []
    for a in range(len(ins)):
        for k, (kx, ky) in enumerate([(1, 0), (0, 1), (1, 1)]):
            tx, ty = _flip(x, kx), _flip(y, ky)
            copies.append(pltpu.make_async_remote_copy(
                src_ref=ins[a].at[2 * tx + ty], dst_ref=outs[a].at[k],
                send_sem=send_sems.at[a, k], recv_sem=recv_sems.at[a, k],
                device_id=(tx, ty, c), device_id_type=MESH_IDS))
    return copies


def _chips_shapes(sums):
    return [jax.ShapeDtypeStruct((3,) + s.shape[1:], s.dtype) for s in sums]


def _direct_copies(ins, outs, send_sems, recv_sems):
    x, y, c = _coords()
    me = 4 * x + 2 * y + c
    copies = []
    for a in range(len(ins)):
        for k in range(1, NDEV):
            kx, ky, kc = (k >> 2) & 1, (k >> 1) & 1, k & 1
            copies.append(pltpu.make_async_remote_copy(
                src_ref=ins[a], dst_ref=outs[a].at[me],
                send_sem=send_sems.at[a, k - 1], recv_sem=recv_sems.at[a, k - 1],
                device_id=(_flip(x, kx), _flip(y, ky), _flip(c, kc)), device_id_type=MESH_IDS))
    return copies


class _Exchange:
    def __init__(self, kind, arrays):
        self.kind, self.arrays, self.n = kind, list(arrays), len(arrays)

    def out_shapes(self):
        return {"gather": _gather_shapes, "direct": _gather_shapes, "sibling": _sibling_shapes,
                "chips": _chips_shapes}[self.kind](self.arrays)

    def sems(self):
        if self.kind in ("gather", "direct"):
            return _gather_sems(self.n)
        k = {"sibling": 4, "chips": 3}[self.kind]
        return [pltpu.SemaphoreType.DMA((self.n, k)), pltpu.SemaphoreType.DMA((self.n, k))]

    def _copies(self, ins, outs, sems):
        if self.kind == "direct":
            x, y, c = _coords()
            own = [pltpu.make_async_copy(ins[a], outs[a].at[4 * x + 2 * y + c], sems[2].at[a]) for a in range(self.n)]
            return own + _direct_copies(ins, outs, sems[0], sems[1])
        return {"sibling": _sibling_copies, "chips": _chips_copies}[self.kind](ins, outs, *sems)

    def start(self, ins, outs, sems):
        if self.kind == "gather":
            _gather_start(ins, outs, *sems)
        else:
            for cpy in self._copies(ins, outs, sems):
                cpy.start()

    def finish(self, ins, outs, sems):
        if self.kind == "gather":
            _gather_finish(ins, outs, *sems)
        else:
            for cpy in self._copies(ins, outs, sems):
                cpy.wait()


def _hosted_call(body, host, name, grid, in_specs, out_specs, out_shape, scratch, sem, vmem_mb, args, manual=False):
    n = host.n if host else 0
    ni, no, ns = len(in_specs), len(out_specs), len(scratch)

    def full(*refs):
        ins, h_in = refs[:ni], refs[ni:ni + n]
        outs, h_out = refs[ni + n:ni + n + no], refs[ni + n + no:ni + 2 * n + no]
        scr, sems = refs[ni + 2 * n + no:ni + 2 * n + no + ns], refs[ni + 2 * n + no + ns:]
        if manual:
            body(ins, outs, scr, lambda: host.start(h_in, h_out, sems), lambda: host.finish(h_in, h_out, sems))
            return
        if host and grid:
            first = functools.reduce(lambda a, b: a & b, [pl.program_id(k) == 0 for k in range(len(grid))])
            last = functools.reduce(lambda a, b: a & b, [pl.program_id(k) == g - 1 for k, g in enumerate(grid)])

            @pl.when(first)
            def _():
                host.start(h_in, h_out, sems)
        elif host:
            host.start(h_in, h_out, sems)

        body(ins, outs, scr)

        if host and grid:
            @pl.when(last)
            def _():
                host.finish(h_in, h_out, sems)
        elif host:
            host.finish(h_in, h_out, sems)

    anyspec = pl.BlockSpec(memory_space=pl.ANY)
    return pl.pallas_call(
        full, name=name, grid=grid,
        in_specs=list(in_specs) + [anyspec] * n, out_specs=list(out_specs) + [anyspec] * n,
        out_shape=list(out_shape) + (host.out_shapes() if host else []),
        scratch_shapes=list(scratch) + (host.sems() if host else []),
        compiler_params=_cp(sem, vmem_mb),
    )(*args, *(host.arrays if host else []))


def _exchange(host, name, after=()):
    n, na = host.n, len(after)

    def body(*refs):
        ins, outs, sems = refs[:n], refs[n + na:2 * n + na], refs[2 * n + na:]
        host.start(ins, outs, sems)
        host.finish(ins, outs, sems)

    anyspec = pl.BlockSpec(memory_space=pl.ANY)
    return pl.pallas_call(
        body, name=name, in_specs=[anyspec] * (n + na), out_specs=[anyspec] * n,
        out_shape=host.out_shapes(), scratch_shapes=host.sems(),
    )(*host.arrays, *after)


def _chips_split_start(sums, name):
    n = len(sums)
    hbm = pl.BlockSpec(memory_space=pltpu.HBM)
    sem = pl.BlockSpec(memory_space=pltpu.SEMAPHORE)

    def body(*refs):
        ins, lands = refs[:n], refs[n:2 * n]
        sems = refs[2 * n:2 * n + 6 * n]
        token = refs[-1]
        for cpy in _chips_copies(ins, lands, _SemGrid(sems[:3 * n], 3), _SemGrid(sems[3 * n:], 3)):
            cpy.start()
        token[...] = jnp.zeros_like(token)

    land_shapes = _chips_shapes(sums)
    lands = [pltpu.with_memory_space_constraint(lax.empty(s.shape, s.dtype), pltpu.HBM) for s in land_shapes]
    return pl.pallas_call(
        body, name=name,
        out_shape=(*[pltpu.SemaphoreType.DMA(())] * (6 * n),
                   *[pltpu.HBM(s.shape, s.dtype) for s in sums],
                   *[pltpu.HBM(s.shape, s.dtype) for s in land_shapes],
                   jax.ShapeDtypeStruct((8, LANES), F32)),
        in_specs=[hbm] * (2 * n),
        out_specs=(*[sem] * (6 * n), *[hbm] * (2 * n), pl.BlockSpec(memory_space=pltpu.VMEM)),
        input_output_aliases={i: 6 * n + i for i in range(2 * n)},
        compiler_params=pltpu.CompilerParams(has_side_effects=pltpu.SideEffectType.DATAFLOW_SIDE_EFFECTING),
    )(*[pltpu.with_memory_space_constraint(s, pltpu.HBM) for s in sums], *lands)


class _SemGrid:
    def __init__(self, sems, k):
        self.sems, self.k = sems, k

    @property
    def at(self):
        return self

    def __getitem__(self, idx):
        return self.sems[idx[0] * self.k + idx[1]]


def _chips_split_wait(started, n, after, name):
    sems = started[:6 * n]
    thru = started[6 * n:8 * n]
    hbm = pl.BlockSpec(memory_space=pltpu.HBM)
    sem = pl.BlockSpec(memory_space=pltpu.SEMAPHORE)

    def body(*refs):
        ins, lands = refs[:n], refs[n:2 * n]
        s = refs[2 * n:2 * n + 6 * n]
        for cpy in _chips_copies(ins, lands, _SemGrid(s[:3 * n], 3), _SemGrid(s[3 * n:], 3)):
            cpy.wait_send()
            cpy.wait_recv()

    outs = pl.pallas_call(
        body, name=name,
        out_shape=tuple(pltpu.HBM(a.shape, a.dtype) for a in thru),
        in_specs=[hbm] * (2 * n) + [sem] * (6 * n) + [pl.BlockSpec(memory_space=pl.ANY)],
        out_specs=tuple([hbm] * (2 * n)),
        input_output_aliases={i: i for i in range(2 * n)},
        compiler_params=pltpu.CompilerParams(has_side_effects=pltpu.SideEffectType.DATAFLOW_SIDE_EFFECTING),
    )(*thru, *sems, after)
    return list(outs[n:])


def _chip_sum(part, recv, sel, tr, name):
    _, _, r, c = part.shape

    def body(sel_ref, p_ref, r_ref, cs_ref, own_ref):
        q = pl.program_id(1)
        s = p_ref[...].astype(F32) + r_ref[...].astype(F32)
        cs_ref[...] = s.astype(BF16)

        @pl.when(q == sel_ref[1])
        def _():
            own_ref[...] = s

    return pl.pallas_call(
        body, name=name,
        grid_spec=pltpu.PrefetchScalarGridSpec(
            num_scalar_prefetch=1, grid=(r // tr, 4),
            in_specs=[pl.BlockSpec((None, None, tr, c), lambda i, q, s: (q, s[0], i, 0)),
                      pl.BlockSpec((None, tr, c), lambda i, q, s: (q, i, 0))],
            out_specs=[pl.BlockSpec((None, tr, c), lambda i, q, s: (q, i, 0)),
                       pl.BlockSpec((tr, c), lambda i, q, s: (i, 0))]),
        out_shape=[jax.ShapeDtypeStruct((4, r, c), BF16), jax.ShapeDtypeStruct((r, c), F32)],
        compiler_params=_cp(("arbitrary", "arbitrary"), 48),
    )(sel, part, recv)


def _gather_direct(src_ref, buf_ref, send_sems, recv_sems):
    x, y, c = _coords()
    me = 4 * x + 2 * y + c
    buf_ref[me] = src_ref[...]
    copies = []
    for k in range(1, NDEV):
        kx, ky, kc = (k >> 2) & 1, (k >> 1) & 1, k & 1
        copies.append(pltpu.make_async_remote_copy(
            src_ref=src_ref, dst_ref=buf_ref.at[me],
            send_sem=send_sems.at[k - 1], recv_sem=recv_sems.at[k - 1],
            device_id=(_flip(x, kx), _flip(y, ky), _flip(c, kc)), device_id_type=MESH_IDS))
    for cpy in copies:
        cpy.start()

    def wait():
        for k in range(1, NDEV):
            kx, ky, kc = (k >> 2) & 1, (k >> 1) & 1, k & 1
            peer = 4 * _flip(x, kx) + 2 * _flip(y, ky) + _flip(c, kc)
            pltpu.make_async_remote_copy(
                src_ref=src_ref, dst_ref=buf_ref.at[peer],
                send_sem=send_sems.at[k - 1], recv_sem=recv_sems.at[k - 1],
                device_id=(x, y, c), device_id_type=MESH_IDS).wait_recv()
        for cpy in copies:
            cpy.wait_send()

    return me, wait


def _mod_exchange(c_row, wmod, bmod, wfmod, bfmod, name, host=None):
    d = c_row.shape[1]
    nm, nf = wmod.shape[1], wfmod.shape[1]
    nw = nm + nf

    def body(ins, outs, scr, host_start, host_finish):
        c_ref, wm_ref, bm_ref, wf_ref, bf_ref = ins
        cs_ref, mod_ref, fmod_ref = outs
        slab_ref, csbuf_ref, mslab_ref, mbuf_ref, s1, r1, s2, r2 = scr
        cv = c_ref[...]
        slab_ref[...] = jnp.broadcast_to(cv * _sigmoid(cv), (8, d))
        _, wait1 = _gather_direct(slab_ref, csbuf_ref, s1, r1)
        wait1()
        host_start()
        for b in range(NDEV):
            cs_ref[b:b + 1, :] = csbuf_ref[b, 0:1, :]
        cs = cs_ref[...]
        mslab_ref[:, 0:nm] = jnp.dot(cs, wm_ref[...], precision=HI, preferred_element_type=F32) + bm_ref[...]
        mslab_ref[:, nm:] = jnp.dot(cs, wf_ref[...], precision=HI, preferred_element_type=F32) + bf_ref[...]
        me, wait2 = _gather_direct(mslab_ref, mbuf_ref, s2, r2)
        host_finish()
        wait2()
        mine = lax.broadcasted_iota(jnp.int32, (8, nw), 0) == me
        for k in range(NDEV):
            rowk = jnp.sum(jnp.where(mine, mbuf_ref[k], 0.0), axis=0, keepdims=True)
            mod_ref[k:k + 1, :] = rowk[:, 0:nm]
            fmod_ref[k:k + 1, :] = rowk[:, nm:]

    vm = pl.BlockSpec(memory_space=pltpu.VMEM)
    return _hosted_call(
        body, host, name, (), [vm] * 5, [vm] * 3,
        [jax.ShapeDtypeStruct((NDEV, d), F32), jax.ShapeDtypeStruct((NDEV, nm), F32),
         jax.ShapeDtypeStruct((NDEV, nf), F32)],
        [pltpu.VMEM((8, d), F32), pltpu.VMEM((NDEV, 8, d), F32),
         pltpu.VMEM((8, nw), F32), pltpu.VMEM((NDEV, 8, nw), F32),
         pltpu.SemaphoreType.DMA((7,)), pltpu.SemaphoreType.DMA((7,)),
         pltpu.SemaphoreType.DMA((7,)), pltpu.SemaphoreType.DMA((7,))],
        None, 40, (c_row, wmod, bmod, wfmod, bfmod), manual=True)


def _table_sum(tabs, name):
    n = len(tabs)

    def body(*refs):
        for a in range(n):
            tot = refs[a][0]
            for k in range(1, NDEV):
                tot = tot + refs[a][k]
            refs[n + a][...] = tot

    vm = pl.BlockSpec(memory_space=pltpu.VMEM)
    return pl.pallas_call(
        body, name=name, in_specs=[vm] * n, out_specs=[vm] * n,
        out_shape=[jax.ShapeDtypeStruct(tb.shape[1:], F32) for tb in tabs],
    )(*tabs)


def _adamw_math(w, g, m, v):
    m = ADAM_B1 * m + (1.0 - ADAM_B1) * g
    v = ADAM_B2 * v + (1.0 - ADAM_B2) * (g * g)
    m_hat = m / (1.0 - ADAM_B1 ** ADAM_STEP)
    v_hat = v / (1.0 - ADAM_B2 ** ADAM_STEP)
    delta = -ADAM_LR * (m_hat / (jnp.sqrt(v_hat) + ADAM_EPS) + ADAM_WD * w)
    return delta, m, v


def _adamw_small(params, name):
    n = len(params)

    def body(*refs):
        for p in range(n):
            w_ref, g_ref, m_ref, v_ref = refs[4 * p:4 * p + 4]
            d_ref, mo_ref, vo_ref = refs[4 * n + 3 * p:4 * n + 3 * p + 3]
            d_ref[...], mo_ref[...], vo_ref[...] = _adamw_math(w_ref[...], g_ref[...], m_ref[...], v_ref[...])

    vm = pl.BlockSpec(memory_space=pltpu.VMEM)
    flat = [a for p in params for a in p]
    outs = pl.pallas_call(
        body, name=name, in_specs=[vm] * (4 * n), out_specs=[vm] * (3 * n),
        out_shape=[jax.ShapeDtypeStruct(p[0].shape, F32) for p in params for _ in range(3)])(*flat)
    return [outs[3 * p:3 * p + 3] for p in range(n)]


def _rs_final(own, recv, w, m, v, tr, name):
    r, c = own.shape

    def body(o_ref, r_ref, w_ref, m_ref, v_ref, g_ref, d_ref, mo_ref, vo_ref):
        g = o_ref[...] + r_ref[0].astype(F32) + r_ref[1].astype(F32) + r_ref[2].astype(F32)
        g_ref[...] = g
        d_ref[...], mo_ref[...], vo_ref[...] = _adamw_math(w_ref[...], g, m_ref[...], v_ref[...])

    tile = pl.BlockSpec((tr, c), lambda i: (i, 0))
    sds = jax.ShapeDtypeStruct((r, c), F32)
    return pl.pallas_call(
        body, name=name, grid=(r // tr,),
        in_specs=[tile, pl.BlockSpec((3, tr, c), lambda i: (0, i, 0)), tile, tile, tile],
        out_specs=[tile] * 4, out_shape=[sds] * 4,
        compiler_params=_cp(("arbitrary",), 48),
    )(own, recv, w, m, v)


def _mod_weight_update(cs, dm, w, m, v, tr, name):
    r, c = w.shape

    def body(cs_ref, dm_ref, w_ref, m_ref, v_ref, g_ref, d_ref, mo_ref, vo_ref):
        g = lax.dot_general(cs_ref[...], dm_ref[...], (((0,), (0,)), ((), ())),
                            precision=HI, preferred_element_type=F32)
        g_ref[...] = g
        d_ref[...], mo_ref[...], vo_ref[...] = _adamw_math(w_ref[...], g, m_ref[...], v_ref[...])

    tile = pl.BlockSpec((tr, c), lambda i: (i, 0))
    sds = jax.ShapeDtypeStruct((r, c), F32)
    return pl.pallas_call(
        body, name=name, grid=(r // tr,),
        in_specs=[pl.BlockSpec((NDEV, tr), lambda i: (0, i)), pl.BlockSpec((NDEV, c), lambda i: (0, 0)),
                  tile, tile, tile],
        out_specs=[tile] * 4, out_shape=[sds] * 4,
        compiler_params=_cp(("arbitrary",), 48),
    )(cs, dm, w, m, v)


def _rows(*vs):
    d = vs[0].shape[-1]
    rows = [v.reshape(1, d) for v in vs]
    return jnp.concatenate(rows + [jnp.zeros((8 - len(rows), d), F32)], axis=0)


def _block_diag_pairs(w):
    hd = w.shape[-1]
    z = jnp.zeros((w.shape[0] // 2, hd, hd), w.dtype)
    top = jnp.concatenate([w[0::2], z], axis=2)
    bot = jnp.concatenate([z, w[1::2]], axis=2)
    return jnp.concatenate([top, bot], axis=1).astype(BF16)


def _diag_pairs(g):
    hd = g.shape[-1] // 2
    both = jnp.stack([g[:, :hd, :hd], g[:, hd:, hd:]], axis=1)
    return both.reshape(2 * g.shape[0], hd, hd)


def kernel(x, c, w_mod, b_mod, g_ffn1, w_ffn1_in, w_ffn1_out, g_mix, w_in, conv_w, conv_b, ln_g, ln_b, rnn_conv_w, rnn_conv_b, w_a, b_a, w_i, b_i, lru_lambda, w_out, g_ffn2, w_ffn2_in, w_ffn2_out, w_fmod, b_fmod, g_final, loss_target, m_w_mod, m_b_mod, m_g_ffn1, m_w_ffn1_in, m_w_ffn1_out, m_g_mix, m_w_in, m_conv_w, m_conv_b, m_ln_g, m_ln_b, m_rnn_conv_w, m_rnn_conv_b, m_w_a, m_b_a, m_w_i, m_b_i, m_lru_lambda, m_w_out, m_g_ffn2, m_w_ffn2_in, m_w_ffn2_out, m_w_fmod, m_b_fmod, m_g_final, v_w_mod, v_b_mod, v_g_ffn1, v_w_ffn1_in, v_w_ffn1_out, v_g_mix, v_w_in, v_conv_w, v_conv_b, v_ln_g, v_ln_b, v_rnn_conv_w, v_rnn_conv_b, v_w_a, v_b_a, v_w_i, v_b_i, v_lru_lambda, v_w_out, v_g_ffn2, v_w_ffn2_in, v_w_ffn2_out, v_w_fmod, v_b_fmod, v_g_final):
    t, d = x.shape[1], x.shape[2]
    fb = w_ffn1_in.shape[2]
    nm = w_mod.shape[2]
    nf = w_fmod.shape[1]
    dc = conv_b.shape[1]
    cl = conv_w.shape[2]
    tm = min(TOKEN_TILE, t)
    tk = min(WGRAD_TILE, t)
    nk = t // tk
    me = 4 * lax.axis_index("x") + 2 * lax.axis_index("y") + lax.axis_index("c")

    tr = jnp.transpose
    bmod_l = lax.dynamic_slice(b_mod, (0, me * nm), (1, nm))
    bfmod_l = lax.dynamic_slice(b_fmod.reshape(1, -1), (0, me * nf), (1, nf))
    cwl = jnp.concatenate([conv_w[0], jnp.zeros((1, cl), F32), rnn_conv_w[0], jnp.zeros((4, cl), F32)], axis=0)
    cs, mod_rows, fmod_rows, wi1, wo1, cwg = _mod_exchange(
        c, w_mod[0], bmod_l, w_fmod, bfmod_l, "mod_and_gather_ffn1",
        host=_Exchange("gather", [tr(w_ffn1_in[0]).astype(BF16), w_ffn1_out[0].astype(BF16), cwl]))
    wi1 = wi1.reshape(2, 4, fb, d)
    wo1 = wo1.reshape(4 * fb, d)
    mod = mod_rows.reshape(9, d)
    fmod = fmod_rows.reshape(2, d)
    vec1 = _rows(g_ffn1, mod[0], mod[1], mod[2])
    vecm = _rows(g_mix, mod[3], mod[4], mod[5])
    vec3 = _rows(g_ffn2, mod[6], mod[7], mod[8])
    vecf = _rows(g_final, fmod[0], fmod[1])

    xin = x[0]
    later = [w_in[0].astype(BF16), w_out[0].astype(BF16), tr(w_ffn2_in[0]).astype(BF16), w_ffn2_out[0].astype(BF16)]
    x1, h1, gu1, f1, win, wout, wi2, wo2 = _ffn_fwd(xin, vec1, wi1, wo1, tm, "ffn1_fwd",
                                                    host=_Exchange("gather", later))
    wi2 = wi2.reshape(2, 4, fb, d)
    wo2 = wo2.reshape(4 * fb, d)
    wout = wout.reshape(d, d)
    h2, proj = _mix_in(x1, vecm, win, tm, "mix_in")
    lnv = _rows(ln_g, ln_b)
    rvec = _rows(rnn_conv_b, b_a, b_i, lru_lambda)
    wab = _block_diag_pairs(w_a[0])
    wib = _block_diag_pairs(w_i[0])
    cwf = jnp.transpose(cwg, (1, 0, 2)).reshape(40, NDEV * cl)
    cw32 = jnp.concatenate([cwf[0:CONV_W], conv_b], axis=0)
    rw8 = cwf[32:40]

    (cv,) = _conv_fwd(proj, cw32, "conv_fwd")
    hr, yr = _rnn_fwd(proj, rw8, rvec, wab, wib, "rnn_fwd")
    x2, ym, ycat = _mix_out(x1, cv, yr, vecm, lnv, wout, tm, "mix_out")
    h3, gu3, dx3, dvf, df3, dva3 = _ffn_fwd_loss(x2, vec3, wi2, wo2, loss_target[0], vecf, tm, "ffn2_fwd_loss")
    a_tok = lambda width: pl.BlockSpec((tk, width), lambda k, s: (s, 0))
    sel = jnp.stack([lax.axis_index("c"), 2 * lax.axis_index("x") + lax.axis_index("y")]).astype(jnp.int32)
    row_tile = {"w_ffn1_in": fb // 4, "w_ffn1_out": fb // 4, "w_in": 512, "w_out": 128,
                "w_ffn2_in": fb // 4, "w_ffn2_out": fb // 4}

    def chip_sums(names, partials, from_sib):
        out = [_chip_sum(p.reshape((4, 2) + p.shape[1:]), r, sel, p.shape[1], "chip_sum_" + nm_)
               for nm_, p, r in zip(names, partials, from_sib)]
        return [o[0] for o in out], [o[1] for o in out]

    dgu3, p_wi2, p_wo2 = _ffn_bwd_w(df3, gu3, h3, wo2, tm, "ffn2_bwd_w")
    p_wi2 = p_wi2.reshape(NDEV, fb, d)
    p_wo2 = p_wo2.reshape(NDEV, fb // 2, d)
    names2 = ["w_ffn2_in", "w_ffn2_out"]
    dx2, dv3, sib_wi2, sib_wo2 = _ffn_bwd_in(dx3, x2, vec3, dgu3, wi2, tm, "ffn2_bwd_in",
                                             host=_Exchange("sibling", [p_wi2, p_wo2]))
    sums2, owns2 = chip_sums(names2, [p_wi2, p_wo2], [sib_wi2, sib_wo2])
    started2 = _chips_split_start(sums2, "rs_chips_ffn2_start")
    dym, dcv, dhr, duy, dln, dgt2 = _mixout_bwd(
        dx2, ym, cv, hr, proj, vecm + started2[-1][0:1, 0:1], lnv, wout, tm, "mixout_bwd")
    dval, dgate, dcw = _conv_bwd(proj, dcv, cw32, "conv_bwd")
    dux, rsm, dwab, dwib = _rnn_bwd(proj, hr, dhr, rw8, rvec, wab, wib, "rnn_bwd")
    parts = [dval, dgate, dux, duy]
    dx1, dvm, df1, dva1 = _mixin_bwd(dx2, x1, parts, vecm, win, f1, vec1, tm, "mixin_bwd")
    p_wout = _mm_tn(ycat, dym, a_tok(d), a_tok(d), 1, nk, d, d, "wgrad_out").reshape(NDEV, d // NDEV, d)
    p_win = _wgrad_in(h2, parts, win.shape[2], min(tk, 1024), "wgrad_in")
    namesm = ["w_in", "w_out"]
    sumsm, ownsm = chip_sums(namesm, [p_win, p_wout],
                             _exchange(_Exchange("sibling", [p_win, p_wout]), "rs_sibling_mix"))
    lane_pad = lambda v: jnp.concatenate([v, jnp.zeros_like(v)], axis=1)
    startedm = _chips_split_start(sumsm, "rs_chips_mix_start")
    early = jnp.concatenate([dva3, dv3, dvf, dvm, dgt2, dva1 + startedm[-1][0:1, 0:1], dcw.reshape(16, d),
                             lane_pad(dln), lane_pad(rsm),
                             _diag_pairs(dwab).reshape(32, d), _diag_pairs(dwib).reshape(32, d)], axis=0)
    dgu1, p_wi1, p_wo1, all_early = _ffn_bwd_w(
        df1, gu1, h1, wo1, tm, "ffn1_bwd_w", host=_Exchange("direct", [early]))
    p_wi1 = p_wi1.reshape(NDEV, fb, d)
    p_wo1 = p_wo1.reshape(NDEV, fb // 2, d)
    names1 = ["w_ffn1_in", "w_ffn1_out"]
    sums1, owns1 = chip_sums(names1, [p_wi1, p_wo1],
                             _exchange(_Exchange("sibling", [p_wi1, p_wo1]), "rs_sibling_ffn1"))
    started = _chips_split_start(sums1, "rs_chips_ffn1_start")
    dx0, dv1 = _ffn_bwd_in(dx1, xin, vec1 + started[-1][0:1, 0:1], dgu1, wi1, tm, "ffn1_bwd_in")
    from_chips = dict(zip(names2, _chips_split_wait(started2, len(sums2), dx0, "rs_chips_ffn2_wait")))
    from_chips.update(zip(namesm, _chips_split_wait(startedm, len(sumsm), dx0, "rs_chips_mix_wait")))
    owns = dict(zip(namesm + names2 + names1, ownsm + owns2 + owns1))

    big = {"w_ffn1_in": (tr(w_ffn1_in[0]), tr(m_w_ffn1_in[0]), tr(v_w_ffn1_in[0])),
           "w_ffn1_out": (w_ffn1_out[0], m_w_ffn1_out[0], v_w_ffn1_out[0]),
           "w_in": (w_in[0], m_w_in[0], v_w_in[0]), "w_out": (w_out[0], m_w_out[0], v_w_out[0]),
           "w_ffn2_in": (tr(w_ffn2_in[0]), tr(m_w_ffn2_in[0]), tr(v_w_ffn2_in[0])),
           "w_ffn2_out": (w_ffn2_out[0], m_w_ffn2_out[0], v_w_ffn2_out[0])}
    res = {}

    def final_sum(nm_):
        out4 = _rs_final(owns[nm_], from_chips[nm_], *big[nm_], row_tile[nm_], "rs_final_" + nm_)
        res[nm_] = [(tr(o) if nm_ in ("w_ffn1_in", "w_ffn2_in") else o)[None] for o in out4]
        return out4[0]

    done = [final_sum(nm_) for nm_ in namesm + names2]
    dfm_all = jnp.concatenate([all_early[:, 17], all_early[:, 19]], axis=1)
    dfm_l = lax.dynamic_slice(dfm_all, (0, me * nf), (NDEV, nf))
    res["w_fmod"] = list(_mod_weight_update(cs, dfm_l, w_fmod, m_w_fmod, v_w_fmod, 256, "w_fmod_update"))
    (all_late,) = _exchange(_Exchange("direct", [dv1]), "late_table", after=done + [res["w_fmod"][0]])
    from_chips["w_ffn1_in"], from_chips["w_ffn1_out"] = _chips_split_wait(
        started, len(sums1), all_late, "rs_chips_ffn1_wait")
    for nm_ in names1:
        final_sum(nm_)
    te, tl = _table_sum([all_early, all_late], "table_sum")
    loss = jnp.sum(te[20])

    mod_rows_of = lambda e, l: [l[1], l[3], e[42], e[25], e[27], e[32], e[9], e[11], e[2]]
    dm_all = jnp.concatenate(mod_rows_of(jnp.swapaxes(all_early, 0, 1), jnp.swapaxes(all_late, 0, 1)), axis=1)
    dm_l = lax.dynamic_slice(dm_all, (0, me * nm), (NDEV, nm))
    res["w_mod"] = [o[None] for o in
                    _mod_weight_update(cs, dm_l, w_mod[0], m_w_mod[0], v_w_mod[0], 256, "w_mod_update")]

    dcw_f = te[48:64].reshape(32, dc)
    rsm_f = te[72:80, 0:dc]
    small_grads = {
        "b_mod": jnp.concatenate(mod_rows_of(te, tl)).reshape(1, 9 * d),
        "b_fmod": jnp.concatenate([te[17], te[19]]),
        "g_ffn1": tl[0:1], "g_mix": te[24:25], "g_ffn2": te[8:9], "g_final": te[16],
        "conv_w": lax.dynamic_slice(dcw_f, (0, me * cl), (CONV_W, cl))[None],
        "conv_b": dcw_f[31:32],
        "ln_g": te[64:65, 0:dc], "ln_b": te[65:66, 0:dc],
        "rnn_conv_w": lax.dynamic_slice(rsm_f, (0, me * cl), (RNN_CONV_W, cl))[None],
        "rnn_conv_b": rsm_f[4:5], "b_a": rsm_f[5:6], "b_i": rsm_f[6:7], "lru_lambda": rsm_f[7:8],
        "w_a": te[80:112].reshape(w_a.shape), "w_i": te[112:144].reshape(w_i.shape),
    }
    small_params = {
        "b_mod": (b_mod, m_b_mod, v_b_mod), "b_fmod": (b_fmod, m_b_fmod, v_b_fmod),
        "g_ffn1": (g_ffn1, m_g_ffn1, v_g_ffn1), "g_mix": (g_mix, m_g_mix, v_g_mix),
        "g_ffn2": (g_ffn2, m_g_ffn2, v_g_ffn2), "g_final": (g_final, m_g_final, v_g_final),
        "conv_w": (conv_w, m_conv_w, v_conv_w), "conv_b": (conv_b, m_conv_b, v_conv_b),
        "ln_g": (ln_g, m_ln_g, v_ln_g), "ln_b": (ln_b, m_ln_b, v_ln_b),
        "rnn_conv_w": (rnn_conv_w, m_rnn_conv_w, v_rnn_conv_w),
        "rnn_conv_b": (rnn_conv_b, m_rnn_conv_b, v_rnn_conv_b),
        "w_a": (w_a, m_w_a, v_w_a), "b_a": (b_a, m_b_a, v_b_a),
        "w_i": (w_i, m_w_i, v_w_i), "b_i": (b_i, m_b_i, v_b_i),
        "lru_lambda": (lru_lambda, m_lru_lambda, v_lru_lambda),
    }
    two_d = lambda w: (-1, w.shape[-1]) if w.ndim > 1 else (1, w.shape[0])
    small_names = list(small_grads)
    small_outs = _adamw_small(
        [(w.reshape(two_d(w)), small_grads[nm_].reshape(two_d(w)), m.reshape(two_d(w)), v.reshape(two_d(w)))
         for nm_ in small_names for (w, m, v) in [small_params[nm_]]], "adamw_small")
    for nm_, outs in zip(small_names, small_outs):
        shp = small_params[nm_][0].shape
        res[nm_] = [small_grads[nm_].reshape(shp)] + [o.reshape(shp) for o in outs]

    order = ["w_mod", "b_mod", "g_ffn1", "w_ffn1_in", "w_ffn1_out", "g_mix", "w_in", "conv_w", "conv_b",
             "ln_g", "ln_b", "rnn_conv_w", "rnn_conv_b", "w_a", "b_a", "w_i", "b_i", "lru_lambda", "w_out",
             "g_ffn2", "w_ffn2_in", "w_ffn2_out", "w_fmod", "b_fmod", "g_final"]
    return (loss, dx0[None], *[res[n][0] for n in order], *[res[n][1] for n in order],
            *[res[n][2] for n in order], *[res[n][3] for n in order])
```

```python
import functools
import math

import jax
import jax.numpy as jnp
from jax import lax
from jax.experimental import pallas as pl
from jax.experimental.pallas import tpu as pltpu

F32 = jnp.float32
BF16 = jnp.bfloat16
MESH_IDS = pl.DeviceIdType.MESH
NDEV = 8
EPS = 1e-6
RG_C = 8.0
CONV_W = 31
RNN_CONV_W = 4
LANES = 128
ADAM_LR = 0.001
ADAM_B1 = 0.9
ADAM_B2 = 0.999
ADAM_EPS = 1e-08
ADAM_WD = 0.01
ADAM_STEP = 10
TOKEN_TILE = 512
WGRAD_TILE = 2048
ROW_GROUP = 16
HI = lax.Precision.HIGHEST


def _cp(sem, vmem_mb):
    return pltpu.CompilerParams(dimension_semantics=sem, vmem_limit_bytes=vmem_mb * 1024 * 1024)


def _dot(a, b):
    return jnp.dot(a, b, preferred_element_type=F32)


def _dot_nt(a, b):
    return lax.dot_general(a, b, (((1,), (1,)), ((), ())), preferred_element_type=F32)


def _dot_tn(a, b):
    return lax.dot_general(a, b, (((0,), (0,)), ((), ())), preferred_element_type=F32)


def _sigmoid(x):
    return 1.0 / (1.0 + jnp.exp(-x))


def _adaln(x, vec_ref):
    rstd = lax.rsqrt(jnp.mean(x * x, axis=-1, keepdims=True) + EPS)
    return (x * rstd) * vec_ref[0:1, :] * (1.0 + vec_ref[2:3, :]) + vec_ref[1:2, :]


def _adaln_bwd(x, dh, vec_ref, dvec_ref):
    rstd = lax.rsqrt(jnp.mean(x * x, axis=-1, keepdims=True) + EPS)
    xhat = x * rstd
    dvec_ref[0:1, :] += jnp.sum(dh * xhat, axis=0, keepdims=True)
    dvec_ref[1:2, :] += jnp.sum(dh, axis=0, keepdims=True)
    dxhat = dh * (vec_ref[0:1, :] * (1.0 + vec_ref[2:3, :]))
    return rstd * (dxhat - xhat * jnp.mean(dxhat * xhat, axis=-1, keepdims=True))


def _adaln_finish(vec_ref, dvec_ref):
    s = dvec_ref[0:1, :]
    dvec_ref[3:4, :] = vec_ref[0:1, :] * s
    dvec_ref[0:1, :] = (1.0 + vec_ref[2:3, :]) * s


def _gelu_and_grad(x):
    k0 = math.sqrt(2.0 / math.pi)
    x2 = x * x
    t = jnp.tanh(k0 * (x + 0.044715 * x * x2))
    g = 0.5 * x * (1.0 + t)
    dg = 0.5 * (1.0 + t) + 0.5 * x * (1.0 - t * t) * (k0 * (1.0 + 3.0 * 0.044715 * x2))
    return g, dg


def _log_sigmoid(x):
    z = jnp.exp(-jnp.abs(x))
    u = 1.0 + z
    d = u - 1.0
    log1p = jnp.where(d == 0.0, z, jnp.log(u) * (z / jnp.where(d == 0.0, 1.0, d)))
    return jnp.minimum(x, 0.0) - log1p


def _neg_expm1(x):
    series = -x * (1.0 + x * (0.5 + x * (1.0 / 6.0 + x * (1.0 / 24.0 + x * (1.0 / 120.0)))))
    return jnp.where(x > -0.05, series, 1.0 - jnp.exp(x))


SUBLANES = 8


def _doubling_scan(a, b, reverse):
    n = a.shape[0]
    row = lax.broadcasted_iota(jnp.int32, a.shape, 0)
    s = 1
    while s < n:
        ok = (row < n - s) if reverse else (row >= s)
        shift = n - s if reverse else s
        b = a * jnp.where(ok, pltpu.roll(b, shift, 0), 0.0) + b
        if 2 * s < n:
            a = a * jnp.where(ok, pltpu.roll(a, shift, 0), 1.0)
        s *= 2
    return b


def _tiled_scan(a, b, reverse, sa_ref, sb_ref, carry_ref, out_ref):
    n = a.shape[0]
    nt8 = n // SUBLANES
    sub = lax.broadcasted_iota(jnp.int32, a.shape, 0) % SUBLANES
    for s in (1, 2, 4):
        ok = (sub < SUBLANES - s) if reverse else (sub >= s)
        shift = n - s if reverse else s
        b = a * jnp.where(ok, pltpu.roll(b, shift, 0), 0.0) + b
        a = a * jnp.where(ok, pltpu.roll(a, shift, 0), 1.0)
    sa_ref[...] = a
    sb_ref[...] = b
    edge = 0 if reverse else SUBLANES - 1
    at = sa_ref[pl.ds(edge, nt8, stride=SUBLANES), :]
    bt = sb_ref[pl.ds(edge, nt8, stride=SUBLANES), :]
    xt = _doubling_scan(at, bt, reverse)
    rowt = lax.broadcasted_iota(jnp.int32, xt.shape, 0)
    if reverse:
        carry_ref[...] = jnp.where(rowt < nt8 - 1, pltpu.roll(xt, nt8 - 1, 0), 0.0)
    else:
        carry_ref[...] = jnp.where(rowt >= 1, pltpu.roll(xt, 1, 0), 0.0)
    for r in range(nt8):
        rows = slice(r * SUBLANES, (r + 1) * SUBLANES)
        out_ref[rows, :] = sa_ref[rows, :] * carry_ref[r:r + 1, :] + sb_ref[rows, :]


def _rglru_gates(xr, wa_ref, wi_ref, rvec_ref):
    xb = xr.astype(BF16)
    r = _sigmoid(_dot(xb, wa_ref[...]) + rvec_ref[1:2, :])
    ig = _sigmoid(_dot(xb, wi_ref[...]) + rvec_ref[2:3, :])
    ls = _log_sigmoid(rvec_ref[3:4, :])
    log_a = RG_C * r * ls
    a = jnp.exp(log_a)
    mult = jnp.sqrt(_neg_expm1(2.0 * log_a))
    return xb, r, ig, ls, a, mult


def _rnn_conv(ux, rw_ref, rvec_ref, ext_ref):
    t = ux.shape[0]
    ext_ref[0:8, :] = jnp.zeros((8, ux.shape[1]), F32)
    ext_ref[8:, :] = ux
    xr = rvec_ref[0:1, :] + rw_ref[RNN_CONV_W - 1:RNN_CONV_W, :] * ux
    for k in range(RNN_CONV_W - 1):
        d = RNN_CONV_W - 1 - k
        xr = xr + rw_ref[k:k + 1, :] * ext_ref[8 - d:8 - d + t, :]
    return xr


def _ffn_fwd(x, vec, wi, wo, tm, name, host=None):
    t, d = x.shape
    nj, fb = wi.shape[1], wi.shape[2]
    nt = t // tm

    def body(ins, outs, scr):
        x_ref, vec_ref, wi_ref, wo_ref = ins
        xo_ref, h_ref, gu_ref, f_ref = outs
        acc_ref, = scr
        j = pl.program_id(1)

        @pl.when(j == 0)
        def _():
            h_ref[...] = _adaln(x_ref[...], vec_ref).astype(BF16)
            acc_ref[...] = jnp.zeros_like(acc_ref)

        h = h_ref[...]
        gate = _dot_nt(h, wi_ref[0])
        up = _dot_nt(h, wi_ref[1])
        gu_ref[0] = gate.astype(BF16)
        gu_ref[1] = up.astype(BF16)
        act = (gate * _sigmoid(gate) * up).astype(BF16)
        acc_ref[...] += _dot(act, wo_ref[...])

        @pl.when(j == nj - 1)
        def _():
            f = acc_ref[...]
            f_ref[...] = f.astype(BF16)
            xo_ref[...] = x_ref[...] + 0.5 * vec_ref[3:4, :] * f

    tile = pl.BlockSpec((tm, d), lambda i, j: (i, 0))
    return _hosted_call(
        body, host, name, (nt, nj),
        [tile,
         pl.BlockSpec((8, d), lambda i, j: (0, 0)),
         pl.BlockSpec((2, None, fb, d), lambda i, j: (0, j, 0, 0)),
         pl.BlockSpec((fb, d), lambda i, j: (j, 0))],
        [tile, tile, pl.BlockSpec((2, None, tm, fb), lambda i, j: (0, j, i, 0)), tile],
        [jax.ShapeDtypeStruct((t, d), F32), jax.ShapeDtypeStruct((t, d), BF16),
         jax.ShapeDtypeStruct((2, nj, t, fb), BF16), jax.ShapeDtypeStruct((t, d), BF16)],
        [pltpu.VMEM((tm, d), F32)], ("arbitrary", "arbitrary"), 48, (x, vec, wi, wo))


def _ffn_fwd_loss(x, vec, wi, wo, tgt, fvec, tm, name):
    t, d = x.shape
    nj, fb = wi.shape[1], wi.shape[2]
    nt = t // tm

    def body(x_ref, vec_ref, wi_ref, wo_ref, t_ref, fvec_ref, h_ref, gu_ref, dx_ref, dvec_ref, df_ref, dgt_ref,
             acc_ref):
        i = pl.program_id(0)
        j = pl.program_id(1)

        @pl.when((i == 0) & (j == 0))
        def _():
            dvec_ref[...] = jnp.zeros_like(dvec_ref)
            dgt_ref[...] = jnp.zeros_like(dgt_ref)

        @pl.when(j == 0)
        def _():
            h_ref[...] = _adaln(x_ref[...], vec_ref).astype(BF16)
            acc_ref[...] = jnp.zeros_like(acc_ref)

        h = h_ref[...]
        gate = _dot_nt(h, wi_ref[0])
        up = _dot_nt(h, wi_ref[1])
        gu_ref[0] = gate.astype(BF16)
        gu_ref[1] = up.astype(BF16)
        act = (gate * _sigmoid(gate) * up).astype(BF16)
        acc_ref[...] += _dot(act, wo_ref[...])

        @pl.when(j == nj - 1)
        def _():
            f = acc_ref[...]
            xo = x_ref[...] + 0.5 * vec_ref[3:4, :] * f
            e = _adaln(xo, fvec_ref) - t_ref[...]
            dvec_ref[4:5, :] += (0.5 / d) * jnp.sum(e * e, axis=0, keepdims=True)
            dx = _adaln_bwd(xo, e * (1.0 / d), fvec_ref, dvec_ref)
            dx_ref[...] = dx
            df_ref[...] = (0.5 * vec_ref[3:4, :] * dx).astype(BF16)
            dgt_ref[2:3, :] += 0.5 * jnp.sum(dx * f, axis=0, keepdims=True)

        @pl.when((i == nt - 1) & (j == nj - 1))
        def _():
            _adaln_finish(fvec_ref, dvec_ref)

    tile = pl.BlockSpec((tm, d), lambda i, j: (i, 0))
    tab = pl.BlockSpec((8, d), lambda i, j: (0, 0))
    return pl.pallas_call(
        body, name=name, grid=(nt, nj),
        in_specs=[tile, tab,
                  pl.BlockSpec((2, None, fb, d), lambda i, j: (0, j, 0, 0)),
                  pl.BlockSpec((fb, d), lambda i, j: (j, 0)),
                  pl.BlockSpec((tm, d), lambda i, j: (jnp.where(j == nj - 1, i, jnp.maximum(i - 1, 0)), 0)), tab],
        out_specs=[tile, pl.BlockSpec((2, None, tm, fb), lambda i, j: (0, j, i, 0)), tile, tab, tile, tab],
        out_shape=[jax.ShapeDtypeStruct((t, d), BF16), jax.ShapeDtypeStruct((2, nj, t, fb), BF16),
                   jax.ShapeDtypeStruct((t, d), F32), jax.ShapeDtypeStruct((8, d), F32),
                   jax.ShapeDtypeStruct((t, d), BF16), jax.ShapeDtypeStruct((8, d), F32)],
        scratch_shapes=[pltpu.VMEM((tm, d), F32)],
        compiler_params=_cp(("arbitrary", "arbitrary"), 56),
    )(x, vec, wi, wo, tgt, fvec)


def _mix_in(x, vec, win, tm, name, host=None):
    t, d = x.shape
    nb, _, cb = win.shape

    def body(ins, outs, scr):
        x_ref, vec_ref, w_ref = ins
        h_ref, p_ref = outs
        h = _adaln(x_ref[...], vec_ref).astype(BF16)
        h_ref[...] = h
        for k in range(nb):
            p_ref[:, k * cb:(k + 1) * cb] = _dot(h, w_ref[k])

    return _hosted_call(
        body, host, name, (t // tm,),
        [pl.BlockSpec((tm, d), lambda i: (i, 0)),
         pl.BlockSpec((8, d), lambda i: (0, 0)),
         pl.BlockSpec((nb, d, cb), lambda i: (0, 0, 0))],
        [pl.BlockSpec((tm, d), lambda i: (i, 0)),
         pl.BlockSpec((tm, nb * cb), lambda i: (i, 0))],
        [jax.ShapeDtypeStruct((t, d), BF16), jax.ShapeDtypeStruct((t, nb * cb), F32)],
        [], ("arbitrary",), 48, (x, vec, win))


def _conv_fwd(proj, cw32, name, host=None):
    t = proj.shape[0]
    nblk = cw32.shape[1] // LANES
    ch = min(t, 128)

    def body(ins, outs, scr):
        val_ref, gate_ref, cw_ref = ins
        cv_ref, = outs
        ext_ref, = scr
        ext_ref[0:32, :] = jnp.zeros((32, LANES), F32)
        ext_ref[32:, :] = val_ref[...] * _sigmoid(gate_ref[...])
        for r in range(t // ch):
            acc = jnp.broadcast_to(cw_ref[31:32, :], (ch, LANES))
            for k in range(CONV_W):
                off = 32 + r * ch - (CONV_W - 1 - k)
                acc = acc + cw_ref[k:k + 1, :] * ext_ref[off:off + ch, :]
            cv_ref[r * ch:(r + 1) * ch, :] = acc

    return _hosted_call(
        body, host, name, (nblk,),
        [pl.BlockSpec((t, LANES), lambda c: (0, c)),
         pl.BlockSpec((t, LANES), lambda c: (0, nblk + c)),
         pl.BlockSpec((32, LANES), lambda c: (0, c))],
        [pl.BlockSpec((t, LANES), lambda c: (0, c))],
        [jax.ShapeDtypeStruct((t, nblk * LANES), F32)],
        [pltpu.VMEM((t + 32, LANES), F32)], ("arbitrary",), 48, (proj, proj, cw32))


def _rnn_fwd(proj, rw8, rvec, wab, wib, name, host=None):
    t = proj.shape[0]
    nblk = rvec.shape[1] // LANES

    def body(ins, outs, scr):
        ux_ref, uy_ref, rw_ref, rvec_ref, wa_ref, wi_ref = ins
        h_ref, yr_ref = outs
        ext_ref, sa_ref, sb_ref, carry_ref = scr
        xr = _rnn_conv(ux_ref[...], rw_ref, rvec_ref, ext_ref)
        _, _, ig, _, a, mult = _rglru_gates(xr, wa_ref, wi_ref, rvec_ref)
        _tiled_scan(a, mult * (ig * xr), False, sa_ref, sb_ref, carry_ref, h_ref)
        ge, _ = _gelu_and_grad(uy_ref[...])
        yr_ref[...] = (ge * h_ref[...]).astype(BF16)

    blk = lambda off: pl.BlockSpec((t, LANES), lambda c: (0, off + c))
    return _hosted_call(
        body, host, name, (nblk,),
        [blk(2 * nblk), blk(3 * nblk),
         pl.BlockSpec((8, LANES), lambda c: (0, c)),
         pl.BlockSpec((8, LANES), lambda c: (0, c)),
         pl.BlockSpec((None, LANES, LANES), lambda c: (c, 0, 0)),
         pl.BlockSpec((None, LANES, LANES), lambda c: (c, 0, 0))],
        [blk(0), blk(0)],
        [jax.ShapeDtypeStruct((t, nblk * LANES), F32), jax.ShapeDtypeStruct((t, nblk * LANES), BF16)],
        [pltpu.VMEM((t + 8, LANES), F32), pltpu.VMEM((t, LANES), F32), pltpu.VMEM((t, LANES), F32),
         pltpu.VMEM((t // SUBLANES, LANES), F32)], ("arbitrary",), 56, (proj, proj, rw8, rvec, wab, wib))


def _ln_silu(cv, lnv_ref):
    mu = jnp.mean(cv, axis=-1, keepdims=True)
    xc = cv - mu
    rs = lax.rsqrt(jnp.mean(xc * xc, axis=-1, keepdims=True) + EPS)
    chat = xc * rs
    z = chat * lnv_ref[0:1, :] + lnv_ref[1:2, :]
    sg = _sigmoid(z)
    return rs, chat, z, sg


def _mix_out(x, cv, yr, vec, lnv, wout, tm, name, host=None):
    t, d = x.shape
    dc = cv.shape[1]

    def body(ins, outs, scr):
        x_ref, cv_ref, yr_ref, vec_ref, lnv_ref, w_ref = ins
        xo_ref, ym_ref, yc_ref = outs
        _, _, z, sg = _ln_silu(cv_ref[...], lnv_ref)
        yc = (z * sg).astype(BF16)
        yr = yr_ref[...]
        yc_ref[:, 0:dc] = yc
        yc_ref[:, dc:] = yr
        ym = _dot(yc, w_ref[0:dc, :]) + _dot(yr, w_ref[dc:, :])
        ym_ref[...] = ym.astype(BF16)
        xo_ref[...] = x_ref[...] + vec_ref[3:4, :] * ym

    tile = pl.BlockSpec((tm, d), lambda i: (i, 0))
    return _hosted_call(
        body, host, name, (t // tm,),
        [tile,
         pl.BlockSpec((tm, dc), lambda i: (i, 0)),
         pl.BlockSpec((tm, dc), lambda i: (i, 0)),
         pl.BlockSpec((8, d), lambda i: (0, 0)),
         pl.BlockSpec((8, dc), lambda i: (0, 0)),
         pl.BlockSpec((d, d), lambda i: (0, 0))],
        [tile, tile, tile],
        [jax.ShapeDtypeStruct((t, d), F32), jax.ShapeDtypeStruct((t, d), BF16), jax.ShapeDtypeStruct((t, d), BF16)],
        [], ("arbitrary",), 48, (x, cv, yr, vec, lnv, wout))


def _emit_df(dx, f_ref, nvec_ref, df_ref, dgt_ref):
    df_ref[...] = (0.5 * nvec_ref[3:4, :] * dx).astype(BF16)
    dgt_ref[2:3, :] += 0.5 * jnp.sum(dx * f_ref[...].astype(F32), axis=0, keepdims=True)


def _ffn_bwd_w(df, gu, h, wo, tm, name, host=None):
    t, d = df.shape
    nj, fb = gu.shape[1], gu.shape[3]
    nt = t // tm
    sub = min(tm, ROW_GROUP)

    def body(ins, outs, scr):
        df_ref, gu_ref, h_ref, wo_ref = ins
        dgu_ref, dwi_ref, dwo_ref = outs
        accg_ref, accu_ref, acco_ref, dact_ref, act_ref = scr
        i = pl.program_id(1)

        @pl.when(i == 0)
        def _():
            accg_ref[...] = jnp.zeros_like(accg_ref)
            accu_ref[...] = jnp.zeros_like(accu_ref)
            acco_ref[...] = jnp.zeros_like(acco_ref)

        dact_ref[...] = _dot_nt(df_ref[...], wo_ref[...])
        for r in range(tm // sub):
            rows = slice(r * sub, (r + 1) * sub)
            g = gu_ref[0, rows, :].astype(F32)
            u = gu_ref[1, rows, :].astype(F32)
            dact = dact_ref[rows, :]
            sg = _sigmoid(g)
            sl = g * sg
            dgu_ref[0, rows, :] = (dact * u * (sg * (1.0 + g * (1.0 - sg)))).astype(BF16)
            dgu_ref[1, rows, :] = (dact * sl).astype(BF16)
            act_ref[rows, :] = (sl * u).astype(BF16)
        hb = h_ref[...]
        acco_ref[...] += _dot_tn(act_ref[...], df_ref[...])
        accg_ref[...] += _dot_tn(dgu_ref[0], hb)
        accu_ref[...] += _dot_tn(dgu_ref[1], hb)

        @pl.when(i == nt - 1)
        def _():
            dwi_ref[0] = accg_ref[...].astype(BF16)
            dwi_ref[1] = accu_ref[...].astype(BF16)
            dwo_ref[...] = acco_ref[...].astype(BF16)

    tile = pl.BlockSpec((tm, d), lambda j, i: (i, 0))
    return _hosted_call(
        body, host, name, (nj, nt),
        [tile,
         pl.BlockSpec((2, None, tm, fb), lambda j, i: (0, j, i, 0)),
         tile,
         pl.BlockSpec((fb, d), lambda j, i: (j, 0))],
        [pl.BlockSpec((2, None, tm, fb), lambda j, i: (0, j, i, 0)),
         pl.BlockSpec((2, None, fb, d), lambda j, i: (0, j, 0, 0)),
         pl.BlockSpec((None, fb, d), lambda j, i: (j, 0, 0))],
        [jax.ShapeDtypeStruct((2, nj, t, fb), BF16), jax.ShapeDtypeStruct((2, nj, fb, d), BF16),
         jax.ShapeDtypeStruct((nj, fb, d), BF16)],
        [pltpu.VMEM((fb, d), F32), pltpu.VMEM((fb, d), F32), pltpu.VMEM((fb, d), F32),
         pltpu.VMEM((tm, fb), F32), pltpu.VMEM((tm, fb), BF16)],
        ("arbitrary", "arbitrary"), 56, (df, gu, h, wo))


def _ffn_bwd_in(dxo, x, vec, dgu, wi, tm, name, host=None):
    t, d = x.shape
    nj, fb = wi.shape[1], wi.shape[2]
    nt = t // tm

    def body(ins, outs, scr):
        dxo_ref, x_ref, vec_ref, dgu_ref, wi_ref = ins
        dx_ref, dvec_ref = outs
        i = pl.program_id(0)

        @pl.when(i == 0)
        def _():
            dvec_ref[...] = jnp.zeros_like(dvec_ref)

        dh = jnp.zeros((tm, d), F32)
        for a in range(2):
            for k in range(nj):
                dh = dh + _dot(dgu_ref[a, k], wi_ref[a, k])
        dx_ref[...] = dxo_ref[...] + _adaln_bwd(x_ref[...], dh, vec_ref, dvec_ref)

        @pl.when(i == nt - 1)
        def _():
            _adaln_finish(vec_ref, dvec_ref)

    tile = pl.BlockSpec((tm, d), lambda i: (i, 0))
    return _hosted_call(
        body, host, name, (nt,),
        [tile, tile,
         pl.BlockSpec((8, d), lambda i: (0, 0)),
         pl.BlockSpec((2, nj, tm, fb), lambda i: (0, 0, i, 0)),
         pl.BlockSpec((2, nj, fb, d), lambda i: (0, 0, 0, 0))],
        [tile, pl.BlockSpec((8, d), lambda i: (0, 0))],
        [jax.ShapeDtypeStruct((t, d), F32), jax.ShapeDtypeStruct((8, d), F32)],
        [], ("arbitrary",), 60, (dxo, x, vec, dgu, wi))


def _mm_tn(a, b, a_spec, b_spec, nblk, nk, m, n, name):
    def body(a_ref, b_ref, o_ref, acc_ref):
        s = pl.program_id(1)

        @pl.when(s == 0)
        def _():
            acc_ref[...] = jnp.zeros_like(acc_ref)

        acc_ref[...] += _dot_tn(a_ref[...], b_ref[...])

        @pl.when(s == nk - 1)
        def _():
            o_ref[...] = acc_ref[...].astype(BF16)

    return pl.pallas_call(
        body, name=name, grid=(nblk, nk),
        in_specs=[a_spec, b_spec],
        out_specs=pl.BlockSpec((None, m, n), lambda k, s: (k, 0, 0)),
        out_shape=jax.ShapeDtypeStruct((nblk, m, n), BF16),
        scratch_shapes=[pltpu.VMEM((m, n), F32)],
        compiler_params=_cp(("arbitrary", "arbitrary"), 56),
    )(a, b)


def _wgrad_in(h, parts, cb, tk, name):
    t, d = h.shape
    dc = parts[0].shape[1]
    per = dc // cb
    nblk = len(parts) * per
    nk = t // tk

    def body(h_ref, p0, p1, p2, p3, o_ref, acc_ref):
        s = pl.program_id(0)

        @pl.when(s == 0)
        def _():
            acc_ref[...] = jnp.zeros_like(acc_ref)

        hb = h_ref[...]
        for p, p_ref in enumerate((p0, p1, p2, p3)):
            acc_ref[p] += _dot_tn(hb, p_ref[...])

        @pl.when(s == nk - 1)
        def _():
            for k in range(nblk):
                o_ref[k] = acc_ref[k // per, :, (k % per) * cb:(k % per + 1) * cb].astype(BF16)

    return pl.pallas_call(
        body, name=name, grid=(nk,),
        in_specs=[pl.BlockSpec((tk, d), lambda s: (s, 0))] + [pl.BlockSpec((tk, dc), lambda s: (s, 0))] * len(parts),
        out_specs=pl.BlockSpec((nblk, d, cb), lambda s: (0, 0, 0)),
        out_shape=jax.ShapeDtypeStruct((nblk, d, cb), BF16),
        scratch_shapes=[pltpu.VMEM((len(parts), d, dc), F32)],
        compiler_params=_cp(("arbitrary",), 56),
    )(h, *parts)


def _mixout_bwd(dxo, ym, ycat, cv, hr, proj, vec, lnv, wout, tm, name, host=None):
    t, d = dxo.shape
    dc = cv.shape[1]
    nt = t // tm

    def body(ins, outs, scr):
        dxo_ref, ym_ref, yc_ref, cv_ref, hr_ref, uy_ref, vec_ref, lnv_ref, w_ref = ins
        dw_ref, dcv_ref, dhr_ref, duy_ref, dln_ref, dgt_ref = outs
        acc_ref, = scr
        i = pl.program_id(0)

        @pl.when(i == 0)
        def _():
            dln_ref[...] = jnp.zeros_like(dln_ref)
            dgt_ref[...] = jnp.zeros_like(dgt_ref)
            acc_ref[...] = jnp.zeros_like(acc_ref)

        dxo_v = dxo_ref[...]
        dym = (vec_ref[3:4, :] * dxo_v).astype(BF16)
        acc_ref[...] += _dot_tn(yc_ref[...], dym)
        dgt_ref[0:1, :] += jnp.sum(dxo_v * ym_ref[...].astype(F32), axis=0, keepdims=True)
        dyc = _dot_nt(dym, w_ref[0:dc, :])
        dyr = _dot_nt(dym, w_ref[dc:, :])
        rs, chat, z, sg = _ln_silu(cv_ref[...], lnv_ref)
        dz = dyc * (sg * (1.0 + z * (1.0 - sg)))
        dln_ref[0:1, :] += jnp.sum(dz * chat, axis=0, keepdims=True)
        dln_ref[1:2, :] += jnp.sum(dz, axis=0, keepdims=True)
        dchat = dz * lnv_ref[0:1, :]
        dcv_ref[...] = (rs * (dchat - jnp.mean(dchat, axis=-1, keepdims=True)
                              - chat * jnp.mean(dchat * chat, axis=-1, keepdims=True))).astype(BF16)
        ge, dge = _gelu_and_grad(uy_ref[...])
        dhr_ref[...] = (dyr * ge).astype(BF16)
        duy_ref[...] = (dyr * hr_ref[...] * dge).astype(BF16)

        @pl.when(i == nt - 1)
        def _():
            dw_ref[...] = acc_ref[...].astype(BF16)

    tile_d = pl.BlockSpec((tm, d), lambda i: (i, 0))
    tile_c = pl.BlockSpec((tm, dc), lambda i: (i, 0))
    full_w = pl.BlockSpec((d, d), lambda i: (0, 0))
    return _hosted_call(
        body, host, name, (nt,),
        [tile_d, tile_d, tile_d, tile_c, tile_c,
         pl.BlockSpec((tm, dc), lambda i: (i, 3)),
         pl.BlockSpec((8, d), lambda i: (0, 0)),
         pl.BlockSpec((8, dc), lambda i: (0, 0)),
         full_w],
        [full_w, tile_c, tile_c, tile_c,
         pl.BlockSpec((8, dc), lambda i: (0, 0)),
         pl.BlockSpec((8, d), lambda i: (0, 0))],
        [jax.ShapeDtypeStruct((d, d), BF16), jax.ShapeDtypeStruct((t, dc), BF16),
         jax.ShapeDtypeStruct((t, dc), BF16), jax.ShapeDtypeStruct((t, dc), BF16),
         jax.ShapeDtypeStruct((8, dc), F32), jax.ShapeDtypeStruct((8, d), F32)],
        [pltpu.VMEM((d, d), F32)], ("arbitrary",), 48, (dxo, ym, ycat, cv, hr, proj, vec, lnv, wout))


def _conv_bwd(proj, dcv, cw32, name):
    t = proj.shape[0]
    nblk = cw32.shape[1] // LANES
    ch = min(t, 128)

    def body(val_ref, gate_ref, dcv_ref, cw_ref, dval_ref, dgate_ref, dcw_ref, extu_ref, extd_ref):
        val = val_ref[...]
        sg = _sigmoid(gate_ref[...])
        extu_ref[0:32, :] = jnp.zeros((32, LANES), F32)
        extu_ref[32:, :] = val * sg
        dcv_v = dcv_ref[...].astype(F32)
        extd_ref[0:t, :] = dcv_v
        extd_ref[t:, :] = jnp.zeros((32, LANES), F32)
        for r in range(t // ch):
            acc = jnp.zeros((ch, LANES), F32)
            for k in range(CONV_W):
                off = r * ch + (CONV_W - 1 - k)
                acc = acc + cw_ref[k:k + 1, :] * extd_ref[off:off + ch, :]
            rows = slice(r * ch, (r + 1) * ch)
            sg_r = _sigmoid(gate_ref[rows, :])
            dval_ref[rows, :] = (acc * sg_r).astype(BF16)
            dgate_ref[rows, :] = (acc * val_ref[rows, :] * sg_r * (1.0 - sg_r)).astype(BF16)
        for k in range(CONV_W):
            off = 32 - (CONV_W - 1 - k)
            dcw_ref[k:k + 1, :] = jnp.sum(dcv_v * extu_ref[off:off + t, :], axis=0, keepdims=True)
        dcw_ref[31:32, :] = jnp.sum(dcv_v, axis=0, keepdims=True)

    blk = lambda off: pl.BlockSpec((t, LANES), lambda c: (0, off + c))
    return pl.pallas_call(
        body, name=name, grid=(nblk,),
        in_specs=[blk(0), blk(nblk), blk(0), pl.BlockSpec((32, LANES), lambda c: (0, c))],
        out_specs=[blk(0), blk(0), pl.BlockSpec((32, LANES), lambda c: (0, c))],
        out_shape=[jax.ShapeDtypeStruct((t, nblk * LANES), BF16), jax.ShapeDtypeStruct((t, nblk * LANES), BF16),
                   jax.ShapeDtypeStruct((32, nblk * LANES), F32)],
        scratch_shapes=[pltpu.VMEM((t + 32, LANES), F32), pltpu.VMEM((t + 32, LANES), F32)],
        compiler_params=_cp(("arbitrary",), 56),
    )(proj, proj, dcv, cw32)


def _rnn_bwd(proj, hr, dhr, rw8, rvec, wab, wib, name, host=None):
    t = proj.shape[0]
    nblk = rvec.shape[1] // LANES

    def body(ins, outs, scr):
        ux_ref, h_ref, dh_ref, rw_ref, rvec_ref, wa_ref, wi_ref = ins
        dux_ref, sm_ref, dwa_ref, dwi_ref = outs
        ext_ref, extd_ref, sa_ref, sb_ref, carry_ref = scr
        xr = _rnn_conv(ux_ref[...], rw_ref, rvec_ref, ext_ref)
        xb, r, ig, ls, a, mult = _rglru_gates(xr, wa_ref, wi_ref, rvec_ref)
        row = lax.broadcasted_iota(jnp.int32, (t, LANES), 0)
        a_next = jnp.where(row < t - 1, pltpu.roll(a, t - 1, 0), 0.0)
        _tiled_scan(a_next, dh_ref[...].astype(F32), True, sa_ref, sb_ref, carry_ref, extd_ref)
        g = extd_ref[0:t, :]
        hprev = jnp.where(row >= 1, pltpu.roll(h_ref[...], 1, 0), 0.0)
        da = g * hprev
        dmult = g * (ig * xr)
        dig = g * mult * xr
        dxr = g * mult * ig
        dlog_a = a * (da - dmult * a / mult)
        dr = dlog_a * (RG_C * ls)
        dls = RG_C * jnp.sum(dlog_a * r, axis=0, keepdims=True)
        dpr = dr * r * (1.0 - r)
        dpi = dig * ig * (1.0 - ig)
        dprb = dpr.astype(BF16)
        dpib = dpi.astype(BF16)
        dxr = dxr + _dot_nt(dprb, wa_ref[...]) + _dot_nt(dpib, wi_ref[...])
        dwa_ref[...] = _dot_tn(xb, dprb)
        dwi_ref[...] = _dot_tn(xb, dpib)
        extd_ref[0:t, :] = dxr
        extd_ref[t:, :] = jnp.zeros((8, LANES), F32)
        dux = rw_ref[RNN_CONV_W - 1:RNN_CONV_W, :] * dxr
        for k in range(RNN_CONV_W - 1):
            d = RNN_CONV_W - 1 - k
            dux = dux + rw_ref[k:k + 1, :] * extd_ref[d:d + t, :]
        dux_ref[...] = dux.astype(BF16)
        for k in range(RNN_CONV_W):
            d = RNN_CONV_W - 1 - k
            sm_ref[k:k + 1, :] = jnp.sum(dxr * ext_ref[8 - d:8 - d + t, :], axis=0, keepdims=True)
        sm_ref[4:5, :] = jnp.sum(dxr, axis=0, keepdims=True)
        sm_ref[5:6, :] = jnp.sum(dpr, axis=0, keepdims=True)
        sm_ref[6:7, :] = jnp.sum(dpi, axis=0, keepdims=True)
        sm_ref[7:8, :] = dls * _sigmoid(-rvec_ref[3:4, :])

    blk = lambda off: pl.BlockSpec((t, LANES), lambda c: (0, off + c))
    sq = pl.BlockSpec((None, LANES, LANES), lambda c: (c, 0, 0))
    return _hosted_call(
        body, host, name, (nblk,),
        [blk(2 * nblk), blk(0), blk(0),
         pl.BlockSpec((8, LANES), lambda c: (0, c)),
         pl.BlockSpec((8, LANES), lambda c: (0, c)), sq, sq],
        [blk(0), pl.BlockSpec((8, LANES), lambda c: (0, c)), sq, sq],
        [jax.ShapeDtypeStruct((t, nblk * LANES), BF16), jax.ShapeDtypeStruct((8, nblk * LANES), F32),
         jax.ShapeDtypeStruct((nblk, LANES, LANES), F32), jax.ShapeDtypeStruct((nblk, LANES, LANES), F32)],
        [pltpu.VMEM((t + 8, LANES), F32), pltpu.VMEM((t + 8, LANES), F32), pltpu.VMEM((t, LANES), F32),
         pltpu.VMEM((t, LANES), F32), pltpu.VMEM((t // SUBLANES, LANES), F32)],
        ("arbitrary",), 60, (proj, hr, dhr, rw8, rvec, wab, wib))


def _mixin_bwd(dxo, x, parts, vec, win, h, f, nvec, tm, name):
    t, d = x.shape
    nb, _, cb = win.shape
    dc = parts[0].shape[1]
    per = dc // cb
    nt = t // tm

    def body(dxo_ref, x_ref, p0, p1, p2, p3, vec_ref, w_ref, h_ref, f_ref, nvec_ref,
             dx_ref, dvec_ref, df_ref, dgt_ref, dw_ref, acc_ref):
        i = pl.program_id(0)

        @pl.when(i == 0)
        def _():
            dvec_ref[...] = jnp.zeros_like(dvec_ref)
            dgt_ref[...] = jnp.zeros_like(dgt_ref)
            acc_ref[...] = jnp.zeros_like(acc_ref)

        prefs = (p0, p1, p2, p3)
        hb = h_ref[...]
        for p in range(len(prefs)):
            acc_ref[p] += _dot_tn(hb, prefs[p][...])
        dh = jnp.zeros((tm, d), F32)
        for k in range(nb):
            dh = dh + _dot_nt(prefs[k // per][:, (k % per) * cb:(k % per + 1) * cb], w_ref[k])
        dx = dxo_ref[...] + _adaln_bwd(x_ref[...], dh, vec_ref, dvec_ref)
        dx_ref[...] = dx
        _emit_df(dx, f_ref, nvec_ref, df_ref, dgt_ref)

        @pl.when(i == nt - 1)
        def _():
            _adaln_finish(vec_ref, dvec_ref)
            for k in range(nb):
                dw_ref[k] = acc_ref[k // per, :, (k % per) * cb:(k % per + 1) * cb].astype(BF16)

    tile_d = pl.BlockSpec((tm, d), lambda i: (i, 0))
    tile_c = pl.BlockSpec((tm, dc), lambda i: (i, 0))
    tab = pl.BlockSpec((8, d), lambda i: (0, 0))
    wspec = pl.BlockSpec((nb, d, cb), lambda i: (0, 0, 0))
    return pl.pallas_call(
        body, name=name, grid=(nt,),
        in_specs=[tile_d, tile_d, tile_c, tile_c, tile_c, tile_c, tab, wspec, tile_d, tile_d, tab],
        out_specs=[tile_d, tab, tile_d, tab, wspec],
        out_shape=[jax.ShapeDtypeStruct((t, d), F32), jax.ShapeDtypeStruct((8, d), F32),
                   jax.ShapeDtypeStruct((t, d), BF16), jax.ShapeDtypeStruct((8, d), F32),
                   jax.ShapeDtypeStruct((nb, d, cb), BF16)],
        scratch_shapes=[pltpu.VMEM((len(parts), d, dc), F32)],
        compiler_params=_cp(("arbitrary",), 56),
    )(dxo, x, *parts, vec, win, h, f, nvec)


def _coords():
    return lax.axis_index("x"), lax.axis_index("y"), lax.axis_index("c")


def _flip(v, bit):
    return 1 - v if bit else v


def _gather_copy(outs, send_sems, recv_sems, a, k, block, to, src=None):
    dst = outs[a].at[block]
    return pltpu.make_async_remote_copy(
        src_ref=dst if src is None else src, dst_ref=dst,
        send_sem=send_sems.at[a, k], recv_sem=recv_sems.at[a, k],
        device_id=to, device_id_type=MESH_IDS)


def _gather_start(ins, outs, send_sems, recv_sems, loc_sems):
    x, y, c = _coords()
    me = 4 * x + 2 * y + c
    for a in range(len(ins)):
        pltpu.make_async_copy(ins[a], outs[a].at[me], loc_sems.at[a]).start()
    for a in range(len(ins)):
        _gather_copy(outs, send_sems, recv_sems, a, 0, me, (x, y, 1 - c), src=ins[a]).start()
        for j, (cx, cy) in enumerate([(1 - x, y), (x, 1 - y), (1 - x, 1 - y)]):
            _gather_copy(outs, send_sems, recv_sems, a, 1 + j, me, (cx, cy, c), src=ins[a]).start()


def _gather_finish(ins, outs, send_sems, recv_sems, loc_sems):
    x, y, c = _coords()
    me = 4 * x + 2 * y + c
    sib = (x, y, 1 - c)
    chips = [(1 - x, y), (x, 1 - y), (1 - x, 1 - y)]
    n = len(ins)
    for a in range(n):
        for j, (cx, cy) in enumerate(chips):
            blk = 4 * cx + 2 * cy + c
            _gather_copy(outs, send_sems, recv_sems, a, 1 + j, blk, sib).wait_recv()
            _gather_copy(outs, send_sems, recv_sems, a, 4 + j, blk, sib).start()
    for a in range(n):
        _gather_copy(outs, send_sems, recv_sems, a, 0, 4 * x + 2 * y + (1 - c), sib).wait_recv()
        for j, (cx, cy) in enumerate(chips):
            _gather_copy(outs, send_sems, recv_sems, a, 4 + j, 4 * cx + 2 * cy + (1 - c), sib).wait_recv()
    for a in range(n):
        _gather_copy(outs, send_sems, recv_sems, a, 0, me, sib, src=ins[a]).wait_send()
        for j, (cx, cy) in enumerate(chips):
            _gather_copy(outs, send_sems, recv_sems, a, 1 + j, me, (cx, cy, c), src=ins[a]).wait_send()
            _gather_copy(outs, send_sems, recv_sems, a, 4 + j, 4 * cx + 2 * cy + c, sib).wait_send()
        pltpu.make_async_copy(ins[a], outs[a].at[me], loc_sems.at[a]).wait()


def _gather_shapes(shards):
    return [jax.ShapeDtypeStruct((NDEV,) + s.shape, s.dtype) for s in shards]


def _gather_sems(n):
    return [pltpu.SemaphoreType.DMA((n, 7)), pltpu.SemaphoreType.DMA((n, 7)), pltpu.SemaphoreType.DMA((n,))]


def _sibling_copies(ins, outs, send_sems, recv_sems):
    x, y, c = _coords()
    return [pltpu.make_async_remote_copy(
        src_ref=ins[a].at[2 * q + (1 - c)], dst_ref=outs[a].at[q],
        send_sem=send_sems.at[a, q], recv_sem=recv_sems.at[a, q],
        device_id=(x, y, 1 - c), device_id_type=MESH_IDS) for a in range(len(ins)) for q in range(4)]


def _sibling_shapes(parts):
    return [jax.ShapeDtypeStruct((4,) + p.shape[1:], p.dtype) for p in parts]


def _chips_copies(ins, outs, send_sems, recv_sems):
    x, y, c = _coords()
    copies = []
    for a in range(len(ins)):
        for k, (kx, ky) in enumerate([(1, 0), (0, 1), (1, 1)]):
            tx, ty = _flip(x, kx), _flip(y, ky)
            copies.append(pltpu.make_async_remote_copy(
                src_ref=ins[a].at[2 * tx + ty], dst_ref=outs[a].at[k],
                send_sem=send_sems.at[a, k], recv_sem=recv_sems.at[a, k],
                device_id=(tx, ty, c), device_id_type=MESH_IDS))
    return copies


def _chips_shapes(sums):
    return [jax.ShapeDtypeStruct((3,) + s.shape[1:], s.dtype) for s in sums]


def _direct_copies(ins, outs, send_sems, recv_sems):
    x, y, c = _coords()
    me = 4 * x + 2 * y + c
    copies = []
    for a in range(len(ins)):
        for k in range(1, NDEV):
            kx, ky, kc = (k >> 2) & 1, (k >> 1) & 1, k & 1
            copies.append(pltpu.make_async_remote_copy(
                src_ref=ins[a], dst_ref=outs[a].at[me],
                send_sem=send_sems.at[a, k - 1], recv_sem=recv_sems.at[a, k - 1],
                device_id=(_flip(x, kx), _flip(y, ky), _flip(c, kc)), device_id_type=MESH_IDS))
    return copies


class _Exchange:
    def __init__(self, kind, arrays):
        self.kind, self.arrays, self.n = kind, list(arrays), len(arrays)

    def out_shapes(self):
        return {"gather": _gather_shapes, "direct": _gather_shapes, "sibling": _sibling_shapes,
                "chips": _chips_shapes}[self.kind](self.arrays)

    def sems(self):
        if self.kind in ("gather", "direct"):
            return _gather_sems(self.n)
        k = {"sibling": 4, "chips": 3}[self.kind]
        return [pltpu.SemaphoreType.DMA((self.n, k)), pltpu.SemaphoreType.DMA((self.n, k))]

    def _copies(self, ins, outs, sems):
        if self.kind == "direct":
            x, y, c = _coords()
            own = [pltpu.make_async_copy(ins[a], outs[a].at[4 * x + 2 * y + c], sems[2].at[a]) for a in range(self.n)]
            return own + _direct_copies(ins, outs, sems[0], sems[1])
        return {"sibling": _sibling_copies, "chips": _chips_copies}[self.kind](ins, outs, *sems)

    def start(self, ins, outs, sems):
        if self.kind == "gather":
            _gather_start(ins, outs, *sems)
        else:
            for cpy in self._copies(ins, outs, sems):
                cpy.start()

    def finish(self, ins, outs, sems):
        if self.kind == "gather":
            _gather_finish(ins, outs, *sems)
        else:
            for cpy in self._copies(ins, outs, sems):
                cpy.wait()


def _hosted_call(body, host, name, grid, in_specs, out_specs, out_shape, scratch, sem, vmem_mb, args, manual=False):
    n = host.n if host else 0
    ni, no, ns = len(in_specs), len(out_specs), len(scratch)

    def full(*refs):
        ins, h_in = refs[:ni], refs[ni:ni + n]
        outs, h_out = refs[ni + n:ni + n + no], refs[ni + n + no:ni + 2 * n + no]
        scr, sems = refs[ni + 2 * n + no:ni + 2 * n + no + ns], refs[ni + 2 * n + no + ns:]
        if manual:
            body(ins, outs, scr, lambda: host.start(h_in, h_out, sems), lambda: host.finish(h_in, h_out, sems))
            return
        if host and grid:
            first = functools.reduce(lambda a, b: a & b, [pl.program_id(k) == 0 for k in range(len(grid))])
            last = functools.reduce(lambda a, b: a & b, [pl.program_id(k) == g - 1 for k, g in enumerate(grid)])

            @pl.when(first)
            def _():
                host.start(h_in, h_out, sems)
        elif host:
            host.start(h_in, h_out, sems)

        body(ins, outs, scr)

        if host and grid:
            @pl.when(last)
            def _():
                host.finish(h_in, h_out, sems)
        elif host:
            host.finish(h_in, h_out, sems)

    anyspec = pl.BlockSpec(memory_space=pl.ANY)
    return pl.pallas_call(
        full, name=name, grid=grid,
        in_specs=list(in_specs) + [anyspec] * n, out_specs=list(out_specs) + [anyspec] * n,
        out_shape=list(out_shape) + (host.out_shapes() if host else []),
        scratch_shapes=list(scratch) + (host.sems() if host else []),
        compiler_params=_cp(sem, vmem_mb),
    )(*args, *(host.arrays if host else []))


def _exchange(host, name, after=()):
    n, na = host.n, len(after)

    def body(*refs):
        ins, outs, sems = refs[:n], refs[n + na:2 * n + na], refs[2 * n + na:]
        host.start(ins, outs, sems)
        host.finish(ins, outs, sems)

    anyspec = pl.BlockSpec(memory_space=pl.ANY)
    return pl.pallas_call(
        body, name=name, in_specs=[anyspec] * (n + na), out_specs=[anyspec] * n,
        out_shape=host.out_shapes(), scratch_shapes=host.sems(),
    )(*host.arrays, *after)


def _chips_split_start(sums, name):
    n = len(sums)
    hbm = pl.BlockSpec(memory_space=pltpu.HBM)
    sem = pl.BlockSpec(memory_space=pltpu.SEMAPHORE)

    def body(*refs):
        ins, lands = refs[:n], refs[n:2 * n]
        sems = refs[2 * n:2 * n + 6 * n]
        token = refs[-1]
        for cpy in _chips_copies(ins, lands, _SemGrid(sems[:3 * n], 3), _SemGrid(sems[3 * n:], 3)):
            cpy.start()
        token[...] = jnp.zeros_like(token)

    land_shapes = _chips_shapes(sums)
    lands = [pltpu.with_memory_space_constraint(lax.empty(s.shape, s.dtype), pltpu.HBM) for s in land_shapes]
    return pl.pallas_call(
        body, name=name,
        out_shape=(*[pltpu.SemaphoreType.DMA(())] * (6 * n),
                   *[pltpu.HBM(s.shape, s.dtype) for s in sums],
                   *[pltpu.HBM(s.shape, s.dtype) for s in land_shapes],
                   jax.ShapeDtypeStruct((8, LANES), F32)),
        in_specs=[hbm] * (2 * n),
        out_specs=(*[sem] * (6 * n), *[hbm] * (2 * n), pl.BlockSpec(memory_space=pltpu.VMEM)),
        input_output_aliases={i: 6 * n + i for i in range(2 * n)},
        compiler_params=pltpu.CompilerParams(has_side_effects=pltpu.SideEffectType.DATAFLOW_SIDE_EFFECTING),
    )(*[pltpu.with_memory_space_constraint(s, pltpu.HBM) for s in sums], *lands)


class _SemGrid:
    def __init__(self, sems, k):
        self.sems, self.k = sems, k

    @property
    def at(self):
        return self

    def __getitem__(self, idx):
        return self.sems[idx[0] * self.k + idx[1]]


def _chips_split_wait(started, n, after, name):
    sems = started[:6 * n]
    thru = started[6 * n:8 * n]
    hbm = pl.BlockSpec(memory_space=pltpu.HBM)
    sem = pl.BlockSpec(memory_space=pltpu.SEMAPHORE)

    def body(*refs):
        ins, lands = refs[:n], refs[n:2 * n]
        s = refs[2 * n:2 * n + 6 * n]
        for cpy in _chips_copies(ins, lands, _SemGrid(s[:3 * n], 3), _SemGrid(s[3 * n:], 3)):
            cpy.wait_send()
            cpy.wait_recv()

    outs = pl.pallas_call(
        body, name=name,
        out_shape=tuple(pltpu.HBM(a.shape, a.dtype) for a in thru),
        in_specs=[hbm] * (2 * n) + [sem] * (6 * n) + [pl.BlockSpec(memory_space=pl.ANY)],
        out_specs=tuple([hbm] * (2 * n)),
        input_output_aliases={i: i for i in range(2 * n)},
        compiler_params=pltpu.CompilerParams(has_side_effects=pltpu.SideEffectType.DATAFLOW_SIDE_EFFECTING),
    )(*thru, *sems, after)
    return list(outs[n:])


def _chip_sum(part, recv, sel, tr, name):
    _, _, r, c = part.shape

    def body(sel_ref, p_ref, r_ref, cs_ref, own_ref):
        q = pl.program_id(1)
        s = p_ref[...].astype(F32) + r_ref[...].astype(F32)
        cs_ref[...] = s.astype(BF16)

        @pl.when(q == sel_ref[1])
        def _():
            own_ref[...] = s

    return pl.pallas_call(
        body, name=name,
        grid_spec=pltpu.PrefetchScalarGridSpec(
            num_scalar_prefetch=1, grid=(r // tr, 4),
            in_specs=[pl.BlockSpec((None, None, tr, c), lambda i, q, s: (q, s[0], i, 0)),
                      pl.BlockSpec((None, tr, c), lambda i, q, s: (q, i, 0))],
            out_specs=[pl.BlockSpec((None, tr, c), lambda i, q, s: (q, i, 0)),
                       pl.BlockSpec((tr, c), lambda i, q, s: (i, 0))]),
        out_shape=[jax.ShapeDtypeStruct((4, r, c), BF16), jax.ShapeDtypeStruct((r, c), F32)],
        compiler_params=_cp(("arbitrary", "arbitrary"), 48),
    )(sel, part, recv)


def _gather_direct(src_ref, buf_ref, send_sems, recv_sems):
    x, y, c = _coords()
    me = 4 * x + 2 * y + c
    buf_ref[me] = src_ref[...]
    copies = []
    for k in range(1, NDEV):
        kx, ky, kc = (k >> 2) & 1, (k >> 1) & 1, k & 1
        copies.append(pltpu.make_async_remote_copy(
            src_ref=src_ref, dst_ref=buf_ref.at[me],
            send_sem=send_sems.at[k - 1], recv_sem=recv_sems.at[k - 1],
            device_id=(_flip(x, kx), _flip(y, ky), _flip(c, kc)), device_id_type=MESH_IDS))
    for cpy in copies:
        cpy.start()

    def wait():
        for k in range(1, NDEV):
            kx, ky, kc = (k >> 2) & 1, (k >> 1) & 1, k & 1
            peer = 4 * _flip(x, kx) + 2 * _flip(y, ky) + _flip(c, kc)
            pltpu.make_async_remote_copy(
                src_ref=src_ref, dst_ref=buf_ref.at[peer],
                send_sem=send_sems.at[k - 1], recv_sem=recv_sems.at[k - 1],
                device_id=(x, y, c), device_id_type=MESH_IDS).wait_recv()
        for cpy in copies:
            cpy.wait_send()

    return me, wait


def _mod_exchange(c_row, wmod, bmod, wfmod, bfmod, name, host=None):
    d = c_row.shape[1]
    nm, nf = wmod.shape[1], wfmod.shape[1]
    nw = nm + nf

    def body(ins, outs, scr, host_start, host_finish):
        c_ref, wm_ref, bm_ref, wf_ref, bf_ref = ins
        cs_ref, mod_ref, fmod_ref = outs
        slab_ref, csbuf_ref, mslab_ref, mbuf_ref, s1, r1, s2, r2 = scr
        cv = c_ref[...]
        slab_ref[...] = jnp.broadcast_to(cv * _sigmoid(cv), (8, d))
        _, wait1 = _gather_direct(slab_ref, csbuf_ref, s1, r1)
        wait1()
        host_start()
        for b in range(NDEV):
            cs_ref[b:b + 1, :] = csbuf_ref[b, 0:1, :]
        cs = cs_ref[...]
        mslab_ref[:, 0:nm] = jnp.dot(cs, wm_ref[...], precision=HI, preferred_element_type=F32) + bm_ref[...]
        mslab_ref[:, nm:] = jnp.dot(cs, wf_ref[...], precision=HI, preferred_element_type=F32) + bf_ref[...]
        me, wait2 = _gather_direct(mslab_ref, mbuf_ref, s2, r2)
        host_finish()
        wait2()
        mine = lax.broadcasted_iota(jnp.int32, (8, nw), 0) == me
        for k in range(NDEV):
            rowk = jnp.sum(jnp.where(mine, mbuf_ref[k], 0.0), axis=0, keepdims=True)
            mod_ref[k:k + 1, :] = rowk[:, 0:nm]
            fmod_ref[k:k + 1, :] = rowk[:, nm:]

    vm = pl.BlockSpec(memory_space=pltpu.VMEM)
    return _hosted_call(
        body, host, name, (), [vm] * 5, [vm] * 3,
        [jax.ShapeDtypeStruct((NDEV, d), F32), jax.ShapeDtypeStruct((NDEV, nm), F32),
         jax.ShapeDtypeStruct((NDEV, nf), F32)],
        [pltpu.VMEM((8, d), F32), pltpu.VMEM((NDEV, 8, d), F32),
         pltpu.VMEM((8, nw), F32), pltpu.VMEM((NDEV, 8, nw), F32),
         pltpu.SemaphoreType.DMA((7,)), pltpu.SemaphoreType.DMA((7,)),
         pltpu.SemaphoreType.DMA((7,)), pltpu.SemaphoreType.DMA((7,))],
        None, 40, (c_row, wmod, bmod, wfmod, bfmod), manual=True)


def _table_sum(tabs, name):
    n = len(tabs)

    def body(*refs):
        for a in range(n):
            tot = refs[a][0]
            for k in range(1, NDEV):
                tot = tot + refs[a][k]
            refs[n + a][...] = tot

    vm = pl.BlockSpec(memory_space=pltpu.VMEM)
    return pl.pallas_call(
        body, name=name, in_specs=[vm] * n, out_specs=[vm] * n,
        out_shape=[jax.ShapeDtypeStruct(tb.shape[1:], F32) for tb in tabs],
    )(*tabs)


def _adamw_math(w, g, m, v):
    m = ADAM_B1 * m + (1.0 - ADAM_B1) * g
    v = ADAM_B2 * v + (1.0 - ADAM_B2) * (g * g)
    m_hat = m / (1.0 - ADAM_B1 ** ADAM_STEP)
    v_hat = v / (1.0 - ADAM_B2 ** ADAM_STEP)
    delta = -ADAM_LR * (m_hat / (jnp.sqrt(v_hat) + ADAM_EPS) + ADAM_WD * w)
    return delta, m, v


def _adamw_small(params, name):
    n = len(params)

    def body(*refs):
        for p in range(n):
            w_ref, g_ref, m_ref, v_ref = refs[4 * p:4 * p + 4]
            d_ref, mo_ref, vo_ref = refs[4 * n + 3 * p:4 * n + 3 * p + 3]
            d_ref[...], mo_ref[...], vo_ref[...] = _adamw_math(w_ref[...], g_ref[...], m_ref[...], v_ref[...])

    vm = pl.BlockSpec(memory_space=pltpu.VMEM)
    flat = [a for p in params for a in p]
    outs = pl.pallas_call(
        body, name=name, in_specs=[vm] * (4 * n), out_specs=[vm] * (3 * n),
        out_shape=[jax.ShapeDtypeStruct(p[0].shape, F32) for p in params for _ in range(3)])(*flat)
    return [outs[3 * p:3 * p + 3] for p in range(n)]


def _rs_final(own, recv, w, m, v, tr, name):
    r, c = own.shape

    def body(o_ref, r_ref, w_ref, m_ref, v_ref, g_ref, d_ref, mo_ref, vo_ref):
        g = o_ref[...] + r_ref[0].astype(F32) + r_ref[1].astype(F32) + r_ref[2].astype(F32)
        g_ref[...] = g
        d_ref[...], mo_ref[...], vo_ref[...] = _adamw_math(w_ref[...], g, m_ref[...], v_ref[...])

    tile = pl.BlockSpec((tr, c), lambda i: (i, 0))
    sds = jax.ShapeDtypeStruct((r, c), F32)
    return pl.pallas_call(
        body, name=name, grid=(r // tr,),
        in_specs=[tile, pl.BlockSpec((3, tr, c), lambda i: (0, i, 0)), tile, tile, tile],
        out_specs=[tile] * 4, out_shape=[sds] * 4,
        compiler_params=_cp(("arbitrary",), 48),
    )(own, recv, w, m, v)


def _mod_weight_update(cs, dm, w, m, v, tr, name):
    r, c = w.shape

    def body(cs_ref, dm_ref, w_ref, m_ref, v_ref, g_ref, d_ref, mo_ref, vo_ref):
        g = lax.dot_general(cs_ref[...], dm_ref[...], (((0,), (0,)), ((), ())),
                            precision=HI, preferred_element_type=F32)
        g_ref[...] = g
        d_ref[...], mo_ref[...], vo_ref[...] = _adamw_math(w_ref[...], g, m_ref[...], v_ref[...])

    tile = pl.BlockSpec((tr, c), lambda i: (i, 0))
    sds = jax.ShapeDtypeStruct((r, c), F32)
    return pl.pallas_call(
        body, name=name, grid=(r // tr,),
        in_specs=[pl.BlockSpec((NDEV, tr), lambda i: (0, i)), pl.BlockSpec((NDEV, c), lambda i: (0, 0)),
                  tile, tile, tile],
        out_specs=[tile] * 4, out_shape=[sds] * 4,
        compiler_params=_cp(("arbitrary",), 48),
    )(cs, dm, w, m, v)


def _rows(*vs):
    d = vs[0].shape[-1]
    rows = [v.reshape(1, d) for v in vs]
    return jnp.concatenate(rows + [jnp.zeros((8 - len(rows), d), F32)], axis=0)


def _block_diag_pairs(w):
    hd = w.shape[-1]
    z = jnp.zeros((w.shape[0] // 2, hd, hd), w.dtype)
    top = jnp.concatenate([w[0::2], z], axis=2)
    bot = jnp.concatenate([z, w[1::2]], axis=2)
    return jnp.concatenate([top, bot], axis=1).astype(BF16)


def _diag_pairs(g):
    hd = g.shape[-1] // 2
    both = jnp.stack([g[:, :hd, :hd], g[:, hd:, hd:]], axis=1)
    return both.reshape(2 * g.shape[0], hd, hd)


def kernel(x, c, w_mod, b_mod, g_ffn1, w_ffn1_in, w_ffn1_out, g_mix, w_in, conv_w, conv_b, ln_g, ln_b, rnn_conv_w, rnn_conv_b, w_a, b_a, w_i, b_i, lru_lambda, w_out, g_ffn2, w_ffn2_in, w_ffn2_out, w_fmod, b_fmod, g_final, loss_target, m_w_mod, m_b_mod, m_g_ffn1, m_w_ffn1_in, m_w_ffn1_out, m_g_mix, m_w_in, m_conv_w, m_conv_b, m_ln_g, m_ln_b, m_rnn_conv_w, m_rnn_conv_b, m_w_a, m_b_a, m_w_i, m_b_i, m_lru_lambda, m_w_out, m_g_ffn2, m_w_ffn2_in, m_w_ffn2_out, m_w_fmod, m_b_fmod, m_g_final, v_w_mod, v_b_mod, v_g_ffn1, v_w_ffn1_in, v_w_ffn1_out, v_g_mix, v_w_in, v_conv_w, v_conv_b, v_ln_g, v_ln_b, v_rnn_conv_w, v_rnn_conv_b, v_w_a, v_b_a, v_w_i, v_b_i, v_lru_lambda, v_w_out, v_g_ffn2, v_w_ffn2_in, v_w_ffn2_out, v_w_fmod, v_b_fmod, v_g_final):
    t, d = x.shape[1], x.shape[2]
    fb = w_ffn1_in.shape[2]
    nm = w_mod.shape[2]
    nf = w_fmod.shape[1]
    dc = conv_b.shape[1]
    cl = conv_w.shape[2]
    tm = min(TOKEN_TILE, t)
    tk = min(WGRAD_TILE, t)
    nk = t // tk
    me = 4 * lax.axis_index("x") + 2 * lax.axis_index("y") + lax.axis_index("c")

    tr = jnp.transpose
    bmod_l = lax.dynamic_slice(b_mod, (0, me * nm), (1, nm))
    bfmod_l = lax.dynamic_slice(b_fmod.reshape(1, -1), (0, me * nf), (1, nf))
    cwl = jnp.concatenate([conv_w[0], jnp.zeros((1, cl), F32), rnn_conv_w[0], jnp.zeros((4, cl), F32)], axis=0)
    cs, mod_rows, fmod_rows, wi1, wo1, cwg = _mod_exchange(
        c, w_mod[0], bmod_l, w_fmod, bfmod_l, "mod_and_gather_ffn1",
        host=_Exchange("gather", [tr(w_ffn1_in[0]).astype(BF16), w_ffn1_out[0].astype(BF16), cwl]))
    wi1 = wi1.reshape(2, 4, fb, d)
    wo1 = wo1.reshape(4 * fb, d)
    mod = mod_rows.reshape(9, d)
    fmod = fmod_rows.reshape(2, d)
    vec1 = _rows(g_ffn1, mod[0], mod[1], mod[2])
    vecm = _rows(g_mix, mod[3], mod[4], mod[5])
    vec3 = _rows(g_ffn2, mod[6], mod[7], mod[8])
    vecf = _rows(g_final, fmod[0], fmod[1])

    xin = x[0]
    later = [w_in[0].astype(BF16), w_out[0].astype(BF16), tr(w_ffn2_in[0]).astype(BF16), w_ffn2_out[0].astype(BF16)]
    x1, h1, gu1, f1, win, wout, wi2, wo2 = _ffn_fwd(xin, vec1, wi1, wo1, tm, "ffn1_fwd",
                                                    host=_Exchange("gather", later))
    wi2 = wi2.reshape(2, 4, fb, d)
    wo2 = wo2.reshape(4 * fb, d)
    wout = wout.reshape(d, d)
    h2, proj = _mix_in(x1, vecm, win, tm, "mix_in")
    lnv = _rows(ln_g, ln_b)
    rvec = _rows(rnn_conv_b, b_a, b_i, lru_lambda)
    wab = _block_diag_pairs(w_a[0])
    wib = _block_diag_pairs(w_i[0])
    cwf = jnp.transpose(cwg, (1, 0, 2)).reshape(40, NDEV * cl)
    cw32 = jnp.concatenate([cwf[0:CONV_W], conv_b], axis=0)
    rw8 = cwf[32:40]

    (cv,) = _conv_fwd(proj, cw32, "conv_fwd")
    hr, yr = _rnn_fwd(proj, rw8, rvec, wab, wib, "rnn_fwd")
    x2, ym, ycat = _mix_out(x1, cv, yr, vecm, lnv, wout, tm, "mix_out")
    h3, gu3, dx3, dvf, df3, dva3 = _ffn_fwd_loss(x2, vec3, wi2, wo2, loss_target[0], vecf, tm, "ffn2_fwd_loss")
    a_tok = lambda width: pl.BlockSpec((tk, width), lambda k, s: (s, 0))
    sel = jnp.stack([lax.axis_index("c"), 2 * lax.axis_index("x") + lax.axis_index("y")]).astype(jnp.int32)
    row_tile = {"w_ffn1_in": fb // 4, "w_ffn1_out": fb // 4, "w_in": 512, "w_out": 128,
                "w_ffn2_in": fb // 4, "w_ffn2_out": fb // 4}

    def chip_sums(names, partials, from_sib):
        out = [_chip_sum(p.reshape((4, 2) + p.shape[1:]), r, sel, p.shape[1], "chip_sum_" + nm_)
               for nm_, p, r in zip(names, partials, from_sib)]
        return [o[0] for o in out], [o[1] for o in out]

    dgu3, p_wi2, p_wo2 = _ffn_bwd_w(df3, gu3, h3, wo2, tm, "ffn2_bwd_w")
    p_wi2 = p_wi2.reshape(NDEV, fb, d)
    p_wo2 = p_wo2.reshape(NDEV, fb // 2, d)
    names2 = ["w_ffn2_in", "w_ffn2_out"]
    dx2, dv3, sib_wi2, sib_wo2 = _ffn_bwd_in(dx3, x2, vec3, dgu3, wi2, tm, "ffn2_bwd_in",
                                             host=_Exchange("sibling", [p_wi2, p_wo2]))
    sums2, owns2 = chip_sums(names2, [p_wi2, p_wo2], [sib_wi2, sib_wo2])
    started2 = _chips_split_start(sums2, "rs_chips_ffn2_start")
    p_wout, dcv, dhr, duy, dln, dgt2 = _mixout_bwd(
        dx2, ym, ycat, cv, hr, proj, vecm + started2[-1][0:1, 0:1], lnv, wout, tm, "mixout_bwd")
    p_wout = p_wout.reshape(NDEV, d // NDEV, d)
    dval, dgate, dcw = _conv_bwd(proj, dcv, cw32, "conv_bwd")
    dux, rsm, dwab, dwib = _rnn_bwd(proj, hr, dhr, rw8, rvec, wab, wib, "rnn_bwd")
    parts = [dval, dgate, dux, duy]
    dx1, dvm, df1, dva1, p_win = _mixin_bwd(dx2, x1, parts, vecm, win, h2, f1, vec1, tm, "mixin_bwd")
    namesm = ["w_in", "w_out"]
    sumsm, ownsm = chip_sums(namesm, [p_win, p_wout],
                             _exchange(_Exchange("sibling", [p_win, p_wout]), "rs_sibling_mix"))
    lane_pad = lambda v: jnp.concatenate([v, jnp.zeros_like(v)], axis=1)
    startedm = _chips_split_start(sumsm, "rs_chips_mix_start")
    early = jnp.concatenate([dva3, dv3, dvf, dvm, dgt2, dva1 + startedm[-1][0:1, 0:1], dcw.reshape(16, d),
                             lane_pad(dln), lane_pad(rsm),
                             _diag_pairs(dwab).reshape(32, d), _diag_pairs(dwib).reshape(32, d)], axis=0)
    dgu1, p_wi1, p_wo1, all_early = _ffn_bwd_w(
        df1, gu1, h1, wo1, tm, "ffn1_bwd_w", host=_Exchange("direct", [early]))
    p_wi1 = p_wi1.reshape(NDEV, fb, d)
    p_wo1 = p_wo1.reshape(NDEV, fb // 2, d)
    names1 = ["w_ffn1_in", "w_ffn1_out"]
    sums1, owns1 = chip_sums(names1, [p_wi1, p_wo1],
                             _exchange(_Exchange("sibling", [p_wi1, p_wo1]), "rs_sibling_ffn1"))
    started = _chips_split_start(sums1, "rs_chips_ffn1_start")
    dx0, dv1 = _ffn_bwd_in(dx1, xin, vec1 + started[-1][0:1, 0:1], dgu1, wi1, tm, "ffn1_bwd_in")
    from_chips = dict(zip(names2, _chips_split_wait(started2, len(sums2), dx0, "rs_chips_ffn2_wait")))
    from_chips.update(zip(namesm, _chips_split_wait(startedm, len(sumsm), dx0, "rs_chips_mix_wait")))
    owns = dict(zip(namesm + names2 + names1, ownsm + owns2 + owns1))

    big = {"w_ffn1_in": (tr(w_ffn1_in[0]), tr(m_w_ffn1_in[0]), tr(v_w_ffn1_in[0])),
           "w_ffn1_out": (w_ffn1_out[0], m_w_ffn1_out[0], v_w_ffn1_out[0]),
           "w_in": (w_in[0], m_w_in[0], v_w_in[0]), "w_out": (w_out[0], m_w_out[0], v_w_out[0]),
           "w_ffn2_in": (tr(w_ffn2_in[0]), tr(m_w_ffn2_in[0]), tr(v_w_ffn2_in[0])),
           "w_ffn2_out": (w_ffn2_out[0], m_w_ffn2_out[0], v_w_ffn2_out[0])}
    res = {}

    def final_sum(nm_):
        out4 = _rs_final(owns[nm_], from_chips[nm_], *big[nm_], row_tile[nm_], "rs_final_" + nm_)
        res[nm_] = [(tr(o) if nm_ in ("w_ffn1_in", "w_ffn2_in") else o)[None] for o in out4]
        return out4[0]

    done = [final_sum(nm_) for nm_ in namesm + names2]
    dfm_all = jnp.concatenate([all_early[:, 17], all_early[:, 19]], axis=1)
    dfm_l = lax.dynamic_slice(dfm_all, (0, me * nf), (NDEV, nf))
    res["w_fmod"] = list(_mod_weight_update(cs, dfm_l, w_fmod, m_w_fmod, v_w_fmod, 256, "w_fmod_update"))
    (all_late,) = _exchange(_Exchange("direct", [dv1]), "late_table", after=done + [res["w_fmod"][0]])
    from_chips["w_ffn1_in"], from_chips["w_ffn1_out"] = _chips_split_wait(
        started, len(sums1), all_late, "rs_chips_ffn1_wait")
    for nm_ in names1:
        final_sum(nm_)
    te, tl = _table_sum([all_early, all_late], "table_sum")
    loss = jnp.sum(te[20])

    mod_rows_of = lambda e, l: [l[1], l[3], e[42], e[25], e[27], e[32], e[9], e[11], e[2]]
    dm_all = jnp.concatenate(mod_rows_of(jnp.swapaxes(all_early, 0, 1), jnp.swapaxes(all_late, 0, 1)), axis=1)
    dm_l = lax.dynamic_slice(dm_all, (0, me * nm), (NDEV, nm))
    res["w_mod"] = [o[None] for o in
                    _mod_weight_update(cs, dm_l, w_mod[0], m_w_mod[0], v_w_mod[0], 256, "w_mod_update")]

    dcw_f = te[48:64].reshape(32, dc)
    rsm_f = te[72:80, 0:dc]
    small_grads = {
        "b_mod": jnp.concatenate(mod_rows_of(te, tl)).reshape(1, 9 * d),
        "b_fmod": jnp.concatenate([te[17], te[19]]),
        "g_ffn1": tl[0:1], "g_mix": te[24:25], "g_ffn2": te[8:9], "g_final": te[16],
        "conv_w": lax.dynamic_slice(dcw_f, (0, me * cl), (CONV_W, cl))[None],
        "conv_b": dcw_f[31:32],
        "ln_g": te[64:65, 0:dc], "ln_b": te[65:66, 0:dc],
        "rnn_conv_w": lax.dynamic_slice(rsm_f, (0, me * cl), (RNN_CONV_W, cl))[None],
        "rnn_conv_b": rsm_f[4:5], "b_a": rsm_f[5:6], "b_i": rsm_f[6:7], "lru_lambda": rsm_f[7:8],
        "w_a": te[80:112].reshape(w_a.shape), "w_i": te[112:144].reshape(w_i.shape),
    }
    small_params = {
        "b_mod": (b_mod, m_b_mod, v_b_mod), "b_fmod": (b_fmod, m_b_fmod, v_b_fmod),
        "g_ffn1": (g_ffn1, m_g_ffn1, v_g_ffn1), "g_mix": (g_mix, m_g_mix, v_g_mix),
        "g_ffn2": (g_ffn2, m_g_ffn2, v_g_ffn2), "g_final": (g_final, m_g_final, v_g_final),
        "conv_w": (conv_w, m_conv_w, v_conv_w), "conv_b": (conv_b, m_conv_b, v_conv_b),
        "ln_g": (ln_g, m_ln_g, v_ln_g), "ln_b": (ln_b, m_ln_b, v_ln_b),
        "rnn_conv_w": (rnn_conv_w, m_rnn_conv_w, v_rnn_conv_w),
        "rnn_conv_b": (rnn_conv_b, m_rnn_conv_b, v_rnn_conv_b),
        "w_a": (w_a, m_w_a, v_w_a), "b_a": (b_a, m_b_a, v_b_a),
        "w_i": (w_i, m_w_i, v_w_i), "b_i": (b_i, m_b_i, v_b_i),
        "lru_lambda": (lru_lambda, m_lru_lambda, v_lru_lambda),
    }
    two_d = lambda w: (-1, w.shape[-1]) if w.ndim > 1 else (1, w.shape[0])
    small_names = list(small_grads)
    small_outs = _adamw_small(
        [(w.reshape(two_d(w)), small_grads[nm_].reshape(two_d(w)), m.reshape(two_d(w)), v.reshape(two_d(w)))
         for nm_ in small_names for (w, m, v) in [small_params[nm_]]], "adamw_small")
    for nm_, outs in zip(small_names, small_outs):
        shp = small_params[nm_][0].shape
        res[nm_] = [small_grads[nm_].reshape(shp)] + [o.reshape(shp) for o in outs]

    order = ["w_mod", "b_mod", "g_ffn1", "w_ffn1_in", "w_ffn1_out", "g_mix", "w_in", "conv_w", "conv_b",
             "ln_g", "ln_b", "rnn_conv_w", "rnn_conv_b", "w_a", "b_a", "w_i", "b_i", "lru_lambda", "w_out",
             "g_ffn2", "w_ffn2_in", "w_ffn2_out", "w_fmod", "b_fmod", "g_final"]
    return (loss, dx0[None], *[res[n][0] for n in order], *[res[n][1] for n in order],
            *[res[n][2] for n in order], *[res[n][3] for n in order])
```

```python
import functools
import math

import jax
import jax.numpy as jnp
from jax import lax
from jax.experimental import pallas as pl
from jax.experimental.pallas import tpu as pltpu

F32 = jnp.float32
BF16 = jnp.bfloat16
MESH_IDS = pl.DeviceIdType.MESH
NDEV = 8
EPS = 1e-6
RG_C = 8.0
CONV_W = 31
RNN_CONV_W = 4
LANES = 128
ADAM_LR = 0.001
ADAM_B1 = 0.9
ADAM_B2 = 0.999
ADAM_EPS = 1e-08
ADAM_WD = 0.01
ADAM_STEP = 10
TOKEN_TILE = 512
ROW_GROUP = 16
HI = lax.Precision.HIGHEST


def _cp(sem, vmem_mb):
    return pltpu.CompilerParams(dimension_semantics=sem, vmem_limit_bytes=vmem_mb * 1024 * 1024)


def _dot(a, b):
    return jnp.dot(a, b, preferred_element_type=F32)


def _dot_nt(a, b):
    return lax.dot_general(a, b, (((1,), (1,)), ((), ())), preferred_element_type=F32)


def _dot_tn(a, b):
    return lax.dot_general(a, b, (((0,), (0,)), ((), ())), preferred_element_type=F32)


def _sigmoid(x):
    return 1.0 / (1.0 + jnp.exp(-x))


def _adaln(x, vec_ref):
    rstd = lax.rsqrt(jnp.mean(x * x, axis=-1, keepdims=True) + EPS)
    return (x * rstd) * vec_ref[0:1, :] * (1.0 + vec_ref[2:3, :]) + vec_ref[1:2, :]


def _adaln_bwd(x, dh, vec_ref, dvec_ref):
    rstd = lax.rsqrt(jnp.mean(x * x, axis=-1, keepdims=True) + EPS)
    xhat = x * rstd
    dvec_ref[0:1, :] += jnp.sum(dh * xhat, axis=0, keepdims=True)
    dvec_ref[1:2, :] += jnp.sum(dh, axis=0, keepdims=True)
    dxhat = dh * (vec_ref[0:1, :] * (1.0 + vec_ref[2:3, :]))
    return rstd * (dxhat - xhat * jnp.mean(dxhat * xhat, axis=-1, keepdims=True))


def _adaln_finish(vec_ref, dvec_ref):
    s = dvec_ref[0:1, :]
    dvec_ref[3:4, :] = vec_ref[0:1, :] * s
    dvec_ref[0:1, :] = (1.0 + vec_ref[2:3, :]) * s


def _gelu_and_grad(x):
    k0 = math.sqrt(2.0 / math.pi)
    x2 = x * x
    t = jnp.tanh(k0 * (x + 0.044715 * x * x2))
    g = 0.5 * x * (1.0 + t)
    dg = 0.5 * (1.0 + t) + 0.5 * x * (1.0 - t * t) * (k0 * (1.0 + 3.0 * 0.044715 * x2))
    return g, dg


def _log_sigmoid(x):
    z = jnp.exp(-jnp.abs(x))
    u = 1.0 + z
    d = u - 1.0
    log1p = jnp.where(d == 0.0, z, jnp.log(u) * (z / jnp.where(d == 0.0, 1.0, d)))
    return jnp.minimum(x, 0.0) - log1p


def _neg_expm1(x):
    series = -x * (1.0 + x * (0.5 + x * (1.0 / 6.0 + x * (1.0 / 24.0 + x * (1.0 / 120.0)))))
    return jnp.where(x > -0.05, series, 1.0 - jnp.exp(x))


SUBLANES = 8


def _doubling_scan(a, b, reverse):
    n = a.shape[0]
    row = lax.broadcasted_iota(jnp.int32, a.shape, 0)
    s = 1
    while s < n:
        ok = (row < n - s) if reverse else (row >= s)
        shift = n - s if reverse else s
        b = a * jnp.where(ok, pltpu.roll(b, shift, 0), 0.0) + b
        if 2 * s < n:
            a = a * jnp.where(ok, pltpu.roll(a, shift, 0), 1.0)
        s *= 2
    return b


def _tiled_scan(a, b, reverse, sa_ref, sb_ref, carry_ref, out_ref):
    n = a.shape[0]
    nt8 = n // SUBLANES
    sub = lax.broadcasted_iota(jnp.int32, a.shape, 0) % SUBLANES
    for s in (1, 2, 4):
        ok = (sub < SUBLANES - s) if reverse else (sub >= s)
        shift = n - s if reverse else s
        b = a * jnp.where(ok, pltpu.roll(b, shift, 0), 0.0) + b
        a = a * jnp.where(ok, pltpu.roll(a, shift, 0), 1.0)
    sa_ref[...] = a
    sb_ref[...] = b
    edge = 0 if reverse else SUBLANES - 1
    at = sa_ref[pl.ds(edge, nt8, stride=SUBLANES), :]
    bt = sb_ref[pl.ds(edge, nt8, stride=SUBLANES), :]
    xt = _doubling_scan(at, bt, reverse)
    rowt = lax.broadcasted_iota(jnp.int32, xt.shape, 0)
    if reverse:
        carry_ref[...] = jnp.where(rowt < nt8 - 1, pltpu.roll(xt, nt8 - 1, 0), 0.0)
    else:
        carry_ref[...] = jnp.where(rowt >= 1, pltpu.roll(xt, 1, 0), 0.0)
    for r in range(nt8):
        rows = slice(r * SUBLANES, (r + 1) * SUBLANES)
        out_ref[rows, :] = sa_ref[rows, :] * carry_ref[r:r + 1, :] + sb_ref[rows, :]


def _rglru_gates(xr, wa_ref, wi_ref, rvec_ref):
    xb = xr.astype(BF16)
    r = _sigmoid(_dot(xb, wa_ref[...]) + rvec_ref[1:2, :])
    ig = _sigmoid(_dot(xb, wi_ref[...]) + rvec_ref[2:3, :])
    ls = _log_sigmoid(rvec_ref[3:4, :])
    log_a = RG_C * r * ls
    a = jnp.exp(log_a)
    mult = jnp.sqrt(_neg_expm1(2.0 * log_a))
    return xb, r, ig, ls, a, mult


def _rnn_conv(ux, rw_ref, rvec_ref, ext_ref):
    t = ux.shape[0]
    ext_ref[0:8, :] = jnp.zeros((8, ux.shape[1]), F32)
    ext_ref[8:, :] = ux
    xr = rvec_ref[0:1, :] + rw_ref[RNN_CONV_W - 1:RNN_CONV_W, :] * ux
    for k in range(RNN_CONV_W - 1):
        d = RNN_CONV_W - 1 - k
        xr = xr + rw_ref[k:k + 1, :] * ext_ref[8 - d:8 - d + t, :]
    return xr


def _ffn_fwd(x, vec, wi, wo, tm, name, host=None):
    t, d = x.shape
    nj, fb = wi.shape[1], wi.shape[2]
    nt = t // tm

    def body(ins, outs, scr):
        x_ref, vec_ref, wi_ref, wo_ref = ins
        xo_ref, h_ref, gu_ref, f_ref = outs
        acc_ref, = scr
        j = pl.program_id(1)

        @pl.when(j == 0)
        def _():
            h_ref[...] = _adaln(x_ref[...], vec_ref).astype(BF16)
            acc_ref[...] = jnp.zeros_like(acc_ref)

        h = h_ref[...]
        gate = _dot_nt(h, wi_ref[0])
        up = _dot_nt(h, wi_ref[1])
        gu_ref[0] = gate.astype(BF16)
        gu_ref[1] = up.astype(BF16)
        act = (gate * _sigmoid(gate) * up).astype(BF16)
        acc_ref[...] += _dot(act, wo_ref[...])

        @pl.when(j == nj - 1)
        def _():
            f = acc_ref[...]
            f_ref[...] = f.astype(BF16)
            xo_ref[...] = x_ref[...] + 0.5 * vec_ref[3:4, :] * f

    tile = pl.BlockSpec((tm, d), lambda i, j: (i, 0))
    return _hosted_call(
        body, host, name, (nt, nj),
        [tile,
         pl.BlockSpec((8, d), lambda i, j: (0, 0)),
         pl.BlockSpec((2, None, fb, d), lambda i, j: (0, j, 0, 0)),
         pl.BlockSpec((fb, d), lambda i, j: (j, 0))],
        [tile, tile, pl.BlockSpec((2, None, tm, fb), lambda i, j: (0, j, i, 0)), tile],
        [jax.ShapeDtypeStruct((t, d), F32), jax.ShapeDtypeStruct((t, d), BF16),
         jax.ShapeDtypeStruct((2, nj, t, fb), BF16), jax.ShapeDtypeStruct((t, d), BF16)],
        [pltpu.VMEM((tm, d), F32)], ("arbitrary", "arbitrary"), 48, (x, vec, wi, wo))


def _ffn_fwd_loss(x, vec, wi, wo, tgt, fvec, tm, name):
    t, d = x.shape
    nj, fb = wi.shape[1], wi.shape[2]
    nt = t // tm

    def body(x_ref, vec_ref, wi_ref, wo_ref, t_ref, fvec_ref, h_ref, gu_ref, dx_ref, dvec_ref, df_ref, dgt_ref,
             acc_ref):
        i = pl.program_id(0)
        j = pl.program_id(1)

        @pl.when((i == 0) & (j == 0))
        def _():
            dvec_ref[...] = jnp.zeros_like(dvec_ref)
            dgt_ref[...] = jnp.zeros_like(dgt_ref)

        @pl.when(j == 0)
        def _():
            h_ref[...] = _adaln(x_ref[...], vec_ref).astype(BF16)
            acc_ref[...] = jnp.zeros_like(acc_ref)

        h = h_ref[...]
        gate = _dot_nt(h, wi_ref[0])
        up = _dot_nt(h, wi_ref[1])
        gu_ref[0] = gate.astype(BF16)
        gu_ref[1] = up.astype(BF16)
        act = (gate * _sigmoid(gate) * up).astype(BF16)
        acc_ref[...] += _dot(act, wo_ref[...])

        @pl.when(j == nj - 1)
        def _():
            f = acc_ref[...]
            xo = x_ref[...] + 0.5 * vec_ref[3:4, :] * f
            e = _adaln(xo, fvec_ref) - t_ref[...]
            dvec_ref[4:5, :] += (0.5 / d) * jnp.sum(e * e, axis=0, keepdims=True)
            dx = _adaln_bwd(xo, e * (1.0 / d), fvec_ref, dvec_ref)
            dx_ref[...] = dx
            df_ref[...] = (0.5 * vec_ref[3:4, :] * dx).astype(BF16)
            dgt_ref[2:3, :] += 0.5 * jnp.sum(dx * f, axis=0, keepdims=True)

        @pl.when((i == nt - 1) & (j == nj - 1))
        def _():
            _adaln_finish(fvec_ref, dvec_ref)

    tile = pl.BlockSpec((tm, d), lambda i, j: (i, 0))
    tab = pl.BlockSpec((8, d), lambda i, j: (0, 0))
    return pl.pallas_call(
        body, name=name, grid=(nt, nj),
        in_specs=[tile, tab,
                  pl.BlockSpec((2, None, fb, d), lambda i, j: (0, j, 0, 0)),
                  pl.BlockSpec((fb, d), lambda i, j: (j, 0)),
                  pl.BlockSpec((tm, d), lambda i, j: (jnp.where(j == nj - 1, i, jnp.maximum(i - 1, 0)), 0)), tab],
        out_specs=[tile, pl.BlockSpec((2, None, tm, fb), lambda i, j: (0, j, i, 0)), tile, tab, tile, tab],
        out_shape=[jax.ShapeDtypeStruct((t, d), BF16), jax.ShapeDtypeStruct((2, nj, t, fb), BF16),
                   jax.ShapeDtypeStruct((t, d), F32), jax.ShapeDtypeStruct((8, d), F32),
                   jax.ShapeDtypeStruct((t, d), BF16), jax.ShapeDtypeStruct((8, d), F32)],
        scratch_shapes=[pltpu.VMEM((tm, d), F32)],
        compiler_params=_cp(("arbitrary", "arbitrary"), 56),
    )(x, vec, wi, wo, tgt, fvec)


def _mix_in(x, vec, win, tm, name, host=None):
    t, d = x.shape
    nb, _, cb = win.shape

    def body(ins, outs, scr):
        x_ref, vec_ref, w_ref = ins
        h_ref, p_ref = outs
        h = _adaln(x_ref[...], vec_ref).astype(BF16)
        h_ref[...] = h
        for k in range(nb):
            p_ref[:, k * cb:(k + 1) * cb] = _dot(h, w_ref[k])

    return _hosted_call(
        body, host, name, (t // tm,),
        [pl.BlockSpec((tm, d), lambda i: (i, 0)),
         pl.BlockSpec((8, d), lambda i: (0, 0)),
         pl.BlockSpec((nb, d, cb), lambda i: (0, 0, 0))],
        [pl.BlockSpec((tm, d), lambda i: (i, 0)),
         pl.BlockSpec((tm, nb * cb), lambda i: (i, 0))],
        [jax.ShapeDtypeStruct((t, d), BF16), jax.ShapeDtypeStruct((t, nb * cb), F32)],
        [], ("arbitrary",), 48, (x, vec, win))


def _conv_fwd(proj, cw32, name, host=None):
    t = proj.shape[0]
    nblk = cw32.shape[1] // LANES
    ch = min(t, 128)

    def body(ins, outs, scr):
        val_ref, gate_ref, cw_ref = ins
        cv_ref, = outs
        ext_ref, = scr
        ext_ref[0:32, :] = jnp.zeros((32, LANES), F32)
        ext_ref[32:, :] = val_ref[...] * _sigmoid(gate_ref[...])
        for r in range(t // ch):
            acc = jnp.broadcast_to(cw_ref[31:32, :], (ch, LANES))
            for k in range(CONV_W):
                off = 32 + r * ch - (CONV_W - 1 - k)
                acc = acc + cw_ref[k:k + 1, :] * ext_ref[off:off + ch, :]
            cv_ref[r * ch:(r + 1) * ch, :] = acc

    return _hosted_call(
        body, host, name, (nblk,),
        [pl.BlockSpec((t, LANES), lambda c: (0, c)),
         pl.BlockSpec((t, LANES), lambda c: (0, nblk + c)),
         pl.BlockSpec((32, LANES), lambda c: (0, c))],
        [pl.BlockSpec((t, LANES), lambda c: (0, c))],
        [jax.ShapeDtypeStruct((t, nblk * LANES), F32)],
        [pltpu.VMEM((t + 32, LANES), F32)], ("arbitrary",), 48, (proj, proj, cw32))


def _rnn_fwd(proj, rw8, rvec, wab, wib, name, host=None):
    t = proj.shape[0]
    nblk = rvec.shape[1] // LANES

    def body(ins, outs, scr):
        ux_ref, uy_ref, rw_ref, rvec_ref, wa_ref, wi_ref = ins
        h_ref, yr_ref = outs
        ext_ref, sa_ref, sb_ref, carry_ref = scr
        xr = _rnn_conv(ux_ref[...], rw_ref, rvec_ref, ext_ref)
        _, _, ig, _, a, mult = _rglru_gates(xr, wa_ref, wi_ref, rvec_ref)
        _tiled_scan(a, mult * (ig * xr), False, sa_ref, sb_ref, carry_ref, h_ref)
        ge, _ = _gelu_and_grad(uy_ref[...])
        yr_ref[...] = (ge * h_ref[...]).astype(BF16)

    blk = lambda off: pl.BlockSpec((t, LANES), lambda c: (0, off + c))
    return _hosted_call(
        body, host, name, (nblk,),
        [blk(2 * nblk), blk(3 * nblk),
         pl.BlockSpec((8, LANES), lambda c: (0, c)),
         pl.BlockSpec((8, LANES), lambda c: (0, c)),
         pl.BlockSpec((None, LANES, LANES), lambda c: (c, 0, 0)),
         pl.BlockSpec((None, LANES, LANES), lambda c: (c, 0, 0))],
        [blk(0), blk(0)],
        [jax.ShapeDtypeStruct((t, nblk * LANES), F32), jax.ShapeDtypeStruct((t, nblk * LANES), BF16)],
        [pltpu.VMEM((t + 8, LANES), F32), pltpu.VMEM((t, LANES), F32), pltpu.VMEM((t, LANES), F32),
         pltpu.VMEM((t // SUBLANES, LANES), F32)], ("arbitrary",), 56, (proj, proj, rw8, rvec, wab, wib))


def _ln_silu(cv, lnv_ref):
    mu = jnp.mean(cv, axis=-1, keepdims=True)
    xc = cv - mu
    rs = lax.rsqrt(jnp.mean(xc * xc, axis=-1, keepdims=True) + EPS)
    chat = xc * rs
    z = chat * lnv_ref[0:1, :] + lnv_ref[1:2, :]
    sg = _sigmoid(z)
    return rs, chat, z, sg


def _mix_out(x, cv, yr, vec, lnv, wout, tm, name, host=None):
    t, d = x.shape
    dc = cv.shape[1]

    def body(ins, outs, scr):
        x_ref, cv_ref, yr_ref, vec_ref, lnv_ref, w_ref = ins
        xo_ref, ym_ref, yc_ref = outs
        _, _, z, sg = _ln_silu(cv_ref[...], lnv_ref)
        yc = (z * sg).astype(BF16)
        yr = yr_ref[...]
        yc_ref[:, 0:dc] = yc
        yc_ref[:, dc:] = yr
        ym = _dot(yc, w_ref[0:dc, :]) + _dot(yr, w_ref[dc:, :])
        ym_ref[...] = ym.astype(BF16)
        xo_ref[...] = x_ref[...] + vec_ref[3:4, :] * ym

    tile = pl.BlockSpec((tm, d), lambda i: (i, 0))
    return _hosted_call(
        body, host, name, (t // tm,),
        [tile,
         pl.BlockSpec((tm, dc), lambda i: (i, 0)),
         pl.BlockSpec((tm, dc), lambda i: (i, 0)),
         pl.BlockSpec((8, d), lambda i: (0, 0)),
         pl.BlockSpec((8, dc), lambda i: (0, 0)),
         pl.BlockSpec((d, d), lambda i: (0, 0))],
        [tile, tile, tile],
        [jax.ShapeDtypeStruct((t, d), F32), jax.ShapeDtypeStruct((t, d), BF16), jax.ShapeDtypeStruct((t, d), BF16)],
        [], ("arbitrary",), 48, (x, cv, yr, vec, lnv, wout))


def _emit_df(dx, f_ref, nvec_ref, df_ref, dgt_ref):
    df_ref[...] = (0.5 * nvec_ref[3:4, :] * dx).astype(BF16)
    dgt_ref[2:3, :] += 0.5 * jnp.sum(dx * f_ref[...].astype(F32), axis=0, keepdims=True)


def _ffn_bwd_w(df, gu, h, wo, tm, name, host=None):
    t, d = df.shape
    nj, fb = gu.shape[1], gu.shape[3]
    nt = t // tm
    sub = min(tm, ROW_GROUP)

    def body(ins, outs, scr):
        df_ref, gu_ref, h_ref, wo_ref = ins
        dgu_ref, dwi_ref, dwo_ref = outs
        accg_ref, accu_ref, acco_ref, dact_ref, act_ref = scr
        i = pl.program_id(1)

        @pl.when(i == 0)
        def _():
            accg_ref[...] = jnp.zeros_like(accg_ref)
            accu_ref[...] = jnp.zeros_like(accu_ref)
            acco_ref[...] = jnp.zeros_like(acco_ref)

        dact_ref[...] = _dot_nt(df_ref[...], wo_ref[...])
        for r in range(tm // sub):
            rows = slice(r * sub, (r + 1) * sub)
            g = gu_ref[0, rows, :].astype(F32)
            u = gu_ref[1, rows, :].astype(F32)
            dact = dact_ref[rows, :]
            sg = _sigmoid(g)
            sl = g * sg
            dgu_ref[0, rows, :] = (dact * u * (sg * (1.0 + g * (1.0 - sg)))).astype(BF16)
            dgu_ref[1, rows, :] = (dact * sl).astype(BF16)
            act_ref[rows, :] = (sl * u).astype(BF16)
        hb = h_ref[...]
        acco_ref[...] += _dot_tn(act_ref[...], df_ref[...])
        accg_ref[...] += _dot_tn(dgu_ref[0], hb)
        accu_ref[...] += _dot_tn(dgu_ref[1], hb)

        @pl.when(i == nt - 1)
        def _():
            dwi_ref[0] = accg_ref[...].astype(BF16)
            dwi_ref[1] = accu_ref[...].astype(BF16)
            dwo_ref[...] = acco_ref[...].astype(BF16)

    tile = pl.BlockSpec((tm, d), lambda j, i: (i, 0))
    return _hosted_call(
        body, host, name, (nj, nt),
        [tile,
         pl.BlockSpec((2, None, tm, fb), lambda j, i: (0, j, i, 0)),
         tile,
         pl.BlockSpec((fb, d), lambda j, i: (j, 0))],
        [pl.BlockSpec((2, None, tm, fb), lambda j, i: (0, j, i, 0)),
         pl.BlockSpec((2, None, fb, d), lambda j, i: (0, j, 0, 0)),
         pl.BlockSpec((None, fb, d), lambda j, i: (j, 0, 0))],
        [jax.ShapeDtypeStruct((2, nj, t, fb), BF16), jax.ShapeDtypeStruct((2, nj, fb, d), BF16),
         jax.ShapeDtypeStruct((nj, fb, d), BF16)],
        [pltpu.VMEM((fb, d), F32), pltpu.VMEM((fb, d), F32), pltpu.VMEM((fb, d), F32),
         pltpu.VMEM((tm, fb), F32), pltpu.VMEM((tm, fb), BF16)],
        ("arbitrary", "arbitrary"), 56, (df, gu, h, wo))


def _ffn_bwd_in(dxo, x, vec, dgu, wi, tm, name, host=None):
    t, d = x.shape
    nj, fb = wi.shape[1], wi.shape[2]
    nt = t // tm

    def body(ins, outs, scr):
        dxo_ref, x_ref, vec_ref, dgu_ref, wi_ref = ins
        dx_ref, dvec_ref = outs
        i = pl.program_id(0)

        @pl.when(i == 0)
        def _():
            dvec_ref[...] = jnp.zeros_like(dvec_ref)

        dh = jnp.zeros((tm, d), F32)
        for a in range(2):
            for k in range(nj):
                dh = dh + _dot(dgu_ref[a, k], wi_ref[a, k])
        dx_ref[...] = dxo_ref[...] + _adaln_bwd(x_ref[...], dh, vec_ref, dvec_ref)

        @pl.when(i == nt - 1)
        def _():
            _adaln_finish(vec_ref, dvec_ref)

    tile = pl.BlockSpec((tm, d), lambda i: (i, 0))
    return _hosted_call(
        body, host, name, (nt,),
        [tile, tile,
         pl.BlockSpec((8, d), lambda i: (0, 0)),
         pl.BlockSpec((2, nj, tm, fb), lambda i: (0, 0, i, 0)),
         pl.BlockSpec((2, nj, fb, d), lambda i: (0, 0, 0, 0))],
        [tile, pl.BlockSpec((8, d), lambda i: (0, 0))],
        [jax.ShapeDtypeStruct((t, d), F32), jax.ShapeDtypeStruct((8, d), F32)],
        [], ("arbitrary",), 60, (dxo, x, vec, dgu, wi))


def _mixout_bwd(dxo, ym, ycat, cv, hr, proj, vec, lnv, wout, tm, name, host=None):
    t, d = dxo.shape
    dc = cv.shape[1]
    nt = t // tm

    def body(ins, outs, scr):
        dxo_ref, ym_ref, yc_ref, cv_ref, hr_ref, uy_ref, vec_ref, lnv_ref, w_ref = ins
        dw_ref, dcv_ref, dhr_ref, duy_ref, dln_ref, dgt_ref = outs
        acc_ref, = scr
        i = pl.program_id(0)

        @pl.when(i == 0)
        def _():
            dln_ref[...] = jnp.zeros_like(dln_ref)
            dgt_ref[...] = jnp.zeros_like(dgt_ref)
            acc_ref[...] = jnp.zeros_like(acc_ref)

        dxo_v = dxo_ref[...]
        dym = (vec_ref[3:4, :] * dxo_v).astype(BF16)
        acc_ref[...] += _dot_tn(yc_ref[...], dym)
        dgt_ref[0:1, :] += jnp.sum(dxo_v * ym_ref[...].astype(F32), axis=0, keepdims=True)
        dyc = _dot_nt(dym, w_ref[0:dc, :])
        dyr = _dot_nt(dym, w_ref[dc:, :])
        rs, chat, z, sg = _ln_silu(cv_ref[...], lnv_ref)
        dz = dyc * (sg * (1.0 + z * (1.0 - sg)))
        dln_ref[0:1, :] += jnp.sum(dz * chat, axis=0, keepdims=True)
        dln_ref[1:2, :] += jnp.sum(dz, axis=0, keepdims=True)
        dchat = dz * lnv_ref[0:1, :]
        dcv_ref[...] = (rs * (dchat - jnp.mean(dchat, axis=-1, keepdims=True)
                              - chat * jnp.mean(dchat * chat, axis=-1, keepdims=True))).astype(BF16)
        ge, dge = _gelu_and_grad(uy_ref[...])
        dhr_ref[...] = (dyr * ge).astype(BF16)
        duy_ref[...] = (dyr * hr_ref[...] * dge).astype(BF16)

        @pl.when(i == nt - 1)
        def _():
            dw_ref[...] = acc_ref[...].astype(BF16)

    tile_d = pl.BlockSpec((tm, d), lambda i: (i, 0))
    tile_c = pl.BlockSpec((tm, dc), lambda i: (i, 0))
    full_w = pl.BlockSpec((d, d), lambda i: (0, 0))
    return _hosted_call(
        body, host, name, (nt,),
        [tile_d, tile_d, tile_d, tile_c, tile_c,
         pl.BlockSpec((tm, dc), lambda i: (i, 3)),
         pl.BlockSpec((8, d), lambda i: (0, 0)),
         pl.BlockSpec((8, dc), lambda i: (0, 0)),
         full_w],
        [full_w, tile_c, tile_c, tile_c,
         pl.BlockSpec((8, dc), lambda i: (0, 0)),
         pl.BlockSpec((8, d), lambda i: (0, 0))],
        [jax.ShapeDtypeStruct((d, d), BF16), jax.ShapeDtypeStruct((t, dc), BF16),
         jax.ShapeDtypeStruct((t, dc), BF16), jax.ShapeDtypeStruct((t, dc), BF16),
         jax.ShapeDtypeStruct((8, dc), F32), jax.ShapeDtypeStruct((8, d), F32)],
        [pltpu.VMEM((d, d), F32)], ("arbitrary",), 48, (dxo, ym, ycat, cv, hr, proj, vec, lnv, wout))


def _conv_bwd(proj, dcv, cw32, name):
    t = proj.shape[0]
    nblk = cw32.shape[1] // LANES
    ch = min(t, 128)

    def body(val_ref, gate_ref, dcv_ref, cw_ref, dval_ref, dgate_ref, dcw_ref, extu_ref, extd_ref):
        val = val_ref[...]
        sg = _sigmoid(gate_ref[...])
        extu_ref[0:32, :] = jnp.zeros((32, LANES), F32)
        extu_ref[32:, :] = val * sg
        dcv_v = dcv_ref[...].astype(F32)
        extd_ref[0:t, :] = dcv_v
        extd_ref[t:, :] = jnp.zeros((32, LANES), F32)
        for r in range(t // ch):
            acc = jnp.zeros((ch, LANES), F32)
            for k in range(CONV_W):
                off = r * ch + (CONV_W - 1 - k)
                acc = acc + cw_ref[k:k + 1, :] * extd_ref[off:off + ch, :]
            rows = slice(r * ch, (r + 1) * ch)
            sg_r = _sigmoid(gate_ref[rows, :])
            dval_ref[rows, :] = (acc * sg_r).astype(BF16)
            dgate_ref[rows, :] = (acc * val_ref[rows, :] * sg_r * (1.0 - sg_r)).astype(BF16)
        for k in range(CONV_W):
            off = 32 - (CONV_W - 1 - k)
            dcw_ref[k:k + 1, :] = jnp.sum(dcv_v * extu_ref[off:off + t, :], axis=0, keepdims=True)
        dcw_ref[31:32, :] = jnp.sum(dcv_v, axis=0, keepdims=True)

    blk = lambda off: pl.BlockSpec((t, LANES), lambda c: (0, off + c))
    return pl.pallas_call(
        body, name=name, grid=(nblk,),
        in_specs=[blk(0), blk(nblk), blk(0), pl.BlockSpec((32, LANES), lambda c: (0, c))],
        out_specs=[blk(0), blk(0), pl.BlockSpec((32, LANES), lambda c: (0, c))],
        out_shape=[jax.ShapeDtypeStruct((t, nblk * LANES), BF16), jax.ShapeDtypeStruct((t, nblk * LANES), BF16),
                   jax.ShapeDtypeStruct((32, nblk * LANES), F32)],
        scratch_shapes=[pltpu.VMEM((t + 32, LANES), F32), pltpu.VMEM((t + 32, LANES), F32)],
        compiler_params=_cp(("arbitrary",), 56),
    )(proj, proj, dcv, cw32)


def _rnn_bwd(proj, hr, dhr, rw8, rvec, wab, wib, name, host=None):
    t = proj.shape[0]
    nblk = rvec.shape[1] // LANES

    def body(ins, outs, scr):
        ux_ref, h_ref, dh_ref, rw_ref, rvec_ref, wa_ref, wi_ref = ins
        dux_ref, sm_ref, dwa_ref, dwi_ref = outs
        ext_ref, extd_ref, sa_ref, sb_ref, carry_ref = scr
        xr = _rnn_conv(ux_ref[...], rw_ref, rvec_ref, ext_ref)
        xb, r, ig, ls, a, mult = _rglru_gates(xr, wa_ref, wi_ref, rvec_ref)
        row = lax.broadcasted_iota(jnp.int32, (t, LANES), 0)
        a_next = jnp.where(row < t - 1, pltpu.roll(a, t - 1, 0), 0.0)
        _tiled_scan(a_next, dh_ref[...].astype(F32), True, sa_ref, sb_ref, carry_ref, extd_ref)
        g = extd_ref[0:t, :]
        hprev = jnp.where(row >= 1, pltpu.roll(h_ref[...], 1, 0), 0.0)
        da = g * hprev
        dmult = g * (ig * xr)
        dig = g * mult * xr
        dxr = g * mult * ig
        dlog_a = a * (da - dmult * a / mult)
        dr = dlog_a * (RG_C * ls)
        dls = RG_C * jnp.sum(dlog_a * r, axis=0, keepdims=True)
        dpr = dr * r * (1.0 - r)
        dpi = dig * ig * (1.0 - ig)
        dprb = dpr.astype(BF16)
        dpib = dpi.astype(BF16)
        dxr = dxr + _dot_nt(dprb, wa_ref[...]) + _dot_nt(dpib, wi_ref[...])
        dwa_ref[...] = _dot_tn(xb, dprb)
        dwi_ref[...] = _dot_tn(xb, dpib)
        extd_ref[0:t, :] = dxr
        extd_ref[t:, :] = jnp.zeros((8, LANES), F32)
        dux = rw_ref[RNN_CONV_W - 1:RNN_CONV_W, :] * dxr
        for k in range(RNN_CONV_W - 1):
            d = RNN_CONV_W - 1 - k
            dux = dux + rw_ref[k:k + 1, :] * extd_ref[d:d + t, :]
        dux_ref[...] = dux.astype(BF16)
        for k in range(RNN_CONV_W):
            d = RNN_CONV_W - 1 - k
            sm_ref[k:k + 1, :] = jnp.sum(dxr * ext_ref[8 - d:8 - d + t, :], axis=0, keepdims=True)
        sm_ref[4:5, :] = jnp.sum(dxr, axis=0, keepdims=True)
        sm_ref[5:6, :] = jnp.sum(dpr, axis=0, keepdims=True)
        sm_ref[6:7, :] = jnp.sum(dpi, axis=0, keepdims=True)
        sm_ref[7:8, :] = dls * _sigmoid(-rvec_ref[3:4, :])

    blk = lambda off: pl.BlockSpec((t, LANES), lambda c: (0, off + c))
    sq = pl.BlockSpec((None, LANES, LANES), lambda c: (c, 0, 0))
    return _hosted_call(
        body, host, name, (nblk,),
        [blk(2 * nblk), blk(0), blk(0),
         pl.BlockSpec((8, LANES), lambda c: (0, c)),
         pl.BlockSpec((8, LANES), lambda c: (0, c)), sq, sq],
        [blk(0), pl.BlockSpec((8, LANES), lambda c: (0, c)), sq, sq],
        [jax.ShapeDtypeStruct((t, nblk * LANES), BF16), jax.ShapeDtypeStruct((8, nblk * LANES), F32),
         jax.ShapeDtypeStruct((nblk, LANES, LANES), F32), jax.ShapeDtypeStruct((nblk, LANES, LANES), F32)],
        [pltpu.VMEM((t + 8, LANES), F32), pltpu.VMEM((t + 8, LANES), F32), pltpu.VMEM((t, LANES), F32),
         pltpu.VMEM((t, LANES), F32), pltpu.VMEM((t // SUBLANES, LANES), F32)],
        ("arbitrary",), 60, (proj, hr, dhr, rw8, rvec, wab, wib))


def _mixin_bwd(dxo, x, parts, vec, win, h, f, nvec, tm, name):
    t, d = x.shape
    nb, _, cb = win.shape
    dc = parts[0].shape[1]
    per = dc // cb
    nt = t // tm

    def body(dxo_ref, x_ref, p0, p1, p2, p3, vec_ref, w_ref, h_ref, f_ref, nvec_ref,
             dx_ref, dvec_ref, df_ref, dgt_ref, dw_ref, acc_ref):
        i = pl.program_id(0)

        @pl.when(i == 0)
        def _():
            dvec_ref[...] = jnp.zeros_like(dvec_ref)
            dgt_ref[...] = jnp.zeros_like(dgt_ref)
            acc_ref[...] = jnp.zeros_like(acc_ref)

        prefs = (p0, p1, p2, p3)
        hb = h_ref[...]
        for p in range(len(prefs)):
            acc_ref[p] += _dot_tn(hb, prefs[p][...])
        dh = jnp.zeros((tm, d), F32)
        for k in range(nb):
            dh = dh + _dot_nt(prefs[k // per][:, (k % per) * cb:(k % per + 1) * cb], w_ref[k])
        dx = dxo_ref[...] + _adaln_bwd(x_ref[...], dh, vec_ref, dvec_ref)
        dx_ref[...] = dx
        _emit_df(dx, f_ref, nvec_ref, df_ref, dgt_ref)

        @pl.when(i == nt - 1)
        def _():
            _adaln_finish(vec_ref, dvec_ref)
            for k in range(nb):
                dw_ref[k] = acc_ref[k // per, :, (k % per) * cb:(k % per + 1) * cb].astype(BF16)

    tile_d = pl.BlockSpec((tm, d), lambda i: (i, 0))
    tile_c = pl.BlockSpec((tm, dc), lambda i: (i, 0))
    tab = pl.BlockSpec((8, d), lambda i: (0, 0))
    wspec = pl.BlockSpec((nb, d, cb), lambda i: (0, 0, 0))
    return pl.pallas_call(
        body, name=name, grid=(nt,),
        in_specs=[tile_d, tile_d, tile_c, tile_c, tile_c, tile_c, tab, wspec, tile_d, tile_d, tab],
        out_specs=[tile_d, tab, tile_d, tab, wspec],
        out_shape=[jax.ShapeDtypeStruct((t, d), F32), jax.ShapeDtypeStruct((8, d), F32),
                   jax.ShapeDtypeStruct((t, d), BF16), jax.ShapeDtypeStruct((8, d), F32),
                   jax.ShapeDtypeStruct((nb, d, cb), BF16)],
        scratch_shapes=[pltpu.VMEM((len(parts), d, dc), F32)],
        compiler_params=_cp(("arbitrary",), 56),
    )(dxo, x, *parts, vec, win, h, f, nvec)


def _coords():
    return lax.axis_index("x"), lax.axis_index("y"), lax.axis_index("c")


def _flip(v, bit):
    return 1 - v if bit else v


def _gather_copy(outs, send_sems, recv_sems, a, k, block, to, src=None):
    dst = outs[a].at[block]
    return pltpu.make_async_remote_copy(
        src_ref=dst if src is None else src, dst_ref=dst,
        send_sem=send_sems.at[a, k], recv_sem=recv_sems.at[a, k],
        device_id=to, device_id_type=MESH_IDS)


def _gather_start(ins, outs, send_sems, recv_sems, loc_sems):
    x, y, c = _coords()
    me = 4 * x + 2 * y + c
    for a in range(len(ins)):
        pltpu.make_async_copy(ins[a], outs[a].at[me], loc_sems.at[a]).start()
    for a in range(len(ins)):
        _gather_copy(outs, send_sems, recv_sems, a, 0, me, (x, y, 1 - c), src=ins[a]).start()
        for j, (cx, cy) in enumerate([(1 - x, y), (x, 1 - y), (1 - x, 1 - y)]):
            _gather_copy(outs, send_sems, recv_sems, a, 1 + j, me, (cx, cy, c), src=ins[a]).start()


def _gather_finish(ins, outs, send_sems, recv_sems, loc_sems):
    x, y, c = _coords()
    me = 4 * x + 2 * y + c
    sib = (x, y, 1 - c)
    chips = [(1 - x, y), (x, 1 - y), (1 - x, 1 - y)]
    n = len(ins)
    for a in range(n):
        for j, (cx, cy) in enumerate(chips):
            blk = 4 * cx + 2 * cy + c
            _gather_copy(outs, send_sems, recv_sems, a, 1 + j, blk, sib).wait_recv()
            _gather_copy(outs, send_sems, recv_sems, a, 4 + j, blk, sib).start()
    for a in range(n):
        _gather_copy(outs, send_sems, recv_sems, a, 0, 4 * x + 2 * y + (1 - c), sib).wait_recv()
        for j, (cx, cy) in enumerate(chips):
            _gather_copy(outs, send_sems, recv_sems, a, 4 + j, 4 * cx + 2 * cy + (1 - c), sib).wait_recv()
    for a in range(n):
        _gather_copy(outs, send_sems, recv_sems, a, 0, me, sib, src=ins[a]).wait_send()
        for j, (cx, cy) in enumerate(chips):
            _gather_copy(outs, send_sems, recv_sems, a, 1 + j, me, (cx, cy, c), src=ins[a]).wait_send()
            _gather_copy(outs, send_sems, recv_sems, a, 4 + j, 4 * cx + 2 * cy + c, sib).wait_send()
        pltpu.make_async_copy(ins[a], outs[a].at[me], loc_sems.at[a]).wait()


def _gather_shapes(shards):
    return [jax.ShapeDtypeStruct((NDEV,) + s.shape, s.dtype) for s in shards]


def _gather_sems(n):
    return [pltpu.SemaphoreType.DMA((n, 7)), pltpu.SemaphoreType.DMA((n, 7)), pltpu.SemaphoreType.DMA((n,))]


def _sibling_copies(ins, outs, send_sems, recv_sems):
    x, y, c = _coords()
    return [pltpu.make_async_remote_copy(
        src_ref=ins[a].at[2 * q + (1 - c)], dst_ref=outs[a].at[q],
        send_sem=send_sems.at[a, q], recv_sem=recv_sems.at[a, q],
        device_id=(x, y, 1 - c), device_id_type=MESH_IDS) for a in range(len(ins)) for q in range(4)]


def _sibling_shapes(parts):
    return [jax.ShapeDtypeStruct((4,) + p.shape[1:], p.dtype) for p in parts]


def _chips_copies(ins, outs, send_sems, recv_sems):
    x, y, c = _coords()
    copies = []
    for a in range(len(ins)):
        for k, (kx, ky) in enumerate([(1, 0), (0, 1), (1, 1)]):
            tx, ty = _flip(x, kx), _flip(y, ky)
            copies.append(pltpu.make_async_remote_copy(
                src_ref=ins[a].at[2 * tx + ty], dst_ref=outs[a].at[k],
                send_sem=send_sems.at[a, k], recv_sem=recv_sems.at[a, k],
                device_id=(tx, ty, c), device_id_type=MESH_IDS))
    return copies


def _chips_shapes(sums):
    return [jax.ShapeDtypeStruct((3,) + s.shape[1:], s.dtype) for s in sums]


def _direct_copies(ins, outs, send_sems, recv_sems):
    x, y, c = _coords()
    me = 4 * x + 2 * y + c
    copies = []
    for a in range(len(ins)):
        for k in range(1, NDEV):
            kx, ky, kc = (k >> 2) & 1, (k >> 1) & 1, k & 1
            copies.append(pltpu.make_async_remote_copy(
                src_ref=ins[a], dst_ref=outs[a].at[me],
                send_sem=send_sems.at[a, k - 1], recv_sem=recv_sems.at[a, k - 1],
                device_id=(_flip(x, kx), _flip(y, ky), _flip(c, kc)), device_id_type=MESH_IDS))
    return copies


class _Exchange:
    def __init__(self, kind, arrays):
        self.kind, self.arrays, self.n = kind, list(arrays), len(arrays)

    def out_shapes(self):
        return {"gather": _gather_shapes, "direct": _gather_shapes, "sibling": _sibling_shapes,
                "chips": _chips_shapes}[self.kind](self.arrays)

    def sems(self):
        if self.kind in ("gather", "direct"):
            return _gather_sems(self.n)
        k = {"sibling": 4, "chips": 3}[self.kind]
        return [pltpu.SemaphoreType.DMA((self.n, k)), pltpu.SemaphoreType.DMA((self.n, k))]

    def _copies(self, ins, outs, sems):
        if self.kind == "direct":
            x, y, c = _coords()
            own = [pltpu.make_async_copy(ins[a], outs[a].at[4 * x + 2 * y + c], sems[2].at[a]) for a in range(self.n)]
            return own + _direct_copies(ins, outs, sems[0], sems[1])
        return {"sibling": _sibling_copies, "chips": _chips_copies}[self.kind](ins, outs, *sems)

    def start(self, ins, outs, sems):
        if self.kind == "gather":
            _gather_start(ins, outs, *sems)
        else:
            for cpy in self._copies(ins, outs, sems):
                cpy.start()

    def finish(self, ins, outs, sems):
        if self.kind == "gather":
            _gather_finish(ins, outs, *sems)
        else:
            for cpy in self._copies(ins, outs, sems):
                cpy.wait()


def _hosted_call(body, host, name, grid, in_specs, out_specs, out_shape, scratch, sem, vmem_mb, args, manual=False):
    n = host.n if host else 0
    ni, no, ns = len(in_specs), len(out_specs), len(scratch)

    def full(*refs):
        ins, h_in = refs[:ni], refs[ni:ni + n]
        outs, h_out = refs[ni + n:ni + n + no], refs[ni + n + no:ni + 2 * n + no]
        scr, sems = refs[ni + 2 * n + no:ni + 2 * n + no + ns], refs[ni + 2 * n + no + ns:]
        if manual:
            body(ins, outs, scr, lambda: host.start(h_in, h_out, sems), lambda: host.finish(h_in, h_out, sems))
            return
        if host and grid:
            first = functools.reduce(lambda a, b: a & b, [pl.program_id(k) == 0 for k in range(len(grid))])
            last = functools.reduce(lambda a, b: a & b, [pl.program_id(k) == g - 1 for k, g in enumerate(grid)])

            @pl.when(first)
            def _():
                host.start(h_in, h_out, sems)
        elif host:
            host.start(h_in, h_out, sems)

        body(ins, outs, scr)

        if host and grid:
            @pl.when(last)
            def _():
                host.finish(h_in, h_out, sems)
        elif host:
            host.finish(h_in, h_out, sems)

    anyspec = pl.BlockSpec(memory_space=pl.ANY)
    return pl.pallas_call(
        full, name=name, grid=grid,
        in_specs=list(in_specs) + [anyspec] * n, out_specs=list(out_specs) + [anyspec] * n,
        out_shape=list(out_shape) + (host.out_shapes() if host else []),
        scratch_shapes=list(scratch) + (host.sems() if host else []),
        compiler_params=_cp(sem, vmem_mb),
    )(*args, *(host.arrays if host else []))


def _exchange(host, name, after=()):
    n, na = host.n, len(after)

    def body(*refs):
        ins, outs, sems = refs[:n], refs[n + na:2 * n + na], refs[2 * n + na:]
        host.start(ins, outs, sems)
        host.finish(ins, outs, sems)

    anyspec = pl.BlockSpec(memory_space=pl.ANY)
    return pl.pallas_call(
        body, name=name, in_specs=[anyspec] * (n + na), out_specs=[anyspec] * n,
        out_shape=host.out_shapes(), scratch_shapes=host.sems(),
    )(*host.arrays, *after)


def _chips_split_start(sums, name):
    n = len(sums)
    hbm = pl.BlockSpec(memory_space=pltpu.HBM)
    sem = pl.BlockSpec(memory_space=pltpu.SEMAPHORE)

    def body(*refs):
        ins, lands = refs[:n], refs[n:2 * n]
        sems = refs[2 * n:2 * n + 6 * n]
        token = refs[-1]
        for cpy in _chips_copies(ins, lands, _SemGrid(sems[:3 * n], 3), _SemGrid(sems[3 * n:], 3)):
            cpy.start()
        token[...] = jnp.zeros_like(token)

    land_shapes = _chips_shapes(sums)
    lands = [pltpu.with_memory_space_constraint(lax.empty(s.shape, s.dtype), pltpu.HBM) for s in land_shapes]
    return pl.pallas_call(
        body, name=name,
        out_shape=(*[pltpu.SemaphoreType.DMA(())] * (6 * n),
                   *[pltpu.HBM(s.shape, s.dtype) for s in sums],
                   *[pltpu.HBM(s.shape, s.dtype) for s in land_shapes],
                   jax.ShapeDtypeStruct((8, LANES), F32)),
        in_specs=[hbm] * (2 * n),
        out_specs=(*[sem] * (6 * n), *[hbm] * (2 * n), pl.BlockSpec(memory_space=pltpu.VMEM)),
        input_output_aliases={i: 6 * n + i for i in range(2 * n)},
        compiler_params=pltpu.CompilerParams(has_side_effects=pltpu.SideEffectType.DATAFLOW_SIDE_EFFECTING),
    )(*[pltpu.with_memory_space_constraint(s, pltpu.HBM) for s in sums], *lands)


def _owner_copies(parts, lands, send_sems, recv_sems):
    x, y, c = _coords()
    copies = []
    for a in range(len(parts)):
        for k in range(1, NDEV):
            px, py, pc = _flip(x, (k >> 2) & 1), _flip(y, (k >> 1) & 1), _flip(c, k & 1)
            copies.append(pltpu.make_async_remote_copy(
                src_ref=parts[a].at[4 * px + 2 * py + pc], dst_ref=lands[a].at[k - 1],
                send_sem=send_sems.at[a, k - 1], recv_sem=recv_sems.at[a, k - 1],
                device_id=(px, py, pc), device_id_type=MESH_IDS))
    return copies


def _owner_split_start(parts, name):
    n = len(parts)
    k = NDEV - 1
    hbm = pl.BlockSpec(memory_space=pltpu.HBM)
    sem = pl.BlockSpec(memory_space=pltpu.SEMAPHORE)

    def body(*refs):
        ins, lands = refs[:n], refs[n:2 * n]
        sems = refs[2 * n:2 * n + 2 * k * n]
        for cpy in _owner_copies(ins, lands, _SemGrid(sems[:k * n], k), _SemGrid(sems[k * n:], k)):
            cpy.start()
        refs[-1][...] = jnp.zeros_like(refs[-1])

    land_shapes = [jax.ShapeDtypeStruct((k,) + p.shape[1:], p.dtype) for p in parts]
    lands = [pltpu.with_memory_space_constraint(lax.empty(s.shape, s.dtype), pltpu.HBM) for s in land_shapes]
    return pl.pallas_call(
        body, name=name,
        out_shape=(*[pltpu.SemaphoreType.DMA(())] * (2 * k * n),
                   *[pltpu.HBM(p.shape, p.dtype) for p in parts],
                   *[pltpu.HBM(s.shape, s.dtype) for s in land_shapes],
                   jax.ShapeDtypeStruct((8, LANES), F32)),
        in_specs=[hbm] * (2 * n),
        out_specs=(*[sem] * (2 * k * n), *[hbm] * (2 * n), pl.BlockSpec(memory_space=pltpu.VMEM)),
        input_output_aliases={i: 2 * k * n + i for i in range(2 * n)},
        compiler_params=pltpu.CompilerParams(has_side_effects=pltpu.SideEffectType.DATAFLOW_SIDE_EFFECTING),
    )(*[pltpu.with_memory_space_constraint(p, pltpu.HBM) for p in parts], *lands)


def _owner_split_wait(started, n, after, name):
    k = NDEV - 1
    sems, thru = started[:2 * k * n], started[2 * k * n:2 * k * n + 2 * n]
    hbm = pl.BlockSpec(memory_space=pltpu.HBM)
    sem = pl.BlockSpec(memory_space=pltpu.SEMAPHORE)

    def body(*refs):
        ins, lands = refs[:n], refs[n:2 * n]
        s = refs[2 * n:2 * n + 2 * k * n]
        for cpy in _owner_copies(ins, lands, _SemGrid(s[:k * n], k), _SemGrid(s[k * n:], k)):
            cpy.wait_send()
            cpy.wait_recv()

    outs = pl.pallas_call(
        body, name=name,
        out_shape=tuple(pltpu.HBM(a.shape, a.dtype) for a in thru),
        in_specs=[hbm] * (2 * n) + [sem] * (2 * k * n) + [pl.BlockSpec(memory_space=pl.ANY)],
        out_specs=tuple([hbm] * (2 * n)),
        input_output_aliases={i: i for i in range(2 * n)},
        compiler_params=pltpu.CompilerParams(has_side_effects=pltpu.SideEffectType.DATAFLOW_SIDE_EFFECTING),
    )(*thru, *sems, after)
    return list(outs[:n]), list(outs[n:])


def _owner_final(part, recv, me, w, m, v, tr, name):
    _, r, c = part.shape

    def body(me_ref, p_ref, r_ref, w_ref, m_ref, v_ref, g_ref, d_ref, mo_ref, vo_ref):
        g = p_ref[...].astype(F32)
        for k in range(NDEV - 1):
            g = g + r_ref[k].astype(F32)
        g_ref[...] = g
        d_ref[...], mo_ref[...], vo_ref[...] = _adamw_math(w_ref[...], g, m_ref[...], v_ref[...])

    tile = pl.BlockSpec((tr, c), lambda i, s: (i, 0))
    sds = jax.ShapeDtypeStruct((r, c), F32)
    return pl.pallas_call(
        body, name=name,
        grid_spec=pltpu.PrefetchScalarGridSpec(
            num_scalar_prefetch=1, grid=(r // tr,),
            in_specs=[pl.BlockSpec((None, tr, c), lambda i, s: (s[0], i, 0)),
                      pl.BlockSpec((NDEV - 1, tr, c), lambda i, s: (0, i, 0)), tile, tile, tile],
            out_specs=[tile] * 4),
        out_shape=[sds] * 4,
        compiler_params=_cp(("arbitrary",), 48),
    )(me, part, recv, w, m, v)


class _SemGrid:
    def __init__(self, sems, k):
        self.sems, self.k = sems, k

    @property
    def at(self):
        return self

    def __getitem__(self, idx):
        return self.sems[idx[0] * self.k + idx[1]]


def _chips_split_wait(started, n, after, name):
    sems = started[:6 * n]
    thru = started[6 * n:8 * n]
    hbm = pl.BlockSpec(memory_space=pltpu.HBM)
    sem = pl.BlockSpec(memory_space=pltpu.SEMAPHORE)

    def body(*refs):
        ins, lands = refs[:n], refs[n:2 * n]
        s = refs[2 * n:2 * n + 6 * n]
        for cpy in _chips_copies(ins, lands, _SemGrid(s[:3 * n], 3), _SemGrid(s[3 * n:], 3)):
            cpy.wait_send()
            cpy.wait_recv()

    outs = pl.pallas_call(
        body, name=name,
        out_shape=tuple(pltpu.HBM(a.shape, a.dtype) for a in thru),
        in_specs=[hbm] * (2 * n) + [sem] * (6 * n) + [pl.BlockSpec(memory_space=pl.ANY)],
        out_specs=tuple([hbm] * (2 * n)),
        input_output_aliases={i: i for i in range(2 * n)},
        compiler_params=pltpu.CompilerParams(has_side_effects=pltpu.SideEffectType.DATAFLOW_SIDE_EFFECTING),
    )(*thru, *sems, after)
    return list(outs[n:])


def _chip_sum(part, recv, sel, tr, name):
    _, _, r, c = part.shape

    def body(sel_ref, p_ref, r_ref, cs_ref, own_ref):
        q = pl.program_id(1)
        s = p_ref[...].astype(F32) + r_ref[...].astype(F32)
        cs_ref[...] = s.astype(BF16)

        @pl.when(q == sel_ref[1])
        def _():
            own_ref[...] = s

    return pl.pallas_call(
        body, name=name,
        grid_spec=pltpu.PrefetchScalarGridSpec(
            num_scalar_prefetch=1, grid=(r // tr, 4),
            in_specs=[pl.BlockSpec((None, None, tr, c), lambda i, q, s: (q, s[0], i, 0)),
                      pl.BlockSpec((None, tr, c), lambda i, q, s: (q, i, 0))],
            out_specs=[pl.BlockSpec((None, tr, c), lambda i, q, s: (q, i, 0)),
                       pl.BlockSpec((tr, c), lambda i, q, s: (i, 0))]),
        out_shape=[jax.ShapeDtypeStruct((4, r, c), BF16), jax.ShapeDtypeStruct((r, c), F32)],
        compiler_params=_cp(("arbitrary", "arbitrary"), 48),
    )(sel, part, recv)


def _gather_direct(src_ref, buf_ref, send_sems, recv_sems):
    x, y, c = _coords()
    me = 4 * x + 2 * y + c
    buf_ref[me] = src_ref[...]
    copies = []
    for k in range(1, NDEV):
        kx, ky, kc = (k >> 2) & 1, (k >> 1) & 1, k & 1
        copies.append(pltpu.make_async_remote_copy(
            src_ref=src_ref, dst_ref=buf_ref.at[me],
            send_sem=send_sems.at[k - 1], recv_sem=recv_sems.at[k - 1],
            device_id=(_flip(x, kx), _flip(y, ky), _flip(c, kc)), device_id_type=MESH_IDS))
    for cpy in copies:
        cpy.start()

    def wait():
        for k in range(1, NDEV):
            kx, ky, kc = (k >> 2) & 1, (k >> 1) & 1, k & 1
            peer = 4 * _flip(x, kx) + 2 * _flip(y, ky) + _flip(c, kc)
            pltpu.make_async_remote_copy(
                src_ref=src_ref, dst_ref=buf_ref.at[peer],
                send_sem=send_sems.at[k - 1], recv_sem=recv_sems.at[k - 1],
                device_id=(x, y, c), device_id_type=MESH_IDS).wait_recv()
        for cpy in copies:
            cpy.wait_send()

    return me, wait


def _mod_exchange(c_row, wmod, bmod, wfmod, bfmod, name, host=None):
    d = c_row.shape[1]
    nm, nf = wmod.shape[1], wfmod.shape[1]
    nw = nm + nf

    def body(ins, outs, scr, host_start, host_finish):
        c_ref, wm_ref, bm_ref, wf_ref, bf_ref = ins
        cs_ref, mod_ref, fmod_ref = outs
        slab_ref, csbuf_ref, mslab_ref, mbuf_ref, s1, r1, s2, r2 = scr
        cv = c_ref[...]
        slab_ref[...] = jnp.broadcast_to(cv * _sigmoid(cv), (8, d))
        _, wait1 = _gather_direct(slab_ref, csbuf_ref, s1, r1)
        wait1()
        host_start()
        for b in range(NDEV):
            cs_ref[b:b + 1, :] = csbuf_ref[b, 0:1, :]
        cs = cs_ref[...]
        mslab_ref[:, 0:nm] = jnp.dot(cs, wm_ref[...], precision=HI, preferred_element_type=F32) + bm_ref[...]
        mslab_ref[:, nm:] = jnp.dot(cs, wf_ref[...], precision=HI, preferred_element_type=F32) + bf_ref[...]
        me, wait2 = _gather_direct(mslab_ref, mbuf_ref, s2, r2)
        host_finish()
        wait2()
        mine = lax.broadcasted_iota(jnp.int32, (8, nw), 0) == me
        for k in range(NDEV):
            rowk = jnp.sum(jnp.where(mine, mbuf_ref[k], 0.0), axis=0, keepdims=True)
            mod_ref[k:k + 1, :] = rowk[:, 0:nm]
            fmod_ref[k:k + 1, :] = rowk[:, nm:]

    vm = pl.BlockSpec(memory_space=pltpu.VMEM)
    return _hosted_call(
        body, host, name, (), [vm] * 5, [vm] * 3,
        [jax.ShapeDtypeStruct((NDEV, d), F32), jax.ShapeDtypeStruct((NDEV, nm), F32),
         jax.ShapeDtypeStruct((NDEV, nf), F32)],
        [pltpu.VMEM((8, d), F32), pltpu.VMEM((NDEV, 8, d), F32),
         pltpu.VMEM((8, nw), F32), pltpu.VMEM((NDEV, 8, nw), F32),
         pltpu.SemaphoreType.DMA((7,)), pltpu.SemaphoreType.DMA((7,)),
         pltpu.SemaphoreType.DMA((7,)), pltpu.SemaphoreType.DMA((7,))],
        None, 40, (c_row, wmod, bmod, wfmod, bfmod), manual=True)


def _table_sum(tabs, name):
    n = len(tabs)

    def body(*refs):
        for a in range(n):
            tot = refs[a][0]
            for k in range(1, NDEV):
                tot = tot + refs[a][k]
            refs[n + a][...] = tot

    vm = pl.BlockSpec(memory_space=pltpu.VMEM)
    return pl.pallas_call(
        body, name=name, in_specs=[vm] * n, out_specs=[vm] * n,
        out_shape=[jax.ShapeDtypeStruct(tb.shape[1:], F32) for tb in tabs],
    )(*tabs)


def _adamw_math(w, g, m, v):
    m = ADAM_B1 * m + (1.0 - ADAM_B1) * g
    v = ADAM_B2 * v + (1.0 - ADAM_B2) * (g * g)
    m_hat = m / (1.0 - ADAM_B1 ** ADAM_STEP)
    v_hat = v / (1.0 - ADAM_B2 ** ADAM_STEP)
    delta = -ADAM_LR * (m_hat / (jnp.sqrt(v_hat) + ADAM_EPS) + ADAM_WD * w)
    return delta, m, v


def _adamw_small(params, name):
    n = len(params)

    def body(*refs):
        for p in range(n):
            w_ref, g_ref, m_ref, v_ref = refs[4 * p:4 * p + 4]
            d_ref, mo_ref, vo_ref = refs[4 * n + 3 * p:4 * n + 3 * p + 3]
            d_ref[...], mo_ref[...], vo_ref[...] = _adamw_math(w_ref[...], g_ref[...], m_ref[...], v_ref[...])

    vm = pl.BlockSpec(memory_space=pltpu.VMEM)
    flat = [a for p in params for a in p]
    outs = pl.pallas_call(
        body, name=name, in_specs=[vm] * (4 * n), out_specs=[vm] * (3 * n),
        out_shape=[jax.ShapeDtypeStruct(p[0].shape, F32) for p in params for _ in range(3)])(*flat)
    return [outs[3 * p:3 * p + 3] for p in range(n)]


def _rs_final(own, recv, w, m, v, tr, name):
    r, c = own.shape

    def body(o_ref, r_ref, w_ref, m_ref, v_ref, g_ref, d_ref, mo_ref, vo_ref):
        g = o_ref[...] + r_ref[0].astype(F32) + r_ref[1].astype(F32) + r_ref[2].astype(F32)
        g_ref[...] = g
        d_ref[...], mo_ref[...], vo_ref[...] = _adamw_math(w_ref[...], g, m_ref[...], v_ref[...])

    tile = pl.BlockSpec((tr, c), lambda i: (i, 0))
    sds = jax.ShapeDtypeStruct((r, c), F32)
    return pl.pallas_call(
        body, name=name, grid=(r // tr,),
        in_specs=[tile, pl.BlockSpec((3, tr, c), lambda i: (0, i, 0)), tile, tile, tile],
        out_specs=[tile] * 4, out_shape=[sds] * 4,
        compiler_params=_cp(("arbitrary",), 48),
    )(own, recv, w, m, v)


def _mod_weight_update(cs, dm, w, m, v, tr, name):
    r, c = w.shape

    def body(cs_ref, dm_ref, w_ref, m_ref, v_ref, g_ref, d_ref, mo_ref, vo_ref):
        g = lax.dot_general(cs_ref[...], dm_ref[...], (((0,), (0,)), ((), ())),
                            precision=HI, preferred_element_type=F32)
        g_ref[...] = g
        d_ref[...], mo_ref[...], vo_ref[...] = _adamw_math(w_ref[...], g, m_ref[...], v_ref[...])

    tile = pl.BlockSpec((tr, c), lambda i: (i, 0))
    sds = jax.ShapeDtypeStruct((r, c), F32)
    return pl.pallas_call(
        body, name=name, grid=(r // tr,),
        in_specs=[pl.BlockSpec((NDEV, tr), lambda i: (0, i)), pl.BlockSpec((NDEV, c), lambda i: (0, 0)),
                  tile, tile, tile],
        out_specs=[tile] * 4, out_shape=[sds] * 4,
        compiler_params=_cp(("arbitrary",), 48),
    )(cs, dm, w, m, v)


def _rows(*vs):
    d = vs[0].shape[-1]
    rows = [v.reshape(1, d) for v in vs]
    return jnp.concatenate(rows + [jnp.zeros((8 - len(rows), d), F32)], axis=0)


def _block_diag_pairs(w):
    hd = w.shape[-1]
    z = jnp.zeros((w.shape[0] // 2, hd, hd), w.dtype)
    top = jnp.concatenate([w[0::2], z], axis=2)
    bot = jnp.concatenate([z, w[1::2]], axis=2)
    return jnp.concatenate([top, bot], axis=1).astype(BF16)


def _diag_pairs(g):
    hd = g.shape[-1] // 2
    both = jnp.stack([g[:, :hd, :hd], g[:, hd:, hd:]], axis=1)
    return both.reshape(2 * g.shape[0], hd, hd)


def kernel(x, c, w_mod, b_mod, g_ffn1, w_ffn1_in, w_ffn1_out, g_mix, w_in, conv_w, conv_b, ln_g, ln_b, rnn_conv_w, rnn_conv_b, w_a, b_a, w_i, b_i, lru_lambda, w_out, g_ffn2, w_ffn2_in, w_ffn2_out, w_fmod, b_fmod, g_final, loss_target, m_w_mod, m_b_mod, m_g_ffn1, m_w_ffn1_in, m_w_ffn1_out, m_g_mix, m_w_in, m_conv_w, m_conv_b, m_ln_g, m_ln_b, m_rnn_conv_w, m_rnn_conv_b, m_w_a, m_b_a, m_w_i, m_b_i, m_lru_lambda, m_w_out, m_g_ffn2, m_w_ffn2_in, m_w_ffn2_out, m_w_fmod, m_b_fmod, m_g_final, v_w_mod, v_b_mod, v_g_ffn1, v_w_ffn1_in, v_w_ffn1_out, v_g_mix, v_w_in, v_conv_w, v_conv_b, v_ln_g, v_ln_b, v_rnn_conv_w, v_rnn_conv_b, v_w_a, v_b_a, v_w_i, v_b_i, v_lru_lambda, v_w_out, v_g_ffn2, v_w_ffn2_in, v_w_ffn2_out, v_w_fmod, v_b_fmod, v_g_final):
    t, d = x.shape[1], x.shape[2]
    fb = w_ffn1_in.shape[2]
    nm = w_mod.shape[2]
    nf = w_fmod.shape[1]
    dc = conv_b.shape[1]
    cl = conv_w.shape[2]
    tm = min(TOKEN_TILE, t)
    me = 4 * lax.axis_index("x") + 2 * lax.axis_index("y") + lax.axis_index("c")

    tr = jnp.transpose
    bmod_l = lax.dynamic_slice(b_mod, (0, me * nm), (1, nm))
    bfmod_l = lax.dynamic_slice(b_fmod.reshape(1, -1), (0, me * nf), (1, nf))
    cwl = jnp.concatenate([conv_w[0], jnp.zeros((1, cl), F32), rnn_conv_w[0], jnp.zeros((4, cl), F32)], axis=0)
    cs, mod_rows, fmod_rows, wi1, wo1, cwg = _mod_exchange(
        c, w_mod[0], bmod_l, w_fmod, bfmod_l, "mod_and_gather_ffn1",
        host=_Exchange("gather", [tr(w_ffn1_in[0]).astype(BF16), w_ffn1_out[0].astype(BF16), cwl]))
    wi1 = wi1.reshape(2, 4, fb, d)
    wo1 = wo1.reshape(4 * fb, d)
    mod = mod_rows.reshape(9, d)
    fmod = fmod_rows.reshape(2, d)
    vec1 = _rows(g_ffn1, mod[0], mod[1], mod[2])
    vecm = _rows(g_mix, mod[3], mod[4], mod[5])
    vec3 = _rows(g_ffn2, mod[6], mod[7], mod[8])
    vecf = _rows(g_final, fmod[0], fmod[1])

    xin = x[0]
    later = [w_in[0].astype(BF16), w_out[0].astype(BF16), tr(w_ffn2_in[0]).astype(BF16), w_ffn2_out[0].astype(BF16)]
    x1, h1, gu1, f1, win, wout, wi2, wo2 = _ffn_fwd(xin, vec1, wi1, wo1, tm, "ffn1_fwd",
                                                    host=_Exchange("gather", later))
    wi2 = wi2.reshape(2, 4, fb, d)
    wo2 = wo2.reshape(4 * fb, d)
    wout = wout.reshape(d, d)
    h2, proj = _mix_in(x1, vecm, win, tm, "mix_in")
    lnv = _rows(ln_g, ln_b)
    rvec = _rows(rnn_conv_b, b_a, b_i, lru_lambda)
    wab = _block_diag_pairs(w_a[0])
    wib = _block_diag_pairs(w_i[0])
    cwf = jnp.transpose(cwg, (1, 0, 2)).reshape(40, NDEV * cl)
    cw32 = jnp.concatenate([cwf[0:CONV_W], conv_b], axis=0)
    rw8 = cwf[32:40]

    (cv,) = _conv_fwd(proj, cw32, "conv_fwd")
    hr, yr = _rnn_fwd(proj, rw8, rvec, wab, wib, "rnn_fwd")
    x2, ym, ycat = _mix_out(x1, cv, yr, vecm, lnv, wout, tm, "mix_out")
    h3, gu3, dx3, dvf, df3, dva3 = _ffn_fwd_loss(x2, vec3, wi2, wo2, loss_target[0], vecf, tm, "ffn2_fwd_loss")
    sel = jnp.stack([lax.axis_index("c"), 2 * lax.axis_index("x") + lax.axis_index("y")]).astype(jnp.int32)
    row_tile = {"w_ffn1_in": fb // 4, "w_ffn1_out": fb // 4, "w_in": 512, "w_out": 128,
                "w_ffn2_in": fb // 4, "w_ffn2_out": fb // 4}

    def chip_sums(names, partials, from_sib):
        out = [_chip_sum(p.reshape((4, 2) + p.shape[1:]), r, sel, p.shape[1], "chip_sum_" + nm_)
               for nm_, p, r in zip(names, partials, from_sib)]
        return [o[0] for o in out], [o[1] for o in out]

    dgu3, p_wi2, p_wo2 = _ffn_bwd_w(df3, gu3, h3, wo2, tm, "ffn2_bwd_w")
    p_wi2 = p_wi2.reshape(NDEV, fb, d)
    p_wo2 = p_wo2.reshape(NDEV, fb // 2, d)
    names2 = ["w_ffn2_in", "w_ffn2_out"]
    started2 = _owner_split_start([p_wi2, p_wo2], "rs_owner_ffn2_start")
    dx2, dv3 = _ffn_bwd_in(dx3, x2, vec3 + started2[-1][0:1, 0:1], dgu3, wi2, tm, "ffn2_bwd_in")
    p_wout, dcv, dhr, duy, dln, dgt2 = _mixout_bwd(
        dx2, ym, ycat, cv, hr, proj, vecm, lnv, wout, tm, "mixout_bwd")
    p_wout = p_wout.reshape(NDEV, d // NDEV, d)
    dval, dgate, dcw = _conv_bwd(proj, dcv, cw32, "conv_bwd")
    dux, rsm, dwab, dwib = _rnn_bwd(proj, hr, dhr, rw8, rvec, wab, wib, "rnn_bwd")
    parts = [dval, dgate, dux, duy]
    dx1, dvm, df1, dva1, p_win = _mixin_bwd(dx2, x1, parts, vecm, win, h2, f1, vec1, tm, "mixin_bwd")
    namesm = ["w_in", "w_out"]
    lane_pad = lambda v: jnp.concatenate([v, jnp.zeros_like(v)], axis=1)
    startedm = _owner_split_start([p_win, p_wout], "rs_owner_mix_start")
    early = jnp.concatenate([dva3, dv3, dvf, dvm, dgt2, dva1 + startedm[-1][0:1, 0:1], dcw.reshape(16, d),
                             lane_pad(dln), lane_pad(rsm),
                             _diag_pairs(dwab).reshape(32, d), _diag_pairs(dwib).reshape(32, d)], axis=0)
    dgu1, p_wi1, p_wo1, all_early = _ffn_bwd_w(
        df1, gu1, h1, wo1, tm, "ffn1_bwd_w", host=_Exchange("direct", [early]))
    p_wi1 = p_wi1.reshape(NDEV, fb, d)
    p_wo1 = p_wo1.reshape(NDEV, fb // 2, d)
    names1 = ["w_ffn1_in", "w_ffn1_out"]
    sums1, owns1 = chip_sums(names1, [p_wi1, p_wo1],
                             _exchange(_Exchange("sibling", [p_wi1, p_wo1]), "rs_sibling_ffn1"))
    started = _chips_split_start(sums1, "rs_chips_ffn1_start")
    dx0, dv1 = _ffn_bwd_in(dx1, xin, vec1 + started[-1][0:1, 0:1], dgu1, wi1, tm, "ffn1_bwd_in")
    direct = {}
    for names, st, label in ((names2, started2, "ffn2"), (namesm, startedm, "mix")):
        kept, got = _owner_split_wait(st, len(names), dx0, "rs_owner_%s_wait" % label)
        direct.update(zip(names, zip(kept, got)))
    from_chips = {}
    owns = dict(zip(names1, owns1))
    me_arr = me.reshape(1).astype(jnp.int32)

    big = {"w_ffn1_in": (tr(w_ffn1_in[0]), tr(m_w_ffn1_in[0]), tr(v_w_ffn1_in[0])),
           "w_ffn1_out": (w_ffn1_out[0], m_w_ffn1_out[0], v_w_ffn1_out[0]),
           "w_in": (w_in[0], m_w_in[0], v_w_in[0]), "w_out": (w_out[0], m_w_out[0], v_w_out[0]),
           "w_ffn2_in": (tr(w_ffn2_in[0]), tr(m_w_ffn2_in[0]), tr(v_w_ffn2_in[0])),
           "w_ffn2_out": (w_ffn2_out[0], m_w_ffn2_out[0], v_w_ffn2_out[0])}
    res = {}

    def final_sum(nm_):
        if nm_ in direct:
            out4 = _owner_final(*direct[nm_], me_arr, *big[nm_], row_tile[nm_], "rs_final_" + nm_)
        else:
            out4 = _rs_final(owns[nm_], from_chips[nm_], *big[nm_], row_tile[nm_], "rs_final_" + nm_)
        res[nm_] = [(tr(o) if nm_ in ("w_ffn1_in", "w_ffn2_in") else o)[None] for o in out4]
        return out4[0]

    done = [final_sum(nm_) for nm_ in namesm + names2]
    dfm_all = jnp.concatenate([all_early[:, 17], all_early[:, 19]], axis=1)
    dfm_l = lax.dynamic_slice(dfm_all, (0, me * nf), (NDEV, nf))
    res["w_fmod"] = list(_mod_weight_update(cs, dfm_l, w_fmod, m_w_fmod, v_w_fmod, 256, "w_fmod_update"))
    (all_late,) = _exchange(_Exchange("direct", [dv1]), "late_table", after=done + [res["w_fmod"][0]])
    from_chips["w_ffn1_in"], from_chips["w_ffn1_out"] = _chips_split_wait(
        started, len(sums1), all_late, "rs_chips_ffn1_wait")
    for nm_ in names1:
        final_sum(nm_)
    te, tl = _table_sum([all_early, all_late], "table_sum")
    loss = jnp.sum(te[20])

    mod_rows_of = lambda e, l: [l[1], l[3], e[42], e[25], e[27], e[32], e[9], e[11], e[2]]
    dm_all = jnp.concatenate(mod_rows_of(jnp.swapaxes(all_early, 0, 1), jnp.swapaxes(all_late, 0, 1)), axis=1)
    dm_l = lax.dynamic_slice(dm_all, (0, me * nm), (NDEV, nm))
    res["w_mod"] = [o[None] for o in
                    _mod_weight_update(cs, dm_l, w_mod[0], m_w_mod[0], v_w_mod[0], 256, "w_mod_update")]

    dcw_f = te[48:64].reshape(32, dc)
    rsm_f = te[72:80, 0:dc]
    small_grads = {
        "b_mod": jnp.concatenate(mod_rows_of(te, tl)).reshape(1, 9 * d),
        "b_fmod": jnp.concatenate([te[17], te[19]]),
        "g_ffn1": tl[0:1], "g_mix": te[24:25], "g_ffn2": te[8:9], "g_final": te[16],
        "conv_w": lax.dynamic_slice(dcw_f, (0, me * cl), (CONV_W, cl))[None],
        "conv_b": dcw_f[31:32],
        "ln_g": te[64:65, 0:dc], "ln_b": te[65:66, 0:dc],
        "rnn_conv_w": lax.dynamic_slice(rsm_f, (0, me * cl), (RNN_CONV_W, cl))[None],
        "rnn_conv_b": rsm_f[4:5], "b_a": rsm_f[5:6], "b_i": rsm_f[6:7], "lru_lambda": rsm_f[7:8],
        "w_a": te[80:112].reshape(w_a.shape), "w_i": te[112:144].reshape(w_i.shape),
    }
    small_params = {
        "b_mod": (b_mod, m_b_mod, v_b_mod), "b_fmod": (b_fmod, m_b_fmod, v_b_fmod),
        "g_ffn1": (g_ffn1, m_g_ffn1, v_g_ffn1), "g_mix": (g_mix, m_g_mix, v_g_mix),
        "g_ffn2": (g_ffn2, m_g_ffn2, v_g_ffn2), "g_final": (g_final, m_g_final, v_g_final),
        "conv_w": (conv_w, m_conv_w, v_conv_w), "conv_b": (conv_b, m_conv_b, v_conv_b),
        "ln_g": (ln_g, m_ln_g, v_ln_g), "ln_b": (ln_b, m_ln_b, v_ln_b),
        "rnn_conv_w": (rnn_conv_w, m_rnn_conv_w, v_rnn_conv_w),
        "rnn_conv_b": (rnn_conv_b, m_rnn_conv_b, v_rnn_conv_b),
        "w_a": (w_a, m_w_a, v_w_a), "b_a": (b_a, m_b_a, v_b_a),
        "w_i": (w_i, m_w_i, v_w_i), "b_i": (b_i, m_b_i, v_b_i),
        "lru_lambda": (lru_lambda, m_lru_lambda, v_lru_lambda),
    }
    two_d = lambda w: (-1, w.shape[-1]) if w.ndim > 1 else (1, w.shape[0])
    small_names = list(small_grads)
    small_outs = _adamw_small(
        [(w.reshape(two_d(w)), small_grads[nm_].reshape(two_d(w)), m.reshape(two_d(w)), v.reshape(two_d(w)))
         for nm_ in small_names for (w, m, v) in [small_params[nm_]]], "adamw_small")
    for nm_, outs in zip(small_names, small_outs):
        shp = small_params[nm_][0].shape
        res[nm_] = [small_grads[nm_].reshape(shp)] + [o.reshape(shp) for o in outs]

    order = ["w_mod", "b_mod", "g_ffn1", "w_ffn1_in", "w_ffn1_out", "g_mix", "w_in", "conv_w", "conv_b",
             "ln_g", "ln_b", "rnn_conv_w", "rnn_conv_b", "w_a", "b_a", "w_i", "b_i", "lru_lambda", "w_out",
             "g_ffn2", "w_ffn2_in", "w_ffn2_out", "w_fmod", "b_fmod", "g_final"]
    return (loss, dx0[None], *[res[n][0] for n in order], *[res[n][1] for n in order],
            *[res[n][2] for n in order], *[res[n][3] for n in order])
```

```python
import functools
import math

import jax
import jax.numpy as jnp
from jax import lax
from jax.experimental import pallas as pl
from jax.experimental.pallas import tpu as pltpu

F32 = jnp.float32
BF16 = jnp.bfloat16
MESH_IDS = pl.DeviceIdType.MESH
NDEV = 8
EPS = 1e-6
RG_C = 8.0
CONV_W = 31
RNN_CONV_W = 4
LANES = 128
ADAM_LR = 0.001
ADAM_B1 = 0.9
ADAM_B2 = 0.999
ADAM_EPS = 1e-08
ADAM_WD = 0.01
ADAM_STEP = 10
TOKEN_TILE = 512
ROW_GROUP = 16
HI = lax.Precision.HIGHEST


def _cp(sem, vmem_mb):
    return pltpu.CompilerParams(dimension_semantics=sem, vmem_limit_bytes=vmem_mb * 1024 * 1024)


def _dot(a, b):
    return jnp.dot(a, b, preferred_element_type=F32)


def _dot_nt(a, b):
    return lax.dot_general(a, b, (((1,), (1,)), ((), ())), preferred_element_type=F32)


def _dot_tn(a, b):
    return lax.dot_general(a, b, (((0,), (0,)), ((), ())), preferred_element_type=F32)


def _sigmoid(x):
    return 1.0 / (1.0 + jnp.exp(-x))


def _adaln(x, vec_ref):
    rstd = lax.rsqrt(jnp.mean(x * x, axis=-1, keepdims=True) + EPS)
    return (x * rstd) * vec_ref[0:1, :] * (1.0 + vec_ref[2:3, :]) + vec_ref[1:2, :]


def _adaln_bwd(x, dh, vec_ref, dvec_ref):
    rstd = lax.rsqrt(jnp.mean(x * x, axis=-1, keepdims=True) + EPS)
    xhat = x * rstd
    dvec_ref[0:1, :] += jnp.sum(dh * xhat, axis=0, keepdims=True)
    dvec_ref[1:2, :] += jnp.sum(dh, axis=0, keepdims=True)
    dxhat = dh * (vec_ref[0:1, :] * (1.0 + vec_ref[2:3, :]))
    return rstd * (dxhat - xhat * jnp.mean(dxhat * xhat, axis=-1, keepdims=True))


def _adaln_finish(vec_ref, dvec_ref):
    s = dvec_ref[0:1, :]
    dvec_ref[3:4, :] = vec_ref[0:1, :] * s
    dvec_ref[0:1, :] = (1.0 + vec_ref[2:3, :]) * s


def _gelu_and_grad(x):
    k0 = math.sqrt(2.0 / math.pi)
    x2 = x * x
    t = jnp.tanh(k0 * (x + 0.044715 * x * x2))
    g = 0.5 * x * (1.0 + t)
    dg = 0.5 * (1.0 + t) + 0.5 * x * (1.0 - t * t) * (k0 * (1.0 + 3.0 * 0.044715 * x2))
    return g, dg


def _log_sigmoid(x):
    z = jnp.exp(-jnp.abs(x))
    u = 1.0 + z
    d = u - 1.0
    log1p = jnp.where(d == 0.0, z, jnp.log(u) * (z / jnp.where(d == 0.0, 1.0, d)))
    return jnp.minimum(x, 0.0) - log1p


def _neg_expm1(x):
    series = -x * (1.0 + x * (0.5 + x * (1.0 / 6.0 + x * (1.0 / 24.0 + x * (1.0 / 120.0)))))
    return jnp.where(x > -0.05, series, 1.0 - jnp.exp(x))


SUBLANES = 8


def _doubling_scan(a, b, reverse):
    n = a.shape[0]
    row = lax.broadcasted_iota(jnp.int32, a.shape, 0)
    s = 1
    while s < n:
        ok = (row < n - s) if reverse else (row >= s)
        shift = n - s if reverse else s
        b = a * jnp.where(ok, pltpu.roll(b, shift, 0), 0.0) + b
        if 2 * s < n:
            a = a * jnp.where(ok, pltpu.roll(a, shift, 0), 1.0)
        s *= 2
    return b


def _tiled_scan(a, b, reverse, sa_ref, sb_ref, carry_ref, out_ref):
    n = a.shape[0]
    nt8 = n // SUBLANES
    sub = lax.broadcasted_iota(jnp.int32, a.shape, 0) % SUBLANES
    for s in (1, 2, 4):
        ok = (sub < SUBLANES - s) if reverse else (sub >= s)
        shift = n - s if reverse else s
        b = a * jnp.where(ok, pltpu.roll(b, shift, 0), 0.0) + b
        a = a * jnp.where(ok, pltpu.roll(a, shift, 0), 1.0)
    sa_ref[...] = a
    sb_ref[...] = b
    edge = 0 if reverse else SUBLANES - 1
    at = sa_ref[pl.ds(edge, nt8, stride=SUBLANES), :]
    bt = sb_ref[pl.ds(edge, nt8, stride=SUBLANES), :]
    xt = _doubling_scan(at, bt, reverse)
    rowt = lax.broadcasted_iota(jnp.int32, xt.shape, 0)
    if reverse:
        carry_ref[...] = jnp.where(rowt < nt8 - 1, pltpu.roll(xt, nt8 - 1, 0), 0.0)
    else:
        carry_ref[...] = jnp.where(rowt >= 1, pltpu.roll(xt, 1, 0), 0.0)
    for r in range(nt8):
        rows = slice(r * SUBLANES, (r + 1) * SUBLANES)
        out_ref[rows, :] = sa_ref[rows, :] * carry_ref[r:r + 1, :] + sb_ref[rows, :]


def _rglru_gates(xr, wa_ref, wi_ref, rvec_ref):
    xb = xr.astype(BF16)
    r = _sigmoid(_dot(xb, wa_ref[...]) + rvec_ref[1:2, :])
    ig = _sigmoid(_dot(xb, wi_ref[...]) + rvec_ref[2:3, :])
    ls = _log_sigmoid(rvec_ref[3:4, :])
    log_a = RG_C * r * ls
    a = jnp.exp(log_a)
    mult = jnp.sqrt(_neg_expm1(2.0 * log_a))
    return xb, r, ig, ls, a, mult


def _rnn_conv(ux, rw_ref, rvec_ref, ext_ref):
    t = ux.shape[0]
    ext_ref[0:8, :] = jnp.zeros((8, ux.shape[1]), F32)
    ext_ref[8:, :] = ux
    xr = rvec_ref[0:1, :] + rw_ref[RNN_CONV_W - 1:RNN_CONV_W, :] * ux
    for k in range(RNN_CONV_W - 1):
        d = RNN_CONV_W - 1 - k
        xr = xr + rw_ref[k:k + 1, :] * ext_ref[8 - d:8 - d + t, :]
    return xr


def _ffn_fwd(x, vec, wi, wo, tm, name, host=None):
    t, d = x.shape
    nj, fb = wi.shape[1], wi.shape[2]
    nt = t // tm

    def body(ins, outs, scr):
        x_ref, vec_ref, wi_ref, wo_ref = ins
        xo_ref, h_ref, gu_ref, f_ref = outs
        acc_ref, = scr
        j = pl.program_id(1)

        @pl.when(j == 0)
        def _():
            h_ref[...] = _adaln(x_ref[...], vec_ref).astype(BF16)
            acc_ref[...] = jnp.zeros_like(acc_ref)

        h = h_ref[...]
        gate = _dot_nt(h, wi_ref[0])
        up = _dot_nt(h, wi_ref[1])
        gu_ref[0] = gate.astype(BF16)
        gu_ref[1] = up.astype(BF16)
        act = (gate * _sigmoid(gate) * up).astype(BF16)
        acc_ref[...] += _dot(act, wo_ref[...])

        @pl.when(j == nj - 1)
        def _():
            f = acc_ref[...]
            f_ref[...] = f.astype(BF16)
            xo_ref[...] = x_ref[...] + 0.5 * vec_ref[3:4, :] * f

    tile = pl.BlockSpec((tm, d), lambda i, j: (i, 0))
    return _hosted_call(
        body, host, name, (nt, nj),
        [tile,
         pl.BlockSpec((8, d), lambda i, j: (0, 0)),
         pl.BlockSpec((2, None, fb, d), lambda i, j: (0, j, 0, 0)),
         pl.BlockSpec((fb, d), lambda i, j: (j, 0))],
        [tile, tile, pl.BlockSpec((2, None, tm, fb), lambda i, j: (0, j, i, 0)), tile],
        [jax.ShapeDtypeStruct((t, d), F32), jax.ShapeDtypeStruct((t, d), BF16),
         jax.ShapeDtypeStruct((2, nj, t, fb), BF16), jax.ShapeDtypeStruct((t, d), BF16)],
        [pltpu.VMEM((tm, d), F32)], ("arbitrary", "arbitrary"), 48, (x, vec, wi, wo))


def _ffn_fwd_loss(x, vec, wi, wo, tgt, fvec, tm, name):
    t, d = x.shape
    nj, fb = wi.shape[1], wi.shape[2]
    nt = t // tm

    def body(x_ref, vec_ref, wi_ref, wo_ref, t_ref, fvec_ref, h_ref, gu_ref, dx_ref, dvec_ref, df_ref, dgt_ref,
             acc_ref):
        i = pl.program_id(0)
        j = pl.program_id(1)

        @pl.when((i == 0) & (j == 0))
        def _():
            dvec_ref[...] = jnp.zeros_like(dvec_ref)
            dgt_ref[...] = jnp.zeros_like(dgt_ref)

        @pl.when(j == 0)
        def _():
            h_ref[...] = _adaln(x_ref[...], vec_ref).astype(BF16)
            acc_ref[...] = jnp.zeros_like(acc_ref)

        h = h_ref[...]
        gate = _dot_nt(h, wi_ref[0])
        up = _dot_nt(h, wi_ref[1])
        gu_ref[0] = gate.astype(BF16)
        gu_ref[1] = up.astype(BF16)
        act = (gate * _sigmoid(gate) * up).astype(BF16)
        acc_ref[...] += _dot(act, wo_ref[...])

        @pl.when(j == nj - 1)
        def _():
            f = acc_ref[...]
            xo = x_ref[...] + 0.5 * vec_ref[3:4, :] * f
            e = _adaln(xo, fvec_ref) - t_ref[...]
            dvec_ref[4:5, :] += (0.5 / d) * jnp.sum(e * e, axis=0, keepdims=True)
            dx = _adaln_bwd(xo, e * (1.0 / d), fvec_ref, dvec_ref)
            dx_ref[...] = dx
            df_ref[...] = (0.5 * vec_ref[3:4, :] * dx).astype(BF16)
            dgt_ref[2:3, :] += 0.5 * jnp.sum(dx * f, axis=0, keepdims=True)

        @pl.when((i == nt - 1) & (j == nj - 1))
        def _():
            _adaln_finish(fvec_ref, dvec_ref)

    tile = pl.BlockSpec((tm, d), lambda i, j: (i, 0))
    tab = pl.BlockSpec((8, d), lambda i, j: (0, 0))
    return pl.pallas_call(
        body, name=name, grid=(nt, nj),
        in_specs=[tile, tab,
                  pl.BlockSpec((2, None, fb, d), lambda i, j: (0, j, 0, 0)),
                  pl.BlockSpec((fb, d), lambda i, j: (j, 0)),
                  pl.BlockSpec((tm, d), lambda i, j: (jnp.where(j == nj - 1, i, jnp.maximum(i - 1, 0)), 0)), tab],
        out_specs=[tile, pl.BlockSpec((2, None, tm, fb), lambda i, j: (0, j, i, 0)), tile, tab, tile, tab],
        out_shape=[jax.ShapeDtypeStruct((t, d), BF16), jax.ShapeDtypeStruct((2, nj, t, fb), BF16),
                   jax.ShapeDtypeStruct((t, d), F32), jax.ShapeDtypeStruct((8, d), F32),
                   jax.ShapeDtypeStruct((t, d), BF16), jax.ShapeDtypeStruct((8, d), F32)],
        scratch_shapes=[pltpu.VMEM((tm, d), F32)],
        compiler_params=_cp(("arbitrary", "arbitrary"), 56),
    )(x, vec, wi, wo, tgt, fvec)


def _mix_in(x, vec, win, tm, name, host=None):
    t, d = x.shape
    nb, _, cb = win.shape

    def body(ins, outs, scr):
        x_ref, vec_ref, w_ref = ins
        h_ref, p_ref = outs
        h = _adaln(x_ref[...], vec_ref).astype(BF16)
        h_ref[...] = h
        for k in range(nb):
            p_ref[:, k * cb:(k + 1) * cb] = _dot(h, w_ref[k])

    return _hosted_call(
        body, host, name, (t // tm,),
        [pl.BlockSpec((tm, d), lambda i: (i, 0)),
         pl.BlockSpec((8, d), lambda i: (0, 0)),
         pl.BlockSpec((nb, d, cb), lambda i: (0, 0, 0))],
        [pl.BlockSpec((tm, d), lambda i: (i, 0)),
         pl.BlockSpec((tm, nb * cb), lambda i: (i, 0))],
        [jax.ShapeDtypeStruct((t, d), BF16), jax.ShapeDtypeStruct((t, nb * cb), F32)],
        [], ("arbitrary",), 48, (x, vec, win))


def _conv_fwd(proj, cw32, name, host=None):
    t = proj.shape[0]
    nblk = cw32.shape[1] // LANES
    ch = min(t, 128)

    def body(ins, outs, scr):
        val_ref, gate_ref, cw_ref = ins
        cv_ref, = outs
        ext_ref, = scr
        ext_ref[0:32, :] = jnp.zeros((32, LANES), F32)
        ext_ref[32:, :] = val_ref[...] * _sigmoid(gate_ref[...])
        for r in range(t // ch):
            acc = jnp.broadcast_to(cw_ref[31:32, :], (ch, LANES))
            for k in range(CONV_W):
                off = 32 + r * ch - (CONV_W - 1 - k)
                acc = acc + cw_ref[k:k + 1, :] * ext_ref[off:off + ch, :]
            cv_ref[r * ch:(r + 1) * ch, :] = acc

    return _hosted_call(
        body, host, name, (nblk,),
        [pl.BlockSpec((t, LANES), lambda c: (0, c)),
         pl.BlockSpec((t, LANES), lambda c: (0, nblk + c)),
         pl.BlockSpec((32, LANES), lambda c: (0, c))],
        [pl.BlockSpec((t, LANES), lambda c: (0, c))],
        [jax.ShapeDtypeStruct((t, nblk * LANES), F32)],
        [pltpu.VMEM((t + 32, LANES), F32)], ("arbitrary",), 48, (proj, proj, cw32))


def _rnn_fwd(proj, rw8, rvec, wab, wib, name, host=None):
    t = proj.shape[0]
    nblk = rvec.shape[1] // LANES

    def body(ins, outs, scr):
        ux_ref, uy_ref, rw_ref, rvec_ref, wa_ref, wi_ref = ins
        h_ref, yr_ref = outs
        ext_ref, sa_ref, sb_ref, carry_ref = scr
        xr = _rnn_conv(ux_ref[...], rw_ref, rvec_ref, ext_ref)
        _, _, ig, _, a, mult = _rglru_gates(xr, wa_ref, wi_ref, rvec_ref)
        _tiled_scan(a, mult * (ig * xr), False, sa_ref, sb_ref, carry_ref, h_ref)
        ge, _ = _gelu_and_grad(uy_ref[...])
        yr_ref[...] = (ge * h_ref[...]).astype(BF16)

    blk = lambda off: pl.BlockSpec((t, LANES), lambda c: (0, off + c))
    return _hosted_call(
        body, host, name, (nblk,),
        [blk(2 * nblk), blk(3 * nblk),
         pl.BlockSpec((8, LANES), lambda c: (0, c)),
         pl.BlockSpec((8, LANES), lambda c: (0, c)),
         pl.BlockSpec((None, LANES, LANES), lambda c: (c, 0, 0)),
         pl.BlockSpec((None, LANES, LANES), lambda c: (c, 0, 0))],
        [blk(0), blk(0)],
        [jax.ShapeDtypeStruct((t, nblk * LANES), F32), jax.ShapeDtypeStruct((t, nblk * LANES), BF16)],
        [pltpu.VMEM((t + 8, LANES), F32), pltpu.VMEM((t, LANES), F32), pltpu.VMEM((t, LANES), F32),
         pltpu.VMEM((t // SUBLANES, LANES), F32)], ("arbitrary",), 56, (proj, proj, rw8, rvec, wab, wib))


def _ln_silu(cv, lnv_ref):
    mu = jnp.mean(cv, axis=-1, keepdims=True)
    xc = cv - mu
    rs = lax.rsqrt(jnp.mean(xc * xc, axis=-1, keepdims=True) + EPS)
    chat = xc * rs
    z = chat * lnv_ref[0:1, :] + lnv_ref[1:2, :]
    sg = _sigmoid(z)
    return rs, chat, z, sg


def _mix_out(x, cv, yr, vec, lnv, wout, tm, name, host=None):
    t, d = x.shape
    dc = cv.shape[1]

    def body(ins, outs, scr):
        x_ref, cv_ref, yr_ref, vec_ref, lnv_ref, w_ref = ins
        xo_ref, ym_ref, yc_ref = outs
        _, _, z, sg = _ln_silu(cv_ref[...], lnv_ref)
        yc = (z * sg).astype(BF16)
        yr = yr_ref[...]
        yc_ref[:, 0:dc] = yc
        yc_ref[:, dc:] = yr
        ym = _dot(yc, w_ref[0:dc, :]) + _dot(yr, w_ref[dc:, :])
        ym_ref[...] = ym.astype(BF16)
        xo_ref[...] = x_ref[...] + vec_ref[3:4, :] * ym

    tile = pl.BlockSpec((tm, d), lambda i: (i, 0))
    return _hosted_call(
        body, host, name, (t // tm,),
        [tile,
         pl.BlockSpec((tm, dc), lambda i: (i, 0)),
         pl.BlockSpec((tm, dc), lambda i: (i, 0)),
         pl.BlockSpec((8, d), lambda i: (0, 0)),
         pl.BlockSpec((8, dc), lambda i: (0, 0)),
         pl.BlockSpec((d, d), lambda i: (0, 0))],
        [tile, tile, tile],
        [jax.ShapeDtypeStruct((t, d), F32), jax.ShapeDtypeStruct((t, d), BF16), jax.ShapeDtypeStruct((t, d), BF16)],
        [], ("arbitrary",), 48, (x, cv, yr, vec, lnv, wout))


def _emit_df(dx, f_ref, nvec_ref, df_ref, dgt_ref):
    df_ref[...] = (0.5 * nvec_ref[3:4, :] * dx).astype(BF16)
    dgt_ref[2:3, :] += 0.5 * jnp.sum(dx * f_ref[...].astype(F32), axis=0, keepdims=True)


def _ffn_bwd_w(df, gu, h, wo, tm, name, host=None):
    t, d = df.shape
    nj, fb = gu.shape[1], gu.shape[3]
    nt = t // tm
    sub = min(tm, ROW_GROUP)

    def body(ins, outs, scr):
        df_ref, gu_ref, h_ref, wo_ref = ins
        dgu_ref, dwi_ref, dwo_ref = outs
        accg_ref, accu_ref, acco_ref, dact_ref, act_ref = scr
        i = pl.program_id(1)

        @pl.when(i == 0)
        def _():
            accg_ref[...] = jnp.zeros_like(accg_ref)
            accu_ref[...] = jnp.zeros_like(accu_ref)
            acco_ref[...] = jnp.zeros_like(acco_ref)

        dact_ref[...] = _dot_nt(df_ref[...], wo_ref[...])
        for r in range(tm // sub):
            rows = slice(r * sub, (r + 1) * sub)
            g = gu_ref[0, rows, :].astype(F32)
            u = gu_ref[1, rows, :].astype(F32)
            dact = dact_ref[rows, :]
            sg = _sigmoid(g)
            sl = g * sg
            dgu_ref[0, rows, :] = (dact * u * (sg * (1.0 + g * (1.0 - sg)))).astype(BF16)
            dgu_ref[1, rows, :] = (dact * sl).astype(BF16)
            act_ref[rows, :] = (sl * u).astype(BF16)
        hb = h_ref[...]
        acco_ref[...] += _dot_tn(act_ref[...], df_ref[...])
        accg_ref[...] += _dot_tn(dgu_ref[0], hb)
        accu_ref[...] += _dot_tn(dgu_ref[1], hb)

        @pl.when(i == nt - 1)
        def _():
            dwi_ref[0] = accg_ref[...].astype(BF16)
            dwi_ref[1] = accu_ref[...].astype(BF16)
            dwo_ref[...] = acco_ref[...].astype(BF16)

    tile = pl.BlockSpec((tm, d), lambda j, i: (i, 0))
    return _hosted_call(
        body, host, name, (nj, nt),
        [tile,
         pl.BlockSpec((2, None, tm, fb), lambda j, i: (0, j, i, 0)),
         tile,
         pl.BlockSpec((fb, d), lambda j, i: (j, 0))],
        [pl.BlockSpec((2, None, tm, fb), lambda j, i: (0, j, i, 0)),
         pl.BlockSpec((2, None, fb, d), lambda j, i: (0, j, 0, 0)),
         pl.BlockSpec((None, fb, d), lambda j, i: (j, 0, 0))],
        [jax.ShapeDtypeStruct((2, nj, t, fb), BF16), jax.ShapeDtypeStruct((2, nj, fb, d), BF16),
         jax.ShapeDtypeStruct((nj, fb, d), BF16)],
        [pltpu.VMEM((fb, d), F32), pltpu.VMEM((fb, d), F32), pltpu.VMEM((fb, d), F32),
         pltpu.VMEM((tm, fb), F32), pltpu.VMEM((tm, fb), BF16)],
        ("arbitrary", "arbitrary"), 56, (df, gu, h, wo))


def _ffn_bwd_in(dxo, x, vec, dgu, wi, tm, name, host=None):
    t, d = x.shape
    nj, fb = wi.shape[1], wi.shape[2]
    nt = t // tm

    def body(ins, outs, scr):
        dxo_ref, x_ref, vec_ref, dgu_ref, wi_ref = ins
        dx_ref, dvec_ref = outs
        i = pl.program_id(0)

        @pl.when(i == 0)
        def _():
            dvec_ref[...] = jnp.zeros_like(dvec_ref)

        dh = jnp.zeros((tm, d), F32)
        for a in range(2):
            for k in range(nj):
                dh = dh + _dot(dgu_ref[a, k], wi_ref[a, k])
        dx_ref[...] = dxo_ref[...] + _adaln_bwd(x_ref[...], dh, vec_ref, dvec_ref)

        @pl.when(i == nt - 1)
        def _():
            _adaln_finish(vec_ref, dvec_ref)

    tile = pl.BlockSpec((tm, d), lambda i: (i, 0))
    return _hosted_call(
        body, host, name, (nt,),
        [tile, tile,
         pl.BlockSpec((8, d), lambda i: (0, 0)),
         pl.BlockSpec((2, nj, tm, fb), lambda i: (0, 0, i, 0)),
         pl.BlockSpec((2, nj, fb, d), lambda i: (0, 0, 0, 0))],
        [tile, pl.BlockSpec((8, d), lambda i: (0, 0))],
        [jax.ShapeDtypeStruct((t, d), F32), jax.ShapeDtypeStruct((8, d), F32)],
        [], ("arbitrary",), 60, (dxo, x, vec, dgu, wi))


def _mixout_bwd(dxo, ym, ycat, cv, hr, proj, vec, lnv, wout, tm, name, host=None):
    t, d = dxo.shape
    dc = cv.shape[1]
    nt = t // tm

    def body(ins, outs, scr):
        dxo_ref, ym_ref, yc_ref, cv_ref, hr_ref, uy_ref, vec_ref, lnv_ref, w_ref = ins
        dw_ref, dcv_ref, dhr_ref, duy_ref, dln_ref, dgt_ref = outs
        acc_ref, = scr
        i = pl.program_id(0)

        @pl.when(i == 0)
        def _():
            dln_ref[...] = jnp.zeros_like(dln_ref)
            dgt_ref[...] = jnp.zeros_like(dgt_ref)
            acc_ref[...] = jnp.zeros_like(acc_ref)

        dxo_v = dxo_ref[...]
        dym = (vec_ref[3:4, :] * dxo_v).astype(BF16)
        acc_ref[...] += _dot_tn(yc_ref[...], dym)
        dgt_ref[0:1, :] += jnp.sum(dxo_v * ym_ref[...].astype(F32), axis=0, keepdims=True)
        dyc = _dot_nt(dym, w_ref[0:dc, :])
        dyr = _dot_nt(dym, w_ref[dc:, :])
        rs, chat, z, sg = _ln_silu(cv_ref[...], lnv_ref)
        dz = dyc * (sg * (1.0 + z * (1.0 - sg)))
        dln_ref[0:1, :] += jnp.sum(dz * chat, axis=0, keepdims=True)
        dln_ref[1:2, :] += jnp.sum(dz, axis=0, keepdims=True)
        dchat = dz * lnv_ref[0:1, :]
        dcv_ref[...] = (rs * (dchat - jnp.mean(dchat, axis=-1, keepdims=True)
                              - chat * jnp.mean(dchat * chat, axis=-1, keepdims=True))).astype(BF16)
        ge, dge = _gelu_and_grad(uy_ref[...])
        dhr_ref[...] = (dyr * ge).astype(BF16)
        duy_ref[...] = (dyr * hr_ref[...] * dge).astype(BF16)

        @pl.when(i == nt - 1)
        def _():
            dw_ref[...] = acc_ref[...].astype(BF16)

    tile_d = pl.BlockSpec((tm, d), lambda i: (i, 0))
    tile_c = pl.BlockSpec((tm, dc), lambda i: (i, 0))
    full_w = pl.BlockSpec((d, d), lambda i: (0, 0))
    return _hosted_call(
        body, host, name, (nt,),
        [tile_d, tile_d, tile_d, tile_c, tile_c,
         pl.BlockSpec((tm, dc), lambda i: (i, 3)),
         pl.BlockSpec((8, d), lambda i: (0, 0)),
         pl.BlockSpec((8, dc), lambda i: (0, 0)),
         full_w],
        [full_w, tile_c, tile_c, tile_c,
         pl.BlockSpec((8, dc), lambda i: (0, 0)),
         pl.BlockSpec((8, d), lambda i: (0, 0))],
        [jax.ShapeDtypeStruct((d, d), BF16), jax.ShapeDtypeStruct((t, dc), BF16),
         jax.ShapeDtypeStruct((t, dc), BF16), jax.ShapeDtypeStruct((t, dc), BF16),
         jax.ShapeDtypeStruct((8, dc), F32), jax.ShapeDtypeStruct((8, d), F32)],
        [pltpu.VMEM((d, d), F32)], ("arbitrary",), 48, (dxo, ym, ycat, cv, hr, proj, vec, lnv, wout))


def _conv_bwd(proj, dcv, cw32, name):
    t = proj.shape[0]
    nblk = cw32.shape[1] // LANES
    ch = min(t, 128)

    def body(val_ref, gate_ref, dcv_ref, cw_ref, dval_ref, dgate_ref, dcw_ref, extu_ref, extd_ref):
        val = val_ref[...]
        sg = _sigmoid(gate_ref[...])
        extu_ref[0:32, :] = jnp.zeros((32, LANES), F32)
        extu_ref[32:, :] = val * sg
        dcv_v = dcv_ref[...].astype(F32)
        extd_ref[0:t, :] = dcv_v
        extd_ref[t:, :] = jnp.zeros((32, LANES), F32)
        for r in range(t // ch):
            acc = jnp.zeros((ch, LANES), F32)
            for k in range(CONV_W):
                off = r * ch + (CONV_W - 1 - k)
                acc = acc + cw_ref[k:k + 1, :] * extd_ref[off:off + ch, :]
            rows = slice(r * ch, (r + 1) * ch)
            sg_r = _sigmoid(gate_ref[rows, :])
            dval_ref[rows, :] = (acc * sg_r).astype(BF16)
            dgate_ref[rows, :] = (acc * val_ref[rows, :] * sg_r * (1.0 - sg_r)).astype(BF16)
        for k in range(CONV_W):
            off = 32 - (CONV_W - 1 - k)
            dcw_ref[k:k + 1, :] = jnp.sum(dcv_v * extu_ref[off:off + t, :], axis=0, keepdims=True)
        dcw_ref[31:32, :] = jnp.sum(dcv_v, axis=0, keepdims=True)

    blk = lambda off: pl.BlockSpec((t, LANES), lambda c: (0, off + c))
    return pl.pallas_call(
        body, name=name, grid=(nblk,),
        in_specs=[blk(0), blk(nblk), blk(0), pl.BlockSpec((32, LANES), lambda c: (0, c))],
        out_specs=[blk(0), blk(0), pl.BlockSpec((32, LANES), lambda c: (0, c))],
        out_shape=[jax.ShapeDtypeStruct((t, nblk * LANES), BF16), jax.ShapeDtypeStruct((t, nblk * LANES), BF16),
                   jax.ShapeDtypeStruct((32, nblk * LANES), F32)],
        scratch_shapes=[pltpu.VMEM((t + 32, LANES), F32), pltpu.VMEM((t + 32, LANES), F32)],
        compiler_params=_cp(("arbitrary",), 56),
    )(proj, proj, dcv, cw32)


def _rnn_bwd(proj, hr, dhr, rw8, rvec, wab, wib, name, host=None):
    t = proj.shape[0]
    nblk = rvec.shape[1] // LANES

    def body(ins, outs, scr):
        ux_ref, h_ref, dh_ref, rw_ref, rvec_ref, wa_ref, wi_ref = ins
        dux_ref, sm_ref, dwa_ref, dwi_ref = outs
        ext_ref, extd_ref, sa_ref, sb_ref, carry_ref = scr
        xr = _rnn_conv(ux_ref[...], rw_ref, rvec_ref, ext_ref)
        xb, r, ig, ls, a, mult = _rglru_gates(xr, wa_ref, wi_ref, rvec_ref)
        row = lax.broadcasted_iota(jnp.int32, (t, LANES), 0)
        a_next = jnp.where(row < t - 1, pltpu.roll(a, t - 1, 0), 0.0)
        _tiled_scan(a_next, dh_ref[...].astype(F32), True, sa_ref, sb_ref, carry_ref, extd_ref)
        g = extd_ref[0:t, :]
        hprev = jnp.where(row >= 1, pltpu.roll(h_ref[...], 1, 0), 0.0)
        da = g * hprev
        dmult = g * (ig * xr)
        dig = g * mult * xr
        dxr = g * mult * ig
        dlog_a = a * (da - dmult * a / mult)
        dr = dlog_a * (RG_C * ls)
        dls = RG_C * jnp.sum(dlog_a * r, axis=0, keepdims=True)
        dpr = dr * r * (1.0 - r)
        dpi = dig * ig * (1.0 - ig)
        dprb = dpr.astype(BF16)
        dpib = dpi.astype(BF16)
        dxr = dxr + _dot_nt(dprb, wa_ref[...]) + _dot_nt(dpib, wi_ref[...])
        dwa_ref[...] = _dot_tn(xb, dprb)
        dwi_ref[...] = _dot_tn(xb, dpib)
        extd_ref[0:t, :] = dxr
        extd_ref[t:, :] = jnp.zeros((8, LANES), F32)
        dux = rw_ref[RNN_CONV_W - 1:RNN_CONV_W, :] * dxr
        for k in range(RNN_CONV_W - 1):
            d = RNN_CONV_W - 1 - k
            dux = dux + rw_ref[k:k + 1, :] * extd_ref[d:d + t, :]
        dux_ref[...] = dux.astype(BF16)
        for k in range(RNN_CONV_W):
            d = RNN_CONV_W - 1 - k
            sm_ref[k:k + 1, :] = jnp.sum(dxr * ext_ref[8 - d:8 - d + t, :], axis=0, keepdims=True)
        sm_ref[4:5, :] = jnp.sum(dxr, axis=0, keepdims=True)
        sm_ref[5:6, :] = jnp.sum(dpr, axis=0, keepdims=True)
        sm_ref[6:7, :] = jnp.sum(dpi, axis=0, keepdims=True)
        sm_ref[7:8, :] = dls * _sigmoid(-rvec_ref[3:4, :])

    blk = lambda off: pl.BlockSpec((t, LANES), lambda c: (0, off + c))
    sq = pl.BlockSpec((None, LANES, LANES), lambda c: (c, 0, 0))
    return _hosted_call(
        body, host, name, (nblk,),
        [blk(2 * nblk), blk(0), blk(0),
         pl.BlockSpec((8, LANES), lambda c: (0, c)),
         pl.BlockSpec((8, LANES), lambda c: (0, c)), sq, sq],
        [blk(0), pl.BlockSpec((8, LANES), lambda c: (0, c)), sq, sq],
        [jax.ShapeDtypeStruct((t, nblk * LANES), BF16), jax.ShapeDtypeStruct((8, nblk * LANES), F32),
         jax.ShapeDtypeStruct((nblk, LANES, LANES), F32), jax.ShapeDtypeStruct((nblk, LANES, LANES), F32)],
        [pltpu.VMEM((t + 8, LANES), F32), pltpu.VMEM((t + 8, LANES), F32), pltpu.VMEM((t, LANES), F32),
         pltpu.VMEM((t, LANES), F32), pltpu.VMEM((t // SUBLANES, LANES), F32)],
        ("arbitrary",), 60, (proj, hr, dhr, rw8, rvec, wab, wib))


def _mixin_bwd(dxo, x, parts, vec, win, h, f, nvec, tm, name):
    t, d = x.shape
    nb, _, cb = win.shape
    dc = parts[0].shape[1]
    per = dc // cb
    nt = t // tm

    def body(dxo_ref, x_ref, p0, p1, p2, p3, vec_ref, w_ref, h_ref, f_ref, nvec_ref,
             dx_ref, dvec_ref, df_ref, dgt_ref, dw_ref, acc_ref):
        i = pl.program_id(0)

        @pl.when(i == 0)
        def _():
            dvec_ref[...] = jnp.zeros_like(dvec_ref)
            dgt_ref[...] = jnp.zeros_like(dgt_ref)
            acc_ref[...] = jnp.zeros_like(acc_ref)

        prefs = (p0, p1, p2, p3)
        hb = h_ref[...]
        for p in range(len(prefs)):
            acc_ref[p] += _dot_tn(hb, prefs[p][...])
        dh = jnp.zeros((tm, d), F32)
        for k in range(nb):
            dh = dh + _dot_nt(prefs[k // per][:, (k % per) * cb:(k % per + 1) * cb], w_ref[k])
        dx = dxo_ref[...] + _adaln_bwd(x_ref[...], dh, vec_ref, dvec_ref)
        dx_ref[...] = dx
        _emit_df(dx, f_ref, nvec_ref, df_ref, dgt_ref)

        @pl.when(i == nt - 1)
        def _():
            _adaln_finish(vec_ref, dvec_ref)
            for k in range(nb):
                dw_ref[k] = acc_ref[k // per, :, (k % per) * cb:(k % per + 1) * cb].astype(BF16)

    tile_d = pl.BlockSpec((tm, d), lambda i: (i, 0))
    tile_c = pl.BlockSpec((tm, dc), lambda i: (i, 0))
    tab = pl.BlockSpec((8, d), lambda i: (0, 0))
    wspec = pl.BlockSpec((nb, d, cb), lambda i: (0, 0, 0))
    return pl.pallas_call(
        body, name=name, grid=(nt,),
        in_specs=[tile_d, tile_d, tile_c, tile_c, tile_c, tile_c, tab, wspec, tile_d, tile_d, tab],
        out_specs=[tile_d, tab, tile_d, tab, wspec],
        out_shape=[jax.ShapeDtypeStruct((t, d), F32), jax.ShapeDtypeStruct((8, d), F32),
                   jax.ShapeDtypeStruct((t, d), BF16), jax.ShapeDtypeStruct((8, d), F32),
                   jax.ShapeDtypeStruct((nb, d, cb), BF16)],
        scratch_shapes=[pltpu.VMEM((len(parts), d, dc), F32)],
        compiler_params=_cp(("arbitrary",), 56),
    )(dxo, x, *parts, vec, win, h, f, nvec)


def _coords():
    return lax.axis_index("x"), lax.axis_index("y"), lax.axis_index("c")


def _flip(v, bit):
    return 1 - v if bit else v


def _gather_copy(outs, send_sems, recv_sems, a, k, block, to, src=None):
    dst = outs[a].at[block]
    return pltpu.make_async_remote_copy(
        src_ref=dst if src is None else src, dst_ref=dst,
        send_sem=send_sems.at[a, k], recv_sem=recv_sems.at[a, k],
        device_id=to, device_id_type=MESH_IDS)


def _gather_start(ins, outs, send_sems, recv_sems, loc_sems):
    x, y, c = _coords()
    me = 4 * x + 2 * y + c
    for a in range(len(ins)):
        pltpu.make_async_copy(ins[a], outs[a].at[me], loc_sems.at[a]).start()
    for a in range(len(ins)):
        _gather_copy(outs, send_sems, recv_sems, a, 0, me, (x, y, 1 - c), src=ins[a]).start()
        for j, (cx, cy) in enumerate([(1 - x, y), (x, 1 - y), (1 - x, 1 - y)]):
            _gather_copy(outs, send_sems, recv_sems, a, 1 + j, me, (cx, cy, c), src=ins[a]).start()


def _gather_finish(ins, outs, send_sems, recv_sems, loc_sems):
    x, y, c = _coords()
    me = 4 * x + 2 * y + c
    sib = (x, y, 1 - c)
    chips = [(1 - x, y), (x, 1 - y), (1 - x, 1 - y)]
    n = len(ins)
    for a in range(n):
        for j, (cx, cy) in enumerate(chips):
            blk = 4 * cx + 2 * cy + c
            _gather_copy(outs, send_sems, recv_sems, a, 1 + j, blk, sib).wait_recv()
            _gather_copy(outs, send_sems, recv_sems, a, 4 + j, blk, sib).start()
    for a in range(n):
        _gather_copy(outs, send_sems, recv_sems, a, 0, 4 * x + 2 * y + (1 - c), sib).wait_recv()
        for j, (cx, cy) in enumerate(chips):
            _gather_copy(outs, send_sems, recv_sems, a, 4 + j, 4 * cx + 2 * cy + (1 - c), sib).wait_recv()
    for a in range(n):
        _gather_copy(outs, send_sems, recv_sems, a, 0, me, sib, src=ins[a]).wait_send()
        for j, (cx, cy) in enumerate(chips):
            _gather_copy(outs, send_sems, recv_sems, a, 1 + j, me, (cx, cy, c), src=ins[a]).wait_send()
            _gather_copy(outs, send_sems, recv_sems, a, 4 + j, 4 * cx + 2 * cy + c, sib).wait_send()
        pltpu.make_async_copy(ins[a], outs[a].at[me], loc_sems.at[a]).wait()


def _gather_shapes(shards):
    return [jax.ShapeDtypeStruct((NDEV,) + s.shape, s.dtype) for s in shards]


def _gather_sems(n):
    return [pltpu.SemaphoreType.DMA((n, 7)), pltpu.SemaphoreType.DMA((n, 7)), pltpu.SemaphoreType.DMA((n,))]


def _sibling_copies(ins, outs, send_sems, recv_sems):
    x, y, c = _coords()
    return [pltpu.make_async_remote_copy(
        src_ref=ins[a].at[2 * q + (1 - c)], dst_ref=outs[a].at[q],
        send_sem=send_sems.at[a, q], recv_sem=recv_sems.at[a, q],
        device_id=(x, y, 1 - c), device_id_type=MESH_IDS) for a in range(len(ins)) for q in range(4)]


def _sibling_shapes(parts):
    return [jax.ShapeDtypeStruct((4,) + p.shape[1:], p.dtype) for p in parts]


def _chips_copies(ins, outs, send_sems, recv_sems):
    x, y, c = _coords()
    copies = []
    for a in range(len(ins)):
        for k, (kx, ky) in enumerate([(1, 0), (0, 1), (1, 1)]):
            tx, ty = _flip(x, kx), _flip(y, ky)
            copies.append(pltpu.make_async_remote_copy(
                src_ref=ins[a].at[2 * tx + ty], dst_ref=outs[a].at[k],
                send_sem=send_sems.at[a, k], recv_sem=recv_sems.at[a, k],
                device_id=(tx, ty, c), device_id_type=MESH_IDS))
    return copies


def _chips_shapes(sums):
    return [jax.ShapeDtypeStruct((3,) + s.shape[1:], s.dtype) for s in sums]


def _direct_copies(ins, outs, send_sems, recv_sems):
    x, y, c = _coords()
    me = 4 * x + 2 * y + c
    copies = []
    for a in range(len(ins)):
        for k in range(1, NDEV):
            kx, ky, kc = (k >> 2) & 1, (k >> 1) & 1, k & 1
            copies.append(pltpu.make_async_remote_copy(
                src_ref=ins[a], dst_ref=outs[a].at[me],
                send_sem=send_sems.at[a, k - 1], recv_sem=recv_sems.at[a, k - 1],
                device_id=(_flip(x, kx), _flip(y, ky), _flip(c, kc)), device_id_type=MESH_IDS))
    return copies


class _Exchange:
    def __init__(self, kind, arrays):
        self.kind, self.arrays, self.n = kind, list(arrays), len(arrays)

    def out_shapes(self):
        return {"gather": _gather_shapes, "direct": _gather_shapes, "sibling": _sibling_shapes,
                "chips": _chips_shapes}[self.kind](self.arrays)

    def sems(self):
        if self.kind in ("gather", "direct"):
            return _gather_sems(self.n)
        k = {"sibling": 4, "chips": 3}[self.kind]
        return [pltpu.SemaphoreType.DMA((self.n, k)), pltpu.SemaphoreType.DMA((self.n, k))]

    def _copies(self, ins, outs, sems):
        if self.kind == "direct":
            x, y, c = _coords()
            own = [pltpu.make_async_copy(ins[a], outs[a].at[4 * x + 2 * y + c], sems[2].at[a]) for a in range(self.n)]
            return own + _direct_copies(ins, outs, sems[0], sems[1])
        return {"sibling": _sibling_copies, "chips": _chips_copies}[self.kind](ins, outs, *sems)

    def start(self, ins, outs, sems):
        if self.kind == "gather":
            _gather_start(ins, outs, *sems)
        else:
            for cpy in self._copies(ins, outs, sems):
                cpy.start()

    def finish(self, ins, outs, sems):
        if self.kind == "gather":
            _gather_finish(ins, outs, *sems)
        else:
            for cpy in self._copies(ins, outs, sems):
                cpy.wait()


def _hosted_call(body, host, name, grid, in_specs, out_specs, out_shape, scratch, sem, vmem_mb, args, manual=False):
    n = host.n if host else 0
    ni, no, ns = len(in_specs), len(out_specs), len(scratch)

    def full(*refs):
        ins, h_in = refs[:ni], refs[ni:ni + n]
        outs, h_out = refs[ni + n:ni + n + no], refs[ni + n + no:ni + 2 * n + no]
        scr, sems = refs[ni + 2 * n + no:ni + 2 * n + no + ns], refs[ni + 2 * n + no + ns:]
        if manual:
            body(ins, outs, scr, lambda: host.start(h_in, h_out, sems), lambda: host.finish(h_in, h_out, sems))
            return
        if host and grid:
            first = functools.reduce(lambda a, b: a & b, [pl.program_id(k) == 0 for k in range(len(grid))])
            last = functools.reduce(lambda a, b: a & b, [pl.program_id(k) == g - 1 for k, g in enumerate(grid)])

            @pl.when(first)
            def _():
                host.start(h_in, h_out, sems)
        elif host:
            host.start(h_in, h_out, sems)

        body(ins, outs, scr)

        if host and grid:
            @pl.when(last)
            def _():
                host.finish(h_in, h_out, sems)
        elif host:
            host.finish(h_in, h_out, sems)

    anyspec = pl.BlockSpec(memory_space=pl.ANY)
    return pl.pallas_call(
        full, name=name, grid=grid,
        in_specs=list(in_specs) + [anyspec] * n, out_specs=list(out_specs) + [anyspec] * n,
        out_shape=list(out_shape) + (host.out_shapes() if host else []),
        scratch_shapes=list(scratch) + (host.sems() if host else []),
        compiler_params=_cp(sem, vmem_mb),
    )(*args, *(host.arrays if host else []))


def _exchange(host, name, after=()):
    n, na = host.n, len(after)

    def body(*refs):
        ins, outs, sems = refs[:n], refs[n + na:2 * n + na], refs[2 * n + na:]
        host.start(ins, outs, sems)
        host.finish(ins, outs, sems)

    anyspec = pl.BlockSpec(memory_space=pl.ANY)
    return pl.pallas_call(
        body, name=name, in_specs=[anyspec] * (n + na), out_specs=[anyspec] * n,
        out_shape=host.out_shapes(), scratch_shapes=host.sems(),
    )(*host.arrays, *after)


def _chips_split_start(sums, name):
    n = len(sums)
    hbm = pl.BlockSpec(memory_space=pltpu.HBM)
    sem = pl.BlockSpec(memory_space=pltpu.SEMAPHORE)

    def body(*refs):
        ins, lands = refs[:n], refs[n:2 * n]
        sems = refs[2 * n:2 * n + 6 * n]
        token = refs[-1]
        for cpy in _chips_copies(ins, lands, _SemGrid(sems[:3 * n], 3), _SemGrid(sems[3 * n:], 3)):
            cpy.start()
        token[...] = jnp.zeros_like(token)

    land_shapes = _chips_shapes(sums)
    lands = [pltpu.with_memory_space_constraint(lax.empty(s.shape, s.dtype), pltpu.HBM) for s in land_shapes]
    return pl.pallas_call(
        body, name=name,
        out_shape=(*[pltpu.SemaphoreType.DMA(())] * (6 * n),
                   *[pltpu.HBM(s.shape, s.dtype) for s in sums],
                   *[pltpu.HBM(s.shape, s.dtype) for s in land_shapes],
                   jax.ShapeDtypeStruct((8, LANES), F32)),
        in_specs=[hbm] * (2 * n),
        out_specs=(*[sem] * (6 * n), *[hbm] * (2 * n), pl.BlockSpec(memory_space=pltpu.VMEM)),
        input_output_aliases={i: 6 * n + i for i in range(2 * n)},
        compiler_params=pltpu.CompilerParams(has_side_effects=pltpu.SideEffectType.DATAFLOW_SIDE_EFFECTING),
    )(*[pltpu.with_memory_space_constraint(s, pltpu.HBM) for s in sums], *lands)


def _owner_copies(parts, lands, send_sems, recv_sems, whole=False):
    x, y, c = _coords()
    copies = []
    for a in range(len(parts)):
        for k in range(1, NDEV):
            px, py, pc = _flip(x, (k >> 2) & 1), _flip(y, (k >> 1) & 1), _flip(c, k & 1)
            copies.append(pltpu.make_async_remote_copy(
                src_ref=parts[a] if whole else parts[a].at[4 * px + 2 * py + pc], dst_ref=lands[a].at[k - 1],
                send_sem=send_sems.at[a, k - 1], recv_sem=recv_sems.at[a, k - 1],
                device_id=(px, py, pc), device_id_type=MESH_IDS))
    return copies


def _owner_split_start(parts, name, whole=False):
    n = len(parts)
    k = NDEV - 1
    hbm = pl.BlockSpec(memory_space=pltpu.HBM)
    sem = pl.BlockSpec(memory_space=pltpu.SEMAPHORE)

    def body(*refs):
        ins, lands = refs[:n], refs[n:2 * n]
        sems = refs[2 * n:2 * n + 2 * k * n]
        for cpy in _owner_copies(ins, lands, _SemGrid(sems[:k * n], k), _SemGrid(sems[k * n:], k), whole):
            cpy.start()
        refs[-1][...] = jnp.zeros_like(refs[-1])

    land_shapes = [jax.ShapeDtypeStruct((k,) + (p.shape if whole else p.shape[1:]), p.dtype) for p in parts]
    lands = [pltpu.with_memory_space_constraint(lax.empty(s.shape, s.dtype), pltpu.HBM) for s in land_shapes]
    return pl.pallas_call(
        body, name=name,
        out_shape=(*[pltpu.SemaphoreType.DMA(())] * (2 * k * n),
                   *[pltpu.HBM(p.shape, p.dtype) for p in parts],
                   *[pltpu.HBM(s.shape, s.dtype) for s in land_shapes],
                   jax.ShapeDtypeStruct((8, LANES), F32)),
        in_specs=[hbm] * (2 * n),
        out_specs=(*[sem] * (2 * k * n), *[hbm] * (2 * n), pl.BlockSpec(memory_space=pltpu.VMEM)),
        input_output_aliases={i: 2 * k * n + i for i in range(2 * n)},
        compiler_params=pltpu.CompilerParams(has_side_effects=pltpu.SideEffectType.DATAFLOW_SIDE_EFFECTING),
    )(*[pltpu.with_memory_space_constraint(p, pltpu.HBM) for p in parts], *lands)


def _owner_split_wait(started, n, after, name, whole=False):
    k = NDEV - 1
    sems, thru = started[:2 * k * n], started[2 * k * n:2 * k * n + 2 * n]
    after = list(after) if isinstance(after, (list, tuple)) else [after]
    hbm = pl.BlockSpec(memory_space=pltpu.HBM)
    sem = pl.BlockSpec(memory_space=pltpu.SEMAPHORE)

    def body(*refs):
        ins, lands = refs[:n], refs[n:2 * n]
        s = refs[2 * n:2 * n + 2 * k * n]
        for cpy in _owner_copies(ins, lands, _SemGrid(s[:k * n], k), _SemGrid(s[k * n:], k), whole):
            cpy.wait_send()
            cpy.wait_recv()

    outs = pl.pallas_call(
        body, name=name,
        out_shape=tuple(pltpu.HBM(a.shape, a.dtype) for a in thru),
        in_specs=[hbm] * (2 * n) + [sem] * (2 * k * n) + [pl.BlockSpec(memory_space=pl.ANY)] * len(after),
        out_specs=tuple([hbm] * (2 * n)),
        input_output_aliases={i: i for i in range(2 * n)},
        compiler_params=pltpu.CompilerParams(has_side_effects=pltpu.SideEffectType.DATAFLOW_SIDE_EFFECTING),
    )(*thru, *sems, *after)
    return list(outs[:n]), list(outs[n:])


def _owner_final(part, recv, me, w, m, v, tr, name):
    _, r, c = part.shape

    def body(me_ref, p_ref, r_ref, w_ref, m_ref, v_ref, g_ref, d_ref, mo_ref, vo_ref):
        g = p_ref[...].astype(F32)
        for k in range(NDEV - 1):
            g = g + r_ref[k].astype(F32)
        g_ref[...] = g
        d_ref[...], mo_ref[...], vo_ref[...] = _adamw_math(w_ref[...], g, m_ref[...], v_ref[...])

    tile = pl.BlockSpec((tr, c), lambda i, s: (i, 0))
    sds = jax.ShapeDtypeStruct((r, c), F32)
    return pl.pallas_call(
        body, name=name,
        grid_spec=pltpu.PrefetchScalarGridSpec(
            num_scalar_prefetch=1, grid=(r // tr,),
            in_specs=[pl.BlockSpec((None, tr, c), lambda i, s: (s[0], i, 0)),
                      pl.BlockSpec((NDEV - 1, tr, c), lambda i, s: (0, i, 0)), tile, tile, tile],
            out_specs=[tile] * 4),
        out_shape=[sds] * 4,
        compiler_params=_cp(("arbitrary",), 48),
    )(me, part, recv, w, m, v)


class _SemGrid:
    def __init__(self, sems, k):
        self.sems, self.k = sems, k

    @property
    def at(self):
        return self

    def __getitem__(self, idx):
        return self.sems[idx[0] * self.k + idx[1]]


def _chips_split_wait(started, n, after, name):
    sems = started[:6 * n]
    thru = started[6 * n:8 * n]
    hbm = pl.BlockSpec(memory_space=pltpu.HBM)
    sem = pl.BlockSpec(memory_space=pltpu.SEMAPHORE)

    def body(*refs):
        ins, lands = refs[:n], refs[n:2 * n]
        s = refs[2 * n:2 * n + 6 * n]
        for cpy in _chips_copies(ins, lands, _SemGrid(s[:3 * n], 3), _SemGrid(s[3 * n:], 3)):
            cpy.wait_send()
            cpy.wait_recv()

    outs = pl.pallas_call(
        body, name=name,
        out_shape=tuple(pltpu.HBM(a.shape, a.dtype) for a in thru),
        in_specs=[hbm] * (2 * n) + [sem] * (6 * n) + [pl.BlockSpec(memory_space=pl.ANY)],
        out_specs=tuple([hbm] * (2 * n)),
        input_output_aliases={i: i for i in range(2 * n)},
        compiler_params=pltpu.CompilerParams(has_side_effects=pltpu.SideEffectType.DATAFLOW_SIDE_EFFECTING),
    )(*thru, *sems, after)
    return list(outs[n:])


def _chip_sum(part, recv, sel, tr, name):
    _, _, r, c = part.shape

    def body(sel_ref, p_ref, r_ref, cs_ref, own_ref):
        q = pl.program_id(1)
        s = p_ref[...].astype(F32) + r_ref[...].astype(F32)
        cs_ref[...] = s.astype(BF16)

        @pl.when(q == sel_ref[1])
        def _():
            own_ref[...] = s

    return pl.pallas_call(
        body, name=name,
        grid_spec=pltpu.PrefetchScalarGridSpec(
            num_scalar_prefetch=1, grid=(r // tr, 4),
            in_specs=[pl.BlockSpec((None, None, tr, c), lambda i, q, s: (q, s[0], i, 0)),
                      pl.BlockSpec((None, tr, c), lambda i, q, s: (q, i, 0))],
            out_specs=[pl.BlockSpec((None, tr, c), lambda i, q, s: (q, i, 0)),
                       pl.BlockSpec((tr, c), lambda i, q, s: (i, 0))]),
        out_shape=[jax.ShapeDtypeStruct((4, r, c), BF16), jax.ShapeDtypeStruct((r, c), F32)],
        compiler_params=_cp(("arbitrary", "arbitrary"), 48),
    )(sel, part, recv)


def _gather_direct(src_ref, buf_ref, send_sems, recv_sems):
    x, y, c = _coords()
    me = 4 * x + 2 * y + c
    buf_ref[me] = src_ref[...]
    copies = []
    for k in range(1, NDEV):
        kx, ky, kc = (k >> 2) & 1, (k >> 1) & 1, k & 1
        copies.append(pltpu.make_async_remote_copy(
            src_ref=src_ref, dst_ref=buf_ref.at[me],
            send_sem=send_sems.at[k - 1], recv_sem=recv_sems.at[k - 1],
            device_id=(_flip(x, kx), _flip(y, ky), _flip(c, kc)), device_id_type=MESH_IDS))
    for cpy in copies:
        cpy.start()

    def wait():
        for k in range(1, NDEV):
            kx, ky, kc = (k >> 2) & 1, (k >> 1) & 1, k & 1
            peer = 4 * _flip(x, kx) + 2 * _flip(y, ky) + _flip(c, kc)
            pltpu.make_async_remote_copy(
                src_ref=src_ref, dst_ref=buf_ref.at[peer],
                send_sem=send_sems.at[k - 1], recv_sem=recv_sems.at[k - 1],
                device_id=(x, y, c), device_id_type=MESH_IDS).wait_recv()
        for cpy in copies:
            cpy.wait_send()

    return me, wait


def _mod_exchange(c_row, wmod, bmod, wfmod, bfmod, name, host=None):
    d = c_row.shape[1]
    nm, nf = wmod.shape[1], wfmod.shape[1]
    nw = nm + nf

    def body(ins, outs, scr, host_start, host_finish):
        c_ref, wm_ref, bm_ref, wf_ref, bf_ref = ins
        cs_ref, mod_ref, fmod_ref = outs
        slab_ref, csbuf_ref, mslab_ref, mbuf_ref, s1, r1, s2, r2 = scr
        cv = c_ref[...]
        slab_ref[...] = jnp.broadcast_to(cv * _sigmoid(cv), (8, d))
        _, wait1 = _gather_direct(slab_ref, csbuf_ref, s1, r1)
        wait1()
        host_start()
        for b in range(NDEV):
            cs_ref[b:b + 1, :] = csbuf_ref[b, 0:1, :]
        cs = cs_ref[...]
        mslab_ref[:, 0:nm] = jnp.dot(cs, wm_ref[...], precision=HI, preferred_element_type=F32) + bm_ref[...]
        mslab_ref[:, nm:] = jnp.dot(cs, wf_ref[...], precision=HI, preferred_element_type=F32) + bf_ref[...]
        me, wait2 = _gather_direct(mslab_ref, mbuf_ref, s2, r2)
        host_finish()
        wait2()
        mine = lax.broadcasted_iota(jnp.int32, (8, nw), 0) == me
        for k in range(NDEV):
            rowk = jnp.sum(jnp.where(mine, mbuf_ref[k], 0.0), axis=0, keepdims=True)
            mod_ref[k:k + 1, :] = rowk[:, 0:nm]
            fmod_ref[k:k + 1, :] = rowk[:, nm:]

    vm = pl.BlockSpec(memory_space=pltpu.VMEM)
    return _hosted_call(
        body, host, name, (), [vm] * 5, [vm] * 3,
        [jax.ShapeDtypeStruct((NDEV, d), F32), jax.ShapeDtypeStruct((NDEV, nm), F32),
         jax.ShapeDtypeStruct((NDEV, nf), F32)],
        [pltpu.VMEM((8, d), F32), pltpu.VMEM((NDEV, 8, d), F32),
         pltpu.VMEM((8, nw), F32), pltpu.VMEM((NDEV, 8, nw), F32),
         pltpu.SemaphoreType.DMA((7,)), pltpu.SemaphoreType.DMA((7,)),
         pltpu.SemaphoreType.DMA((7,)), pltpu.SemaphoreType.DMA((7,))],
        None, 40, (c_row, wmod, bmod, wfmod, bfmod), manual=True)


def _table_sum(own, recv, me, name):
    r, d = own.shape

    def body(me_ref, own_ref, recv_ref, all_ref, sum_ref):
        tot = None
        for dev in range(NDEV):
            k = jnp.bitwise_xor(me_ref[0], dev)
            tab = jnp.where(k == 0, own_ref[...], recv_ref[jnp.maximum(k - 1, 0)])
            all_ref[dev] = tab
            tot = tab if tot is None else tot + tab
        sum_ref[...] = tot

    return pl.pallas_call(
        body, name=name,
        grid_spec=pltpu.PrefetchScalarGridSpec(
            num_scalar_prefetch=1, grid=(1,),
            in_specs=[pl.BlockSpec((r, d), lambda i, s: (0, 0)),
                      pl.BlockSpec((NDEV - 1, r, d), lambda i, s: (0, 0, 0))],
            out_specs=[pl.BlockSpec((NDEV, r, d), lambda i, s: (0, 0, 0)),
                       pl.BlockSpec((r, d), lambda i, s: (0, 0))]),
        out_shape=[jax.ShapeDtypeStruct((NDEV, r, d), F32), jax.ShapeDtypeStruct((r, d), F32)],
        compiler_params=_cp(("arbitrary",), 48),
    )(me, own, recv)


def _adamw_math(w, g, m, v):
    m = ADAM_B1 * m + (1.0 - ADAM_B1) * g
    v = ADAM_B2 * v + (1.0 - ADAM_B2) * (g * g)
    m_hat = m / (1.0 - ADAM_B1 ** ADAM_STEP)
    v_hat = v / (1.0 - ADAM_B2 ** ADAM_STEP)
    delta = -ADAM_LR * (m_hat / (jnp.sqrt(v_hat) + ADAM_EPS) + ADAM_WD * w)
    return delta, m, v


def _adamw_small(params, name):
    n = len(params)

    def body(*refs):
        for p in range(n):
            w_ref, g_ref, m_ref, v_ref = refs[4 * p:4 * p + 4]
            d_ref, mo_ref, vo_ref = refs[4 * n + 3 * p:4 * n + 3 * p + 3]
            d_ref[...], mo_ref[...], vo_ref[...] = _adamw_math(w_ref[...], g_ref[...], m_ref[...], v_ref[...])

    vm = pl.BlockSpec(memory_space=pltpu.VMEM)
    flat = [a for p in params for a in p]
    outs = pl.pallas_call(
        body, name=name, in_specs=[vm] * (4 * n), out_specs=[vm] * (3 * n),
        out_shape=[jax.ShapeDtypeStruct(p[0].shape, F32) for p in params for _ in range(3)])(*flat)
    return [outs[3 * p:3 * p + 3] for p in range(n)]


def _rs_final(own, recv, w, m, v, tr, name):
    r, c = own.shape

    def body(o_ref, r_ref, w_ref, m_ref, v_ref, g_ref, d_ref, mo_ref, vo_ref):
        g = o_ref[...] + r_ref[0].astype(F32) + r_ref[1].astype(F32) + r_ref[2].astype(F32)
        g_ref[...] = g
        d_ref[...], mo_ref[...], vo_ref[...] = _adamw_math(w_ref[...], g, m_ref[...], v_ref[...])

    tile = pl.BlockSpec((tr, c), lambda i: (i, 0))
    sds = jax.ShapeDtypeStruct((r, c), F32)
    return pl.pallas_call(
        body, name=name, grid=(r // tr,),
        in_specs=[tile, pl.BlockSpec((3, tr, c), lambda i: (0, i, 0)), tile, tile, tile],
        out_specs=[tile] * 4, out_shape=[sds] * 4,
        compiler_params=_cp(("arbitrary",), 48),
    )(own, recv, w, m, v)


def _mod_weight_update(cs, dm, w, m, v, tr, name):
    r, c = w.shape

    def body(cs_ref, dm_ref, w_ref, m_ref, v_ref, g_ref, d_ref, mo_ref, vo_ref):
        g = lax.dot_general(cs_ref[...], dm_ref[...], (((0,), (0,)), ((), ())),
                            precision=HI, preferred_element_type=F32)
        g_ref[...] = g
        d_ref[...], mo_ref[...], vo_ref[...] = _adamw_math(w_ref[...], g, m_ref[...], v_ref[...])

    tile = pl.BlockSpec((tr, c), lambda i: (i, 0))
    sds = jax.ShapeDtypeStruct((r, c), F32)
    return pl.pallas_call(
        body, name=name, grid=(r // tr,),
        in_specs=[pl.BlockSpec((NDEV, tr), lambda i: (0, i)), pl.BlockSpec((NDEV, c), lambda i: (0, 0)),
                  tile, tile, tile],
        out_specs=[tile] * 4, out_shape=[sds] * 4,
        compiler_params=_cp(("arbitrary",), 48),
    )(cs, dm, w, m, v)


def _rows(*vs):
    d = vs[0].shape[-1]
    rows = [v.reshape(1, d) for v in vs]
    return jnp.concatenate(rows + [jnp.zeros((8 - len(rows), d), F32)], axis=0)


def _block_diag_pairs(w):
    hd = w.shape[-1]
    z = jnp.zeros((w.shape[0] // 2, hd, hd), w.dtype)
    top = jnp.concatenate([w[0::2], z], axis=2)
    bot = jnp.concatenate([z, w[1::2]], axis=2)
    return jnp.concatenate([top, bot], axis=1).astype(BF16)


def _diag_pairs(g):
    hd = g.shape[-1] // 2
    both = jnp.stack([g[:, :hd, :hd], g[:, hd:, hd:]], axis=1)
    return both.reshape(2 * g.shape[0], hd, hd)


def kernel(x, c, w_mod, b_mod, g_ffn1, w_ffn1_in, w_ffn1_out, g_mix, w_in, conv_w, conv_b, ln_g, ln_b, rnn_conv_w, rnn_conv_b, w_a, b_a, w_i, b_i, lru_lambda, w_out, g_ffn2, w_ffn2_in, w_ffn2_out, w_fmod, b_fmod, g_final, loss_target, m_w_mod, m_b_mod, m_g_ffn1, m_w_ffn1_in, m_w_ffn1_out, m_g_mix, m_w_in, m_conv_w, m_conv_b, m_ln_g, m_ln_b, m_rnn_conv_w, m_rnn_conv_b, m_w_a, m_b_a, m_w_i, m_b_i, m_lru_lambda, m_w_out, m_g_ffn2, m_w_ffn2_in, m_w_ffn2_out, m_w_fmod, m_b_fmod, m_g_final, v_w_mod, v_b_mod, v_g_ffn1, v_w_ffn1_in, v_w_ffn1_out, v_g_mix, v_w_in, v_conv_w, v_conv_b, v_ln_g, v_ln_b, v_rnn_conv_w, v_rnn_conv_b, v_w_a, v_b_a, v_w_i, v_b_i, v_lru_lambda, v_w_out, v_g_ffn2, v_w_ffn2_in, v_w_ffn2_out, v_w_fmod, v_b_fmod, v_g_final):
    t, d = x.shape[1], x.shape[2]
    fb = w_ffn1_in.shape[2]
    nm = w_mod.shape[2]
    nf = w_fmod.shape[1]
    dc = conv_b.shape[1]
    cl = conv_w.shape[2]
    tm = min(TOKEN_TILE, t)
    me = 4 * lax.axis_index("x") + 2 * lax.axis_index("y") + lax.axis_index("c")

    tr = jnp.transpose
    bmod_l = lax.dynamic_slice(b_mod, (0, me * nm), (1, nm))
    bfmod_l = lax.dynamic_slice(b_fmod.reshape(1, -1), (0, me * nf), (1, nf))
    cwl = jnp.concatenate([conv_w[0], jnp.zeros((1, cl), F32), rnn_conv_w[0], jnp.zeros((4, cl), F32)], axis=0)
    cs, mod_rows, fmod_rows, wi1, wo1, cwg = _mod_exchange(
        c, w_mod[0], bmod_l, w_fmod, bfmod_l, "mod_and_gather_ffn1",
        host=_Exchange("gather", [tr(w_ffn1_in[0]).astype(BF16), w_ffn1_out[0].astype(BF16), cwl]))
    wi1 = wi1.reshape(2, 4, fb, d)
    wo1 = wo1.reshape(4 * fb, d)
    mod = mod_rows.reshape(9, d)
    fmod = fmod_rows.reshape(2, d)
    vec1 = _rows(g_ffn1, mod[0], mod[1], mod[2])
    vecm = _rows(g_mix, mod[3], mod[4], mod[5])
    vec3 = _rows(g_ffn2, mod[6], mod[7], mod[8])
    vecf = _rows(g_final, fmod[0], fmod[1])

    xin = x[0]
    later = [w_in[0].astype(BF16), w_out[0].astype(BF16), tr(w_ffn2_in[0]).astype(BF16), w_ffn2_out[0].astype(BF16)]
    x1, h1, gu1, f1, win, wout, wi2, wo2 = _ffn_fwd(xin, vec1, wi1, wo1, tm, "ffn1_fwd",
                                                    host=_Exchange("gather", later))
    wi2 = wi2.reshape(2, 4, fb, d)
    wo2 = wo2.reshape(4 * fb, d)
    wout = wout.reshape(d, d)
    h2, proj = _mix_in(x1, vecm, win, tm, "mix_in")
    lnv = _rows(ln_g, ln_b)
    rvec = _rows(rnn_conv_b, b_a, b_i, lru_lambda)
    wab = _block_diag_pairs(w_a[0])
    wib = _block_diag_pairs(w_i[0])
    cwf = jnp.transpose(cwg, (1, 0, 2)).reshape(40, NDEV * cl)
    cw32 = jnp.concatenate([cwf[0:CONV_W], conv_b], axis=0)
    rw8 = cwf[32:40]

    (cv,) = _conv_fwd(proj, cw32, "conv_fwd")
    hr, yr = _rnn_fwd(proj, rw8, rvec, wab, wib, "rnn_fwd")
    x2, ym, ycat = _mix_out(x1, cv, yr, vecm, lnv, wout, tm, "mix_out")
    h3, gu3, dx3, dvf, df3, dva3 = _ffn_fwd_loss(x2, vec3, wi2, wo2, loss_target[0], vecf, tm, "ffn2_fwd_loss")
    sel = jnp.stack([lax.axis_index("c"), 2 * lax.axis_index("x") + lax.axis_index("y")]).astype(jnp.int32)
    row_tile = {"w_ffn1_in": fb // 4, "w_ffn1_out": fb // 4, "w_in": 512, "w_out": 128,
                "w_ffn2_in": fb // 4, "w_ffn2_out": fb // 4}

    def chip_sums(names, partials, from_sib):
        out = [_chip_sum(p.reshape((4, 2) + p.shape[1:]), r, sel, p.shape[1], "chip_sum_" + nm_)
               for nm_, p, r in zip(names, partials, from_sib)]
        return [o[0] for o in out], [o[1] for o in out]

    dgu3, p_wi2, p_wo2 = _ffn_bwd_w(df3, gu3, h3, wo2, tm, "ffn2_bwd_w")
    p_wi2 = p_wi2.reshape(NDEV, fb, d)
    p_wo2 = p_wo2.reshape(NDEV, fb // 2, d)
    names2 = ["w_ffn2_in", "w_ffn2_out"]
    started2 = _owner_split_start([p_wi2, p_wo2], "rs_owner_ffn2_start")
    dx2, dv3 = _ffn_bwd_in(dx3, x2, vec3 + started2[-1][0:1, 0:1], dgu3, wi2, tm, "ffn2_bwd_in")
    p_wout, dcv, dhr, duy, dln, dgt2 = _mixout_bwd(
        dx2, ym, ycat, cv, hr, proj, vecm, lnv, wout, tm, "mixout_bwd")
    p_wout = p_wout.reshape(NDEV, d // NDEV, d)
    dval, dgate, dcw = _conv_bwd(proj, dcv, cw32, "conv_bwd")
    dux, rsm, dwab, dwib = _rnn_bwd(proj, hr, dhr, rw8, rvec, wab, wib, "rnn_bwd")
    parts = [dval, dgate, dux, duy]
    dx1, dvm, df1, dva1, p_win = _mixin_bwd(dx2, x1, parts, vecm, win, h2, f1, vec1, tm, "mixin_bwd")
    namesm = ["w_in", "w_out"]
    lane_pad = lambda v: jnp.concatenate([v, jnp.zeros_like(v)], axis=1)
    startedm = _owner_split_start([p_win, p_wout], "rs_owner_mix_start")
    early = jnp.concatenate([dva3, dv3, dvf, dvm, dgt2, dva1 + startedm[-1][0:1, 0:1], dcw.reshape(16, d),
                             lane_pad(dln), lane_pad(rsm),
                             _diag_pairs(dwab).reshape(32, d), _diag_pairs(dwib).reshape(32, d)], axis=0)
    started_e = _owner_split_start([early], "tables_start", whole=True)
    dgu1, p_wi1, p_wo1 = _ffn_bwd_w(df1, gu1, h1, wo1, tm, "ffn1_bwd_w")
    p_wi1 = p_wi1.reshape(NDEV, fb, d)
    p_wo1 = p_wo1.reshape(NDEV, fb // 2, d)
    names1 = ["w_ffn1_in", "w_ffn1_out"]
    sums1, owns1 = chip_sums(names1, [p_wi1, p_wo1],
                             _exchange(_Exchange("sibling", [p_wi1, p_wo1]), "rs_sibling_ffn1"))
    started = _chips_split_start(sums1, "rs_chips_ffn1_start")
    dx0, dv1 = _ffn_bwd_in(dx1, xin, vec1 + started[-1][0:1, 0:1], dgu1, wi1, tm, "ffn1_bwd_in")
    direct = {}
    for names, st, label in ((names2, started2, "ffn2"), (namesm, startedm, "mix")):
        kept, got = _owner_split_wait(st, len(names), dx0, "rs_owner_%s_wait" % label)
        direct.update(zip(names, zip(kept, got)))
    from_chips = {}
    owns = dict(zip(names1, owns1))
    me_arr = me.reshape(1).astype(jnp.int32)
    (own_e,), (got_e,) = _owner_split_wait(started_e, 1, dx0, "tables_wait", whole=True)
    all_early, te = _table_sum(own_e, got_e, me_arr, "table_sum")
    started_l = _owner_split_start([dv1], "late_table_start", whole=True)

    big = {"w_ffn1_in": (tr(w_ffn1_in[0]), tr(m_w_ffn1_in[0]), tr(v_w_ffn1_in[0])),
           "w_ffn1_out": (w_ffn1_out[0], m_w_ffn1_out[0], v_w_ffn1_out[0]),
           "w_in": (w_in[0], m_w_in[0], v_w_in[0]), "w_out": (w_out[0], m_w_out[0], v_w_out[0]),
           "w_ffn2_in": (tr(w_ffn2_in[0]), tr(m_w_ffn2_in[0]), tr(v_w_ffn2_in[0])),
           "w_ffn2_out": (w_ffn2_out[0], m_w_ffn2_out[0], v_w_ffn2_out[0])}
    res = {}

    def final_sum(nm_):
        if nm_ in direct:
            out4 = _owner_final(*direct[nm_], me_arr, *big[nm_], row_tile[nm_], "rs_final_" + nm_)
        else:
            out4 = _rs_final(owns[nm_], from_chips[nm_], *big[nm_], row_tile[nm_], "rs_final_" + nm_)
        res[nm_] = [(tr(o) if nm_ in ("w_ffn1_in", "w_ffn2_in") else o)[None] for o in out4]
        return out4[0]

    done = [final_sum(nm_) for nm_ in namesm + names2]
    dfm_all = jnp.concatenate([all_early[:, 17], all_early[:, 19]], axis=1)
    dfm_l = lax.dynamic_slice(dfm_all, (0, me * nf), (NDEV, nf))
    res["w_fmod"] = list(_mod_weight_update(cs, dfm_l, w_fmod, m_w_fmod, v_w_fmod, 256, "w_fmod_update"))
    (own_l,), (got_l,) = _owner_split_wait(started_l, 1, done + [res["w_fmod"][0]], "late_table_wait", whole=True)
    from_chips["w_ffn1_in"], from_chips["w_ffn1_out"] = _chips_split_wait(
        started, len(sums1), got_l, "rs_chips_ffn1_wait")
    for nm_ in names1:
        final_sum(nm_)
    all_late, tl = _table_sum(own_l, got_l, me_arr, "late_table_sum")
    loss = jnp.sum(te[20])

    mod_rows_of = lambda e, l: [l[1], l[3], e[42], e[25], e[27], e[32], e[9], e[11], e[2]]
    dm_all = jnp.concatenate(mod_rows_of(jnp.swapaxes(all_early, 0, 1), jnp.swapaxes(all_late, 0, 1)), axis=1)
    dm_l = lax.dynamic_slice(dm_all, (0, me * nm), (NDEV, nm))
    res["w_mod"] = [o[None] for o in
                    _mod_weight_update(cs, dm_l, w_mod[0], m_w_mod[0], v_w_mod[0], 256, "w_mod_update")]

    dcw_f = te[48:64].reshape(32, dc)
    rsm_f = te[72:80, 0:dc]
    small_grads = {
        "b_mod": jnp.concatenate(mod_rows_of(te, tl)).reshape(1, 9 * d),
        "b_fmod": jnp.concatenate([te[17], te[19]]),
        "g_ffn1": tl[0:1], "g_mix": te[24:25], "g_ffn2": te[8:9], "g_final": te[16],
        "conv_w": lax.dynamic_slice(dcw_f, (0, me * cl), (CONV_W, cl))[None],
        "conv_b": dcw_f[31:32],
        "ln_g": te[64:65, 0:dc], "ln_b": te[65:66, 0:dc],
        "rnn_conv_w": lax.dynamic_slice(rsm_f, (0, me * cl), (RNN_CONV_W, cl))[None],
        "rnn_conv_b": rsm_f[4:5], "b_a": rsm_f[5:6], "b_i": rsm_f[6:7], "lru_lambda": rsm_f[7:8],
        "w_a": te[80:112].reshape(w_a.shape), "w_i": te[112:144].reshape(w_i.shape),
    }
    small_params = {
        "b_mod": (b_mod, m_b_mod, v_b_mod), "b_fmod": (b_fmod, m_b_fmod, v_b_fmod),
        "g_ffn1": (g_ffn1, m_g_ffn1, v_g_ffn1), "g_mix": (g_mix, m_g_mix, v_g_mix),
        "g_ffn2": (g_ffn2, m_g_ffn2, v_g_ffn2), "g_final": (g_final, m_g_final, v_g_final),
        "conv_w": (conv_w, m_conv_w, v_conv_w), "conv_b": (conv_b, m_conv_b, v_conv_b),
        "ln_g": (ln_g, m_ln_g, v_ln_g), "ln_b": (ln_b, m_ln_b, v_ln_b),
        "rnn_conv_w": (rnn_conv_w, m_rnn_conv_w, v_rnn_conv_w),
        "rnn_conv_b": (rnn_conv_b, m_rnn_conv_b, v_rnn_conv_b),
        "w_a": (w_a, m_w_a, v_w_a), "b_a": (b_a, m_b_a, v_b_a),
        "w_i": (w_i, m_w_i, v_w_i), "b_i": (b_i, m_b_i, v_b_i),
        "lru_lambda": (lru_lambda, m_lru_lambda, v_lru_lambda),
    }
    two_d = lambda w: (-1, w.shape[-1]) if w.ndim > 1 else (1, w.shape[0])
    small_names = list(small_grads)
    small_outs = _adamw_small(
        [(w.reshape(two_d(w)), small_grads[nm_].reshape(two_d(w)), m.reshape(two_d(w)), v.reshape(two_d(w)))
         for nm_ in small_names for (w, m, v) in [small_params[nm_]]], "adamw_small")
    for nm_, outs in zip(small_names, small_outs):
        shp = small_params[nm_][0].shape
        res[nm_] = [small_grads[nm_].reshape(shp)] + [o.reshape(shp) for o in outs]

    order = ["w_mod", "b_mod", "g_ffn1", "w_ffn1_in", "w_ffn1_out", "g_mix", "w_in", "conv_w", "conv_b",
             "ln_g", "ln_b", "rnn_conv_w", "rnn_conv_b", "w_a", "b_a", "w_i", "b_i", "lru_lambda", "w_out",
             "g_ffn2", "w_ffn2_in", "w_ffn2_out", "w_fmod", "b_fmod", "g_final"]
    return (loss, dx0[None], *[res[n][0] for n in order], *[res[n][1] for n in order],
            *[res[n][2] for n in order], *[res[n][3] for n in order])
```

```python
import functools
import math

import jax
import jax.numpy as jnp
from jax import lax
from jax.experimental import pallas as pl
from jax.experimental.pallas import tpu as pltpu

F32 = jnp.float32
BF16 = jnp.bfloat16
MESH_IDS = pl.DeviceIdType.MESH
NDEV = 8
EPS = 1e-6
RG_C = 8.0
CONV_W = 31
RNN_CONV_W = 4
LANES = 128
ADAM_LR = 0.001
ADAM_B1 = 0.9
ADAM_B2 = 0.999
ADAM_EPS = 1e-08
ADAM_WD = 0.01
ADAM_STEP = 10
TOKEN_TILE = 512
ROW_GROUP = 16
HI = lax.Precision.HIGHEST


def _cp(sem, vmem_mb):
    return pltpu.CompilerParams(dimension_semantics=sem, vmem_limit_bytes=vmem_mb * 1024 * 1024)


def _dot(a, b):
    return jnp.dot(a, b, preferred_element_type=F32)


def _dot_nt(a, b):
    return lax.dot_general(a, b, (((1,), (1,)), ((), ())), preferred_element_type=F32)


def _dot_tn(a, b):
    return lax.dot_general(a, b, (((0,), (0,)), ((), ())), preferred_element_type=F32)


def _sigmoid(x):
    return 1.0 / (1.0 + jnp.exp(-x))


def _adaln(x, vec_ref):
    rstd = lax.rsqrt(jnp.mean(x * x, axis=-1, keepdims=True) + EPS)
    return (x * rstd) * vec_ref[0:1, :] * (1.0 + vec_ref[2:3, :]) + vec_ref[1:2, :]


def _adaln_bwd(x, dh, vec_ref, dvec_ref):
    rstd = lax.rsqrt(jnp.mean(x * x, axis=-1, keepdims=True) + EPS)
    xhat = x * rstd
    dvec_ref[0:1, :] += jnp.sum(dh * xhat, axis=0, keepdims=True)
    dvec_ref[1:2, :] += jnp.sum(dh, axis=0, keepdims=True)
    dxhat = dh * (vec_ref[0:1, :] * (1.0 + vec_ref[2:3, :]))
    return rstd * (dxhat - xhat * jnp.mean(dxhat * xhat, axis=-1, keepdims=True))


def _adaln_finish(vec_ref, dvec_ref):
    s = dvec_ref[0:1, :]
    dvec_ref[3:4, :] = vec_ref[0:1, :] * s
    dvec_ref[0:1, :] = (1.0 + vec_ref[2:3, :]) * s


def _gelu_and_grad(x):
    k0 = math.sqrt(2.0 / math.pi)
    x2 = x * x
    t = jnp.tanh(k0 * (x + 0.044715 * x * x2))
    g = 0.5 * x * (1.0 + t)
    dg = 0.5 * (1.0 + t) + 0.5 * x * (1.0 - t * t) * (k0 * (1.0 + 3.0 * 0.044715 * x2))
    return g, dg


def _log_sigmoid(x):
    z = jnp.exp(-jnp.abs(x))
    u = 1.0 + z
    d = u - 1.0
    log1p = jnp.where(d == 0.0, z, jnp.log(u) * (z / jnp.where(d == 0.0, 1.0, d)))
    return jnp.minimum(x, 0.0) - log1p


def _neg_expm1(x):
    series = -x * (1.0 + x * (0.5 + x * (1.0 / 6.0 + x * (1.0 / 24.0 + x * (1.0 / 120.0)))))
    return jnp.where(x > -0.05, series, 1.0 - jnp.exp(x))


SUBLANES = 8


def _doubling_scan(a, b, reverse):
    n = a.shape[0]
    row = lax.broadcasted_iota(jnp.int32, a.shape, 0)
    s = 1
    while s < n:
        ok = (row < n - s) if reverse else (row >= s)
        shift = n - s if reverse else s
        b = a * jnp.where(ok, pltpu.roll(b, shift, 0), 0.0) + b
        if 2 * s < n:
            a = a * jnp.where(ok, pltpu.roll(a, shift, 0), 1.0)
        s *= 2
    return b


def _tiled_scan(a, b, reverse, sa_ref, sb_ref, carry_ref, out_ref):
    n = a.shape[0]
    nt8 = n // SUBLANES
    sub = lax.broadcasted_iota(jnp.int32, a.shape, 0) % SUBLANES
    for s in (1, 2, 4):
        ok = (sub < SUBLANES - s) if reverse else (sub >= s)
        shift = n - s if reverse else s
        b = a * jnp.where(ok, pltpu.roll(b, shift, 0), 0.0) + b
        a = a * jnp.where(ok, pltpu.roll(a, shift, 0), 1.0)
    sa_ref[...] = a
    sb_ref[...] = b
    edge = 0 if reverse else SUBLANES - 1
    at = sa_ref[pl.ds(edge, nt8, stride=SUBLANES), :]
    bt = sb_ref[pl.ds(edge, nt8, stride=SUBLANES), :]
    xt = _doubling_scan(at, bt, reverse)
    rowt = lax.broadcasted_iota(jnp.int32, xt.shape, 0)
    if reverse:
        carry_ref[...] = jnp.where(rowt < nt8 - 1, pltpu.roll(xt, nt8 - 1, 0), 0.0)
    else:
        carry_ref[...] = jnp.where(rowt >= 1, pltpu.roll(xt, 1, 0), 0.0)
    for r in range(nt8):
        rows = slice(r * SUBLANES, (r + 1) * SUBLANES)
        out_ref[rows, :] = sa_ref[rows, :] * carry_ref[r:r + 1, :] + sb_ref[rows, :]


def _rglru_gates(xr, wa_ref, wi_ref, rvec_ref):
    xb = xr.astype(BF16)
    r = _sigmoid(_dot(xb, wa_ref[...]) + rvec_ref[1:2, :])
    ig = _sigmoid(_dot(xb, wi_ref[...]) + rvec_ref[2:3, :])
    ls = _log_sigmoid(rvec_ref[3:4, :])
    log_a = RG_C * r * ls
    a = jnp.exp(log_a)
    mult = jnp.sqrt(_neg_expm1(2.0 * log_a))
    return xb, r, ig, ls, a, mult


def _rnn_conv(ux, rw_ref, rvec_ref, ext_ref):
    t = ux.shape[0]
    ext_ref[0:8, :] = jnp.zeros((8, ux.shape[1]), F32)
    ext_ref[8:, :] = ux
    xr = rvec_ref[0:1, :] + rw_ref[RNN_CONV_W - 1:RNN_CONV_W, :] * ux
    for k in range(RNN_CONV_W - 1):
        d = RNN_CONV_W - 1 - k
        xr = xr + rw_ref[k:k + 1, :] * ext_ref[8 - d:8 - d + t, :]
    return xr


def _ffn_fwd(x, vec, wi, wo, tm, name, host=None):
    t, d = x.shape
    nj, fb = wi.shape[1], wi.shape[2]
    nt = t // tm

    def body(ins, outs, scr):
        x_ref, vec_ref, wi_ref, wo_ref = ins
        xo_ref, h_ref, gu_ref, f_ref = outs
        acc_ref, = scr
        j = pl.program_id(1)

        @pl.when(j == 0)
        def _():
            h_ref[...] = _adaln(x_ref[...], vec_ref).astype(BF16)
            acc_ref[...] = jnp.zeros_like(acc_ref)

        h = h_ref[...]
        gate = _dot_nt(h, wi_ref[0])
        up = _dot_nt(h, wi_ref[1])
        gu_ref[0] = gate.astype(BF16)
        gu_ref[1] = up.astype(BF16)
        act = (gate * _sigmoid(gate) * up).astype(BF16)
        acc_ref[...] += _dot(act, wo_ref[...])

        @pl.when(j == nj - 1)
        def _():
            f = acc_ref[...]
            f_ref[...] = f.astype(BF16)
            xo_ref[...] = x_ref[...] + 0.5 * vec_ref[3:4, :] * f

    tile = pl.BlockSpec((tm, d), lambda i, j: (i, 0))
    return _hosted_call(
        body, host, name, (nt, nj),
        [tile,
         pl.BlockSpec((8, d), lambda i, j: (0, 0)),
         pl.BlockSpec((2, None, fb, d), lambda i, j: (0, j, 0, 0)),
         pl.BlockSpec((fb, d), lambda i, j: (j, 0))],
        [tile, tile, pl.BlockSpec((2, None, tm, fb), lambda i, j: (0, j, i, 0)), tile],
        [jax.ShapeDtypeStruct((t, d), F32), jax.ShapeDtypeStruct((t, d), BF16),
         jax.ShapeDtypeStruct((2, nj, t, fb), BF16), jax.ShapeDtypeStruct((t, d), BF16)],
        [pltpu.VMEM((tm, d), F32)], ("arbitrary", "arbitrary"), 48, (x, vec, wi, wo))


def _ffn_fwd_loss(x, vec, wi, wo, tgt, fvec, tm, name):
    t, d = x.shape
    nj, fb = wi.shape[1], wi.shape[2]
    nt = t // tm

    def body(x_ref, vec_ref, wi_ref, wo_ref, t_ref, fvec_ref, h_ref, gu_ref, dx_ref, dvec_ref, df_ref, dgt_ref,
             acc_ref):
        i = pl.program_id(0)
        j = pl.program_id(1)

        @pl.when((i == 0) & (j == 0))
        def _():
            dvec_ref[...] = jnp.zeros_like(dvec_ref)
            dgt_ref[...] = jnp.zeros_like(dgt_ref)

        @pl.when(j == 0)
        def _():
            h_ref[...] = _adaln(x_ref[...], vec_ref).astype(BF16)
            acc_ref[...] = jnp.zeros_like(acc_ref)

        h = h_ref[...]
        gate = _dot_nt(h, wi_ref[0])
        up = _dot_nt(h, wi_ref[1])
        gu_ref[0] = gate.astype(BF16)
        gu_ref[1] = up.astype(BF16)
        act = (gate * _sigmoid(gate) * up).astype(BF16)
        acc_ref[...] += _dot(act, wo_ref[...])

        @pl.when(j == nj - 1)
        def _():
            f = acc_ref[...]
            xo = x_ref[...] + 0.5 * vec_ref[3:4, :] * f
            e = _adaln(xo, fvec_ref) - t_ref[...]
            dvec_ref[4:5, :] += (0.5 / d) * jnp.sum(e * e, axis=0, keepdims=True)
            dx = _adaln_bwd(xo, e * (1.0 / d), fvec_ref, dvec_ref)
            dx_ref[...] = dx
            df_ref[...] = (0.5 * vec_ref[3:4, :] * dx).astype(BF16)
            dgt_ref[2:3, :] += 0.5 * jnp.sum(dx * f, axis=0, keepdims=True)

        @pl.when((i == nt - 1) & (j == nj - 1))
        def _():
            _adaln_finish(fvec_ref, dvec_ref)

    tile = pl.BlockSpec((tm, d), lambda i, j: (i, 0))
    tab = pl.BlockSpec((8, d), lambda i, j: (0, 0))
    return pl.pallas_call(
        body, name=name, grid=(nt, nj),
        in_specs=[tile, tab,
                  pl.BlockSpec((2, None, fb, d), lambda i, j: (0, j, 0, 0)),
                  pl.BlockSpec((fb, d), lambda i, j: (j, 0)),
                  pl.BlockSpec((tm, d), lambda i, j: (jnp.where(j == nj - 1, i, jnp.maximum(i - 1, 0)), 0)), tab],
        out_specs=[tile, pl.BlockSpec((2, None, tm, fb), lambda i, j: (0, j, i, 0)), tile, tab, tile, tab],
        out_shape=[jax.ShapeDtypeStruct((t, d), BF16), jax.ShapeDtypeStruct((2, nj, t, fb), BF16),
                   jax.ShapeDtypeStruct((t, d), F32), jax.ShapeDtypeStruct((8, d), F32),
                   jax.ShapeDtypeStruct((t, d), BF16), jax.ShapeDtypeStruct((8, d), F32)],
        scratch_shapes=[pltpu.VMEM((tm, d), F32)],
        compiler_params=_cp(("arbitrary", "arbitrary"), 56),
    )(x, vec, wi, wo, tgt, fvec)


def _mix_in(x, vec, win, tm, name, host=None):
    t, d = x.shape
    nb, _, cb = win.shape

    def body(ins, outs, scr):
        x_ref, vec_ref, w_ref = ins
        h_ref, p_ref = outs
        h = _adaln(x_ref[...], vec_ref).astype(BF16)
        h_ref[...] = h
        for k in range(nb):
            p_ref[:, k * cb:(k + 1) * cb] = _dot(h, w_ref[k])

    return _hosted_call(
        body, host, name, (t // tm,),
        [pl.BlockSpec((tm, d), lambda i: (i, 0)),
         pl.BlockSpec((8, d), lambda i: (0, 0)),
         pl.BlockSpec((nb, d, cb), lambda i: (0, 0, 0))],
        [pl.BlockSpec((tm, d), lambda i: (i, 0)),
         pl.BlockSpec((tm, nb * cb), lambda i: (i, 0))],
        [jax.ShapeDtypeStruct((t, d), BF16), jax.ShapeDtypeStruct((t, nb * cb), F32)],
        [], ("arbitrary",), 48, (x, vec, win))


def _conv_fwd(proj, cw32, name, host=None):
    t = proj.shape[0]
    nblk = cw32.shape[1] // LANES
    ch = min(t, 128)

    def body(ins, outs, scr):
        val_ref, gate_ref, cw_ref = ins
        cv_ref, = outs
        ext_ref, = scr
        ext_ref[0:32, :] = jnp.zeros((32, LANES), F32)
        ext_ref[32:, :] = val_ref[...] * _sigmoid(gate_ref[...])
        for r in range(t // ch):
            acc = jnp.broadcast_to(cw_ref[31:32, :], (ch, LANES))
            for k in range(CONV_W):
                off = 32 + r * ch - (CONV_W - 1 - k)
                acc = acc + cw_ref[k:k + 1, :] * ext_ref[off:off + ch, :]
            cv_ref[r * ch:(r + 1) * ch, :] = acc

    return _hosted_call(
        body, host, name, (nblk,),
        [pl.BlockSpec((t, LANES), lambda c: (0, c)),
         pl.BlockSpec((t, LANES), lambda c: (0, nblk + c)),
         pl.BlockSpec((32, LANES), lambda c: (0, c))],
        [pl.BlockSpec((t, LANES), lambda c: (0, c))],
        [jax.ShapeDtypeStruct((t, nblk * LANES), F32)],
        [pltpu.VMEM((t + 32, LANES), F32)], ("arbitrary",), 48, (proj, proj, cw32))


def _rnn_fwd(proj, rw8, rvec, wab, wib, name, host=None):
    t = proj.shape[0]
    nblk = rvec.shape[1] // LANES

    def body(ins, outs, scr):
        ux_ref, uy_ref, rw_ref, rvec_ref, wa_ref, wi_ref = ins
        h_ref, yr_ref = outs
        ext_ref, sa_ref, sb_ref, carry_ref = scr
        xr = _rnn_conv(ux_ref[...], rw_ref, rvec_ref, ext_ref)
        _, _, ig, _, a, mult = _rglru_gates(xr, wa_ref, wi_ref, rvec_ref)
        _tiled_scan(a, mult * (ig * xr), False, sa_ref, sb_ref, carry_ref, h_ref)
        ge, _ = _gelu_and_grad(uy_ref[...])
        yr_ref[...] = (ge * h_ref[...]).astype(BF16)

    blk = lambda off: pl.BlockSpec((t, LANES), lambda c: (0, off + c))
    return _hosted_call(
        body, host, name, (nblk,),
        [blk(2 * nblk), blk(3 * nblk),
         pl.BlockSpec((8, LANES), lambda c: (0, c)),
         pl.BlockSpec((8, LANES), lambda c: (0, c)),
         pl.BlockSpec((None, LANES, LANES), lambda c: (c, 0, 0)),
         pl.BlockSpec((None, LANES, LANES), lambda c: (c, 0, 0))],
        [blk(0), blk(0)],
        [jax.ShapeDtypeStruct((t, nblk * LANES), F32), jax.ShapeDtypeStruct((t, nblk * LANES), BF16)],
        [pltpu.VMEM((t + 8, LANES), F32), pltpu.VMEM((t, LANES), F32), pltpu.VMEM((t, LANES), F32),
         pltpu.VMEM((t // SUBLANES, LANES), F32)], ("arbitrary",), 56, (proj, proj, rw8, rvec, wab, wib))


def _ln_silu(cv, lnv_ref):
    mu = jnp.mean(cv, axis=-1, keepdims=True)
    xc = cv - mu
    rs = lax.rsqrt(jnp.mean(xc * xc, axis=-1, keepdims=True) + EPS)
    chat = xc * rs
    z = chat * lnv_ref[0:1, :] + lnv_ref[1:2, :]
    sg = _sigmoid(z)
    return rs, chat, z, sg


def _mix_out(x, cv, yr, vec, lnv, wout, tm, name, host=None):
    t, d = x.shape
    dc = cv.shape[1]

    def body(ins, outs, scr):
        x_ref, cv_ref, yr_ref, vec_ref, lnv_ref, w_ref = ins
        xo_ref, ym_ref, yc_ref = outs
        _, _, z, sg = _ln_silu(cv_ref[...], lnv_ref)
        yc = (z * sg).astype(BF16)
        yr = yr_ref[...]
        yc_ref[:, 0:dc] = yc
        yc_ref[:, dc:] = yr
        ym = _dot(yc, w_ref[0:dc, :]) + _dot(yr, w_ref[dc:, :])
        ym_ref[...] = ym.astype(BF16)
        xo_ref[...] = x_ref[...] + vec_ref[3:4, :] * ym

    tile = pl.BlockSpec((tm, d), lambda i: (i, 0))
    return _hosted_call(
        body, host, name, (t // tm,),
        [tile,
         pl.BlockSpec((tm, dc), lambda i: (i, 0)),
         pl.BlockSpec((tm, dc), lambda i: (i, 0)),
         pl.BlockSpec((8, d), lambda i: (0, 0)),
         pl.BlockSpec((8, dc), lambda i: (0, 0)),
         pl.BlockSpec((d, d), lambda i: (0, 0))],
        [tile, tile, tile],
        [jax.ShapeDtypeStruct((t, d), F32), jax.ShapeDtypeStruct((t, d), BF16), jax.ShapeDtypeStruct((t, d), BF16)],
        [], ("arbitrary",), 48, (x, cv, yr, vec, lnv, wout))


def _emit_df(dx, f_ref, nvec_ref, df_ref, dgt_ref):
    df_ref[...] = (0.5 * nvec_ref[3:4, :] * dx).astype(BF16)
    dgt_ref[2:3, :] += 0.5 * jnp.sum(dx * f_ref[...].astype(F32), axis=0, keepdims=True)


def _ffn_bwd_w(df, gu, h, wo, tm, name, host=None):
    t, d = df.shape
    nj, fb = gu.shape[1], gu.shape[3]
    tm = min(2 * tm, t)
    nt = t // tm
    sub = min(tm, ROW_GROUP)

    def body(ins, outs, scr):
        df_ref, gu_ref, h_ref, wo_ref = ins
        dgu_ref, dwi_ref, dwo_ref = outs
        accg_ref, accu_ref, acco_ref, dact_ref, act_ref = scr
        i = pl.program_id(1)

        @pl.when(i == 0)
        def _():
            accg_ref[...] = jnp.zeros_like(accg_ref)
            accu_ref[...] = jnp.zeros_like(accu_ref)
            acco_ref[...] = jnp.zeros_like(acco_ref)

        dact_ref[...] = _dot_nt(df_ref[...], wo_ref[...])
        for r in range(tm // sub):
            rows = slice(r * sub, (r + 1) * sub)
            g = gu_ref[0, rows, :].astype(F32)
            u = gu_ref[1, rows, :].astype(F32)
            dact = dact_ref[rows, :]
            sg = _sigmoid(g)
            sl = g * sg
            dgu_ref[0, rows, :] = (dact * u * (sg * (1.0 + g * (1.0 - sg)))).astype(BF16)
            dgu_ref[1, rows, :] = (dact * sl).astype(BF16)
            act_ref[rows, :] = (sl * u).astype(BF16)
        hb = h_ref[...]
        acco_ref[...] += _dot_tn(act_ref[...], df_ref[...])
        accg_ref[...] += _dot_tn(dgu_ref[0], hb)
        accu_ref[...] += _dot_tn(dgu_ref[1], hb)

        @pl.when(i == nt - 1)
        def _():
            dwi_ref[0] = accg_ref[...].astype(BF16)
            dwi_ref[1] = accu_ref[...].astype(BF16)
            dwo_ref[...] = acco_ref[...].astype(BF16)

    tile = pl.BlockSpec((tm, d), lambda j, i: (i, 0))
    return _hosted_call(
        body, host, name, (nj, nt),
        [tile,
         pl.BlockSpec((2, None, tm, fb), lambda j, i: (0, j, i, 0)),
         tile,
         pl.BlockSpec((fb, d), lambda j, i: (j, 0), pipeline_mode=pl.Buffered(1))],
        [pl.BlockSpec((2, None, tm, fb), lambda j, i: (0, j, i, 0)),
         pl.BlockSpec((2, None, fb, d), lambda j, i: (0, j, 0, 0), pipeline_mode=pl.Buffered(1)),
         pl.BlockSpec((None, fb, d), lambda j, i: (j, 0, 0), pipeline_mode=pl.Buffered(1))],
        [jax.ShapeDtypeStruct((2, nj, t, fb), BF16), jax.ShapeDtypeStruct((2, nj, fb, d), BF16),
         jax.ShapeDtypeStruct((nj, fb, d), BF16)],
        [pltpu.VMEM((fb, d), F32), pltpu.VMEM((fb, d), F32), pltpu.VMEM((fb, d), F32),
         pltpu.VMEM((tm, fb), F32), pltpu.VMEM((tm, fb), BF16)],
        ("arbitrary", "arbitrary"), 56, (df, gu, h, wo))


def _ffn_bwd_in(dxo, x, vec, dgu, wi, tm, name, host=None):
    t, d = x.shape
    nj, fb = wi.shape[1], wi.shape[2]
    nt = t // tm

    def body(ins, outs, scr):
        dxo_ref, x_ref, vec_ref, dgu_ref, wi_ref = ins
        dx_ref, dvec_ref = outs
        i = pl.program_id(0)

        @pl.when(i == 0)
        def _():
            dvec_ref[...] = jnp.zeros_like(dvec_ref)

        dh = jnp.zeros((tm, d), F32)
        for a in range(2):
            for k in range(nj):
                dh = dh + _dot(dgu_ref[a, k], wi_ref[a, k])
        dx_ref[...] = dxo_ref[...] + _adaln_bwd(x_ref[...], dh, vec_ref, dvec_ref)

        @pl.when(i == nt - 1)
        def _():
            _adaln_finish(vec_ref, dvec_ref)

    tile = pl.BlockSpec((tm, d), lambda i: (i, 0))
    return _hosted_call(
        body, host, name, (nt,),
        [tile, tile,
         pl.BlockSpec((8, d), lambda i: (0, 0)),
         pl.BlockSpec((2, nj, tm, fb), lambda i: (0, 0, i, 0)),
         pl.BlockSpec((2, nj, fb, d), lambda i: (0, 0, 0, 0))],
        [tile, pl.BlockSpec((8, d), lambda i: (0, 0))],
        [jax.ShapeDtypeStruct((t, d), F32), jax.ShapeDtypeStruct((8, d), F32)],
        [], ("arbitrary",), 60, (dxo, x, vec, dgu, wi))


def _mixout_bwd(dxo, ym, ycat, cv, hr, proj, vec, lnv, wout, tm, name, host=None):
    t, d = dxo.shape
    dc = cv.shape[1]
    nt = t // tm

    def body(ins, outs, scr):
        dxo_ref, ym_ref, yc_ref, cv_ref, hr_ref, uy_ref, vec_ref, lnv_ref, w_ref = ins
        dw_ref, dcv_ref, dhr_ref, duy_ref, dln_ref, dgt_ref = outs
        acc_ref, = scr
        i = pl.program_id(0)

        @pl.when(i == 0)
        def _():
            dln_ref[...] = jnp.zeros_like(dln_ref)
            dgt_ref[...] = jnp.zeros_like(dgt_ref)
            acc_ref[...] = jnp.zeros_like(acc_ref)

        dxo_v = dxo_ref[...]
        dym = (vec_ref[3:4, :] * dxo_v).astype(BF16)
        acc_ref[...] += _dot_tn(yc_ref[...], dym)
        dgt_ref[0:1, :] += jnp.sum(dxo_v * ym_ref[...].astype(F32), axis=0, keepdims=True)
        dyc = _dot_nt(dym, w_ref[0:dc, :])
        dyr = _dot_nt(dym, w_ref[dc:, :])
        rs, chat, z, sg = _ln_silu(cv_ref[...], lnv_ref)
        dz = dyc * (sg * (1.0 + z * (1.0 - sg)))
        dln_ref[0:1, :] += jnp.sum(dz * chat, axis=0, keepdims=True)
        dln_ref[1:2, :] += jnp.sum(dz, axis=0, keepdims=True)
        dchat = dz * lnv_ref[0:1, :]
        dcv_ref[...] = (rs * (dchat - jnp.mean(dchat, axis=-1, keepdims=True)
                              - chat * jnp.mean(dchat * chat, axis=-1, keepdims=True))).astype(BF16)
        ge, dge = _gelu_and_grad(uy_ref[...])
        dhr_ref[...] = (dyr * ge).astype(BF16)
        duy_ref[...] = (dyr * hr_ref[...] * dge).astype(BF16)

        @pl.when(i == nt - 1)
        def _():
            dw_ref[...] = acc_ref[...].astype(BF16)

    tile_d = pl.BlockSpec((tm, d), lambda i: (i, 0))
    tile_c = pl.BlockSpec((tm, dc), lambda i: (i, 0))
    full_w = pl.BlockSpec((d, d), lambda i: (0, 0))
    return _hosted_call(
        body, host, name, (nt,),
        [tile_d, tile_d, tile_d, tile_c, tile_c,
         pl.BlockSpec((tm, dc), lambda i: (i, 3)),
         pl.BlockSpec((8, d), lambda i: (0, 0)),
         pl.BlockSpec((8, dc), lambda i: (0, 0)),
         full_w],
        [full_w, tile_c, tile_c, tile_c,
         pl.BlockSpec((8, dc), lambda i: (0, 0)),
         pl.BlockSpec((8, d), lambda i: (0, 0))],
        [jax.ShapeDtypeStruct((d, d), BF16), jax.ShapeDtypeStruct((t, dc), BF16),
         jax.ShapeDtypeStruct((t, dc), BF16), jax.ShapeDtypeStruct((t, dc), BF16),
         jax.ShapeDtypeStruct((8, dc), F32), jax.ShapeDtypeStruct((8, d), F32)],
        [pltpu.VMEM((d, d), F32)], ("arbitrary",), 48, (dxo, ym, ycat, cv, hr, proj, vec, lnv, wout))


def _conv_bwd(proj, dcv, cw32, name):
    t = proj.shape[0]
    nblk = cw32.shape[1] // LANES
    ch = min(t, 128)

    def body(val_ref, gate_ref, dcv_ref, cw_ref, dval_ref, dgate_ref, dcw_ref, extu_ref, extd_ref):
        val = val_ref[...]
        sg = _sigmoid(gate_ref[...])
        extu_ref[0:32, :] = jnp.zeros((32, LANES), F32)
        extu_ref[32:, :] = val * sg
        dcv_v = dcv_ref[...].astype(F32)
        extd_ref[0:t, :] = dcv_v
        extd_ref[t:, :] = jnp.zeros((32, LANES), F32)
        for r in range(t // ch):
            acc = jnp.zeros((ch, LANES), F32)
            for k in range(CONV_W):
                off = r * ch + (CONV_W - 1 - k)
                acc = acc + cw_ref[k:k + 1, :] * extd_ref[off:off + ch, :]
            rows = slice(r * ch, (r + 1) * ch)
            sg_r = _sigmoid(gate_ref[rows, :])
            dval_ref[rows, :] = (acc * sg_r).astype(BF16)
            dgate_ref[rows, :] = (acc * val_ref[rows, :] * sg_r * (1.0 - sg_r)).astype(BF16)
        for k in range(CONV_W):
            off = 32 - (CONV_W - 1 - k)
            dcw_ref[k:k + 1, :] = jnp.sum(dcv_v * extu_ref[off:off + t, :], axis=0, keepdims=True)
        dcw_ref[31:32, :] = jnp.sum(dcv_v, axis=0, keepdims=True)

    blk = lambda off: pl.BlockSpec((t, LANES), lambda c: (0, off + c))
    return pl.pallas_call(
        body, name=name, grid=(nblk,),
        in_specs=[blk(0), blk(nblk), blk(0), pl.BlockSpec((32, LANES), lambda c: (0, c))],
        out_specs=[blk(0), blk(0), pl.BlockSpec((32, LANES), lambda c: (0, c))],
        out_shape=[jax.ShapeDtypeStruct((t, nblk * LANES), BF16), jax.ShapeDtypeStruct((t, nblk * LANES), BF16),
                   jax.ShapeDtypeStruct((32, nblk * LANES), F32)],
        scratch_shapes=[pltpu.VMEM((t + 32, LANES), F32), pltpu.VMEM((t + 32, LANES), F32)],
        compiler_params=_cp(("arbitrary",), 56),
    )(proj, proj, dcv, cw32)


def _rnn_bwd(proj, hr, dhr, rw8, rvec, wab, wib, name, host=None):
    t = proj.shape[0]
    nblk = rvec.shape[1] // LANES

    def body(ins, outs, scr):
        ux_ref, h_ref, dh_ref, rw_ref, rvec_ref, wa_ref, wi_ref = ins
        dux_ref, sm_ref, dwa_ref, dwi_ref = outs
        ext_ref, extd_ref, sa_ref, sb_ref, carry_ref = scr
        xr = _rnn_conv(ux_ref[...], rw_ref, rvec_ref, ext_ref)
        xb, r, ig, ls, a, mult = _rglru_gates(xr, wa_ref, wi_ref, rvec_ref)
        row = lax.broadcasted_iota(jnp.int32, (t, LANES), 0)
        a_next = jnp.where(row < t - 1, pltpu.roll(a, t - 1, 0), 0.0)
        _tiled_scan(a_next, dh_ref[...].astype(F32), True, sa_ref, sb_ref, carry_ref, extd_ref)
        g = extd_ref[0:t, :]
        hprev = jnp.where(row >= 1, pltpu.roll(h_ref[...], 1, 0), 0.0)
        da = g * hprev
        dmult = g * (ig * xr)
        dig = g * mult * xr
        dxr = g * mult * ig
        dlog_a = a * (da - dmult * a / mult)
        dr = dlog_a * (RG_C * ls)
        dls = RG_C * jnp.sum(dlog_a * r, axis=0, keepdims=True)
        dpr = dr * r * (1.0 - r)
        dpi = dig * ig * (1.0 - ig)
        dprb = dpr.astype(BF16)
        dpib = dpi.astype(BF16)
        dxr = dxr + _dot_nt(dprb, wa_ref[...]) + _dot_nt(dpib, wi_ref[...])
        dwa_ref[...] = _dot_tn(xb, dprb)
        dwi_ref[...] = _dot_tn(xb, dpib)
        extd_ref[0:t, :] = dxr
        extd_ref[t:, :] = jnp.zeros((8, LANES), F32)
        dux = rw_ref[RNN_CONV_W - 1:RNN_CONV_W, :] * dxr
        for k in range(RNN_CONV_W - 1):
            d = RNN_CONV_W - 1 - k
            dux = dux + rw_ref[k:k + 1, :] * extd_ref[d:d + t, :]
        dux_ref[...] = dux.astype(BF16)
        for k in range(RNN_CONV_W):
            d = RNN_CONV_W - 1 - k
            sm_ref[k:k + 1, :] = jnp.sum(dxr * ext_ref[8 - d:8 - d + t, :], axis=0, keepdims=True)
        sm_ref[4:5, :] = jnp.sum(dxr, axis=0, keepdims=True)
        sm_ref[5:6, :] = jnp.sum(dpr, axis=0, keepdims=True)
        sm_ref[6:7, :] = jnp.sum(dpi, axis=0, keepdims=True)
        sm_ref[7:8, :] = dls * _sigmoid(-rvec_ref[3:4, :])

    blk = lambda off: pl.BlockSpec((t, LANES), lambda c: (0, off + c))
    sq = pl.BlockSpec((None, LANES, LANES), lambda c: (c, 0, 0))
    return _hosted_call(
        body, host, name, (nblk,),
        [blk(2 * nblk), blk(0), blk(0),
         pl.BlockSpec((8, LANES), lambda c: (0, c)),
         pl.BlockSpec((8, LANES), lambda c: (0, c)), sq, sq],
        [blk(0), pl.BlockSpec((8, LANES), lambda c: (0, c)), sq, sq],
        [jax.ShapeDtypeStruct((t, nblk * LANES), BF16), jax.ShapeDtypeStruct((8, nblk * LANES), F32),
         jax.ShapeDtypeStruct((nblk, LANES, LANES), F32), jax.ShapeDtypeStruct((nblk, LANES, LANES), F32)],
        [pltpu.VMEM((t + 8, LANES), F32), pltpu.VMEM((t + 8, LANES), F32), pltpu.VMEM((t, LANES), F32),
         pltpu.VMEM((t, LANES), F32), pltpu.VMEM((t // SUBLANES, LANES), F32)],
        ("arbitrary",), 60, (proj, hr, dhr, rw8, rvec, wab, wib))


def _mixin_bwd(dxo, x, parts, vec, win, h, f, nvec, tm, name):
    t, d = x.shape
    nb, _, cb = win.shape
    dc = parts[0].shape[1]
    per = dc // cb
    nt = t // tm

    def body(dxo_ref, x_ref, p0, p1, p2, p3, vec_ref, w_ref, h_ref, f_ref, nvec_ref,
             dx_ref, dvec_ref, df_ref, dgt_ref, dw_ref, acc_ref):
        i = pl.program_id(0)

        @pl.when(i == 0)
        def _():
            dvec_ref[...] = jnp.zeros_like(dvec_ref)
            dgt_ref[...] = jnp.zeros_like(dgt_ref)
            acc_ref[...] = jnp.zeros_like(acc_ref)

        prefs = (p0, p1, p2, p3)
        hb = h_ref[...]
        for p in range(len(prefs)):
            acc_ref[p] += _dot_tn(hb, prefs[p][...])
        dh = jnp.zeros((tm, d), F32)
        for k in range(nb):
            dh = dh + _dot_nt(prefs[k // per][:, (k % per) * cb:(k % per + 1) * cb], w_ref[k])
        dx = dxo_ref[...] + _adaln_bwd(x_ref[...], dh, vec_ref, dvec_ref)
        dx_ref[...] = dx
        _emit_df(dx, f_ref, nvec_ref, df_ref, dgt_ref)

        @pl.when(i == nt - 1)
        def _():
            _adaln_finish(vec_ref, dvec_ref)
            for k in range(nb):
                dw_ref[k] = acc_ref[k // per, :, (k % per) * cb:(k % per + 1) * cb].astype(BF16)

    tile_d = pl.BlockSpec((tm, d), lambda i: (i, 0))
    tile_c = pl.BlockSpec((tm, dc), lambda i: (i, 0))
    tab = pl.BlockSpec((8, d), lambda i: (0, 0))
    wspec = pl.BlockSpec((nb, d, cb), lambda i: (0, 0, 0))
    return pl.pallas_call(
        body, name=name, grid=(nt,),
        in_specs=[tile_d, tile_d, tile_c, tile_c, tile_c, tile_c, tab, wspec, tile_d, tile_d, tab],
        out_specs=[tile_d, tab, tile_d, tab, wspec],
        out_shape=[jax.ShapeDtypeStruct((t, d), F32), jax.ShapeDtypeStruct((8, d), F32),
                   jax.ShapeDtypeStruct((t, d), BF16), jax.ShapeDtypeStruct((8, d), F32),
                   jax.ShapeDtypeStruct((nb, d, cb), BF16)],
        scratch_shapes=[pltpu.VMEM((len(parts), d, dc), F32)],
        compiler_params=_cp(("arbitrary",), 56),
    )(dxo, x, *parts, vec, win, h, f, nvec)


def _coords():
    return lax.axis_index("x"), lax.axis_index("y"), lax.axis_index("c")


def _flip(v, bit):
    return 1 - v if bit else v


def _gather_copy(outs, send_sems, recv_sems, a, k, block, to, src=None):
    dst = outs[a].at[block]
    return pltpu.make_async_remote_copy(
        src_ref=dst if src is None else src, dst_ref=dst,
        send_sem=send_sems.at[a, k], recv_sem=recv_sems.at[a, k],
        device_id=to, device_id_type=MESH_IDS)


def _gather_start(ins, outs, send_sems, recv_sems, loc_sems):
    x, y, c = _coords()
    me = 4 * x + 2 * y + c
    for a in range(len(ins)):
        pltpu.make_async_copy(ins[a], outs[a].at[me], loc_sems.at[a]).start()
    for a in range(len(ins)):
        _gather_copy(outs, send_sems, recv_sems, a, 0, me, (x, y, 1 - c), src=ins[a]).start()
        for j, (cx, cy) in enumerate([(1 - x, y), (x, 1 - y), (1 - x, 1 - y)]):
            _gather_copy(outs, send_sems, recv_sems, a, 1 + j, me, (cx, cy, c), src=ins[a]).start()


def _gather_finish(ins, outs, send_sems, recv_sems, loc_sems):
    x, y, c = _coords()
    me = 4 * x + 2 * y + c
    sib = (x, y, 1 - c)
    chips = [(1 - x, y), (x, 1 - y), (1 - x, 1 - y)]
    n = len(ins)
    for a in range(n):
        for j, (cx, cy) in enumerate(chips):
            blk = 4 * cx + 2 * cy + c
            _gather_copy(outs, send_sems, recv_sems, a, 1 + j, blk, sib).wait_recv()
            _gather_copy(outs, send_sems, recv_sems, a, 4 + j, blk, sib).start()
    for a in range(n):
        _gather_copy(outs, send_sems, recv_sems, a, 0, 4 * x + 2 * y + (1 - c), sib).wait_recv()
        for j, (cx, cy) in enumerate(chips):
            _gather_copy(outs, send_sems, recv_sems, a, 4 + j, 4 * cx + 2 * cy + (1 - c), sib).wait_recv()
    for a in range(n):
        _gather_copy(outs, send_sems, recv_sems, a, 0, me, sib, src=ins[a]).wait_send()
        for j, (cx, cy) in enumerate(chips):
            _gather_copy(outs, send_sems, recv_sems, a, 1 + j, me, (cx, cy, c), src=ins[a]).wait_send()
            _gather_copy(outs, send_sems, recv_sems, a, 4 + j, 4 * cx + 2 * cy + c, sib).wait_send()
        pltpu.make_async_copy(ins[a], outs[a].at[me], loc_sems.at[a]).wait()


def _gather_shapes(shards):
    return [jax.ShapeDtypeStruct((NDEV,) + s.shape, s.dtype) for s in shards]


def _gather_sems(n):
    return [pltpu.SemaphoreType.DMA((n, 7)), pltpu.SemaphoreType.DMA((n, 7)), pltpu.SemaphoreType.DMA((n,))]


def _sibling_copies(ins, outs, send_sems, recv_sems):
    x, y, c = _coords()
    return [pltpu.make_async_remote_copy(
        src_ref=ins[a].at[2 * q + (1 - c)], dst_ref=outs[a].at[q],
        send_sem=send_sems.at[a, q], recv_sem=recv_sems.at[a, q],
        device_id=(x, y, 1 - c), device_id_type=MESH_IDS) for a in range(len(ins)) for q in range(4)]


def _sibling_shapes(parts):
    return [jax.ShapeDtypeStruct((4,) + p.shape[1:], p.dtype) for p in parts]


def _chips_copies(ins, outs, send_sems, recv_sems):
    x, y, c = _coords()
    copies = []
    for a in range(len(ins)):
        for k, (kx, ky) in enumerate([(1, 0), (0, 1), (1, 1)]):
            tx, ty = _flip(x, kx), _flip(y, ky)
            copies.append(pltpu.make_async_remote_copy(
                src_ref=ins[a].at[2 * tx + ty], dst_ref=outs[a].at[k],
                send_sem=send_sems.at[a, k], recv_sem=recv_sems.at[a, k],
                device_id=(tx, ty, c), device_id_type=MESH_IDS))
    return copies


def _chips_shapes(sums):
    return [jax.ShapeDtypeStruct((3,) + s.shape[1:], s.dtype) for s in sums]


def _direct_copies(ins, outs, send_sems, recv_sems):
    x, y, c = _coords()
    me = 4 * x + 2 * y + c
    copies = []
    for a in range(len(ins)):
        for k in range(1, NDEV):
            kx, ky, kc = (k >> 2) & 1, (k >> 1) & 1, k & 1
            copies.append(pltpu.make_async_remote_copy(
                src_ref=ins[a], dst_ref=outs[a].at[me],
                send_sem=send_sems.at[a, k - 1], recv_sem=recv_sems.at[a, k - 1],
                device_id=(_flip(x, kx), _flip(y, ky), _flip(c, kc)), device_id_type=MESH_IDS))
    return copies


class _Exchange:
    def __init__(self, kind, arrays):
        self.kind, self.arrays, self.n = kind, list(arrays), len(arrays)

    def out_shapes(self):
        return {"gather": _gather_shapes, "direct": _gather_shapes, "sibling": _sibling_shapes,
                "chips": _chips_shapes}[self.kind](self.arrays)

    def sems(self):
        if self.kind in ("gather", "direct"):
            return _gather_sems(self.n)
        k = {"sibling": 4, "chips": 3}[self.kind]
        return [pltpu.SemaphoreType.DMA((self.n, k)), pltpu.SemaphoreType.DMA((self.n, k))]

    def _copies(self, ins, outs, sems):
        if self.kind == "direct":
            x, y, c = _coords()
            own = [pltpu.make_async_copy(ins[a], outs[a].at[4 * x + 2 * y + c], sems[2].at[a]) for a in range(self.n)]
            return own + _direct_copies(ins, outs, sems[0], sems[1])
        return {"sibling": _sibling_copies, "chips": _chips_copies}[self.kind](ins, outs, *sems)

    def start(self, ins, outs, sems):
        if self.kind == "gather":
            _gather_start(ins, outs, *sems)
        else:
            for cpy in self._copies(ins, outs, sems):
                cpy.start()

    def finish(self, ins, outs, sems):
        if self.kind == "gather":
            _gather_finish(ins, outs, *sems)
        else:
            for cpy in self._copies(ins, outs, sems):
                cpy.wait()


def _hosted_call(body, host, name, grid, in_specs, out_specs, out_shape, scratch, sem, vmem_mb, args, manual=False):
    n = host.n if host else 0
    ni, no, ns = len(in_specs), len(out_specs), len(scratch)

    def full(*refs):
        ins, h_in = refs[:ni], refs[ni:ni + n]
        outs, h_out = refs[ni + n:ni + n + no], refs[ni + n + no:ni + 2 * n + no]
        scr, sems = refs[ni + 2 * n + no:ni + 2 * n + no + ns], refs[ni + 2 * n + no + ns:]
        if manual:
            body(ins, outs, scr, lambda: host.start(h_in, h_out, sems), lambda: host.finish(h_in, h_out, sems))
            return
        if host and grid:
            first = functools.reduce(lambda a, b: a & b, [pl.program_id(k) == 0 for k in range(len(grid))])
            last = functools.reduce(lambda a, b: a & b, [pl.program_id(k) == g - 1 for k, g in enumerate(grid)])

            @pl.when(first)
            def _():
                host.start(h_in, h_out, sems)
        elif host:
            host.start(h_in, h_out, sems)

        body(ins, outs, scr)

        if host and grid:
            @pl.when(last)
            def _():
                host.finish(h_in, h_out, sems)
        elif host:
            host.finish(h_in, h_out, sems)

    anyspec = pl.BlockSpec(memory_space=pl.ANY)
    return pl.pallas_call(
        full, name=name, grid=grid,
        in_specs=list(in_specs) + [anyspec] * n, out_specs=list(out_specs) + [anyspec] * n,
        out_shape=list(out_shape) + (host.out_shapes() if host else []),
        scratch_shapes=list(scratch) + (host.sems() if host else []),
        compiler_params=_cp(sem, vmem_mb),
    )(*args, *(host.arrays if host else []))


def _exchange(host, name, after=()):
    n, na = host.n, len(after)

    def body(*refs):
        ins, outs, sems = refs[:n], refs[n + na:2 * n + na], refs[2 * n + na:]
        host.start(ins, outs, sems)
        host.finish(ins, outs, sems)

    anyspec = pl.BlockSpec(memory_space=pl.ANY)
    return pl.pallas_call(
        body, name=name, in_specs=[anyspec] * (n + na), out_specs=[anyspec] * n,
        out_shape=host.out_shapes(), scratch_shapes=host.sems(),
    )(*host.arrays, *after)


def _chips_split_start(sums, name):
    n = len(sums)
    hbm = pl.BlockSpec(memory_space=pltpu.HBM)
    sem = pl.BlockSpec(memory_space=pltpu.SEMAPHORE)

    def body(*refs):
        ins, lands = refs[:n], refs[n:2 * n]
        sems = refs[2 * n:2 * n + 6 * n]
        token = refs[-1]
        for cpy in _chips_copies(ins, lands, _SemGrid(sems[:3 * n], 3), _SemGrid(sems[3 * n:], 3)):
            cpy.start()
        token[...] = jnp.zeros_like(token)

    land_shapes = _chips_shapes(sums)
    lands = [pltpu.with_memory_space_constraint(lax.empty(s.shape, s.dtype), pltpu.HBM) for s in land_shapes]
    return pl.pallas_call(
        body, name=name,
        out_shape=(*[pltpu.SemaphoreType.DMA(())] * (6 * n),
                   *[pltpu.HBM(s.shape, s.dtype) for s in sums],
                   *[pltpu.HBM(s.shape, s.dtype) for s in land_shapes],
                   jax.ShapeDtypeStruct((8, LANES), F32)),
        in_specs=[hbm] * (2 * n),
        out_specs=(*[sem] * (6 * n), *[hbm] * (2 * n), pl.BlockSpec(memory_space=pltpu.VMEM)),
        input_output_aliases={i: 6 * n + i for i in range(2 * n)},
        compiler_params=pltpu.CompilerParams(has_side_effects=pltpu.SideEffectType.DATAFLOW_SIDE_EFFECTING),
    )(*[pltpu.with_memory_space_constraint(s, pltpu.HBM) for s in sums], *lands)


def _owner_copies(parts, lands, send_sems, recv_sems):
    x, y, c = _coords()
    copies = []
    for a in range(len(parts)):
        for k in range(1, NDEV):
            px, py, pc = _flip(x, (k >> 2) & 1), _flip(y, (k >> 1) & 1), _flip(c, k & 1)
            copies.append(pltpu.make_async_remote_copy(
                src_ref=parts[a].at[4 * px + 2 * py + pc], dst_ref=lands[a].at[k - 1],
                send_sem=send_sems.at[a, k - 1], recv_sem=recv_sems.at[a, k - 1],
                device_id=(px, py, pc), device_id_type=MESH_IDS))
    return copies


def _owner_split_start(parts, name):
    n = len(parts)
    k = NDEV - 1
    hbm = pl.BlockSpec(memory_space=pltpu.HBM)
    sem = pl.BlockSpec(memory_space=pltpu.SEMAPHORE)

    def body(*refs):
        ins, lands = refs[:n], refs[n:2 * n]
        sems = refs[2 * n:2 * n + 2 * k * n]
        for cpy in _owner_copies(ins, lands, _SemGrid(sems[:k * n], k), _SemGrid(sems[k * n:], k)):
            cpy.start()
        refs[-1][...] = jnp.zeros_like(refs[-1])

    land_shapes = [jax.ShapeDtypeStruct((k,) + p.shape[1:], p.dtype) for p in parts]
    lands = [pltpu.with_memory_space_constraint(lax.empty(s.shape, s.dtype), pltpu.HBM) for s in land_shapes]
    return pl.pallas_call(
        body, name=name,
        out_shape=(*[pltpu.SemaphoreType.DMA(())] * (2 * k * n),
                   *[pltpu.HBM(p.shape, p.dtype) for p in parts],
                   *[pltpu.HBM(s.shape, s.dtype) for s in land_shapes],
                   jax.ShapeDtypeStruct((8, LANES), F32)),
        in_specs=[hbm] * (2 * n),
        out_specs=(*[sem] * (2 * k * n), *[hbm] * (2 * n), pl.BlockSpec(memory_space=pltpu.VMEM)),
        input_output_aliases={i: 2 * k * n + i for i in range(2 * n)},
        compiler_params=pltpu.CompilerParams(has_side_effects=pltpu.SideEffectType.DATAFLOW_SIDE_EFFECTING),
    )(*[pltpu.with_memory_space_constraint(p, pltpu.HBM) for p in parts], *lands)


def _owner_split_wait(started, n, after, name):
    k = NDEV - 1
    sems, thru = started[:2 * k * n], started[2 * k * n:2 * k * n + 2 * n]
    hbm = pl.BlockSpec(memory_space=pltpu.HBM)
    sem = pl.BlockSpec(memory_space=pltpu.SEMAPHORE)

    def body(*refs):
        ins, lands = refs[:n], refs[n:2 * n]
        s = refs[2 * n:2 * n + 2 * k * n]
        for cpy in _owner_copies(ins, lands, _SemGrid(s[:k * n], k), _SemGrid(s[k * n:], k)):
            cpy.wait_send()
            cpy.wait_recv()

    outs = pl.pallas_call(
        body, name=name,
        out_shape=tuple(pltpu.HBM(a.shape, a.dtype) for a in thru),
        in_specs=[hbm] * (2 * n) + [sem] * (2 * k * n) + [pl.BlockSpec(memory_space=pl.ANY)],
        out_specs=tuple([hbm] * (2 * n)),
        input_output_aliases={i: i for i in range(2 * n)},
        compiler_params=pltpu.CompilerParams(has_side_effects=pltpu.SideEffectType.DATAFLOW_SIDE_EFFECTING),
    )(*thru, *sems, after)
    return list(outs[:n]), list(outs[n:])


def _owner_final(part, recv, me, w, m, v, tr, name):
    _, r, c = part.shape

    def body(me_ref, p_ref, r_ref, w_ref, m_ref, v_ref, g_ref, d_ref, mo_ref, vo_ref):
        g = p_ref[...].astype(F32)
        for k in range(NDEV - 1):
            g = g + r_ref[k].astype(F32)
        g_ref[...] = g
        d_ref[...], mo_ref[...], vo_ref[...] = _adamw_math(w_ref[...], g, m_ref[...], v_ref[...])

    tile = pl.BlockSpec((tr, c), lambda i, s: (i, 0))
    sds = jax.ShapeDtypeStruct((r, c), F32)
    return pl.pallas_call(
        body, name=name,
        grid_spec=pltpu.PrefetchScalarGridSpec(
            num_scalar_prefetch=1, grid=(r // tr,),
            in_specs=[pl.BlockSpec((None, tr, c), lambda i, s: (s[0], i, 0)),
                      pl.BlockSpec((NDEV - 1, tr, c), lambda i, s: (0, i, 0)), tile, tile, tile],
            out_specs=[tile] * 4),
        out_shape=[sds] * 4,
        compiler_params=_cp(("arbitrary",), 48),
    )(me, part, recv, w, m, v)


class _SemGrid:
    def __init__(self, sems, k):
        self.sems, self.k = sems, k

    @property
    def at(self):
        return self

    def __getitem__(self, idx):
        return self.sems[idx[0] * self.k + idx[1]]


def _chips_split_wait(started, n, after, name):
    sems = started[:6 * n]
    thru = started[6 * n:8 * n]
    hbm = pl.BlockSpec(memory_space=pltpu.HBM)
    sem = pl.BlockSpec(memory_space=pltpu.SEMAPHORE)

    def body(*refs):
        ins, lands = refs[:n], refs[n:2 * n]
        s = refs[2 * n:2 * n + 6 * n]
        for cpy in _chips_copies(ins, lands, _SemGrid(s[:3 * n], 3), _SemGrid(s[3 * n:], 3)):
            cpy.wait_send()
            cpy.wait_recv()

    outs = pl.pallas_call(
        body, name=name,
        out_shape=tuple(pltpu.HBM(a.shape, a.dtype) for a in thru),
        in_specs=[hbm] * (2 * n) + [sem] * (6 * n) + [pl.BlockSpec(memory_space=pl.ANY)],
        out_specs=tuple([hbm] * (2 * n)),
        input_output_aliases={i: i for i in range(2 * n)},
        compiler_params=pltpu.CompilerParams(has_side_effects=pltpu.SideEffectType.DATAFLOW_SIDE_EFFECTING),
    )(*thru, *sems, after)
    return list(outs[n:])


def _chip_sum(part, recv, sel, tr, name):
    _, _, r, c = part.shape

    def body(sel_ref, p_ref, r_ref, cs_ref, own_ref):
        q = pl.program_id(1)
        s = p_ref[...].astype(F32) + r_ref[...].astype(F32)
        cs_ref[...] = s.astype(BF16)

        @pl.when(q == sel_ref[1])
        def _():
            own_ref[...] = s

    return pl.pallas_call(
        body, name=name,
        grid_spec=pltpu.PrefetchScalarGridSpec(
            num_scalar_prefetch=1, grid=(r // tr, 4),
            in_specs=[pl.BlockSpec((None, None, tr, c), lambda i, q, s: (q, s[0], i, 0)),
                      pl.BlockSpec((None, tr, c), lambda i, q, s: (q, i, 0))],
            out_specs=[pl.BlockSpec((None, tr, c), lambda i, q, s: (q, i, 0)),
                       pl.BlockSpec((tr, c), lambda i, q, s: (i, 0))]),
        out_shape=[jax.ShapeDtypeStruct((4, r, c), BF16), jax.ShapeDtypeStruct((r, c), F32)],
        compiler_params=_cp(("arbitrary", "arbitrary"), 48),
    )(sel, part, recv)


def _gather_direct(src_ref, buf_ref, send_sems, recv_sems):
    x, y, c = _coords()
    me = 4 * x + 2 * y + c
    buf_ref[me] = src_ref[...]
    copies = []
    for k in range(1, NDEV):
        kx, ky, kc = (k >> 2) & 1, (k >> 1) & 1, k & 1
        copies.append(pltpu.make_async_remote_copy(
            src_ref=src_ref, dst_ref=buf_ref.at[me],
            send_sem=send_sems.at[k - 1], recv_sem=recv_sems.at[k - 1],
            device_id=(_flip(x, kx), _flip(y, ky), _flip(c, kc)), device_id_type=MESH_IDS))
    for cpy in copies:
        cpy.start()

    def wait():
        for k in range(1, NDEV):
            kx, ky, kc = (k >> 2) & 1, (k >> 1) & 1, k & 1
            peer = 4 * _flip(x, kx) + 2 * _flip(y, ky) + _flip(c, kc)
            pltpu.make_async_remote_copy(
                src_ref=src_ref, dst_ref=buf_ref.at[peer],
                send_sem=send_sems.at[k - 1], recv_sem=recv_sems.at[k - 1],
                device_id=(x, y, c), device_id_type=MESH_IDS).wait_recv()
        for cpy in copies:
            cpy.wait_send()

    return me, wait


def _mod_exchange(c_row, wmod, bmod, wfmod, bfmod, name, host=None):
    d = c_row.shape[1]
    nm, nf = wmod.shape[1], wfmod.shape[1]
    nw = nm + nf

    def body(ins, outs, scr, host_start, host_finish):
        c_ref, wm_ref, bm_ref, wf_ref, bf_ref = ins
        cs_ref, mod_ref, fmod_ref = outs
        slab_ref, csbuf_ref, mslab_ref, mbuf_ref, s1, r1, s2, r2 = scr
        cv = c_ref[...]
        slab_ref[...] = jnp.broadcast_to(cv * _sigmoid(cv), (8, d))
        _, wait1 = _gather_direct(slab_ref, csbuf_ref, s1, r1)
        wait1()
        host_start()
        for b in range(NDEV):
            cs_ref[b:b + 1, :] = csbuf_ref[b, 0:1, :]
        cs = cs_ref[...]
        mslab_ref[:, 0:nm] = jnp.dot(cs, wm_ref[...], precision=HI, preferred_element_type=F32) + bm_ref[...]
        mslab_ref[:, nm:] = jnp.dot(cs, wf_ref[...], precision=HI, preferred_element_type=F32) + bf_ref[...]
        me, wait2 = _gather_direct(mslab_ref, mbuf_ref, s2, r2)
        host_finish()
        wait2()
        mine = lax.broadcasted_iota(jnp.int32, (8, nw), 0) == me
        for k in range(NDEV):
            rowk = jnp.sum(jnp.where(mine, mbuf_ref[k], 0.0), axis=0, keepdims=True)
            mod_ref[k:k + 1, :] = rowk[:, 0:nm]
            fmod_ref[k:k + 1, :] = rowk[:, nm:]

    vm = pl.BlockSpec(memory_space=pltpu.VMEM)
    return _hosted_call(
        body, host, name, (), [vm] * 5, [vm] * 3,
        [jax.ShapeDtypeStruct((NDEV, d), F32), jax.ShapeDtypeStruct((NDEV, nm), F32),
         jax.ShapeDtypeStruct((NDEV, nf), F32)],
        [pltpu.VMEM((8, d), F32), pltpu.VMEM((NDEV, 8, d), F32),
         pltpu.VMEM((8, nw), F32), pltpu.VMEM((NDEV, 8, nw), F32),
         pltpu.SemaphoreType.DMA((7,)), pltpu.SemaphoreType.DMA((7,)),
         pltpu.SemaphoreType.DMA((7,)), pltpu.SemaphoreType.DMA((7,))],
        None, 40, (c_row, wmod, bmod, wfmod, bfmod), manual=True)


def _table_sum(tabs, name):
    n = len(tabs)

    def body(*refs):
        for a in range(n):
            tot = refs[a][0]
            for k in range(1, NDEV):
                tot = tot + refs[a][k]
            refs[n + a][...] = tot

    vm = pl.BlockSpec(memory_space=pltpu.VMEM)
    return pl.pallas_call(
        body, name=name, in_specs=[vm] * n, out_specs=[vm] * n,
        out_shape=[jax.ShapeDtypeStruct(tb.shape[1:], F32) for tb in tabs],
    )(*tabs)


def _adamw_math(w, g, m, v):
    m = ADAM_B1 * m + (1.0 - ADAM_B1) * g
    v = ADAM_B2 * v + (1.0 - ADAM_B2) * (g * g)
    m_hat = m / (1.0 - ADAM_B1 ** ADAM_STEP)
    v_hat = v / (1.0 - ADAM_B2 ** ADAM_STEP)
    delta = -ADAM_LR * (m_hat / (jnp.sqrt(v_hat) + ADAM_EPS) + ADAM_WD * w)
    return delta, m, v


def _adamw_small(params, name):
    n = len(params)

    def body(*refs):
        for p in range(n):
            w_ref, g_ref, m_ref, v_ref = refs[4 * p:4 * p + 4]
            d_ref, mo_ref, vo_ref = refs[4 * n + 3 * p:4 * n + 3 * p + 3]
            d_ref[...], mo_ref[...], vo_ref[...] = _adamw_math(w_ref[...], g_ref[...], m_ref[...], v_ref[...])

    vm = pl.BlockSpec(memory_space=pltpu.VMEM)
    flat = [a for p in params for a in p]
    outs = pl.pallas_call(
        body, name=name, in_specs=[vm] * (4 * n), out_specs=[vm] * (3 * n),
        out_shape=[jax.ShapeDtypeStruct(p[0].shape, F32) for p in params for _ in range(3)])(*flat)
    return [outs[3 * p:3 * p + 3] for p in range(n)]


def _rs_final(own, recv, w, m, v, tr, name):
    r, c = own.shape

    def body(o_ref, r_ref, w_ref, m_ref, v_ref, g_ref, d_ref, mo_ref, vo_ref):
        g = o_ref[...] + r_ref[0].astype(F32) + r_ref[1].astype(F32) + r_ref[2].astype(F32)
        g_ref[...] = g
        d_ref[...], mo_ref[...], vo_ref[...] = _adamw_math(w_ref[...], g, m_ref[...], v_ref[...])

    tile = pl.BlockSpec((tr, c), lambda i: (i, 0))
    sds = jax.ShapeDtypeStruct((r, c), F32)
    return pl.pallas_call(
        body, name=name, grid=(r // tr,),
        in_specs=[tile, pl.BlockSpec((3, tr, c), lambda i: (0, i, 0)), tile, tile, tile],
        out_specs=[tile] * 4, out_shape=[sds] * 4,
        compiler_params=_cp(("arbitrary",), 48),
    )(own, recv, w, m, v)


def _mod_weight_update(cs, dm, w, m, v, tr, name):
    r, c = w.shape

    def body(cs_ref, dm_ref, w_ref, m_ref, v_ref, g_ref, d_ref, mo_ref, vo_ref):
        g = lax.dot_general(cs_ref[...], dm_ref[...], (((0,), (0,)), ((), ())),
                            precision=HI, preferred_element_type=F32)
        g_ref[...] = g
        d_ref[...], mo_ref[...], vo_ref[...] = _adamw_math(w_ref[...], g, m_ref[...], v_ref[...])

    tile = pl.BlockSpec((tr, c), lambda i: (i, 0))
    sds = jax.ShapeDtypeStruct((r, c), F32)
    return pl.pallas_call(
        body, name=name, grid=(r // tr,),
        in_specs=[pl.BlockSpec((NDEV, tr), lambda i: (0, i)), pl.BlockSpec((NDEV, c), lambda i: (0, 0)),
                  tile, tile, tile],
        out_specs=[tile] * 4, out_shape=[sds] * 4,
        compiler_params=_cp(("arbitrary",), 48),
    )(cs, dm, w, m, v)


def _rows(*vs):
    d = vs[0].shape[-1]
    rows = [v.reshape(1, d) for v in vs]
    return jnp.concatenate(rows + [jnp.zeros((8 - len(rows), d), F32)], axis=0)


def _block_diag_pairs(w):
    hd = w.shape[-1]
    z = jnp.zeros((w.shape[0] // 2, hd, hd), w.dtype)
    top = jnp.concatenate([w[0::2], z], axis=2)
    bot = jnp.concatenate([z, w[1::2]], axis=2)
    return jnp.concatenate([top, bot], axis=1).astype(BF16)


def _diag_pairs(g):
    hd = g.shape[-1] // 2
    both = jnp.stack([g[:, :hd, :hd], g[:, hd:, hd:]], axis=1)
    return both.reshape(2 * g.shape[0], hd, hd)


def kernel(x, c, w_mod, b_mod, g_ffn1, w_ffn1_in, w_ffn1_out, g_mix, w_in, conv_w, conv_b, ln_g, ln_b, rnn_conv_w, rnn_conv_b, w_a, b_a, w_i, b_i, lru_lambda, w_out, g_ffn2, w_ffn2_in, w_ffn2_out, w_fmod, b_fmod, g_final, loss_target, m_w_mod, m_b_mod, m_g_ffn1, m_w_ffn1_in, m_w_ffn1_out, m_g_mix, m_w_in, m_conv_w, m_conv_b, m_ln_g, m_ln_b, m_rnn_conv_w, m_rnn_conv_b, m_w_a, m_b_a, m_w_i, m_b_i, m_lru_lambda, m_w_out, m_g_ffn2, m_w_ffn2_in, m_w_ffn2_out, m_w_fmod, m_b_fmod, m_g_final, v_w_mod, v_b_mod, v_g_ffn1, v_w_ffn1_in, v_w_ffn1_out, v_g_mix, v_w_in, v_conv_w, v_conv_b, v_ln_g, v_ln_b, v_rnn_conv_w, v_rnn_conv_b, v_w_a, v_b_a, v_w_i, v_b_i, v_lru_lambda, v_w_out, v_g_ffn2, v_w_ffn2_in, v_w_ffn2_out, v_w_fmod, v_b_fmod, v_g_final):
    t, d = x.shape[1], x.shape[2]
    fb = w_ffn1_in.shape[2]
    nm = w_mod.shape[2]
    nf = w_fmod.shape[1]
    dc = conv_b.shape[1]
    cl = conv_w.shape[2]
    tm = min(TOKEN_TILE, t)
    me = 4 * lax.axis_index("x") + 2 * lax.axis_index("y") + lax.axis_index("c")

    tr = jnp.transpose
    bmod_l = lax.dynamic_slice(b_mod, (0, me * nm), (1, nm))
    bfmod_l = lax.dynamic_slice(b_fmod.reshape(1, -1), (0, me * nf), (1, nf))
    cwl = jnp.concatenate([conv_w[0], jnp.zeros((1, cl), F32), rnn_conv_w[0], jnp.zeros((4, cl), F32)], axis=0)
    cs, mod_rows, fmod_rows, wi1, wo1, cwg = _mod_exchange(
        c, w_mod[0], bmod_l, w_fmod, bfmod_l, "mod_and_gather_ffn1",
        host=_Exchange("gather", [tr(w_ffn1_in[0]).astype(BF16), w_ffn1_out[0].astype(BF16), cwl]))
    wi1 = wi1.reshape(2, 4, fb, d)
    wo1 = wo1.reshape(4 * fb, d)
    mod = mod_rows.reshape(9, d)
    fmod = fmod_rows.reshape(2, d)
    vec1 = _rows(g_ffn1, mod[0], mod[1], mod[2])
    vecm = _rows(g_mix, mod[3], mod[4], mod[5])
    vec3 = _rows(g_ffn2, mod[6], mod[7], mod[8])
    vecf = _rows(g_final, fmod[0], fmod[1])

    xin = x[0]
    later = [w_in[0].astype(BF16), w_out[0].astype(BF16), tr(w_ffn2_in[0]).astype(BF16), w_ffn2_out[0].astype(BF16)]
    x1, h1, gu1, f1, win, wout, wi2, wo2 = _ffn_fwd(xin, vec1, wi1, wo1, tm, "ffn1_fwd",
                                                    host=_Exchange("gather", later))
    wi2 = wi2.reshape(2, 4, fb, d)
    wo2 = wo2.reshape(4 * fb, d)
    wout = wout.reshape(d, d)
    h2, proj = _mix_in(x1, vecm, win, tm, "mix_in")
    lnv = _rows(ln_g, ln_b)
    rvec = _rows(rnn_conv_b, b_a, b_i, lru_lambda)
    wab = _block_diag_pairs(w_a[0])
    wib = _block_diag_pairs(w_i[0])
    cwf = jnp.transpose(cwg, (1, 0, 2)).reshape(40, NDEV * cl)
    cw32 = jnp.concatenate([cwf[0:CONV_W], conv_b], axis=0)
    rw8 = cwf[32:40]

    (cv,) = _conv_fwd(proj, cw32, "conv_fwd")
    hr, yr = _rnn_fwd(proj, rw8, rvec, wab, wib, "rnn_fwd")
    x2, ym, ycat = _mix_out(x1, cv, yr, vecm, lnv, wout, tm, "mix_out")
    h3, gu3, dx3, dvf, df3, dva3 = _ffn_fwd_loss(x2, vec3, wi2, wo2, loss_target[0], vecf, tm, "ffn2_fwd_loss")
    sel = jnp.stack([lax.axis_index("c"), 2 * lax.axis_index("x") + lax.axis_index("y")]).astype(jnp.int32)
    row_tile = {"w_ffn1_in": fb // 4, "w_ffn1_out": fb // 4, "w_in": 512, "w_out": 128,
                "w_ffn2_in": fb // 4, "w_ffn2_out": fb // 4}

    def chip_sums(names, partials, from_sib):
        out = [_chip_sum(p.reshape((4, 2) + p.shape[1:]), r, sel, p.shape[1], "chip_sum_" + nm_)
               for nm_, p, r in zip(names, partials, from_sib)]
        return [o[0] for o in out], [o[1] for o in out]

    dgu3, p_wi2, p_wo2 = _ffn_bwd_w(df3, gu3, h3, wo2, tm, "ffn2_bwd_w")
    p_wi2 = p_wi2.reshape(NDEV, fb, d)
    p_wo2 = p_wo2.reshape(NDEV, fb // 2, d)
    names2 = ["w_ffn2_in", "w_ffn2_out"]
    started2 = _owner_split_start([p_wi2, p_wo2], "rs_owner_ffn2_start")
    dx2, dv3 = _ffn_bwd_in(dx3, x2, vec3 + started2[-1][0:1, 0:1], dgu3, wi2, tm, "ffn2_bwd_in")
    p_wout, dcv, dhr, duy, dln, dgt2 = _mixout_bwd(
        dx2, ym, ycat, cv, hr, proj, vecm, lnv, wout, tm, "mixout_bwd")
    p_wout = p_wout.reshape(NDEV, d // NDEV, d)
    dval, dgate, dcw = _conv_bwd(proj, dcv, cw32, "conv_bwd")
    dux, rsm, dwab, dwib = _rnn_bwd(proj, hr, dhr, rw8, rvec, wab, wib, "rnn_bwd")
    parts = [dval, dgate, dux, duy]
    dx1, dvm, df1, dva1, p_win = _mixin_bwd(dx2, x1, parts, vecm, win, h2, f1, vec1, tm, "mixin_bwd")
    namesm = ["w_in", "w_out"]
    lane_pad = lambda v: jnp.concatenate([v, jnp.zeros_like(v)], axis=1)
    startedm = _owner_split_start([p_win, p_wout], "rs_owner_mix_start")
    early = jnp.concatenate([dva3, dv3, dvf, dvm, dgt2, dva1 + startedm[-1][0:1, 0:1], dcw.reshape(16, d),
                             lane_pad(dln), lane_pad(rsm),
                             _diag_pairs(dwab).reshape(32, d), _diag_pairs(dwib).reshape(32, d)], axis=0)
    dgu1, p_wi1, p_wo1, all_early = _ffn_bwd_w(
        df1, gu1, h1, wo1, tm, "ffn1_bwd_w", host=_Exchange("direct", [early]))
    p_wi1 = p_wi1.reshape(NDEV, fb, d)
    p_wo1 = p_wo1.reshape(NDEV, fb // 2, d)
    names1 = ["w_ffn1_in", "w_ffn1_out"]
    sums1, owns1 = chip_sums(names1, [p_wi1, p_wo1],
                             _exchange(_Exchange("sibling", [p_wi1, p_wo1]), "rs_sibling_ffn1"))
    started = _chips_split_start(sums1, "rs_chips_ffn1_start")
    dx0, dv1 = _ffn_bwd_in(dx1, xin, vec1 + started[-1][0:1, 0:1], dgu1, wi1, tm, "ffn1_bwd_in")
    direct = {}
    for names, st, label in ((names2, started2, "ffn2"), (namesm, startedm, "mix")):
        kept, got = _owner_split_wait(st, len(names), dx0, "rs_owner_%s_wait" % label)
        direct.update(zip(names, zip(kept, got)))
    from_chips = {}
    owns = dict(zip(names1, owns1))
    me_arr = me.reshape(1).astype(jnp.int32)

    big = {"w_ffn1_in": (tr(w_ffn1_in[0]), tr(m_w_ffn1_in[0]), tr(v_w_ffn1_in[0])),
           "w_ffn1_out": (w_ffn1_out[0], m_w_ffn1_out[0], v_w_ffn1_out[0]),
           "w_in": (w_in[0], m_w_in[0], v_w_in[0]), "w_out": (w_out[0], m_w_out[0], v_w_out[0]),
           "w_ffn2_in": (tr(w_ffn2_in[0]), tr(m_w_ffn2_in[0]), tr(v_w_ffn2_in[0])),
           "w_ffn2_out": (w_ffn2_out[0], m_w_ffn2_out[0], v_w_ffn2_out[0])}
    res = {}

    def final_sum(nm_):
        if nm_ in direct:
            out4 = _owner_final(*direct[nm_], me_arr, *big[nm_], row_tile[nm_], "rs_final_" + nm_)
        else:
            out4 = _rs_final(owns[nm_], from_chips[nm_], *big[nm_], row_tile[nm_], "rs_final_" + nm_)
        res[nm_] = [(tr(o) if nm_ in ("w_ffn1_in", "w_ffn2_in") else o)[None] for o in out4]
        return out4[0]

    done = [final_sum(nm_) for nm_ in namesm + names2]
    dfm_all = jnp.concatenate([all_early[:, 17], all_early[:, 19]], axis=1)
    dfm_l = lax.dynamic_slice(dfm_all, (0, me * nf), (NDEV, nf))
    res["w_fmod"] = list(_mod_weight_update(cs, dfm_l, w_fmod, m_w_fmod, v_w_fmod, 256, "w_fmod_update"))
    (all_late,) = _exchange(_Exchange("direct", [dv1]), "late_table", after=done + [res["w_fmod"][0]])
    from_chips["w_ffn1_in"], from_chips["w_ffn1_out"] = _chips_split_wait(
        started, len(sums1), all_late, "rs_chips_ffn1_wait")
    for nm_ in names1:
        final_sum(nm_)
    te, tl = _table_sum([all_early, all_late], "table_sum")
    loss = jnp.sum(te[20])

    mod_rows_of = lambda e, l: [l[1], l[3], e[42], e[25], e[27], e[32], e[9], e[11], e[2]]
    dm_all = jnp.concatenate(mod_rows_of(jnp.swapaxes(all_early, 0, 1), jnp.swapaxes(all_late, 0, 1)), axis=1)
    dm_l = lax.dynamic_slice(dm_all, (0, me * nm), (NDEV, nm))
    res["w_mod"] = [o[None] for o in
                    _mod_weight_update(cs, dm_l, w_mod[0], m_w_mod[0], v_w_mod[0], 256, "w_mod_update")]

    dcw_f = te[48:64].reshape(32, dc)
    rsm_f = te[72:80, 0:dc]
    small_grads = {
        "b_mod": jnp.concatenate(mod_rows_of(te, tl)).reshape(1, 9 * d),
        "b_fmod": jnp.concatenate([te[17], te[19]]),
        "g_ffn1": tl[0:1], "g_mix": te[24:25], "g_ffn2": te[8:9], "g_final": te[16],
        "conv_w": lax.dynamic_slice(dcw_f, (0, me * cl), (CONV_W, cl))[None],
        "conv_b": dcw_f[31:32],
        "ln_g": te[64:65, 0:dc], "ln_b": te[65:66, 0:dc],
        "rnn_conv_w": lax.dynamic_slice(rsm_f, (0, me * cl), (RNN_CONV_W, cl))[None],
        "rnn_conv_b": rsm_f[4:5], "b_a": rsm_f[5:6], "b_i": rsm_f[6:7], "lru_lambda": rsm_f[7:8],
        "w_a": te[80:112].reshape(w_a.shape), "w_i": te[112:144].reshape(w_i.shape),
    }
    small_params = {
        "b_mod": (b_mod, m_b_mod, v_b_mod), "b_fmod": (b_fmod, m_b_fmod, v_b_fmod),
        "g_ffn1": (g_ffn1, m_g_ffn1, v_g_ffn1), "g_mix": (g_mix, m_g_mix, v_g_mix),
        "g_ffn2": (g_ffn2, m_g_ffn2, v_g_ffn2), "g_final": (g_final, m_g_final, v_g_final),
        "conv_w": (conv_w, m_conv_w, v_conv_w), "conv_b": (conv_b, m_conv_b, v_conv_b),
        "ln_g": (ln_g, m_ln_g, v_ln_g), "ln_b": (ln_b, m_ln_b, v_ln_b),
        "rnn_conv_w": (rnn_conv_w, m_rnn_conv_w, v_rnn_conv_w),
        "rnn_conv_b": (rnn_conv_b, m_rnn_conv_b, v_rnn_conv_b),
        "w_a": (w_a, m_w_a, v_w_a), "b_a": (b_a, m_b_a, v_b_a),
        "w_i": (w_i, m_w_i, v_w_i), "b_i": (b_i, m_b_i, v_b_i),
        "lru_lambda": (lru_lambda, m_lru_lambda, v_lru_lambda),
    }
    two_d = lambda w: (-1, w.shape[-1]) if w.ndim > 1 else (1, w.shape[0])
    small_names = list(small_grads)
    small_outs = _adamw_small(
        [(w.reshape(two_d(w)), small_grads[nm_].reshape(two_d(w)), m.reshape(two_d(w)), v.reshape(two_d(w)))
         for nm_ in small_names for (w, m, v) in [small_params[nm_]]], "adamw_small")
    for nm_, outs in zip(small_names, small_outs):
        shp = small_params[nm_][0].shape
        res[nm_] = [small_grads[nm_].reshape(shp)] + [o.reshape(shp) for o in outs]

    order = ["w_mod", "b_mod", "g_ffn1", "w_ffn1_in", "w_ffn1_out", "g_mix", "w_in", "conv_w", "conv_b",
             "ln_g", "ln_b", "rnn_conv_w", "rnn_conv_b", "w_a", "b_a", "w_i", "b_i", "lru_lambda", "w_out",
             "g_ffn2", "w_ffn2_in", "w_ffn2_out", "w_fmod", "b_fmod", "g_final"]
    return (loss, dx0[None], *[res[n][0] for n in order], *[res[n][1] for n in order],
            *[res[n][2] for n in order], *[res[n][3] for n in order])
```
